```python
import jax, jax.numpy as jnp
from jax import lax
import numpy as np

D_MODEL = 1024
BATCH = 8
SEQ = 8192
DEPTH = 4

N_MIXERS = 2
N_LRU_LAYERS = (DEPTH + 1) // 2
N_POOL_LAYERS = DEPTH // 2
D_FF = 11 * D_MODEL // 4
D_RNN = 5 * D_MODEL // 4
LRU_HEADS = 16
LRU_HEAD_DIM = D_RNN // LRU_HEADS
CONV_WIDTH = 4
LRU_C = 8.0
POOL_WINDOWS = (2, 4, 8, 16)
POOL_GROUPS = len(POOL_WINDOWS)
POOL_GROUP_DIM = D_MODEL // POOL_GROUPS
PLE_DIM = 256
RMS_EPS = 1e-6

kernel_name = "hybrid_rglru_pool_macaron_ple"


def rms_norm(x, g):
    xf = x.astype(jnp.float32)
    y = xf * lax.rsqrt(jnp.mean(xf * xf, axis=-1, keepdims=True) + RMS_EPS)
    return (y * g.astype(jnp.float32)).astype(x.dtype)


def swiglu(x, w_gate, w_up, w_down):
    return (jax.nn.silu(x @ w_gate) * (x @ w_up)) @ w_down


def _lin_combine(c1, c2):
    a1, b1 = c1
    a2, b2 = c2
    return a1 * a2, a2 * b1 + b2


def rglru_mixer(x, w_in, conv_w, conv_b, w_a, b_a, w_x, b_x, a_param, w_out):
    B, S, _ = x.shape
    z = x @ w_in
    gate_branch, xb = z[..., :D_RNN], z[..., D_RNN:]
    xp = jnp.pad(xb, ((0, 0), (CONV_WIDTH - 1, 0), (0, 0)))
    xc = conv_b + conv_w[0] * xp[:, 0:S]
    for k in range(1, CONV_WIDTH):
        xc = xc + conv_w[k] * xp[:, k:k + S]
    xh = xc.reshape(B, S, LRU_HEADS, LRU_HEAD_DIM)
    r = jax.nn.sigmoid(jnp.einsum('bshi,hij->bshj', xh, w_a).reshape(B, S, D_RNN) + b_a)
    ig = jax.nn.sigmoid(jnp.einsum('bshi,hij->bshj', xh, w_x).reshape(B, S, D_RNN) + b_x)
    log_a = -LRU_C * r.astype(jnp.float32) * jax.nn.softplus(-a_param.astype(jnp.float32))
    a = jnp.exp(log_a)
    mult = jnp.sqrt(-jnp.expm1(2.0 * log_a))
    bterm = mult * (ig * xc).astype(jnp.float32)
    _, h = lax.associative_scan(_lin_combine, (a, bterm), axis=1)
    y = h.astype(x.dtype) * jax.nn.gelu(gate_branch)
    return y @ w_out


def pool_mixer(x, w, b, scale):
    B, S, _ = x.shape
    xf = x.astype(jnp.float32)
    cs = jnp.cumsum(xf, axis=1)
    t = jnp.arange(S)
    outs = []
    for g, win in enumerate(POOL_WINDOWS):
        lo, hi = g * POOL_GROUP_DIM, (g + 1) * POOL_GROUP_DIM
        c = cs[..., lo:hi]
        prev = jnp.pad(c[:, :S - win], ((0, 0), (win, 0), (0, 0)))
        count = jnp.minimum(t + 1, win).astype(jnp.float32)[None, :, None]
        outs.append((c - prev) / count - xf[..., lo:hi])
    u = jnp.stack(outs, axis=2).astype(x.dtype)
    y = jnp.einsum('bsgc,gcd->bsgd', u, w).reshape(B, S, D_MODEL)
    return (y + b) * scale


def _fwd_setup_inputs(seed: int = 0) -> dict:
    key = jax.random.key(seed)
    ks = iter(jax.random.split(key, 40))

    def nrm(shape, scale):
        return scale * jax.random.normal(next(ks), shape, jnp.float32)

    D, F, L = D_MODEL, D_FF, DEPTH
    NL, NP = N_LRU_LAYERS, N_POOL_LAYERS
    x = nrm((BATCH, SEQ, D), 1.0)
    p = nrm((DEPTH, BATCH, SEQ, PLE_DIM), 1.0)
    ffn1_norm = 1.0 + nrm((L, D), 0.1)
    ffn1_w_gate = nrm((L, D, F), D ** -0.5)
    ffn1_w_up = nrm((L, D, F), D ** -0.5)
    ffn1_w_down = nrm((L, F, D), F ** -0.5)
    mix_norm = 1.0 + nrm((L, D), 0.1)
    lru_w_in = nrm((NL, D, 2 * D_RNN), D ** -0.5)
    lru_conv_w = nrm((NL, CONV_WIDTH, D_RNN), CONV_WIDTH ** -0.5)
    lru_conv_b = nrm((NL, D_RNN), 0.01)
    lru_w_a = nrm((NL, LRU_HEADS, LRU_HEAD_DIM, LRU_HEAD_DIM), LRU_HEAD_DIM ** -0.5)
    lru_b_a = nrm((NL, D_RNN), 0.01)
    lru_w_x = nrm((NL, LRU_HEADS, LRU_HEAD_DIM, LRU_HEAD_DIM), LRU_HEAD_DIM ** -0.5)
    lru_b_x = nrm((NL, D_RNN), 0.01)
    u = jax.random.uniform(next(ks), (NL, D_RNN), jnp.float32, minval=0.9, maxval=0.999)
    a0 = u ** (1.0 / LRU_C)
    lru_a_param = jnp.log(a0) - jnp.log1p(-a0)
    lru_w_out = nrm((NL, D_RNN, D), D_RNN ** -0.5)
    pool_w = nrm((NP, POOL_GROUPS, POOL_GROUP_DIM, POOL_GROUP_DIM), POOL_GROUP_DIM ** -0.5)
    pool_b = nrm((NP, D), 0.01)
    pool_scale = 1.0 + nrm((NP, D), 0.1)
    ffn2_norm = 1.0 + nrm((L, D), 0.1)
    ffn2_w_gate = nrm((L, D, F), D ** -0.5)
    ffn2_w_up = nrm((L, D, F), D ** -0.5)
    ffn2_w_down = nrm((L, F, D), F ** -0.5)
    ple_norm = 1.0 + nrm((L, D), 0.1)
    ple_w_gate = nrm((L, D, D), D ** -0.5)
    ple_w_proj = nrm((L, PLE_DIM, D), PLE_DIM ** -0.5)
    final_norm = 1.0 + nrm((D,), 0.1)
    return {"x": x, "p": p,
            "ffn1_norm": ffn1_norm, "ffn1_w_gate": ffn1_w_gate, "ffn1_w_up": ffn1_w_up, "ffn1_w_down": ffn1_w_down,
            "mix_norm": mix_norm,
            "lru_w_in": lru_w_in, "lru_conv_w": lru_conv_w, "lru_conv_b": lru_conv_b,
            "lru_w_a": lru_w_a, "lru_b_a": lru_b_a, "lru_w_x": lru_w_x, "lru_b_x": lru_b_x,
            "lru_a_param": lru_a_param, "lru_w_out": lru_w_out,
            "pool_w": pool_w, "pool_b": pool_b, "pool_scale": pool_scale,
            "ffn2_norm": ffn2_norm, "ffn2_w_gate": ffn2_w_gate, "ffn2_w_up": ffn2_w_up, "ffn2_w_down": ffn2_w_down,
            "ple_norm": ple_norm, "ple_w_gate": ple_w_gate, "ple_w_proj": ple_w_proj,
            "final_norm": final_norm}


def _fwd_reference(x, p, ffn1_norm, ffn1_w_gate, ffn1_w_up, ffn1_w_down, mix_norm,
              lru_w_in, lru_conv_w, lru_conv_b, lru_w_a, lru_b_a, lru_w_x, lru_b_x, lru_a_param, lru_w_out,
              pool_w, pool_b, pool_scale,
              ffn2_norm, ffn2_w_gate, ffn2_w_up, ffn2_w_down,
              ple_norm, ple_w_gate, ple_w_proj, final_norm):
    h = x
    for i in range(DEPTH):
        h = h + 0.5 * swiglu(rms_norm(h, ffn1_norm[i]), ffn1_w_gate[i], ffn1_w_up[i], ffn1_w_down[i])
        hn = rms_norm(h, mix_norm[i])
        j = i // N_MIXERS
        if i % N_MIXERS == 0:
            m = rglru_mixer(hn, lru_w_in[j], lru_conv_w[j], lru_conv_b[j], lru_w_a[j], lru_b_a[j],
                            lru_w_x[j], lru_b_x[j], lru_a_param[j], lru_w_out[j])
        else:
            m = pool_mixer(hn, pool_w[j], pool_b[j], pool_scale[j])
        h = h + m
        h = h + 0.5 * swiglu(rms_norm(h, ffn2_norm[i]), ffn2_w_gate[i], ffn2_w_up[i], ffn2_w_down[i])
        gate = jax.nn.sigmoid(rms_norm(h, ple_norm[i]) @ ple_w_gate[i])
        h = h + gate * (p[i].astype(h.dtype) @ ple_w_proj[i])
    return rms_norm(h, final_norm)


import jax as _jax
import jax.numpy as _jnp

TWIN_FORMAT = 'train_step'
FWD_PARAMS = ['x', 'p', 'ffn1_norm', 'ffn1_w_gate', 'ffn1_w_up', 'ffn1_w_down', 'mix_norm', 'lru_w_in', 'lru_conv_w', 'lru_conv_b', 'lru_w_a', 'lru_b_a', 'lru_w_x', 'lru_b_x', 'lru_a_param', 'lru_w_out', 'pool_w', 'pool_b', 'pool_scale', 'ffn2_norm', 'ffn2_w_gate', 'ffn2_w_up', 'ffn2_w_down', 'ple_norm', 'ple_w_gate', 'ple_w_proj', 'final_norm']
TWIN_WEIGHTS = ['ffn1_norm', 'ffn1_w_gate', 'ffn1_w_up', 'ffn1_w_down', 'mix_norm', 'lru_w_in', 'lru_conv_w', 'lru_conv_b', 'lru_w_a', 'lru_b_a', 'lru_w_x', 'lru_b_x', 'lru_a_param', 'lru_w_out', 'pool_w', 'pool_b', 'pool_scale', 'ffn2_norm', 'ffn2_w_gate', 'ffn2_w_up', 'ffn2_w_down', 'ple_norm', 'ple_w_gate', 'ple_w_proj', 'final_norm']
TWIN_DIFF_INPUT = 'x'
TWIN_INPUTS = ['x', 'p', 'ffn1_norm', 'ffn1_w_gate', 'ffn1_w_up', 'ffn1_w_down', 'mix_norm', 'lru_w_in', 'lru_conv_w', 'lru_conv_b', 'lru_w_a', 'lru_b_a', 'lru_w_x', 'lru_b_x', 'lru_a_param', 'lru_w_out', 'pool_w', 'pool_b', 'pool_scale', 'ffn2_norm', 'ffn2_w_gate', 'ffn2_w_up', 'ffn2_w_down', 'ple_norm', 'ple_w_gate', 'ple_w_proj', 'final_norm', 'loss_target', 'm_ffn1_norm', 'm_ffn1_w_gate', 'm_ffn1_w_up', 'm_ffn1_w_down', 'm_mix_norm', 'm_lru_w_in', 'm_lru_conv_w', 'm_lru_conv_b', 'm_lru_w_a', 'm_lru_b_a', 'm_lru_w_x', 'm_lru_b_x', 'm_lru_a_param', 'm_lru_w_out', 'm_pool_w', 'm_pool_b', 'm_pool_scale', 'm_ffn2_norm', 'm_ffn2_w_gate', 'm_ffn2_w_up', 'm_ffn2_w_down', 'm_ple_norm', 'm_ple_w_gate', 'm_ple_w_proj', 'm_final_norm', 'v_ffn1_norm', 'v_ffn1_w_gate', 'v_ffn1_w_up', 'v_ffn1_w_down', 'v_mix_norm', 'v_lru_w_in', 'v_lru_conv_w', 'v_lru_conv_b', 'v_lru_w_a', 'v_lru_b_a', 'v_lru_w_x', 'v_lru_b_x', 'v_lru_a_param', 'v_lru_w_out', 'v_pool_w', 'v_pool_b', 'v_pool_scale', 'v_ffn2_norm', 'v_ffn2_w_gate', 'v_ffn2_w_up', 'v_ffn2_w_down', 'v_ple_norm', 'v_ple_w_gate', 'v_ple_w_proj', 'v_final_norm']
TWIN_OUTPUTS = ['loss', 'grad_x', 'grad_ffn1_norm', 'grad_ffn1_w_gate', 'grad_ffn1_w_up', 'grad_ffn1_w_down', 'grad_mix_norm', 'grad_lru_w_in', 'grad_lru_conv_w', 'grad_lru_conv_b', 'grad_lru_w_a', 'grad_lru_b_a', 'grad_lru_w_x', 'grad_lru_b_x', 'grad_lru_a_param', 'grad_lru_w_out', 'grad_pool_w', 'grad_pool_b', 'grad_pool_scale', 'grad_ffn2_norm', 'grad_ffn2_w_gate', 'grad_ffn2_w_up', 'grad_ffn2_w_down', 'grad_ple_norm', 'grad_ple_w_gate', 'grad_ple_w_proj', 'grad_final_norm', 'delta_ffn1_norm', 'delta_ffn1_w_gate', 'delta_ffn1_w_up', 'delta_ffn1_w_down', 'delta_mix_norm', 'delta_lru_w_in', 'delta_lru_conv_w', 'delta_lru_conv_b', 'delta_lru_w_a', 'delta_lru_b_a', 'delta_lru_w_x', 'delta_lru_b_x', 'delta_lru_a_param', 'delta_lru_w_out', 'delta_pool_w', 'delta_pool_b', 'delta_pool_scale', 'delta_ffn2_norm', 'delta_ffn2_w_gate', 'delta_ffn2_w_up', 'delta_ffn2_w_down', 'delta_ple_norm', 'delta_ple_w_gate', 'delta_ple_w_proj', 'delta_final_norm', 'new_m_ffn1_norm', 'new_m_ffn1_w_gate', 'new_m_ffn1_w_up', 'new_m_ffn1_w_down', 'new_m_mix_norm', 'new_m_lru_w_in', 'new_m_lru_conv_w', 'new_m_lru_conv_b', 'new_m_lru_w_a', 'new_m_lru_b_a', 'new_m_lru_w_x', 'new_m_lru_b_x', 'new_m_lru_a_param', 'new_m_lru_w_out', 'new_m_pool_w', 'new_m_pool_b', 'new_m_pool_scale', 'new_m_ffn2_norm', 'new_m_ffn2_w_gate', 'new_m_ffn2_w_up', 'new_m_ffn2_w_down', 'new_m_ple_norm', 'new_m_ple_w_gate', 'new_m_ple_w_proj', 'new_m_final_norm', 'new_v_ffn1_norm', 'new_v_ffn1_w_gate', 'new_v_ffn1_w_up', 'new_v_ffn1_w_down', 'new_v_mix_norm', 'new_v_lru_w_in', 'new_v_lru_conv_w', 'new_v_lru_conv_b', 'new_v_lru_w_a', 'new_v_lru_b_a', 'new_v_lru_w_x', 'new_v_lru_b_x', 'new_v_lru_a_param', 'new_v_lru_w_out', 'new_v_pool_w', 'new_v_pool_b', 'new_v_pool_scale', 'new_v_ffn2_norm', 'new_v_ffn2_w_gate', 'new_v_ffn2_w_up', 'new_v_ffn2_w_down', 'new_v_ple_norm', 'new_v_ple_w_gate', 'new_v_ple_w_proj', 'new_v_final_norm']
TWIN_LEAF_KINDS = {'loss': 'loss', 'grad_x': 'grad_x', 'grad_ffn1_norm': 'grad_w', 'grad_ffn1_w_gate': 'grad_w', 'grad_ffn1_w_up': 'grad_w', 'grad_ffn1_w_down': 'grad_w', 'grad_mix_norm': 'grad_w', 'grad_lru_w_in': 'grad_w', 'grad_lru_conv_w': 'grad_w', 'grad_lru_conv_b': 'grad_w', 'grad_lru_w_a': 'grad_w', 'grad_lru_b_a': 'grad_w', 'grad_lru_w_x': 'grad_w', 'grad_lru_b_x': 'grad_w', 'grad_lru_a_param': 'grad_w', 'grad_lru_w_out': 'grad_w', 'grad_pool_w': 'grad_w', 'grad_pool_b': 'grad_w', 'grad_pool_scale': 'grad_w', 'grad_ffn2_norm': 'grad_w', 'grad_ffn2_w_gate': 'grad_w', 'grad_ffn2_w_up': 'grad_w', 'grad_ffn2_w_down': 'grad_w', 'grad_ple_norm': 'grad_w', 'grad_ple_w_gate': 'grad_w', 'grad_ple_w_proj': 'grad_w', 'grad_final_norm': 'grad_w', 'delta_ffn1_norm': 'delta_w', 'delta_ffn1_w_gate': 'delta_w', 'delta_ffn1_w_up': 'delta_w', 'delta_ffn1_w_down': 'delta_w', 'delta_mix_norm': 'delta_w', 'delta_lru_w_in': 'delta_w', 'delta_lru_conv_w': 'delta_w', 'delta_lru_conv_b': 'delta_w', 'delta_lru_w_a': 'delta_w', 'delta_lru_b_a': 'delta_w', 'delta_lru_w_x': 'delta_w', 'delta_lru_b_x': 'delta_w', 'delta_lru_a_param': 'delta_w', 'delta_lru_w_out': 'delta_w', 'delta_pool_w': 'delta_w', 'delta_pool_b': 'delta_w', 'delta_pool_scale': 'delta_w', 'delta_ffn2_norm': 'delta_w', 'delta_ffn2_w_gate': 'delta_w', 'delta_ffn2_w_up': 'delta_w', 'delta_ffn2_w_down': 'delta_w', 'delta_ple_norm': 'delta_w', 'delta_ple_w_gate': 'delta_w', 'delta_ple_w_proj': 'delta_w', 'delta_final_norm': 'delta_w', 'new_m_ffn1_norm': 'new_m', 'new_m_ffn1_w_gate': 'new_m', 'new_m_ffn1_w_up': 'new_m', 'new_m_ffn1_w_down': 'new_m', 'new_m_mix_norm': 'new_m', 'new_m_lru_w_in': 'new_m', 'new_m_lru_conv_w': 'new_m', 'new_m_lru_conv_b': 'new_m', 'new_m_lru_w_a': 'new_m', 'new_m_lru_b_a': 'new_m', 'new_m_lru_w_x': 'new_m', 'new_m_lru_b_x': 'new_m', 'new_m_lru_a_param': 'new_m', 'new_m_lru_w_out': 'new_m', 'new_m_pool_w': 'new_m', 'new_m_pool_b': 'new_m', 'new_m_pool_scale': 'new_m', 'new_m_ffn2_norm': 'new_m', 'new_m_ffn2_w_gate': 'new_m', 'new_m_ffn2_w_up': 'new_m', 'new_m_ffn2_w_down': 'new_m', 'new_m_ple_norm': 'new_m', 'new_m_ple_w_gate': 'new_m', 'new_m_ple_w_proj': 'new_m', 'new_m_final_norm': 'new_m', 'new_v_ffn1_norm': 'new_v', 'new_v_ffn1_w_gate': 'new_v', 'new_v_ffn1_w_up': 'new_v', 'new_v_ffn1_w_down': 'new_v', 'new_v_mix_norm': 'new_v', 'new_v_lru_w_in': 'new_v', 'new_v_lru_conv_w': 'new_v', 'new_v_lru_conv_b': 'new_v', 'new_v_lru_w_a': 'new_v', 'new_v_lru_b_a': 'new_v', 'new_v_lru_w_x': 'new_v', 'new_v_lru_b_x': 'new_v', 'new_v_lru_a_param': 'new_v', 'new_v_lru_w_out': 'new_v', 'new_v_pool_w': 'new_v', 'new_v_pool_b': 'new_v', 'new_v_pool_scale': 'new_v', 'new_v_ffn2_norm': 'new_v', 'new_v_ffn2_w_gate': 'new_v', 'new_v_ffn2_w_up': 'new_v', 'new_v_ffn2_w_down': 'new_v', 'new_v_ple_norm': 'new_v', 'new_v_ple_w_gate': 'new_v', 'new_v_ple_w_proj': 'new_v', 'new_v_final_norm': 'new_v'}


def _forward(args):
    return _fwd_reference(*[args[k] for k in FWD_PARAMS])


def _output_shape():
    def fwd():
        inp = _fwd_setup_inputs(0)
        return _fwd_reference(*[inp[k] for k in FWD_PARAMS])
    out = _jax.eval_shape(fwd)
    return out.shape, out.dtype

N_MICROBATCH = 1
ADAM_LR = 0.001
ADAM_B1 = 0.9
ADAM_B2 = 0.999
ADAM_EPS = 1e-08
ADAM_WD = 0.01
ADAM_STEP = 10
PER_EXAMPLE_BATCH_AXIS = {'x': 0, 'p': 1, 'loss_target': 0}
SHARED_INPUTS = []
_WEIGHT_DTYPES = {'ffn1_norm': _jnp.float32, 'ffn1_w_gate': _jnp.float32, 'ffn1_w_up': _jnp.float32, 'ffn1_w_down': _jnp.float32, 'mix_norm': _jnp.float32, 'lru_w_in': _jnp.float32, 'lru_conv_w': _jnp.float32, 'lru_conv_b': _jnp.float32, 'lru_w_a': _jnp.float32, 'lru_b_a': _jnp.float32, 'lru_w_x': _jnp.float32, 'lru_b_x': _jnp.float32, 'lru_a_param': _jnp.float32, 'lru_w_out': _jnp.float32, 'pool_w': _jnp.float32, 'pool_b': _jnp.float32, 'pool_scale': _jnp.float32, 'ffn2_norm': _jnp.float32, 'ffn2_w_gate': _jnp.float32, 'ffn2_w_up': _jnp.float32, 'ffn2_w_down': _jnp.float32, 'ple_norm': _jnp.float32, 'ple_w_gate': _jnp.float32, 'ple_w_proj': _jnp.float32, 'final_norm': _jnp.float32}
MOMENT_SCALE = {'ffn1_norm': 8.363381e-02, 'ffn1_w_gate': 3.553653e-02, 'ffn1_w_up': 3.505074e-02, 'ffn1_w_down': 5.824482e-02, 'mix_norm': 1.978719e-01, 'lru_w_in': 9.240788e-02, 'lru_conv_w': 1.038781e-01, 'lru_conv_b': 1.420391e+00, 'lru_w_a': 4.047453e-02, 'lru_b_a': 2.724298e-02, 'lru_w_x': 7.358761e-02, 'lru_b_x': 3.842400e-02, 'lru_a_param': 4.706232e-02, 'lru_w_out': 1.383861e-01, 'pool_w': 2.185666e-01, 'pool_b': 6.000825e-01, 'pool_scale': 1.960973e+00, 'ffn2_norm': 7.220790e-02, 'ffn2_w_gate': 2.979292e-02, 'ffn2_w_up': 2.903352e-02, 'ffn2_w_down': 4.831796e-02, 'ple_norm': 3.433961e-02, 'ple_w_gate': 3.460353e-02, 'ple_w_proj': 8.731298e-02, 'final_norm': 6.464947e+01}


def _to_microbatches(a, axis):
    t = _jnp.moveaxis(a, axis, 0)
    t = t.reshape((N_MICROBATCH, t.shape[0] // N_MICROBATCH) + t.shape[1:])
    return _jnp.moveaxis(t, 1, axis + 1)


def setup_inputs(seed: int = 0) -> dict:
    inp = _fwd_setup_inputs(seed)
    key = _jax.random.fold_in(_jax.random.key(seed), 7919)
    shape, _ = _output_shape()
    out = dict(inp)
    out["loss_target"] = _jax.random.normal(_jax.random.fold_in(key, 0), shape, _jnp.float32)
    for i, name in enumerate(TWIN_WEIGHTS):
        w = inp[name].astype(_jnp.float32)
        if MOMENT_SCALE is None:
            s = _jnp.sqrt(_jnp.mean(_jnp.square(w)) + 1e-30)
        else:
            s = MOMENT_SCALE[name]
        km, kv = _jax.random.split(_jax.random.fold_in(key, i + 1))
        out[name] = w
        out["m_" + name] = s * _jax.random.normal(km, w.shape, _jnp.float32)
        out["v_" + name] = (s * s) * _jax.random.uniform(kv, w.shape, _jnp.float32, 0.5, 1.5)
    if N_MICROBATCH > 1:
        for name, axis in PER_EXAMPLE_BATCH_AXIS.items():
            out[name] = _to_microbatches(out[name], axis)
    return {'x': out['x'], 'p': out['p'], 'ffn1_norm': out['ffn1_norm'], 'ffn1_w_gate': out['ffn1_w_gate'], 'ffn1_w_up': out['ffn1_w_up'], 'ffn1_w_down': out['ffn1_w_down'], 'mix_norm': out['mix_norm'], 'lru_w_in': out['lru_w_in'], 'lru_conv_w': out['lru_conv_w'], 'lru_conv_b': out['lru_conv_b'], 'lru_w_a': out['lru_w_a'], 'lru_b_a': out['lru_b_a'], 'lru_w_x': out['lru_w_x'], 'lru_b_x': out['lru_b_x'], 'lru_a_param': out['lru_a_param'], 'lru_w_out': out['lru_w_out'], 'pool_w': out['pool_w'], 'pool_b': out['pool_b'], 'pool_scale': out['pool_scale'], 'ffn2_norm': out['ffn2_norm'], 'ffn2_w_gate': out['ffn2_w_gate'], 'ffn2_w_up': out['ffn2_w_up'], 'ffn2_w_down': out['ffn2_w_down'], 'ple_norm': out['ple_norm'], 'ple_w_gate': out['ple_w_gate'], 'ple_w_proj': out['ple_w_proj'], 'final_norm': out['final_norm'], 'loss_target': out['loss_target'], 'm_ffn1_norm': out['m_ffn1_norm'], 'm_ffn1_w_gate': out['m_ffn1_w_gate'], 'm_ffn1_w_up': out['m_ffn1_w_up'], 'm_ffn1_w_down': out['m_ffn1_w_down'], 'm_mix_norm': out['m_mix_norm'], 'm_lru_w_in': out['m_lru_w_in'], 'm_lru_conv_w': out['m_lru_conv_w'], 'm_lru_conv_b': out['m_lru_conv_b'], 'm_lru_w_a': out['m_lru_w_a'], 'm_lru_b_a': out['m_lru_b_a'], 'm_lru_w_x': out['m_lru_w_x'], 'm_lru_b_x': out['m_lru_b_x'], 'm_lru_a_param': out['m_lru_a_param'], 'm_lru_w_out': out['m_lru_w_out'], 'm_pool_w': out['m_pool_w'], 'm_pool_b': out['m_pool_b'], 'm_pool_scale': out['m_pool_scale'], 'm_ffn2_norm': out['m_ffn2_norm'], 'm_ffn2_w_gate': out['m_ffn2_w_gate'], 'm_ffn2_w_up': out['m_ffn2_w_up'], 'm_ffn2_w_down': out['m_ffn2_w_down'], 'm_ple_norm': out['m_ple_norm'], 'm_ple_w_gate': out['m_ple_w_gate'], 'm_ple_w_proj': out['m_ple_w_proj'], 'm_final_norm': out['m_final_norm'], 'v_ffn1_norm': out['v_ffn1_norm'], 'v_ffn1_w_gate': out['v_ffn1_w_gate'], 'v_ffn1_w_up': out['v_ffn1_w_up'], 'v_ffn1_w_down': out['v_ffn1_w_down'], 'v_mix_norm': out['v_mix_norm'], 'v_lru_w_in': out['v_lru_w_in'], 'v_lru_conv_w': out['v_lru_conv_w'], 'v_lru_conv_b': out['v_lru_conv_b'], 'v_lru_w_a': out['v_lru_w_a'], 'v_lru_b_a': out['v_lru_b_a'], 'v_lru_w_x': out['v_lru_w_x'], 'v_lru_b_x': out['v_lru_b_x'], 'v_lru_a_param': out['v_lru_a_param'], 'v_lru_w_out': out['v_lru_w_out'], 'v_pool_w': out['v_pool_w'], 'v_pool_b': out['v_pool_b'], 'v_pool_scale': out['v_pool_scale'], 'v_ffn2_norm': out['v_ffn2_norm'], 'v_ffn2_w_gate': out['v_ffn2_w_gate'], 'v_ffn2_w_up': out['v_ffn2_w_up'], 'v_ffn2_w_down': out['v_ffn2_w_down'], 'v_ple_norm': out['v_ple_norm'], 'v_ple_w_gate': out['v_ple_w_gate'], 'v_ple_w_proj': out['v_ple_w_proj'], 'v_final_norm': out['v_final_norm']}


def _loss(weights, diff, rest, loss_target):
    with _jax.named_scope("forward"):
        args = {**rest, TWIN_DIFF_INPUT: diff, **{k: w.astype(_WEIGHT_DTYPES[k]) for k, w in weights.items()}}
        y = _forward(args)
    with _jax.named_scope("loss_head"):
        err = _jnp.square(y.astype(_jnp.float32) - loss_target)
        return 0.5 * _jnp.sum(_jnp.mean(err, axis=-1)) if err.ndim else 0.5 * err


def _adamw(w, g, m, v):
    m = ADAM_B1 * m + (1.0 - ADAM_B1) * g
    v = ADAM_B2 * v + (1.0 - ADAM_B2) * _jnp.square(g)
    m_hat = m / (1.0 - ADAM_B1 ** ADAM_STEP)
    v_hat = v / (1.0 - ADAM_B2 ** ADAM_STEP)
    delta = -ADAM_LR * (m_hat / (_jnp.sqrt(v_hat) + ADAM_EPS) + ADAM_WD * w)
    return delta, m, v


def reference(x, p, ffn1_norm, ffn1_w_gate, ffn1_w_up, ffn1_w_down, mix_norm, lru_w_in, lru_conv_w, lru_conv_b, lru_w_a, lru_b_a, lru_w_x, lru_b_x, lru_a_param, lru_w_out, pool_w, pool_b, pool_scale, ffn2_norm, ffn2_w_gate, ffn2_w_up, ffn2_w_down, ple_norm, ple_w_gate, ple_w_proj, final_norm, loss_target, m_ffn1_norm, m_ffn1_w_gate, m_ffn1_w_up, m_ffn1_w_down, m_mix_norm, m_lru_w_in, m_lru_conv_w, m_lru_conv_b, m_lru_w_a, m_lru_b_a, m_lru_w_x, m_lru_b_x, m_lru_a_param, m_lru_w_out, m_pool_w, m_pool_b, m_pool_scale, m_ffn2_norm, m_ffn2_w_gate, m_ffn2_w_up, m_ffn2_w_down, m_ple_norm, m_ple_w_gate, m_ple_w_proj, m_final_norm, v_ffn1_norm, v_ffn1_w_gate, v_ffn1_w_up, v_ffn1_w_down, v_mix_norm, v_lru_w_in, v_lru_conv_w, v_lru_conv_b, v_lru_w_a, v_lru_b_a, v_lru_w_x, v_lru_b_x, v_lru_a_param, v_lru_w_out, v_pool_w, v_pool_b, v_pool_scale, v_ffn2_norm, v_ffn2_w_gate, v_ffn2_w_up, v_ffn2_w_down, v_ple_norm, v_ple_w_gate, v_ple_w_proj, v_final_norm):
    given = dict(x=x, p=p, ffn1_norm=ffn1_norm, ffn1_w_gate=ffn1_w_gate, ffn1_w_up=ffn1_w_up, ffn1_w_down=ffn1_w_down, mix_norm=mix_norm, lru_w_in=lru_w_in, lru_conv_w=lru_conv_w, lru_conv_b=lru_conv_b, lru_w_a=lru_w_a, lru_b_a=lru_b_a, lru_w_x=lru_w_x, lru_b_x=lru_b_x, lru_a_param=lru_a_param, lru_w_out=lru_w_out, pool_w=pool_w, pool_b=pool_b, pool_scale=pool_scale, ffn2_norm=ffn2_norm, ffn2_w_gate=ffn2_w_gate, ffn2_w_up=ffn2_w_up, ffn2_w_down=ffn2_w_down, ple_norm=ple_norm, ple_w_gate=ple_w_gate, ple_w_proj=ple_w_proj, final_norm=final_norm, loss_target=loss_target, m_ffn1_norm=m_ffn1_norm, m_ffn1_w_gate=m_ffn1_w_gate, m_ffn1_w_up=m_ffn1_w_up, m_ffn1_w_down=m_ffn1_w_down, m_mix_norm=m_mix_norm, m_lru_w_in=m_lru_w_in, m_lru_conv_w=m_lru_conv_w, m_lru_conv_b=m_lru_conv_b, m_lru_w_a=m_lru_w_a, m_lru_b_a=m_lru_b_a, m_lru_w_x=m_lru_w_x, m_lru_b_x=m_lru_b_x, m_lru_a_param=m_lru_a_param, m_lru_w_out=m_lru_w_out, m_pool_w=m_pool_w, m_pool_b=m_pool_b, m_pool_scale=m_pool_scale, m_ffn2_norm=m_ffn2_norm, m_ffn2_w_gate=m_ffn2_w_gate, m_ffn2_w_up=m_ffn2_w_up, m_ffn2_w_down=m_ffn2_w_down, m_ple_norm=m_ple_norm, m_ple_w_gate=m_ple_w_gate, m_ple_w_proj=m_ple_w_proj, m_final_norm=m_final_norm, v_ffn1_norm=v_ffn1_norm, v_ffn1_w_gate=v_ffn1_w_gate, v_ffn1_w_up=v_ffn1_w_up, v_ffn1_w_down=v_ffn1_w_down, v_mix_norm=v_mix_norm, v_lru_w_in=v_lru_w_in, v_lru_conv_w=v_lru_conv_w, v_lru_conv_b=v_lru_conv_b, v_lru_w_a=v_lru_w_a, v_lru_b_a=v_lru_b_a, v_lru_w_x=v_lru_w_x, v_lru_b_x=v_lru_b_x, v_lru_a_param=v_lru_a_param, v_lru_w_out=v_lru_w_out, v_pool_w=v_pool_w, v_pool_b=v_pool_b, v_pool_scale=v_pool_scale, v_ffn2_norm=v_ffn2_norm, v_ffn2_w_gate=v_ffn2_w_gate, v_ffn2_w_up=v_ffn2_w_up, v_ffn2_w_down=v_ffn2_w_down, v_ple_norm=v_ple_norm, v_ple_w_gate=v_ple_w_gate, v_ple_w_proj=v_ple_w_proj, v_final_norm=v_final_norm)
    weights = {n: given[n] for n in TWIN_WEIGHTS}
    shared = {n: given[n] for n in SHARED_INPUTS}
    per_example = {n: given[n] for n in ['x', 'p']}
    grad_fn = _jax.value_and_grad(_loss, argnums=(0, 1))

    def one_microbatch(ex, loss_target):
        ex = dict(ex)
        diff = ex.pop(TWIN_DIFF_INPUT)
        return grad_fn(weights, diff, {**shared, **ex}, loss_target)

    if N_MICROBATCH == 1:
        loss, (grad_w, grad_x) = one_microbatch(per_example, given["loss_target"])
    else:
        def body(carry, xs):
            loss_sum, grad_sum = carry
            l_k, (gw_k, gx_k) = one_microbatch(xs[0], xs[1])
            with _jax.named_scope("update"):
                return (loss_sum + l_k, _jax.tree.map(_jnp.add, grad_sum, gw_k)), gx_k

        init = (_jnp.zeros((), _jnp.float32), _jax.tree.map(_jnp.zeros_like, weights))
        (loss, grad_w), grad_x = _jax.lax.scan(body, init, (per_example, given["loss_target"]))
    with _jax.named_scope("update"):
        delta_w, new_m, new_v = {}, {}, {}
        for n in TWIN_WEIGHTS:
            delta_w[n], new_m[n], new_v[n] = _adamw(weights[n], grad_w[n], given["m_" + n], given["v_" + n])
    return (loss, grad_x, *[grad_w[n] for n in TWIN_WEIGHTS], *[delta_w[n] for n in TWIN_WEIGHTS],
            *[new_m[n] for n in TWIN_WEIGHTS], *[new_v[n] for n in TWIN_WEIGHTS])
```

```python
import functools

import jax
import jax.numpy as jnp
from jax import lax
from jax.experimental import pallas as pl
from jax.experimental.pallas import tpu as pltpu

F32 = jnp.float32
BF = jnp.bfloat16
MESH = pl.DeviceIdType.MESH

RMS_EPS = 1e-6
LRU_C = 8.0
LRU_HEADS = 16
CONV_WIDTH = 4
POOL_WINDOWS = (2, 4, 8, 16)
ADAM_LR, ADAM_B1, ADAM_B2, ADAM_EPS, ADAM_WD, ADAM_STEP = 0.001, 0.9, 0.999, 1e-08, 0.01, 10

N_DEV = 8
LANES = 128
SUBLANES = 8
HALO = 16
VMEM_LIMIT = 56 * 1024 * 1024

TM_FFN_FWD = 1024
TM_FFN_BWD = 512
TF_FFN = 256
TK_FFN_WG = 1024
TB_SEQ = 256
TB_SCAN = 512
TC_SCAN = 256
TM_EW = 512
TM_MM, TN_MM, TK_MM = 1024, 512, 1024
TR_ADAM = 512


def _tile(n, pref, align):
    if n <= pref:
        return n
    t = (pref // align) * align
    while t >= align:
        if n % t == 0:
            return t
        t -= align
    raise ValueError(f"no tile for {n} (pref {pref}, align {align})")


def _params(*sem):
    return pltpu.CompilerParams(dimension_semantics=sem, vmem_limit_bytes=VMEM_LIMIT)


def _dot(a, b):
    return lax.dot_general(a, b, (((1,), (0,)), ((), ())), preferred_element_type=F32)


def _dot_nt(a, b):
    return lax.dot_general(a, b, (((1,), (1,)), ((), ())), preferred_element_type=F32)


def _dot_tn(a, b):
    return lax.dot_general(a, b, (((0,), (0,)), ((), ())), preferred_element_type=F32)


def _sigmoid(x):
    return 1.0 / (1.0 + jnp.exp(-x))


def _gelu_parts(x):
    k0, k1 = 0.7978845608028654, 0.044715
    t = jnp.tanh(k0 * (x + k1 * x * x * x))
    g = 0.5 * x * (1.0 + t)
    dg = 0.5 * (1.0 + t) + 0.5 * x * (1.0 - t * t) * k0 * (1.0 + 3.0 * k1 * x * x)
    return g, dg


def _neg_expm1(x):
    p = x * (1.0 + x * (0.5 + x * (1.0 / 6 + x * (1.0 / 24 + x * (1.0 / 120 + x * (1.0 / 720 + x * (1.0 / 5040)))))))
    return jnp.where(x > -0.35, -p, 1.0 - jnp.exp(x))


def _softplus_neg(l):
    u = jnp.exp(-jnp.abs(l))
    w = 1.0 + u
    log1p = jnp.where(w == 1.0, u, jnp.log(w) * (u / jnp.where(w == 1.0, 1.0, w - 1.0)))
    return jnp.maximum(-l, 0.0) + log1p


def _rms_parts(x, g):
    r = lax.rsqrt(jnp.mean(x * x, axis=-1, keepdims=True) + RMS_EPS)
    nhat = x * r
    return nhat * g, nhat, r


def _rms_bwd_parts(x, g, dn):
    _, nhat, r = _rms_parts(x, g)
    u = dn * g
    dx = r * (u - nhat * jnp.mean(u * nhat, axis=-1, keepdims=True))
    return dx, jnp.sum(dn * nhat, axis=0, keepdims=True)


def _row_spec(tm, d):
    return pl.BlockSpec((tm, d), lambda i, *_: (i, 0))


def _vec_spec(d, rows=1):
    return pl.BlockSpec((rows, d), lambda *_: (0, 0))


def _mm(x, w, mode, name, out_dtype=F32, res=None, alpha=1.0, tm=None, tn=None, tk=None):
    if mode == "nn":
        (M, K), (_, N) = x.shape, w.shape
    elif mode == "nt":
        (M, K), (N, _) = x.shape, w.shape
    else:
        (K, M), (_, N) = x.shape, w.shape
    tm = _tile(M, tm or TM_MM, LANES if mode == "tn" else SUBLANES)
    tn = _tile(N, tn or TN_MM, LANES)
    tk = _tile(K, tk or TK_MM, LANES if mode != "tn" else 16)
    nk = K // tk
    dot = {"nn": _dot, "nt": _dot_nt, "tn": _dot_tn}[mode]

    def body(*refs):
        if res is None:
            x_ref, w_ref, o_ref, acc = refs
        else:
            x_ref, w_ref, r_ref, o_ref, acc = refs
        k = pl.program_id(2)

        @pl.when(k == 0)
        def _():
            acc[...] = jnp.zeros_like(acc)

        acc[...] += dot(x_ref[...].astype(BF), w_ref[...].astype(BF))

        @pl.when(k == nk - 1)
        def _():
            r = acc[...] if alpha == 1.0 else acc[...] * alpha
            if res is not None:
                r = r_ref[...] + r
            o_ref[...] = r.astype(out_dtype)

    if mode == "nn":
        specs = [pl.BlockSpec((tm, tk), lambda i, j, k: (i, k)), pl.BlockSpec((tk, tn), lambda i, j, k: (k, j))]
    elif mode == "nt":
        specs = [pl.BlockSpec((tm, tk), lambda i, j, k: (i, k)), pl.BlockSpec((tn, tk), lambda i, j, k: (j, k))]
    else:
        specs = [pl.BlockSpec((tk, tm), lambda i, j, k: (k, i)), pl.BlockSpec((tk, tn), lambda i, j, k: (k, j))]
    args = [x, w]
    if res is not None:
        specs.append(pl.BlockSpec((tm, tn), lambda i, j, k: (i, j)))
        args.append(res)
    return pl.pallas_call(
        body, name=name, grid=(M // tm, N // tn, nk), in_specs=specs,
        out_specs=pl.BlockSpec((tm, tn), lambda i, j, k: (i, j)),
        out_shape=jax.ShapeDtypeStruct((M, N), out_dtype),
        scratch_shapes=[pltpu.VMEM((tm, tn), F32)],
        compiler_params=_params("parallel", "parallel", "arbitrary"),
    )(*args)


def _rms_fwd(h, g, out_dtype, name):
    T, D = h.shape
    tm = _tile(T, TM_EW, 16)

    def body(h_ref, g_ref, o_ref):
        o_ref[...] = _rms_parts(h_ref[...], g_ref[...])[0].astype(out_dtype)

    return pl.pallas_call(
        body, name=name, grid=(T // tm,), in_specs=[_row_spec(tm, D), _vec_spec(D)], out_specs=_row_spec(tm, D),
        out_shape=jax.ShapeDtypeStruct((T, D), out_dtype), compiler_params=_params("parallel"),
    )(h, g.reshape(1, D))


def _rms_bwd(h, g, dn, dres, name):
    T, D = h.shape
    tm = _tile(T, TM_EW, 16)

    def body(h_ref, g_ref, dn_ref, dr_ref, dh_ref, dg_ref):
        @pl.when(pl.program_id(0) == 0)
        def _():
            dg_ref[...] = jnp.zeros_like(dg_ref)

        dx, dg = _rms_bwd_parts(h_ref[...], g_ref[...], dn_ref[...].astype(F32))
        dh_ref[...] = dr_ref[...] + dx
        dg_ref[...] += dg

    return pl.pallas_call(
        body, name=name, grid=(T // tm,),
        in_specs=[_row_spec(tm, D), _vec_spec(D), _row_spec(tm, D), _row_spec(tm, D)],
        out_specs=[_row_spec(tm, D), _vec_spec(D)],
        out_shape=[jax.ShapeDtypeStruct((T, D), F32), jax.ShapeDtypeStruct((1, D), F32)],
        compiler_params=_params("arbitrary"),
    )(h, g.reshape(1, D), dn, dres)


def _loss_head(h, g, tgt):
    T, D = h.shape
    tm = _tile(T, TM_EW, 16)

    def body(h_ref, g_ref, t_ref, loss_ref, dh_ref, dg_ref):
        @pl.when(pl.program_id(0) == 0)
        def _():
            dg_ref[...] = jnp.zeros_like(dg_ref)
            loss_ref[...] = jnp.zeros_like(loss_ref)

        x, gg = h_ref[...], g_ref[...]
        y = _rms_parts(x, gg)[0]
        e = y - t_ref[...]
        part = jnp.sum(jnp.sum(e * e, axis=0, keepdims=True), axis=1, keepdims=True) * (0.5 / D)
        loss_ref[...] += jnp.broadcast_to(part, loss_ref.shape)
        dx, dg = _rms_bwd_parts(x, gg, e * (1.0 / D))
        dh_ref[...] = dx
        dg_ref[...] += dg

    return pl.pallas_call(
        body, name="loss_head", grid=(T // tm,),
        in_specs=[_row_spec(tm, D), _vec_spec(D), _row_spec(tm, D)],
        out_specs=[_vec_spec(LANES), _row_spec(tm, D), _vec_spec(D)],
        out_shape=[jax.ShapeDtypeStruct((1, LANES), F32), jax.ShapeDtypeStruct((T, D), F32),
                   jax.ShapeDtypeStruct((1, D), F32)],
        compiler_params=_params("arbitrary"),
    )(h, g.reshape(1, D), tgt)


def _ffn_w_specs(tf, D, base, imap):
    return [pl.BlockSpec((None, tf, D), functools.partial(imap, base + k)) for k in range(3)]


def _ffn_fwd(h, g, wffn, base, name):
    T, D = h.shape
    F = wffn.shape[1]
    tm, tf = _tile(T, TM_FFN_FWD, 16), _tile(F, TF_FFN, LANES)
    nf = F // tf

    def body(h_ref, g_ref, wg_ref, wu_ref, wd_ref, a_ref, b_ref, o_ref, n_sc, acc_sc):
        j = pl.program_id(1)

        @pl.when(j == 0)
        def _():
            n_sc[...] = _rms_parts(h_ref[...], g_ref[...])[0].astype(BF)
            acc_sc[...] = jnp.zeros_like(acc_sc)

        n = n_sc[...]
        a = _dot_nt(n, wg_ref[...])
        b = _dot_nt(n, wu_ref[...])
        s = a * _sigmoid(a) * b
        acc_sc[...] += _dot(s.astype(BF), wd_ref[...])
        a_ref[...] = a.astype(BF)
        b_ref[...] = b.astype(BF)

        @pl.when(j == nf - 1)
        def _():
            o_ref[...] = h_ref[...] + 0.5 * acc_sc[...]

    tile = pl.BlockSpec((tm, tf), lambda i, j: (i, j))
    return pl.pallas_call(
        body, name=name, grid=(T // tm, nf),
        in_specs=[_row_spec(tm, D), _vec_spec(D)] + _ffn_w_specs(tf, D, base, lambda b, i, j: (b, j, 0)),
        out_specs=[tile, tile, _row_spec(tm, D)],
        out_shape=[jax.ShapeDtypeStruct((T, F), BF), jax.ShapeDtypeStruct((T, F), BF),
                   jax.ShapeDtypeStruct((T, D), F32)],
        scratch_shapes=[pltpu.VMEM((tm, D), BF), pltpu.VMEM((tm, D), F32)],
        compiler_params=_params("parallel", "arbitrary"),
    )(h, g.reshape(1, D), wffn, wffn, wffn)


def _ffn_bwd_act(dh, h, g, a, b, wffn, base, name):
    T, D = h.shape
    F = wffn.shape[1]
    tm, tf = _tile(T, TM_FFN_BWD, 16), _tile(F, TF_FFN, LANES)
    nf = F // tf

    def body(dh_ref, h_ref, g_ref, a_ref, b_ref, wg_ref, wu_ref, wd_ref,
             da_ref, db_ref, s_ref, dhi_ref, n_ref, dhb_ref, dg_ref, dn_sc, dhb_sc):
        i, j = pl.program_id(0), pl.program_id(1)

        @pl.when(jnp.logical_and(i == 0, j == 0))
        def _():
            dg_ref[...] = jnp.zeros_like(dg_ref)

        @pl.when(j == 0)
        def _():
            dhb_sc[...] = dh_ref[...].astype(BF)
            dn_sc[...] = jnp.zeros_like(dn_sc)

        ds = 0.5 * _dot_nt(dhb_sc[...], wd_ref[...])
        av, bv = a_ref[...].astype(F32), b_ref[...].astype(F32)
        sig = _sigmoid(av)
        silu = av * sig
        da = (ds * bv * (sig * (1.0 + av * (1.0 - sig)))).astype(BF)
        db = (ds * silu).astype(BF)
        dn_sc[...] += _dot(da, wg_ref[...]) + _dot(db, wu_ref[...])
        da_ref[...] = da
        db_ref[...] = db
        s_ref[...] = (silu * bv).astype(BF)

        @pl.when(j == nf - 1)
        def _():
            x, gg = h_ref[...], g_ref[...]
            dx, dg = _rms_bwd_parts(x, gg, dn_sc[...])
            dhi_ref[...] = dh_ref[...] + dx
            dg_ref[...] += dg
            n_ref[...] = _rms_parts(x, gg)[0].astype(BF)
            dhb_ref[...] = dhb_sc[...]

    tile = pl.BlockSpec((tm, tf), lambda i, j: (i, j))
    return pl.pallas_call(
        body, name=name, grid=(T // tm, nf),
        in_specs=[_row_spec(tm, D), _row_spec(tm, D), _vec_spec(D), tile, tile]
        + _ffn_w_specs(tf, D, base, lambda b, i, j: (b, j, 0)),
        out_specs=[tile, tile, tile, _row_spec(tm, D), _row_spec(tm, D), _row_spec(tm, D), _vec_spec(D)],
        out_shape=[jax.ShapeDtypeStruct((T, F), BF)] * 3
        + [jax.ShapeDtypeStruct((T, D), F32), jax.ShapeDtypeStruct((T, D), BF), jax.ShapeDtypeStruct((T, D), BF),
           jax.ShapeDtypeStruct((1, D), F32)],
        scratch_shapes=[pltpu.VMEM((tm, D), F32), pltpu.VMEM((tm, D), BF)],
        compiler_params=_params("arbitrary", "arbitrary"),
    )(dh, h, g.reshape(1, D), a, b, wffn, wffn, wffn)


def _ffn_bwd_w(da, db, s, n, dhb, name):
    T, F = da.shape
    D = n.shape[1]
    tf, tk = _tile(F, TF_FFN, LANES), _tile(T, TK_FFN_WG, 16)
    nk = T // tk

    def body(da_ref, db_ref, s_ref, n_ref, dh_ref, g_ref, u_ref, d_ref, g_sc, u_sc, d_sc):
        k = pl.program_id(1)

        @pl.when(k == 0)
        def _():
            g_sc[...] = jnp.zeros_like(g_sc)
            u_sc[...] = jnp.zeros_like(u_sc)
            d_sc[...] = jnp.zeros_like(d_sc)

        nv = n_ref[...]
        g_sc[...] += _dot_tn(da_ref[...], nv)
        u_sc[...] += _dot_tn(db_ref[...], nv)
        d_sc[...] += _dot_tn(s_ref[...], dh_ref[...])

        @pl.when(k == nk - 1)
        def _():
            g_ref[...] = g_sc[...].astype(BF)
            u_ref[...] = u_sc[...].astype(BF)
            d_ref[...] = (0.5 * d_sc[...]).astype(BF)

    act = pl.BlockSpec((tk, tf), lambda j, k: (k, j))
    tok = pl.BlockSpec((tk, D), lambda j, k: (k, 0))
    out = pl.BlockSpec((tf, D), lambda j, k: (j, 0))
    return pl.pallas_call(
        body, name=name, grid=(F // tf, nk), in_specs=[act, act, act, tok, tok], out_specs=[out, out, out],
        out_shape=[jax.ShapeDtypeStruct((F, D), BF)] * 3,
        scratch_shapes=[pltpu.VMEM((tf, D), F32)] * 3,
        compiler_params=_params("parallel", "arbitrary"),
    )(da, db, s, n, dhb)


def _ple_fwd_ew(h, z, pp, name):
    T, D = h.shape
    tm = _tile(T, TM_EW, 16)

    def body(h_ref, z_ref, p_ref, o_ref):
        o_ref[...] = h_ref[...] + _sigmoid(z_ref[...]) * p_ref[...]

    return pl.pallas_call(
        body, name=name, grid=(T // tm,), in_specs=[_row_spec(tm, D)] * 3, out_specs=_row_spec(tm, D),
        out_shape=jax.ShapeDtypeStruct((T, D), F32), compiler_params=_params("parallel"),
    )(h, z, pp)


def _ple_bwd_ew(dh, z, pp, name):
    T, D = dh.shape
    tm = _tile(T, TM_EW, 16)

    def body(dh_ref, z_ref, p_ref, dz_ref, dp_ref):
        gate = _sigmoid(z_ref[...])
        d = dh_ref[...]
        dz_ref[...] = (d * p_ref[...] * gate * (1.0 - gate)).astype(BF)
        dp_ref[...] = (d * gate).astype(BF)

    return pl.pallas_call(
        body, name=name, grid=(T // tm,), in_specs=[_row_spec(tm, D)] * 3, out_specs=[_row_spec(tm, D)] * 2,
        out_shape=[jax.ShapeDtypeStruct((T, D), BF)] * 2, compiler_params=_params("parallel"),
    )(dh, z, pp)


def _lru_gates(z, conv_w, conv_b, wa, wx, b_a, b_x, a_param, name):
    T, R2 = z.shape
    R = R2 // 2
    tb = _tile(T, TB_SEQ, HALO)
    per = tb // HALO

    def body(x_ref, halo_ref, cw_ref, cb_ref, wa_ref, wx_ref, ba_ref, bx_ref, ap_ref,
             xc_ref, r_ref, ig_ref, a_ref, bt_ref, ext):
        i = pl.program_id(0)
        ext[pl.ds(0, HALO), :] = jnp.where(i > 0, halo_ref[...], 0.0)
        ext[pl.ds(HALO, tb), :] = x_ref[...]
        xc = cb_ref[...] + cw_ref[0:1, :] * ext[pl.ds(HALO - 3, tb), :]
        for k in range(1, CONV_WIDTH):
            xc = xc + cw_ref[k:k + 1, :] * ext[pl.ds(HALO - 3 + k, tb), :]
        xcb = xc.astype(BF)
        r = _sigmoid(_dot(xcb, wa_ref[...]) + ba_ref[...])
        ig = _sigmoid(_dot(xcb, wx_ref[...]) + bx_ref[...])
        la = -LRU_C * r * _softplus_neg(ap_ref[...])
        xc_ref[...] = xc
        r_ref[...] = r
        ig_ref[...] = ig
        a_ref[...] = jnp.exp(la)
        bt_ref[...] = jnp.sqrt(_neg_expm1(2.0 * la)) * (ig * xc)

    tile = pl.BlockSpec((tb, R), lambda i: (i, 1))
    halo = pl.BlockSpec((HALO, R), lambda i: (jnp.maximum(i * per - 1, 0), 1))
    out = pl.BlockSpec((tb, R), lambda i: (i, 0))
    return pl.pallas_call(
        body, name=name, grid=(T // tb,),
        in_specs=[tile, halo, _vec_spec(R, CONV_WIDTH), _vec_spec(R), _vec_spec(R, R), _vec_spec(R, R),
                  _vec_spec(R), _vec_spec(R), _vec_spec(R)],
        out_specs=[out] * 5, out_shape=[jax.ShapeDtypeStruct((T, R), F32)] * 5,
        scratch_shapes=[pltpu.VMEM((HALO + tb, R), F32)],
        compiler_params=_params("parallel"),
    )(z, z, conv_w, conv_b.reshape(1, R), wa, wx, b_a.reshape(1, R), b_x.reshape(1, R), a_param.reshape(1, R))


def _lru_scan(a, b, reverse, name):
    T, R = a.shape
    tb, tc = _tile(T, TB_SCAN, SUBLANES), _tile(R, TC_SCAN, LANES)
    nt, ng = T // tb, tb // SUBLANES

    def body(a_ref, b_ref, o_ref, carry, a_sc, b_sc):
        @pl.when(pl.program_id(1) == 0)
        def _():
            carry[...] = jnp.zeros_like(carry)

        A = a_ref[...]
        B = A * b_ref[...] if reverse else b_ref[...]
        sub = lax.broadcasted_iota(jnp.int32, (tb, tc), 0) & (SUBLANES - 1)
        for k in (1, 2, 4):
            m = (sub < SUBLANES - k) if reverse else (sub >= k)
            shift = tb - k if reverse else k
            a_n = jnp.where(m, pltpu.roll(A, shift, 0), 1.0)
            b_n = jnp.where(m, pltpu.roll(B, shift, 0), 0.0)
            B = A * b_n + B
            A = A * a_n
        a_sc[...] = A
        b_sc[...] = B
        sub8 = lax.broadcasted_iota(jnp.int32, (SUBLANES, tc), 0)

        def group(q, c):
            g = (ng - 1 - q) if reverse else q
            rows = pl.ds(pl.multiple_of(g * SUBLANES, SUBLANES), SUBLANES)
            hg = a_sc[rows, :] * c + b_sc[rows, :]
            if reverse:
                nxt = jnp.where(sub8 == SUBLANES - 1, c, pltpu.roll(hg, SUBLANES - 1, 0))
                o_ref[rows, :] = b_ref[rows, :] + nxt
                return hg[0:1, :]
            o_ref[rows, :] = hg
            return hg[SUBLANES - 1:SUBLANES, :]

        carry[...] = lax.fori_loop(0, ng, group, carry[...])

    spec = pl.BlockSpec((tb, tc), (lambda c, t: (nt - 1 - t, c)) if reverse else (lambda c, t: (t, c)))
    return pl.pallas_call(
        body, name=name, grid=(R // tc, nt), in_specs=[spec, spec], out_specs=spec,
        out_shape=jax.ShapeDtypeStruct((T, R), F32),
        scratch_shapes=[pltpu.VMEM((1, tc), F32), pltpu.VMEM((tb, tc), F32), pltpu.VMEM((tb, tc), F32)],
        compiler_params=_params("parallel", "arbitrary"),
    )(a, b)


def _lru_out_ew(hs, z, name):
    T, R = hs.shape
    tm = _tile(T, TM_EW, 16)

    def body(h_ref, g_ref, y_ref):
        y_ref[...] = (h_ref[...] * _gelu_parts(g_ref[...])[0]).astype(BF)

    return pl.pallas_call(
        body, name=name, grid=(T // tm,), in_specs=[_row_spec(tm, R), _row_spec(tm, R)], out_specs=_row_spec(tm, R),
        out_shape=jax.ShapeDtypeStruct((T, R), BF), compiler_params=_params("parallel"),
    )(hs, z)


def _lru_bwd_ew(dy, hs, z, name):
    T, R = hs.shape
    tm = _tile(T, TM_EW, 16)

    def body(dy_ref, h_ref, g_ref, dhd_ref, dgb_ref):
        g, dg = _gelu_parts(g_ref[...])
        d = dy_ref[...]
        dhd_ref[...] = d * g
        dgb_ref[...] = (d * h_ref[...] * dg).astype(BF)

    return pl.pallas_call(
        body, name=name, grid=(T // tm,), in_specs=[_row_spec(tm, R)] * 3, out_specs=[_row_spec(tm, R)] * 2,
        out_shape=[jax.ShapeDtypeStruct((T, R), F32), jax.ShapeDtypeStruct((T, R), BF)],
        compiler_params=_params("parallel"),
    )(dy, hs, z)


def _lru_gates_bwd(lam, hs, r, ig, xc, wa, wx, a_param, name):
    T, R = lam.shape
    tb = _tile(T, TB_SEQ, HALO)
    per = tb // HALO
    nt = T // tb

    def body(l_ref, h_ref, hh_ref, r_ref, ig_ref, xc_ref, wa_ref, wx_ref, ap_ref,
             dxc_ref, dpa_ref, dpx_ref, dsp_ref, dba_ref, dbx_ref, ext):
        i = pl.program_id(0)

        @pl.when(i == 0)
        def _():
            dsp_ref[...] = jnp.zeros_like(dsp_ref)
            dba_ref[...] = jnp.zeros_like(dba_ref)
            dbx_ref[...] = jnp.zeros_like(dbx_ref)

        ext[pl.ds(0, HALO), :] = jnp.where(i > 0, hh_ref[...], 0.0)
        ext[pl.ds(HALO, tb), :] = h_ref[...]
        h_prev = ext[pl.ds(HALO - 1, tb), :]
        lam_v, rv, igv, xcv = l_ref[...], r_ref[...], ig_ref[...], xc_ref[...]
        sp = _softplus_neg(ap_ref[...])
        la = -LRU_C * rv * sp
        av = jnp.exp(la)
        mult = jnp.sqrt(_neg_expm1(2.0 * la))
        dla = lam_v * h_prev * av - lam_v * (igv * xcv) * (av * av) / mult
        du = lam_v * mult
        dpa = (dla * (-LRU_C) * sp) * rv * (1.0 - rv)
        dpx = (du * xcv) * igv * (1.0 - igv)
        dsp_ref[...] += jnp.sum(dla * (-LRU_C) * rv, axis=0, keepdims=True)
        dba_ref[...] += jnp.sum(dpa, axis=0, keepdims=True)
        dbx_ref[...] += jnp.sum(dpx, axis=0, keepdims=True)
        dpab, dpxb = dpa.astype(BF), dpx.astype(BF)
        dxc_ref[...] = du * igv + _dot_nt(dpab, wa_ref[...]) + _dot_nt(dpxb, wx_ref[...])
        dpa_ref[...] = dpab
        dpx_ref[...] = dpxb

        @pl.when(i == nt - 1)
        def _():
            dsp_ref[...] = dsp_ref[...] * (-_sigmoid(-ap_ref[...]))

    tile = _row_spec(tb, R)
    halo = pl.BlockSpec((HALO, R), lambda i: (jnp.maximum(i * per - 1, 0), 0))
    return pl.pallas_call(
        body, name=name, grid=(T // tb,),
        in_specs=[tile, tile, halo, tile, tile, tile, _vec_spec(R, R), _vec_spec(R, R), _vec_spec(R)],
        out_specs=[tile, tile, tile, _vec_spec(R), _vec_spec(R), _vec_spec(R)],
        out_shape=[jax.ShapeDtypeStruct((T, R), F32), jax.ShapeDtypeStruct((T, R), BF), jax.ShapeDtypeStruct((T, R), BF)]
        + [jax.ShapeDtypeStruct((1, R), F32)] * 3,
        scratch_shapes=[pltpu.VMEM((HALO + tb, R), F32)],
        compiler_params=_params("arbitrary"),
    )(lam, hs, hs, r, ig, xc, wa, wx, a_param.reshape(1, R))


def _lru_conv_bwd(dxc, z, conv_w, name):
    T, R = dxc.shape
    tb = _tile(T, TB_SEQ, HALO)
    per = tb // HALO
    nt = T // tb

    def body(d_ref, dn_ref, x_ref, xp_ref, cw_ref, dxb_ref, dcw_ref, dcb_ref, dext, xext):
        i = pl.program_id(0)

        @pl.when(i == 0)
        def _():
            dcw_ref[...] = jnp.zeros_like(dcw_ref)
            dcb_ref[...] = jnp.zeros_like(dcb_ref)

        d = d_ref[...]
        dext[pl.ds(0, tb), :] = d
        dext[pl.ds(tb, HALO), :] = jnp.where(i < nt - 1, dn_ref[...], 0.0)
        xext[pl.ds(0, HALO), :] = jnp.where(i > 0, xp_ref[...], 0.0)
        xext[pl.ds(HALO, tb), :] = x_ref[...]
        dxb = cw_ref[CONV_WIDTH - 1:CONV_WIDTH, :] * d
        for k in range(CONV_WIDTH - 1):
            dxb = dxb + cw_ref[k:k + 1, :] * dext[pl.ds(CONV_WIDTH - 1 - k, tb), :]
        dxb_ref[...] = dxb.astype(BF)
        for k in range(CONV_WIDTH):
            dcw_ref[k:k + 1, :] += jnp.sum(d * xext[pl.ds(HALO - 3 + k, tb), :], axis=0, keepdims=True)
        dcb_ref[...] += jnp.sum(d, axis=0, keepdims=True)

    tile = _row_spec(tb, R)
    nxt = pl.BlockSpec((HALO, R), lambda i: (jnp.minimum((i + 1) * per, T // HALO - 1), 0))
    xtile = pl.BlockSpec((tb, R), lambda i: (i, 1))
    xprev = pl.BlockSpec((HALO, R), lambda i: (jnp.maximum(i * per - 1, 0), 1))
    return pl.pallas_call(
        body, name=name, grid=(nt,), in_specs=[tile, nxt, xtile, xprev, _vec_spec(R, CONV_WIDTH)],
        out_specs=[tile, _vec_spec(R, SUBLANES), _vec_spec(R)],
        out_shape=[jax.ShapeDtypeStruct((T, R), BF), jax.ShapeDtypeStruct((SUBLANES, R), F32),
                   jax.ShapeDtypeStruct((1, R), F32)],
        scratch_shapes=[pltpu.VMEM((tb + HALO, R), F32), pltpu.VMEM((HALO + tb, R), F32)],
        compiler_params=_params("arbitrary"),
    )(dxc, dxc, z, z, conv_w)


def _window_sums(e, n, back):
    out, s = [], e
    for k in (1, 2, 4, 8):
        s = s + pltpu.roll(s, k if back else n - k, 0)
        out.append(s)
    return out


def _pool_fwd(hn, h, w, b, scale, name):
    T, D = hn.shape
    G = len(POOL_WINDOWS)
    gd = D // G
    tb = _tile(T, TB_SEQ, HALO)
    per = tb // HALO

    def body(x_ref, xp_ref, h_ref, w_ref, b_ref, s_ref, o_ref, u_ref, yb_ref):
        i = pl.program_id(0)
        t = i * tb + lax.broadcasted_iota(jnp.int32, (tb, gd), 0) + 1
        for g, win in enumerate(POOL_WINDOWS):
            cols = pl.ds(g * gd, gd)
            x = x_ref[:, cols]
            e = jnp.concatenate([jnp.where(i > 0, xp_ref[:, cols], 0.0), x], axis=0)
            sw = _window_sums(e, HALO + tb, True)[g][HALO:, :]
            u = (sw / jnp.minimum(t, win).astype(F32) - x).astype(BF)
            yb = _dot(u, w_ref[g]) + b_ref[:, cols]
            u_ref[:, cols] = u
            yb_ref[:, cols] = yb
            o_ref[:, cols] = h_ref[:, cols] + yb * s_ref[:, cols]

    tile = _row_spec(tb, D)
    prev = pl.BlockSpec((HALO, D), lambda i: (jnp.maximum(i * per - 1, 0), 0))
    return pl.pallas_call(
        body, name=name, grid=(T // tb,),
        in_specs=[tile, prev, tile, pl.BlockSpec((G, gd, gd), lambda i: (0, 0, 0)), _vec_spec(D), _vec_spec(D)],
        out_specs=[tile, tile, tile],
        out_shape=[jax.ShapeDtypeStruct((T, D), F32), jax.ShapeDtypeStruct((T, D), BF), jax.ShapeDtypeStruct((T, D), F32)],
        compiler_params=_params("parallel"),
    )(hn, hn, h, w, b.reshape(1, D), scale.reshape(1, D))


def _pool_bwd(dm, u, yb, w, scale, name):
    T, D = dm.shape
    G = len(POOL_WINDOWS)
    gd = D // G
    tb = _tile(T, TB_SEQ, HALO)

    def body(d_ref, u_ref, yb_ref, w_ref, s_ref, du_ref, v_ref, dw_ref, db_ref, ds_ref):
        i = pl.program_id(0)

        @pl.when(i == 0)
        def _():
            dw_ref[...] = jnp.zeros_like(dw_ref)
            db_ref[...] = jnp.zeros_like(db_ref)
            ds_ref[...] = jnp.zeros_like(ds_ref)

        d, sc = d_ref[...], s_ref[...]
        ds_ref[...] += jnp.sum(d * yb_ref[...], axis=0, keepdims=True)
        db_ref[...] += jnp.sum(d * sc, axis=0, keepdims=True)
        t = i * tb + lax.broadcasted_iota(jnp.int32, (tb, gd), 0) + 1
        for g, win in enumerate(POOL_WINDOWS):
            cols = pl.ds(g * gd, gd)
            dy = (d_ref[:, cols] * s_ref[:, cols]).astype(BF)
            du = _dot_nt(dy, w_ref[g])
            dw_ref[g] += _dot_tn(u_ref[:, cols], dy)
            du_ref[:, cols] = du
            v_ref[:, cols] = du / jnp.minimum(t, win).astype(F32)

    tile = _row_spec(tb, D)
    return pl.pallas_call(
        body, name=name, grid=(T // tb,),
        in_specs=[tile, tile, tile, pl.BlockSpec((G, gd, gd), lambda i: (0, 0, 0)), _vec_spec(D)],
        out_specs=[tile, tile, pl.BlockSpec((G, gd, gd), lambda i: (0, 0, 0)), _vec_spec(D), _vec_spec(D)],
        out_shape=[jax.ShapeDtypeStruct((T, D), F32), jax.ShapeDtypeStruct((T, D), F32),
                   jax.ShapeDtypeStruct((G, gd, gd), F32), jax.ShapeDtypeStruct((1, D), F32),
                   jax.ShapeDtypeStruct((1, D), F32)],
        compiler_params=_params("arbitrary"),
    )(dm, u, yb, w, scale.reshape(1, D))


def _pool_bwd_win(v, du, name):
    T, D = v.shape
    G = len(POOL_WINDOWS)
    gd = D // G
    tb = _tile(T, TB_SEQ, HALO)
    per = tb // HALO
    nt = T // tb

    def body(v_ref, vn_ref, du_ref, o_ref):
        i = pl.program_id(0)
        for g in range(G):
            cols = pl.ds(g * gd, gd)
            e = jnp.concatenate([v_ref[:, cols], jnp.where(i < nt - 1, vn_ref[:, cols], 0.0)], axis=0)
            o_ref[:, cols] = _window_sums(e, tb + HALO, False)[g][:tb, :] - du_ref[:, cols]

    tile = _row_spec(tb, D)
    nxt = pl.BlockSpec((HALO, D), lambda i: (jnp.minimum((i + 1) * per, T // HALO - 1), 0))
    return pl.pallas_call(
        body, name=name, grid=(nt,), in_specs=[tile, nxt, tile], out_specs=tile,
        out_shape=jax.ShapeDtypeStruct((T, D), F32), compiler_params=_params("parallel"),
    )(v, v, du)


def _adamw(w, g, m, v, name):
    shape = w.shape
    cols = shape[-1] if w.ndim > 1 else shape[0]
    rows = w.size // cols
    tr = _tile(rows, TR_ADAM, SUBLANES)
    c1, c2 = 1.0 / (1.0 - ADAM_B1 ** ADAM_STEP), 1.0 / (1.0 - ADAM_B2 ** ADAM_STEP)

    def body(w_ref, g_ref, m_ref, v_ref, d_ref, mo_ref, vo_ref):
        gv = g_ref[...]
        mn = ADAM_B1 * m_ref[...] + (1.0 - ADAM_B1) * gv
        vn = ADAM_B2 * v_ref[...] + (1.0 - ADAM_B2) * (gv * gv)
        d_ref[...] = -ADAM_LR * ((mn * c1) / (jnp.sqrt(vn * c2) + ADAM_EPS) + ADAM_WD * w_ref[...])
        mo_ref[...] = mn
        vo_ref[...] = vn

    spec = _row_spec(tr, cols)
    outs = pl.pallas_call(
        body, name=name, grid=(rows // tr,), in_specs=[spec] * 4, out_specs=[spec] * 3,
        out_shape=[jax.ShapeDtypeStruct((rows, cols), F32)] * 3, compiler_params=_params("parallel"),
    )(*[t.reshape(rows, cols) for t in (w, g, m, v)])
    return [o.reshape(shape) for o in outs]


def _sum_devices(parts, name):
    n, rows, cols = parts.shape
    tr = _tile(rows, 1024, SUBLANES)

    def body(p_ref, o_ref):
        acc = p_ref[0]
        for k in range(1, n):
            acc = acc + p_ref[k]
        o_ref[...] = acc

    return pl.pallas_call(
        body, name=name, grid=(rows // tr,), in_specs=[pl.BlockSpec((n, tr, cols), lambda i: (0, i, 0))],
        out_specs=_row_spec(tr, cols), out_shape=jax.ShapeDtypeStruct((rows, cols), F32),
        compiler_params=_params("parallel"),
    )(parts)


def _position():
    return lax.axis_index("x"), lax.axis_index("y"), lax.axis_index("c")


def _all_gather(blocks, name):
    ng = len(blocks)

    def body(*refs):
        srcs, outs = refs[:ng], refs[ng:2 * ng]
        send_sems, recv_sems, local_sems = refs[2 * ng:]
        x, y, c = _position()
        me, sibling = (x, y, c), (x, y, 1 - c)
        chips = [(1 - x, y), (x, 1 - y), (1 - x, 1 - y)]

        def rows(g, px, py, pc):
            r = blocks[g].shape[1]
            return outs[g].at[:, pl.ds((4 * px + 2 * py + pc) * r, r), :]

        def copy(g, k, block, to, src=None):
            return pltpu.make_async_remote_copy(
                src_ref=rows(g, *block) if src is None else src, dst_ref=rows(g, *block),
                send_sem=send_sems.at[g, k], recv_sem=recv_sems.at[g, k], device_id=to, device_id_type=MESH)

        mine = [pltpu.make_async_copy(srcs[g], rows(g, *me), local_sems.at[g]) for g in range(ng)]
        for cp in mine:
            cp.start()
        first = []
        for g in range(ng):
            first.append(copy(g, 0, me, sibling, src=srcs[g]))
            first += [copy(g, 1 + j, me, (*chip, c), src=srcs[g]) for j, chip in enumerate(chips)]
        for cp in first:
            cp.start()
        passed = []
        for j, chip in enumerate(chips):
            for g in range(ng):
                copy(g, 1 + j, (*chip, c), me).wait_recv()
                fwd = copy(g, 4 + j, (*chip, c), sibling)
                fwd.start()
                passed.append(fwd)
        for g in range(ng):
            copy(g, 0, sibling, me).wait_recv()
            for j, chip in enumerate(chips):
                copy(g, 4 + j, (*chip, 1 - c), me).wait_recv()
        for cp in first + passed:
            cp.wait_send()
        for cp in mine:
            cp.wait()

    hbm = pl.BlockSpec(memory_space=pl.ANY)
    return pl.pallas_call(
        body, name=name, in_specs=[hbm] * ng, out_specs=[hbm] * ng,
        out_shape=[jax.ShapeDtypeStruct((b.shape[0], N_DEV * b.shape[1], b.shape[2]), b.dtype) for b in blocks],
        scratch_shapes=[pltpu.SemaphoreType.DMA((ng, 7)), pltpu.SemaphoreType.DMA((ng, 7)),
                        pltpu.SemaphoreType.DMA((ng,))],
    )(*blocks)


def _exchange_sibling(grads, name):
    ng = len(grads)

    def body(*refs):
        srcs, outs = refs[:ng], refs[ng:2 * ng]
        send_sems, recv_sems = refs[2 * ng:]
        x, y, c = _position()
        copies = []
        for g in range(ng):
            r = grads[g].shape[1] // N_DEV
            for q in range(4):
                copies.append(pltpu.make_async_remote_copy(
                    src_ref=srcs[g].at[:, pl.ds((2 * q + 1 - c) * r, r), :], dst_ref=outs[g].at[q],
                    send_sem=send_sems.at[g, q], recv_sem=recv_sems.at[g, q],
                    device_id=(x, y, 1 - c), device_id_type=MESH))
        for cp in copies:
            cp.start()
        for cp in copies:
            cp.wait()

    hbm = pl.BlockSpec(memory_space=pl.ANY)
    return pl.pallas_call(
        body, name=name, in_specs=[hbm] * ng, out_specs=[hbm] * ng,
        out_shape=[jax.ShapeDtypeStruct((4, g.shape[0], g.shape[1] // N_DEV, g.shape[2]), g.dtype) for g in grads],
        scratch_shapes=[pltpu.SemaphoreType.DMA((ng, 4)), pltpu.SemaphoreType.DMA((ng, 4))],
    )(*grads)


def _pair_sum(grad, recv, pos, name):
    n, r8, c = grad.shape
    r = r8 // N_DEV

    def body(pos_ref, g_ref, r_ref, o_ref):
        o_ref[...] = (g_ref[...].astype(F32) + r_ref[...].astype(F32)).astype(o_ref.dtype)

    return pl.pallas_call(
        body, name=name,
        grid_spec=pltpu.PrefetchScalarGridSpec(
            num_scalar_prefetch=1, grid=(4, n),
            in_specs=[pl.BlockSpec((None, r, c), lambda q, i, pos: (i, 2 * q + pos[2], 0)),
                      pl.BlockSpec((None, None, r, c), lambda q, i, pos: (q, i, 0, 0))],
            out_specs=pl.BlockSpec((None, None, r, c), lambda q, i, pos: (q, i, 0, 0))),
        out_shape=jax.ShapeDtypeStruct((4, n, r, c), grad.dtype), compiler_params=_params("parallel", "parallel"),
    )(pos, grad, recv)


def _exchange_chips(pairs, name):
    ng = len(pairs)

    def body(*refs):
        srcs, outs = refs[:ng], refs[ng:2 * ng]
        send_sems, recv_sems = refs[2 * ng:]
        x, y, c = _position()
        chips = [(1 - x, y), (x, 1 - y), (1 - x, 1 - y)]
        copies = []
        for g in range(ng):
            for j, (cx, cy) in enumerate(chips):
                copies.append(pltpu.make_async_remote_copy(
                    src_ref=srcs[g].at[2 * cx + cy], dst_ref=outs[g].at[j],
                    send_sem=send_sems.at[g, j], recv_sem=recv_sems.at[g, j],
                    device_id=(cx, cy, c), device_id_type=MESH))
        for cp in copies:
            cp.start()
        for cp in copies:
            cp.wait()

    hbm = pl.BlockSpec(memory_space=pl.ANY)
    return pl.pallas_call(
        body, name=name, in_specs=[hbm] * ng, out_specs=[hbm] * ng,
        out_shape=[jax.ShapeDtypeStruct((3,) + p.shape[1:], p.dtype) for p in pairs],
        scratch_shapes=[pltpu.SemaphoreType.DMA((ng, 3)), pltpu.SemaphoreType.DMA((ng, 3))],
    )(*pairs)


def _chip_sum(pair, recv, pos, name):
    _, n, r, c = pair.shape

    def body(pos_ref, p_ref, r_ref, o_ref):
        acc = p_ref[...].astype(F32)
        for j in range(3):
            acc = acc + r_ref[j].astype(F32)
        o_ref[...] = acc

    return pl.pallas_call(
        body, name=name,
        grid_spec=pltpu.PrefetchScalarGridSpec(
            num_scalar_prefetch=1, grid=(n,),
            in_specs=[pl.BlockSpec((None, None, r, c), lambda i, pos: (2 * pos[0] + pos[1], i, 0, 0)),
                      pl.BlockSpec((3, None, r, c), lambda i, pos: (0, i, 0, 0))],
            out_specs=pl.BlockSpec((None, r, c), lambda i, pos: (i, 0, 0))),
        out_shape=jax.ShapeDtypeStruct((n, r, c), F32), compiler_params=_params("parallel"),
    )(pos, pair, recv)


def _reduce_scatter(grads, pos):
    from_sibling = _exchange_sibling(grads, "rs_sibling")
    pairs = [_pair_sum(g, s, pos, f"rs_pair_sum{k}") for k, (g, s) in enumerate(zip(grads, from_sibling))]
    from_chips = _exchange_chips(pairs, "rs_chips")
    return [_chip_sum(p, s, pos, f"rs_chip_sum{k}") for k, (p, s) in enumerate(zip(pairs, from_chips))]


def _block_diag(w):
    H, d, _ = w.shape
    return (jnp.eye(H, dtype=w.dtype)[:, None, :, None] * w[:, :, None, :]).reshape(H * d, H * d)


def _diag_blocks(dense, H):
    d = dense.shape[0] // H
    return jnp.stack([dense[i * d:(i + 1) * d, i * d:(i + 1) * d] for i in range(H)])


def _local_step(x, p, tgt, W):
    L = p.shape[0]
    R = W["lru_out"].shape[1]
    saved = []
    h = x
    for i in range(L):
        j = i // 2
        s = {"h0": h}
        s["a1"], s["b1"], h = _ffn_fwd(h, W["ffn1_norm"][i], W["ffn"], 6 * i, f"ffn1_fwd_{i}")
        s["h1"] = h
        if i % 2 == 0:
            hn = _rms_fwd(h, W["mix_norm"][i], BF, f"mix_norm_{i}")
            z = _mm(hn, W["lru_in"][j], "nt", f"lru_in_{i}")
            wa, wx = _block_diag(W["lru_w_a"][j]).astype(BF), _block_diag(W["lru_w_x"][j]).astype(BF)
            xc, r, ig, a, bt = _lru_gates(z, W["lru_conv_w"][j], W["lru_conv_b"][j], wa, wx, W["lru_b_a"][j],
                                          W["lru_b_x"][j], W["lru_a_param"][j], f"lru_gates_{i}")
            hs = _lru_scan(a, bt, False, f"lru_scan_{i}")
            y = _lru_out_ew(hs, z, f"lru_out_ew_{i}")
            h = _mm(y, W["lru_out"][j], "nn", f"lru_out_{i}", res=h)
            s.update(hn=hn, z=z, wa=wa, wx=wx, xc=xc, r=r, ig=ig, a=a, hs=hs, y=y)
        else:
            hn = _rms_fwd(h, W["mix_norm"][i], F32, f"mix_norm_{i}")
            h, s["u"], s["yb"] = _pool_fwd(hn, h, W["pool_w"][j], W["pool_b"][j], W["pool_scale"][j], f"pool_fwd_{i}")
        s["h2"] = h
        s["a2"], s["b2"], h = _ffn_fwd(h, W["ffn2_norm"][i], W["ffn"], 6 * i + 3, f"ffn2_fwd_{i}")
        s["h3"] = h
        s["n4"] = _rms_fwd(h, W["ple_norm"][i], BF, f"ple_norm_{i}")
        s["zp"] = _mm(s["n4"], W["ple_gate"][i], "nn", f"ple_gate_{i}")
        s["pp"] = _mm(p[i], W["ple_proj"][i], "nt", f"ple_proj_{i}")
        h = _ple_fwd_ew(h, s["zp"], s["pp"], f"ple_fwd_ew_{i}")
        saved.append(s)

    loss, dh, d_final = _loss_head(h, W["final_norm"], tgt)

    big = {k: [None] * n for k, n in (("ffn", 6 * L), ("lru_in", L // 2 + L % 2), ("lru_out", L // 2 + L % 2),
                                      ("ple_gate", L), ("ple_proj", L), ("pool_w", L // 2))}
    small = {k: [None] * L for k in ("ffn1_norm", "mix_norm", "ffn2_norm", "ple_norm")}
    for k in ("lru_conv_w", "lru_conv_b", "lru_w_a", "lru_b_a", "lru_w_x", "lru_b_x", "lru_a_param"):
        small[k] = [None] * (L // 2 + L % 2)
    for k in ("pool_b", "pool_scale"):
        small[k] = [None] * (L // 2)
    small["final_norm"] = d_final[0]

    for i in reversed(range(L)):
        j = i // 2
        s = saved[i]
        dz, dpp = _ple_bwd_ew(dh, s["zp"], s["pp"], f"ple_bwd_ew_{i}")
        big["ple_gate"][i] = _mm(s["n4"], dz, "tn", f"ple_gate_dw_{i}", out_dtype=BF)
        big["ple_proj"][i] = _mm(dpp, p[i], "tn", f"ple_proj_dw_{i}", out_dtype=BF)
        dn4 = _mm(dz, W["ple_gate"][i], "nt", f"ple_gate_dx_{i}")
        dh, dg = _rms_bwd(s["h3"], W["ple_norm"][i], dn4, dh, f"ple_norm_bwd_{i}")
        small["ple_norm"][i] = dg[0]
        da, db, sv, dh, n, dhb, dg = _ffn_bwd_act(dh, s["h2"], W["ffn2_norm"][i], s["a2"], s["b2"], W["ffn"],
                                                  6 * i + 3, f"ffn2_bwd_{i}")
        small["ffn2_norm"][i] = dg[0]
        big["ffn"][6 * i + 3:6 * i + 6] = _ffn_bwd_w(da, db, sv, n, dhb, f"ffn2_dw_{i}")
        if i % 2 == 0:
            big["lru_out"][j] = _mm(s["y"], dh, "tn", f"lru_out_dw_{i}", out_dtype=BF)
            dy = _mm(dh, W["lru_out"][j], "nt", f"lru_out_dx_{i}")
            dhd, dgb = _lru_bwd_ew(dy, s["hs"], s["z"], f"lru_bwd_ew_{i}")
            lam = _lru_scan(s["a"], dhd, True, f"lru_scan_bwd_{i}")
            dxc, dpa, dpx, dsp, dba, dbx = _lru_gates_bwd(lam, s["hs"], s["r"], s["ig"], s["xc"], s["wa"], s["wx"],
                                                          W["lru_a_param"][j], f"lru_gates_bwd_{i}")
            small["lru_a_param"][j], small["lru_b_a"][j], small["lru_b_x"][j] = dsp[0], dba[0], dbx[0]
            small["lru_w_a"][j] = _diag_blocks(_mm(s["xc"], dpa, "tn", f"lru_wa_dw_{i}"), LRU_HEADS)
            small["lru_w_x"][j] = _diag_blocks(_mm(s["xc"], dpx, "tn", f"lru_wx_dw_{i}"), LRU_HEADS)
            dxb, dcw, dcb = _lru_conv_bwd(dxc, s["z"], W["lru_conv_w"][j], f"lru_conv_bwd_{i}")
            small["lru_conv_w"][j], small["lru_conv_b"][j] = dcw[:CONV_WIDTH], dcb[0]
            win = W["lru_in"][j]
            big["lru_in"][j] = jnp.concatenate([_mm(dgb, s["hn"], "tn", f"lru_in_dw_g_{i}", out_dtype=BF),
                                                _mm(dxb, s["hn"], "tn", f"lru_in_dw_x_{i}", out_dtype=BF)], axis=0)
            dhn = _mm(dgb, win[:R], "nn", f"lru_in_dx_g_{i}")
            dhn = _mm(dxb, win[R:], "nn", f"lru_in_dx_x_{i}", res=dhn)
        else:
            du, v, dw, dbp, dsc = _pool_bwd(dh, s["u"], s["yb"], W["pool_w"][j], W["pool_scale"][j], f"pool_bwd_{i}")
            big["pool_w"][j] = dw.astype(BF)
            small["pool_b"][j], small["pool_scale"][j] = dbp[0], dsc[0]
            dhn = _pool_bwd_win(v, du, f"pool_bwd_win_{i}")
        dh, dg = _rms_bwd(s["h1"], W["mix_norm"][i], dhn, dh, f"mix_norm_bwd_{i}")
        small["mix_norm"][i] = dg[0]
        da, db, sv, dh, n, dhb, dg = _ffn_bwd_act(dh, s["h0"], W["ffn1_norm"][i], s["a1"], s["b1"], W["ffn"],
                                                  6 * i, f"ffn1_bwd_{i}")
        small["ffn1_norm"][i] = dg[0]
        big["ffn"][6 * i:6 * i + 3] = _ffn_bwd_w(da, db, sv, n, dhb, f"ffn1_dw_{i}")

    big = {k: jnp.stack(v) for k, v in big.items()}
    G = len(POOL_WINDOWS)
    big["pool_w"] = big["pool_w"].reshape((L // 2) * G, *big["pool_w"].shape[2:])
    small = {k: (v if k == "final_norm" else jnp.stack(v)) for k, v in small.items()}
    return loss, dh, big, small


BIG = ("ffn", "lru_in", "lru_out", "ple_gate", "ple_proj", "pool_w")
SMALL_SHARDED = ("pool_b", "pool_scale", "lru_conv_w")
SMALL = ("ffn1_norm", "mix_norm", "ffn2_norm", "ple_norm", "final_norm", "lru_conv_b", "lru_w_a", "lru_b_a",
         "lru_w_x", "lru_b_x", "lru_a_param", "pool_b", "pool_scale", "lru_conv_w")


def _pack_big(w):
    t = lambda a: jnp.swapaxes(a, -1, -2)
    ffn = jnp.stack([jnp.stack([t(w[f"ffn{f}_w_gate"]), t(w[f"ffn{f}_w_up"]), w[f"ffn{f}_w_down"]], axis=1)
                     for f in (1, 2)], axis=1)
    pw = w["pool_w"]
    return {"ffn": ffn.reshape(-1, *ffn.shape[3:]), "lru_in": t(w["lru_w_in"]), "lru_out": w["lru_w_out"],
            "ple_gate": w["ple_w_gate"], "ple_proj": t(w["ple_w_proj"]), "pool_w": pw.reshape(-1, *pw.shape[2:])}


def _unpack_big(b, L):
    t = lambda a: jnp.swapaxes(a, -1, -2)
    ffn = b["ffn"].reshape(L, 2, 3, *b["ffn"].shape[1:])
    out = {"lru_w_in": t(b["lru_in"]), "lru_w_out": b["lru_out"], "ple_w_gate": b["ple_gate"],
           "ple_w_proj": t(b["ple_proj"]), "pool_w": b["pool_w"].reshape(L // 2, -1, *b["pool_w"].shape[1:])}
    for f in (1, 2):
        out[f"ffn{f}_w_gate"], out[f"ffn{f}_w_up"] = t(ffn[:, f - 1, 0]), t(ffn[:, f - 1, 1])
        out[f"ffn{f}_w_down"] = ffn[:, f - 1, 2]
    return out


def _flatten(parts, names, rows_of=LANES):
    flat = jnp.concatenate([parts[k].reshape(-1) for k in names])
    pad = (-flat.size) % (16 * rows_of)
    return jnp.pad(flat, (0, pad)).reshape(1, -1, rows_of)


def _unflatten(flat, like, names):
    out, o = {}, 0
    flat = flat.reshape(-1)
    for k in names:
        n = like[k].size
        out[k] = flat[o:o + n].reshape(like[k].shape)
        o += n
    return out


def kernel(x, p, ffn1_norm, ffn1_w_gate, ffn1_w_up, ffn1_w_down, mix_norm, lru_w_in, lru_conv_w, lru_conv_b, lru_w_a, lru_b_a, lru_w_x, lru_b_x, lru_a_param, lru_w_out, pool_w, pool_b, pool_scale, ffn2_norm, ffn2_w_gate, ffn2_w_up, ffn2_w_down, ple_norm, ple_w_gate, ple_w_proj, final_norm, loss_target, m_ffn1_norm, m_ffn1_w_gate, m_ffn1_w_up, m_ffn1_w_down, m_mix_norm, m_lru_w_in, m_lru_conv_w, m_lru_conv_b, m_lru_w_a, m_lru_b_a, m_lru_w_x, m_lru_b_x, m_lru_a_param, m_lru_w_out, m_pool_w, m_pool_b, m_pool_scale, m_ffn2_norm, m_ffn2_w_gate, m_ffn2_w_up, m_ffn2_w_down, m_ple_norm, m_ple_w_gate, m_ple_w_proj, m_final_norm, v_ffn1_norm, v_ffn1_w_gate, v_ffn1_w_up, v_ffn1_w_down, v_mix_norm, v_lru_w_in, v_lru_conv_w, v_lru_conv_b, v_lru_w_a, v_lru_b_a, v_lru_w_x, v_lru_b_x, v_lru_a_param, v_lru_w_out, v_pool_w, v_pool_b, v_pool_scale, v_ffn2_norm, v_ffn2_w_gate, v_ffn2_w_up, v_ffn2_w_down, v_ple_norm, v_ple_w_gate, v_ple_w_proj, v_final_norm):
    names = ["ffn1_norm", "ffn1_w_gate", "ffn1_w_up", "ffn1_w_down", "mix_norm", "lru_w_in", "lru_conv_w", "lru_conv_b",
             "lru_w_a", "lru_b_a", "lru_w_x", "lru_b_x", "lru_a_param", "lru_w_out", "pool_w", "pool_b", "pool_scale",
             "ffn2_norm", "ffn2_w_gate", "ffn2_w_up", "ffn2_w_down", "ple_norm", "ple_w_gate", "ple_w_proj", "final_norm"]
    w = dict(zip(names, [ffn1_norm, ffn1_w_gate, ffn1_w_up, ffn1_w_down, mix_norm, lru_w_in, lru_conv_w, lru_conv_b, lru_w_a, lru_b_a, lru_w_x, lru_b_x, lru_a_param, lru_w_out, pool_w, pool_b, pool_scale, ffn2_norm, ffn2_w_gate, ffn2_w_up, ffn2_w_down, ple_norm, ple_w_gate, ple_w_proj, final_norm]))
    m = dict(zip(names, [m_ffn1_norm, m_ffn1_w_gate, m_ffn1_w_up, m_ffn1_w_down, m_mix_norm, m_lru_w_in, m_lru_conv_w, m_lru_conv_b, m_lru_w_a, m_lru_b_a, m_lru_w_x, m_lru_b_x, m_lru_a_param, m_lru_w_out, m_pool_w, m_pool_b, m_pool_scale, m_ffn2_norm, m_ffn2_w_gate, m_ffn2_w_up, m_ffn2_w_down, m_ple_norm, m_ple_w_gate, m_ple_w_proj, m_final_norm]))
    v = dict(zip(names, [v_ffn1_norm, v_ffn1_w_gate, v_ffn1_w_up, v_ffn1_w_down, v_mix_norm, v_lru_w_in, v_lru_conv_w, v_lru_conv_b, v_lru_w_a, v_lru_b_a, v_lru_w_x, v_lru_b_x, v_lru_a_param, v_lru_w_out, v_pool_w, v_pool_b, v_pool_scale, v_ffn2_norm, v_ffn2_w_gate, v_ffn2_w_up, v_ffn2_w_down, v_ple_norm, v_ple_w_gate, v_ple_w_proj, v_final_norm]))
    L = p.shape[0]
    px, py, pc = _position()
    pos = jnp.stack([px, py, pc]).astype(jnp.int32)
    me = 4 * px + 2 * py + pc

    packed = _pack_big(w)
    blocks = [packed[k].astype(BF) for k in BIG] + [_flatten(w, SMALL_SHARDED)]
    gathered = _all_gather(blocks, "gather_weights")
    W = dict(zip(BIG, gathered[:-1]))
    G = len(POOL_WINDOWS)
    W["pool_w"] = W["pool_w"].reshape(L // 2, G, *W["pool_w"].shape[1:])
    per_dev = gathered[-1].reshape(N_DEV, -1)
    shards = [_unflatten(per_dev[k], w, SMALL_SHARDED) for k in range(N_DEV)]
    W["pool_b"] = jnp.concatenate([s["pool_b"] for s in shards], axis=-1)
    W["pool_scale"] = jnp.concatenate([s["pool_scale"] for s in shards], axis=-1)
    W["lru_conv_w"] = jnp.concatenate([s["lru_conv_w"] for s in shards], axis=-1)
    for k in SMALL:
        if k not in SMALL_SHARDED:
            W[k] = w[k]

    loss, dx, big, small = _local_step(x[0], p[:, 0], loss_target[0], W)

    mine = dict(zip(BIG, _reduce_scatter([big[k] for k in BIG], pos)))
    grads = _unpack_big(mine, L)
    parts = _all_gather([_flatten(small, SMALL)], "gather_small_grads")[0]
    total = _sum_devices(parts.reshape(N_DEV, -1, LANES), "sum_small_grads")
    full = _unflatten(total, {k: (W[k] if k in SMALL_SHARDED else w[k]) for k in SMALL}, SMALL)
    for k in SMALL:
        if k in SMALL_SHARDED:
            n = w[k].shape[-1]
            grads[k] = lax.dynamic_slice_in_dim(full[k], me * n, n, axis=-1)
        else:
            grads[k] = full[k]

    delta, new_m, new_v = {}, {}, {}
    for k in names:
        delta[k], new_m[k], new_v[k] = _adamw(w[k], grads[k], m[k], v[k], f"adamw_{k}")
    total_loss = lax.psum(loss[0, 0], ("x", "y", "c"))
    return (total_loss, dx[None], *[grads[k] for k in names], *[delta[k] for k in names],
            *[new_m[k] for k in names], *[new_v[k] for k in names])
```

```python
import functools

import jax
import jax.numpy as jnp
from jax import lax
from jax.experimental import pallas as pl
from jax.experimental.pallas import tpu as pltpu

F32 = jnp.float32
BF = jnp.bfloat16
MESH = pl.DeviceIdType.MESH

RMS_EPS = 1e-6
LRU_C = 8.0
LRU_HEADS = 16
CONV_WIDTH = 4
POOL_WINDOWS = (2, 4, 8, 16)
ADAM_LR, ADAM_B1, ADAM_B2, ADAM_EPS, ADAM_WD, ADAM_STEP = 0.001, 0.9, 0.999, 1e-08, 0.01, 10

N_DEV = 8
LANES = 128
SUBLANES = 8
HALO = 16
VMEM_LIMIT = 56 * 1024 * 1024

TM_FFN_FWD = 2048
TM_FFN_BWD = 1024
TF_FFN = 256
TF_FFN_WG = 1408
TK_FFN_WG = 512
TB_SEQ = 256
TB_SCAN = 512
TC_SCAN = 256
TM_EW = 512
TM_MM, TN_MM, TK_MM = 1024, 512, 1024
TR_ADAM = 512


def _tile(n, pref, align):
    if n <= pref:
        return n
    t = (pref // align) * align
    while t >= align:
        if n % t == 0:
            return t
        t -= align
    raise ValueError(f"no tile for {n} (pref {pref}, align {align})")


def _params(*sem):
    return pltpu.CompilerParams(dimension_semantics=sem, vmem_limit_bytes=VMEM_LIMIT)


def _dot(a, b):
    return lax.dot_general(a, b, (((1,), (0,)), ((), ())), preferred_element_type=F32)


def _dot_nt(a, b):
    return lax.dot_general(a, b, (((1,), (1,)), ((), ())), preferred_element_type=F32)


def _dot_tn(a, b):
    return lax.dot_general(a, b, (((0,), (0,)), ((), ())), preferred_element_type=F32)


def _sigmoid(x):
    return 1.0 / (1.0 + jnp.exp(-x))


def _gelu_parts(x):
    k0, k1 = 0.7978845608028654, 0.044715
    t = jnp.tanh(k0 * (x + k1 * x * x * x))
    g = 0.5 * x * (1.0 + t)
    dg = 0.5 * (1.0 + t) + 0.5 * x * (1.0 - t * t) * k0 * (1.0 + 3.0 * k1 * x * x)
    return g, dg


def _neg_expm1(x):
    p = x * (1.0 + x * (0.5 + x * (1.0 / 6 + x * (1.0 / 24 + x * (1.0 / 120 + x * (1.0 / 720 + x * (1.0 / 5040)))))))
    return jnp.where(x > -0.35, -p, 1.0 - jnp.exp(x))


def _softplus_neg(l):
    u = jnp.exp(-jnp.abs(l))
    w = 1.0 + u
    log1p = jnp.where(w == 1.0, u, jnp.log(w) * (u / jnp.where(w == 1.0, 1.0, w - 1.0)))
    return jnp.maximum(-l, 0.0) + log1p


def _rms_parts(x, g):
    r = lax.rsqrt(jnp.mean(x * x, axis=-1, keepdims=True) + RMS_EPS)
    nhat = x * r
    return nhat * g, nhat, r


def _rms_bwd_parts(x, g, dn):
    _, nhat, r = _rms_parts(x, g)
    u = dn * g
    dx = r * (u - nhat * jnp.mean(u * nhat, axis=-1, keepdims=True))
    return dx, jnp.sum(dn * nhat, axis=0, keepdims=True)


def _row_spec(tm, d, single=False):
    if single:
        return pl.BlockSpec((tm, d), lambda i, *_: (i, 0), pipeline_mode=pl.Buffered(1))
    return pl.BlockSpec((tm, d), lambda i, *_: (i, 0))


def _vec_spec(d, rows=1):
    return pl.BlockSpec((rows, d), lambda *_: (0, 0))


def _mm(x, w, mode, name, out_dtype=F32, res=None, alpha=1.0, tm=None, tn=None, tk=None):
    if mode == "nn":
        (M, K), (_, N) = x.shape, w.shape
    elif mode == "nt":
        (M, K), (N, _) = x.shape, w.shape
    else:
        (K, M), (_, N) = x.shape, w.shape
    tm = _tile(M, tm or TM_MM, LANES if mode == "tn" else SUBLANES)
    tn = _tile(N, tn or TN_MM, LANES)
    tk = _tile(K, tk or TK_MM, LANES if mode != "tn" else 16)
    nk = K // tk
    dot = {"nn": _dot, "nt": _dot_nt, "tn": _dot_tn}[mode]

    def body(*refs):
        if res is None:
            x_ref, w_ref, o_ref, acc = refs
        else:
            x_ref, w_ref, r_ref, o_ref, acc = refs
        k = pl.program_id(2)

        @pl.when(k == 0)
        def _():
            acc[...] = jnp.zeros_like(acc)

        acc[...] += dot(x_ref[...].astype(BF), w_ref[...].astype(BF))

        @pl.when(k == nk - 1)
        def _():
            r = acc[...] if alpha == 1.0 else acc[...] * alpha
            if res is not None:
                r = r_ref[...] + r
            o_ref[...] = r.astype(out_dtype)

    if mode == "nn":
        specs = [pl.BlockSpec((tm, tk), lambda i, j, k: (i, k)), pl.BlockSpec((tk, tn), lambda i, j, k: (k, j))]
    elif mode == "nt":
        specs = [pl.BlockSpec((tm, tk), lambda i, j, k: (i, k)), pl.BlockSpec((tn, tk), lambda i, j, k: (j, k))]
    else:
        specs = [pl.BlockSpec((tk, tm), lambda i, j, k: (k, i)), pl.BlockSpec((tk, tn), lambda i, j, k: (k, j))]
    args = [x, w]
    if res is not None:
        specs.append(pl.BlockSpec((tm, tn), lambda i, j, k: (i, j)))
        args.append(res)
    return pl.pallas_call(
        body, name=name, grid=(M // tm, N // tn, nk), in_specs=specs,
        out_specs=pl.BlockSpec((tm, tn), lambda i, j, k: (i, j)),
        out_shape=jax.ShapeDtypeStruct((M, N), out_dtype),
        scratch_shapes=[pltpu.VMEM((tm, tn), F32)],
        compiler_params=_params("parallel", "parallel", "arbitrary"),
    )(*args)


def _rms_fwd(h, g, out_dtype, name):
    T, D = h.shape
    tm = _tile(T, TM_EW, 16)

    def body(h_ref, g_ref, o_ref):
        o_ref[...] = _rms_parts(h_ref[...], g_ref[...])[0].astype(out_dtype)

    return pl.pallas_call(
        body, name=name, grid=(T // tm,), in_specs=[_row_spec(tm, D), _vec_spec(D)], out_specs=_row_spec(tm, D),
        out_shape=jax.ShapeDtypeStruct((T, D), out_dtype), compiler_params=_params("parallel"),
    )(h, g.reshape(1, D))


def _rms_bwd(h, g, dn, dres, name):
    T, D = h.shape
    tm = _tile(T, TM_EW, 16)

    def body(h_ref, g_ref, dn_ref, dr_ref, dh_ref, dg_ref):
        @pl.when(pl.program_id(0) == 0)
        def _():
            dg_ref[...] = jnp.zeros_like(dg_ref)

        dx, dg = _rms_bwd_parts(h_ref[...], g_ref[...], dn_ref[...].astype(F32))
        dh_ref[...] = dr_ref[...] + dx
        dg_ref[...] += dg

    return pl.pallas_call(
        body, name=name, grid=(T // tm,),
        in_specs=[_row_spec(tm, D), _vec_spec(D), _row_spec(tm, D), _row_spec(tm, D)],
        out_specs=[_row_spec(tm, D), _vec_spec(D)],
        out_shape=[jax.ShapeDtypeStruct((T, D), F32), jax.ShapeDtypeStruct((1, D), F32)],
        compiler_params=_params("arbitrary"),
    )(h, g.reshape(1, D), dn, dres)


def _loss_head(h, g, tgt):
    T, D = h.shape
    tm = _tile(T, TM_EW, 16)

    def body(h_ref, g_ref, t_ref, loss_ref, dh_ref, dg_ref):
        @pl.when(pl.program_id(0) == 0)
        def _():
            dg_ref[...] = jnp.zeros_like(dg_ref)
            loss_ref[...] = jnp.zeros_like(loss_ref)

        x, gg = h_ref[...], g_ref[...]
        y = _rms_parts(x, gg)[0]
        e = y - t_ref[...]
        part = jnp.sum(jnp.sum(e * e, axis=0, keepdims=True), axis=1, keepdims=True) * (0.5 / D)
        loss_ref[...] += jnp.broadcast_to(part, loss_ref.shape)
        dx, dg = _rms_bwd_parts(x, gg, e * (1.0 / D))
        dh_ref[...] = dx
        dg_ref[...] += dg

    return pl.pallas_call(
        body, name="loss_head", grid=(T // tm,),
        in_specs=[_row_spec(tm, D), _vec_spec(D), _row_spec(tm, D)],
        out_specs=[_vec_spec(LANES), _row_spec(tm, D), _vec_spec(D)],
        out_shape=[jax.ShapeDtypeStruct((1, LANES), F32), jax.ShapeDtypeStruct((T, D), F32),
                   jax.ShapeDtypeStruct((1, D), F32)],
        compiler_params=_params("arbitrary"),
    )(h, g.reshape(1, D), tgt)


def _ffn_w_specs(tf, D, base, imap):
    return [pl.BlockSpec((None, tf, D), functools.partial(imap, base + k)) for k in range(3)]


def _ffn_fwd(h, g, wffn, base, name, gather=()):
    T, D = h.shape
    F = wffn.shape[1]
    tm, tf = _tile(T, TM_FFN_FWD, 16), _tile(F, TF_FFN, LANES)
    ni, nf, ng = T // tm, F // tf, len(gather)

    def body(*refs):
        h_ref, g_ref, wg_ref, wu_ref, wd_ref = refs[:5]
        srcs, (a_ref, b_ref, o_ref), outs = refs[5:5 + ng], refs[5 + ng:8 + ng], refs[8 + ng:8 + 2 * ng]
        n_sc, acc_sc = refs[8 + 2 * ng:10 + 2 * ng]
        i, j = pl.program_id(0), pl.program_id(1)
        if ng:
            start, forward, finish = _gather_plan(gather, srcs, outs, *refs[10 + 2 * ng:])
            pl.when(jnp.logical_and(i == 0, j == 0))(start)
            pl.when(jnp.logical_and(i == ni // 2, j == 0))(forward)

        @pl.when(j == 0)
        def _():
            n_sc[...] = _rms_parts(h_ref[...], g_ref[...])[0].astype(BF)
            acc_sc[...] = jnp.zeros_like(acc_sc)

        n = n_sc[...]
        a = _dot_nt(n, wg_ref[...])
        b = _dot_nt(n, wu_ref[...])
        s = a * _sigmoid(a) * b
        acc_sc[...] += _dot(s.astype(BF), wd_ref[...])
        a_ref[...] = a.astype(BF)
        b_ref[...] = b.astype(BF)

        @pl.when(j == nf - 1)
        def _():
            o_ref[...] = h_ref[...] + 0.5 * acc_sc[...]

        if ng:
            pl.when(jnp.logical_and(i == ni - 1, j == nf - 1))(finish)

    tile = pl.BlockSpec((tm, tf), lambda i, j: (i, j))
    hbm = pl.BlockSpec(memory_space=pl.ANY)
    outs = pl.pallas_call(
        body, name=name, grid=(ni, nf),
        in_specs=[_row_spec(tm, D, True), _vec_spec(D)] + _ffn_w_specs(tf, D, base, lambda b, i, j: (b, j, 0))
        + [hbm] * ng,
        out_specs=[tile, tile, _row_spec(tm, D, True)] + [hbm] * ng,
        out_shape=[jax.ShapeDtypeStruct((T, F), BF), jax.ShapeDtypeStruct((T, F), BF),
                   jax.ShapeDtypeStruct((T, D), F32)] + _gathered_shapes(gather),
        scratch_shapes=[pltpu.VMEM((tm, D), BF), pltpu.VMEM((tm, D), F32)] + _gather_sems(ng),
        compiler_params=_params("arbitrary", "arbitrary"),
    )(h, g.reshape(1, D), wffn, wffn, wffn, *gather)
    return outs[0], outs[1], outs[2], list(outs[3:])


def _ffn_bwd_act(dh, h, g, a, b, wffn, base, name, scatter=()):
    T, D = h.shape
    F = wffn.shape[1]
    tm, tf = _tile(T, TM_FFN_BWD, 16), _tile(F, TF_FFN, LANES)
    ni, nf, ng = T // tm, F // tf, len(scatter)

    def body(*refs):
        dh_ref, h_ref, g_ref, a_ref, b_ref, wg_ref, wu_ref, wd_ref = refs[:8]
        srcs = refs[8:8 + ng]
        da_ref, db_ref, s_ref, dhi_ref, n_ref, dhb_ref, dg_ref = refs[8 + ng:15 + ng]
        outs = refs[15 + ng:15 + 2 * ng]
        dn_sc, dhb_sc = refs[15 + 2 * ng:17 + 2 * ng]
        i, j = pl.program_id(0), pl.program_id(1)
        if ng:
            start, finish = _scatter_plan(scatter, srcs, outs, *refs[17 + 2 * ng:])
            pl.when(jnp.logical_and(i == 0, j == 0))(start)

        @pl.when(jnp.logical_and(i == 0, j == 0))
        def _():
            dg_ref[...] = jnp.zeros_like(dg_ref)

        @pl.when(j == 0)
        def _():
            dhb_sc[...] = dh_ref[...].astype(BF)
            dn_sc[...] = jnp.zeros_like(dn_sc)

        ds = 0.5 * _dot_nt(dhb_sc[...], wd_ref[...])
        av, bv = a_ref[...].astype(F32), b_ref[...].astype(F32)
        sig = _sigmoid(av)
        silu = av * sig
        da = (ds * bv * (sig * (1.0 + av * (1.0 - sig)))).astype(BF)
        db = (ds * silu).astype(BF)
        dn_sc[...] += _dot(da, wg_ref[...]) + _dot(db, wu_ref[...])
        da_ref[...] = da
        db_ref[...] = db
        s_ref[...] = (silu * bv).astype(BF)

        @pl.when(j == nf - 1)
        def _():
            x, gg = h_ref[...], g_ref[...]
            dx, dg = _rms_bwd_parts(x, gg, dn_sc[...])
            dhi_ref[...] = dh_ref[...] + dx
            dg_ref[...] += dg
            n_ref[...] = _rms_parts(x, gg)[0].astype(BF)
            dhb_ref[...] = dhb_sc[...]

        if ng:
            pl.when(jnp.logical_and(i == ni - 1, j == nf - 1))(finish)

    tile = pl.BlockSpec((tm, tf), lambda i, j: (i, j))
    hbm = pl.BlockSpec(memory_space=pl.ANY)
    outs = pl.pallas_call(
        body, name=name, grid=(ni, nf),
        in_specs=[_row_spec(tm, D, True), _row_spec(tm, D, True), _vec_spec(D), tile, tile]
        + _ffn_w_specs(tf, D, base, lambda b, i, j: (b, j, 0)) + [hbm] * ng,
        out_specs=[tile, tile, tile, _row_spec(tm, D, True), _row_spec(tm, D, True), _row_spec(tm, D, True),
                   _vec_spec(D)] + [hbm] * ng,
        out_shape=[jax.ShapeDtypeStruct((T, F), BF)] * 3
        + [jax.ShapeDtypeStruct((T, D), F32), jax.ShapeDtypeStruct((T, D), BF), jax.ShapeDtypeStruct((T, D), BF),
           jax.ShapeDtypeStruct((1, D), F32)] + _scattered_shapes(scatter),
        scratch_shapes=[pltpu.VMEM((tm, D), F32), pltpu.VMEM((tm, D), BF)] + _scatter_sems(ng),
        compiler_params=_params("arbitrary", "arbitrary"),
    )(dh, h, g.reshape(1, D), a, b, wffn, wffn, wffn, *scatter)
    return tuple(outs[:7]) + (list(outs[7:]),)


def _ffn_bwd_w(da, db, s, n, dhb, name, scatter=()):
    T, F = da.shape
    D = n.shape[1]
    tf, tk = _tile(F, TF_FFN_WG, LANES), _tile(T, TK_FFN_WG, 16)
    nj, nk, ng = F // tf, T // tk, len(scatter)

    def body(*refs):
        da_ref, db_ref, s_ref, n_ref, dh_ref = refs[:5]
        srcs, o_ref, outs = refs[5:5 + ng], refs[5 + ng], refs[6 + ng:6 + 2 * ng]
        g_sc, u_sc, d_sc = refs[6 + 2 * ng:9 + 2 * ng]
        j, k = pl.program_id(0), pl.program_id(1)
        if ng:
            start, finish = _scatter_plan(scatter, srcs, outs, *refs[9 + 2 * ng:])
            pl.when(jnp.logical_and(j == 0, k == 0))(start)

        @pl.when(k == 0)
        def _():
            g_sc[...] = jnp.zeros_like(g_sc)
            u_sc[...] = jnp.zeros_like(u_sc)
            d_sc[...] = jnp.zeros_like(d_sc)

        nv = n_ref[...]
        g_sc[...] += _dot_tn(da_ref[...], nv)
        u_sc[...] += _dot_tn(db_ref[...], nv)
        d_sc[...] += _dot_tn(s_ref[...], dh_ref[...])

        @pl.when(k == nk - 1)
        def _():
            o_ref[0] = g_sc[...].astype(BF)
            o_ref[1] = u_sc[...].astype(BF)
            o_ref[2] = (0.5 * d_sc[...]).astype(BF)

        if ng:
            pl.when(jnp.logical_and(j == nj - 1, k == nk - 1))(finish)

    act = pl.BlockSpec((tk, tf), lambda j, k: (k, j))
    tok = pl.BlockSpec((tk, D), lambda j, k: (k, 0))
    hbm = pl.BlockSpec(memory_space=pl.ANY)
    outs = pl.pallas_call(
        body, name=name, grid=(nj, nk), in_specs=[act, act, act, tok, tok] + [hbm] * ng,
        out_specs=[pl.BlockSpec((3, tf, D), lambda j, k: (0, j, 0), pipeline_mode=pl.Buffered(1))] + [hbm] * ng,
        out_shape=[jax.ShapeDtypeStruct((3, F, D), BF)] + _scattered_shapes(scatter),
        scratch_shapes=[pltpu.VMEM((tf, D), F32)] * 3 + _scatter_sems(ng),
        compiler_params=_params("arbitrary", "arbitrary"),
    )(da, db, s, n, dhb, *scatter)
    return outs[0], list(outs[1:])


def _ple_fwd_ew(h, z, pp, name):
    T, D = h.shape
    tm = _tile(T, TM_EW, 16)

    def body(h_ref, z_ref, p_ref, o_ref):
        o_ref[...] = h_ref[...] + _sigmoid(z_ref[...]) * p_ref[...]

    return pl.pallas_call(
        body, name=name, grid=(T // tm,), in_specs=[_row_spec(tm, D)] * 3, out_specs=_row_spec(tm, D),
        out_shape=jax.ShapeDtypeStruct((T, D), F32), compiler_params=_params("parallel"),
    )(h, z, pp)


def _ple_bwd_ew(dh, z, pp, name):
    T, D = dh.shape
    tm = _tile(T, TM_EW, 16)

    def body(dh_ref, z_ref, p_ref, dz_ref, dp_ref):
        gate = _sigmoid(z_ref[...])
        d = dh_ref[...]
        dz_ref[...] = (d * p_ref[...] * gate * (1.0 - gate)).astype(BF)
        dp_ref[...] = (d * gate).astype(BF)

    return pl.pallas_call(
        body, name=name, grid=(T // tm,), in_specs=[_row_spec(tm, D)] * 3, out_specs=[_row_spec(tm, D)] * 2,
        out_shape=[jax.ShapeDtypeStruct((T, D), BF)] * 2, compiler_params=_params("parallel"),
    )(dh, z, pp)


def _lru_gates(z, conv_w, conv_b, wa, wx, b_a, b_x, a_param, name):
    T, R2 = z.shape
    R = R2 // 2
    tb = _tile(T, TB_SEQ, HALO)
    per = tb // HALO

    def body(x_ref, halo_ref, cw_ref, cb_ref, wa_ref, wx_ref, ba_ref, bx_ref, ap_ref,
             xc_ref, r_ref, ig_ref, a_ref, bt_ref, ext):
        i = pl.program_id(0)
        ext[pl.ds(0, HALO), :] = jnp.where(i > 0, halo_ref[...], 0.0)
        ext[pl.ds(HALO, tb), :] = x_ref[...]
        xc = cb_ref[...] + cw_ref[0:1, :] * ext[pl.ds(HALO - 3, tb), :]
        for k in range(1, CONV_WIDTH):
            xc = xc + cw_ref[k:k + 1, :] * ext[pl.ds(HALO - 3 + k, tb), :]
        xcb = xc.astype(BF)
        r = _sigmoid(_dot(xcb, wa_ref[...]) + ba_ref[...])
        ig = _sigmoid(_dot(xcb, wx_ref[...]) + bx_ref[...])
        la = -LRU_C * r * _softplus_neg(ap_ref[...])
        xc_ref[...] = xc
        r_ref[...] = r
        ig_ref[...] = ig
        a_ref[...] = jnp.exp(la)
        bt_ref[...] = jnp.sqrt(_neg_expm1(2.0 * la)) * (ig * xc)

    tile = pl.BlockSpec((tb, R), lambda i: (i, 1))
    halo = pl.BlockSpec((HALO, R), lambda i: (jnp.maximum(i * per - 1, 0), 1))
    out = pl.BlockSpec((tb, R), lambda i: (i, 0))
    return pl.pallas_call(
        body, name=name, grid=(T // tb,),
        in_specs=[tile, halo, _vec_spec(R, CONV_WIDTH), _vec_spec(R), _vec_spec(R, R), _vec_spec(R, R),
                  _vec_spec(R), _vec_spec(R), _vec_spec(R)],
        out_specs=[out] * 5, out_shape=[jax.ShapeDtypeStruct((T, R), F32)] * 5,
        scratch_shapes=[pltpu.VMEM((HALO + tb, R), F32)],
        compiler_params=_params("parallel"),
    )(z, z, conv_w, conv_b.reshape(1, R), wa, wx, b_a.reshape(1, R), b_x.reshape(1, R), a_param.reshape(1, R))


def _lru_scan(a, b, reverse, name):
    T, R = a.shape
    tb, tc = _tile(T, TB_SCAN, SUBLANES), _tile(R, TC_SCAN, LANES)
    nt, ng = T // tb, tb // SUBLANES

    def body(a_ref, b_ref, o_ref, carry, a_sc, b_sc):
        @pl.when(pl.program_id(1) == 0)
        def _():
            carry[...] = jnp.zeros_like(carry)

        A = a_ref[...]
        B = A * b_ref[...] if reverse else b_ref[...]
        sub = lax.broadcasted_iota(jnp.int32, (tb, tc), 0) & (SUBLANES - 1)
        for k in (1, 2, 4):
            m = (sub < SUBLANES - k) if reverse else (sub >= k)
            shift = tb - k if reverse else k
            a_n = jnp.where(m, pltpu.roll(A, shift, 0), 1.0)
            b_n = jnp.where(m, pltpu.roll(B, shift, 0), 0.0)
            B = A * b_n + B
            A = A * a_n
        a_sc[...] = A
        b_sc[...] = B
        sub8 = lax.broadcasted_iota(jnp.int32, (SUBLANES, tc), 0)

        def group(q, c):
            g = (ng - 1 - q) if reverse else q
            rows = pl.ds(pl.multiple_of(g * SUBLANES, SUBLANES), SUBLANES)
            hg = a_sc[rows, :] * c + b_sc[rows, :]
            if reverse:
                nxt = jnp.where(sub8 == SUBLANES - 1, c, pltpu.roll(hg, SUBLANES - 1, 0))
                o_ref[rows, :] = b_ref[rows, :] + nxt
                return hg[0:1, :]
            o_ref[rows, :] = hg
            return hg[SUBLANES - 1:SUBLANES, :]

        carry[...] = lax.fori_loop(0, ng, group, carry[...])

    spec = pl.BlockSpec((tb, tc), (lambda c, t: (nt - 1 - t, c)) if reverse else (lambda c, t: (t, c)))
    return pl.pallas_call(
        body, name=name, grid=(R // tc, nt), in_specs=[spec, spec], out_specs=spec,
        out_shape=jax.ShapeDtypeStruct((T, R), F32),
        scratch_shapes=[pltpu.VMEM((1, tc), F32), pltpu.VMEM((tb, tc), F32), pltpu.VMEM((tb, tc), F32)],
        compiler_params=_params("parallel", "arbitrary"),
    )(a, b)


def _lru_out_ew(hs, z, name):
    T, R = hs.shape
    tm = _tile(T, TM_EW, 16)

    def body(h_ref, g_ref, y_ref):
        y_ref[...] = (h_ref[...] * _gelu_parts(g_ref[...])[0]).astype(BF)

    return pl.pallas_call(
        body, name=name, grid=(T // tm,), in_specs=[_row_spec(tm, R), _row_spec(tm, R)], out_specs=_row_spec(tm, R),
        out_shape=jax.ShapeDtypeStruct((T, R), BF), compiler_params=_params("parallel"),
    )(hs, z)


def _lru_bwd_ew(dy, hs, z, name):
    T, R = hs.shape
    tm = _tile(T, TM_EW, 16)

    def body(dy_ref, h_ref, g_ref, dhd_ref, dgb_ref):
        g, dg = _gelu_parts(g_ref[...])
        d = dy_ref[...]
        dhd_ref[...] = d * g
        dgb_ref[...] = (d * h_ref[...] * dg).astype(BF)

    return pl.pallas_call(
        body, name=name, grid=(T // tm,), in_specs=[_row_spec(tm, R)] * 3, out_specs=[_row_spec(tm, R)] * 2,
        out_shape=[jax.ShapeDtypeStruct((T, R), F32), jax.ShapeDtypeStruct((T, R), BF)],
        compiler_params=_params("parallel"),
    )(dy, hs, z)


def _lru_gates_bwd(lam, hs, r, ig, xc, wa, wx, a_param, name):
    T, R = lam.shape
    tb = _tile(T, TB_SEQ, HALO)
    per = tb // HALO
    nt = T // tb

    def body(l_ref, h_ref, hh_ref, r_ref, ig_ref, xc_ref, wa_ref, wx_ref, ap_ref,
             dxc_ref, dpa_ref, dpx_ref, dsp_ref, dba_ref, dbx_ref, ext):
        i = pl.program_id(0)

        @pl.when(i == 0)
        def _():
            dsp_ref[...] = jnp.zeros_like(dsp_ref)
            dba_ref[...] = jnp.zeros_like(dba_ref)
            dbx_ref[...] = jnp.zeros_like(dbx_ref)

        ext[pl.ds(0, HALO), :] = jnp.where(i > 0, hh_ref[...], 0.0)
        ext[pl.ds(HALO, tb), :] = h_ref[...]
        h_prev = ext[pl.ds(HALO - 1, tb), :]
        lam_v, rv, igv, xcv = l_ref[...], r_ref[...], ig_ref[...], xc_ref[...]
        sp = _softplus_neg(ap_ref[...])
        la = -LRU_C * rv * sp
        av = jnp.exp(la)
        mult = jnp.sqrt(_neg_expm1(2.0 * la))
        dla = lam_v * h_prev * av - lam_v * (igv * xcv) * (av * av) / mult
        du = lam_v * mult
        dpa = (dla * (-LRU_C) * sp) * rv * (1.0 - rv)
        dpx = (du * xcv) * igv * (1.0 - igv)
        dsp_ref[...] += jnp.sum(dla * (-LRU_C) * rv, axis=0, keepdims=True)
        dba_ref[...] += jnp.sum(dpa, axis=0, keepdims=True)
        dbx_ref[...] += jnp.sum(dpx, axis=0, keepdims=True)
        dpab, dpxb = dpa.astype(BF), dpx.astype(BF)
        dxc_ref[...] = du * igv + _dot_nt(dpab, wa_ref[...]) + _dot_nt(dpxb, wx_ref[...])
        dpa_ref[...] = dpab
        dpx_ref[...] = dpxb

        @pl.when(i == nt - 1)
        def _():
            dsp_ref[...] = dsp_ref[...] * (-_sigmoid(-ap_ref[...]))

    tile = _row_spec(tb, R)
    halo = pl.BlockSpec((HALO, R), lambda i: (jnp.maximum(i * per - 1, 0), 0))
    return pl.pallas_call(
        body, name=name, grid=(T // tb,),
        in_specs=[tile, tile, halo, tile, tile, tile, _vec_spec(R, R), _vec_spec(R, R), _vec_spec(R)],
        out_specs=[tile, tile, tile, _vec_spec(R), _vec_spec(R), _vec_spec(R)],
        out_shape=[jax.ShapeDtypeStruct((T, R), F32), jax.ShapeDtypeStruct((T, R), BF), jax.ShapeDtypeStruct((T, R), BF)]
        + [jax.ShapeDtypeStruct((1, R), F32)] * 3,
        scratch_shapes=[pltpu.VMEM((HALO + tb, R), F32)],
        compiler_params=_params("arbitrary"),
    )(lam, hs, hs, r, ig, xc, wa, wx, a_param.reshape(1, R))


def _lru_conv_bwd(dxc, z, conv_w, name):
    T, R = dxc.shape
    tb = _tile(T, TB_SEQ, HALO)
    per = tb // HALO
    nt = T // tb

    def body(d_ref, dn_ref, x_ref, xp_ref, cw_ref, dxb_ref, dcw_ref, dcb_ref, dext, xext):
        i = pl.program_id(0)

        @pl.when(i == 0)
        def _():
            dcw_ref[...] = jnp.zeros_like(dcw_ref)
            dcb_ref[...] = jnp.zeros_like(dcb_ref)

        d = d_ref[...]
        dext[pl.ds(0, tb), :] = d
        dext[pl.ds(tb, HALO), :] = jnp.where(i < nt - 1, dn_ref[...], 0.0)
        xext[pl.ds(0, HALO), :] = jnp.where(i > 0, xp_ref[...], 0.0)
        xext[pl.ds(HALO, tb), :] = x_ref[...]
        dxb = cw_ref[CONV_WIDTH - 1:CONV_WIDTH, :] * d
        for k in range(CONV_WIDTH - 1):
            dxb = dxb + cw_ref[k:k + 1, :] * dext[pl.ds(CONV_WIDTH - 1 - k, tb), :]
        dxb_ref[...] = dxb.astype(BF)
        for k in range(CONV_WIDTH):
            dcw_ref[k:k + 1, :] += jnp.sum(d * xext[pl.ds(HALO - 3 + k, tb), :], axis=0, keepdims=True)
        dcb_ref[...] += jnp.sum(d, axis=0, keepdims=True)

    tile = _row_spec(tb, R)
    nxt = pl.BlockSpec((HALO, R), lambda i: (jnp.minimum((i + 1) * per, T // HALO - 1), 0))
    xtile = pl.BlockSpec((tb, R), lambda i: (i, 1))
    xprev = pl.BlockSpec((HALO, R), lambda i: (jnp.maximum(i * per - 1, 0), 1))
    return pl.pallas_call(
        body, name=name, grid=(nt,), in_specs=[tile, nxt, xtile, xprev, _vec_spec(R, CONV_WIDTH)],
        out_specs=[tile, _vec_spec(R, SUBLANES), _vec_spec(R)],
        out_shape=[jax.ShapeDtypeStruct((T, R), BF), jax.ShapeDtypeStruct((SUBLANES, R), F32),
                   jax.ShapeDtypeStruct((1, R), F32)],
        scratch_shapes=[pltpu.VMEM((tb + HALO, R), F32), pltpu.VMEM((HALO + tb, R), F32)],
        compiler_params=_params("arbitrary"),
    )(dxc, dxc, z, z, conv_w)


def _window_sums(e, n, back):
    out, s = [], e
    for k in (1, 2, 4, 8):
        s = s + pltpu.roll(s, k if back else n - k, 0)
        out.append(s)
    return out


def _pool_fwd(hn, h, w, b, scale, name):
    T, D = hn.shape
    G = len(POOL_WINDOWS)
    gd = D // G
    tb = _tile(T, TB_SEQ, HALO)
    per = tb // HALO

    def body(x_ref, xp_ref, h_ref, w_ref, b_ref, s_ref, o_ref, u_ref, yb_ref):
        i = pl.program_id(0)
        t = i * tb + lax.broadcasted_iota(jnp.int32, (tb, gd), 0) + 1
        for g, win in enumerate(POOL_WINDOWS):
            cols = pl.ds(g * gd, gd)
            x = x_ref[:, cols]
            e = jnp.concatenate([jnp.where(i > 0, xp_ref[:, cols], 0.0), x], axis=0)
            sw = _window_sums(e, HALO + tb, True)[g][HALO:, :]
            u = (sw / jnp.minimum(t, win).astype(F32) - x).astype(BF)
            yb = _dot(u, w_ref[g]) + b_ref[:, cols]
            u_ref[:, cols] = u
            yb_ref[:, cols] = yb
            o_ref[:, cols] = h_ref[:, cols] + yb * s_ref[:, cols]

    tile = _row_spec(tb, D)
    prev = pl.BlockSpec((HALO, D), lambda i: (jnp.maximum(i * per - 1, 0), 0))
    return pl.pallas_call(
        body, name=name, grid=(T // tb,),
        in_specs=[tile, prev, tile, pl.BlockSpec((G, gd, gd), lambda i: (0, 0, 0)), _vec_spec(D), _vec_spec(D)],
        out_specs=[tile, tile, tile],
        out_shape=[jax.ShapeDtypeStruct((T, D), F32), jax.ShapeDtypeStruct((T, D), BF), jax.ShapeDtypeStruct((T, D), F32)],
        compiler_params=_params("parallel"),
    )(hn, hn, h, w, b.reshape(1, D), scale.reshape(1, D))


def _pool_bwd(dm, u, yb, w, scale, name):
    T, D = dm.shape
    G = len(POOL_WINDOWS)
    gd = D // G
    tb = _tile(T, TB_SEQ, HALO)

    def body(d_ref, u_ref, yb_ref, w_ref, s_ref, du_ref, v_ref, dw_ref, db_ref, ds_ref):
        i = pl.program_id(0)

        @pl.when(i == 0)
        def _():
            dw_ref[...] = jnp.zeros_like(dw_ref)
            db_ref[...] = jnp.zeros_like(db_ref)
            ds_ref[...] = jnp.zeros_like(ds_ref)

        d, sc = d_ref[...], s_ref[...]
        ds_ref[...] += jnp.sum(d * yb_ref[...], axis=0, keepdims=True)
        db_ref[...] += jnp.sum(d * sc, axis=0, keepdims=True)
        t = i * tb + lax.broadcasted_iota(jnp.int32, (tb, gd), 0) + 1
        for g, win in enumerate(POOL_WINDOWS):
            cols = pl.ds(g * gd, gd)
            dy = (d_ref[:, cols] * s_ref[:, cols]).astype(BF)
            du = _dot_nt(dy, w_ref[g])
            dw_ref[g] += _dot_tn(u_ref[:, cols], dy)
            du_ref[:, cols] = du
            v_ref[:, cols] = du / jnp.minimum(t, win).astype(F32)

    tile = _row_spec(tb, D)
    return pl.pallas_call(
        body, name=name, grid=(T // tb,),
        in_specs=[tile, tile, tile, pl.BlockSpec((G, gd, gd), lambda i: (0, 0, 0)), _vec_spec(D)],
        out_specs=[tile, tile, pl.BlockSpec((G, gd, gd), lambda i: (0, 0, 0)), _vec_spec(D), _vec_spec(D)],
        out_shape=[jax.ShapeDtypeStruct((T, D), F32), jax.ShapeDtypeStruct((T, D), F32),
                   jax.ShapeDtypeStruct((G, gd, gd), F32), jax.ShapeDtypeStruct((1, D), F32),
                   jax.ShapeDtypeStruct((1, D), F32)],
        compiler_params=_params("arbitrary"),
    )(dm, u, yb, w, scale.reshape(1, D))


def _pool_bwd_win(v, du, name):
    T, D = v.shape
    G = len(POOL_WINDOWS)
    gd = D // G
    tb = _tile(T, TB_SEQ, HALO)
    per = tb // HALO
    nt = T // tb

    def body(v_ref, vn_ref, du_ref, o_ref):
        i = pl.program_id(0)
        for g in range(G):
            cols = pl.ds(g * gd, gd)
            e = jnp.concatenate([v_ref[:, cols], jnp.where(i < nt - 1, vn_ref[:, cols], 0.0)], axis=0)
            o_ref[:, cols] = _window_sums(e, tb + HALO, False)[g][:tb, :] - du_ref[:, cols]

    tile = _row_spec(tb, D)
    nxt = pl.BlockSpec((HALO, D), lambda i: (jnp.minimum((i + 1) * per, T // HALO - 1), 0))
    return pl.pallas_call(
        body, name=name, grid=(nt,), in_specs=[tile, nxt, tile], out_specs=tile,
        out_shape=jax.ShapeDtypeStruct((T, D), F32), compiler_params=_params("parallel"),
    )(v, v, du)


def _adamw(w, g, m, v, name):
    shape = w.shape
    cols = shape[-1] if w.ndim > 1 else shape[0]
    rows = w.size // cols
    tr = _tile(rows, TR_ADAM, SUBLANES)
    c1, c2 = 1.0 / (1.0 - ADAM_B1 ** ADAM_STEP), 1.0 / (1.0 - ADAM_B2 ** ADAM_STEP)

    def body(w_ref, g_ref, m_ref, v_ref, d_ref, mo_ref, vo_ref):
        gv = g_ref[...]
        mn = ADAM_B1 * m_ref[...] + (1.0 - ADAM_B1) * gv
        vn = ADAM_B2 * v_ref[...] + (1.0 - ADAM_B2) * (gv * gv)
        d_ref[...] = -ADAM_LR * ((mn * c1) / (jnp.sqrt(vn * c2) + ADAM_EPS) + ADAM_WD * w_ref[...])
        mo_ref[...] = mn
        vo_ref[...] = vn

    spec = _row_spec(tr, cols)
    outs = pl.pallas_call(
        body, name=name, grid=(rows // tr,), in_specs=[spec] * 4, out_specs=[spec] * 3,
        out_shape=[jax.ShapeDtypeStruct((rows, cols), F32)] * 3, compiler_params=_params("parallel"),
    )(*[t.reshape(rows, cols) for t in (w, g, m, v)])
    return [o.reshape(shape) for o in outs]


def _sum_devices(parts, name):
    n, rows, cols = parts.shape
    tr = _tile(rows, 1024, SUBLANES)

    def body(p_ref, o_ref):
        acc = p_ref[0]
        for k in range(1, n):
            acc = acc + p_ref[k]
        o_ref[...] = acc

    return pl.pallas_call(
        body, name=name, grid=(rows // tr,), in_specs=[pl.BlockSpec((n, tr, cols), lambda i: (0, i, 0))],
        out_specs=_row_spec(tr, cols), out_shape=jax.ShapeDtypeStruct((rows, cols), F32),
        compiler_params=_params("parallel"),
    )(parts)


def _position():
    return lax.axis_index("x"), lax.axis_index("y"), lax.axis_index("c")


def _gathered_shapes(blocks):
    return [jax.ShapeDtypeStruct((b.shape[0], N_DEV * b.shape[1], b.shape[2]), b.dtype) for b in blocks]


def _gather_sems(ng):
    return [pltpu.SemaphoreType.DMA((ng, 7)), pltpu.SemaphoreType.DMA((ng, 7)), pltpu.SemaphoreType.DMA((ng,))] if ng else []


def _gather_plan(blocks, srcs, outs, send_sems, recv_sems, local_sems):
    ng = len(blocks)
    x, y, c = _position()
    me, sibling = (x, y, c), (x, y, 1 - c)
    chips = [(1 - x, y), (x, 1 - y), (1 - x, 1 - y)]

    def rows(g, px, py, pc):
        r = blocks[g].shape[1]
        return outs[g].at[:, pl.ds((4 * px + 2 * py + pc) * r, r), :]

    def copy(g, k, block, to, src=None):
        return pltpu.make_async_remote_copy(
            src_ref=rows(g, *block) if src is None else src, dst_ref=rows(g, *block),
            send_sem=send_sems.at[g, k], recv_sem=recv_sems.at[g, k], device_id=to, device_id_type=MESH)

    def mine(g):
        return pltpu.make_async_copy(srcs[g], rows(g, *me), local_sems.at[g])

    def first(g):
        return [copy(g, 0, me, sibling, src=srcs[g])] + [copy(g, 1 + j, me, (*chip, c), src=srcs[g])
                                                         for j, chip in enumerate(chips)]

    def passed(g):
        return [copy(g, 4 + j, (*chip, c), sibling) for j, chip in enumerate(chips)]

    def start():
        for g in range(ng):
            mine(g).start()
            for cp in first(g):
                cp.start()

    def forward():
        for j, chip in enumerate(chips):
            for g in range(ng):
                copy(g, 1 + j, (*chip, c), me).wait_recv()
                copy(g, 4 + j, (*chip, c), sibling).start()

    def finish():
        for g in range(ng):
            copy(g, 0, sibling, me).wait_recv()
            for j, chip in enumerate(chips):
                copy(g, 4 + j, (*chip, 1 - c), me).wait_recv()
            for cp in first(g) + passed(g):
                cp.wait_send()
            mine(g).wait()

    return start, forward, finish


def _all_gather(blocks, name):
    ng = len(blocks)

    def body(*refs):
        start, forward, finish = _gather_plan(blocks, refs[:ng], refs[ng:2 * ng], *refs[2 * ng:])
        start()
        forward()
        finish()

    hbm = pl.BlockSpec(memory_space=pl.ANY)
    return pl.pallas_call(
        body, name=name, in_specs=[hbm] * ng, out_specs=[hbm] * ng, out_shape=_gathered_shapes(blocks),
        scratch_shapes=_gather_sems(ng),
    )(*blocks)


FLIPS = ((0, 0, 1), (1, 0, 0), (0, 1, 0), (1, 1, 0), (1, 0, 1), (0, 1, 1), (1, 1, 1))


def _scattered_shapes(grads):
    return [jax.ShapeDtypeStruct((len(FLIPS), g.shape[0], g.shape[1] // N_DEV, g.shape[2]), g.dtype) for g in grads]


def _scatter_sems(ng):
    return [pltpu.SemaphoreType.DMA((ng, len(FLIPS))), pltpu.SemaphoreType.DMA((ng, len(FLIPS)))] if ng else []


def _scatter_plan(grads, srcs, outs, send_sems, recv_sems):
    x, y, c = _position()

    def copies():
        out = []
        for g in range(len(grads)):
            r = grads[g].shape[1] // N_DEV
            for k, (fx, fy, fc) in enumerate(FLIPS):
                tx, ty, tc = (1 - x if fx else x), (1 - y if fy else y), (1 - c if fc else c)
                out.append(pltpu.make_async_remote_copy(
                    src_ref=srcs[g].at[:, pl.ds((4 * tx + 2 * ty + tc) * r, r), :], dst_ref=outs[g].at[k],
                    send_sem=send_sems.at[g, k], recv_sem=recv_sems.at[g, k],
                    device_id=(tx, ty, tc), device_id_type=MESH))
        return out

    def start():
        for cp in copies():
            cp.start()

    def finish():
        for cp in copies():
            cp.wait()

    return start, finish


def _scatter(grads, name):
    ng = len(grads)

    def body(*refs):
        start, finish = _scatter_plan(grads, refs[:ng], refs[ng:2 * ng], *refs[2 * ng:])
        start()
        finish()

    hbm = pl.BlockSpec(memory_space=pl.ANY)
    return pl.pallas_call(
        body, name=name, in_specs=[hbm] * ng, out_specs=[hbm] * ng, out_shape=_scattered_shapes(grads),
        scratch_shapes=_scatter_sems(ng),
    )(*grads)


def _scatter_sum(grad, recv, pos, name):
    n, r8, c = grad.shape
    r = r8 // N_DEV

    def body(pos_ref, g_ref, r_ref, o_ref):
        acc = g_ref[...].astype(F32)
        for k in range(len(FLIPS)):
            acc = acc + r_ref[k].astype(F32)
        o_ref[...] = acc

    return pl.pallas_call(
        body, name=name,
        grid_spec=pltpu.PrefetchScalarGridSpec(
            num_scalar_prefetch=1, grid=(n,),
            in_specs=[pl.BlockSpec((None, r, c), lambda i, pos: (i, 4 * pos[0] + 2 * pos[1] + pos[2], 0)),
                      pl.BlockSpec((len(FLIPS), None, r, c), lambda i, pos: (0, i, 0, 0))],
            out_specs=pl.BlockSpec((None, r, c), lambda i, pos: (i, 0, 0))),
        out_shape=jax.ShapeDtypeStruct((n, r, c), F32), compiler_params=_params("parallel"),
    )(pos, grad, recv)


def _block_diag(w):
    H, d, _ = w.shape
    return (jnp.eye(H, dtype=w.dtype)[:, None, :, None] * w[:, :, None, :]).reshape(H * d, H * d)


def _diag_blocks(dense, H):
    d = dense.shape[0] // H
    return jnp.stack([dense[i * d:(i + 1) * d, i * d:(i + 1) * d] for i in range(H)])


def _local_step(x, p, tgt, W, blocks=None, pos=None):
    dist = blocks is not None
    L = p.shape[0]
    W = dict(W)

    def gathering(keys):
        return [k for k in keys if k not in W] if dist else []

    def ffn_fwd(h, g, i, f, name, nxt):
        nxt = gathering(nxt)
        a, b, h, got = _ffn_fwd(h, g, W[("ffn", i, f)], 0, name, gather=[blocks[k] for k in nxt])
        W.update(zip(nxt, got))
        return a, b, h

    saved = []
    h = x
    for i in range(L):
        j = i // 2
        lru = i % 2 == 0
        s = {"h0": h}
        mixer = [("lru_in", j), ("lru_out", j)] if lru else [("pool_w", j)]
        s["a1"], s["b1"], h = ffn_fwd(h, W["ffn1_norm"][i], i, 1, f"ffn1_fwd_{i}", mixer + [("ffn", i, 2)])
        s["h1"] = h
        if lru:
            hn = _rms_fwd(h, W["mix_norm"][i], BF, f"mix_norm_{i}")
            z = _mm(hn, W[("lru_in", j)][0], "nt", f"lru_in_{i}")
            wa, wx = _block_diag(W["lru_w_a"][j]).astype(BF), _block_diag(W["lru_w_x"][j]).astype(BF)
            xc, r, ig, a, bt = _lru_gates(z, W["lru_conv_w"][j], W["lru_conv_b"][j], wa, wx, W["lru_b_a"][j],
                                          W["lru_b_x"][j], W["lru_a_param"][j], f"lru_gates_{i}")
            hs = _lru_scan(a, bt, False, f"lru_scan_{i}")
            y = _lru_out_ew(hs, z, f"lru_out_ew_{i}")
            h = _mm(y, W[("lru_out", j)][0], "nn", f"lru_out_{i}", res=h)
            s.update(hn=hn, z=z, wa=wa, wx=wx, xc=xc, r=r, ig=ig, a=a, hs=hs, y=y)
        else:
            hn = _rms_fwd(h, W["mix_norm"][i], F32, f"mix_norm_{i}")
            h, s["u"], s["yb"] = _pool_fwd(hn, h, W[("pool_w", j)], W["pool_b"][j], W["pool_scale"][j],
                                           f"pool_fwd_{i}")
        s["h2"] = h
        nxt = [("ple_gate", i), ("ple_proj", i)] + ([("ffn", i + 1, 1)] if i + 1 < L else [])
        s["a2"], s["b2"], h = ffn_fwd(h, W["ffn2_norm"][i], i, 2, f"ffn2_fwd_{i}", nxt)
        s["h3"] = h
        s["n4"] = _rms_fwd(h, W["ple_norm"][i], BF, f"ple_norm_{i}")
        s["zp"] = _mm(s["n4"], W[("ple_gate", i)][0], "nn", f"ple_gate_{i}")
        s["pp"] = _mm(p[i], W[("ple_proj", i)][0], "nt", f"ple_proj_{i}")
        h = _ple_fwd_ew(h, s["zp"], s["pp"], f"ple_fwd_ew_{i}")
        saved.append(s)

    loss, dh, d_final = _loss_head(h, W["final_norm"], tgt)

    big, recv = {}, {}
    n_lru, n_pool = L // 2 + L % 2, L // 2
    small = {k: [None] * L for k in ("ffn1_norm", "mix_norm", "ffn2_norm", "ple_norm")}
    for k in ("lru_conv_w", "lru_conv_b", "lru_w_a", "lru_b_a", "lru_w_x", "lru_b_x", "lru_a_param"):
        small[k] = [None] * n_lru
    for k in ("pool_b", "pool_scale"):
        small[k] = [None] * n_pool

    def scattering(keys):
        return [k for k in keys if k in big] if dist else []

    for i in reversed(range(L)):
        j = i // 2
        lru = i % 2 == 0
        s = saved[i]
        dz, dpp = _ple_bwd_ew(dh, s["zp"], s["pp"], f"ple_bwd_ew_{i}")
        big[("ple_gate", i)] = _mm(s["n4"], dz, "tn", f"ple_gate_dw_{i}", out_dtype=BF)[None]
        big[("ple_proj", i)] = _mm(dpp, p[i], "tn", f"ple_proj_dw_{i}", out_dtype=BF)[None]
        dn4 = _mm(dz, W[("ple_gate", i)][0], "nt", f"ple_gate_dx_{i}")
        dh, dg = _rms_bwd(s["h3"], W["ple_norm"][i], dn4, dh, f"ple_norm_bwd_{i}")
        small["ple_norm"][i] = dg[0]
        out = scattering([("ffn", i + 1, 1)])
        da, db, sv, dh, n, dhb, dg, got = _ffn_bwd_act(dh, s["h2"], W["ffn2_norm"][i], s["a2"], s["b2"], W[("ffn", i, 2)],
                                                       0, f"ffn2_bwd_{i}", scatter=[big[k] for k in out])
        recv.update(zip(out, got))
        small["ffn2_norm"][i] = dg[0]
        out = scattering([("ple_gate", i), ("ple_proj", i)])
        big[("ffn", i, 2)], got = _ffn_bwd_w(da, db, sv, n, dhb, f"ffn2_dw_{i}", scatter=[big[k] for k in out])
        recv.update(zip(out, got))
        if lru:
            R = W[("lru_out", j)].shape[1]
            big[("lru_out", j)] = _mm(s["y"], dh, "tn", f"lru_out_dw_{i}", out_dtype=BF)[None]
            dy = _mm(dh, W[("lru_out", j)][0], "nt", f"lru_out_dx_{i}")
            dhd, dgb = _lru_bwd_ew(dy, s["hs"], s["z"], f"lru_bwd_ew_{i}")
            lam = _lru_scan(s["a"], dhd, True, f"lru_scan_bwd_{i}")
            dxc, dpa, dpx, dsp, dba, dbx = _lru_gates_bwd(lam, s["hs"], s["r"], s["ig"], s["xc"], s["wa"], s["wx"],
                                                          W["lru_a_param"][j], f"lru_gates_bwd_{i}")
            small["lru_a_param"][j], small["lru_b_a"][j], small["lru_b_x"][j] = dsp[0], dba[0], dbx[0]
            small["lru_w_a"][j] = _diag_blocks(_mm(s["xc"], dpa, "tn", f"lru_wa_dw_{i}"), LRU_HEADS)
            small["lru_w_x"][j] = _diag_blocks(_mm(s["xc"], dpx, "tn", f"lru_wx_dw_{i}"), LRU_HEADS)
            dxb, dcw, dcb = _lru_conv_bwd(dxc, s["z"], W["lru_conv_w"][j], f"lru_conv_bwd_{i}")
            small["lru_conv_w"][j], small["lru_conv_b"][j] = dcw[:CONV_WIDTH], dcb[0]
            win = W[("lru_in", j)][0]
            big[("lru_in", j)] = jnp.concatenate([_mm(dgb, s["hn"], "tn", f"lru_in_dw_g_{i}", out_dtype=BF),
                                                  _mm(dxb, s["hn"], "tn", f"lru_in_dw_x_{i}", out_dtype=BF)])[None]
            dhn = _mm(dgb, win[:R], "nn", f"lru_in_dx_g_{i}")
            dhn = _mm(dxb, win[R:], "nn", f"lru_in_dx_x_{i}", res=dhn)
            mixer = [("lru_in", j), ("lru_out", j)]
        else:
            du, v, dw, dbp, dsc = _pool_bwd(dh, s["u"], s["yb"], W[("pool_w", j)], W["pool_scale"][j], f"pool_bwd_{i}")
            big[("pool_w", j)] = dw.astype(BF)
            small["pool_b"][j], small["pool_scale"][j] = dbp[0], dsc[0]
            dhn = _pool_bwd_win(v, du, f"pool_bwd_win_{i}")
            mixer = [("pool_w", j)]
        dh, dg = _rms_bwd(s["h1"], W["mix_norm"][i], dhn, dh, f"mix_norm_bwd_{i}")
        small["mix_norm"][i] = dg[0]
        out = scattering([("ffn", i, 2)])
        da, db, sv, dh, n, dhb, dg, got = _ffn_bwd_act(dh, s["h0"], W["ffn1_norm"][i], s["a1"], s["b1"], W[("ffn", i, 1)],
                                                       0, f"ffn1_bwd_{i}", scatter=[big[k] for k in out])
        recv.update(zip(out, got))
        small["ffn1_norm"][i] = dg[0]
        out = scattering(mixer)
        big[("ffn", i, 1)], got = _ffn_bwd_w(da, db, sv, n, dhb, f"ffn1_dw_{i}", scatter=[big[k] for k in out])
        recv.update(zip(out, got))

    small = {k: jnp.stack(v) for k, v in small.items()}
    small["final_norm"] = d_final[0]
    if dist:
        last = [k for k in big if k not in recv]
        recv.update(zip(last, _scatter([big[k] for k in last], "scatter_last")))
        big = {k: _scatter_sum(big[k], recv[k], pos, "sum_" + "_".join(map(str, k))) for k in big}
    return loss, dh, big, small


SMALL_SHARDED = ("pool_b", "pool_scale", "lru_conv_w")
SMALL = ("ffn1_norm", "mix_norm", "ffn2_norm", "ple_norm", "final_norm", "lru_conv_b", "lru_w_a", "lru_b_a",
         "lru_w_x", "lru_b_x", "lru_a_param", "pool_b", "pool_scale", "lru_conv_w")


def _pack_big(w):
    t = lambda a: jnp.swapaxes(a, -1, -2)
    out = {}
    for i in range(w["ffn1_norm"].shape[0]):
        for f in (1, 2):
            out[("ffn", i, f)] = jnp.stack([t(w[f"ffn{f}_w_gate"][i]), t(w[f"ffn{f}_w_up"][i]), w[f"ffn{f}_w_down"][i]])
        out[("ple_gate", i)], out[("ple_proj", i)] = w["ple_w_gate"][i][None], t(w["ple_w_proj"][i])[None]
    for j in range(w["lru_w_in"].shape[0]):
        out[("lru_in", j)], out[("lru_out", j)] = t(w["lru_w_in"][j])[None], w["lru_w_out"][j][None]
    for j in range(w["pool_w"].shape[0]):
        out[("pool_w", j)] = w["pool_w"][j]
    return out


def _unpack_big(b, L):
    t = lambda a: jnp.swapaxes(a, -1, -2)
    n_lru, n_pool = L // 2 + L % 2, L // 2
    out = {"lru_w_in": jnp.stack([t(b[("lru_in", j)][0]) for j in range(n_lru)]),
           "lru_w_out": jnp.stack([b[("lru_out", j)][0] for j in range(n_lru)]),
           "pool_w": jnp.stack([b[("pool_w", j)] for j in range(n_pool)]),
           "ple_w_gate": jnp.stack([b[("ple_gate", i)][0] for i in range(L)]),
           "ple_w_proj": jnp.stack([t(b[("ple_proj", i)][0]) for i in range(L)])}
    for f in (1, 2):
        out[f"ffn{f}_w_gate"] = jnp.stack([t(b[("ffn", i, f)][0]) for i in range(L)])
        out[f"ffn{f}_w_up"] = jnp.stack([t(b[("ffn", i, f)][1]) for i in range(L)])
        out[f"ffn{f}_w_down"] = jnp.stack([b[("ffn", i, f)][2] for i in range(L)])
    return out


def _flatten(parts, names, rows_of=LANES):
    flat = jnp.concatenate([parts[k].reshape(-1) for k in names])
    pad = (-flat.size) % (16 * rows_of)
    return jnp.pad(flat, (0, pad)).reshape(1, -1, rows_of)


def _unflatten(flat, like, names):
    out, o = {}, 0
    flat = flat.reshape(-1)
    for k in names:
        n = like[k].size
        out[k] = flat[o:o + n].reshape(like[k].shape)
        o += n
    return out


def kernel(x, p, ffn1_norm, ffn1_w_gate, ffn1_w_up, ffn1_w_down, mix_norm, lru_w_in, lru_conv_w, lru_conv_b, lru_w_a, lru_b_a, lru_w_x, lru_b_x, lru_a_param, lru_w_out, pool_w, pool_b, pool_scale, ffn2_norm, ffn2_w_gate, ffn2_w_up, ffn2_w_down, ple_norm, ple_w_gate, ple_w_proj, final_norm, loss_target, m_ffn1_norm, m_ffn1_w_gate, m_ffn1_w_up, m_ffn1_w_down, m_mix_norm, m_lru_w_in, m_lru_conv_w, m_lru_conv_b, m_lru_w_a, m_lru_b_a, m_lru_w_x, m_lru_b_x, m_lru_a_param, m_lru_w_out, m_pool_w, m_pool_b, m_pool_scale, m_ffn2_norm, m_ffn2_w_gate, m_ffn2_w_up, m_ffn2_w_down, m_ple_norm, m_ple_w_gate, m_ple_w_proj, m_final_norm, v_ffn1_norm, v_ffn1_w_gate, v_ffn1_w_up, v_ffn1_w_down, v_mix_norm, v_lru_w_in, v_lru_conv_w, v_lru_conv_b, v_lru_w_a, v_lru_b_a, v_lru_w_x, v_lru_b_x, v_lru_a_param, v_lru_w_out, v_pool_w, v_pool_b, v_pool_scale, v_ffn2_norm, v_ffn2_w_gate, v_ffn2_w_up, v_ffn2_w_down, v_ple_norm, v_ple_w_gate, v_ple_w_proj, v_final_norm):
    names = ["ffn1_norm", "ffn1_w_gate", "ffn1_w_up", "ffn1_w_down", "mix_norm", "lru_w_in", "lru_conv_w", "lru_conv_b",
             "lru_w_a", "lru_b_a", "lru_w_x", "lru_b_x", "lru_a_param", "lru_w_out", "pool_w", "pool_b", "pool_scale",
             "ffn2_norm", "ffn2_w_gate", "ffn2_w_up", "ffn2_w_down", "ple_norm", "ple_w_gate", "ple_w_proj", "final_norm"]
    w = dict(zip(names, [ffn1_norm, ffn1_w_gate, ffn1_w_up, ffn1_w_down, mix_norm, lru_w_in, lru_conv_w, lru_conv_b, lru_w_a, lru_b_a, lru_w_x, lru_b_x, lru_a_param, lru_w_out, pool_w, pool_b, pool_scale, ffn2_norm, ffn2_w_gate, ffn2_w_up, ffn2_w_down, ple_norm, ple_w_gate, ple_w_proj, final_norm]))
    m = dict(zip(names, [m_ffn1_norm, m_ffn1_w_gate, m_ffn1_w_up, m_ffn1_w_down, m_mix_norm, m_lru_w_in, m_lru_conv_w, m_lru_conv_b, m_lru_w_a, m_lru_b_a, m_lru_w_x, m_lru_b_x, m_lru_a_param, m_lru_w_out, m_pool_w, m_pool_b, m_pool_scale, m_ffn2_norm, m_ffn2_w_gate, m_ffn2_w_up, m_ffn2_w_down, m_ple_norm, m_ple_w_gate, m_ple_w_proj, m_final_norm]))
    v = dict(zip(names, [v_ffn1_norm, v_ffn1_w_gate, v_ffn1_w_up, v_ffn1_w_down, v_mix_norm, v_lru_w_in, v_lru_conv_w, v_lru_conv_b, v_lru_w_a, v_lru_b_a, v_lru_w_x, v_lru_b_x, v_lru_a_param, v_lru_w_out, v_pool_w, v_pool_b, v_pool_scale, v_ffn2_norm, v_ffn2_w_gate, v_ffn2_w_up, v_ffn2_w_down, v_ple_norm, v_ple_w_gate, v_ple_w_proj, v_final_norm]))
    L = p.shape[0]
    px, py, pc = _position()
    pos = jnp.stack([px, py, pc]).astype(jnp.int32)
    me = 4 * px + 2 * py + pc

    blocks = {k: b.astype(BF) for k, b in _pack_big(w).items()}
    first = ("ffn", 0, 1)
    got, small_blocks = _all_gather([blocks[first], _flatten(w, SMALL_SHARDED)], "gather_first")
    W = {first: got}
    per_dev = small_blocks.reshape(N_DEV, -1)
    shards = [_unflatten(per_dev[k], w, SMALL_SHARDED) for k in range(N_DEV)]
    for k in SMALL:
        W[k] = jnp.concatenate([s[k] for s in shards], axis=-1) if k in SMALL_SHARDED else w[k]

    loss, dx, mine, small = _local_step(x[0], p[:, 0], loss_target[0], W, blocks, pos)
    grads = _unpack_big(mine, L)

    parts = _all_gather([_flatten(small, SMALL)], "gather_small_grads")[0]
    total = _sum_devices(parts.reshape(N_DEV, -1, LANES), "sum_small_grads")
    full = _unflatten(total, {k: W[k] for k in SMALL}, SMALL)
    for k in SMALL:
        if k in SMALL_SHARDED:
            n = w[k].shape[-1]
            grads[k] = lax.dynamic_slice_in_dim(full[k], me * n, n, axis=-1)
        else:
            grads[k] = full[k]

    delta, new_m, new_v = {}, {}, {}
    for k in names:
        delta[k], new_m[k], new_v[k] = _adamw(w[k], grads[k], m[k], v[k], f"adamw_{k}")
    total_loss = lax.psum(loss[0, 0], ("x", "y", "c"))
    return (total_loss, dx[None], *[grads[k] for k in names], *[delta[k] for k in names],
            *[new_m[k] for k in names], *[new_v[k] for k in names])
```

```python
import functools

import jax
import jax.numpy as jnp
from jax import lax
from jax.experimental import pallas as pl
from jax.experimental.pallas import tpu as pltpu

F32 = jnp.float32
BF = jnp.bfloat16
MESH = pl.DeviceIdType.MESH

RMS_EPS = 1e-6
LRU_C = 8.0
LRU_HEADS = 16
CONV_WIDTH = 4
POOL_WINDOWS = (2, 4, 8, 16)
ADAM_LR, ADAM_B1, ADAM_B2, ADAM_EPS, ADAM_WD, ADAM_STEP = 0.001, 0.9, 0.999, 1e-08, 0.01, 10

N_DEV = 8
LANES = 128
SUBLANES = 8
HALO = 16
VMEM_LIMIT = 56 * 1024 * 1024

TM_FFN = 1024
TM_FFN_IN = 512
TN_FFN_OUT = 512
TF_FFN = 256
TF_FFN_WG = 1408
TK_FFN_WG = 512
TB_SEQ = 256
TB_SCAN = 512
TC_SCAN = 256
TM_EW = 512
TM_MM, TN_MM, TK_MM = 1024, 512, 1024
TR_ADAM = 512


def _tile(n, pref, align):
    if n <= pref:
        return n
    t = (pref // align) * align
    while t >= align:
        if n % t == 0:
            return t
        t -= align
    raise ValueError(f"no tile for {n} (pref {pref}, align {align})")


def _params(*sem):
    return pltpu.CompilerParams(dimension_semantics=sem, vmem_limit_bytes=VMEM_LIMIT)


def _dot(a, b):
    return lax.dot_general(a, b, (((1,), (0,)), ((), ())), preferred_element_type=F32)


def _dot_nt(a, b):
    return lax.dot_general(a, b, (((1,), (1,)), ((), ())), preferred_element_type=F32)


def _dot_tn(a, b):
    return lax.dot_general(a, b, (((0,), (0,)), ((), ())), preferred_element_type=F32)


def _sigmoid(x):
    return 1.0 / (1.0 + jnp.exp(-x))


def _gelu_parts(x):
    k0, k1 = 0.7978845608028654, 0.044715
    t = jnp.tanh(k0 * (x + k1 * x * x * x))
    g = 0.5 * x * (1.0 + t)
    dg = 0.5 * (1.0 + t) + 0.5 * x * (1.0 - t * t) * k0 * (1.0 + 3.0 * k1 * x * x)
    return g, dg


def _neg_expm1(x):
    p = x * (1.0 + x * (0.5 + x * (1.0 / 6 + x * (1.0 / 24 + x * (1.0 / 120 + x * (1.0 / 720 + x * (1.0 / 5040)))))))
    return jnp.where(x > -0.35, -p, 1.0 - jnp.exp(x))


def _softplus_neg(l):
    u = jnp.exp(-jnp.abs(l))
    w = 1.0 + u
    log1p = jnp.where(w == 1.0, u, jnp.log(w) * (u / jnp.where(w == 1.0, 1.0, w - 1.0)))
    return jnp.maximum(-l, 0.0) + log1p


def _rms_parts(x, g):
    r = lax.rsqrt(jnp.mean(x * x, axis=-1, keepdims=True) + RMS_EPS)
    nhat = x * r
    return nhat * g, nhat, r


def _rms_bwd_parts(x, g, dn):
    _, nhat, r = _rms_parts(x, g)
    u = dn * g
    dx = r * (u - nhat * jnp.mean(u * nhat, axis=-1, keepdims=True))
    return dx, jnp.sum(dn * nhat, axis=0, keepdims=True)


def _row_spec(tm, d, single=False):
    if single:
        return pl.BlockSpec((tm, d), lambda i, *_: (i, 0), pipeline_mode=pl.Buffered(1))
    return pl.BlockSpec((tm, d), lambda i, *_: (i, 0))


def _vec_spec(d, rows=1):
    return pl.BlockSpec((rows, d), lambda *_: (0, 0))


def _mm(x, w, mode, name, out_dtype=F32, res=None, alpha=1.0, tm=None, tn=None, tk=None):
    if mode == "nn":
        (M, K), (_, N) = x.shape, w.shape
    elif mode == "nt":
        (M, K), (N, _) = x.shape, w.shape
    else:
        (K, M), (_, N) = x.shape, w.shape
    tm = _tile(M, tm or TM_MM, LANES if mode == "tn" else SUBLANES)
    tn = _tile(N, tn or TN_MM, LANES)
    tk = _tile(K, tk or TK_MM, LANES if mode != "tn" else 16)
    nk = K // tk
    dot = {"nn": _dot, "nt": _dot_nt, "tn": _dot_tn}[mode]

    def body(*refs):
        if res is None:
            x_ref, w_ref, o_ref, acc = refs
        else:
            x_ref, w_ref, r_ref, o_ref, acc = refs
        k = pl.program_id(2)

        @pl.when(k == 0)
        def _():
            acc[...] = jnp.zeros_like(acc)

        acc[...] += dot(x_ref[...].astype(BF), w_ref[...].astype(BF))

        @pl.when(k == nk - 1)
        def _():
            r = acc[...] if alpha == 1.0 else acc[...] * alpha
            if res is not None:
                r = r_ref[...] + r
            o_ref[...] = r.astype(out_dtype)

    if mode == "nn":
        specs = [pl.BlockSpec((tm, tk), lambda i, j, k: (i, k)), pl.BlockSpec((tk, tn), lambda i, j, k: (k, j))]
    elif mode == "nt":
        specs = [pl.BlockSpec((tm, tk), lambda i, j, k: (i, k)), pl.BlockSpec((tn, tk), lambda i, j, k: (j, k))]
    else:
        specs = [pl.BlockSpec((tk, tm), lambda i, j, k: (k, i)), pl.BlockSpec((tk, tn), lambda i, j, k: (k, j))]
    args = [x, w]
    if res is not None:
        specs.append(pl.BlockSpec((tm, tn), lambda i, j, k: (i, j)))
        args.append(res)
    return pl.pallas_call(
        body, name=name, grid=(M // tm, N // tn, nk), in_specs=specs,
        out_specs=pl.BlockSpec((tm, tn), lambda i, j, k: (i, j)),
        out_shape=jax.ShapeDtypeStruct((M, N), out_dtype),
        scratch_shapes=[pltpu.VMEM((tm, tn), F32)],
        compiler_params=_params("parallel", "parallel", "arbitrary"),
    )(*args)


def _rms_fwd(h, g, out_dtype, name):
    T, D = h.shape
    tm = _tile(T, TM_EW, 16)

    def body(h_ref, g_ref, o_ref):
        o_ref[...] = _rms_parts(h_ref[...], g_ref[...])[0].astype(out_dtype)

    return pl.pallas_call(
        body, name=name, grid=(T // tm,), in_specs=[_row_spec(tm, D), _vec_spec(D)], out_specs=_row_spec(tm, D),
        out_shape=jax.ShapeDtypeStruct((T, D), out_dtype), compiler_params=_params("parallel"),
    )(h, g.reshape(1, D))


def _rms_bwd(h, g, dn, dres, name):
    T, D = h.shape
    tm = _tile(T, TM_EW, 16)

    def body(h_ref, g_ref, dn_ref, dr_ref, dh_ref, dg_ref):
        @pl.when(pl.program_id(0) == 0)
        def _():
            dg_ref[...] = jnp.zeros_like(dg_ref)

        dx, dg = _rms_bwd_parts(h_ref[...], g_ref[...], dn_ref[...].astype(F32))
        dh_ref[...] = dr_ref[...] + dx
        dg_ref[...] += dg

    return pl.pallas_call(
        body, name=name, grid=(T // tm,),
        in_specs=[_row_spec(tm, D), _vec_spec(D), _row_spec(tm, D), _row_spec(tm, D)],
        out_specs=[_row_spec(tm, D), _vec_spec(D)],
        out_shape=[jax.ShapeDtypeStruct((T, D), F32), jax.ShapeDtypeStruct((1, D), F32)],
        compiler_params=_params("arbitrary"),
    )(h, g.reshape(1, D), dn, dres)


def _loss_head(h, g, tgt):
    T, D = h.shape
    tm = _tile(T, TM_EW, 16)

    def body(h_ref, g_ref, t_ref, loss_ref, dh_ref, dg_ref):
        @pl.when(pl.program_id(0) == 0)
        def _():
            dg_ref[...] = jnp.zeros_like(dg_ref)
            loss_ref[...] = jnp.zeros_like(loss_ref)

        x, gg = h_ref[...], g_ref[...]
        y = _rms_parts(x, gg)[0]
        e = y - t_ref[...]
        part = jnp.sum(jnp.sum(e * e, axis=0, keepdims=True), axis=1, keepdims=True) * (0.5 / D)
        loss_ref[...] += jnp.broadcast_to(part, loss_ref.shape)
        dx, dg = _rms_bwd_parts(x, gg, e * (1.0 / D))
        dh_ref[...] = dx
        dg_ref[...] += dg

    return pl.pallas_call(
        body, name="loss_head", grid=(T // tm,),
        in_specs=[_row_spec(tm, D), _vec_spec(D), _row_spec(tm, D)],
        out_specs=[_vec_spec(LANES), _row_spec(tm, D), _vec_spec(D)],
        out_shape=[jax.ShapeDtypeStruct((1, LANES), F32), jax.ShapeDtypeStruct((T, D), F32),
                   jax.ShapeDtypeStruct((1, D), F32)],
        compiler_params=_params("arbitrary"),
    )(h, g.reshape(1, D), tgt)


def _carry(plan, first, mid, last):
    pl.when(first)(plan[0])
    if len(plan) == 3:
        pl.when(mid)(plan[1])
    pl.when(last)(plan[-1])


def _ffn_fwd_act(h, g, wffn, name, gather=()):
    T, D = h.shape
    F = wffn.shape[1]
    tm, tf = _tile(T, TM_FFN, 16), _tile(F, TF_FFN, LANES)
    ni, nf, ng = T // tm, F // tf, len(gather)

    def body(*refs):
        h_ref, g_ref, wg_ref, wu_ref = refs[:4]
        srcs, (a_ref, b_ref, s_ref), outs = refs[4:4 + ng], refs[4 + ng:7 + ng], refs[7 + ng:7 + 2 * ng]
        n_sc = refs[7 + 2 * ng]
        i, j = pl.program_id(0), pl.program_id(1)
        if ng:
            _carry(_gather_plan(gather, srcs, outs, *refs[8 + 2 * ng:]), jnp.logical_and(i == 0, j == 0),
                   jnp.logical_and(i == (3 * ni) // 4, j == 0), jnp.logical_and(i == ni - 1, j == nf - 1))

        @pl.when(j == 0)
        def _():
            n_sc[...] = _rms_parts(h_ref[...], g_ref[...])[0].astype(BF)

        n = n_sc[...]
        a = _dot_nt(n, wg_ref[...])
        b = _dot_nt(n, wu_ref[...])
        a_ref[...] = a.astype(BF)
        b_ref[...] = b.astype(BF)
        s_ref[...] = (a * _sigmoid(a) * b).astype(BF)

    tile = pl.BlockSpec((tm, tf), lambda i, j: (i, j))
    w = [pl.BlockSpec((None, tf, D), functools.partial(lambda k, i, j: (k, j, 0), k)) for k in (0, 1)]
    hbm = pl.BlockSpec(memory_space=pl.ANY)
    outs = pl.pallas_call(
        body, name=name, grid=(ni, nf), in_specs=[_row_spec(tm, D), _vec_spec(D)] + w + [hbm] * ng,
        out_specs=[tile, tile, tile] + [hbm] * ng,
        out_shape=[jax.ShapeDtypeStruct((T, F), BF)] * 3 + _gathered_shapes(gather),
        scratch_shapes=[pltpu.VMEM((tm, D), BF)] + _gather_sems(ng),
        compiler_params=_params("arbitrary", "arbitrary"),
    )(h, g.reshape(1, D), wffn, wffn, *gather)
    return outs[0], outs[1], outs[2], list(outs[3:])


def _ffn_fwd_out(s, wffn, h, name, gather=()):
    T, F = s.shape
    D = h.shape[1]
    tm, tn = _tile(T, TM_FFN, 16), _tile(D, TN_FFN_OUT, LANES)
    ni, nj, ng = T // tm, D // tn, len(gather)

    def body(*refs):
        s_ref, w_ref, h_ref = refs[:3]
        srcs, o_ref, outs = refs[3:3 + ng], refs[3 + ng], refs[4 + ng:4 + 2 * ng]
        i, j = pl.program_id(0), pl.program_id(1)
        if ng:
            _carry(_gather_plan(gather, srcs, outs, *refs[4 + 2 * ng:]), jnp.logical_and(i == 0, j == 0),
                   jnp.logical_and(i == (3 * ni) // 4, j == 0), jnp.logical_and(i == ni - 1, j == nj - 1))
        o_ref[...] = h_ref[...] + 0.5 * _dot(s_ref[...], w_ref[...])

    hbm = pl.BlockSpec(memory_space=pl.ANY)
    tile = pl.BlockSpec((tm, tn), lambda i, j: (i, j))
    outs = pl.pallas_call(
        body, name=name, grid=(ni, nj),
        in_specs=[pl.BlockSpec((tm, F), lambda i, j: (i, 0)), pl.BlockSpec((None, F, tn), lambda i, j: (2, 0, j)), tile]
        + [hbm] * ng,
        out_specs=[tile] + [hbm] * ng, out_shape=[jax.ShapeDtypeStruct((T, D), F32)] + _gathered_shapes(gather),
        scratch_shapes=_gather_sems(ng), compiler_params=_params("arbitrary", "arbitrary"),
    )(s, wffn, h, *gather)
    return outs[0], list(outs[1:])


def _ffn_bwd_act(dh, a, b, wffn, name, scatter=()):
    T, D = dh.shape
    F = wffn.shape[1]
    tm, tf = _tile(T, TM_FFN, 16), _tile(F, TF_FFN, LANES)
    ni, nf, ng = T // tm, F // tf, len(scatter)

    def body(*refs):
        dh_ref, a_ref, b_ref, wd_ref = refs[:4]
        srcs, (da_ref, db_ref, dhb_ref), outs = refs[4:4 + ng], refs[4 + ng:7 + ng], refs[7 + ng:7 + 2 * ng]
        i, j = pl.program_id(0), pl.program_id(1)
        if ng:
            _carry(_scatter_plan(scatter, srcs, outs, *refs[7 + 2 * ng:]), jnp.logical_and(i == 0, j == 0), None,
                   jnp.logical_and(i == ni - 1, j == nf - 1))

        @pl.when(j == 0)
        def _():
            dhb_ref[...] = dh_ref[...].astype(BF)

        ds = 0.5 * _dot_nt(dhb_ref[...], wd_ref[...])
        av, bv = a_ref[...].astype(F32), b_ref[...].astype(F32)
        sig = _sigmoid(av)
        da_ref[...] = (ds * bv * (sig * (1.0 + av * (1.0 - sig)))).astype(BF)
        db_ref[...] = (ds * (av * sig)).astype(BF)

    tile = pl.BlockSpec((tm, tf), lambda i, j: (i, j))
    hbm = pl.BlockSpec(memory_space=pl.ANY)
    outs = pl.pallas_call(
        body, name=name, grid=(ni, nf),
        in_specs=[_row_spec(tm, D), tile, tile, pl.BlockSpec((None, tf, D), lambda i, j: (2, j, 0))] + [hbm] * ng,
        out_specs=[tile, tile, _row_spec(tm, D)] + [hbm] * ng,
        out_shape=[jax.ShapeDtypeStruct((T, F), BF)] * 2 + [jax.ShapeDtypeStruct((T, D), BF)]
        + _scattered_shapes(scatter),
        scratch_shapes=_scatter_sems(ng), compiler_params=_params("arbitrary", "arbitrary"),
    )(dh, a, b, wffn, *[piece[0] for piece in scatter])
    return outs[0], outs[1], outs[2], list(outs[3:])


def _ffn_bwd_in(da, db, wffn, h, g, dh, name, scatter=()):
    T, F = da.shape
    D = h.shape[1]
    tm = _tile(T, TM_FFN_IN, 16)
    ni, ng = T // tm, len(scatter)

    def body(*refs):
        da_ref, db_ref, wg_ref, wu_ref, h_ref, g_ref, dh_ref = refs[:7]
        srcs, (o_ref, dg_ref), outs = refs[7:7 + ng], refs[7 + ng:9 + ng], refs[9 + ng:9 + 2 * ng]
        i = pl.program_id(0)
        if ng:
            _carry(_scatter_plan(scatter, srcs, outs, *refs[9 + 2 * ng:]), i == 0, None, i == ni - 1)

        @pl.when(i == 0)
        def _():
            dg_ref[...] = jnp.zeros_like(dg_ref)

        dn = _dot(da_ref[...], wg_ref[...]) + _dot(db_ref[...], wu_ref[...])
        dx, dg = _rms_bwd_parts(h_ref[...], g_ref[...], dn)
        o_ref[...] = dh_ref[...] + dx
        dg_ref[...] += dg

    hbm = pl.BlockSpec(memory_space=pl.ANY)
    act = pl.BlockSpec((tm, F), lambda i: (i, 0))
    w = [pl.BlockSpec((None, F, D), functools.partial(lambda k, i: (k, 0, 0), k), pipeline_mode=pl.Buffered(1))
         for k in (0, 1)]
    outs = pl.pallas_call(
        body, name=name, grid=(ni,),
        in_specs=[act, act] + w + [_row_spec(tm, D), _vec_spec(D), _row_spec(tm, D)] + [hbm] * ng,
        out_specs=[_row_spec(tm, D), _vec_spec(D)] + [hbm] * ng,
        out_shape=[jax.ShapeDtypeStruct((T, D), F32), jax.ShapeDtypeStruct((1, D), F32)] + _scattered_shapes(scatter),
        scratch_shapes=_scatter_sems(ng), compiler_params=_params("arbitrary"),
    )(da, db, wffn, wffn, h, g.reshape(1, D), dh, *[piece[0] for piece in scatter])
    return outs[0], outs[1], list(outs[2:])


def _ffn_bwd_w(da, db, s, h, g, dhb, name, scatter=()):
    T, F = da.shape
    D = h.shape[1]
    tf, tk = _tile(F, TF_FFN_WG, LANES), _tile(T, TK_FFN_WG, 16)
    nj, nk, ng = F // tf, T // tk, len(scatter)

    def body(*refs):
        da_ref, db_ref, s_ref, h_ref, g_ref, dh_ref = refs[:6]
        srcs, o_ref, outs = refs[6:6 + ng], refs[6 + ng], refs[7 + ng:7 + 2 * ng]
        g_sc, u_sc, d_sc = refs[7 + 2 * ng:10 + 2 * ng]
        j, k = pl.program_id(0), pl.program_id(1)
        if ng:
            _carry(_scatter_plan(scatter, srcs, outs, *refs[10 + 2 * ng:]), jnp.logical_and(j == 0, k == 0), None,
                   jnp.logical_and(j == nj - 1, k == nk - 1))

        @pl.when(k == 0)
        def _():
            g_sc[...] = jnp.zeros_like(g_sc)
            u_sc[...] = jnp.zeros_like(u_sc)
            d_sc[...] = jnp.zeros_like(d_sc)

        nv = _rms_parts(h_ref[...], g_ref[...])[0].astype(BF)
        g_sc[...] += _dot_tn(da_ref[...], nv)
        u_sc[...] += _dot_tn(db_ref[...], nv)
        d_sc[...] += _dot_tn(s_ref[...], dh_ref[...])

        @pl.when(k == nk - 1)
        def _():
            o_ref[0] = g_sc[...].astype(BF)
            o_ref[1] = u_sc[...].astype(BF)
            o_ref[2] = (0.5 * d_sc[...]).astype(BF)

    act = pl.BlockSpec((tk, tf), lambda j, k: (k, j))
    tok = pl.BlockSpec((tk, D), lambda j, k: (k, 0))
    hbm = pl.BlockSpec(memory_space=pl.ANY)
    outs = pl.pallas_call(
        body, name=name, grid=(nj, nk), in_specs=[act, act, act, tok, _vec_spec(D), tok] + [hbm] * ng,
        out_specs=[pl.BlockSpec((3, tf, D), lambda j, k: (0, j, 0), pipeline_mode=pl.Buffered(1))] + [hbm] * ng,
        out_shape=[jax.ShapeDtypeStruct((3, F, D), BF)] + _scattered_shapes(scatter),
        scratch_shapes=[pltpu.VMEM((tf, D), F32)] * 3 + _scatter_sems(ng),
        compiler_params=_params("arbitrary", "arbitrary"),
    )(da, db, s, h, g.reshape(1, D), dhb, *[piece[0] for piece in scatter])
    return outs[0], list(outs[1:])


def _ple_fwd_ew(h, z, pp, name):
    T, D = h.shape
    tm = _tile(T, TM_EW, 16)

    def body(h_ref, z_ref, p_ref, o_ref):
        o_ref[...] = h_ref[...] + _sigmoid(z_ref[...]) * p_ref[...]

    return pl.pallas_call(
        body, name=name, grid=(T // tm,), in_specs=[_row_spec(tm, D)] * 3, out_specs=_row_spec(tm, D),
        out_shape=jax.ShapeDtypeStruct((T, D), F32), compiler_params=_params("parallel"),
    )(h, z, pp)


def _ple_bwd_ew(dh, z, pp, name):
    T, D = dh.shape
    tm = _tile(T, TM_EW, 16)

    def body(dh_ref, z_ref, p_ref, dz_ref, dp_ref):
        gate = _sigmoid(z_ref[...])
        d = dh_ref[...]
        dz_ref[...] = (d * p_ref[...] * gate * (1.0 - gate)).astype(BF)
        dp_ref[...] = (d * gate).astype(BF)

    return pl.pallas_call(
        body, name=name, grid=(T // tm,), in_specs=[_row_spec(tm, D)] * 3, out_specs=[_row_spec(tm, D)] * 2,
        out_shape=[jax.ShapeDtypeStruct((T, D), BF)] * 2, compiler_params=_params("parallel"),
    )(dh, z, pp)


def _lru_gates(z, conv_w, conv_b, wa, wx, b_a, b_x, a_param, name):
    T, R2 = z.shape
    R = R2 // 2
    tb = _tile(T, TB_SEQ, HALO)
    per = tb // HALO

    def body(x_ref, halo_ref, cw_ref, cb_ref, wa_ref, wx_ref, ba_ref, bx_ref, ap_ref,
             xc_ref, r_ref, ig_ref, a_ref, bt_ref, ext):
        i = pl.program_id(0)
        ext[pl.ds(0, HALO), :] = jnp.where(i > 0, halo_ref[...], 0.0)
        ext[pl.ds(HALO, tb), :] = x_ref[...]
        xc = cb_ref[...] + cw_ref[0:1, :] * ext[pl.ds(HALO - 3, tb), :]
        for k in range(1, CONV_WIDTH):
            xc = xc + cw_ref[k:k + 1, :] * ext[pl.ds(HALO - 3 + k, tb), :]
        xcb = xc.astype(BF)
        r = _sigmoid(_dot(xcb, wa_ref[...]) + ba_ref[...])
        ig = _sigmoid(_dot(xcb, wx_ref[...]) + bx_ref[...])
        la = -LRU_C * r * _softplus_neg(ap_ref[...])
        xc_ref[...] = xc
        r_ref[...] = r
        ig_ref[...] = ig
        a_ref[...] = jnp.exp(la)
        bt_ref[...] = jnp.sqrt(_neg_expm1(2.0 * la)) * (ig * xc)

    tile = pl.BlockSpec((tb, R), lambda i: (i, 1))
    halo = pl.BlockSpec((HALO, R), lambda i: (jnp.maximum(i * per - 1, 0), 1))
    out = pl.BlockSpec((tb, R), lambda i: (i, 0))
    return pl.pallas_call(
        body, name=name, grid=(T // tb,),
        in_specs=[tile, halo, _vec_spec(R, CONV_WIDTH), _vec_spec(R), _vec_spec(R, R), _vec_spec(R, R),
                  _vec_spec(R), _vec_spec(R), _vec_spec(R)],
        out_specs=[out] * 5, out_shape=[jax.ShapeDtypeStruct((T, R), F32)] * 5,
        scratch_shapes=[pltpu.VMEM((HALO + tb, R), F32)],
        compiler_params=_params("parallel"),
    )(z, z, conv_w, conv_b.reshape(1, R), wa, wx, b_a.reshape(1, R), b_x.reshape(1, R), a_param.reshape(1, R))


def _lru_scan(a, b, reverse, name):
    T, R = a.shape
    tb, tc = _tile(T, TB_SCAN, SUBLANES), _tile(R, TC_SCAN, LANES)
    nt, ng = T // tb, tb // SUBLANES

    def body(a_ref, b_ref, o_ref, carry, a_sc, b_sc):
        @pl.when(pl.program_id(1) == 0)
        def _():
            carry[...] = jnp.zeros_like(carry)

        A = a_ref[...]
        B = A * b_ref[...] if reverse else b_ref[...]
        sub = lax.broadcasted_iota(jnp.int32, (tb, tc), 0) & (SUBLANES - 1)
        for k in (1, 2, 4):
            m = (sub < SUBLANES - k) if reverse else (sub >= k)
            shift = tb - k if reverse else k
            a_n = jnp.where(m, pltpu.roll(A, shift, 0), 1.0)
            b_n = jnp.where(m, pltpu.roll(B, shift, 0), 0.0)
            B = A * b_n + B
            A = A * a_n
        a_sc[...] = A
        b_sc[...] = B
        sub8 = lax.broadcasted_iota(jnp.int32, (SUBLANES, tc), 0)

        def group(q, c):
            g = (ng - 1 - q) if reverse else q
            rows = pl.ds(pl.multiple_of(g * SUBLANES, SUBLANES), SUBLANES)
            hg = a_sc[rows, :] * c + b_sc[rows, :]
            if reverse:
                nxt = jnp.where(sub8 == SUBLANES - 1, c, pltpu.roll(hg, SUBLANES - 1, 0))
                o_ref[rows, :] = b_ref[rows, :] + nxt
                return hg[0:1, :]
            o_ref[rows, :] = hg
            return hg[SUBLANES - 1:SUBLANES, :]

        carry[...] = lax.fori_loop(0, ng, group, carry[...])

    spec = pl.BlockSpec((tb, tc), (lambda c, t: (nt - 1 - t, c)) if reverse else (lambda c, t: (t, c)))
    return pl.pallas_call(
        body, name=name, grid=(R // tc, nt), in_specs=[spec, spec], out_specs=spec,
        out_shape=jax.ShapeDtypeStruct((T, R), F32),
        scratch_shapes=[pltpu.VMEM((1, tc), F32), pltpu.VMEM((tb, tc), F32), pltpu.VMEM((tb, tc), F32)],
        compiler_params=_params("parallel", "arbitrary"),
    )(a, b)


def _lru_out_ew(hs, z, name):
    T, R = hs.shape
    tm = _tile(T, TM_EW, 16)

    def body(h_ref, g_ref, y_ref):
        y_ref[...] = (h_ref[...] * _gelu_parts(g_ref[...])[0]).astype(BF)

    return pl.pallas_call(
        body, name=name, grid=(T // tm,), in_specs=[_row_spec(tm, R), _row_spec(tm, R)], out_specs=_row_spec(tm, R),
        out_shape=jax.ShapeDtypeStruct((T, R), BF), compiler_params=_params("parallel"),
    )(hs, z)


def _lru_bwd_ew(dy, hs, z, name):
    T, R = hs.shape
    tm = _tile(T, TM_EW, 16)

    def body(dy_ref, h_ref, g_ref, dhd_ref, dgb_ref):
        g, dg = _gelu_parts(g_ref[...])
        d = dy_ref[...]
        dhd_ref[...] = d * g
        dgb_ref[...] = (d * h_ref[...] * dg).astype(BF)

    return pl.pallas_call(
        body, name=name, grid=(T // tm,), in_specs=[_row_spec(tm, R)] * 3, out_specs=[_row_spec(tm, R)] * 2,
        out_shape=[jax.ShapeDtypeStruct((T, R), F32), jax.ShapeDtypeStruct((T, R), BF)],
        compiler_params=_params("parallel"),
    )(dy, hs, z)


def _lru_gates_bwd(lam, hs, r, ig, xc, wa, wx, a_param, name):
    T, R = lam.shape
    tb = _tile(T, TB_SEQ, HALO)
    per = tb // HALO
    nt = T // tb

    def body(l_ref, h_ref, hh_ref, r_ref, ig_ref, xc_ref, wa_ref, wx_ref, ap_ref,
             dxc_ref, dpa_ref, dpx_ref, dsp_ref, dba_ref, dbx_ref, ext):
        i = pl.program_id(0)

        @pl.when(i == 0)
        def _():
            dsp_ref[...] = jnp.zeros_like(dsp_ref)
            dba_ref[...] = jnp.zeros_like(dba_ref)
            dbx_ref[...] = jnp.zeros_like(dbx_ref)

        ext[pl.ds(0, HALO), :] = jnp.where(i > 0, hh_ref[...], 0.0)
        ext[pl.ds(HALO, tb), :] = h_ref[...]
        h_prev = ext[pl.ds(HALO - 1, tb), :]
        lam_v, rv, igv, xcv = l_ref[...], r_ref[...], ig_ref[...], xc_ref[...]
        sp = _softplus_neg(ap_ref[...])
        la = -LRU_C * rv * sp
        av = jnp.exp(la)
        mult = jnp.sqrt(_neg_expm1(2.0 * la))
        dla = lam_v * h_prev * av - lam_v * (igv * xcv) * (av * av) / mult
        du = lam_v * mult
        dpa = (dla * (-LRU_C) * sp) * rv * (1.0 - rv)
        dpx = (du * xcv) * igv * (1.0 - igv)
        dsp_ref[...] += jnp.sum(dla * (-LRU_C) * rv, axis=0, keepdims=True)
        dba_ref[...] += jnp.sum(dpa, axis=0, keepdims=True)
        dbx_ref[...] += jnp.sum(dpx, axis=0, keepdims=True)
        dpab, dpxb = dpa.astype(BF), dpx.astype(BF)
        dxc_ref[...] = du * igv + _dot_nt(dpab, wa_ref[...]) + _dot_nt(dpxb, wx_ref[...])
        dpa_ref[...] = dpab
        dpx_ref[...] = dpxb

        @pl.when(i == nt - 1)
        def _():
            dsp_ref[...] = dsp_ref[...] * (-_sigmoid(-ap_ref[...]))

    tile = _row_spec(tb, R)
    halo = pl.BlockSpec((HALO, R), lambda i: (jnp.maximum(i * per - 1, 0), 0))
    return pl.pallas_call(
        body, name=name, grid=(T // tb,),
        in_specs=[tile, tile, halo, tile, tile, tile, _vec_spec(R, R), _vec_spec(R, R), _vec_spec(R)],
        out_specs=[tile, tile, tile, _vec_spec(R), _vec_spec(R), _vec_spec(R)],
        out_shape=[jax.ShapeDtypeStruct((T, R), F32), jax.ShapeDtypeStruct((T, R), BF), jax.ShapeDtypeStruct((T, R), BF)]
        + [jax.ShapeDtypeStruct((1, R), F32)] * 3,
        scratch_shapes=[pltpu.VMEM((HALO + tb, R), F32)],
        compiler_params=_params("arbitrary"),
    )(lam, hs, hs, r, ig, xc, wa, wx, a_param.reshape(1, R))


def _lru_conv_bwd(dxc, z, conv_w, name):
    T, R = dxc.shape
    tb = _tile(T, TB_SEQ, HALO)
    per = tb // HALO
    nt = T // tb

    def body(d_ref, dn_ref, x_ref, xp_ref, cw_ref, dxb_ref, dcw_ref, dcb_ref, dext, xext):
        i = pl.program_id(0)

        @pl.when(i == 0)
        def _():
            dcw_ref[...] = jnp.zeros_like(dcw_ref)
            dcb_ref[...] = jnp.zeros_like(dcb_ref)

        d = d_ref[...]
        dext[pl.ds(0, tb), :] = d
        dext[pl.ds(tb, HALO), :] = jnp.where(i < nt - 1, dn_ref[...], 0.0)
        xext[pl.ds(0, HALO), :] = jnp.where(i > 0, xp_ref[...], 0.0)
        xext[pl.ds(HALO, tb), :] = x_ref[...]
        dxb = cw_ref[CONV_WIDTH - 1:CONV_WIDTH, :] * d
        for k in range(CONV_WIDTH - 1):
            dxb = dxb + cw_ref[k:k + 1, :] * dext[pl.ds(CONV_WIDTH - 1 - k, tb), :]
        dxb_ref[...] = dxb.astype(BF)
        for k in range(CONV_WIDTH):
            dcw_ref[k:k + 1, :] += jnp.sum(d * xext[pl.ds(HALO - 3 + k, tb), :], axis=0, keepdims=True)
        dcb_ref[...] += jnp.sum(d, axis=0, keepdims=True)

    tile = _row_spec(tb, R)
    nxt = pl.BlockSpec((HALO, R), lambda i: (jnp.minimum((i + 1) * per, T // HALO - 1), 0))
    xtile = pl.BlockSpec((tb, R), lambda i: (i, 1))
    xprev = pl.BlockSpec((HALO, R), lambda i: (jnp.maximum(i * per - 1, 0), 1))
    return pl.pallas_call(
        body, name=name, grid=(nt,), in_specs=[tile, nxt, xtile, xprev, _vec_spec(R, CONV_WIDTH)],
        out_specs=[tile, _vec_spec(R, SUBLANES), _vec_spec(R)],
        out_shape=[jax.ShapeDtypeStruct((T, R), BF), jax.ShapeDtypeStruct((SUBLANES, R), F32),
                   jax.ShapeDtypeStruct((1, R), F32)],
        scratch_shapes=[pltpu.VMEM((tb + HALO, R), F32), pltpu.VMEM((HALO + tb, R), F32)],
        compiler_params=_params("arbitrary"),
    )(dxc, dxc, z, z, conv_w)


def _window_sums(e, n, back):
    out, s = [], e
    for k in (1, 2, 4, 8):
        s = s + pltpu.roll(s, k if back else n - k, 0)
        out.append(s)
    return out


def _pool_fwd(hn, h, w, b, scale, name):
    T, D = hn.shape
    G = len(POOL_WINDOWS)
    gd = D // G
    tb = _tile(T, TB_SEQ, HALO)
    per = tb // HALO

    def body(x_ref, xp_ref, h_ref, w_ref, b_ref, s_ref, o_ref, u_ref, yb_ref):
        i = pl.program_id(0)
        t = i * tb + lax.broadcasted_iota(jnp.int32, (tb, gd), 0) + 1
        for g, win in enumerate(POOL_WINDOWS):
            cols = pl.ds(g * gd, gd)
            x = x_ref[:, cols]
            e = jnp.concatenate([jnp.where(i > 0, xp_ref[:, cols], 0.0), x], axis=0)
            sw = _window_sums(e, HALO + tb, True)[g][HALO:, :]
            u = (sw / jnp.minimum(t, win).astype(F32) - x).astype(BF)
            yb = _dot(u, w_ref[g]) + b_ref[:, cols]
            u_ref[:, cols] = u
            yb_ref[:, cols] = yb
            o_ref[:, cols] = h_ref[:, cols] + yb * s_ref[:, cols]

    tile = _row_spec(tb, D)
    prev = pl.BlockSpec((HALO, D), lambda i: (jnp.maximum(i * per - 1, 0), 0))
    return pl.pallas_call(
        body, name=name, grid=(T // tb,),
        in_specs=[tile, prev, tile, pl.BlockSpec((G, gd, gd), lambda i: (0, 0, 0)), _vec_spec(D), _vec_spec(D)],
        out_specs=[tile, tile, tile],
        out_shape=[jax.ShapeDtypeStruct((T, D), F32), jax.ShapeDtypeStruct((T, D), BF), jax.ShapeDtypeStruct((T, D), F32)],
        compiler_params=_params("parallel"),
    )(hn, hn, h, w, b.reshape(1, D), scale.reshape(1, D))


def _pool_bwd(dm, u, yb, w, scale, name):
    T, D = dm.shape
    G = len(POOL_WINDOWS)
    gd = D // G
    tb = _tile(T, TB_SEQ, HALO)

    def body(d_ref, u_ref, yb_ref, w_ref, s_ref, du_ref, v_ref, dw_ref, db_ref, ds_ref):
        i = pl.program_id(0)

        @pl.when(i == 0)
        def _():
            dw_ref[...] = jnp.zeros_like(dw_ref)
            db_ref[...] = jnp.zeros_like(db_ref)
            ds_ref[...] = jnp.zeros_like(ds_ref)

        d, sc = d_ref[...], s_ref[...]
        ds_ref[...] += jnp.sum(d * yb_ref[...], axis=0, keepdims=True)
        db_ref[...] += jnp.sum(d * sc, axis=0, keepdims=True)
        t = i * tb + lax.broadcasted_iota(jnp.int32, (tb, gd), 0) + 1
        for g, win in enumerate(POOL_WINDOWS):
            cols = pl.ds(g * gd, gd)
            dy = (d_ref[:, cols] * s_ref[:, cols]).astype(BF)
            du = _dot_nt(dy, w_ref[g])
            dw_ref[g] += _dot_tn(u_ref[:, cols], dy)
            du_ref[:, cols] = du
            v_ref[:, cols] = du / jnp.minimum(t, win).astype(F32)

    tile = _row_spec(tb, D)
    return pl.pallas_call(
        body, name=name, grid=(T // tb,),
        in_specs=[tile, tile, tile, pl.BlockSpec((G, gd, gd), lambda i: (0, 0, 0)), _vec_spec(D)],
        out_specs=[tile, tile, pl.BlockSpec((G, gd, gd), lambda i: (0, 0, 0)), _vec_spec(D), _vec_spec(D)],
        out_shape=[jax.ShapeDtypeStruct((T, D), F32), jax.ShapeDtypeStruct((T, D), F32),
                   jax.ShapeDtypeStruct((G, gd, gd), F32), jax.ShapeDtypeStruct((1, D), F32),
                   jax.ShapeDtypeStruct((1, D), F32)],
        compiler_params=_params("arbitrary"),
    )(dm, u, yb, w, scale.reshape(1, D))


def _pool_bwd_win(v, du, name):
    T, D = v.shape
    G = len(POOL_WINDOWS)
    gd = D // G
    tb = _tile(T, TB_SEQ, HALO)
    per = tb // HALO
    nt = T // tb

    def body(v_ref, vn_ref, du_ref, o_ref):
        i = pl.program_id(0)
        for g in range(G):
            cols = pl.ds(g * gd, gd)
            e = jnp.concatenate([v_ref[:, cols], jnp.where(i < nt - 1, vn_ref[:, cols], 0.0)], axis=0)
            o_ref[:, cols] = _window_sums(e, tb + HALO, False)[g][:tb, :] - du_ref[:, cols]

    tile = _row_spec(tb, D)
    nxt = pl.BlockSpec((HALO, D), lambda i: (jnp.minimum((i + 1) * per, T // HALO - 1), 0))
    return pl.pallas_call(
        body, name=name, grid=(nt,), in_specs=[tile, nxt, tile], out_specs=tile,
        out_shape=jax.ShapeDtypeStruct((T, D), F32), compiler_params=_params("parallel"),
    )(v, v, du)


def _adamw(w, g, m, v, name):
    shape = w.shape
    cols = shape[-1] if w.ndim > 1 else shape[0]
    rows = w.size // cols
    tr = _tile(rows, TR_ADAM, SUBLANES)
    c1, c2 = 1.0 / (1.0 - ADAM_B1 ** ADAM_STEP), 1.0 / (1.0 - ADAM_B2 ** ADAM_STEP)

    def body(w_ref, g_ref, m_ref, v_ref, d_ref, mo_ref, vo_ref):
        gv = g_ref[...]
        mn = ADAM_B1 * m_ref[...] + (1.0 - ADAM_B1) * gv
        vn = ADAM_B2 * v_ref[...] + (1.0 - ADAM_B2) * (gv * gv)
        d_ref[...] = -ADAM_LR * ((mn * c1) / (jnp.sqrt(vn * c2) + ADAM_EPS) + ADAM_WD * w_ref[...])
        mo_ref[...] = mn
        vo_ref[...] = vn

    spec = _row_spec(tr, cols)
    outs = pl.pallas_call(
        body, name=name, grid=(rows // tr,), in_specs=[spec] * 4, out_specs=[spec] * 3,
        out_shape=[jax.ShapeDtypeStruct((rows, cols), F32)] * 3, compiler_params=_params("parallel"),
    )(*[t.reshape(rows, cols) for t in (w, g, m, v)])
    return [o.reshape(shape) for o in outs]


def _sum_devices(parts, name):
    n, rows, cols = parts.shape
    tr = _tile(rows, 1024, SUBLANES)

    def body(p_ref, o_ref):
        acc = p_ref[0]
        for k in range(1, n):
            acc = acc + p_ref[k]
        o_ref[...] = acc

    return pl.pallas_call(
        body, name=name, grid=(rows // tr,), in_specs=[pl.BlockSpec((n, tr, cols), lambda i: (0, i, 0))],
        out_specs=_row_spec(tr, cols), out_shape=jax.ShapeDtypeStruct((rows, cols), F32),
        compiler_params=_params("parallel"),
    )(parts)


def _position():
    return lax.axis_index("x"), lax.axis_index("y"), lax.axis_index("c")


def _gathered_shapes(blocks):
    return [jax.ShapeDtypeStruct((b.shape[0], N_DEV * b.shape[1], b.shape[2]), b.dtype) for b in blocks]


def _gather_sems(ng):
    return [pltpu.SemaphoreType.DMA((ng, 7)), pltpu.SemaphoreType.DMA((ng, 7)), pltpu.SemaphoreType.DMA((ng,))] if ng else []


def _gather_plan(blocks, srcs, outs, send_sems, recv_sems, local_sems):
    ng = len(blocks)
    x, y, c = _position()
    me, sibling = (x, y, c), (x, y, 1 - c)
    chips = [(1 - x, y), (x, 1 - y), (1 - x, 1 - y)]

    def rows(g, px, py, pc):
        r = blocks[g].shape[1]
        return outs[g].at[:, pl.ds((4 * px + 2 * py + pc) * r, r), :]

    def copy(g, k, block, to, src=None):
        return pltpu.make_async_remote_copy(
            src_ref=rows(g, *block) if src is None else src, dst_ref=rows(g, *block),
            send_sem=send_sems.at[g, k], recv_sem=recv_sems.at[g, k], device_id=to, device_id_type=MESH)

    def mine(g):
        return pltpu.make_async_copy(srcs[g], rows(g, *me), local_sems.at[g])

    def first(g):
        return [copy(g, 0, me, sibling, src=srcs[g])] + [copy(g, 1 + j, me, (*chip, c), src=srcs[g])
                                                         for j, chip in enumerate(chips)]

    def passed(g):
        return [copy(g, 4 + j, (*chip, c), sibling) for j, chip in enumerate(chips)]

    def start():
        for g in range(ng):
            mine(g).start()
            for cp in first(g):
                cp.start()

    def forward():
        for j, chip in enumerate(chips):
            for g in range(ng):
                copy(g, 1 + j, (*chip, c), me).wait_recv()
                copy(g, 4 + j, (*chip, c), sibling).start()

    def finish():
        for g in range(ng):
            copy(g, 0, sibling, me).wait_recv()
            for j, chip in enumerate(chips):
                copy(g, 4 + j, (*chip, 1 - c), me).wait_recv()
            for cp in first(g) + passed(g):
                cp.wait_send()
            mine(g).wait()

    return start, forward, finish


def _all_gather(blocks, name):
    ng = len(blocks)

    def body(*refs):
        start, forward, finish = _gather_plan(blocks, refs[:ng], refs[ng:2 * ng], *refs[2 * ng:])
        start()
        forward()
        finish()

    hbm = pl.BlockSpec(memory_space=pl.ANY)
    return pl.pallas_call(
        body, name=name, in_specs=[hbm] * ng, out_specs=[hbm] * ng, out_shape=_gathered_shapes(blocks),
        scratch_shapes=_gather_sems(ng),
    )(*blocks)


FLIPS = ((0, 0, 1), (1, 0, 0), (0, 1, 0), (1, 1, 0), (1, 0, 1), (0, 1, 1), (1, 1, 1))


def _piece_rows(piece):
    arr, m = piece
    return arr.shape[0] if m is None else 1


def _scattered_shapes(pieces):
    return [jax.ShapeDtypeStruct((N_DEV, _piece_rows(p), p[0].shape[1] // N_DEV, p[0].shape[2]), p[0].dtype)
            for p in pieces]


def _scatter_sems(ng):
    n = len(FLIPS)
    return [pltpu.SemaphoreType.DMA((ng, n)), pltpu.SemaphoreType.DMA((ng, n)), pltpu.SemaphoreType.DMA((ng,))] if ng else []


def _scatter_plan(pieces, srcs, outs, send_sems, recv_sems, local_sems):
    x, y, c = _position()

    def block(g, tx, ty, tc):
        arr, m = pieces[g]
        r = arr.shape[1] // N_DEV
        lead = slice(None) if m is None else pl.ds(m, 1)
        return srcs[g].at[lead, pl.ds((4 * tx + 2 * ty + tc) * r, r), :]

    def copies(g):
        out = []
        for k, (fx, fy, fc) in enumerate(FLIPS):
            tx, ty, tc = (1 - x if fx else x), (1 - y if fy else y), (1 - c if fc else c)
            out.append(pltpu.make_async_remote_copy(
                src_ref=block(g, tx, ty, tc), dst_ref=outs[g].at[k], send_sem=send_sems.at[g, k],
                recv_sem=recv_sems.at[g, k], device_id=(tx, ty, tc), device_id_type=MESH))
        return out

    def mine(g):
        return pltpu.make_async_copy(block(g, x, y, c), outs[g].at[len(FLIPS)], local_sems.at[g])

    def start():
        for g in range(len(pieces)):
            mine(g).start()
            for cp in copies(g):
                cp.start()

    def finish():
        for g in range(len(pieces)):
            for cp in copies(g):
                cp.wait()
            mine(g).wait()

    return start, finish


def _scatter(pieces, name):
    ng = len(pieces)

    def body(*refs):
        start, finish = _scatter_plan(pieces, refs[:ng], refs[ng:2 * ng], *refs[2 * ng:])
        start()
        finish()

    hbm = pl.BlockSpec(memory_space=pl.ANY)
    return pl.pallas_call(
        body, name=name, in_specs=[hbm] * ng, out_specs=[hbm] * ng, out_shape=_scattered_shapes(pieces),
        scratch_shapes=_scatter_sems(ng),
    )(*[p[0] for p in pieces])


def _scatter_sum(recv, name):
    _, n, r, c = recv.shape

    def body(r_ref, o_ref):
        acc = r_ref[len(FLIPS)].astype(F32)
        for k in range(len(FLIPS)):
            acc = acc + r_ref[k].astype(F32)
        o_ref[...] = acc

    return pl.pallas_call(
        body, name=name, grid=(n,), in_specs=[pl.BlockSpec((N_DEV, None, r, c), lambda i: (0, i, 0, 0))],
        out_specs=pl.BlockSpec((None, r, c), lambda i: (i, 0, 0)),
        out_shape=jax.ShapeDtypeStruct((n, r, c), F32), compiler_params=_params("parallel"),
    )(recv)


def _block_diag(w):
    H, d, _ = w.shape
    return (jnp.eye(H, dtype=w.dtype)[:, None, :, None] * w[:, :, None, :]).reshape(H * d, H * d)


def _diag_blocks(dense, H):
    d = dense.shape[0] // H
    return jnp.stack([dense[i * d:(i + 1) * d, i * d:(i + 1) * d] for i in range(H)])


def _local_step(x, p, tgt, W, blocks=None):
    dist = blocks is not None
    L = p.shape[0]
    W = dict(W)

    def gathering(keys):
        return [k for k in keys if k not in W] if dist else []

    def ffn_fwd(h, g, i, f, during_act, during_out):
        w = W[("ffn", i, f)]
        keys = gathering(during_act)
        a, b, s, got = _ffn_fwd_act(h, g, w, f"ffn{f}_fwd_act_{i}", gather=[blocks[k] for k in keys])
        W.update(zip(keys, got))
        keys = gathering(during_out)
        h, got = _ffn_fwd_out(s, w, h, f"ffn{f}_fwd_out_{i}", gather=[blocks[k] for k in keys])
        W.update(zip(keys, got))
        return a, b, s, h

    saved = []
    h = x
    for i in range(L):
        j = i // 2
        lru = i % 2 == 0
        s = {"h0": h}
        mixer = [("lru_in", j), ("lru_out", j)] if lru else [("pool_w", j)]
        s["a1"], s["b1"], s["s1"], h = ffn_fwd(h, W["ffn1_norm"][i], i, 1, [("ffn", i, 2)], mixer)
        s["h1"] = h
        if lru:
            hn = _rms_fwd(h, W["mix_norm"][i], BF, f"mix_norm_{i}")
            z = _mm(hn, W[("lru_in", j)][0], "nt", f"lru_in_{i}")
            wa, wx = _block_diag(W["lru_w_a"][j]).astype(BF), _block_diag(W["lru_w_x"][j]).astype(BF)
            xc, r, ig, a, bt = _lru_gates(z, W["lru_conv_w"][j], W["lru_conv_b"][j], wa, wx, W["lru_b_a"][j],
                                          W["lru_b_x"][j], W["lru_a_param"][j], f"lru_gates_{i}")
            hs = _lru_scan(a, bt, False, f"lru_scan_{i}")
            y = _lru_out_ew(hs, z, f"lru_out_ew_{i}")
            h = _mm(y, W[("lru_out", j)][0], "nn", f"lru_out_{i}", res=h)
            s.update(hn=hn, z=z, wa=wa, wx=wx, xc=xc, r=r, ig=ig, a=a, hs=hs, y=y)
        else:
            hn = _rms_fwd(h, W["mix_norm"][i], F32, f"mix_norm_{i}")
            h, s["u"], s["yb"] = _pool_fwd(hn, h, W[("pool_w", j)], W["pool_b"][j], W["pool_scale"][j],
                                           f"pool_fwd_{i}")
        s["h2"] = h
        s["a2"], s["b2"], s["s2"], h = ffn_fwd(h, W["ffn2_norm"][i], i, 2, [("ffn", i + 1, 1)] if i + 1 < L else [],
                                               [("ple_gate", i), ("ple_proj", i)])
        s["h3"] = h
        s["n4"] = _rms_fwd(h, W["ple_norm"][i], BF, f"ple_norm_{i}")
        s["zp"] = _mm(s["n4"], W[("ple_gate", i)][0], "nn", f"ple_gate_{i}")
        s["pp"] = _mm(p[i], W[("ple_proj", i)][0], "nt", f"ple_proj_{i}")
        h = _ple_fwd_ew(h, s["zp"], s["pp"], f"ple_fwd_ew_{i}")
        saved.append(s)

    loss, dh, d_final = _loss_head(h, W["final_norm"], tgt)

    big, recv = {}, {}
    n_lru, n_pool = L // 2 + L % 2, L // 2
    small = {k: [None] * L for k in ("ffn1_norm", "mix_norm", "ffn2_norm", "ple_norm")}
    for k in ("lru_conv_w", "lru_conv_b", "lru_w_a", "lru_b_a", "lru_w_x", "lru_b_x", "lru_a_param"):
        small[k] = [None] * n_lru
    for k in ("pool_b", "pool_scale"):
        small[k] = [None] * n_pool

    def scattering(pieces):
        return [(k, m) for k, m in pieces if k in big] if dist else []

    def ffn_bwd(dh, h_in, g, a, b, sv, i, f, during):
        w = W[("ffn", i, f)]
        out = [scattering(d) for d in during]
        sent = [[(big[k], m) for k, m in o] for o in out]
        da, db, dhb, got0 = _ffn_bwd_act(dh, a, b, w, f"ffn{f}_bwd_act_{i}", scatter=sent[0])
        dh, dg, got1 = _ffn_bwd_in(da, db, w, h_in, g, dh, f"ffn{f}_bwd_in_{i}", scatter=sent[1])
        dw, got2 = _ffn_bwd_w(da, db, sv, h_in, g, dhb, f"ffn{f}_dw_{i}", scatter=sent[2])
        for o, got in zip(out, (got0, got1, got2)):
            recv.update(zip(o, got))
        return dh, dg, dw

    for i in reversed(range(L)):
        j = i // 2
        lru = i % 2 == 0
        s = saved[i]
        dz, dpp = _ple_bwd_ew(dh, s["zp"], s["pp"], f"ple_bwd_ew_{i}")
        big[("ple_gate", i)] = _mm(s["n4"], dz, "tn", f"ple_gate_dw_{i}", out_dtype=BF)[None]
        big[("ple_proj", i)] = _mm(dpp, p[i], "tn", f"ple_proj_dw_{i}", out_dtype=BF)[None]
        dn4 = _mm(dz, W[("ple_gate", i)][0], "nt", f"ple_gate_dx_{i}")
        dh, dg = _rms_bwd(s["h3"], W["ple_norm"][i], dn4, dh, f"ple_norm_bwd_{i}")
        small["ple_norm"][i] = dg[0]
        above = ("ffn", i + 1, 1)
        dh, dg, big[("ffn", i, 2)] = ffn_bwd(dh, s["h2"], W["ffn2_norm"][i], s["a2"], s["b2"], s["s2"], i, 2, [
            [(above, 0)], [(above, 1)], [(above, 2), (("ple_gate", i), None), (("ple_proj", i), None)]])
        small["ffn2_norm"][i] = dg[0]
        if lru:
            R = W[("lru_out", j)].shape[1]
            big[("lru_out", j)] = _mm(s["y"], dh, "tn", f"lru_out_dw_{i}", out_dtype=BF)[None]
            dy = _mm(dh, W[("lru_out", j)][0], "nt", f"lru_out_dx_{i}")
            dhd, dgb = _lru_bwd_ew(dy, s["hs"], s["z"], f"lru_bwd_ew_{i}")
            lam = _lru_scan(s["a"], dhd, True, f"lru_scan_bwd_{i}")
            dxc, dpa, dpx, dsp, dba, dbx = _lru_gates_bwd(lam, s["hs"], s["r"], s["ig"], s["xc"], s["wa"], s["wx"],
                                                          W["lru_a_param"][j], f"lru_gates_bwd_{i}")
            small["lru_a_param"][j], small["lru_b_a"][j], small["lru_b_x"][j] = dsp[0], dba[0], dbx[0]
            small["lru_w_a"][j] = _diag_blocks(_mm(s["xc"], dpa, "tn", f"lru_wa_dw_{i}"), LRU_HEADS)
            small["lru_w_x"][j] = _diag_blocks(_mm(s["xc"], dpx, "tn", f"lru_wx_dw_{i}"), LRU_HEADS)
            dxb, dcw, dcb = _lru_conv_bwd(dxc, s["z"], W["lru_conv_w"][j], f"lru_conv_bwd_{i}")
            small["lru_conv_w"][j], small["lru_conv_b"][j] = dcw[:CONV_WIDTH], dcb[0]
            win = W[("lru_in", j)][0]
            big[("lru_in", j)] = jnp.concatenate([_mm(dgb, s["hn"], "tn", f"lru_in_dw_g_{i}", out_dtype=BF),
                                                  _mm(dxb, s["hn"], "tn", f"lru_in_dw_x_{i}", out_dtype=BF)])[None]
            dhn = _mm(dgb, win[:R], "nn", f"lru_in_dx_g_{i}")
            dhn = _mm(dxb, win[R:], "nn", f"lru_in_dx_x_{i}", res=dhn)
            mixer = [("lru_in", j), ("lru_out", j)]
        else:
            du, v, dw, dbp, dsc = _pool_bwd(dh, s["u"], s["yb"], W[("pool_w", j)], W["pool_scale"][j], f"pool_bwd_{i}")
            big[("pool_w", j)] = dw.astype(BF)
            small["pool_b"][j], small["pool_scale"][j] = dbp[0], dsc[0]
            dhn = _pool_bwd_win(v, du, f"pool_bwd_win_{i}")
            mixer = [("pool_w", j)]
        dh, dg = _rms_bwd(s["h1"], W["mix_norm"][i], dhn, dh, f"mix_norm_bwd_{i}")
        small["mix_norm"][i] = dg[0]
        second = ("ffn", i, 2)
        dh, dg, big[("ffn", i, 1)] = ffn_bwd(dh, s["h0"], W["ffn1_norm"][i], s["a1"], s["b1"], s["s1"], i, 1, [
            [(second, 0)], [(second, 1)], [(second, 2)] + [(k, None) for k in mixer]])
        small["ffn1_norm"][i] = dg[0]

    small = {k: jnp.stack(v) for k, v in small.items()}
    small["final_norm"] = d_final[0]
    if dist:
        last = [(k, None) for k in big if (k, None) not in recv and (k, 0) not in recv]
        recv.update(zip(last, _scatter([(big[k], m) for k, m in last], "scatter_last")))

        def total(k):
            tag = "sum_" + "_".join(map(str, k))
            if (k, None) in recv:
                return _scatter_sum(recv[(k, None)], tag)
            return jnp.concatenate([_scatter_sum(recv[(k, m)], f"{tag}_{m}") for m in range(big[k].shape[0])])

        big = {k: total(k) for k in big}
    return loss, dh, big, small


SMALL_SHARDED = ("pool_b", "pool_scale", "lru_conv_w")
SMALL = ("ffn1_norm", "mix_norm", "ffn2_norm", "ple_norm", "final_norm", "lru_conv_b", "lru_w_a", "lru_b_a",
         "lru_w_x", "lru_b_x", "lru_a_param", "pool_b", "pool_scale", "lru_conv_w")


def _pack_big(w):
    t = lambda a: jnp.swapaxes(a, -1, -2)
    out = {}
    for i in range(w["ffn1_norm"].shape[0]):
        for f in (1, 2):
            out[("ffn", i, f)] = jnp.stack([t(w[f"ffn{f}_w_gate"][i]), t(w[f"ffn{f}_w_up"][i]), w[f"ffn{f}_w_down"][i]])
        out[("ple_gate", i)], out[("ple_proj", i)] = w["ple_w_gate"][i][None], t(w["ple_w_proj"][i])[None]
    for j in range(w["lru_w_in"].shape[0]):
        out[("lru_in", j)], out[("lru_out", j)] = t(w["lru_w_in"][j])[None], w["lru_w_out"][j][None]
    for j in range(w["pool_w"].shape[0]):
        out[("pool_w", j)] = w["pool_w"][j]
    return out


def _unpack_big(b, L):
    t = lambda a: jnp.swapaxes(a, -1, -2)
    n_lru, n_pool = L // 2 + L % 2, L // 2
    out = {"lru_w_in": jnp.stack([t(b[("lru_in", j)][0]) for j in range(n_lru)]),
           "lru_w_out": jnp.stack([b[("lru_out", j)][0] for j in range(n_lru)]),
           "pool_w": jnp.stack([b[("pool_w", j)] for j in range(n_pool)]),
           "ple_w_gate": jnp.stack([b[("ple_gate", i)][0] for i in range(L)]),
           "ple_w_proj": jnp.stack([t(b[("ple_proj", i)][0]) for i in range(L)])}
    for f in (1, 2):
        out[f"ffn{f}_w_gate"] = jnp.stack([t(b[("ffn", i, f)][0]) for i in range(L)])
        out[f"ffn{f}_w_up"] = jnp.stack([t(b[("ffn", i, f)][1]) for i in range(L)])
        out[f"ffn{f}_w_down"] = jnp.stack([b[("ffn", i, f)][2] for i in range(L)])
    return out


def _flatten(parts, names, rows_of=LANES):
    flat = jnp.concatenate([parts[k].reshape(-1) for k in names])
    pad = (-flat.size) % (16 * rows_of)
    return jnp.pad(flat, (0, pad)).reshape(1, -1, rows_of)


def _unflatten(flat, like, names):
    out, o = {}, 0
    flat = flat.reshape(-1)
    for k in names:
        n = like[k].size
        out[k] = flat[o:o + n].reshape(like[k].shape)
        o += n
    return out


def kernel(x, p, ffn1_norm, ffn1_w_gate, ffn1_w_up, ffn1_w_down, mix_norm, lru_w_in, lru_conv_w, lru_conv_b, lru_w_a, lru_b_a, lru_w_x, lru_b_x, lru_a_param, lru_w_out, pool_w, pool_b, pool_scale, ffn2_norm, ffn2_w_gate, ffn2_w_up, ffn2_w_down, ple_norm, ple_w_gate, ple_w_proj, final_norm, loss_target, m_ffn1_norm, m_ffn1_w_gate, m_ffn1_w_up, m_ffn1_w_down, m_mix_norm, m_lru_w_in, m_lru_conv_w, m_lru_conv_b, m_lru_w_a, m_lru_b_a, m_lru_w_x, m_lru_b_x, m_lru_a_param, m_lru_w_out, m_pool_w, m_pool_b, m_pool_scale, m_ffn2_norm, m_ffn2_w_gate, m_ffn2_w_up, m_ffn2_w_down, m_ple_norm, m_ple_w_gate, m_ple_w_proj, m_final_norm, v_ffn1_norm, v_ffn1_w_gate, v_ffn1_w_up, v_ffn1_w_down, v_mix_norm, v_lru_w_in, v_lru_conv_w, v_lru_conv_b, v_lru_w_a, v_lru_b_a, v_lru_w_x, v_lru_b_x, v_lru_a_param, v_lru_w_out, v_pool_w, v_pool_b, v_pool_scale, v_ffn2_norm, v_ffn2_w_gate, v_ffn2_w_up, v_ffn2_w_down, v_ple_norm, v_ple_w_gate, v_ple_w_proj, v_final_norm):
    names = ["ffn1_norm", "ffn1_w_gate", "ffn1_w_up", "ffn1_w_down", "mix_norm", "lru_w_in", "lru_conv_w", "lru_conv_b",
             "lru_w_a", "lru_b_a", "lru_w_x", "lru_b_x", "lru_a_param", "lru_w_out", "pool_w", "pool_b", "pool_scale",
             "ffn2_norm", "ffn2_w_gate", "ffn2_w_up", "ffn2_w_down", "ple_norm", "ple_w_gate", "ple_w_proj", "final_norm"]
    w = dict(zip(names, [ffn1_norm, ffn1_w_gate, ffn1_w_up, ffn1_w_down, mix_norm, lru_w_in, lru_conv_w, lru_conv_b, lru_w_a, lru_b_a, lru_w_x, lru_b_x, lru_a_param, lru_w_out, pool_w, pool_b, pool_scale, ffn2_norm, ffn2_w_gate, ffn2_w_up, ffn2_w_down, ple_norm, ple_w_gate, ple_w_proj, final_norm]))
    m = dict(zip(names, [m_ffn1_norm, m_ffn1_w_gate, m_ffn1_w_up, m_ffn1_w_down, m_mix_norm, m_lru_w_in, m_lru_conv_w, m_lru_conv_b, m_lru_w_a, m_lru_b_a, m_lru_w_x, m_lru_b_x, m_lru_a_param, m_lru_w_out, m_pool_w, m_pool_b, m_pool_scale, m_ffn2_norm, m_ffn2_w_gate, m_ffn2_w_up, m_ffn2_w_down, m_ple_norm, m_ple_w_gate, m_ple_w_proj, m_final_norm]))
    v = dict(zip(names, [v_ffn1_norm, v_ffn1_w_gate, v_ffn1_w_up, v_ffn1_w_down, v_mix_norm, v_lru_w_in, v_lru_conv_w, v_lru_conv_b, v_lru_w_a, v_lru_b_a, v_lru_w_x, v_lru_b_x, v_lru_a_param, v_lru_w_out, v_pool_w, v_pool_b, v_pool_scale, v_ffn2_norm, v_ffn2_w_gate, v_ffn2_w_up, v_ffn2_w_down, v_ple_norm, v_ple_w_gate, v_ple_w_proj, v_final_norm]))
    L = p.shape[0]
    px, py, pc = _position()
    me = 4 * px + 2 * py + pc

    blocks = {k: b.astype(BF) for k, b in _pack_big(w).items()}
    first = ("ffn", 0, 1)
    got, small_blocks = _all_gather([blocks[first], _flatten(w, SMALL_SHARDED)], "gather_first")
    W = {first: got}
    per_dev = small_blocks.reshape(N_DEV, -1)
    shards = [_unflatten(per_dev[k], w, SMALL_SHARDED) for k in range(N_DEV)]
    for k in SMALL:
        W[k] = jnp.concatenate([s[k] for s in shards], axis=-1) if k in SMALL_SHARDED else w[k]

    loss, dx, mine, small = _local_step(x[0], p[:, 0], loss_target[0], W, blocks)
    grads = _unpack_big(mine, L)

    parts = _all_gather([_flatten(small, SMALL)], "gather_small_grads")[0]
    total = _sum_devices(parts.reshape(N_DEV, -1, LANES), "sum_small_grads")
    full = _unflatten(total, {k: W[k] for k in SMALL}, SMALL)
    for k in SMALL:
        if k in SMALL_SHARDED:
            n = w[k].shape[-1]
            grads[k] = lax.dynamic_slice_in_dim(full[k], me * n, n, axis=-1)
        else:
            grads[k] = full[k]

    delta, new_m, new_v = {}, {}, {}
    for k in names:
        delta[k], new_m[k], new_v[k] = _adamw(w[k], grads[k], m[k], v[k], f"adamw_{k}")
    total_loss = lax.psum(loss[0, 0], ("x", "y", "c"))
    return (total_loss, dx[None], *[grads[k] for k in names], *[delta[k] for k in names],
            *[new_m[k] for k in names], *[new_v[k] for k in names])
```

```python
import functools

import jax
import jax.numpy as jnp
from jax import lax
from jax.experimental import pallas as pl
from jax.experimental.pallas import tpu as pltpu

F32 = jnp.float32
BF = jnp.bfloat16
MESH = pl.DeviceIdType.MESH

RMS_EPS = 1e-6
LRU_C = 8.0
LRU_HEADS = 16
CONV_WIDTH = 4
POOL_WINDOWS = (2, 4, 8, 16)
ADAM_LR, ADAM_B1, ADAM_B2, ADAM_EPS, ADAM_WD, ADAM_STEP = 0.001, 0.9, 0.999, 1e-08, 0.01, 10

N_DEV = 8
LANES = 128
SUBLANES = 8
GATE_COLS, GATE_SPAN = 256, 512
HALO = 16
VMEM_LIMIT = 56 * 1024 * 1024

TM_FFN = 1024
TM_FFN_ACT = 2048
TM_FFN_IN = 512
TN_FFN_OUT = 512
TF_FFN = 256
TF_FFN_WG = 1408
TK_FFN_WG = 512
TB_SEQ = 256
TB_SCAN = 512
TC_SCAN = 256
TM_EW = 512
TM_MM, TN_MM, TK_MM = 1024, 512, 1024
TR_ADAM = 512


def _tile(n, pref, align):
    if n <= pref:
        return n
    t = (pref // align) * align
    while t >= align:
        if n % t == 0:
            return t
        t -= align
    raise ValueError(f"no tile for {n} (pref {pref}, align {align})")


def _params(*sem):
    return pltpu.CompilerParams(dimension_semantics=sem, vmem_limit_bytes=VMEM_LIMIT)


def _dot(a, b):
    return lax.dot_general(a, b, (((1,), (0,)), ((), ())), preferred_element_type=F32)


def _dot_nt(a, b):
    return lax.dot_general(a, b, (((1,), (1,)), ((), ())), preferred_element_type=F32)


def _dot_tn(a, b):
    return lax.dot_general(a, b, (((0,), (0,)), ((), ())), preferred_element_type=F32)


def _sigmoid(x):
    return 1.0 / (1.0 + jnp.exp(-x))


def _gelu_parts(x):
    k0, k1 = 0.7978845608028654, 0.044715
    t = jnp.tanh(k0 * (x + k1 * x * x * x))
    g = 0.5 * x * (1.0 + t)
    dg = 0.5 * (1.0 + t) + 0.5 * x * (1.0 - t * t) * k0 * (1.0 + 3.0 * k1 * x * x)
    return g, dg


def _neg_expm1(x):
    p = x * (1.0 + x * (0.5 + x * (1.0 / 6 + x * (1.0 / 24 + x * (1.0 / 120 + x * (1.0 / 720 + x * (1.0 / 5040)))))))
    return jnp.where(x > -0.35, -p, 1.0 - jnp.exp(x))


def _softplus_neg(l):
    u = jnp.exp(-jnp.abs(l))
    w = 1.0 + u
    log1p = jnp.where(w == 1.0, u, jnp.log(w) * (u / jnp.where(w == 1.0, 1.0, w - 1.0)))
    return jnp.maximum(-l, 0.0) + log1p


def _rms_parts(x, g):
    r = lax.rsqrt(jnp.mean(x * x, axis=-1, keepdims=True) + RMS_EPS)
    nhat = x * r
    return nhat * g, nhat, r


def _rms_bwd_parts(x, g, dn):
    _, nhat, r = _rms_parts(x, g)
    u = dn * g
    dx = r * (u - nhat * jnp.mean(u * nhat, axis=-1, keepdims=True))
    return dx, jnp.sum(dn * nhat, axis=0, keepdims=True)


def _row_spec(tm, d, single=False):
    if single:
        return pl.BlockSpec((tm, d), lambda i, *_: (i, 0), pipeline_mode=pl.Buffered(1))
    return pl.BlockSpec((tm, d), lambda i, *_: (i, 0))


def _vec_spec(d, rows=1):
    return pl.BlockSpec((rows, d), lambda *_: (0, 0))


def _mm(x, w, mode, name, out_dtype=F32, res=None, alpha=1.0, tm=None, tn=None, tk=None):
    if mode == "nn":
        (M, K), (_, N) = x.shape, w.shape
    elif mode == "nt":
        (M, K), (N, _) = x.shape, w.shape
    else:
        (K, M), (_, N) = x.shape, w.shape
    tm = _tile(M, tm or TM_MM, LANES if mode == "tn" else SUBLANES)
    tn = _tile(N, tn or TN_MM, LANES)
    tk = _tile(K, tk or TK_MM, LANES if mode != "tn" else 16)
    nk = K // tk
    dot = {"nn": _dot, "nt": _dot_nt, "tn": _dot_tn}[mode]

    def body(*refs):
        if res is None:
            x_ref, w_ref, o_ref, acc = refs
        else:
            x_ref, w_ref, r_ref, o_ref, acc = refs
        k = pl.program_id(2)

        @pl.when(k == 0)
        def _():
            acc[...] = jnp.zeros_like(acc)

        acc[...] += dot(x_ref[...].astype(BF), w_ref[...].astype(BF))

        @pl.when(k == nk - 1)
        def _():
            r = acc[...] if alpha == 1.0 else acc[...] * alpha
            if res is not None:
                r = r_ref[...] + r
            o_ref[...] = r.astype(out_dtype)

    if mode == "nn":
        specs = [pl.BlockSpec((tm, tk), lambda i, j, k: (i, k)), pl.BlockSpec((tk, tn), lambda i, j, k: (k, j))]
    elif mode == "nt":
        specs = [pl.BlockSpec((tm, tk), lambda i, j, k: (i, k)), pl.BlockSpec((tn, tk), lambda i, j, k: (j, k))]
    else:
        specs = [pl.BlockSpec((tk, tm), lambda i, j, k: (k, i)), pl.BlockSpec((tk, tn), lambda i, j, k: (k, j))]
    args = [x, w]
    if res is not None:
        specs.append(pl.BlockSpec((tm, tn), lambda i, j, k: (i, j)))
        args.append(res)
    return pl.pallas_call(
        body, name=name, grid=(M // tm, N // tn, nk), in_specs=specs,
        out_specs=pl.BlockSpec((tm, tn), lambda i, j, k: (i, j)),
        out_shape=jax.ShapeDtypeStruct((M, N), out_dtype),
        scratch_shapes=[pltpu.VMEM((tm, tn), F32)],
        compiler_params=_params("parallel", "parallel", "arbitrary"),
    )(*args)


def _rms_fwd(h, g, out_dtype, name):
    T, D = h.shape
    tm = _tile(T, TM_EW, 16)

    def body(h_ref, g_ref, o_ref):
        o_ref[...] = _rms_parts(h_ref[...], g_ref[...])[0].astype(out_dtype)

    return pl.pallas_call(
        body, name=name, grid=(T // tm,), in_specs=[_row_spec(tm, D), _vec_spec(D)], out_specs=_row_spec(tm, D),
        out_shape=jax.ShapeDtypeStruct((T, D), out_dtype), compiler_params=_params("parallel"),
    )(h, g.reshape(1, D))


def _rms_bwd(h, g, dn, dres, name):
    T, D = h.shape
    tm = _tile(T, TM_EW, 16)

    def body(h_ref, g_ref, dn_ref, dr_ref, dh_ref, dg_ref):
        @pl.when(pl.program_id(0) == 0)
        def _():
            dg_ref[...] = jnp.zeros_like(dg_ref)

        dx, dg = _rms_bwd_parts(h_ref[...], g_ref[...], dn_ref[...].astype(F32))
        dh_ref[...] = dr_ref[...] + dx
        dg_ref[...] += dg

    return pl.pallas_call(
        body, name=name, grid=(T // tm,),
        in_specs=[_row_spec(tm, D), _vec_spec(D), _row_spec(tm, D), _row_spec(tm, D)],
        out_specs=[_row_spec(tm, D), _vec_spec(D)],
        out_shape=[jax.ShapeDtypeStruct((T, D), F32), jax.ShapeDtypeStruct((1, D), F32)],
        compiler_params=_params("arbitrary"),
    )(h, g.reshape(1, D), dn, dres)


def _loss_head(h, g, tgt):
    T, D = h.shape
    tm = _tile(T, TM_EW, 16)

    def body(h_ref, g_ref, t_ref, loss_ref, dh_ref, dg_ref):
        @pl.when(pl.program_id(0) == 0)
        def _():
            dg_ref[...] = jnp.zeros_like(dg_ref)
            loss_ref[...] = jnp.zeros_like(loss_ref)

        x, gg = h_ref[...], g_ref[...]
        y = _rms_parts(x, gg)[0]
        e = y - t_ref[...]
        part = jnp.sum(jnp.sum(e * e, axis=0, keepdims=True), axis=1, keepdims=True) * (0.5 / D)
        loss_ref[...] += jnp.broadcast_to(part, loss_ref.shape)
        dx, dg = _rms_bwd_parts(x, gg, e * (1.0 / D))
        dh_ref[...] = dx
        dg_ref[...] += dg

    return pl.pallas_call(
        body, name="loss_head", grid=(T // tm,),
        in_specs=[_row_spec(tm, D), _vec_spec(D), _row_spec(tm, D)],
        out_specs=[_vec_spec(LANES), _row_spec(tm, D), _vec_spec(D)],
        out_shape=[jax.ShapeDtypeStruct((1, LANES), F32), jax.ShapeDtypeStruct((T, D), F32),
                   jax.ShapeDtypeStruct((1, D), F32)],
        compiler_params=_params("arbitrary"),
    )(h, g.reshape(1, D), tgt)


def _carry(plan, first, mid, last):
    pl.when(first)(plan[0])
    if len(plan) == 3:
        pl.when(mid)(plan[1])
    pl.when(last)(plan[-1])


def _ffn_fwd_act(h, g, wffn, name, gather=()):
    T, D = h.shape
    F = wffn.shape[1]
    tm, tf = _tile(T, TM_FFN_ACT, 16), _tile(F, TF_FFN, LANES)
    ni, nf, ng = T // tm, F // tf, len(gather)

    def body(*refs):
        h_ref, g_ref, wg_ref, wu_ref = refs[:4]
        srcs, (a_ref, b_ref, s_ref), outs = refs[4:4 + ng], refs[4 + ng:7 + ng], refs[7 + ng:7 + 2 * ng]
        n_sc = refs[7 + 2 * ng]
        i, j = pl.program_id(0), pl.program_id(1)
        if ng:
            _carry(_gather_plan(gather, srcs, outs, *refs[8 + 2 * ng:]), jnp.logical_and(i == 0, j == 0),
                   jnp.logical_and(i == (3 * ni) // 4, j == 0), jnp.logical_and(i == ni - 1, j == nf - 1))

        @pl.when(j == 0)
        def _():
            n_sc[...] = _rms_parts(h_ref[...], g_ref[...])[0].astype(BF)

        n = n_sc[...]
        a = _dot_nt(n, wg_ref[...])
        b = _dot_nt(n, wu_ref[...])
        a_ref[...] = a.astype(BF)
        b_ref[...] = b.astype(BF)
        s_ref[...] = (a * _sigmoid(a) * b).astype(BF)

    tile = pl.BlockSpec((tm, tf), lambda i, j: (i, j))
    w = [pl.BlockSpec((None, tf, D), functools.partial(lambda k, i, j: (k, j, 0), k)) for k in (0, 1)]
    hbm = pl.BlockSpec(memory_space=pl.ANY)
    outs = pl.pallas_call(
        body, name=name, grid=(ni, nf), in_specs=[_row_spec(tm, D), _vec_spec(D)] + w + [hbm] * ng,
        out_specs=[tile, tile, tile] + [hbm] * ng,
        out_shape=[jax.ShapeDtypeStruct((T, F), BF)] * 3 + _gathered_shapes(gather),
        scratch_shapes=[pltpu.VMEM((tm, D), BF)] + _gather_sems(ng),
        compiler_params=_params("arbitrary", "arbitrary"),
    )(h, g.reshape(1, D), wffn, wffn, *gather)
    return outs[0], outs[1], outs[2], list(outs[3:])


def _ffn_fwd_out(s, wffn, h, name, gather=()):
    T, F = s.shape
    D = h.shape[1]
    tm, tn = _tile(T, TM_FFN, 16), _tile(D, TN_FFN_OUT, LANES)
    ni, nj, ng = T // tm, D // tn, len(gather)

    def body(*refs):
        s_ref, w_ref, h_ref = refs[:3]
        srcs, o_ref, outs = refs[3:3 + ng], refs[3 + ng], refs[4 + ng:4 + 2 * ng]
        i, j = pl.program_id(0), pl.program_id(1)
        if ng:
            _carry(_gather_plan(gather, srcs, outs, *refs[4 + 2 * ng:]), jnp.logical_and(i == 0, j == 0),
                   jnp.logical_and(i == (3 * ni) // 4, j == 0), jnp.logical_and(i == ni - 1, j == nj - 1))
        o_ref[...] = h_ref[...] + 0.5 * _dot(s_ref[...], w_ref[...])

    hbm = pl.BlockSpec(memory_space=pl.ANY)
    tile = pl.BlockSpec((tm, tn), lambda i, j: (i, j))
    outs = pl.pallas_call(
        body, name=name, grid=(ni, nj),
        in_specs=[pl.BlockSpec((tm, F), lambda i, j: (i, 0)), pl.BlockSpec((None, F, tn), lambda i, j: (2, 0, j)), tile]
        + [hbm] * ng,
        out_specs=[tile] + [hbm] * ng, out_shape=[jax.ShapeDtypeStruct((T, D), F32)] + _gathered_shapes(gather),
        scratch_shapes=_gather_sems(ng), compiler_params=_params("arbitrary", "arbitrary"),
    )(s, wffn, h, *gather)
    return outs[0], list(outs[1:])


def _ffn_bwd_act(dh, a, b, wffn, name, scatter=()):
    T, D = dh.shape
    F = wffn.shape[1]
    tm, tf = _tile(T, TM_FFN_ACT, 16), _tile(F, TF_FFN, LANES)
    ni, nf, ng = T // tm, F // tf, len(scatter)

    def body(*refs):
        dh_ref, a_ref, b_ref, wd_ref = refs[:4]
        srcs, (da_ref, db_ref, dhb_ref), outs = refs[4:4 + ng], refs[4 + ng:7 + ng], refs[7 + ng:7 + 2 * ng]
        i, j = pl.program_id(0), pl.program_id(1)
        if ng:
            _carry(_scatter_plan(scatter, srcs, outs, *refs[7 + 2 * ng:]), jnp.logical_and(i == 0, j == 0), None,
                   jnp.logical_and(i == ni - 1, j == nf - 1))

        @pl.when(j == 0)
        def _():
            dhb_ref[...] = dh_ref[...].astype(BF)

        ds = 0.5 * _dot_nt(dhb_ref[...], wd_ref[...])
        av, bv = a_ref[...].astype(F32), b_ref[...].astype(F32)
        sig = _sigmoid(av)
        da_ref[...] = (ds * bv * (sig * (1.0 + av * (1.0 - sig)))).astype(BF)
        db_ref[...] = (ds * (av * sig)).astype(BF)

    tile = pl.BlockSpec((tm, tf), lambda i, j: (i, j))
    hbm = pl.BlockSpec(memory_space=pl.ANY)
    outs = pl.pallas_call(
        body, name=name, grid=(ni, nf),
        in_specs=[_row_spec(tm, D), tile, tile, pl.BlockSpec((None, tf, D), lambda i, j: (2, j, 0))] + [hbm] * ng,
        out_specs=[tile, tile, _row_spec(tm, D)] + [hbm] * ng,
        out_shape=[jax.ShapeDtypeStruct((T, F), BF)] * 2 + [jax.ShapeDtypeStruct((T, D), BF)]
        + _scattered_shapes(scatter),
        scratch_shapes=_scatter_sems(ng), compiler_params=_params("arbitrary", "arbitrary"),
    )(dh, a, b, wffn, *[piece[0] for piece in scatter])
    return outs[0], outs[1], outs[2], list(outs[3:])


def _ffn_bwd_in(da, db, wffn, h, g, dh, name, scatter=()):
    T, F = da.shape
    D = h.shape[1]
    tm = _tile(T, TM_FFN_IN, 16)
    ni, ng = T // tm, len(scatter)

    def body(*refs):
        da_ref, db_ref, wg_ref, wu_ref, h_ref, g_ref, dh_ref = refs[:7]
        srcs, (o_ref, dg_ref), outs = refs[7:7 + ng], refs[7 + ng:9 + ng], refs[9 + ng:9 + 2 * ng]
        i = pl.program_id(0)
        if ng:
            _carry(_scatter_plan(scatter, srcs, outs, *refs[9 + 2 * ng:]), i == 0, None, i == ni - 1)

        @pl.when(i == 0)
        def _():
            dg_ref[...] = jnp.zeros_like(dg_ref)

        dn = _dot(da_ref[...], wg_ref[...]) + _dot(db_ref[...], wu_ref[...])
        dx, dg = _rms_bwd_parts(h_ref[...], g_ref[...], dn)
        o_ref[...] = dh_ref[...] + dx
        dg_ref[...] += dg

    hbm = pl.BlockSpec(memory_space=pl.ANY)
    act = pl.BlockSpec((tm, F), lambda i: (i, 0))
    w = [pl.BlockSpec((None, F, D), functools.partial(lambda k, i: (k, 0, 0), k), pipeline_mode=pl.Buffered(1))
         for k in (0, 1)]
    outs = pl.pallas_call(
        body, name=name, grid=(ni,),
        in_specs=[act, act] + w + [_row_spec(tm, D), _vec_spec(D), _row_spec(tm, D)] + [hbm] * ng,
        out_specs=[_row_spec(tm, D), _vec_spec(D)] + [hbm] * ng,
        out_shape=[jax.ShapeDtypeStruct((T, D), F32), jax.ShapeDtypeStruct((1, D), F32)] + _scattered_shapes(scatter),
        scratch_shapes=_scatter_sems(ng), compiler_params=_params("arbitrary"),
    )(da, db, wffn, wffn, h, g.reshape(1, D), dh, *[piece[0] for piece in scatter])
    return outs[0], outs[1], list(outs[2:])


def _ffn_bwd_w(da, db, s, h, g, dhb, name, scatter=()):
    T, F = da.shape
    D = h.shape[1]
    tf, tk = _tile(F, TF_FFN_WG, LANES), _tile(T, TK_FFN_WG, 16)
    nj, nk, ng = F // tf, T // tk, len(scatter)

    def body(*refs):
        da_ref, db_ref, s_ref, h_ref, g_ref, dh_ref = refs[:6]
        srcs, o_ref, outs = refs[6:6 + ng], refs[6 + ng], refs[7 + ng:7 + 2 * ng]
        g_sc, u_sc, d_sc = refs[7 + 2 * ng:10 + 2 * ng]
        j, k = pl.program_id(0), pl.program_id(1)
        if ng:
            _carry(_scatter_plan(scatter, srcs, outs, *refs[10 + 2 * ng:]), jnp.logical_and(j == 0, k == 0), None,
                   jnp.logical_and(j == nj - 1, k == nk - 1))

        @pl.when(k == 0)
        def _():
            g_sc[...] = jnp.zeros_like(g_sc)
            u_sc[...] = jnp.zeros_like(u_sc)
            d_sc[...] = jnp.zeros_like(d_sc)

        nv = _rms_parts(h_ref[...], g_ref[...])[0].astype(BF)
        g_sc[...] += _dot_tn(da_ref[...], nv)
        u_sc[...] += _dot_tn(db_ref[...], nv)
        d_sc[...] += _dot_tn(s_ref[...], dh_ref[...])

        @pl.when(k == nk - 1)
        def _():
            o_ref[0] = g_sc[...].astype(BF)
            o_ref[1] = u_sc[...].astype(BF)
            o_ref[2] = (0.5 * d_sc[...]).astype(BF)

    act = pl.BlockSpec((tk, tf), lambda j, k: (k, j))
    tok = pl.BlockSpec((tk, D), lambda j, k: (k, 0))
    hbm = pl.BlockSpec(memory_space=pl.ANY)
    outs = pl.pallas_call(
        body, name=name, grid=(nj, nk), in_specs=[act, act, act, tok, _vec_spec(D), tok] + [hbm] * ng,
        out_specs=[pl.BlockSpec((3, tf, D), lambda j, k: (0, j, 0), pipeline_mode=pl.Buffered(1))] + [hbm] * ng,
        out_shape=[jax.ShapeDtypeStruct((3, F, D), BF)] + _scattered_shapes(scatter),
        scratch_shapes=[pltpu.VMEM((tf, D), F32)] * 3 + _scatter_sems(ng),
        compiler_params=_params("arbitrary", "arbitrary"),
    )(da, db, s, h, g.reshape(1, D), dhb, *[piece[0] for piece in scatter])
    return outs[0], list(outs[1:])


def _ple_fwd(h, g, wg, wp, p, name):
    T, D = h.shape
    P = p.shape[1]
    tm = _tile(T, TM_EW, 16)

    def body(h_ref, g_ref, wg_ref, wp_ref, p_ref, o_ref, n_ref, gate_ref, pp_ref):
        x = h_ref[...]
        n = _rms_parts(x, g_ref[...])[0].astype(BF)
        gate = _sigmoid(_dot(n, wg_ref[...]))
        pp = _dot_nt(p_ref[...].astype(BF), wp_ref[...])
        o_ref[...] = x + gate * pp
        n_ref[...] = n
        gate_ref[...] = gate.astype(BF)
        pp_ref[...] = pp.astype(BF)

    row = _row_spec(tm, D)
    return pl.pallas_call(
        body, name=name, grid=(T // tm,),
        in_specs=[row, _vec_spec(D), _vec_spec(D, D), _vec_spec(P, D), _row_spec(tm, P)], out_specs=[row] * 4,
        out_shape=[jax.ShapeDtypeStruct((T, D), F32)] + [jax.ShapeDtypeStruct((T, D), BF)] * 3,
        compiler_params=_params("parallel"),
    )(h, g.reshape(1, D), wg, wp, p)


def _ple_bwd(dh, gate, pp, h, g, wg, name):
    T, D = dh.shape
    tm = _tile(T, TM_EW, 16)

    def body(dh_ref, gate_ref, pp_ref, h_ref, g_ref, wg_ref, o_ref, dz_ref, dp_ref, dg_ref):
        @pl.when(pl.program_id(0) == 0)
        def _():
            dg_ref[...] = jnp.zeros_like(dg_ref)

        d, gate = dh_ref[...], gate_ref[...].astype(F32)
        dz = (d * pp_ref[...].astype(F32) * gate * (1.0 - gate)).astype(BF)
        dx, dg = _rms_bwd_parts(h_ref[...], g_ref[...], _dot_nt(dz, wg_ref[...]))
        o_ref[...] = d + dx
        dz_ref[...] = dz
        dp_ref[...] = (d * gate).astype(BF)
        dg_ref[...] += dg

    row = _row_spec(tm, D)
    return pl.pallas_call(
        body, name=name, grid=(T // tm,), in_specs=[row, row, row, row, _vec_spec(D), _vec_spec(D, D)],
        out_specs=[row, row, row, _vec_spec(D)],
        out_shape=[jax.ShapeDtypeStruct((T, D), F32), jax.ShapeDtypeStruct((T, D), BF), jax.ShapeDtypeStruct((T, D), BF),
                   jax.ShapeDtypeStruct((1, D), F32)],
        compiler_params=_params("arbitrary"),
    )(dh, gate, pp, h, g.reshape(1, D), wg)


def _lru_gates(z, conv_w, conv_b, wa, wx, b_a, b_x, a_param, name):
    T, R2 = z.shape
    R = R2 // 2
    tb = _tile(T, TB_SEQ, HALO)
    per = tb // HALO

    def body(x_ref, halo_ref, cw_ref, cb_ref, wa_ref, wx_ref, ba_ref, bx_ref, ap_ref,
             xc_ref, r_ref, ig_ref, a_ref, bt_ref, ext):
        i = pl.program_id(0)
        ext[pl.ds(0, HALO), :] = jnp.where(i > 0, halo_ref[...], 0.0)
        ext[pl.ds(HALO, tb), :] = x_ref[...]
        xc = cb_ref[...] + cw_ref[0:1, :] * ext[pl.ds(HALO - 3, tb), :]
        for k in range(1, CONV_WIDTH):
            xc = xc + cw_ref[k:k + 1, :] * ext[pl.ds(HALO - 3 + k, tb), :]
        xcb = xc.astype(BF)
        r = _sigmoid(_dot(xcb, wa_ref[...]) + ba_ref[...])
        ig = _sigmoid(_dot(xcb, wx_ref[...]) + bx_ref[...])
        la = -LRU_C * r * _softplus_neg(ap_ref[...])
        xc_ref[...] = xc
        r_ref[...] = r
        ig_ref[...] = ig
        a_ref[...] = jnp.exp(la)
        bt_ref[...] = jnp.sqrt(_neg_expm1(2.0 * la)) * (ig * xc)

    tile = pl.BlockSpec((tb, R), lambda i: (i, 1))
    halo = pl.BlockSpec((HALO, R), lambda i: (jnp.maximum(i * per - 1, 0), 1))
    out = pl.BlockSpec((tb, R), lambda i: (i, 0))
    return pl.pallas_call(
        body, name=name, grid=(T // tb,),
        in_specs=[tile, halo, _vec_spec(R, CONV_WIDTH), _vec_spec(R), _vec_spec(R, R), _vec_spec(R, R),
                  _vec_spec(R), _vec_spec(R), _vec_spec(R)],
        out_specs=[out] * 5, out_shape=[jax.ShapeDtypeStruct((T, R), F32)] * 5,
        scratch_shapes=[pltpu.VMEM((HALO + tb, R), F32)],
        compiler_params=_params("parallel"),
    )(z, z, conv_w, conv_b.reshape(1, R), wa, wx, b_a.reshape(1, R), b_x.reshape(1, R), a_param.reshape(1, R))


def _lru_scan(a, b, reverse, name):
    T, R = a.shape
    tb, tc = _tile(T, TB_SCAN, SUBLANES), _tile(R, TC_SCAN, LANES)
    nt, ng = T // tb, tb // SUBLANES

    def body(a_ref, b_ref, o_ref, carry, a_sc, b_sc):
        @pl.when(pl.program_id(1) == 0)
        def _():
            carry[...] = jnp.zeros_like(carry)

        A = a_ref[...]
        B = A * b_ref[...] if reverse else b_ref[...]
        sub = lax.broadcasted_iota(jnp.int32, (tb, tc), 0) & (SUBLANES - 1)
        for k in (1, 2, 4):
            m = (sub < SUBLANES - k) if reverse else (sub >= k)
            shift = tb - k if reverse else k
            a_n = jnp.where(m, pltpu.roll(A, shift, 0), 1.0)
            b_n = jnp.where(m, pltpu.roll(B, shift, 0), 0.0)
            B = A * b_n + B
            A = A * a_n
        a_sc[...] = A
        b_sc[...] = B
        sub8 = lax.broadcasted_iota(jnp.int32, (SUBLANES, tc), 0)

        def group(q, c):
            g = (ng - 1 - q) if reverse else q
            rows = pl.ds(pl.multiple_of(g * SUBLANES, SUBLANES), SUBLANES)
            hg = a_sc[rows, :] * c + b_sc[rows, :]
            if reverse:
                nxt = jnp.where(sub8 == SUBLANES - 1, c, pltpu.roll(hg, SUBLANES - 1, 0))
                o_ref[rows, :] = b_ref[rows, :] + nxt
                return hg[0:1, :]
            o_ref[rows, :] = hg
            return hg[SUBLANES - 1:SUBLANES, :]

        carry[...] = lax.fori_loop(0, ng, group, carry[...])

    spec = pl.BlockSpec((tb, tc), (lambda c, t: (nt - 1 - t, c)) if reverse else (lambda c, t: (t, c)))
    return pl.pallas_call(
        body, name=name, grid=(R // tc, nt), in_specs=[spec, spec], out_specs=spec,
        out_shape=jax.ShapeDtypeStruct((T, R), F32),
        scratch_shapes=[pltpu.VMEM((1, tc), F32), pltpu.VMEM((tb, tc), F32), pltpu.VMEM((tb, tc), F32)],
        compiler_params=_params("parallel", "arbitrary"),
    )(a, b)


def _lru_out_ew(hs, z, name):
    T, R = hs.shape
    tm = _tile(T, TM_EW, 16)

    def body(h_ref, g_ref, y_ref):
        y_ref[...] = (h_ref[...] * _gelu_parts(g_ref[...])[0]).astype(BF)

    return pl.pallas_call(
        body, name=name, grid=(T // tm,), in_specs=[_row_spec(tm, R), _row_spec(tm, R)], out_specs=_row_spec(tm, R),
        out_shape=jax.ShapeDtypeStruct((T, R), BF), compiler_params=_params("parallel"),
    )(hs, z)


def _lru_bwd_ew(dy, hs, z, name):
    T, R = hs.shape
    tm = _tile(T, TM_EW, 16)

    def body(dy_ref, h_ref, g_ref, dhd_ref, dgb_ref):
        g, dg = _gelu_parts(g_ref[...])
        d = dy_ref[...]
        dhd_ref[...] = d * g
        dgb_ref[...] = (d * h_ref[...] * dg).astype(BF)

    return pl.pallas_call(
        body, name=name, grid=(T // tm,), in_specs=[_row_spec(tm, R)] * 3, out_specs=[_row_spec(tm, R)] * 2,
        out_shape=[jax.ShapeDtypeStruct((T, R), F32), jax.ShapeDtypeStruct((T, R), BF)],
        compiler_params=_params("parallel"),
    )(dy, hs, z)


def _lru_gates_bwd(lam, hs, r, ig, xc, wa, wx, a_param, name):
    T, R = lam.shape
    tb = _tile(T, TB_SEQ, HALO)
    per = tb // HALO
    nt = T // tb

    def body(l_ref, h_ref, hh_ref, r_ref, ig_ref, xc_ref, wa_ref, wx_ref, ap_ref,
             dxc_ref, dpa_ref, dpx_ref, dsp_ref, dba_ref, dbx_ref, ext):
        i = pl.program_id(0)

        @pl.when(i == 0)
        def _():
            dsp_ref[...] = jnp.zeros_like(dsp_ref)
            dba_ref[...] = jnp.zeros_like(dba_ref)
            dbx_ref[...] = jnp.zeros_like(dbx_ref)

        ext[pl.ds(0, HALO), :] = jnp.where(i > 0, hh_ref[...], 0.0)
        ext[pl.ds(HALO, tb), :] = h_ref[...]
        h_prev = ext[pl.ds(HALO - 1, tb), :]
        lam_v, rv, igv, xcv = l_ref[...], r_ref[...], ig_ref[...], xc_ref[...]
        sp = _softplus_neg(ap_ref[...])
        la = -LRU_C * rv * sp
        av = jnp.exp(la)
        mult = jnp.sqrt(_neg_expm1(2.0 * la))
        dla = lam_v * h_prev * av - lam_v * (igv * xcv) * (av * av) / mult
        du = lam_v * mult
        dpa = (dla * (-LRU_C) * sp) * rv * (1.0 - rv)
        dpx = (du * xcv) * igv * (1.0 - igv)
        dsp_ref[...] += jnp.sum(dla * (-LRU_C) * rv, axis=0, keepdims=True)
        dba_ref[...] += jnp.sum(dpa, axis=0, keepdims=True)
        dbx_ref[...] += jnp.sum(dpx, axis=0, keepdims=True)
        dpab, dpxb = dpa.astype(BF), dpx.astype(BF)
        dxc_ref[...] = du * igv + _dot_nt(dpab, wa_ref[...]) + _dot_nt(dpxb, wx_ref[...])
        dpa_ref[...] = dpab
        dpx_ref[...] = dpxb

        @pl.when(i == nt - 1)
        def _():
            dsp_ref[...] = dsp_ref[...] * (-_sigmoid(-ap_ref[...]))

    tile = _row_spec(tb, R)
    halo = pl.BlockSpec((HALO, R), lambda i: (jnp.maximum(i * per - 1, 0), 0))
    return pl.pallas_call(
        body, name=name, grid=(T // tb,),
        in_specs=[tile, tile, halo, tile, tile, tile, _vec_spec(R, R), _vec_spec(R, R), _vec_spec(R)],
        out_specs=[tile, tile, tile, _vec_spec(R), _vec_spec(R), _vec_spec(R)],
        out_shape=[jax.ShapeDtypeStruct((T, R), F32), jax.ShapeDtypeStruct((T, R), BF), jax.ShapeDtypeStruct((T, R), BF)]
        + [jax.ShapeDtypeStruct((1, R), F32)] * 3,
        scratch_shapes=[pltpu.VMEM((HALO + tb, R), F32)],
        compiler_params=_params("arbitrary"),
    )(lam, hs, hs, r, ig, xc, wa, wx, a_param.reshape(1, R))


def _gate_spans(R):
    d = R // LRU_HEADS
    spans = [min((j * GATE_COLS // d) * d // LANES * LANES, R - GATE_SPAN) for j in range(R // GATE_COLS)]
    assert R % GATE_COLS == 0 and all(lo + GATE_SPAN >= (((j + 1) * GATE_COLS - 1) // d + 1) * d for j, lo in enumerate(spans))
    return spans


def _lru_gates_dw(xc, dpa, dpx, name):
    T, R = xc.shape
    tk = _tile(T, 1024, 16)
    spans = _gate_spans(R)
    nb = len(spans)

    def body(x_ref, a_ref, b_ref, o_ref):
        @pl.when(pl.program_id(0) == 0)
        def _():
            o_ref[...] = jnp.zeros_like(o_ref)

        for j, lo in enumerate(spans):
            xs = x_ref[:, pl.ds(lo, GATE_SPAN)].astype(BF)
            cols = pl.ds(j * GATE_COLS, GATE_COLS)
            o_ref[0, j] += _dot_tn(xs, a_ref[:, cols])
            o_ref[1, j] += _dot_tn(xs, b_ref[:, cols])

    row = _row_spec(tk, R)
    out = pl.pallas_call(
        body, name=name, grid=(T // tk,), in_specs=[row, row, row],
        out_specs=pl.BlockSpec((2, nb, GATE_SPAN, GATE_COLS), lambda i: (0, 0, 0, 0)),
        out_shape=jax.ShapeDtypeStruct((2, nb, GATE_SPAN, GATE_COLS), F32), compiler_params=_params("arbitrary"),
    )(xc, dpa, dpx)
    dense = jnp.zeros((2, R, R), F32)
    for j, lo in enumerate(spans):
        dense = dense.at[:, lo:lo + GATE_SPAN, j * GATE_COLS:(j + 1) * GATE_COLS].set(out[:, j])
    return dense[0], dense[1]


def _lru_conv_bwd(dxc, z, conv_w, name):
    T, R = dxc.shape
    tb = _tile(T, TB_SEQ, HALO)
    per = tb // HALO
    nt = T // tb

    def body(d_ref, dn_ref, x_ref, xp_ref, cw_ref, dxb_ref, dcw_ref, dcb_ref, dext, xext):
        i = pl.program_id(0)

        @pl.when(i == 0)
        def _():
            dcw_ref[...] = jnp.zeros_like(dcw_ref)
            dcb_ref[...] = jnp.zeros_like(dcb_ref)

        d = d_ref[...]
        dext[pl.ds(0, tb), :] = d
        dext[pl.ds(tb, HALO), :] = jnp.where(i < nt - 1, dn_ref[...], 0.0)
        xext[pl.ds(0, HALO), :] = jnp.where(i > 0, xp_ref[...], 0.0)
        xext[pl.ds(HALO, tb), :] = x_ref[...]
        dxb = cw_ref[CONV_WIDTH - 1:CONV_WIDTH, :] * d
        for k in range(CONV_WIDTH - 1):
            dxb = dxb + cw_ref[k:k + 1, :] * dext[pl.ds(CONV_WIDTH - 1 - k, tb), :]
        dxb_ref[...] = dxb.astype(BF)
        for k in range(CONV_WIDTH):
            dcw_ref[k:k + 1, :] += jnp.sum(d * xext[pl.ds(HALO - 3 + k, tb), :], axis=0, keepdims=True)
        dcb_ref[...] += jnp.sum(d, axis=0, keepdims=True)

    tile = _row_spec(tb, R)
    nxt = pl.BlockSpec((HALO, R), lambda i: (jnp.minimum((i + 1) * per, T // HALO - 1), 0))
    xtile = pl.BlockSpec((tb, R), lambda i: (i, 1))
    xprev = pl.BlockSpec((HALO, R), lambda i: (jnp.maximum(i * per - 1, 0), 1))
    return pl.pallas_call(
        body, name=name, grid=(nt,), in_specs=[tile, nxt, xtile, xprev, _vec_spec(R, CONV_WIDTH)],
        out_specs=[tile, _vec_spec(R, SUBLANES), _vec_spec(R)],
        out_shape=[jax.ShapeDtypeStruct((T, R), BF), jax.ShapeDtypeStruct((SUBLANES, R), F32),
                   jax.ShapeDtypeStruct((1, R), F32)],
        scratch_shapes=[pltpu.VMEM((tb + HALO, R), F32), pltpu.VMEM((HALO + tb, R), F32)],
        compiler_params=_params("arbitrary"),
    )(dxc, dxc, z, z, conv_w)


def _window_sums(e, n, back):
    out, s = [], e
    for k in (1, 2, 4, 8):
        s = s + pltpu.roll(s, k if back else n - k, 0)
        out.append(s)
    return out


def _pool_fwd(hn, h, w, b, scale, name):
    T, D = hn.shape
    G = len(POOL_WINDOWS)
    gd = D // G
    tb = _tile(T, TB_SEQ, HALO)
    per = tb // HALO

    def body(x_ref, xp_ref, h_ref, w_ref, b_ref, s_ref, o_ref, u_ref, yb_ref):
        i = pl.program_id(0)
        t = i * tb + lax.broadcasted_iota(jnp.int32, (tb, gd), 0) + 1
        for g, win in enumerate(POOL_WINDOWS):
            cols = pl.ds(g * gd, gd)
            x = x_ref[:, cols]
            e = jnp.concatenate([jnp.where(i > 0, xp_ref[:, cols], 0.0), x], axis=0)
            sw = _window_sums(e, HALO + tb, True)[g][HALO:, :]
            u = (sw / jnp.minimum(t, win).astype(F32) - x).astype(BF)
            yb = _dot(u, w_ref[g]) + b_ref[:, cols]
            u_ref[:, cols] = u
            yb_ref[:, cols] = yb
            o_ref[:, cols] = h_ref[:, cols] + yb * s_ref[:, cols]

    tile = _row_spec(tb, D)
    prev = pl.BlockSpec((HALO, D), lambda i: (jnp.maximum(i * per - 1, 0), 0))
    return pl.pallas_call(
        body, name=name, grid=(T // tb,),
        in_specs=[tile, prev, tile, pl.BlockSpec((G, gd, gd), lambda i: (0, 0, 0)), _vec_spec(D), _vec_spec(D)],
        out_specs=[tile, tile, tile],
        out_shape=[jax.ShapeDtypeStruct((T, D), F32), jax.ShapeDtypeStruct((T, D), BF), jax.ShapeDtypeStruct((T, D), F32)],
        compiler_params=_params("parallel"),
    )(hn, hn, h, w, b.reshape(1, D), scale.reshape(1, D))


def _pool_bwd(dm, u, yb, w, scale, name):
    T, D = dm.shape
    G = len(POOL_WINDOWS)
    gd = D // G
    tb = _tile(T, TB_SEQ, HALO)

    def body(d_ref, u_ref, yb_ref, w_ref, s_ref, du_ref, v_ref, dw_ref, db_ref, ds_ref):
        i = pl.program_id(0)

        @pl.when(i == 0)
        def _():
            dw_ref[...] = jnp.zeros_like(dw_ref)
            db_ref[...] = jnp.zeros_like(db_ref)
            ds_ref[...] = jnp.zeros_like(ds_ref)

        d, sc = d_ref[...], s_ref[...]
        ds_ref[...] += jnp.sum(d * yb_ref[...], axis=0, keepdims=True)
        db_ref[...] += jnp.sum(d * sc, axis=0, keepdims=True)
        t = i * tb + lax.broadcasted_iota(jnp.int32, (tb, gd), 0) + 1
        for g, win in enumerate(POOL_WINDOWS):
            cols = pl.ds(g * gd, gd)
            dy = (d_ref[:, cols] * s_ref[:, cols]).astype(BF)
            du = _dot_nt(dy, w_ref[g])
            dw_ref[g] += _dot_tn(u_ref[:, cols], dy)
            du_ref[:, cols] = du
            v_ref[:, cols] = du / jnp.minimum(t, win).astype(F32)

    tile = _row_spec(tb, D)
    return pl.pallas_call(
        body, name=name, grid=(T // tb,),
        in_specs=[tile, tile, tile, pl.BlockSpec((G, gd, gd), lambda i: (0, 0, 0)), _vec_spec(D)],
        out_specs=[tile, tile, pl.BlockSpec((G, gd, gd), lambda i: (0, 0, 0)), _vec_spec(D), _vec_spec(D)],
        out_shape=[jax.ShapeDtypeStruct((T, D), F32), jax.ShapeDtypeStruct((T, D), F32),
                   jax.ShapeDtypeStruct((G, gd, gd), F32), jax.ShapeDtypeStruct((1, D), F32),
                   jax.ShapeDtypeStruct((1, D), F32)],
        compiler_params=_params("arbitrary"),
    )(dm, u, yb, w, scale.reshape(1, D))


def _pool_bwd_win(v, du, name):
    T, D = v.shape
    G = len(POOL_WINDOWS)
    gd = D // G
    tb = _tile(T, TB_SEQ, HALO)
    per = tb // HALO
    nt = T // tb

    def body(v_ref, vn_ref, du_ref, o_ref):
        i = pl.program_id(0)
        for g in range(G):
            cols = pl.ds(g * gd, gd)
            e = jnp.concatenate([v_ref[:, cols], jnp.where(i < nt - 1, vn_ref[:, cols], 0.0)], axis=0)
            o_ref[:, cols] = _window_sums(e, tb + HALO, False)[g][:tb, :] - du_ref[:, cols]

    tile = _row_spec(tb, D)
    nxt = pl.BlockSpec((HALO, D), lambda i: (jnp.minimum((i + 1) * per, T // HALO - 1), 0))
    return pl.pallas_call(
        body, name=name, grid=(nt,), in_specs=[tile, nxt, tile], out_specs=tile,
        out_shape=jax.ShapeDtypeStruct((T, D), F32), compiler_params=_params("parallel"),
    )(v, v, du)


def _adamw(w, g, m, v, name):
    shape = w.shape
    cols = shape[-1] if w.ndim > 1 else shape[0]
    rows = w.size // cols
    tr = _tile(rows, TR_ADAM, SUBLANES)
    c1, c2 = 1.0 / (1.0 - ADAM_B1 ** ADAM_STEP), 1.0 / (1.0 - ADAM_B2 ** ADAM_STEP)

    def body(w_ref, g_ref, m_ref, v_ref, d_ref, mo_ref, vo_ref):
        gv = g_ref[...]
        mn = ADAM_B1 * m_ref[...] + (1.0 - ADAM_B1) * gv
        vn = ADAM_B2 * v_ref[...] + (1.0 - ADAM_B2) * (gv * gv)
        d_ref[...] = -ADAM_LR * ((mn * c1) / (jnp.sqrt(vn * c2) + ADAM_EPS) + ADAM_WD * w_ref[...])
        mo_ref[...] = mn
        vo_ref[...] = vn

    spec = _row_spec(tr, cols)
    outs = pl.pallas_call(
        body, name=name, grid=(rows // tr,), in_specs=[spec] * 4, out_specs=[spec] * 3,
        out_shape=[jax.ShapeDtypeStruct((rows, cols), F32)] * 3, compiler_params=_params("parallel"),
    )(*[t.reshape(rows, cols) for t in (w, g, m, v)])
    return [o.reshape(shape) for o in outs]


def _sum_devices(parts, name):
    n, rows, cols = parts.shape
    tr = _tile(rows, 1024, SUBLANES)

    def body(p_ref, o_ref):
        acc = p_ref[0]
        for k in range(1, n):
            acc = acc + p_ref[k]
        o_ref[...] = acc

    return pl.pallas_call(
        body, name=name, grid=(rows // tr,), in_specs=[pl.BlockSpec((n, tr, cols), lambda i: (0, i, 0))],
        out_specs=_row_spec(tr, cols), out_shape=jax.ShapeDtypeStruct((rows, cols), F32),
        compiler_params=_params("parallel"),
    )(parts)


def _position():
    return lax.axis_index("x"), lax.axis_index("y"), lax.axis_index("c")


def _gathered_shapes(blocks):
    return [jax.ShapeDtypeStruct((b.shape[0], N_DEV * b.shape[1], b.shape[2]), b.dtype) for b in blocks]


def _gather_sems(ng):
    return [pltpu.SemaphoreType.DMA((ng, 7)), pltpu.SemaphoreType.DMA((ng, 7)), pltpu.SemaphoreType.DMA((ng,))] if ng else []


def _gather_plan(blocks, srcs, outs, send_sems, recv_sems, local_sems):
    ng = len(blocks)
    x, y, c = _position()
    me, sibling = (x, y, c), (x, y, 1 - c)
    chips = [(1 - x, y), (x, 1 - y), (1 - x, 1 - y)]

    def rows(g, px, py, pc):
        r = blocks[g].shape[1]
        return outs[g].at[:, pl.ds((4 * px + 2 * py + pc) * r, r), :]

    def copy(g, k, block, to, src=None):
        return pltpu.make_async_remote_copy(
            src_ref=rows(g, *block) if src is None else src, dst_ref=rows(g, *block),
            send_sem=send_sems.at[g, k], recv_sem=recv_sems.at[g, k], device_id=to, device_id_type=MESH)

    def mine(g):
        return pltpu.make_async_copy(srcs[g], rows(g, *me), local_sems.at[g])

    def first(g):
        return [copy(g, 0, me, sibling, src=srcs[g])] + [copy(g, 1 + j, me, (*chip, c), src=srcs[g])
                                                         for j, chip in enumerate(chips)]

    def passed(g):
        return [copy(g, 4 + j, (*chip, c), sibling) for j, chip in enumerate(chips)]

    def start():
        for g in range(ng):
            mine(g).start()
            for cp in first(g):
                cp.start()

    def forward():
        for j, chip in enumerate(chips):
            for g in range(ng):
                copy(g, 1 + j, (*chip, c), me).wait_recv()
                copy(g, 4 + j, (*chip, c), sibling).start()

    def finish():
        for g in range(ng):
            copy(g, 0, sibling, me).wait_recv()
            for j, chip in enumerate(chips):
                copy(g, 4 + j, (*chip, 1 - c), me).wait_recv()
            for cp in first(g) + passed(g):
                cp.wait_send()
            mine(g).wait()

    return start, forward, finish


def _all_gather(blocks, name):
    ng = len(blocks)

    def body(*refs):
        start, forward, finish = _gather_plan(blocks, refs[:ng], refs[ng:2 * ng], *refs[2 * ng:])
        start()
        forward()
        finish()

    hbm = pl.BlockSpec(memory_space=pl.ANY)
    return pl.pallas_call(
        body, name=name, in_specs=[hbm] * ng, out_specs=[hbm] * ng, out_shape=_gathered_shapes(blocks),
        scratch_shapes=_gather_sems(ng),
    )(*blocks)


FLIPS = ((0, 0, 1), (1, 0, 0), (0, 1, 0), (1, 1, 0), (1, 0, 1), (0, 1, 1), (1, 1, 1))


def _piece_rows(piece):
    arr, m = piece
    return arr.shape[0] if m is None else 1


def _scattered_shapes(pieces):
    return [jax.ShapeDtypeStruct((N_DEV, _piece_rows(p), p[0].shape[1] // N_DEV, p[0].shape[2]), p[0].dtype)
            for p in pieces]


def _scatter_sems(ng):
    n = len(FLIPS)
    return [pltpu.SemaphoreType.DMA((ng, n)), pltpu.SemaphoreType.DMA((ng, n)), pltpu.SemaphoreType.DMA((ng,))] if ng else []


def _scatter_plan(pieces, srcs, outs, send_sems, recv_sems, local_sems):
    x, y, c = _position()

    def block(g, tx, ty, tc):
        arr, m = pieces[g]
        r = arr.shape[1] // N_DEV
        lead = slice(None) if m is None else pl.ds(m, 1)
        return srcs[g].at[lead, pl.ds((4 * tx + 2 * ty + tc) * r, r), :]

    def copies(g):
        out = []
        for k, (fx, fy, fc) in enumerate(FLIPS):
            tx, ty, tc = (1 - x if fx else x), (1 - y if fy else y), (1 - c if fc else c)
            out.append(pltpu.make_async_remote_copy(
                src_ref=block(g, tx, ty, tc), dst_ref=outs[g].at[k], send_sem=send_sems.at[g, k],
                recv_sem=recv_sems.at[g, k], device_id=(tx, ty, tc), device_id_type=MESH))
        return out

    def mine(g):
        return pltpu.make_async_copy(block(g, x, y, c), outs[g].at[len(FLIPS)], local_sems.at[g])

    def start():
        for g in range(len(pieces)):
            mine(g).start()
            for cp in copies(g):
                cp.start()

    def finish():
        for g in range(len(pieces)):
            for cp in copies(g):
                cp.wait()
            mine(g).wait()

    return start, finish


def _scatter_and_gather(pieces, blocks, name):
    n_p, n_b = len(pieces), len(blocks)

    def body(*refs):
        ins, outs, sems = refs[:n_p + n_b], refs[n_p + n_b:2 * (n_p + n_b)], refs[2 * (n_p + n_b):]
        s_start, s_finish = _scatter_plan(pieces, ins[:n_p], outs[:n_p], *sems[:3])
        g_start, g_forward, g_finish = _gather_plan(blocks, ins[n_p:], outs[n_p:], *sems[3:])
        s_start()
        g_start()
        g_forward()
        g_finish()
        s_finish()

    hbm = pl.BlockSpec(memory_space=pl.ANY)
    outs = pl.pallas_call(
        body, name=name, in_specs=[hbm] * (n_p + n_b), out_specs=[hbm] * (n_p + n_b),
        out_shape=_scattered_shapes(pieces) + _gathered_shapes(blocks),
        scratch_shapes=_scatter_sems(n_p) + _gather_sems(n_b),
    )(*[p[0] for p in pieces], *blocks)
    return list(outs[:n_p]), list(outs[n_p:])


def _scatter_sum(recv, name):
    _, n, r, c = recv.shape

    def body(r_ref, o_ref):
        acc = r_ref[len(FLIPS)].astype(F32)
        for k in range(len(FLIPS)):
            acc = acc + r_ref[k].astype(F32)
        o_ref[...] = acc

    return pl.pallas_call(
        body, name=name, grid=(n,), in_specs=[pl.BlockSpec((N_DEV, None, r, c), lambda i: (0, i, 0, 0))],
        out_specs=pl.BlockSpec((None, r, c), lambda i: (i, 0, 0)),
        out_shape=jax.ShapeDtypeStruct((n, r, c), F32), compiler_params=_params("parallel"),
    )(recv)


def _block_diag(w):
    H, d, _ = w.shape
    return (jnp.eye(H, dtype=w.dtype)[:, None, :, None] * w[:, :, None, :]).reshape(H * d, H * d)


def _diag_blocks(dense, H):
    d = dense.shape[0] // H
    return jnp.stack([dense[i * d:(i + 1) * d, i * d:(i + 1) * d] for i in range(H)])


def _local_step(x, p, tgt, W, blocks=None):
    dist = blocks is not None
    L = p.shape[0]
    W = dict(W)

    def gathering(keys):
        return [k for k in keys if k not in W] if dist else []

    def ffn_fwd(h, g, i, f, during_act, during_out):
        w = W[("ffn", i, f)]
        keys = gathering(during_act)
        a, b, s, got = _ffn_fwd_act(h, g, w, f"ffn{f}_fwd_act_{i}", gather=[blocks[k] for k in keys])
        W.update(zip(keys, got))
        keys = gathering(during_out)
        h, got = _ffn_fwd_out(s, w, h, f"ffn{f}_fwd_out_{i}", gather=[blocks[k] for k in keys])
        W.update(zip(keys, got))
        return a, b, s, h

    saved = []
    h = x
    for i in range(L):
        j = i // 2
        lru = i % 2 == 0
        s = {"h0": h}
        mixer = [("lru_in", j), ("lru_out", j)] if lru else [("pool_w", j)]
        s["a1"], s["b1"], s["s1"], h = ffn_fwd(h, W["ffn1_norm"][i], i, 1, [("ffn", i, 2)], mixer)
        s["h1"] = h
        if lru:
            hn = _rms_fwd(h, W["mix_norm"][i], BF, f"mix_norm_{i}")
            z = _mm(hn, W[("lru_in", j)][0], "nt", f"lru_in_{i}")
            wa, wx = _block_diag(W["lru_w_a"][j]).astype(BF), _block_diag(W["lru_w_x"][j]).astype(BF)
            xc, r, ig, a, bt = _lru_gates(z, W["lru_conv_w"][j], W["lru_conv_b"][j], wa, wx, W["lru_b_a"][j],
                                          W["lru_b_x"][j], W["lru_a_param"][j], f"lru_gates_{i}")
            hs = _lru_scan(a, bt, False, f"lru_scan_{i}")
            y = _lru_out_ew(hs, z, f"lru_out_ew_{i}")
            h = _mm(y, W[("lru_out", j)][0], "nn", f"lru_out_{i}", res=h)
            s.update(hn=hn, z=z, wa=wa, wx=wx, xc=xc, r=r, ig=ig, a=a, hs=hs, y=y)
        else:
            hn = _rms_fwd(h, W["mix_norm"][i], F32, f"mix_norm_{i}")
            h, s["u"], s["yb"] = _pool_fwd(hn, h, W[("pool_w", j)], W["pool_b"][j], W["pool_scale"][j],
                                           f"pool_fwd_{i}")
        s["h2"] = h
        s["a2"], s["b2"], s["s2"], h = ffn_fwd(h, W["ffn2_norm"][i], i, 2, [("ffn", i + 1, 1)] if i + 1 < L else [],
                                               [("ple_gate", i), ("ple_proj", i)])
        s["h3"] = h
        h, s["n4"], s["gate"], s["pp"] = _ple_fwd(h, W["ple_norm"][i], W[("ple_gate", i)][0], W[("ple_proj", i)][0], p[i],
                                                  f"ple_fwd_{i}")
        saved.append(s)

    loss, dh, d_final = _loss_head(h, W["final_norm"], tgt)

    big, recv = {}, {}
    n_lru, n_pool = L // 2 + L % 2, L // 2
    small = {k: [None] * L for k in ("ffn1_norm", "mix_norm", "ffn2_norm", "ple_norm")}
    for k in ("lru_conv_w", "lru_conv_b", "lru_w_a", "lru_b_a", "lru_w_x", "lru_b_x", "lru_a_param"):
        small[k] = [None] * n_lru
    for k in ("pool_b", "pool_scale"):
        small[k] = [None] * n_pool

    def scattering(pieces):
        return [(k, m) for k, m in pieces if k in big] if dist else []

    def ffn_bwd(dh, h_in, g, a, b, sv, i, f, during):
        key, w = ("ffn", i, f), W[("ffn", i, f)]
        out = [scattering(d) for d in during]
        sent = [[(big[k], m) for k, m in o] for o in out]
        da, db, dhb, got0 = _ffn_bwd_act(dh, a, b, w, f"ffn{f}_bwd_act_{i}", scatter=sent[0])
        big[key], got1 = _ffn_bwd_w(da, db, sv, h_in, g, dhb, f"ffn{f}_dw_{i}", scatter=sent[1])
        out.append(scattering([(key, 0)]))
        dh, dg, got2 = _ffn_bwd_in(da, db, w, h_in, g, dh, f"ffn{f}_bwd_in_{i}", scatter=[(big[key], 0)] if dist else [])
        for o, got in zip(out, (got0, got1, got2)):
            recv.update(zip(o, got))
        return dh, dg

    for i in reversed(range(L)):
        j = i // 2
        lru = i % 2 == 0
        s = saved[i]
        dh, dz, dpp, dg = _ple_bwd(dh, s["gate"], s["pp"], s["h3"], W["ple_norm"][i], W[("ple_gate", i)][0],
                                   f"ple_bwd_{i}")
        big[("ple_gate", i)] = _mm(s["n4"], dz, "tn", f"ple_gate_dw_{i}", out_dtype=BF)[None]
        big[("ple_proj", i)] = _mm(dpp, p[i], "tn", f"ple_proj_dw_{i}", out_dtype=BF)[None]
        small["ple_norm"][i] = dg[0]
        above = ("ffn", i + 1, 1)
        dh, dg = ffn_bwd(dh, s["h2"], W["ffn2_norm"][i], s["a2"], s["b2"], s["s2"], i, 2, [
            [(above, 1)], [(above, 2), (("ple_gate", i), None), (("ple_proj", i), None)]])
        small["ffn2_norm"][i] = dg[0]
        if lru:
            R = W[("lru_out", j)].shape[1]
            big[("lru_out", j)] = _mm(s["y"], dh, "tn", f"lru_out_dw_{i}", out_dtype=BF)[None]
            dy = _mm(dh, W[("lru_out", j)][0], "nt", f"lru_out_dx_{i}")
            dhd, dgb = _lru_bwd_ew(dy, s["hs"], s["z"], f"lru_bwd_ew_{i}")
            lam = _lru_scan(s["a"], dhd, True, f"lru_scan_bwd_{i}")
            dxc, dpa, dpx, dsp, dba, dbx = _lru_gates_bwd(lam, s["hs"], s["r"], s["ig"], s["xc"], s["wa"], s["wx"],
                                                          W["lru_a_param"][j], f"lru_gates_bwd_{i}")
            small["lru_a_param"][j], small["lru_b_a"][j], small["lru_b_x"][j] = dsp[0], dba[0], dbx[0]
            dwa, dwx = _lru_gates_dw(s["xc"], dpa, dpx, f"lru_gates_dw_{i}")
            small["lru_w_a"][j], small["lru_w_x"][j] = _diag_blocks(dwa, LRU_HEADS), _diag_blocks(dwx, LRU_HEADS)
            dxb, dcw, dcb = _lru_conv_bwd(dxc, s["z"], W["lru_conv_w"][j], f"lru_conv_bwd_{i}")
            small["lru_conv_w"][j], small["lru_conv_b"][j] = dcw[:CONV_WIDTH], dcb[0]
            win = W[("lru_in", j)][0]
            big[("lru_in", j)] = jnp.concatenate([_mm(dgb, s["hn"], "tn", f"lru_in_dw_g_{i}", out_dtype=BF),
                                                  _mm(dxb, s["hn"], "tn", f"lru_in_dw_x_{i}", out_dtype=BF)])[None]
            dhn = _mm(dgb, win[:R], "nn", f"lru_in_dx_g_{i}")
            dhn = _mm(dxb, win[R:], "nn", f"lru_in_dx_x_{i}", res=dhn)
            mixer = [("lru_in", j), ("lru_out", j)]
        else:
            du, v, dw, dbp, dsc = _pool_bwd(dh, s["u"], s["yb"], W[("pool_w", j)], W["pool_scale"][j], f"pool_bwd_{i}")
            big[("pool_w", j)] = dw.astype(BF)
            small["pool_b"][j], small["pool_scale"][j] = dbp[0], dsc[0]
            dhn = _pool_bwd_win(v, du, f"pool_bwd_win_{i}")
            mixer = [("pool_w", j)]
        dh, dg = _rms_bwd(s["h1"], W["mix_norm"][i], dhn, dh, f"mix_norm_bwd_{i}")
        small["mix_norm"][i] = dg[0]
        second = ("ffn", i, 2)
        dh, dg = ffn_bwd(dh, s["h0"], W["ffn1_norm"][i], s["a1"], s["b1"], s["s1"], i, 1, [
            [(second, 1)], [(second, 2)] + [(k, None) for k in mixer]])
        small["ffn1_norm"][i] = dg[0]

    small = {k: jnp.stack(v) for k, v in small.items()}
    small["final_norm"] = d_final[0]
    return loss, dh, big, recv, small


SMALL_SHARDED = ("pool_b", "pool_scale", "lru_conv_w")
SMALL = ("ffn1_norm", "mix_norm", "ffn2_norm", "ple_norm", "final_norm", "lru_conv_b", "lru_w_a", "lru_b_a",
         "lru_w_x", "lru_b_x", "lru_a_param", "pool_b", "pool_scale", "lru_conv_w")


def _pack_big(w):
    t = lambda a: jnp.swapaxes(a, -1, -2)
    out = {}
    for i in range(w["ffn1_norm"].shape[0]):
        for f in (1, 2):
            out[("ffn", i, f)] = jnp.stack([t(w[f"ffn{f}_w_gate"][i]), t(w[f"ffn{f}_w_up"][i]), w[f"ffn{f}_w_down"][i]])
        out[("ple_gate", i)], out[("ple_proj", i)] = w["ple_w_gate"][i][None], t(w["ple_w_proj"][i])[None]
    for j in range(w["lru_w_in"].shape[0]):
        out[("lru_in", j)], out[("lru_out", j)] = t(w["lru_w_in"][j])[None], w["lru_w_out"][j][None]
    for j in range(w["pool_w"].shape[0]):
        out[("pool_w", j)] = w["pool_w"][j]
    return out


def _unpack_big(b, L):
    t = lambda a: jnp.swapaxes(a, -1, -2)
    n_lru, n_pool = L // 2 + L % 2, L // 2
    out = {"lru_w_in": jnp.stack([t(b[("lru_in", j)][0]) for j in range(n_lru)]),
           "lru_w_out": jnp.stack([b[("lru_out", j)][0] for j in range(n_lru)]),
           "pool_w": jnp.stack([b[("pool_w", j)] for j in range(n_pool)]),
           "ple_w_gate": jnp.stack([b[("ple_gate", i)][0] for i in range(L)]),
           "ple_w_proj": jnp.stack([t(b[("ple_proj", i)][0]) for i in range(L)])}
    for f in (1, 2):
        out[f"ffn{f}_w_gate"] = jnp.stack([t(b[("ffn", i, f)][0]) for i in range(L)])
        out[f"ffn{f}_w_up"] = jnp.stack([t(b[("ffn", i, f)][1]) for i in range(L)])
        out[f"ffn{f}_w_down"] = jnp.stack([b[("ffn", i, f)][2] for i in range(L)])
    return out


def _flatten(parts, names, rows_of=LANES):
    flat = jnp.concatenate([parts[k].reshape(-1) for k in names])
    pad = (-flat.size) % (16 * rows_of)
    return jnp.pad(flat, (0, pad)).reshape(1, -1, rows_of)


def _unflatten(flat, like, names):
    out, o = {}, 0
    flat = flat.reshape(-1)
    for k in names:
        n = like[k].size
        out[k] = flat[o:o + n].reshape(like[k].shape)
        o += n
    return out


def kernel(x, p, ffn1_norm, ffn1_w_gate, ffn1_w_up, ffn1_w_down, mix_norm, lru_w_in, lru_conv_w, lru_conv_b, lru_w_a, lru_b_a, lru_w_x, lru_b_x, lru_a_param, lru_w_out, pool_w, pool_b, pool_scale, ffn2_norm, ffn2_w_gate, ffn2_w_up, ffn2_w_down, ple_norm, ple_w_gate, ple_w_proj, final_norm, loss_target, m_ffn1_norm, m_ffn1_w_gate, m_ffn1_w_up, m_ffn1_w_down, m_mix_norm, m_lru_w_in, m_lru_conv_w, m_lru_conv_b, m_lru_w_a, m_lru_b_a, m_lru_w_x, m_lru_b_x, m_lru_a_param, m_lru_w_out, m_pool_w, m_pool_b, m_pool_scale, m_ffn2_norm, m_ffn2_w_gate, m_ffn2_w_up, m_ffn2_w_down, m_ple_norm, m_ple_w_gate, m_ple_w_proj, m_final_norm, v_ffn1_norm, v_ffn1_w_gate, v_ffn1_w_up, v_ffn1_w_down, v_mix_norm, v_lru_w_in, v_lru_conv_w, v_lru_conv_b, v_lru_w_a, v_lru_b_a, v_lru_w_x, v_lru_b_x, v_lru_a_param, v_lru_w_out, v_pool_w, v_pool_b, v_pool_scale, v_ffn2_norm, v_ffn2_w_gate, v_ffn2_w_up, v_ffn2_w_down, v_ple_norm, v_ple_w_gate, v_ple_w_proj, v_final_norm):
    names = ["ffn1_norm", "ffn1_w_gate", "ffn1_w_up", "ffn1_w_down", "mix_norm", "lru_w_in", "lru_conv_w", "lru_conv_b",
             "lru_w_a", "lru_b_a", "lru_w_x", "lru_b_x", "lru_a_param", "lru_w_out", "pool_w", "pool_b", "pool_scale",
             "ffn2_norm", "ffn2_w_gate", "ffn2_w_up", "ffn2_w_down", "ple_norm", "ple_w_gate", "ple_w_proj", "final_norm"]
    w = dict(zip(names, [ffn1_norm, ffn1_w_gate, ffn1_w_up, ffn1_w_down, mix_norm, lru_w_in, lru_conv_w, lru_conv_b, lru_w_a, lru_b_a, lru_w_x, lru_b_x, lru_a_param, lru_w_out, pool_w, pool_b, pool_scale, ffn2_norm, ffn2_w_gate, ffn2_w_up, ffn2_w_down, ple_norm, ple_w_gate, ple_w_proj, final_norm]))
    m = dict(zip(names, [m_ffn1_norm, m_ffn1_w_gate, m_ffn1_w_up, m_ffn1_w_down, m_mix_norm, m_lru_w_in, m_lru_conv_w, m_lru_conv_b, m_lru_w_a, m_lru_b_a, m_lru_w_x, m_lru_b_x, m_lru_a_param, m_lru_w_out, m_pool_w, m_pool_b, m_pool_scale, m_ffn2_norm, m_ffn2_w_gate, m_ffn2_w_up, m_ffn2_w_down, m_ple_norm, m_ple_w_gate, m_ple_w_proj, m_final_norm]))
    v = dict(zip(names, [v_ffn1_norm, v_ffn1_w_gate, v_ffn1_w_up, v_ffn1_w_down, v_mix_norm, v_lru_w_in, v_lru_conv_w, v_lru_conv_b, v_lru_w_a, v_lru_b_a, v_lru_w_x, v_lru_b_x, v_lru_a_param, v_lru_w_out, v_pool_w, v_pool_b, v_pool_scale, v_ffn2_norm, v_ffn2_w_gate, v_ffn2_w_up, v_ffn2_w_down, v_ple_norm, v_ple_w_gate, v_ple_w_proj, v_final_norm]))
    L = p.shape[0]
    px, py, pc = _position()
    me = 4 * px + 2 * py + pc

    blocks = {k: b.astype(BF) for k, b in _pack_big(w).items()}
    first = ("ffn", 0, 1)
    got, small_blocks = _all_gather([blocks[first], _flatten(w, SMALL_SHARDED)], "gather_first")
    W = {first: got}
    per_dev = small_blocks.reshape(N_DEV, -1)
    shards = [_unflatten(per_dev[k], w, SMALL_SHARDED) for k in range(N_DEV)]
    for k in SMALL:
        W[k] = jnp.concatenate([s[k] for s in shards], axis=-1) if k in SMALL_SHARDED else w[k]

    loss, dx, big, recv, small = _local_step(x[0], p[:, 0], loss_target[0], W, blocks)

    last = [(k, m) for k in big if (k, None) not in recv for m in range(big[k].shape[0]) if (k, m) not in recv]
    got, (parts,) = _scatter_and_gather([(big[k], m) for k, m in last], [_flatten(small, SMALL)], "scatter_last_gather_small")
    recv.update(zip(last, got))

    def total(k):
        tag = "sum_" + "_".join(map(str, k))
        if (k, None) in recv:
            return _scatter_sum(recv[(k, None)], tag)
        return jnp.concatenate([_scatter_sum(recv[(k, m)], f"{tag}_{m}") for m in range(big[k].shape[0])])

    grads = _unpack_big({k: total(k) for k in big}, L)
    total_small = _sum_devices(parts.reshape(N_DEV, -1, LANES), "sum_small_grads")
    full = _unflatten(total_small, {k: W[k] for k in SMALL}, SMALL)
    for k in SMALL:
        if k in SMALL_SHARDED:
            n = w[k].shape[-1]
            grads[k] = lax.dynamic_slice_in_dim(full[k], me * n, n, axis=-1)
        else:
            grads[k] = full[k]

    delta, new_m, new_v = {}, {}, {}
    for k in names:
        delta[k], new_m[k], new_v[k] = _adamw(w[k], grads[k], m[k], v[k], f"adamw_{k}")
    total_loss = lax.psum(loss[0, 0], ("x", "y", "c"))
    return (total_loss, dx[None], *[grads[k] for k in names], *[delta[k] for k in names],
            *[new_m[k] for k in names], *[new_v[k] for k in names])
```

```python
import functools

import jax
import jax.numpy as jnp
from jax import lax
from jax.experimental import pallas as pl
from jax.experimental.pallas import tpu as pltpu

F32 = jnp.float32
BF = jnp.bfloat16
MESH = pl.DeviceIdType.MESH

RMS_EPS = 1e-6
LRU_C = 8.0
LRU_HEADS = 16
CONV_WIDTH = 4
POOL_WINDOWS = (2, 4, 8, 16)
ADAM_LR, ADAM_B1, ADAM_B2, ADAM_EPS, ADAM_WD, ADAM_STEP = 0.001, 0.9, 0.999, 1e-08, 0.01, 10

N_DEV = 8
LANES = 128
SUBLANES = 8
GATE_COLS, GATE_SPAN = 256, 512
HALO = 16
VMEM_LIMIT = 56 * 1024 * 1024

TM_FFN = 1024
TM_FFN_ACT = 2048
TM_FFN_IN = 512
TN_FFN_OUT = 512
TF_FFN = 256
TF_FFN_WG = 1408
TK_FFN_WG = 512
TB_SEQ = 256
TB_SCAN = 256
TC_SCAN = 640
TM_EW = 512
TM_MM, TN_MM, TK_MM = 1024, 512, 1024
TR_ADAM = 512


def _tile(n, pref, align):
    if n <= pref:
        return n
    t = (pref // align) * align
    while t >= align:
        if n % t == 0:
            return t
        t -= align
    raise ValueError(f"no tile for {n} (pref {pref}, align {align})")


def _params(*sem):
    return pltpu.CompilerParams(dimension_semantics=sem, vmem_limit_bytes=VMEM_LIMIT)


def _dot(a, b):
    return lax.dot_general(a, b, (((1,), (0,)), ((), ())), preferred_element_type=F32)


def _dot_nt(a, b):
    return lax.dot_general(a, b, (((1,), (1,)), ((), ())), preferred_element_type=F32)


def _dot_tn(a, b):
    return lax.dot_general(a, b, (((0,), (0,)), ((), ())), preferred_element_type=F32)


def _sigmoid(x):
    return 1.0 / (1.0 + jnp.exp(-x))


def _gelu_parts(x):
    k0, k1 = 0.7978845608028654, 0.044715
    t = jnp.tanh(k0 * (x + k1 * x * x * x))
    g = 0.5 * x * (1.0 + t)
    dg = 0.5 * (1.0 + t) + 0.5 * x * (1.0 - t * t) * k0 * (1.0 + 3.0 * k1 * x * x)
    return g, dg


def _neg_expm1(x):
    p = x * (1.0 + x * (0.5 + x * (1.0 / 6 + x * (1.0 / 24 + x * (1.0 / 120 + x * (1.0 / 720 + x * (1.0 / 5040)))))))
    return jnp.where(x > -0.35, -p, 1.0 - jnp.exp(x))


def _softplus_neg(l):
    u = jnp.exp(-jnp.abs(l))
    w = 1.0 + u
    log1p = jnp.where(w == 1.0, u, jnp.log(w) * (u / jnp.where(w == 1.0, 1.0, w - 1.0)))
    return jnp.maximum(-l, 0.0) + log1p


def _rms_parts(x, g):
    r = lax.rsqrt(jnp.mean(x * x, axis=-1, keepdims=True) + RMS_EPS)
    nhat = x * r
    return nhat * g, nhat, r


def _rms_bwd_parts(x, g, dn):
    _, nhat, r = _rms_parts(x, g)
    u = dn * g
    dx = r * (u - nhat * jnp.mean(u * nhat, axis=-1, keepdims=True))
    return dx, jnp.sum(dn * nhat, axis=0, keepdims=True)


def _row_spec(tm, d, single=False):
    if single:
        return pl.BlockSpec((tm, d), lambda i, *_: (i, 0), pipeline_mode=pl.Buffered(1))
    return pl.BlockSpec((tm, d), lambda i, *_: (i, 0))


def _vec_spec(d, rows=1):
    return pl.BlockSpec((rows, d), lambda *_: (0, 0))


def _mm(x, w, mode, name, out_dtype=F32, res=None, alpha=1.0, tm=None, tn=None, tk=None):
    if mode == "nn":
        (M, K), (_, N) = x.shape, w.shape
    elif mode == "nt":
        (M, K), (N, _) = x.shape, w.shape
    else:
        (K, M), (_, N) = x.shape, w.shape
    tm = _tile(M, tm or TM_MM, LANES if mode == "tn" else SUBLANES)
    tn = _tile(N, tn or TN_MM, LANES)
    tk = _tile(K, tk or TK_MM, LANES if mode != "tn" else 16)
    nk = K // tk
    dot = {"nn": _dot, "nt": _dot_nt, "tn": _dot_tn}[mode]

    def body(*refs):
        if res is None:
            x_ref, w_ref, o_ref, acc = refs
        else:
            x_ref, w_ref, r_ref, o_ref, acc = refs
        k = pl.program_id(2)

        @pl.when(k == 0)
        def _():
            acc[...] = jnp.zeros_like(acc)

        acc[...] += dot(x_ref[...].astype(BF), w_ref[...].astype(BF))

        @pl.when(k == nk - 1)
        def _():
            r = acc[...] if alpha == 1.0 else acc[...] * alpha
            if res is not None:
                r = r_ref[...] + r
            o_ref[...] = r.astype(out_dtype)

    if mode == "nn":
        specs = [pl.BlockSpec((tm, tk), lambda i, j, k: (i, k)), pl.BlockSpec((tk, tn), lambda i, j, k: (k, j))]
    elif mode == "nt":
        specs = [pl.BlockSpec((tm, tk), lambda i, j, k: (i, k)), pl.BlockSpec((tn, tk), lambda i, j, k: (j, k))]
    else:
        specs = [pl.BlockSpec((tk, tm), lambda i, j, k: (k, i)), pl.BlockSpec((tk, tn), lambda i, j, k: (k, j))]
    args = [x, w]
    if res is not None:
        specs.append(pl.BlockSpec((tm, tn), lambda i, j, k: (i, j)))
        args.append(res)
    return pl.pallas_call(
        body, name=name, grid=(M // tm, N // tn, nk), in_specs=specs,
        out_specs=pl.BlockSpec((tm, tn), lambda i, j, k: (i, j)),
        out_shape=jax.ShapeDtypeStruct((M, N), out_dtype),
        scratch_shapes=[pltpu.VMEM((tm, tn), F32)],
        compiler_params=_params("parallel", "parallel", "arbitrary"),
    )(*args)


def _rms_fwd(h, g, out_dtype, name):
    T, D = h.shape
    tm = _tile(T, TM_EW, 16)

    def body(h_ref, g_ref, o_ref):
        o_ref[...] = _rms_parts(h_ref[...], g_ref[...])[0].astype(out_dtype)

    return pl.pallas_call(
        body, name=name, grid=(T // tm,), in_specs=[_row_spec(tm, D), _vec_spec(D)], out_specs=_row_spec(tm, D),
        out_shape=jax.ShapeDtypeStruct((T, D), out_dtype), compiler_params=_params("parallel"),
    )(h, g.reshape(1, D))


def _loss_head(h, g, tgt):
    T, D = h.shape
    tm = _tile(T, TM_EW, 16)

    def body(h_ref, g_ref, t_ref, loss_ref, dh_ref, dg_ref):
        @pl.when(pl.program_id(0) == 0)
        def _():
            dg_ref[...] = jnp.zeros_like(dg_ref)
            loss_ref[...] = jnp.zeros_like(loss_ref)

        x, gg = h_ref[...], g_ref[...]
        y = _rms_parts(x, gg)[0]
        e = y - t_ref[...]
        part = jnp.sum(jnp.sum(e * e, axis=0, keepdims=True), axis=1, keepdims=True) * (0.5 / D)
        loss_ref[...] += jnp.broadcast_to(part, loss_ref.shape)
        dx, dg = _rms_bwd_parts(x, gg, e * (1.0 / D))
        dh_ref[...] = dx
        dg_ref[...] += dg

    return pl.pallas_call(
        body, name="loss_head", grid=(T // tm,),
        in_specs=[_row_spec(tm, D), _vec_spec(D), _row_spec(tm, D)],
        out_specs=[_vec_spec(LANES), _row_spec(tm, D), _vec_spec(D)],
        out_shape=[jax.ShapeDtypeStruct((1, LANES), F32), jax.ShapeDtypeStruct((T, D), F32),
                   jax.ShapeDtypeStruct((1, D), F32)],
        compiler_params=_params("arbitrary"),
    )(h, g.reshape(1, D), tgt)


def _carry(plan, first, mid, last):
    pl.when(first)(plan[0])
    if len(plan) == 3:
        pl.when(mid)(plan[1])
    pl.when(last)(plan[-1])


def _ffn_fwd_act(h, g, wffn, name, gather=()):
    T, D = h.shape
    F = wffn.shape[1]
    tm, tf = _tile(T, TM_FFN_ACT, 16), _tile(F, TF_FFN, LANES)
    ni, nf, ng = T // tm, F // tf, len(gather)

    def body(*refs):
        h_ref, g_ref, wg_ref, wu_ref = refs[:4]
        srcs, (a_ref, b_ref, s_ref), outs = refs[4:4 + ng], refs[4 + ng:7 + ng], refs[7 + ng:7 + 2 * ng]
        n_sc = refs[7 + 2 * ng]
        i, j = pl.program_id(0), pl.program_id(1)
        if ng:
            _carry(_gather_plan(gather, srcs, outs, *refs[8 + 2 * ng:]), jnp.logical_and(i == 0, j == 0),
                   jnp.logical_and(i == (3 * ni) // 4, j == 0), jnp.logical_and(i == ni - 1, j == nf - 1))

        @pl.when(j == 0)
        def _():
            n_sc[...] = _rms_parts(h_ref[...], g_ref[...])[0].astype(BF)

        n = n_sc[...]
        a = _dot_nt(n, wg_ref[...])
        b = _dot_nt(n, wu_ref[...])
        a_ref[...] = a.astype(BF)
        b_ref[...] = b.astype(BF)
        s_ref[...] = (a * _sigmoid(a) * b).astype(BF)

    tile = pl.BlockSpec((tm, tf), lambda i, j: (i, j))
    w = [pl.BlockSpec((None, tf, D), functools.partial(lambda k, i, j: (k, j, 0), k)) for k in (0, 1)]
    hbm = pl.BlockSpec(memory_space=pl.ANY)
    outs = pl.pallas_call(
        body, name=name, grid=(ni, nf), in_specs=[_row_spec(tm, D), _vec_spec(D)] + w + [hbm] * ng,
        out_specs=[tile, tile, tile] + [hbm] * ng,
        out_shape=[jax.ShapeDtypeStruct((T, F), BF)] * 3 + _gathered_shapes(gather),
        scratch_shapes=[pltpu.VMEM((tm, D), BF)] + _gather_sems(ng),
        compiler_params=_params("arbitrary", "arbitrary"),
    )(h, g.reshape(1, D), wffn, wffn, *gather)
    return outs[0], outs[1], outs[2], list(outs[3:])


def _ffn_fwd_out(s, wffn, h, name, gather=()):
    T, F = s.shape
    D = h.shape[1]
    tm, tn = _tile(T, TM_FFN, 16), _tile(D, TN_FFN_OUT, LANES)
    ni, nj, ng = T // tm, D // tn, len(gather)

    def body(*refs):
        s_ref, w_ref, h_ref = refs[:3]
        srcs, o_ref, outs = refs[3:3 + ng], refs[3 + ng], refs[4 + ng:4 + 2 * ng]
        i, j = pl.program_id(0), pl.program_id(1)
        if ng:
            _carry(_gather_plan(gather, srcs, outs, *refs[4 + 2 * ng:]), jnp.logical_and(i == 0, j == 0),
                   jnp.logical_and(i == (3 * ni) // 4, j == 0), jnp.logical_and(i == ni - 1, j == nj - 1))
        o_ref[...] = h_ref[...] + 0.5 * _dot(s_ref[...], w_ref[...])

    hbm = pl.BlockSpec(memory_space=pl.ANY)
    tile = pl.BlockSpec((tm, tn), lambda i, j: (i, j))
    outs = pl.pallas_call(
        body, name=name, grid=(ni, nj),
        in_specs=[pl.BlockSpec((tm, F), lambda i, j: (i, 0)), pl.BlockSpec((None, F, tn), lambda i, j: (2, 0, j)), tile]
        + [hbm] * ng,
        out_specs=[tile] + [hbm] * ng, out_shape=[jax.ShapeDtypeStruct((T, D), F32)] + _gathered_shapes(gather),
        scratch_shapes=_gather_sems(ng), compiler_params=_params("arbitrary", "arbitrary"),
    )(s, wffn, h, *gather)
    return outs[0], list(outs[1:])


def _ffn_bwd_act(dh, a, b, wffn, name, scatter=()):
    T, D = dh.shape
    F = wffn.shape[1]
    tm, tf = _tile(T, TM_FFN_ACT, 16), _tile(F, TF_FFN, LANES)
    ni, nf, ng = T // tm, F // tf, len(scatter)

    def body(*refs):
        dh_ref, a_ref, b_ref, wd_ref = refs[:4]
        srcs, (da_ref, db_ref, dhb_ref), outs = refs[4:4 + ng], refs[4 + ng:7 + ng], refs[7 + ng:7 + 2 * ng]
        i, j = pl.program_id(0), pl.program_id(1)
        if ng:
            _carry(_scatter_plan(scatter, srcs, outs, *refs[7 + 2 * ng:]), jnp.logical_and(i == 0, j == 0), None,
                   jnp.logical_and(i == ni - 1, j == nf - 1))

        @pl.when(j == 0)
        def _():
            dhb_ref[...] = dh_ref[...].astype(BF)

        ds = 0.5 * _dot_nt(dhb_ref[...], wd_ref[...])
        av, bv = a_ref[...].astype(F32), b_ref[...].astype(F32)
        sig = _sigmoid(av)
        da_ref[...] = (ds * bv * (sig * (1.0 + av * (1.0 - sig)))).astype(BF)
        db_ref[...] = (ds * (av * sig)).astype(BF)

    tile = pl.BlockSpec((tm, tf), lambda i, j: (i, j))
    hbm = pl.BlockSpec(memory_space=pl.ANY)
    outs = pl.pallas_call(
        body, name=name, grid=(ni, nf),
        in_specs=[_row_spec(tm, D), tile, tile, pl.BlockSpec((None, tf, D), lambda i, j: (2, j, 0))] + [hbm] * ng,
        out_specs=[tile, tile, _row_spec(tm, D)] + [hbm] * ng,
        out_shape=[jax.ShapeDtypeStruct((T, F), BF)] * 2 + [jax.ShapeDtypeStruct((T, D), BF)]
        + _scattered_shapes(scatter),
        scratch_shapes=_scatter_sems(ng), compiler_params=_params("arbitrary", "arbitrary"),
    )(dh, a, b, wffn, *[piece[0] for piece in scatter])
    return outs[0], outs[1], outs[2], list(outs[3:])


def _two_dot_norm_bwd(x1, x2, w, w_specs, h, g, dh, name, scatter=()):
    T, K = x1.shape
    D = h.shape[1]
    tm = _tile(T, TM_FFN_IN, 16)
    ni, ng = T // tm, len(scatter)

    def body(*refs):
        x1_ref, x2_ref, w1_ref, w2_ref, h_ref, g_ref, dh_ref = refs[:7]
        srcs, (o_ref, dg_ref), outs = refs[7:7 + ng], refs[7 + ng:9 + ng], refs[9 + ng:9 + 2 * ng]
        i = pl.program_id(0)
        if ng:
            _carry(_scatter_plan(scatter, srcs, outs, *refs[9 + 2 * ng:]), i == 0, None, i == ni - 1)

        @pl.when(i == 0)
        def _():
            dg_ref[...] = jnp.zeros_like(dg_ref)

        dn = _dot(x1_ref[...], w1_ref[...]) + _dot(x2_ref[...], w2_ref[...])
        dx, dg = _rms_bwd_parts(h_ref[...], g_ref[...], dn)
        o_ref[...] = dh_ref[...] + dx
        dg_ref[...] += dg

    hbm = pl.BlockSpec(memory_space=pl.ANY)
    act = pl.BlockSpec((tm, K), lambda i: (i, 0))
    outs = pl.pallas_call(
        body, name=name, grid=(ni,),
        in_specs=[act, act] + w_specs + [_row_spec(tm, D), _vec_spec(D), _row_spec(tm, D)] + [hbm] * ng,
        out_specs=[_row_spec(tm, D), _vec_spec(D)] + [hbm] * ng,
        out_shape=[jax.ShapeDtypeStruct((T, D), F32), jax.ShapeDtypeStruct((1, D), F32)] + _scattered_shapes(scatter),
        scratch_shapes=_scatter_sems(ng), compiler_params=_params("arbitrary"),
    )(x1, x2, w, w, h, g.reshape(1, D), dh, *[piece[0] for piece in scatter])
    return outs[0], outs[1], list(outs[2:])


def _ffn_bwd_in(da, db, wffn, h, g, dh, name, scatter=()):
    F, D = wffn.shape[1:]
    specs = [pl.BlockSpec((None, F, D), functools.partial(lambda k, i: (k, 0, 0), k), pipeline_mode=pl.Buffered(1))
             for k in (0, 1)]
    return _two_dot_norm_bwd(da, db, wffn, specs, h, g, dh, name, scatter)


def _lru_in_bwd(dgb, dxb, win, h, g, dh, name):
    R, D = win.shape[0] // 2, win.shape[1]
    specs = [pl.BlockSpec((R, D), functools.partial(lambda k, i: (k, 0), k), pipeline_mode=pl.Buffered(1)) for k in (0, 1)]
    return _two_dot_norm_bwd(dgb, dxb, win, specs, h, g, dh, name)[:2]


def _ffn_bwd_w(da, db, s, h, g, dhb, name, scatter=()):
    T, F = da.shape
    D = h.shape[1]
    tf, tk = _tile(F, TF_FFN_WG, LANES), _tile(T, TK_FFN_WG, 16)
    nj, nk, ng = F // tf, T // tk, len(scatter)

    def body(*refs):
        da_ref, db_ref, s_ref, h_ref, g_ref, dh_ref = refs[:6]
        srcs, o_ref, outs = refs[6:6 + ng], refs[6 + ng], refs[7 + ng:7 + 2 * ng]
        g_sc, u_sc, d_sc = refs[7 + 2 * ng:10 + 2 * ng]
        j, k = pl.program_id(0), pl.program_id(1)
        if ng:
            _carry(_scatter_plan(scatter, srcs, outs, *refs[10 + 2 * ng:]), jnp.logical_and(j == 0, k == 0), None,
                   jnp.logical_and(j == nj - 1, k == nk - 1))

        @pl.when(k == 0)
        def _():
            g_sc[...] = jnp.zeros_like(g_sc)
            u_sc[...] = jnp.zeros_like(u_sc)
            d_sc[...] = jnp.zeros_like(d_sc)

        nv = _rms_parts(h_ref[...], g_ref[...])[0].astype(BF)
        g_sc[...] += _dot_tn(da_ref[...], nv)
        u_sc[...] += _dot_tn(db_ref[...], nv)
        d_sc[...] += _dot_tn(s_ref[...], dh_ref[...])

        @pl.when(k == nk - 1)
        def _():
            o_ref[0] = g_sc[...].astype(BF)
            o_ref[1] = u_sc[...].astype(BF)
            o_ref[2] = (0.5 * d_sc[...]).astype(BF)

    act = pl.BlockSpec((tk, tf), lambda j, k: (k, j))
    tok = pl.BlockSpec((tk, D), lambda j, k: (k, 0))
    hbm = pl.BlockSpec(memory_space=pl.ANY)
    outs = pl.pallas_call(
        body, name=name, grid=(nj, nk), in_specs=[act, act, act, tok, _vec_spec(D), tok] + [hbm] * ng,
        out_specs=[pl.BlockSpec((3, tf, D), lambda j, k: (0, j, 0), pipeline_mode=pl.Buffered(1))] + [hbm] * ng,
        out_shape=[jax.ShapeDtypeStruct((3, F, D), BF)] + _scattered_shapes(scatter),
        scratch_shapes=[pltpu.VMEM((tf, D), F32)] * 3 + _scatter_sems(ng),
        compiler_params=_params("arbitrary", "arbitrary"),
    )(da, db, s, h, g.reshape(1, D), dhb, *[piece[0] for piece in scatter])
    return outs[0], list(outs[1:])


def _ple_fwd(h, g, wg, wp, p, name):
    T, D = h.shape
    P = p.shape[1]
    tm = _tile(T, TM_EW, 16)

    def body(h_ref, g_ref, wg_ref, wp_ref, p_ref, o_ref, n_ref, gate_ref, pp_ref):
        x = h_ref[...]
        n = _rms_parts(x, g_ref[...])[0].astype(BF)
        gate = _sigmoid(_dot(n, wg_ref[...]))
        pp = _dot_nt(p_ref[...].astype(BF), wp_ref[...])
        o_ref[...] = x + gate * pp
        n_ref[...] = n
        gate_ref[...] = gate.astype(BF)
        pp_ref[...] = pp.astype(BF)

    row = _row_spec(tm, D)
    return pl.pallas_call(
        body, name=name, grid=(T // tm,),
        in_specs=[row, _vec_spec(D), _vec_spec(D, D), _vec_spec(P, D), _row_spec(tm, P)], out_specs=[row] * 4,
        out_shape=[jax.ShapeDtypeStruct((T, D), F32)] + [jax.ShapeDtypeStruct((T, D), BF)] * 3,
        compiler_params=_params("parallel"),
    )(h, g.reshape(1, D), wg, wp, p)


def _ple_bwd(dh, gate, pp, h, g, wg, name):
    T, D = dh.shape
    tm = _tile(T, TM_EW, 16)

    def body(dh_ref, gate_ref, pp_ref, h_ref, g_ref, wg_ref, o_ref, dz_ref, dp_ref, dg_ref):
        @pl.when(pl.program_id(0) == 0)
        def _():
            dg_ref[...] = jnp.zeros_like(dg_ref)

        d, gate = dh_ref[...], gate_ref[...].astype(F32)
        dz = (d * pp_ref[...].astype(F32) * gate * (1.0 - gate)).astype(BF)
        dx, dg = _rms_bwd_parts(h_ref[...], g_ref[...], _dot_nt(dz, wg_ref[...]))
        o_ref[...] = d + dx
        dz_ref[...] = dz
        dp_ref[...] = (d * gate).astype(BF)
        dg_ref[...] += dg

    row = _row_spec(tm, D)
    return pl.pallas_call(
        body, name=name, grid=(T // tm,), in_specs=[row, row, row, row, _vec_spec(D), _vec_spec(D, D)],
        out_specs=[row, row, row, _vec_spec(D)],
        out_shape=[jax.ShapeDtypeStruct((T, D), F32), jax.ShapeDtypeStruct((T, D), BF), jax.ShapeDtypeStruct((T, D), BF),
                   jax.ShapeDtypeStruct((1, D), F32)],
        compiler_params=_params("arbitrary"),
    )(dh, gate, pp, h, g.reshape(1, D), wg)


def _lru_gates(z, conv_w, conv_b, wa, wx, b_a, b_x, a_param, name):
    T, R2 = z.shape
    R = R2 // 2
    tb = _tile(T, TB_SEQ, HALO)
    per = tb // HALO

    def body(x_ref, halo_ref, cw_ref, cb_ref, wa_ref, wx_ref, ba_ref, bx_ref, ap_ref,
             xc_ref, r_ref, ig_ref, a_ref, bt_ref, ext):
        i = pl.program_id(0)
        ext[pl.ds(0, HALO), :] = jnp.where(i > 0, halo_ref[...], 0.0)
        ext[pl.ds(HALO, tb), :] = x_ref[...]
        xc = cb_ref[...] + cw_ref[0:1, :] * ext[pl.ds(HALO - 3, tb), :]
        for k in range(1, CONV_WIDTH):
            xc = xc + cw_ref[k:k + 1, :] * ext[pl.ds(HALO - 3 + k, tb), :]
        xcb = xc.astype(BF)
        r = _sigmoid(_dot(xcb, wa_ref[...]) + ba_ref[...])
        ig = _sigmoid(_dot(xcb, wx_ref[...]) + bx_ref[...])
        la = -LRU_C * r * _softplus_neg(ap_ref[...])
        xc_ref[...] = xc
        r_ref[...] = r
        ig_ref[...] = ig
        a_ref[...] = jnp.exp(la)
        bt_ref[...] = jnp.sqrt(_neg_expm1(2.0 * la)) * (ig * xc)

    tile = pl.BlockSpec((tb, R), lambda i: (i, 1))
    halo = pl.BlockSpec((HALO, R), lambda i: (jnp.maximum(i * per - 1, 0), 1))
    out = pl.BlockSpec((tb, R), lambda i: (i, 0))
    return pl.pallas_call(
        body, name=name, grid=(T // tb,),
        in_specs=[tile, halo, _vec_spec(R, CONV_WIDTH), _vec_spec(R), _vec_spec(R, R), _vec_spec(R, R),
                  _vec_spec(R), _vec_spec(R), _vec_spec(R)],
        out_specs=[out] * 5, out_shape=[jax.ShapeDtypeStruct((T, R), F32)] * 5,
        scratch_shapes=[pltpu.VMEM((HALO + tb, R), F32)],
        compiler_params=_params("parallel"),
    )(z, z, conv_w, conv_b.reshape(1, R), wa, wx, b_a.reshape(1, R), b_x.reshape(1, R), a_param.reshape(1, R))


def _lru_scan(a, b, reverse, name):
    T, R = a.shape
    tb, tc = _tile(T, TB_SCAN, SUBLANES), _tile(R, TC_SCAN, LANES)
    nt, ng = T // tb, tb // SUBLANES

    def body(a_ref, b_ref, o_ref, carry, a_sc, b_sc):
        @pl.when(pl.program_id(1) == 0)
        def _():
            carry[...] = jnp.zeros_like(carry)

        A = a_ref[...].reshape(ng, SUBLANES, tc)
        B = A * b_ref[...].reshape(ng, SUBLANES, tc) if reverse else b_ref[...].reshape(ng, SUBLANES, tc)
        sub = lax.broadcasted_iota(jnp.int32, (1, SUBLANES, tc), 1)
        for k in (1, 2, 4):
            m = (sub < SUBLANES - k) if reverse else (sub >= k)
            shift = SUBLANES - k if reverse else k
            a_n = jnp.where(m, pltpu.roll(A, shift, 1), 1.0)
            b_n = jnp.where(m, pltpu.roll(B, shift, 1), 0.0)
            B = A * b_n + B
            A = A * a_n
        a_sc[...] = A.reshape(tb, tc)
        b_sc[...] = B.reshape(tb, tc)
        sub8 = lax.broadcasted_iota(jnp.int32, (SUBLANES, tc), 0)

        def group(q, c):
            g = (ng - 1 - q) if reverse else q
            rows = pl.ds(pl.multiple_of(g * SUBLANES, SUBLANES), SUBLANES)
            hg = a_sc[rows, :] * c + b_sc[rows, :]
            if reverse:
                nxt = jnp.where(sub8 == SUBLANES - 1, c, pltpu.roll(hg, SUBLANES - 1, 0))
                o_ref[rows, :] = b_ref[rows, :] + nxt
                return hg[0:1, :]
            o_ref[rows, :] = hg
            return hg[SUBLANES - 1:SUBLANES, :]

        carry[...] = lax.fori_loop(0, ng, group, carry[...])

    spec = pl.BlockSpec((tb, tc), (lambda c, t: (nt - 1 - t, c)) if reverse else (lambda c, t: (t, c)))
    return pl.pallas_call(
        body, name=name, grid=(R // tc, nt), in_specs=[spec, spec], out_specs=spec,
        out_shape=jax.ShapeDtypeStruct((T, R), F32),
        scratch_shapes=[pltpu.VMEM((1, tc), F32), pltpu.VMEM((tb, tc), F32), pltpu.VMEM((tb, tc), F32)],
        compiler_params=_params("parallel", "arbitrary"),
    )(a, b)


def _lru_out(hs, z, wout, h, name):
    T, R = hs.shape
    D = h.shape[1]
    tm = _tile(T, TM_EW, 16)

    def body(s_ref, g_ref, w_ref, h_ref, o_ref, y_ref):
        y = (s_ref[...] * _gelu_parts(g_ref[...])[0]).astype(BF)
        y_ref[...] = y
        o_ref[...] = h_ref[...] + _dot(y, w_ref[...])

    return pl.pallas_call(
        body, name=name, grid=(T // tm,),
        in_specs=[_row_spec(tm, R), _row_spec(tm, R), _vec_spec(D, R), _row_spec(tm, D)],
        out_specs=[_row_spec(tm, D), _row_spec(tm, R)],
        out_shape=[jax.ShapeDtypeStruct((T, D), F32), jax.ShapeDtypeStruct((T, R), BF)],
        compiler_params=_params("parallel"),
    )(hs, z, wout, h)


def _lru_bwd_ew(dy, hs, z, name):
    T, R = hs.shape
    tm = _tile(T, TM_EW, 16)

    def body(dy_ref, h_ref, g_ref, dhd_ref, dgb_ref):
        g, dg = _gelu_parts(g_ref[...])
        d = dy_ref[...]
        dhd_ref[...] = d * g
        dgb_ref[...] = (d * h_ref[...] * dg).astype(BF)

    return pl.pallas_call(
        body, name=name, grid=(T // tm,), in_specs=[_row_spec(tm, R)] * 3, out_specs=[_row_spec(tm, R)] * 2,
        out_shape=[jax.ShapeDtypeStruct((T, R), F32), jax.ShapeDtypeStruct((T, R), BF)],
        compiler_params=_params("parallel"),
    )(dy, hs, z)


def _lru_gates_bwd(lam, hs, r, ig, xc, wa, wx, a_param, name):
    T, R = lam.shape
    tb = _tile(T, TB_SEQ, HALO)
    per = tb // HALO
    nt = T // tb

    def body(l_ref, h_ref, hh_ref, r_ref, ig_ref, xc_ref, wa_ref, wx_ref, ap_ref,
             dxc_ref, dpa_ref, dpx_ref, dsp_ref, dba_ref, dbx_ref, ext):
        i = pl.program_id(0)

        @pl.when(i == 0)
        def _():
            dsp_ref[...] = jnp.zeros_like(dsp_ref)
            dba_ref[...] = jnp.zeros_like(dba_ref)
            dbx_ref[...] = jnp.zeros_like(dbx_ref)

        ext[pl.ds(0, HALO), :] = jnp.where(i > 0, hh_ref[...], 0.0)
        ext[pl.ds(HALO, tb), :] = h_ref[...]
        h_prev = ext[pl.ds(HALO - 1, tb), :]
        lam_v, rv, igv, xcv = l_ref[...], r_ref[...], ig_ref[...], xc_ref[...]
        sp = _softplus_neg(ap_ref[...])
        la = -LRU_C * rv * sp
        av = jnp.exp(la)
        mult = jnp.sqrt(_neg_expm1(2.0 * la))
        dla = lam_v * h_prev * av - lam_v * (igv * xcv) * (av * av) / mult
        du = lam_v * mult
        dpa = (dla * (-LRU_C) * sp) * rv * (1.0 - rv)
        dpx = (du * xcv) * igv * (1.0 - igv)
        dsp_ref[...] += jnp.sum(dla * (-LRU_C) * rv, axis=0, keepdims=True)
        dba_ref[...] += jnp.sum(dpa, axis=0, keepdims=True)
        dbx_ref[...] += jnp.sum(dpx, axis=0, keepdims=True)
        dpab, dpxb = dpa.astype(BF), dpx.astype(BF)
        dxc_ref[...] = du * igv + _dot_nt(dpab, wa_ref[...]) + _dot_nt(dpxb, wx_ref[...])
        dpa_ref[...] = dpab
        dpx_ref[...] = dpxb

        @pl.when(i == nt - 1)
        def _():
            dsp_ref[...] = dsp_ref[...] * (-_sigmoid(-ap_ref[...]))

    tile = _row_spec(tb, R)
    halo = pl.BlockSpec((HALO, R), lambda i: (jnp.maximum(i * per - 1, 0), 0))
    return pl.pallas_call(
        body, name=name, grid=(T // tb,),
        in_specs=[tile, tile, halo, tile, tile, tile, _vec_spec(R, R), _vec_spec(R, R), _vec_spec(R)],
        out_specs=[tile, tile, tile, _vec_spec(R), _vec_spec(R), _vec_spec(R)],
        out_shape=[jax.ShapeDtypeStruct((T, R), F32), jax.ShapeDtypeStruct((T, R), BF), jax.ShapeDtypeStruct((T, R), BF)]
        + [jax.ShapeDtypeStruct((1, R), F32)] * 3,
        scratch_shapes=[pltpu.VMEM((HALO + tb, R), F32)],
        compiler_params=_params("arbitrary"),
    )(lam, hs, hs, r, ig, xc, wa, wx, a_param.reshape(1, R))


def _gate_spans(R):
    d = R // LRU_HEADS
    spans = [min((j * GATE_COLS // d) * d // LANES * LANES, R - GATE_SPAN) for j in range(R // GATE_COLS)]
    assert R % GATE_COLS == 0 and all(lo + GATE_SPAN >= (((j + 1) * GATE_COLS - 1) // d + 1) * d for j, lo in enumerate(spans))
    return spans


def _lru_gates_dw(xc, dpa, dpx, name):
    T, R = xc.shape
    tk = _tile(T, 1024, 16)
    spans = _gate_spans(R)
    nb = len(spans)

    def body(x_ref, a_ref, b_ref, o_ref):
        @pl.when(pl.program_id(0) == 0)
        def _():
            o_ref[...] = jnp.zeros_like(o_ref)

        for j, lo in enumerate(spans):
            xs = x_ref[:, pl.ds(lo, GATE_SPAN)].astype(BF)
            cols = pl.ds(j * GATE_COLS, GATE_COLS)
            o_ref[0, j] += _dot_tn(xs, a_ref[:, cols])
            o_ref[1, j] += _dot_tn(xs, b_ref[:, cols])

    row = _row_spec(tk, R)
    out = pl.pallas_call(
        body, name=name, grid=(T // tk,), in_specs=[row, row, row],
        out_specs=pl.BlockSpec((2, nb, GATE_SPAN, GATE_COLS), lambda i: (0, 0, 0, 0)),
        out_shape=jax.ShapeDtypeStruct((2, nb, GATE_SPAN, GATE_COLS), F32), compiler_params=_params("arbitrary"),
    )(xc, dpa, dpx)
    dense = jnp.zeros((2, R, R), F32)
    for j, lo in enumerate(spans):
        dense = dense.at[:, lo:lo + GATE_SPAN, j * GATE_COLS:(j + 1) * GATE_COLS].set(out[:, j])
    return dense[0], dense[1]


def _lru_conv_bwd(dxc, z, conv_w, name):
    T, R = dxc.shape
    tb = _tile(T, TB_SEQ, HALO)
    per = tb // HALO
    nt = T // tb

    def body(d_ref, dn_ref, x_ref, xp_ref, cw_ref, dxb_ref, dcw_ref, dcb_ref, dext, xext):
        i = pl.program_id(0)

        @pl.when(i == 0)
        def _():
            dcw_ref[...] = jnp.zeros_like(dcw_ref)
            dcb_ref[...] = jnp.zeros_like(dcb_ref)

        d = d_ref[...]
        dext[pl.ds(0, tb), :] = d
        dext[pl.ds(tb, HALO), :] = jnp.where(i < nt - 1, dn_ref[...], 0.0)
        xext[pl.ds(0, HALO), :] = jnp.where(i > 0, xp_ref[...], 0.0)
        xext[pl.ds(HALO, tb), :] = x_ref[...]
        dxb = cw_ref[CONV_WIDTH - 1:CONV_WIDTH, :] * d
        for k in range(CONV_WIDTH - 1):
            dxb = dxb + cw_ref[k:k + 1, :] * dext[pl.ds(CONV_WIDTH - 1 - k, tb), :]
        dxb_ref[...] = dxb.astype(BF)
        for k in range(CONV_WIDTH):
            dcw_ref[k:k + 1, :] += jnp.sum(d * xext[pl.ds(HALO - 3 + k, tb), :], axis=0, keepdims=True)
        dcb_ref[...] += jnp.sum(d, axis=0, keepdims=True)

    tile = _row_spec(tb, R)
    nxt = pl.BlockSpec((HALO, R), lambda i: (jnp.minimum((i + 1) * per, T // HALO - 1), 0))
    xtile = pl.BlockSpec((tb, R), lambda i: (i, 1))
    xprev = pl.BlockSpec((HALO, R), lambda i: (jnp.maximum(i * per - 1, 0), 1))
    return pl.pallas_call(
        body, name=name, grid=(nt,), in_specs=[tile, nxt, xtile, xprev, _vec_spec(R, CONV_WIDTH)],
        out_specs=[tile, _vec_spec(R, SUBLANES), _vec_spec(R)],
        out_shape=[jax.ShapeDtypeStruct((T, R), BF), jax.ShapeDtypeStruct((SUBLANES, R), F32),
                   jax.ShapeDtypeStruct((1, R), F32)],
        scratch_shapes=[pltpu.VMEM((tb + HALO, R), F32), pltpu.VMEM((HALO + tb, R), F32)],
        compiler_params=_params("arbitrary"),
    )(dxc, dxc, z, z, conv_w)


def _window_sums(e, n, back):
    out, s = [], e
    for k in (1, 2, 4, 8):
        s = s + pltpu.roll(s, k if back else n - k, 0)
        out.append(s)
    return out


def _pool_fwd(h, g, w, b, scale, name):
    T, D = h.shape
    G = len(POOL_WINDOWS)
    gd = D // G
    tb = _tile(T, TB_SEQ, HALO)
    per = tb // HALO

    def body(h_ref, hp_ref, g_ref, w_ref, b_ref, s_ref, o_ref, u_ref, yb_ref):
        i = pl.program_id(0)
        t = i * tb + lax.broadcasted_iota(jnp.int32, (tb, gd), 0) + 1
        hv = h_ref[...]
        xn = _rms_parts(hv, g_ref[...])[0]
        xp = jnp.where(i > 0, _rms_parts(hp_ref[...], g_ref[...])[0], 0.0)
        for k, win in enumerate(POOL_WINDOWS):
            cols = slice(k * gd, (k + 1) * gd)
            x = xn[:, cols]
            e = jnp.concatenate([xp[:, cols], x], axis=0)
            sw = _window_sums(e, HALO + tb, True)[k][HALO:, :]
            u = (sw / jnp.minimum(t, win).astype(F32) - x).astype(BF)
            yb = _dot(u, w_ref[k]) + b_ref[:, cols]
            u_ref[:, cols] = u
            yb_ref[:, cols] = yb
            o_ref[:, cols] = hv[:, cols] + yb * s_ref[:, cols]

    tile = _row_spec(tb, D)
    prev = pl.BlockSpec((HALO, D), lambda i: (jnp.maximum(i * per - 1, 0), 0))
    return pl.pallas_call(
        body, name=name, grid=(T // tb,),
        in_specs=[tile, prev, _vec_spec(D), pl.BlockSpec((G, gd, gd), lambda i: (0, 0, 0)), _vec_spec(D), _vec_spec(D)],
        out_specs=[tile, tile, tile],
        out_shape=[jax.ShapeDtypeStruct((T, D), F32), jax.ShapeDtypeStruct((T, D), BF), jax.ShapeDtypeStruct((T, D), F32)],
        compiler_params=_params("parallel"),
    )(h, h, g.reshape(1, D), w, b.reshape(1, D), scale.reshape(1, D))


def _pool_bwd(dm, u, yb, w, scale, name):
    T, D = dm.shape
    G = len(POOL_WINDOWS)
    gd = D // G
    tb = _tile(T, TB_SEQ, HALO)

    def body(d_ref, u_ref, yb_ref, w_ref, s_ref, du_ref, v_ref, dw_ref, db_ref, ds_ref):
        i = pl.program_id(0)

        @pl.when(i == 0)
        def _():
            dw_ref[...] = jnp.zeros_like(dw_ref)
            db_ref[...] = jnp.zeros_like(db_ref)
            ds_ref[...] = jnp.zeros_like(ds_ref)

        d, sc = d_ref[...], s_ref[...]
        ds_ref[...] += jnp.sum(d * yb_ref[...], axis=0, keepdims=True)
        db_ref[...] += jnp.sum(d * sc, axis=0, keepdims=True)
        t = i * tb + lax.broadcasted_iota(jnp.int32, (tb, gd), 0) + 1
        for g, win in enumerate(POOL_WINDOWS):
            cols = pl.ds(g * gd, gd)
            dy = (d_ref[:, cols] * s_ref[:, cols]).astype(BF)
            du = _dot_nt(dy, w_ref[g])
            dw_ref[g] += _dot_tn(u_ref[:, cols], dy)
            du_ref[:, cols] = du
            v_ref[:, cols] = du / jnp.minimum(t, win).astype(F32)

    tile = _row_spec(tb, D)
    return pl.pallas_call(
        body, name=name, grid=(T // tb,),
        in_specs=[tile, tile, tile, pl.BlockSpec((G, gd, gd), lambda i: (0, 0, 0)), _vec_spec(D)],
        out_specs=[tile, tile, pl.BlockSpec((G, gd, gd), lambda i: (0, 0, 0)), _vec_spec(D), _vec_spec(D)],
        out_shape=[jax.ShapeDtypeStruct((T, D), F32), jax.ShapeDtypeStruct((T, D), F32),
                   jax.ShapeDtypeStruct((G, gd, gd), F32), jax.ShapeDtypeStruct((1, D), F32),
                   jax.ShapeDtypeStruct((1, D), F32)],
        compiler_params=_params("arbitrary"),
    )(dm, u, yb, w, scale.reshape(1, D))


def _pool_bwd_win(v, du, h, g, dh, name):
    T, D = v.shape
    G = len(POOL_WINDOWS)
    gd = D // G
    tb = _tile(T, TB_SEQ, HALO)
    per = tb // HALO
    nt = T // tb

    def body(v_ref, vn_ref, du_ref, h_ref, g_ref, dh_ref, o_ref, dg_ref):
        i = pl.program_id(0)

        @pl.when(i == 0)
        def _():
            dg_ref[...] = jnp.zeros_like(dg_ref)

        parts = []
        for k in range(G):
            cols = pl.ds(k * gd, gd)
            e = jnp.concatenate([v_ref[:, cols], jnp.where(i < nt - 1, vn_ref[:, cols], 0.0)], axis=0)
            parts.append(_window_sums(e, tb + HALO, False)[k][:tb, :] - du_ref[:, cols])
        dx, dg = _rms_bwd_parts(h_ref[...], g_ref[...], jnp.concatenate(parts, axis=1))
        o_ref[...] = dh_ref[...] + dx
        dg_ref[...] += dg

    tile = _row_spec(tb, D)
    nxt = pl.BlockSpec((HALO, D), lambda i: (jnp.minimum((i + 1) * per, T // HALO - 1), 0))
    return pl.pallas_call(
        body, name=name, grid=(nt,), in_specs=[tile, nxt, tile, tile, _vec_spec(D), tile], out_specs=[tile, _vec_spec(D)],
        out_shape=[jax.ShapeDtypeStruct((T, D), F32), jax.ShapeDtypeStruct((1, D), F32)],
        compiler_params=_params("arbitrary"),
    )(v, v, du, h, g.reshape(1, D), dh)


def _adamw(w, g, m, v, name):
    shape = w.shape
    cols = shape[-1] if w.ndim > 1 else shape[0]
    rows = w.size // cols
    tr = _tile(rows, TR_ADAM, SUBLANES)
    c1, c2 = 1.0 / (1.0 - ADAM_B1 ** ADAM_STEP), 1.0 / (1.0 - ADAM_B2 ** ADAM_STEP)

    def body(w_ref, g_ref, m_ref, v_ref, d_ref, mo_ref, vo_ref):
        gv = g_ref[...]
        mn = ADAM_B1 * m_ref[...] + (1.0 - ADAM_B1) * gv
        vn = ADAM_B2 * v_ref[...] + (1.0 - ADAM_B2) * (gv * gv)
        d_ref[...] = -ADAM_LR * ((mn * c1) / (jnp.sqrt(vn * c2) + ADAM_EPS) + ADAM_WD * w_ref[...])
        mo_ref[...] = mn
        vo_ref[...] = vn

    spec = _row_spec(tr, cols)
    outs = pl.pallas_call(
        body, name=name, grid=(rows // tr,), in_specs=[spec] * 4, out_specs=[spec] * 3,
        out_shape=[jax.ShapeDtypeStruct((rows, cols), F32)] * 3, compiler_params=_params("parallel"),
    )(*[t.reshape(rows, cols) for t in (w, g, m, v)])
    return [o.reshape(shape) for o in outs]


def _sum_devices(parts, name):
    n, rows, cols = parts.shape
    tr = _tile(rows, 1024, SUBLANES)

    def body(p_ref, o_ref):
        acc = p_ref[0]
        for k in range(1, n):
            acc = acc + p_ref[k]
        o_ref[...] = acc

    return pl.pallas_call(
        body, name=name, grid=(rows // tr,), in_specs=[pl.BlockSpec((n, tr, cols), lambda i: (0, i, 0))],
        out_specs=_row_spec(tr, cols), out_shape=jax.ShapeDtypeStruct((rows, cols), F32),
        compiler_params=_params("parallel"),
    )(parts)


def _position():
    return lax.axis_index("x"), lax.axis_index("y"), lax.axis_index("c")


def _gathered_shapes(blocks):
    return [jax.ShapeDtypeStruct((b.shape[0], N_DEV * b.shape[1], b.shape[2]), b.dtype) for b in blocks]


def _gather_sems(ng):
    return [pltpu.SemaphoreType.DMA((ng, 7)), pltpu.SemaphoreType.DMA((ng, 7)), pltpu.SemaphoreType.DMA((ng,))] if ng else []


def _gather_plan(blocks, srcs, outs, send_sems, recv_sems, local_sems):
    ng = len(blocks)
    x, y, c = _position()
    me, sibling = (x, y, c), (x, y, 1 - c)
    chips = [(1 - x, y), (x, 1 - y), (1 - x, 1 - y)]

    def rows(g, px, py, pc):
        r = blocks[g].shape[1]
        return outs[g].at[:, pl.ds((4 * px + 2 * py + pc) * r, r), :]

    def copy(g, k, block, to, src=None):
        return pltpu.make_async_remote_copy(
            src_ref=rows(g, *block) if src is None else src, dst_ref=rows(g, *block),
            send_sem=send_sems.at[g, k], recv_sem=recv_sems.at[g, k], device_id=to, device_id_type=MESH)

    def mine(g):
        return pltpu.make_async_copy(srcs[g], rows(g, *me), local_sems.at[g])

    def first(g):
        return [copy(g, 0, me, sibling, src=srcs[g])] + [copy(g, 1 + j, me, (*chip, c), src=srcs[g])
                                                         for j, chip in enumerate(chips)]

    def passed(g):
        return [copy(g, 4 + j, (*chip, c), sibling) for j, chip in enumerate(chips)]

    def start():
        for g in range(ng):
            mine(g).start()
            for cp in first(g):
                cp.start()

    def forward():
        for j, chip in enumerate(chips):
            for g in range(ng):
                copy(g, 1 + j, (*chip, c), me).wait_recv()
                copy(g, 4 + j, (*chip, c), sibling).start()

    def finish():
        for g in range(ng):
            copy(g, 0, sibling, me).wait_recv()
            for j, chip in enumerate(chips):
                copy(g, 4 + j, (*chip, 1 - c), me).wait_recv()
            for cp in first(g) + passed(g):
                cp.wait_send()
            mine(g).wait()

    return start, forward, finish


def _all_gather(blocks, name):
    ng = len(blocks)

    def body(*refs):
        start, forward, finish = _gather_plan(blocks, refs[:ng], refs[ng:2 * ng], *refs[2 * ng:])
        start()
        forward()
        finish()

    hbm = pl.BlockSpec(memory_space=pl.ANY)
    return pl.pallas_call(
        body, name=name, in_specs=[hbm] * ng, out_specs=[hbm] * ng, out_shape=_gathered_shapes(blocks),
        scratch_shapes=_gather_sems(ng),
    )(*blocks)


FLIPS = ((0, 0, 1), (1, 0, 0), (0, 1, 0), (1, 1, 0), (1, 0, 1), (0, 1, 1), (1, 1, 1))


def _piece_rows(piece):
    arr, m = piece
    return arr.shape[0] if m is None else 1


def _scattered_shapes(pieces):
    return [jax.ShapeDtypeStruct((N_DEV, _piece_rows(p), p[0].shape[1] // N_DEV, p[0].shape[2]), p[0].dtype)
            for p in pieces]


def _scatter_sems(ng):
    n = len(FLIPS)
    return [pltpu.SemaphoreType.DMA((ng, n)), pltpu.SemaphoreType.DMA((ng, n)), pltpu.SemaphoreType.DMA((ng,))] if ng else []


def _scatter_plan(pieces, srcs, outs, send_sems, recv_sems, local_sems):
    x, y, c = _position()

    def block(g, tx, ty, tc):
        arr, m = pieces[g]
        r = arr.shape[1] // N_DEV
        lead = slice(None) if m is None else pl.ds(m, 1)
        return srcs[g].at[lead, pl.ds((4 * tx + 2 * ty + tc) * r, r), :]

    def copies(g):
        out = []
        for k, (fx, fy, fc) in enumerate(FLIPS):
            tx, ty, tc = (1 - x if fx else x), (1 - y if fy else y), (1 - c if fc else c)
            out.append(pltpu.make_async_remote_copy(
                src_ref=block(g, tx, ty, tc), dst_ref=outs[g].at[k], send_sem=send_sems.at[g, k],
                recv_sem=recv_sems.at[g, k], device_id=(tx, ty, tc), device_id_type=MESH))
        return out

    def mine(g):
        return pltpu.make_async_copy(block(g, x, y, c), outs[g].at[len(FLIPS)], local_sems.at[g])

    def start():
        for g in range(len(pieces)):
            mine(g).start()
            for cp in copies(g):
                cp.start()

    def finish():
        for g in range(len(pieces)):
            for cp in copies(g):
                cp.wait()
            mine(g).wait()

    return start, finish


def _scatter_and_gather(pieces, blocks, name):
    n_p, n_b = len(pieces), len(blocks)

    def body(*refs):
        ins, outs, sems = refs[:n_p + n_b], refs[n_p + n_b:2 * (n_p + n_b)], refs[2 * (n_p + n_b):]
        s_start, s_finish = _scatter_plan(pieces, ins[:n_p], outs[:n_p], *sems[:3])
        g_start, g_forward, g_finish = _gather_plan(blocks, ins[n_p:], outs[n_p:], *sems[3:])
        s_start()
        g_start()
        g_forward()
        g_finish()
        s_finish()

    hbm = pl.BlockSpec(memory_space=pl.ANY)
    outs = pl.pallas_call(
        body, name=name, in_specs=[hbm] * (n_p + n_b), out_specs=[hbm] * (n_p + n_b),
        out_shape=_scattered_shapes(pieces) + _gathered_shapes(blocks),
        scratch_shapes=_scatter_sems(n_p) + _gather_sems(n_b),
    )(*[p[0] for p in pieces], *blocks)
    return list(outs[:n_p]), list(outs[n_p:])


def _scatter_sum(recv, name):
    _, n, r, c = recv.shape

    def body(r_ref, o_ref):
        acc = r_ref[len(FLIPS)].astype(F32)
        for k in range(len(FLIPS)):
            acc = acc + r_ref[k].astype(F32)
        o_ref[...] = acc

    return pl.pallas_call(
        body, name=name, grid=(n,), in_specs=[pl.BlockSpec((N_DEV, None, r, c), lambda i: (0, i, 0, 0))],
        out_specs=pl.BlockSpec((None, r, c), lambda i: (i, 0, 0)),
        out_shape=jax.ShapeDtypeStruct((n, r, c), F32), compiler_params=_params("parallel"),
    )(recv)


def _block_diag(w):
    H, d, _ = w.shape
    return (jnp.eye(H, dtype=w.dtype)[:, None, :, None] * w[:, :, None, :]).reshape(H * d, H * d)


def _diag_blocks(dense, H):
    d = dense.shape[0] // H
    return jnp.stack([dense[i * d:(i + 1) * d, i * d:(i + 1) * d] for i in range(H)])


def _local_step(x, p, tgt, W, blocks=None):
    dist = blocks is not None
    L = p.shape[0]
    W = dict(W)

    def gathering(keys):
        return [k for k in keys if k not in W] if dist else []

    def ffn_fwd(h, g, i, f, during_act, during_out):
        w = W[("ffn", i, f)]
        keys = gathering(during_act)
        a, b, s, got = _ffn_fwd_act(h, g, w, f"ffn{f}_fwd_act_{i}", gather=[blocks[k] for k in keys])
        W.update(zip(keys, got))
        keys = gathering(during_out)
        h, got = _ffn_fwd_out(s, w, h, f"ffn{f}_fwd_out_{i}", gather=[blocks[k] for k in keys])
        W.update(zip(keys, got))
        return a, b, s, h

    saved = []
    h = x
    for i in range(L):
        j = i // 2
        lru = i % 2 == 0
        s = {"h0": h}
        mixer = [("lru_in", j), ("lru_out", j)] if lru else [("pool_w", j)]
        s["a1"], s["b1"], s["s1"], h = ffn_fwd(h, W["ffn1_norm"][i], i, 1, [("ffn", i, 2)], mixer)
        s["h1"] = h
        if lru:
            hn = _rms_fwd(h, W["mix_norm"][i], BF, f"mix_norm_{i}")
            z = _mm(hn, W[("lru_in", j)][0], "nt", f"lru_in_{i}")
            wa, wx = _block_diag(W["lru_w_a"][j]).astype(BF), _block_diag(W["lru_w_x"][j]).astype(BF)
            xc, r, ig, a, bt = _lru_gates(z, W["lru_conv_w"][j], W["lru_conv_b"][j], wa, wx, W["lru_b_a"][j],
                                          W["lru_b_x"][j], W["lru_a_param"][j], f"lru_gates_{i}")
            hs = _lru_scan(a, bt, False, f"lru_scan_{i}")
            h, y = _lru_out(hs, z, W[("lru_out", j)][0], h, f"lru_out_{i}")
            s.update(hn=hn, z=z, wa=wa, wx=wx, xc=xc, r=r, ig=ig, a=a, hs=hs, y=y)
        else:
            h, s["u"], s["yb"] = _pool_fwd(h, W["mix_norm"][i], W[("pool_w", j)], W["pool_b"][j], W["pool_scale"][j],
                                           f"pool_fwd_{i}")
        s["h2"] = h
        s["a2"], s["b2"], s["s2"], h = ffn_fwd(h, W["ffn2_norm"][i], i, 2, [("ffn", i + 1, 1)] if i + 1 < L else [],
                                               [("ple_gate", i), ("ple_proj", i)])
        s["h3"] = h
        h, s["n4"], s["gate"], s["pp"] = _ple_fwd(h, W["ple_norm"][i], W[("ple_gate", i)][0], W[("ple_proj", i)][0], p[i],
                                                  f"ple_fwd_{i}")
        saved.append(s)

    loss, dh, d_final = _loss_head(h, W["final_norm"], tgt)

    big, recv = {}, {}
    n_lru, n_pool = L // 2 + L % 2, L // 2
    small = {k: [None] * L for k in ("ffn1_norm", "mix_norm", "ffn2_norm", "ple_norm")}
    for k in ("lru_conv_w", "lru_conv_b", "lru_w_a", "lru_b_a", "lru_w_x", "lru_b_x", "lru_a_param"):
        small[k] = [None] * n_lru
    for k in ("pool_b", "pool_scale"):
        small[k] = [None] * n_pool

    def scattering(pieces):
        return [(k, m) for k, m in pieces if k in big] if dist else []

    def ffn_bwd(dh, h_in, g, a, b, sv, i, f, during):
        key, w = ("ffn", i, f), W[("ffn", i, f)]
        out = [scattering(d) for d in during]
        sent = [[(big[k], m) for k, m in o] for o in out]
        da, db, dhb, got0 = _ffn_bwd_act(dh, a, b, w, f"ffn{f}_bwd_act_{i}", scatter=sent[0])
        big[key], got1 = _ffn_bwd_w(da, db, sv, h_in, g, dhb, f"ffn{f}_dw_{i}", scatter=sent[1])
        out.append(scattering([(key, 0)]))
        dh, dg, got2 = _ffn_bwd_in(da, db, w, h_in, g, dh, f"ffn{f}_bwd_in_{i}", scatter=[(big[key], 0)] if dist else [])
        for o, got in zip(out, (got0, got1, got2)):
            recv.update(zip(o, got))
        return dh, dg

    for i in reversed(range(L)):
        j = i // 2
        lru = i % 2 == 0
        s = saved[i]
        dh, dz, dpp, dg = _ple_bwd(dh, s["gate"], s["pp"], s["h3"], W["ple_norm"][i], W[("ple_gate", i)][0],
                                   f"ple_bwd_{i}")
        big[("ple_gate", i)] = _mm(s["n4"], dz, "tn", f"ple_gate_dw_{i}", out_dtype=BF)[None]
        big[("ple_proj", i)] = _mm(dpp, p[i], "tn", f"ple_proj_dw_{i}", out_dtype=BF)[None]
        small["ple_norm"][i] = dg[0]
        above = ("ffn", i + 1, 1)
        dh, dg = ffn_bwd(dh, s["h2"], W["ffn2_norm"][i], s["a2"], s["b2"], s["s2"], i, 2, [
            [(above, 1)], [(above, 2), (("ple_gate", i), None), (("ple_proj", i), None)]])
        small["ffn2_norm"][i] = dg[0]
        if lru:
            big[("lru_out", j)] = _mm(s["y"], dh, "tn", f"lru_out_dw_{i}", out_dtype=BF)[None]
            dy = _mm(dh, W[("lru_out", j)][0], "nt", f"lru_out_dx_{i}")
            dhd, dgb = _lru_bwd_ew(dy, s["hs"], s["z"], f"lru_bwd_ew_{i}")
            lam = _lru_scan(s["a"], dhd, True, f"lru_scan_bwd_{i}")
            dxc, dpa, dpx, dsp, dba, dbx = _lru_gates_bwd(lam, s["hs"], s["r"], s["ig"], s["xc"], s["wa"], s["wx"],
                                                          W["lru_a_param"][j], f"lru_gates_bwd_{i}")
            small["lru_a_param"][j], small["lru_b_a"][j], small["lru_b_x"][j] = dsp[0], dba[0], dbx[0]
            dwa, dwx = _lru_gates_dw(s["xc"], dpa, dpx, f"lru_gates_dw_{i}")
            small["lru_w_a"][j], small["lru_w_x"][j] = _diag_blocks(dwa, LRU_HEADS), _diag_blocks(dwx, LRU_HEADS)
            dxb, dcw, dcb = _lru_conv_bwd(dxc, s["z"], W["lru_conv_w"][j], f"lru_conv_bwd_{i}")
            small["lru_conv_w"][j], small["lru_conv_b"][j] = dcw[:CONV_WIDTH], dcb[0]
            big[("lru_in", j)] = jnp.concatenate([_mm(dgb, s["hn"], "tn", f"lru_in_dw_g_{i}", out_dtype=BF),
                                                  _mm(dxb, s["hn"], "tn", f"lru_in_dw_x_{i}", out_dtype=BF)])[None]
            dh, dg = _lru_in_bwd(dgb, dxb, W[("lru_in", j)][0], s["h1"], W["mix_norm"][i], dh, f"lru_in_bwd_{i}")
            mixer = [("lru_in", j), ("lru_out", j)]
        else:
            du, v, dw, dbp, dsc = _pool_bwd(dh, s["u"], s["yb"], W[("pool_w", j)], W["pool_scale"][j], f"pool_bwd_{i}")
            big[("pool_w", j)] = dw.astype(BF)
            small["pool_b"][j], small["pool_scale"][j] = dbp[0], dsc[0]
            dh, dg = _pool_bwd_win(v, du, s["h1"], W["mix_norm"][i], dh, f"pool_bwd_win_{i}")
            mixer = [("pool_w", j)]
        small["mix_norm"][i] = dg[0]
        second = ("ffn", i, 2)
        dh, dg = ffn_bwd(dh, s["h0"], W["ffn1_norm"][i], s["a1"], s["b1"], s["s1"], i, 1, [
            [(second, 1)], [(second, 2)] + [(k, None) for k in mixer]])
        small["ffn1_norm"][i] = dg[0]

    small = {k: jnp.stack(v) for k, v in small.items()}
    small["final_norm"] = d_final[0]
    return loss, dh, big, recv, small


SMALL_SHARDED = ("pool_b", "pool_scale", "lru_conv_w")
SMALL = ("ffn1_norm", "mix_norm", "ffn2_norm", "ple_norm", "final_norm", "lru_conv_b", "lru_w_a", "lru_b_a",
         "lru_w_x", "lru_b_x", "lru_a_param", "pool_b", "pool_scale", "lru_conv_w")


def _pack_big(w):
    t = lambda a: jnp.swapaxes(a, -1, -2)
    out = {}
    for i in range(w["ffn1_norm"].shape[0]):
        for f in (1, 2):
            out[("ffn", i, f)] = jnp.stack([t(w[f"ffn{f}_w_gate"][i]), t(w[f"ffn{f}_w_up"][i]), w[f"ffn{f}_w_down"][i]])
        out[("ple_gate", i)], out[("ple_proj", i)] = w["ple_w_gate"][i][None], t(w["ple_w_proj"][i])[None]
    for j in range(w["lru_w_in"].shape[0]):
        out[("lru_in", j)], out[("lru_out", j)] = t(w["lru_w_in"][j])[None], w["lru_w_out"][j][None]
    for j in range(w["pool_w"].shape[0]):
        out[("pool_w", j)] = w["pool_w"][j]
    return out


def _unpack_big(b, L):
    t = lambda a: jnp.swapaxes(a, -1, -2)
    n_lru, n_pool = L // 2 + L % 2, L // 2
    out = {"lru_w_in": jnp.stack([t(b[("lru_in", j)][0]) for j in range(n_lru)]),
           "lru_w_out": jnp.stack([b[("lru_out", j)][0] for j in range(n_lru)]),
           "pool_w": jnp.stack([b[("pool_w", j)] for j in range(n_pool)]),
           "ple_w_gate": jnp.stack([b[("ple_gate", i)][0] for i in range(L)]),
           "ple_w_proj": jnp.stack([t(b[("ple_proj", i)][0]) for i in range(L)])}
    for f in (1, 2):
        out[f"ffn{f}_w_gate"] = jnp.stack([t(b[("ffn", i, f)][0]) for i in range(L)])
        out[f"ffn{f}_w_up"] = jnp.stack([t(b[("ffn", i, f)][1]) for i in range(L)])
        out[f"ffn{f}_w_down"] = jnp.stack([b[("ffn", i, f)][2] for i in range(L)])
    return out


def _flatten(parts, names, rows_of=LANES):
    flat = jnp.concatenate([parts[k].reshape(-1) for k in names])
    pad = (-flat.size) % (16 * rows_of)
    return jnp.pad(flat, (0, pad)).reshape(1, -1, rows_of)


def _unflatten(flat, like, names):
    out, o = {}, 0
    flat = flat.reshape(-1)
    for k in names:
        n = like[k].size
        out[k] = flat[o:o + n].reshape(like[k].shape)
        o += n
    return out


def kernel(x, p, ffn1_norm, ffn1_w_gate, ffn1_w_up, ffn1_w_down, mix_norm, lru_w_in, lru_conv_w, lru_conv_b, lru_w_a, lru_b_a, lru_w_x, lru_b_x, lru_a_param, lru_w_out, pool_w, pool_b, pool_scale, ffn2_norm, ffn2_w_gate, ffn2_w_up, ffn2_w_down, ple_norm, ple_w_gate, ple_w_proj, final_norm, loss_target, m_ffn1_norm, m_ffn1_w_gate, m_ffn1_w_up, m_ffn1_w_down, m_mix_norm, m_lru_w_in, m_lru_conv_w, m_lru_conv_b, m_lru_w_a, m_lru_b_a, m_lru_w_x, m_lru_b_x, m_lru_a_param, m_lru_w_out, m_pool_w, m_pool_b, m_pool_scale, m_ffn2_norm, m_ffn2_w_gate, m_ffn2_w_up, m_ffn2_w_down, m_ple_norm, m_ple_w_gate, m_ple_w_proj, m_final_norm, v_ffn1_norm, v_ffn1_w_gate, v_ffn1_w_up, v_ffn1_w_down, v_mix_norm, v_lru_w_in, v_lru_conv_w, v_lru_conv_b, v_lru_w_a, v_lru_b_a, v_lru_w_x, v_lru_b_x, v_lru_a_param, v_lru_w_out, v_pool_w, v_pool_b, v_pool_scale, v_ffn2_norm, v_ffn2_w_gate, v_ffn2_w_up, v_ffn2_w_down, v_ple_norm, v_ple_w_gate, v_ple_w_proj, v_final_norm):
    names = ["ffn1_norm", "ffn1_w_gate", "ffn1_w_up", "ffn1_w_down", "mix_norm", "lru_w_in", "lru_conv_w", "lru_conv_b",
             "lru_w_a", "lru_b_a", "lru_w_x", "lru_b_x", "lru_a_param", "lru_w_out", "pool_w", "pool_b", "pool_scale",
             "ffn2_norm", "ffn2_w_gate", "ffn2_w_up", "ffn2_w_down", "ple_norm", "ple_w_gate", "ple_w_proj", "final_norm"]
    w = dict(zip(names, [ffn1_norm, ffn1_w_gate, ffn1_w_up, ffn1_w_down, mix_norm, lru_w_in, lru_conv_w, lru_conv_b, lru_w_a, lru_b_a, lru_w_x, lru_b_x, lru_a_param, lru_w_out, pool_w, pool_b, pool_scale, ffn2_norm, ffn2_w_gate, ffn2_w_up, ffn2_w_down, ple_norm, ple_w_gate, ple_w_proj, final_norm]))
    m = dict(zip(names, [m_ffn1_norm, m_ffn1_w_gate, m_ffn1_w_up, m_ffn1_w_down, m_mix_norm, m_lru_w_in, m_lru_conv_w, m_lru_conv_b, m_lru_w_a, m_lru_b_a, m_lru_w_x, m_lru_b_x, m_lru_a_param, m_lru_w_out, m_pool_w, m_pool_b, m_pool_scale, m_ffn2_norm, m_ffn2_w_gate, m_ffn2_w_up, m_ffn2_w_down, m_ple_norm, m_ple_w_gate, m_ple_w_proj, m_final_norm]))
    v = dict(zip(names, [v_ffn1_norm, v_ffn1_w_gate, v_ffn1_w_up, v_ffn1_w_down, v_mix_norm, v_lru_w_in, v_lru_conv_w, v_lru_conv_b, v_lru_w_a, v_lru_b_a, v_lru_w_x, v_lru_b_x, v_lru_a_param, v_lru_w_out, v_pool_w, v_pool_b, v_pool_scale, v_ffn2_norm, v_ffn2_w_gate, v_ffn2_w_up, v_ffn2_w_down, v_ple_norm, v_ple_w_gate, v_ple_w_proj, v_final_norm]))
    L = p.shape[0]
    px, py, pc = _position()
    me = 4 * px + 2 * py + pc

    blocks = {k: b.astype(BF) for k, b in _pack_big(w).items()}
    first = ("ffn", 0, 1)
    got, small_blocks = _all_gather([blocks[first], _flatten(w, SMALL_SHARDED)], "gather_first")
    W = {first: got}
    per_dev = small_blocks.reshape(N_DEV, -1)
    shards = [_unflatten(per_dev[k], w, SMALL_SHARDED) for k in range(N_DEV)]
    for k in SMALL:
        W[k] = jnp.concatenate([s[k] for s in shards], axis=-1) if k in SMALL_SHARDED else w[k]

    loss, dx, big, recv, small = _local_step(x[0], p[:, 0], loss_target[0], W, blocks)

    last = [(k, m) for k in big if (k, None) not in recv for m in range(big[k].shape[0]) if (k, m) not in recv]
    got, (parts,) = _scatter_and_gather([(big[k], m) for k, m in last], [_flatten(small, SMALL)], "scatter_last_gather_small")
    recv.update(zip(last, got))

    def total(k):
        tag = "sum_" + "_".join(map(str, k))
        if (k, None) in recv:
            return _scatter_sum(recv[(k, None)], tag)
        return jnp.concatenate([_scatter_sum(recv[(k, m)], f"{tag}_{m}") for m in range(big[k].shape[0])])

    grads = _unpack_big({k: total(k) for k in big}, L)
    total_small = _sum_devices(parts.reshape(N_DEV, -1, LANES), "sum_small_grads")
    full = _unflatten(total_small, {k: W[k] for k in SMALL}, SMALL)
    for k in SMALL:
        if k in SMALL_SHARDED:
            n = w[k].shape[-1]
            grads[k] = lax.dynamic_slice_in_dim(full[k], me * n, n, axis=-1)
        else:
            grads[k] = full[k]

    delta, new_m, new_v = {}, {}, {}
    for k in names:
        delta[k], new_m[k], new_v[k] = _adamw(w[k], grads[k], m[k], v[k], f"adamw_{k}")
    total_loss = lax.psum(loss[0, 0], ("x", "y", "c"))
    return (total_loss, dx[None], *[grads[k] for k in names], *[delta[k] for k in names],
            *[new_m[k] for k in names], *[new_v[k] for k in names])
```

```python
import functools

import jax
import jax.numpy as jnp
from jax import lax
from jax.experimental import pallas as pl
from jax.experimental.pallas import tpu as pltpu

F32 = jnp.float32
BF = jnp.bfloat16
MESH = pl.DeviceIdType.MESH

RMS_EPS = 1e-6
LRU_C = 8.0
LRU_HEADS = 16
CONV_WIDTH = 4
POOL_WINDOWS = (2, 4, 8, 16)
ADAM_LR, ADAM_B1, ADAM_B2, ADAM_EPS, ADAM_WD, ADAM_STEP = 0.001, 0.9, 0.999, 1e-08, 0.01, 10

N_DEV = 8
LANES = 128
SUBLANES = 8
GATE_COLS, GATE_SPAN = 256, 512
HALO = 16
VMEM_LIMIT = 56 * 1024 * 1024

TM_FFN = 1024
TM_FFN_ACT = 2048
TM_FFN_IN = 512
TF_FFN = 256
TF_FFN_WG = 1408
TK_FFN_WG = 512
TB_SEQ = 256
TB_SCAN = 256
TC_SCAN = 640
TM_EW = 512
TM_MM, TN_MM, TK_MM = 1024, 512, 1024
TR_ADAM = 512


def _tile(n, pref, align):
    if n <= pref:
        return n
    t = (pref // align) * align
    while t >= align:
        if n % t == 0:
            return t
        t -= align
    raise ValueError(f"no tile for {n} (pref {pref}, align {align})")


def _params(*sem):
    return pltpu.CompilerParams(dimension_semantics=sem, vmem_limit_bytes=VMEM_LIMIT)


def _dot(a, b):
    return lax.dot_general(a, b, (((1,), (0,)), ((), ())), preferred_element_type=F32)


def _dot_nt(a, b):
    return lax.dot_general(a, b, (((1,), (1,)), ((), ())), preferred_element_type=F32)


def _dot_tn(a, b):
    return lax.dot_general(a, b, (((0,), (0,)), ((), ())), preferred_element_type=F32)


def _sigmoid(x):
    return 1.0 / (1.0 + jnp.exp(-x))


def _gelu_parts(x):
    k0, k1 = 0.7978845608028654, 0.044715
    t = jnp.tanh(k0 * (x + k1 * x * x * x))
    g = 0.5 * x * (1.0 + t)
    dg = 0.5 * (1.0 + t) + 0.5 * x * (1.0 - t * t) * k0 * (1.0 + 3.0 * k1 * x * x)
    return g, dg


def _neg_expm1(x):
    p = x * (1.0 + x * (0.5 + x * (1.0 / 6 + x * (1.0 / 24 + x * (1.0 / 120 + x * (1.0 / 720 + x * (1.0 / 5040)))))))
    return jnp.where(x > -0.35, -p, 1.0 - jnp.exp(x))


def _softplus_neg(l):
    u = jnp.exp(-jnp.abs(l))
    w = 1.0 + u
    log1p = jnp.where(w == 1.0, u, jnp.log(w) * (u / jnp.where(w == 1.0, 1.0, w - 1.0)))
    return jnp.maximum(-l, 0.0) + log1p


def _rms_parts(x, g):
    r = lax.rsqrt(jnp.mean(x * x, axis=-1, keepdims=True) + RMS_EPS)
    nhat = x * r
    return nhat * g, nhat, r


def _rms_bwd_parts(x, g, dn):
    _, nhat, r = _rms_parts(x, g)
    u = dn * g
    dx = r * (u - nhat * jnp.mean(u * nhat, axis=-1, keepdims=True))
    return dx, jnp.sum(dn * nhat, axis=0, keepdims=True)


def _row_spec(tm, d, single=False):
    if single:
        return pl.BlockSpec((tm, d), lambda i, *_: (i, 0), pipeline_mode=pl.Buffered(1))
    return pl.BlockSpec((tm, d), lambda i, *_: (i, 0))


def _vec_spec(d, rows=1):
    return pl.BlockSpec((rows, d), lambda *_: (0, 0))


def _mm(x, w, mode, name, out_dtype=F32, res=None, alpha=1.0, tm=None, tn=None, tk=None):
    if mode == "nn":
        (M, K), (_, N) = x.shape, w.shape
    elif mode == "nt":
        (M, K), (N, _) = x.shape, w.shape
    else:
        (K, M), (_, N) = x.shape, w.shape
    tm = _tile(M, tm or TM_MM, LANES if mode == "tn" else SUBLANES)
    tn = _tile(N, tn or TN_MM, LANES)
    tk = _tile(K, tk or TK_MM, LANES if mode != "tn" else 16)
    nk = K // tk
    dot = {"nn": _dot, "nt": _dot_nt, "tn": _dot_tn}[mode]

    def body(*refs):
        if res is None:
            x_ref, w_ref, o_ref, acc = refs
        else:
            x_ref, w_ref, r_ref, o_ref, acc = refs
        k = pl.program_id(2)

        @pl.when(k == 0)
        def _():
            acc[...] = jnp.zeros_like(acc)

        acc[...] += dot(x_ref[...].astype(BF), w_ref[...].astype(BF))

        @pl.when(k == nk - 1)
        def _():
            r = acc[...] if alpha == 1.0 else acc[...] * alpha
            if res is not None:
                r = r_ref[...] + r
            o_ref[...] = r.astype(out_dtype)

    if mode == "nn":
        specs = [pl.BlockSpec((tm, tk), lambda i, j, k: (i, k)), pl.BlockSpec((tk, tn), lambda i, j, k: (k, j))]
    elif mode == "nt":
        specs = [pl.BlockSpec((tm, tk), lambda i, j, k: (i, k)), pl.BlockSpec((tn, tk), lambda i, j, k: (j, k))]
    else:
        specs = [pl.BlockSpec((tk, tm), lambda i, j, k: (k, i)), pl.BlockSpec((tk, tn), lambda i, j, k: (k, j))]
    args = [x, w]
    if res is not None:
        specs.append(pl.BlockSpec((tm, tn), lambda i, j, k: (i, j)))
        args.append(res)
    return pl.pallas_call(
        body, name=name, grid=(M // tm, N // tn, nk), in_specs=specs,
        out_specs=pl.BlockSpec((tm, tn), lambda i, j, k: (i, j)),
        out_shape=jax.ShapeDtypeStruct((M, N), out_dtype),
        scratch_shapes=[pltpu.VMEM((tm, tn), F32)],
        compiler_params=_params("parallel", "parallel", "arbitrary"),
    )(*args)


def _rms_fwd(h, g, out_dtype, name):
    T, D = h.shape
    tm = _tile(T, TM_EW, 16)

    def body(h_ref, g_ref, o_ref):
        o_ref[...] = _rms_parts(h_ref[...], g_ref[...])[0].astype(out_dtype)

    return pl.pallas_call(
        body, name=name, grid=(T // tm,), in_specs=[_row_spec(tm, D), _vec_spec(D)], out_specs=_row_spec(tm, D),
        out_shape=jax.ShapeDtypeStruct((T, D), out_dtype), compiler_params=_params("parallel"),
    )(h, g.reshape(1, D))


def _loss_head(h, g, tgt):
    T, D = h.shape
    tm = _tile(T, TM_EW, 16)

    def body(h_ref, g_ref, t_ref, loss_ref, dh_ref, dg_ref):
        @pl.when(pl.program_id(0) == 0)
        def _():
            dg_ref[...] = jnp.zeros_like(dg_ref)
            loss_ref[...] = jnp.zeros_like(loss_ref)

        x, gg = h_ref[...], g_ref[...]
        y = _rms_parts(x, gg)[0]
        e = y - t_ref[...]
        part = jnp.sum(jnp.sum(e * e, axis=0, keepdims=True), axis=1, keepdims=True) * (0.5 / D)
        loss_ref[...] += jnp.broadcast_to(part, loss_ref.shape)
        dx, dg = _rms_bwd_parts(x, gg, e * (1.0 / D))
        dh_ref[...] = dx
        dg_ref[...] += dg

    return pl.pallas_call(
        body, name="loss_head", grid=(T // tm,),
        in_specs=[_row_spec(tm, D), _vec_spec(D), _row_spec(tm, D)],
        out_specs=[_vec_spec(LANES), _row_spec(tm, D), _vec_spec(D)],
        out_shape=[jax.ShapeDtypeStruct((1, LANES), F32), jax.ShapeDtypeStruct((T, D), F32),
                   jax.ShapeDtypeStruct((1, D), F32)],
        compiler_params=_params("arbitrary"),
    )(h, g.reshape(1, D), tgt)


def _carry(plan, first, mid, last):
    pl.when(first)(plan[0])
    if len(plan) == 3:
        pl.when(mid)(plan[1])
    pl.when(last)(plan[-1])


def _ffn_fwd_act(h, g, wffn, name, gather=()):
    T, D = h.shape
    F = wffn.shape[1]
    tm, tf = _tile(T, TM_FFN_ACT, 16), _tile(F, TF_FFN, LANES)
    ni, nf, ng = T // tm, F // tf, len(gather)

    def body(*refs):
        h_ref, g_ref, wg_ref, wu_ref = refs[:4]
        srcs, (a_ref, b_ref, s_ref), outs = refs[4:4 + ng], refs[4 + ng:7 + ng], refs[7 + ng:7 + 2 * ng]
        n_sc = refs[7 + 2 * ng]
        i, j = pl.program_id(0), pl.program_id(1)
        if ng:
            _carry(_gather_plan(gather, srcs, outs, *refs[8 + 2 * ng:]), jnp.logical_and(i == 0, j == 0),
                   jnp.logical_and(i == (3 * ni) // 4, j == 0), jnp.logical_and(i == ni - 1, j == nf - 1))

        @pl.when(j == 0)
        def _():
            n_sc[...] = _rms_parts(h_ref[...], g_ref[...])[0].astype(BF)

        n = n_sc[...]
        a = _dot_nt(n, wg_ref[...])
        b = _dot_nt(n, wu_ref[...])
        a_ref[...] = a.astype(BF)
        b_ref[...] = b.astype(BF)
        s_ref[...] = (a * _sigmoid(a) * b).astype(BF)

    tile = pl.BlockSpec((tm, tf), lambda i, j: (i, j))
    w = [pl.BlockSpec((None, tf, D), functools.partial(lambda k, i, j: (k, j, 0), k)) for k in (0, 1)]
    hbm = pl.BlockSpec(memory_space=pl.ANY)
    outs = pl.pallas_call(
        body, name=name, grid=(ni, nf), in_specs=[_row_spec(tm, D), _vec_spec(D)] + w + [hbm] * ng,
        out_specs=[tile, tile, tile] + [hbm] * ng,
        out_shape=[jax.ShapeDtypeStruct((T, F), BF)] * 3 + _gathered_shapes(gather),
        scratch_shapes=[pltpu.VMEM((tm, D), BF)] + _gather_sems(ng),
        compiler_params=_params("arbitrary", "arbitrary"),
    )(h, g.reshape(1, D), wffn, wffn, *gather)
    return outs[0], outs[1], outs[2], list(outs[3:])


def _ffn_fwd_out(s, wffn, h, name, gather=()):
    T, F = s.shape
    D = h.shape[1]
    tm = _tile(T, TM_FFN_IN, 16)
    ni, ng = T // tm, len(gather)

    def body(*refs):
        s_ref, w_ref, h_ref = refs[:3]
        srcs, o_ref, outs = refs[3:3 + ng], refs[3 + ng], refs[4 + ng:4 + 2 * ng]
        i = pl.program_id(0)
        if ng:
            _carry(_gather_plan(gather, srcs, outs, *refs[4 + 2 * ng:]), i == 0, i == (3 * ni) // 4, i == ni - 1)
        o_ref[...] = h_ref[...] + 0.5 * _dot(s_ref[...], w_ref[...])

    hbm = pl.BlockSpec(memory_space=pl.ANY)
    outs = pl.pallas_call(
        body, name=name, grid=(ni,),
        in_specs=[_row_spec(tm, F), pl.BlockSpec((None, F, D), lambda i: (2, 0, 0), pipeline_mode=pl.Buffered(1)),
                  _row_spec(tm, D)] + [hbm] * ng,
        out_specs=[_row_spec(tm, D)] + [hbm] * ng,
        out_shape=[jax.ShapeDtypeStruct((T, D), F32)] + _gathered_shapes(gather),
        scratch_shapes=_gather_sems(ng), compiler_params=_params("arbitrary"),
    )(s, wffn, h, *gather)
    return outs[0], list(outs[1:])


def _ffn_bwd_act(dh, a, b, wffn, name, scatter=()):
    T, D = dh.shape
    F = wffn.shape[1]
    tm, tf = _tile(T, TM_FFN_ACT, 16), _tile(F, TF_FFN, LANES)
    ni, nf, ng = T // tm, F // tf, len(scatter)

    def body(*refs):
        dh_ref, a_ref, b_ref, wd_ref = refs[:4]
        srcs, (da_ref, db_ref, dhb_ref), outs = refs[4:4 + ng], refs[4 + ng:7 + ng], refs[7 + ng:7 + 2 * ng]
        i, j = pl.program_id(0), pl.program_id(1)
        if ng:
            _carry(_scatter_plan(scatter, srcs, outs, *refs[7 + 2 * ng:]), jnp.logical_and(i == 0, j == 0), None,
                   jnp.logical_and(i == ni - 1, j == nf - 1))

        @pl.when(j == 0)
        def _():
            dhb_ref[...] = dh_ref[...].astype(BF)

        ds = 0.5 * _dot_nt(dhb_ref[...], wd_ref[...])
        av, bv = a_ref[...].astype(F32), b_ref[...].astype(F32)
        sig = _sigmoid(av)
        da_ref[...] = (ds * bv * (sig * (1.0 + av * (1.0 - sig)))).astype(BF)
        db_ref[...] = (ds * (av * sig)).astype(BF)

    tile = pl.BlockSpec((tm, tf), lambda i, j: (i, j))
    hbm = pl.BlockSpec(memory_space=pl.ANY)
    outs = pl.pallas_call(
        body, name=name, grid=(ni, nf),
        in_specs=[_row_spec(tm, D), tile, tile, pl.BlockSpec((None, tf, D), lambda i, j: (2, j, 0))] + [hbm] * ng,
        out_specs=[tile, tile, _row_spec(tm, D)] + [hbm] * ng,
        out_shape=[jax.ShapeDtypeStruct((T, F), BF)] * 2 + [jax.ShapeDtypeStruct((T, D), BF)]
        + _scattered_shapes(scatter),
        scratch_shapes=_scatter_sems(ng), compiler_params=_params("arbitrary", "arbitrary"),
    )(dh, a, b, wffn, *[piece[0] for piece in scatter])
    return outs[0], outs[1], outs[2], list(outs[3:])


def _two_dot_norm_bwd(x1, x2, w, w_specs, h, g, dh, name, scatter=()):
    T, K = x1.shape
    D = h.shape[1]
    tm = _tile(T, TM_FFN_IN, 16)
    ni, ng = T // tm, len(scatter)

    def body(*refs):
        x1_ref, x2_ref, w1_ref, w2_ref, h_ref, g_ref, dh_ref = refs[:7]
        srcs, (o_ref, dg_ref), outs = refs[7:7 + ng], refs[7 + ng:9 + ng], refs[9 + ng:9 + 2 * ng]
        i = pl.program_id(0)
        if ng:
            _carry(_scatter_plan(scatter, srcs, outs, *refs[9 + 2 * ng:]), i == 0, None, i == ni - 1)

        @pl.when(i == 0)
        def _():
            dg_ref[...] = jnp.zeros_like(dg_ref)

        dn = _dot(x1_ref[...], w1_ref[...]) + _dot(x2_ref[...], w2_ref[...])
        dx, dg = _rms_bwd_parts(h_ref[...], g_ref[...], dn)
        o_ref[...] = dh_ref[...] + dx
        dg_ref[...] += dg

    hbm = pl.BlockSpec(memory_space=pl.ANY)
    act = pl.BlockSpec((tm, K), lambda i: (i, 0))
    outs = pl.pallas_call(
        body, name=name, grid=(ni,),
        in_specs=[act, act] + w_specs + [_row_spec(tm, D), _vec_spec(D), _row_spec(tm, D)] + [hbm] * ng,
        out_specs=[_row_spec(tm, D), _vec_spec(D)] + [hbm] * ng,
        out_shape=[jax.ShapeDtypeStruct((T, D), F32), jax.ShapeDtypeStruct((1, D), F32)] + _scattered_shapes(scatter),
        scratch_shapes=_scatter_sems(ng), compiler_params=_params("arbitrary"),
    )(x1, x2, w, w, h, g.reshape(1, D), dh, *[piece[0] for piece in scatter])
    return outs[0], outs[1], list(outs[2:])


def _ffn_bwd_in(da, db, wffn, h, g, dh, name, scatter=()):
    F, D = wffn.shape[1:]
    specs = [pl.BlockSpec((None, F, D), functools.partial(lambda k, i: (k, 0, 0), k), pipeline_mode=pl.Buffered(1))
             for k in (0, 1)]
    return _two_dot_norm_bwd(da, db, wffn, specs, h, g, dh, name, scatter)


def _lru_in_bwd(dgb, dxb, win, h, g, dh, name):
    R, D = win.shape[0] // 2, win.shape[1]
    specs = [pl.BlockSpec((R, D), functools.partial(lambda k, i: (k, 0), k), pipeline_mode=pl.Buffered(1)) for k in (0, 1)]
    return _two_dot_norm_bwd(dgb, dxb, win, specs, h, g, dh, name)[:2]


def _ffn_bwd_w(da, db, s, h, g, dhb, name, scatter=()):
    T, F = da.shape
    D = h.shape[1]
    tf, tk = _tile(F, TF_FFN_WG, LANES), _tile(T, TK_FFN_WG, 16)
    nj, nk, ng = F // tf, T // tk, len(scatter)

    def body(*refs):
        da_ref, db_ref, s_ref, h_ref, g_ref, dh_ref = refs[:6]
        srcs, o_ref, outs = refs[6:6 + ng], refs[6 + ng], refs[7 + ng:7 + 2 * ng]
        g_sc, u_sc, d_sc = refs[7 + 2 * ng:10 + 2 * ng]
        j, k = pl.program_id(0), pl.program_id(1)
        if ng:
            _carry(_scatter_plan(scatter, srcs, outs, *refs[10 + 2 * ng:]), jnp.logical_and(j == 0, k == 0), None,
                   jnp.logical_and(j == nj - 1, k == nk - 1))

        @pl.when(k == 0)
        def _():
            g_sc[...] = jnp.zeros_like(g_sc)
            u_sc[...] = jnp.zeros_like(u_sc)
            d_sc[...] = jnp.zeros_like(d_sc)

        nv = _rms_parts(h_ref[...], g_ref[...])[0].astype(BF)
        g_sc[...] += _dot_tn(da_ref[...], nv)
        u_sc[...] += _dot_tn(db_ref[...], nv)
        d_sc[...] += _dot_tn(s_ref[...], dh_ref[...])

        @pl.when(k == nk - 1)
        def _():
            o_ref[0] = g_sc[...].astype(BF)
            o_ref[1] = u_sc[...].astype(BF)
            o_ref[2] = (0.5 * d_sc[...]).astype(BF)

    act = pl.BlockSpec((tk, tf), lambda j, k: (k, j))
    tok = pl.BlockSpec((tk, D), lambda j, k: (k, 0))
    hbm = pl.BlockSpec(memory_space=pl.ANY)
    outs = pl.pallas_call(
        body, name=name, grid=(nj, nk), in_specs=[act, act, act, tok, _vec_spec(D), tok] + [hbm] * ng,
        out_specs=[pl.BlockSpec((3, tf, D), lambda j, k: (0, j, 0), pipeline_mode=pl.Buffered(1))] + [hbm] * ng,
        out_shape=[jax.ShapeDtypeStruct((3, F, D), BF)] + _scattered_shapes(scatter),
        scratch_shapes=[pltpu.VMEM((tf, D), F32)] * 3 + _scatter_sems(ng),
        compiler_params=_params("arbitrary", "arbitrary"),
    )(da, db, s, h, g.reshape(1, D), dhb, *[piece[0] for piece in scatter])
    return outs[0], list(outs[1:])


def _ple_fwd(h, g, wg, wp, p, name):
    T, D = h.shape
    P = p.shape[1]
    tm = _tile(T, TM_EW, 16)

    def body(h_ref, g_ref, wg_ref, wp_ref, p_ref, o_ref, n_ref, gate_ref, pp_ref):
        x = h_ref[...]
        n = _rms_parts(x, g_ref[...])[0].astype(BF)
        gate = _sigmoid(_dot(n, wg_ref[...]))
        pp = _dot_nt(p_ref[...].astype(BF), wp_ref[...])
        o_ref[...] = x + gate * pp
        n_ref[...] = n
        gate_ref[...] = gate.astype(BF)
        pp_ref[...] = pp.astype(BF)

    row = _row_spec(tm, D)
    return pl.pallas_call(
        body, name=name, grid=(T // tm,),
        in_specs=[row, _vec_spec(D), _vec_spec(D, D), _vec_spec(P, D), _row_spec(tm, P)], out_specs=[row] * 4,
        out_shape=[jax.ShapeDtypeStruct((T, D), F32)] + [jax.ShapeDtypeStruct((T, D), BF)] * 3,
        compiler_params=_params("parallel"),
    )(h, g.reshape(1, D), wg, wp, p)


def _ple_bwd(dh, gate, pp, h, g, wg, name):
    T, D = dh.shape
    tm = _tile(T, TM_EW, 16)

    def body(dh_ref, gate_ref, pp_ref, h_ref, g_ref, wg_ref, o_ref, dz_ref, dp_ref, dg_ref):
        @pl.when(pl.program_id(0) == 0)
        def _():
            dg_ref[...] = jnp.zeros_like(dg_ref)

        d, gate = dh_ref[...], gate_ref[...].astype(F32)
        dz = (d * pp_ref[...].astype(F32) * gate * (1.0 - gate)).astype(BF)
        dx, dg = _rms_bwd_parts(h_ref[...], g_ref[...], _dot_nt(dz, wg_ref[...]))
        o_ref[...] = d + dx
        dz_ref[...] = dz
        dp_ref[...] = (d * gate).astype(BF)
        dg_ref[...] += dg

    row = _row_spec(tm, D)
    return pl.pallas_call(
        body, name=name, grid=(T // tm,), in_specs=[row, row, row, row, _vec_spec(D), _vec_spec(D, D)],
        out_specs=[row, row, row, _vec_spec(D)],
        out_shape=[jax.ShapeDtypeStruct((T, D), F32), jax.ShapeDtypeStruct((T, D), BF), jax.ShapeDtypeStruct((T, D), BF),
                   jax.ShapeDtypeStruct((1, D), F32)],
        compiler_params=_params("arbitrary"),
    )(dh, gate, pp, h, g.reshape(1, D), wg)


def _lru_fwd(z, conv_w, conv_b, wa, wx, b_a, b_x, a_param, wout, h, name):
    T, R2 = z.shape
    R, D = R2 // 2, h.shape[1]
    tb = _tile(T, TB_SEQ, HALO)
    per, ng = tb // HALO, tb // SUBLANES

    def body(g_ref, x_ref, halo_ref, cw_ref, cb_ref, wa_ref, wx_ref, ba_ref, bx_ref, ap_ref, wo_ref, h_ref,
             o_ref, xc_ref, r_ref, ig_ref, a_ref, hs_ref, y_ref, ext, carry, a_sc, b_sc):
        i = pl.program_id(0)

        @pl.when(i == 0)
        def _():
            carry[...] = jnp.zeros_like(carry)

        ext[pl.ds(0, HALO), :] = jnp.where(i > 0, halo_ref[...], 0.0)
        ext[pl.ds(HALO, tb), :] = x_ref[...]
        xc = cb_ref[...] + cw_ref[0:1, :] * ext[pl.ds(HALO - 3, tb), :]
        for k in range(1, CONV_WIDTH):
            xc = xc + cw_ref[k:k + 1, :] * ext[pl.ds(HALO - 3 + k, tb), :]
        xcb = xc.astype(BF)
        r = _sigmoid(_dot(xcb, wa_ref[...]) + ba_ref[...])
        ig = _sigmoid(_dot(xcb, wx_ref[...]) + bx_ref[...])
        la = -LRU_C * r * _softplus_neg(ap_ref[...])
        av = jnp.exp(la)
        xc_ref[...] = xc
        r_ref[...] = r
        ig_ref[...] = ig
        a_ref[...] = av
        A = av.reshape(ng, SUBLANES, R)
        B = (jnp.sqrt(_neg_expm1(2.0 * la)) * (ig * xc)).reshape(ng, SUBLANES, R)
        sub = lax.broadcasted_iota(jnp.int32, (1, SUBLANES, R), 1)
        for k in (1, 2, 4):
            m = sub >= k
            a_n = jnp.where(m, pltpu.roll(A, k, 1), 1.0)
            b_n = jnp.where(m, pltpu.roll(B, k, 1), 0.0)
            B = A * b_n + B
            A = A * a_n
        a_sc[...] = A.reshape(tb, R)
        b_sc[...] = B.reshape(tb, R)

        def group(q, c):
            rows = pl.ds(pl.multiple_of(q * SUBLANES, SUBLANES), SUBLANES)
            hg = a_sc[rows, :] * c + b_sc[rows, :]
            hs_ref[rows, :] = hg
            return hg[SUBLANES - 1:SUBLANES, :]

        carry[...] = lax.fori_loop(0, ng, group, carry[...])
        y = (hs_ref[...] * _gelu_parts(g_ref[...])[0]).astype(BF)
        y_ref[...] = y
        o_ref[...] = h_ref[...] + _dot(y, wo_ref[...])

    once = lambda rows, cols: pl.BlockSpec((rows, cols), lambda i: (0, 0), pipeline_mode=pl.Buffered(1))
    gate = pl.BlockSpec((tb, R), lambda i: (i, 0))
    tile = pl.BlockSpec((tb, R), lambda i: (i, 1))
    halo = pl.BlockSpec((HALO, R), lambda i: (jnp.maximum(i * per - 1, 0), 1))
    return pl.pallas_call(
        body, name=name, grid=(T // tb,),
        in_specs=[gate, tile, halo, _vec_spec(R, CONV_WIDTH), _vec_spec(R), once(R, R), once(R, R), _vec_spec(R),
                  _vec_spec(R), _vec_spec(R), once(R, D), _row_spec(tb, D)],
        out_specs=[_row_spec(tb, D)] + [gate] * 6,
        out_shape=[jax.ShapeDtypeStruct((T, D), F32)] + [jax.ShapeDtypeStruct((T, R), F32)] * 5
        + [jax.ShapeDtypeStruct((T, R), BF)],
        scratch_shapes=[pltpu.VMEM((HALO + tb, R), F32), pltpu.VMEM((1, R), F32), pltpu.VMEM((tb, R), F32),
                        pltpu.VMEM((tb, R), F32)],
        compiler_params=_params("arbitrary"),
    )(z, z, z, conv_w, conv_b.reshape(1, R), wa, wx, b_a.reshape(1, R), b_x.reshape(1, R), a_param.reshape(1, R), wout, h)


def _lru_scan(a, b, reverse, name):
    T, R = a.shape
    tb, tc = _tile(T, TB_SCAN, SUBLANES), _tile(R, TC_SCAN, LANES)
    nt, ng = T // tb, tb // SUBLANES

    def body(a_ref, b_ref, o_ref, carry, a_sc, b_sc):
        @pl.when(pl.program_id(1) == 0)
        def _():
            carry[...] = jnp.zeros_like(carry)

        A = a_ref[...].reshape(ng, SUBLANES, tc)
        B = A * b_ref[...].reshape(ng, SUBLANES, tc) if reverse else b_ref[...].reshape(ng, SUBLANES, tc)
        sub = lax.broadcasted_iota(jnp.int32, (1, SUBLANES, tc), 1)
        for k in (1, 2, 4):
            m = (sub < SUBLANES - k) if reverse else (sub >= k)
            shift = SUBLANES - k if reverse else k
            a_n = jnp.where(m, pltpu.roll(A, shift, 1), 1.0)
            b_n = jnp.where(m, pltpu.roll(B, shift, 1), 0.0)
            B = A * b_n + B
            A = A * a_n
        a_sc[...] = A.reshape(tb, tc)
        b_sc[...] = B.reshape(tb, tc)
        sub8 = lax.broadcasted_iota(jnp.int32, (SUBLANES, tc), 0)

        def group(q, c):
            g = (ng - 1 - q) if reverse else q
            rows = pl.ds(pl.multiple_of(g * SUBLANES, SUBLANES), SUBLANES)
            hg = a_sc[rows, :] * c + b_sc[rows, :]
            if reverse:
                nxt = jnp.where(sub8 == SUBLANES - 1, c, pltpu.roll(hg, SUBLANES - 1, 0))
                o_ref[rows, :] = b_ref[rows, :] + nxt
                return hg[0:1, :]
            o_ref[rows, :] = hg
            return hg[SUBLANES - 1:SUBLANES, :]

        carry[...] = lax.fori_loop(0, ng, group, carry[...])

    spec = pl.BlockSpec((tb, tc), (lambda c, t: (nt - 1 - t, c)) if reverse else (lambda c, t: (t, c)))
    return pl.pallas_call(
        body, name=name, grid=(R // tc, nt), in_specs=[spec, spec], out_specs=spec,
        out_shape=jax.ShapeDtypeStruct((T, R), F32),
        scratch_shapes=[pltpu.VMEM((1, tc), F32), pltpu.VMEM((tb, tc), F32), pltpu.VMEM((tb, tc), F32)],
        compiler_params=_params("parallel", "arbitrary"),
    )(a, b)


def _lru_bwd_ew(dy, hs, z, name):
    T, R = hs.shape
    tm = _tile(T, TM_EW, 16)

    def body(dy_ref, h_ref, g_ref, dhd_ref, dgb_ref):
        g, dg = _gelu_parts(g_ref[...])
        d = dy_ref[...]
        dhd_ref[...] = d * g
        dgb_ref[...] = (d * h_ref[...] * dg).astype(BF)

    return pl.pallas_call(
        body, name=name, grid=(T // tm,), in_specs=[_row_spec(tm, R)] * 3, out_specs=[_row_spec(tm, R)] * 2,
        out_shape=[jax.ShapeDtypeStruct((T, R), F32), jax.ShapeDtypeStruct((T, R), BF)],
        compiler_params=_params("parallel"),
    )(dy, hs, z)


def _lru_gates_bwd(lam, hs, r, ig, xc, wa, wx, a_param, name):
    T, R = lam.shape
    tb = _tile(T, TB_SEQ, HALO)
    per = tb // HALO
    nt = T // tb

    def body(l_ref, h_ref, hh_ref, r_ref, ig_ref, xc_ref, wa_ref, wx_ref, ap_ref,
             dxc_ref, dpa_ref, dpx_ref, dsp_ref, dba_ref, dbx_ref, ext):
        i = pl.program_id(0)

        @pl.when(i == 0)
        def _():
            dsp_ref[...] = jnp.zeros_like(dsp_ref)
            dba_ref[...] = jnp.zeros_like(dba_ref)
            dbx_ref[...] = jnp.zeros_like(dbx_ref)

        ext[pl.ds(0, HALO), :] = jnp.where(i > 0, hh_ref[...], 0.0)
        ext[pl.ds(HALO, tb), :] = h_ref[...]
        h_prev = ext[pl.ds(HALO - 1, tb), :]
        lam_v, rv, igv, xcv = l_ref[...], r_ref[...], ig_ref[...], xc_ref[...]
        sp = _softplus_neg(ap_ref[...])
        la = -LRU_C * rv * sp
        av = jnp.exp(la)
        mult = jnp.sqrt(_neg_expm1(2.0 * la))
        dla = lam_v * h_prev * av - lam_v * (igv * xcv) * (av * av) / mult
        du = lam_v * mult
        dpa = (dla * (-LRU_C) * sp) * rv * (1.0 - rv)
        dpx = (du * xcv) * igv * (1.0 - igv)
        dsp_ref[...] += jnp.sum(dla * (-LRU_C) * rv, axis=0, keepdims=True)
        dba_ref[...] += jnp.sum(dpa, axis=0, keepdims=True)
        dbx_ref[...] += jnp.sum(dpx, axis=0, keepdims=True)
        dpab, dpxb = dpa.astype(BF), dpx.astype(BF)
        dxc_ref[...] = du * igv + _dot_nt(dpab, wa_ref[...]) + _dot_nt(dpxb, wx_ref[...])
        dpa_ref[...] = dpab
        dpx_ref[...] = dpxb

        @pl.when(i == nt - 1)
        def _():
            dsp_ref[...] = dsp_ref[...] * (-_sigmoid(-ap_ref[...]))

    tile = _row_spec(tb, R)
    halo = pl.BlockSpec((HALO, R), lambda i: (jnp.maximum(i * per - 1, 0), 0))
    return pl.pallas_call(
        body, name=name, grid=(T // tb,),
        in_specs=[tile, tile, halo, tile, tile, tile, _vec_spec(R, R), _vec_spec(R, R), _vec_spec(R)],
        out_specs=[tile, tile, tile, _vec_spec(R), _vec_spec(R), _vec_spec(R)],
        out_shape=[jax.ShapeDtypeStruct((T, R), F32), jax.ShapeDtypeStruct((T, R), BF), jax.ShapeDtypeStruct((T, R), BF)]
        + [jax.ShapeDtypeStruct((1, R), F32)] * 3,
        scratch_shapes=[pltpu.VMEM((HALO + tb, R), F32)],
        compiler_params=_params("arbitrary"),
    )(lam, hs, hs, r, ig, xc, wa, wx, a_param.reshape(1, R))


def _gate_spans(R):
    d = R // LRU_HEADS
    spans = [min((j * GATE_COLS // d) * d // LANES * LANES, R - GATE_SPAN) for j in range(R // GATE_COLS)]
    assert R % GATE_COLS == 0 and all(lo + GATE_SPAN >= (((j + 1) * GATE_COLS - 1) // d + 1) * d for j, lo in enumerate(spans))
    return spans


def _lru_gates_dw(xc, dpa, dpx, name):
    T, R = xc.shape
    tk = _tile(T, 1024, 16)
    spans = _gate_spans(R)
    nb = len(spans)

    def body(x_ref, a_ref, b_ref, o_ref):
        @pl.when(pl.program_id(0) == 0)
        def _():
            o_ref[...] = jnp.zeros_like(o_ref)

        for j, lo in enumerate(spans):
            xs = x_ref[:, pl.ds(lo, GATE_SPAN)].astype(BF)
            cols = pl.ds(j * GATE_COLS, GATE_COLS)
            o_ref[0, j] += _dot_tn(xs, a_ref[:, cols])
            o_ref[1, j] += _dot_tn(xs, b_ref[:, cols])

    row = _row_spec(tk, R)
    out = pl.pallas_call(
        body, name=name, grid=(T // tk,), in_specs=[row, row, row],
        out_specs=pl.BlockSpec((2, nb, GATE_SPAN, GATE_COLS), lambda i: (0, 0, 0, 0)),
        out_shape=jax.ShapeDtypeStruct((2, nb, GATE_SPAN, GATE_COLS), F32), compiler_params=_params("arbitrary"),
    )(xc, dpa, dpx)
    dense = jnp.zeros((2, R, R), F32)
    for j, lo in enumerate(spans):
        dense = dense.at[:, lo:lo + GATE_SPAN, j * GATE_COLS:(j + 1) * GATE_COLS].set(out[:, j])
    return dense[0], dense[1]


def _lru_conv_bwd(dxc, z, conv_w, name):
    T, R = dxc.shape
    tb = _tile(T, TB_SEQ, HALO)
    per = tb // HALO
    nt = T // tb

    def body(d_ref, dn_ref, x_ref, xp_ref, cw_ref, dxb_ref, dcw_ref, dcb_ref, dext, xext):
        i = pl.program_id(0)

        @pl.when(i == 0)
        def _():
            dcw_ref[...] = jnp.zeros_like(dcw_ref)
            dcb_ref[...] = jnp.zeros_like(dcb_ref)

        d = d_ref[...]
        dext[pl.ds(0, tb), :] = d
        dext[pl.ds(tb, HALO), :] = jnp.where(i < nt - 1, dn_ref[...], 0.0)
        xext[pl.ds(0, HALO), :] = jnp.where(i > 0, xp_ref[...], 0.0)
        xext[pl.ds(HALO, tb), :] = x_ref[...]
        dxb = cw_ref[CONV_WIDTH - 1:CONV_WIDTH, :] * d
        for k in range(CONV_WIDTH - 1):
            dxb = dxb + cw_ref[k:k + 1, :] * dext[pl.ds(CONV_WIDTH - 1 - k, tb), :]
        dxb_ref[...] = dxb.astype(BF)
        for k in range(CONV_WIDTH):
            dcw_ref[k:k + 1, :] += jnp.sum(d * xext[pl.ds(HALO - 3 + k, tb), :], axis=0, keepdims=True)
        dcb_ref[...] += jnp.sum(d, axis=0, keepdims=True)

    tile = _row_spec(tb, R)
    nxt = pl.BlockSpec((HALO, R), lambda i: (jnp.minimum((i + 1) * per, T // HALO - 1), 0))
    xtile = pl.BlockSpec((tb, R), lambda i: (i, 1))
    xprev = pl.BlockSpec((HALO, R), lambda i: (jnp.maximum(i * per - 1, 0), 1))
    return pl.pallas_call(
        body, name=name, grid=(nt,), in_specs=[tile, nxt, xtile, xprev, _vec_spec(R, CONV_WIDTH)],
        out_specs=[tile, _vec_spec(R, SUBLANES), _vec_spec(R)],
        out_shape=[jax.ShapeDtypeStruct((T, R), BF), jax.ShapeDtypeStruct((SUBLANES, R), F32),
                   jax.ShapeDtypeStruct((1, R), F32)],
        scratch_shapes=[pltpu.VMEM((tb + HALO, R), F32), pltpu.VMEM((HALO + tb, R), F32)],
        compiler_params=_params("arbitrary"),
    )(dxc, dxc, z, z, conv_w)


def _window_sums(e, n, back):
    out, s = [], e
    for k in (1, 2, 4, 8):
        s = s + pltpu.roll(s, k if back else n - k, 0)
        out.append(s)
    return out


def _pool_fwd(h, g, w, b, scale, name):
    T, D = h.shape
    G = len(POOL_WINDOWS)
    gd = D // G
    tb = _tile(T, TB_SEQ, HALO)
    per = tb // HALO

    def body(h_ref, hp_ref, g_ref, w_ref, b_ref, s_ref, o_ref, u_ref, yb_ref):
        i = pl.program_id(0)
        t = i * tb + lax.broadcasted_iota(jnp.int32, (tb, gd), 0) + 1
        hv = h_ref[...]
        xn = _rms_parts(hv, g_ref[...])[0]
        xp = jnp.where(i > 0, _rms_parts(hp_ref[...], g_ref[...])[0], 0.0)
        for k, win in enumerate(POOL_WINDOWS):
            cols = slice(k * gd, (k + 1) * gd)
            x = xn[:, cols]
            e = jnp.concatenate([xp[:, cols], x], axis=0)
            sw = _window_sums(e, HALO + tb, True)[k][HALO:, :]
            u = (sw / jnp.minimum(t, win).astype(F32) - x).astype(BF)
            yb = _dot(u, w_ref[k]) + b_ref[:, cols]
            u_ref[:, cols] = u
            yb_ref[:, cols] = yb
            o_ref[:, cols] = hv[:, cols] + yb * s_ref[:, cols]

    tile = _row_spec(tb, D)
    prev = pl.BlockSpec((HALO, D), lambda i: (jnp.maximum(i * per - 1, 0), 0))
    return pl.pallas_call(
        body, name=name, grid=(T // tb,),
        in_specs=[tile, prev, _vec_spec(D), pl.BlockSpec((G, gd, gd), lambda i: (0, 0, 0)), _vec_spec(D), _vec_spec(D)],
        out_specs=[tile, tile, tile],
        out_shape=[jax.ShapeDtypeStruct((T, D), F32), jax.ShapeDtypeStruct((T, D), BF), jax.ShapeDtypeStruct((T, D), F32)],
        compiler_params=_params("parallel"),
    )(h, h, g.reshape(1, D), w, b.reshape(1, D), scale.reshape(1, D))


def _pool_bwd(dm, u, yb, w, scale, name):
    T, D = dm.shape
    G = len(POOL_WINDOWS)
    gd = D // G
    tb = _tile(T, TB_SEQ, HALO)

    def body(d_ref, u_ref, yb_ref, w_ref, s_ref, du_ref, v_ref, dw_ref, db_ref, ds_ref):
        i = pl.program_id(0)

        @pl.when(i == 0)
        def _():
            dw_ref[...] = jnp.zeros_like(dw_ref)
            db_ref[...] = jnp.zeros_like(db_ref)
            ds_ref[...] = jnp.zeros_like(ds_ref)

        d, sc = d_ref[...], s_ref[...]
        ds_ref[...] += jnp.sum(d * yb_ref[...], axis=0, keepdims=True)
        db_ref[...] += jnp.sum(d * sc, axis=0, keepdims=True)
        t = i * tb + lax.broadcasted_iota(jnp.int32, (tb, gd), 0) + 1
        for g, win in enumerate(POOL_WINDOWS):
            cols = pl.ds(g * gd, gd)
            dy = (d_ref[:, cols] * s_ref[:, cols]).astype(BF)
            du = _dot_nt(dy, w_ref[g])
            dw_ref[g] += _dot_tn(u_ref[:, cols], dy)
            du_ref[:, cols] = du
            v_ref[:, cols] = du / jnp.minimum(t, win).astype(F32)

    tile = _row_spec(tb, D)
    return pl.pallas_call(
        body, name=name, grid=(T // tb,),
        in_specs=[tile, tile, tile, pl.BlockSpec((G, gd, gd), lambda i: (0, 0, 0)), _vec_spec(D)],
        out_specs=[tile, tile, pl.BlockSpec((G, gd, gd), lambda i: (0, 0, 0)), _vec_spec(D), _vec_spec(D)],
        out_shape=[jax.ShapeDtypeStruct((T, D), F32), jax.ShapeDtypeStruct((T, D), F32),
                   jax.ShapeDtypeStruct((G, gd, gd), F32), jax.ShapeDtypeStruct((1, D), F32),
                   jax.ShapeDtypeStruct((1, D), F32)],
        compiler_params=_params("arbitrary"),
    )(dm, u, yb, w, scale.reshape(1, D))


def _pool_bwd_win(v, du, h, g, dh, name):
    T, D = v.shape
    G = len(POOL_WINDOWS)
    gd = D // G
    tb = _tile(T, TB_SEQ, HALO)
    per = tb // HALO
    nt = T // tb

    def body(v_ref, vn_ref, du_ref, h_ref, g_ref, dh_ref, o_ref, dg_ref):
        i = pl.program_id(0)

        @pl.when(i == 0)
        def _():
            dg_ref[...] = jnp.zeros_like(dg_ref)

        parts = []
        for k in range(G):
            cols = pl.ds(k * gd, gd)
            e = jnp.concatenate([v_ref[:, cols], jnp.where(i < nt - 1, vn_ref[:, cols], 0.0)], axis=0)
            parts.append(_window_sums(e, tb + HALO, False)[k][:tb, :] - du_ref[:, cols])
        dx, dg = _rms_bwd_parts(h_ref[...], g_ref[...], jnp.concatenate(parts, axis=1))
        o_ref[...] = dh_ref[...] + dx
        dg_ref[...] += dg

    tile = _row_spec(tb, D)
    nxt = pl.BlockSpec((HALO, D), lambda i: (jnp.minimum((i + 1) * per, T // HALO - 1), 0))
    return pl.pallas_call(
        body, name=name, grid=(nt,), in_specs=[tile, nxt, tile, tile, _vec_spec(D), tile], out_specs=[tile, _vec_spec(D)],
        out_shape=[jax.ShapeDtypeStruct((T, D), F32), jax.ShapeDtypeStruct((1, D), F32)],
        compiler_params=_params("arbitrary"),
    )(v, v, du, h, g.reshape(1, D), dh)


def _adamw(w, g, m, v, name):
    shape = w.shape
    cols = shape[-1] if w.ndim > 1 else shape[0]
    rows = w.size // cols
    tr = _tile(rows, TR_ADAM, SUBLANES)
    c1, c2 = 1.0 / (1.0 - ADAM_B1 ** ADAM_STEP), 1.0 / (1.0 - ADAM_B2 ** ADAM_STEP)

    def body(w_ref, g_ref, m_ref, v_ref, d_ref, mo_ref, vo_ref):
        gv = g_ref[...]
        mn = ADAM_B1 * m_ref[...] + (1.0 - ADAM_B1) * gv
        vn = ADAM_B2 * v_ref[...] + (1.0 - ADAM_B2) * (gv * gv)
        d_ref[...] = -ADAM_LR * ((mn * c1) / (jnp.sqrt(vn * c2) + ADAM_EPS) + ADAM_WD * w_ref[...])
        mo_ref[...] = mn
        vo_ref[...] = vn

    spec = _row_spec(tr, cols)
    outs = pl.pallas_call(
        body, name=name, grid=(rows // tr,), in_specs=[spec] * 4, out_specs=[spec] * 3,
        out_shape=[jax.ShapeDtypeStruct((rows, cols), F32)] * 3, compiler_params=_params("parallel"),
    )(*[t.reshape(rows, cols) for t in (w, g, m, v)])
    return [o.reshape(shape) for o in outs]


def _sum_devices(parts, name):
    n, rows, cols = parts.shape
    tr = _tile(rows, 1024, SUBLANES)

    def body(p_ref, o_ref):
        acc = p_ref[0].astype(F32)
        for k in range(1, n):
            acc = acc + p_ref[k].astype(F32)
        o_ref[...] = acc

    return pl.pallas_call(
        body, name=name, grid=(rows // tr,), in_specs=[pl.BlockSpec((n, tr, cols), lambda i: (0, i, 0))],
        out_specs=_row_spec(tr, cols), out_shape=jax.ShapeDtypeStruct((rows, cols), F32),
        compiler_params=_params("parallel"),
    )(parts)


def _position():
    return lax.axis_index("x"), lax.axis_index("y"), lax.axis_index("c")


def _gathered_shapes(blocks):
    return [jax.ShapeDtypeStruct((b.shape[0], N_DEV * b.shape[1], b.shape[2]), b.dtype) for b in blocks]


def _gather_sems(ng):
    return [pltpu.SemaphoreType.DMA((ng, 7)), pltpu.SemaphoreType.DMA((ng, 7)), pltpu.SemaphoreType.DMA((ng,))] if ng else []


def _gather_plan(blocks, srcs, outs, send_sems, recv_sems, local_sems):
    ng = len(blocks)
    x, y, c = _position()
    me, sibling = (x, y, c), (x, y, 1 - c)
    chips = [(1 - x, y), (x, 1 - y), (1 - x, 1 - y)]

    def rows(g, px, py, pc):
        r = blocks[g].shape[1]
        return outs[g].at[:, pl.ds((4 * px + 2 * py + pc) * r, r), :]

    def copy(g, k, block, to, src=None):
        return pltpu.make_async_remote_copy(
            src_ref=rows(g, *block) if src is None else src, dst_ref=rows(g, *block),
            send_sem=send_sems.at[g, k], recv_sem=recv_sems.at[g, k], device_id=to, device_id_type=MESH)

    def mine(g):
        return pltpu.make_async_copy(srcs[g], rows(g, *me), local_sems.at[g])

    def first(g):
        return [copy(g, 0, me, sibling, src=srcs[g])] + [copy(g, 1 + j, me, (*chip, c), src=srcs[g])
                                                         for j, chip in enumerate(chips)]

    def passed(g):
        return [copy(g, 4 + j, (*chip, c), sibling) for j, chip in enumerate(chips)]

    def start():
        for g in range(ng):
            mine(g).start()
            for cp in first(g):
                cp.start()

    def forward():
        for j, chip in enumerate(chips):
            for g in range(ng):
                copy(g, 1 + j, (*chip, c), me).wait_recv()
                copy(g, 4 + j, (*chip, c), sibling).start()

    def finish():
        for g in range(ng):
            copy(g, 0, sibling, me).wait_recv()
            for j, chip in enumerate(chips):
                copy(g, 4 + j, (*chip, 1 - c), me).wait_recv()
            for cp in first(g) + passed(g):
                cp.wait_send()
            mine(g).wait()

    return start, forward, finish


def _all_gather(blocks, name):
    ng = len(blocks)

    def body(*refs):
        start, forward, finish = _gather_plan(blocks, refs[:ng], refs[ng:2 * ng], *refs[2 * ng:])
        start()
        forward()
        finish()

    hbm = pl.BlockSpec(memory_space=pl.ANY)
    return pl.pallas_call(
        body, name=name, in_specs=[hbm] * ng, out_specs=[hbm] * ng, out_shape=_gathered_shapes(blocks),
        scratch_shapes=_gather_sems(ng),
    )(*blocks)


FLIPS = ((0, 0, 1), (1, 0, 0), (0, 1, 0), (1, 1, 0), (1, 0, 1), (0, 1, 1), (1, 1, 1))


def _piece_rows(piece):
    arr, m = piece
    return arr.shape[0] if m is None else 1


def _scattered_shapes(pieces):
    return [jax.ShapeDtypeStruct((N_DEV, _piece_rows(p), p[0].shape[1] // N_DEV, p[0].shape[2]), p[0].dtype)
            for p in pieces]


def _scatter_sems(ng):
    n = len(FLIPS)
    return [pltpu.SemaphoreType.DMA((ng, n)), pltpu.SemaphoreType.DMA((ng, n)), pltpu.SemaphoreType.DMA((ng,))] if ng else []


def _scatter_plan(pieces, srcs, outs, send_sems, recv_sems, local_sems):
    x, y, c = _position()

    def block(g, tx, ty, tc):
        arr, m = pieces[g]
        r = arr.shape[1] // N_DEV
        lead = slice(None) if m is None else pl.ds(m, 1)
        return srcs[g].at[lead, pl.ds((4 * tx + 2 * ty + tc) * r, r), :]

    def copies(g):
        out = []
        for k, (fx, fy, fc) in enumerate(FLIPS):
            tx, ty, tc = (1 - x if fx else x), (1 - y if fy else y), (1 - c if fc else c)
            out.append(pltpu.make_async_remote_copy(
                src_ref=block(g, tx, ty, tc), dst_ref=outs[g].at[k], send_sem=send_sems.at[g, k],
                recv_sem=recv_sems.at[g, k], device_id=(tx, ty, tc), device_id_type=MESH))
        return out

    def mine(g):
        return pltpu.make_async_copy(block(g, x, y, c), outs[g].at[len(FLIPS)], local_sems.at[g])

    def start():
        for g in range(len(pieces)):
            mine(g).start()
            for cp in copies(g):
                cp.start()

    def finish():
        for g in range(len(pieces)):
            for cp in copies(g):
                cp.wait()
            mine(g).wait()

    return start, finish


def _scatter_and_gather(pieces, blocks, name):
    n_p, n_b = len(pieces), len(blocks)

    def body(*refs):
        ins, outs, sems = refs[:n_p + n_b], refs[n_p + n_b:2 * (n_p + n_b)], refs[2 * (n_p + n_b):]
        s_start, s_finish = _scatter_plan(pieces, ins[:n_p], outs[:n_p], *sems[:3])
        g_start, g_forward, g_finish = _gather_plan(blocks, ins[n_p:], outs[n_p:], *sems[3:])
        s_start()
        g_start()
        g_forward()
        g_finish()
        s_finish()

    hbm = pl.BlockSpec(memory_space=pl.ANY)
    outs = pl.pallas_call(
        body, name=name, in_specs=[hbm] * (n_p + n_b), out_specs=[hbm] * (n_p + n_b),
        out_shape=_scattered_shapes(pieces) + _gathered_shapes(blocks),
        scratch_shapes=_scatter_sems(n_p) + _gather_sems(n_b),
    )(*[p[0] for p in pieces], *blocks)
    return list(outs[:n_p]), list(outs[n_p:])


def _scatter_sum(recv, name):
    _, n, r, c = recv.shape

    def body(r_ref, o_ref):
        acc = r_ref[len(FLIPS)].astype(F32)
        for k in range(len(FLIPS)):
            acc = acc + r_ref[k].astype(F32)
        o_ref[...] = acc

    return pl.pallas_call(
        body, name=name, grid=(n,), in_specs=[pl.BlockSpec((N_DEV, None, r, c), lambda i: (0, i, 0, 0))],
        out_specs=pl.BlockSpec((None, r, c), lambda i: (i, 0, 0)),
        out_shape=jax.ShapeDtypeStruct((n, r, c), F32), compiler_params=_params("parallel"),
    )(recv)


def _block_diag(w):
    H, d, _ = w.shape
    return (jnp.eye(H, dtype=w.dtype)[:, None, :, None] * w[:, :, None, :]).reshape(H * d, H * d)


def _diag_blocks(dense, H):
    d = dense.shape[0] // H
    return jnp.stack([dense[i * d:(i + 1) * d, i * d:(i + 1) * d] for i in range(H)])


def _local_step(x, p, tgt, W, blocks=None):
    dist = blocks is not None
    L = p.shape[0]
    W = dict(W)

    def gathering(keys):
        return [k for k in keys if k not in W] if dist else []

    def ffn_fwd(h, g, i, f, during_act, during_out):
        w = W[("ffn", i, f)]
        keys = gathering(during_act)
        a, b, s, got = _ffn_fwd_act(h, g, w, f"ffn{f}_fwd_act_{i}", gather=[blocks[k] for k in keys])
        W.update(zip(keys, got))
        keys = gathering(during_out)
        h, got = _ffn_fwd_out(s, w, h, f"ffn{f}_fwd_out_{i}", gather=[blocks[k] for k in keys])
        W.update(zip(keys, got))
        return a, b, s, h

    saved = []
    h = x
    for i in range(L):
        j = i // 2
        lru = i % 2 == 0
        s = {"h0": h}
        mixer = [("lru_in", j), ("lru_out", j)] if lru else [("pool_w", j)]
        s["a1"], s["b1"], s["s1"], h = ffn_fwd(h, W["ffn1_norm"][i], i, 1, [("ffn", i, 2)], mixer)
        s["h1"] = h
        if lru:
            hn = _rms_fwd(h, W["mix_norm"][i], BF, f"mix_norm_{i}")
            z = _mm(hn, W[("lru_in", j)][0], "nt", f"lru_in_{i}")
            wa, wx = _block_diag(W["lru_w_a"][j]).astype(BF), _block_diag(W["lru_w_x"][j]).astype(BF)
            h, xc, r, ig, a, hs, y = _lru_fwd(z, W["lru_conv_w"][j], W["lru_conv_b"][j], wa, wx, W["lru_b_a"][j],
                                              W["lru_b_x"][j], W["lru_a_param"][j], W[("lru_out", j)][0], h, f"lru_fwd_{i}")
            s.update(hn=hn, z=z, wa=wa, wx=wx, xc=xc, r=r, ig=ig, a=a, hs=hs, y=y)
        else:
            h, s["u"], s["yb"] = _pool_fwd(h, W["mix_norm"][i], W[("pool_w", j)], W["pool_b"][j], W["pool_scale"][j],
                                           f"pool_fwd_{i}")
        s["h2"] = h
        s["a2"], s["b2"], s["s2"], h = ffn_fwd(h, W["ffn2_norm"][i], i, 2, [("ffn", i + 1, 1)] if i + 1 < L else [],
                                               [("ple_gate", i), ("ple_proj", i)])
        s["h3"] = h
        h, s["n4"], s["gate"], s["pp"] = _ple_fwd(h, W["ple_norm"][i], W[("ple_gate", i)][0], W[("ple_proj", i)][0], p[i],
                                                  f"ple_fwd_{i}")
        saved.append(s)

    loss, dh, d_final = _loss_head(h, W["final_norm"], tgt)

    big, recv = {}, {}
    n_lru, n_pool = L // 2 + L % 2, L // 2
    small = {k: [None] * L for k in ("ffn1_norm", "mix_norm", "ffn2_norm", "ple_norm")}
    for k in ("lru_conv_w", "lru_conv_b", "lru_w_a", "lru_b_a", "lru_w_x", "lru_b_x", "lru_a_param"):
        small[k] = [None] * n_lru
    for k in ("pool_b", "pool_scale"):
        small[k] = [None] * n_pool

    def scattering(pieces):
        return [(k, m) for k, m in pieces if k in big] if dist else []

    def ffn_bwd(dh, h_in, g, a, b, sv, i, f, during):
        key, w = ("ffn", i, f), W[("ffn", i, f)]
        out = [scattering(d) for d in during]
        sent = [[(big[k], m) for k, m in o] for o in out]
        da, db, dhb, got0 = _ffn_bwd_act(dh, a, b, w, f"ffn{f}_bwd_act_{i}", scatter=sent[0])
        big[key], got1 = _ffn_bwd_w(da, db, sv, h_in, g, dhb, f"ffn{f}_dw_{i}", scatter=sent[1])
        out.append(scattering([(key, 0)] + ([(key, 1)] if (i, f) == (0, 1) else [])))
        dh, dg, got2 = _ffn_bwd_in(da, db, w, h_in, g, dh, f"ffn{f}_bwd_in_{i}", scatter=[(big[k], m) for k, m in out[2]])
        for o, got in zip(out, (got0, got1, got2)):
            recv.update(zip(o, got))
        return dh, dg

    for i in reversed(range(L)):
        j = i // 2
        lru = i % 2 == 0
        s = saved[i]
        dh, dz, dpp, dg = _ple_bwd(dh, s["gate"], s["pp"], s["h3"], W["ple_norm"][i], W[("ple_gate", i)][0],
                                   f"ple_bwd_{i}")
        big[("ple_gate", i)] = _mm(s["n4"], dz, "tn", f"ple_gate_dw_{i}", out_dtype=BF)[None]
        big[("ple_proj", i)] = _mm(dpp, p[i], "tn", f"ple_proj_dw_{i}", out_dtype=BF)[None]
        small["ple_norm"][i] = dg[0]
        above = ("ffn", i + 1, 1)
        dh, dg = ffn_bwd(dh, s["h2"], W["ffn2_norm"][i], s["a2"], s["b2"], s["s2"], i, 2, [
            [(above, 1)], [(above, 2), (("ple_gate", i), None), (("ple_proj", i), None)]])
        small["ffn2_norm"][i] = dg[0]
        if lru:
            big[("lru_out", j)] = _mm(s["y"], dh, "tn", f"lru_out_dw_{i}", out_dtype=BF)[None]
            dy = _mm(dh, W[("lru_out", j)][0], "nt", f"lru_out_dx_{i}")
            dhd, dgb = _lru_bwd_ew(dy, s["hs"], s["z"], f"lru_bwd_ew_{i}")
            lam = _lru_scan(s["a"], dhd, True, f"lru_scan_bwd_{i}")
            dxc, dpa, dpx, dsp, dba, dbx = _lru_gates_bwd(lam, s["hs"], s["r"], s["ig"], s["xc"], s["wa"], s["wx"],
                                                          W["lru_a_param"][j], f"lru_gates_bwd_{i}")
            small["lru_a_param"][j], small["lru_b_a"][j], small["lru_b_x"][j] = dsp[0], dba[0], dbx[0]
            dwa, dwx = _lru_gates_dw(s["xc"], dpa, dpx, f"lru_gates_dw_{i}")
            small["lru_w_a"][j], small["lru_w_x"][j] = _diag_blocks(dwa, LRU_HEADS), _diag_blocks(dwx, LRU_HEADS)
            dxb, dcw, dcb = _lru_conv_bwd(dxc, s["z"], W["lru_conv_w"][j], f"lru_conv_bwd_{i}")
            small["lru_conv_w"][j], small["lru_conv_b"][j] = dcw[:CONV_WIDTH], dcb[0]
            big[("lru_in", j)] = jnp.concatenate([_mm(dgb, s["hn"], "tn", f"lru_in_dw_g_{i}", out_dtype=BF),
                                                  _mm(dxb, s["hn"], "tn", f"lru_in_dw_x_{i}", out_dtype=BF)])[None]
            dh, dg = _lru_in_bwd(dgb, dxb, W[("lru_in", j)][0], s["h1"], W["mix_norm"][i], dh, f"lru_in_bwd_{i}")
            mixer = [("lru_in", j), ("lru_out", j)]
        else:
            du, v, dw, dbp, dsc = _pool_bwd(dh, s["u"], s["yb"], W[("pool_w", j)], W["pool_scale"][j], f"pool_bwd_{i}")
            big[("pool_w", j)] = dw.astype(BF)
            small["pool_b"][j], small["pool_scale"][j] = dbp[0], dsc[0]
            dh, dg = _pool_bwd_win(v, du, s["h1"], W["mix_norm"][i], dh, f"pool_bwd_win_{i}")
            mixer = [("pool_w", j)]
        small["mix_norm"][i] = dg[0]
        second = ("ffn", i, 2)
        dh, dg = ffn_bwd(dh, s["h0"], W["ffn1_norm"][i], s["a1"], s["b1"], s["s1"], i, 1, [
            [(second, 1)], [(second, 2)] + [(k, None) for k in mixer]])
        small["ffn1_norm"][i] = dg[0]

    small = {k: jnp.stack(v) for k, v in small.items()}
    small["final_norm"] = d_final[0]
    return loss, dh, big, recv, small


SMALL_SHARDED = ("pool_b", "pool_scale", "lru_conv_w")
SMALL = ("ffn1_norm", "mix_norm", "ffn2_norm", "ple_norm", "final_norm", "lru_conv_b", "lru_w_a", "lru_b_a",
         "lru_w_x", "lru_b_x", "lru_a_param", "pool_b", "pool_scale", "lru_conv_w")


def _pack_big(w):
    t = lambda a: jnp.swapaxes(a, -1, -2)
    out = {}
    for i in range(w["ffn1_norm"].shape[0]):
        for f in (1, 2):
            out[("ffn", i, f)] = jnp.stack([t(w[f"ffn{f}_w_gate"][i]), t(w[f"ffn{f}_w_up"][i]), w[f"ffn{f}_w_down"][i]])
        out[("ple_gate", i)], out[("ple_proj", i)] = w["ple_w_gate"][i][None], t(w["ple_w_proj"][i])[None]
    for j in range(w["lru_w_in"].shape[0]):
        out[("lru_in", j)], out[("lru_out", j)] = t(w["lru_w_in"][j])[None], w["lru_w_out"][j][None]
    for j in range(w["pool_w"].shape[0]):
        out[("pool_w", j)] = w["pool_w"][j]
    return out


def _unpack_big(b, L):
    t = lambda a: jnp.swapaxes(a, -1, -2)
    n_lru, n_pool = L // 2 + L % 2, L // 2
    out = {"lru_w_in": jnp.stack([t(b[("lru_in", j)][0]) for j in range(n_lru)]),
           "lru_w_out": jnp.stack([b[("lru_out", j)][0] for j in range(n_lru)]),
           "pool_w": jnp.stack([b[("pool_w", j)] for j in range(n_pool)]),
           "ple_w_gate": jnp.stack([b[("ple_gate", i)][0] for i in range(L)]),
           "ple_w_proj": jnp.stack([t(b[("ple_proj", i)][0]) for i in range(L)])}
    for f in (1, 2):
        out[f"ffn{f}_w_gate"] = jnp.stack([t(b[("ffn", i, f)][0]) for i in range(L)])
        out[f"ffn{f}_w_up"] = jnp.stack([t(b[("ffn", i, f)][1]) for i in range(L)])
        out[f"ffn{f}_w_down"] = jnp.stack([b[("ffn", i, f)][2] for i in range(L)])
    return out


def _flatten(parts, names, rows_of=LANES):
    flat = jnp.concatenate([parts[k].reshape(-1) for k in names])
    pad = (-flat.size) % (16 * rows_of)
    return jnp.pad(flat, (0, pad)).reshape(1, -1, rows_of)


def _unflatten(flat, like, names):
    out, o = {}, 0
    flat = flat.reshape(-1)
    for k in names:
        n = like[k].size
        out[k] = flat[o:o + n].reshape(like[k].shape)
        o += n
    return out


def kernel(x, p, ffn1_norm, ffn1_w_gate, ffn1_w_up, ffn1_w_down, mix_norm, lru_w_in, lru_conv_w, lru_conv_b, lru_w_a, lru_b_a, lru_w_x, lru_b_x, lru_a_param, lru_w_out, pool_w, pool_b, pool_scale, ffn2_norm, ffn2_w_gate, ffn2_w_up, ffn2_w_down, ple_norm, ple_w_gate, ple_w_proj, final_norm, loss_target, m_ffn1_norm, m_ffn1_w_gate, m_ffn1_w_up, m_ffn1_w_down, m_mix_norm, m_lru_w_in, m_lru_conv_w, m_lru_conv_b, m_lru_w_a, m_lru_b_a, m_lru_w_x, m_lru_b_x, m_lru_a_param, m_lru_w_out, m_pool_w, m_pool_b, m_pool_scale, m_ffn2_norm, m_ffn2_w_gate, m_ffn2_w_up, m_ffn2_w_down, m_ple_norm, m_ple_w_gate, m_ple_w_proj, m_final_norm, v_ffn1_norm, v_ffn1_w_gate, v_ffn1_w_up, v_ffn1_w_down, v_mix_norm, v_lru_w_in, v_lru_conv_w, v_lru_conv_b, v_lru_w_a, v_lru_b_a, v_lru_w_x, v_lru_b_x, v_lru_a_param, v_lru_w_out, v_pool_w, v_pool_b, v_pool_scale, v_ffn2_norm, v_ffn2_w_gate, v_ffn2_w_up, v_ffn2_w_down, v_ple_norm, v_ple_w_gate, v_ple_w_proj, v_final_norm):
    names = ["ffn1_norm", "ffn1_w_gate", "ffn1_w_up", "ffn1_w_down", "mix_norm", "lru_w_in", "lru_conv_w", "lru_conv_b",
             "lru_w_a", "lru_b_a", "lru_w_x", "lru_b_x", "lru_a_param", "lru_w_out", "pool_w", "pool_b", "pool_scale",
             "ffn2_norm", "ffn2_w_gate", "ffn2_w_up", "ffn2_w_down", "ple_norm", "ple_w_gate", "ple_w_proj", "final_norm"]
    w = dict(zip(names, [ffn1_norm, ffn1_w_gate, ffn1_w_up, ffn1_w_down, mix_norm, lru_w_in, lru_conv_w, lru_conv_b, lru_w_a, lru_b_a, lru_w_x, lru_b_x, lru_a_param, lru_w_out, pool_w, pool_b, pool_scale, ffn2_norm, ffn2_w_gate, ffn2_w_up, ffn2_w_down, ple_norm, ple_w_gate, ple_w_proj, final_norm]))
    m = dict(zip(names, [m_ffn1_norm, m_ffn1_w_gate, m_ffn1_w_up, m_ffn1_w_down, m_mix_norm, m_lru_w_in, m_lru_conv_w, m_lru_conv_b, m_lru_w_a, m_lru_b_a, m_lru_w_x, m_lru_b_x, m_lru_a_param, m_lru_w_out, m_pool_w, m_pool_b, m_pool_scale, m_ffn2_norm, m_ffn2_w_gate, m_ffn2_w_up, m_ffn2_w_down, m_ple_norm, m_ple_w_gate, m_ple_w_proj, m_final_norm]))
    v = dict(zip(names, [v_ffn1_norm, v_ffn1_w_gate, v_ffn1_w_up, v_ffn1_w_down, v_mix_norm, v_lru_w_in, v_lru_conv_w, v_lru_conv_b, v_lru_w_a, v_lru_b_a, v_lru_w_x, v_lru_b_x, v_lru_a_param, v_lru_w_out, v_pool_w, v_pool_b, v_pool_scale, v_ffn2_norm, v_ffn2_w_gate, v_ffn2_w_up, v_ffn2_w_down, v_ple_norm, v_ple_w_gate, v_ple_w_proj, v_final_norm]))
    L = p.shape[0]
    px, py, pc = _position()
    me = 4 * px + 2 * py + pc

    blocks = {k: b.astype(BF) for k, b in _pack_big(w).items()}
    first = ("ffn", 0, 1)
    got, small_blocks = _all_gather([blocks[first], _flatten(w, SMALL_SHARDED)], "gather_first")
    W = {first: got}
    per_dev = small_blocks.reshape(N_DEV, -1)
    shards = [_unflatten(per_dev[k], w, SMALL_SHARDED) for k in range(N_DEV)]
    for k in SMALL:
        W[k] = jnp.concatenate([s[k] for s in shards], axis=-1) if k in SMALL_SHARDED else w[k]

    loss, dx, big, recv, small = _local_step(x[0], p[:, 0], loss_target[0], W, blocks)

    last = [(k, m) for k in big if (k, None) not in recv for m in range(big[k].shape[0]) if (k, m) not in recv]
    got, (parts,) = _scatter_and_gather([(big[k], m) for k, m in last], [_flatten(small, SMALL).astype(BF)],
                                        "scatter_last_gather_small")
    recv.update(zip(last, got))

    def total(k):
        tag = "sum_" + "_".join(map(str, k))
        if (k, None) in recv:
            return _scatter_sum(recv[(k, None)], tag)
        return jnp.concatenate([_scatter_sum(recv[(k, m)], f"{tag}_{m}") for m in range(big[k].shape[0])])

    grads = _unpack_big({k: total(k) for k in big}, L)
    total_small = _sum_devices(parts.reshape(N_DEV, -1, LANES), "sum_small_grads")
    full = _unflatten(total_small, {k: W[k] for k in SMALL}, SMALL)
    for k in SMALL:
        if k in SMALL_SHARDED:
            n = w[k].shape[-1]
            grads[k] = lax.dynamic_slice_in_dim(full[k], me * n, n, axis=-1)
        else:
            grads[k] = full[k]

    delta, new_m, new_v = {}, {}, {}
    for k in names:
        delta[k], new_m[k], new_v[k] = _adamw(w[k], grads[k], m[k], v[k], f"adamw_{k}")
    total_loss = lax.psum(loss[0, 0], ("x", "y", "c"))
    return (total_loss, dx[None], *[grads[k] for k in names], *[delta[k] for k in names],
            *[new_m[k] for k in names], *[new_v[k] for k in names])
```

```python
import functools

import jax
import jax.numpy as jnp
from jax import lax
from jax.experimental import pallas as pl
from jax.experimental.pallas import tpu as pltpu

F32 = jnp.float32
BF = jnp.bfloat16
MESH = pl.DeviceIdType.MESH

RMS_EPS = 1e-6
LRU_C = 8.0
LRU_HEADS = 16
CONV_WIDTH = 4
POOL_WINDOWS = (2, 4, 8, 16)
ADAM_LR, ADAM_B1, ADAM_B2, ADAM_EPS, ADAM_WD, ADAM_STEP = 0.001, 0.9, 0.999, 1e-08, 0.01, 10

N_DEV = 8
LANES = 128
SUBLANES = 8
GATE_COLS, GATE_SPAN = 256, 512
HALO = 16
VMEM_LIMIT = 56 * 1024 * 1024

TM_FFN = 1024
TM_FFN_ACT = 2048
TM_FFN_IN = 512
TF_FFN = 256
TF_FFN_WG = 1408
TK_FFN_WG = 512
TB_SEQ = 256
TM_EW = 512
TM_MM, TN_MM, TK_MM = 1024, 512, 1024
TR_ADAM = 512


def _tile(n, pref, align):
    if n <= pref:
        return n
    t = (pref // align) * align
    while t >= align:
        if n % t == 0:
            return t
        t -= align
    raise ValueError(f"no tile for {n} (pref {pref}, align {align})")


def _params(*sem):
    return pltpu.CompilerParams(dimension_semantics=sem, vmem_limit_bytes=VMEM_LIMIT)


def _dot(a, b):
    return lax.dot_general(a, b, (((1,), (0,)), ((), ())), preferred_element_type=F32)


def _dot_nt(a, b):
    return lax.dot_general(a, b, (((1,), (1,)), ((), ())), preferred_element_type=F32)


def _dot_tn(a, b):
    return lax.dot_general(a, b, (((0,), (0,)), ((), ())), preferred_element_type=F32)


def _sigmoid(x):
    return 1.0 / (1.0 + jnp.exp(-x))


def _gelu_parts(x):
    k0, k1 = 0.7978845608028654, 0.044715
    t = jnp.tanh(k0 * (x + k1 * x * x * x))
    g = 0.5 * x * (1.0 + t)
    dg = 0.5 * (1.0 + t) + 0.5 * x * (1.0 - t * t) * k0 * (1.0 + 3.0 * k1 * x * x)
    return g, dg


def _neg_expm1(x):
    p = x * (1.0 + x * (0.5 + x * (1.0 / 6 + x * (1.0 / 24 + x * (1.0 / 120 + x * (1.0 / 720 + x * (1.0 / 5040)))))))
    return jnp.where(x > -0.35, -p, 1.0 - jnp.exp(x))


def _softplus_neg(l):
    u = jnp.exp(-jnp.abs(l))
    w = 1.0 + u
    log1p = jnp.where(w == 1.0, u, jnp.log(w) * (u / jnp.where(w == 1.0, 1.0, w - 1.0)))
    return jnp.maximum(-l, 0.0) + log1p


def _rms_parts(x, g):
    r = lax.rsqrt(jnp.mean(x * x, axis=-1, keepdims=True) + RMS_EPS)
    nhat = x * r
    return nhat * g, nhat, r


def _rms_bwd_parts(x, g, dn):
    _, nhat, r = _rms_parts(x, g)
    u = dn * g
    dx = r * (u - nhat * jnp.mean(u * nhat, axis=-1, keepdims=True))
    return dx, jnp.sum(dn * nhat, axis=0, keepdims=True)


def _row_spec(tm, d, single=False):
    if single:
        return pl.BlockSpec((tm, d), lambda i, *_: (i, 0), pipeline_mode=pl.Buffered(1))
    return pl.BlockSpec((tm, d), lambda i, *_: (i, 0))


def _vec_spec(d, rows=1):
    return pl.BlockSpec((rows, d), lambda *_: (0, 0))


def _mm(x, w, mode, name, out_dtype=F32, res=None, alpha=1.0, tm=None, tn=None, tk=None):
    if mode == "nn":
        (M, K), (_, N) = x.shape, w.shape
    elif mode == "nt":
        (M, K), (N, _) = x.shape, w.shape
    else:
        (K, M), (_, N) = x.shape, w.shape
    tm = _tile(M, tm or TM_MM, LANES if mode == "tn" else SUBLANES)
    tn = _tile(N, tn or TN_MM, LANES)
    tk = _tile(K, tk or TK_MM, LANES if mode != "tn" else 16)
    nk = K // tk
    dot = {"nn": _dot, "nt": _dot_nt, "tn": _dot_tn}[mode]

    def body(*refs):
        if res is None:
            x_ref, w_ref, o_ref, acc = refs
        else:
            x_ref, w_ref, r_ref, o_ref, acc = refs
        k = pl.program_id(2)

        @pl.when(k == 0)
        def _():
            acc[...] = jnp.zeros_like(acc)

        acc[...] += dot(x_ref[...].astype(BF), w_ref[...].astype(BF))

        @pl.when(k == nk - 1)
        def _():
            r = acc[...] if alpha == 1.0 else acc[...] * alpha
            if res is not None:
                r = r_ref[...] + r
            o_ref[...] = r.astype(out_dtype)

    if mode == "nn":
        specs = [pl.BlockSpec((tm, tk), lambda i, j, k: (i, k)), pl.BlockSpec((tk, tn), lambda i, j, k: (k, j))]
    elif mode == "nt":
        specs = [pl.BlockSpec((tm, tk), lambda i, j, k: (i, k)), pl.BlockSpec((tn, tk), lambda i, j, k: (j, k))]
    else:
        specs = [pl.BlockSpec((tk, tm), lambda i, j, k: (k, i)), pl.BlockSpec((tk, tn), lambda i, j, k: (k, j))]
    args = [x, w]
    if res is not None:
        specs.append(pl.BlockSpec((tm, tn), lambda i, j, k: (i, j)))
        args.append(res)
    return pl.pallas_call(
        body, name=name, grid=(M // tm, N // tn, nk), in_specs=specs,
        out_specs=pl.BlockSpec((tm, tn), lambda i, j, k: (i, j)),
        out_shape=jax.ShapeDtypeStruct((M, N), out_dtype),
        scratch_shapes=[pltpu.VMEM((tm, tn), F32)],
        compiler_params=_params("parallel", "parallel", "arbitrary"),
    )(*args)


def _rms_fwd(h, g, out_dtype, name):
    T, D = h.shape
    tm = _tile(T, TM_EW, 16)

    def body(h_ref, g_ref, o_ref):
        o_ref[...] = _rms_parts(h_ref[...], g_ref[...])[0].astype(out_dtype)

    return pl.pallas_call(
        body, name=name, grid=(T // tm,), in_specs=[_row_spec(tm, D), _vec_spec(D)], out_specs=_row_spec(tm, D),
        out_shape=jax.ShapeDtypeStruct((T, D), out_dtype), compiler_params=_params("parallel"),
    )(h, g.reshape(1, D))


def _loss_head(h, g, tgt):
    T, D = h.shape
    tm = _tile(T, TM_EW, 16)

    def body(h_ref, g_ref, t_ref, loss_ref, dh_ref, dg_ref):
        @pl.when(pl.program_id(0) == 0)
        def _():
            dg_ref[...] = jnp.zeros_like(dg_ref)
            loss_ref[...] = jnp.zeros_like(loss_ref)

        x, gg = h_ref[...], g_ref[...]
        y = _rms_parts(x, gg)[0]
        e = y - t_ref[...]
        part = jnp.sum(jnp.sum(e * e, axis=0, keepdims=True), axis=1, keepdims=True) * (0.5 / D)
        loss_ref[...] += jnp.broadcast_to(part, loss_ref.shape)
        dx, dg = _rms_bwd_parts(x, gg, e * (1.0 / D))
        dh_ref[...] = dx
        dg_ref[...] += dg

    return pl.pallas_call(
        body, name="loss_head", grid=(T // tm,),
        in_specs=[_row_spec(tm, D), _vec_spec(D), _row_spec(tm, D)],
        out_specs=[_vec_spec(LANES), _row_spec(tm, D), _vec_spec(D)],
        out_shape=[jax.ShapeDtypeStruct((1, LANES), F32), jax.ShapeDtypeStruct((T, D), F32),
                   jax.ShapeDtypeStruct((1, D), F32)],
        compiler_params=_params("arbitrary"),
    )(h, g.reshape(1, D), tgt)


def _carry(plan, first, mid, last):
    pl.when(first)(plan[0])
    if len(plan) == 3:
        pl.when(mid)(plan[1])
    pl.when(last)(plan[-1])


def _ffn_fwd_act(h, g, wffn, name, gather=()):
    T, D = h.shape
    F = wffn.shape[1]
    tm, tf = _tile(T, TM_FFN_ACT, 16), _tile(F, TF_FFN, LANES)
    ni, nf, ng = T // tm, F // tf, len(gather)

    def body(*refs):
        h_ref, g_ref, wg_ref, wu_ref = refs[:4]
        srcs, (a_ref, b_ref, s_ref), outs = refs[4:4 + ng], refs[4 + ng:7 + ng], refs[7 + ng:7 + 2 * ng]
        n_sc = refs[7 + 2 * ng]
        i, j = pl.program_id(0), pl.program_id(1)
        if ng:
            _carry(_gather_plan(gather, srcs, outs, *refs[8 + 2 * ng:]), jnp.logical_and(i == 0, j == 0),
                   jnp.logical_and(i == (3 * ni) // 4, j == 0), jnp.logical_and(i == ni - 1, j == nf - 1))

        @pl.when(j == 0)
        def _():
            n_sc[...] = _rms_parts(h_ref[...], g_ref[...])[0].astype(BF)

        n = n_sc[...]
        a = _dot_nt(n, wg_ref[...])
        b = _dot_nt(n, wu_ref[...])
        a_ref[...] = a.astype(BF)
        b_ref[...] = b.astype(BF)
        s_ref[...] = (a * _sigmoid(a) * b).astype(BF)

    tile = pl.BlockSpec((tm, tf), lambda i, j: (i, j))
    w = [pl.BlockSpec((None, tf, D), functools.partial(lambda k, i, j: (k, j, 0), k)) for k in (0, 1)]
    hbm = pl.BlockSpec(memory_space=pl.ANY)
    outs = pl.pallas_call(
        body, name=name, grid=(ni, nf), in_specs=[_row_spec(tm, D), _vec_spec(D)] + w + [hbm] * ng,
        out_specs=[tile, tile, tile] + [hbm] * ng,
        out_shape=[jax.ShapeDtypeStruct((T, F), BF)] * 3 + _gathered_shapes(gather),
        scratch_shapes=[pltpu.VMEM((tm, D), BF)] + _gather_sems(ng),
        compiler_params=_params("arbitrary", "arbitrary"),
    )(h, g.reshape(1, D), wffn, wffn, *gather)
    return outs[0], outs[1], outs[2], list(outs[3:])


def _ffn_fwd_out(s, wffn, h, name, gather=()):
    T, F = s.shape
    D = h.shape[1]
    tm = _tile(T, TM_FFN_IN, 16)
    ni, ng = T // tm, len(gather)

    def body(*refs):
        s_ref, w_ref, h_ref = refs[:3]
        srcs, o_ref, outs = refs[3:3 + ng], refs[3 + ng], refs[4 + ng:4 + 2 * ng]
        i = pl.program_id(0)
        if ng:
            _carry(_gather_plan(gather, srcs, outs, *refs[4 + 2 * ng:]), i == 0, i == (3 * ni) // 4, i == ni - 1)
        o_ref[...] = h_ref[...] + 0.5 * _dot(s_ref[...], w_ref[...])

    hbm = pl.BlockSpec(memory_space=pl.ANY)
    outs = pl.pallas_call(
        body, name=name, grid=(ni,),
        in_specs=[_row_spec(tm, F), pl.BlockSpec((None, F, D), lambda i: (2, 0, 0), pipeline_mode=pl.Buffered(1)),
                  _row_spec(tm, D)] + [hbm] * ng,
        out_specs=[_row_spec(tm, D)] + [hbm] * ng,
        out_shape=[jax.ShapeDtypeStruct((T, D), F32)] + _gathered_shapes(gather),
        scratch_shapes=_gather_sems(ng), compiler_params=_params("arbitrary"),
    )(s, wffn, h, *gather)
    return outs[0], list(outs[1:])


def _ffn_bwd_act(dh, a, b, wffn, name, scatter=()):
    T, D = dh.shape
    F = wffn.shape[1]
    tm, tf = _tile(T, TM_FFN_ACT, 16), _tile(F, TF_FFN, LANES)
    ni, nf, ng = T // tm, F // tf, len(scatter)

    def body(*refs):
        dh_ref, a_ref, b_ref, wd_ref = refs[:4]
        srcs, (da_ref, db_ref, dhb_ref), outs = refs[4:4 + ng], refs[4 + ng:7 + ng], refs[7 + ng:7 + 2 * ng]
        i, j = pl.program_id(0), pl.program_id(1)
        if ng:
            _carry(_scatter_plan(scatter, srcs, outs, *refs[7 + 2 * ng:]), jnp.logical_and(i == 0, j == 0), None,
                   jnp.logical_and(i == ni - 1, j == nf - 1))

        @pl.when(j == 0)
        def _():
            dhb_ref[...] = dh_ref[...].astype(BF)

        ds = 0.5 * _dot_nt(dhb_ref[...], wd_ref[...])
        av, bv = a_ref[...].astype(F32), b_ref[...].astype(F32)
        sig = _sigmoid(av)
        da_ref[...] = (ds * bv * (sig * (1.0 + av * (1.0 - sig)))).astype(BF)
        db_ref[...] = (ds * (av * sig)).astype(BF)

    tile = pl.BlockSpec((tm, tf), lambda i, j: (i, j))
    hbm = pl.BlockSpec(memory_space=pl.ANY)
    outs = pl.pallas_call(
        body, name=name, grid=(ni, nf),
        in_specs=[_row_spec(tm, D), tile, tile, pl.BlockSpec((None, tf, D), lambda i, j: (2, j, 0))] + [hbm] * ng,
        out_specs=[tile, tile, _row_spec(tm, D)] + [hbm] * ng,
        out_shape=[jax.ShapeDtypeStruct((T, F), BF)] * 2 + [jax.ShapeDtypeStruct((T, D), BF)]
        + _scattered_shapes(scatter),
        scratch_shapes=_scatter_sems(ng), compiler_params=_params("arbitrary", "arbitrary"),
    )(dh, a, b, wffn, *[piece[0] for piece in scatter])
    return outs[0], outs[1], outs[2], list(outs[3:])


def _two_dot_norm_bwd(x1, x2, w, w_specs, h, g, dh, name, scatter=()):
    T, K = x1.shape
    D = h.shape[1]
    tm = _tile(T, TM_FFN_IN, 16)
    ni, ng = T // tm, len(scatter)

    def body(*refs):
        x1_ref, x2_ref, w1_ref, w2_ref, h_ref, g_ref, dh_ref = refs[:7]
        srcs, (o_ref, dg_ref), outs = refs[7:7 + ng], refs[7 + ng:9 + ng], refs[9 + ng:9 + 2 * ng]
        i = pl.program_id(0)
        if ng:
            _carry(_scatter_plan(scatter, srcs, outs, *refs[9 + 2 * ng:]), i == 0, None, i == ni - 1)

        @pl.when(i == 0)
        def _():
            dg_ref[...] = jnp.zeros_like(dg_ref)

        dn = _dot(x1_ref[...], w1_ref[...]) + _dot(x2_ref[...], w2_ref[...])
        dx, dg = _rms_bwd_parts(h_ref[...], g_ref[...], dn)
        o_ref[...] = dh_ref[...] + dx
        dg_ref[...] += dg

    hbm = pl.BlockSpec(memory_space=pl.ANY)
    act = pl.BlockSpec((tm, K), lambda i: (i, 0))
    outs = pl.pallas_call(
        body, name=name, grid=(ni,),
        in_specs=[act, act] + w_specs + [_row_spec(tm, D), _vec_spec(D), _row_spec(tm, D)] + [hbm] * ng,
        out_specs=[_row_spec(tm, D), _vec_spec(D)] + [hbm] * ng,
        out_shape=[jax.ShapeDtypeStruct((T, D), F32), jax.ShapeDtypeStruct((1, D), F32)] + _scattered_shapes(scatter),
        scratch_shapes=_scatter_sems(ng), compiler_params=_params("arbitrary"),
    )(x1, x2, w, w, h, g.reshape(1, D), dh, *[piece[0] for piece in scatter])
    return outs[0], outs[1], list(outs[2:])


def _ffn_bwd_in(da, db, wffn, h, g, dh, name, scatter=()):
    F, D = wffn.shape[1:]
    specs = [pl.BlockSpec((None, F, D), functools.partial(lambda k, i: (k, 0, 0), k), pipeline_mode=pl.Buffered(1))
             for k in (0, 1)]
    return _two_dot_norm_bwd(da, db, wffn, specs, h, g, dh, name, scatter)


def _lru_in_bwd(dgb, dxb, win, h, g, dh, name):
    R, D = win.shape[0] // 2, win.shape[1]
    specs = [pl.BlockSpec((R, D), functools.partial(lambda k, i: (k, 0), k), pipeline_mode=pl.Buffered(1)) for k in (0, 1)]
    return _two_dot_norm_bwd(dgb, dxb, win, specs, h, g, dh, name)[:2]


def _ffn_bwd_w(da, db, s, h, g, dhb, name, scatter=()):
    T, F = da.shape
    D = h.shape[1]
    tf, tk = _tile(F, TF_FFN_WG, LANES), _tile(T, TK_FFN_WG, 16)
    nj, nk, ng = F // tf, T // tk, len(scatter)

    def body(*refs):
        da_ref, db_ref, s_ref, h_ref, g_ref, dh_ref = refs[:6]
        srcs, o_ref, outs = refs[6:6 + ng], refs[6 + ng], refs[7 + ng:7 + 2 * ng]
        g_sc, u_sc, d_sc = refs[7 + 2 * ng:10 + 2 * ng]
        j, k = pl.program_id(0), pl.program_id(1)
        if ng:
            _carry(_scatter_plan(scatter, srcs, outs, *refs[10 + 2 * ng:]), jnp.logical_and(j == 0, k == 0), None,
                   jnp.logical_and(j == nj - 1, k == nk - 1))

        @pl.when(k == 0)
        def _():
            g_sc[...] = jnp.zeros_like(g_sc)
            u_sc[...] = jnp.zeros_like(u_sc)
            d_sc[...] = jnp.zeros_like(d_sc)

        nv = _rms_parts(h_ref[...], g_ref[...])[0].astype(BF)
        g_sc[...] += _dot_tn(da_ref[...], nv)
        u_sc[...] += _dot_tn(db_ref[...], nv)
        d_sc[...] += _dot_tn(s_ref[...], dh_ref[...])

        @pl.when(k == nk - 1)
        def _():
            o_ref[0] = g_sc[...].astype(BF)
            o_ref[1] = u_sc[...].astype(BF)
            o_ref[2] = (0.5 * d_sc[...]).astype(BF)

    act = pl.BlockSpec((tk, tf), lambda j, k: (k, j))
    tok = pl.BlockSpec((tk, D), lambda j, k: (k, 0))
    hbm = pl.BlockSpec(memory_space=pl.ANY)
    outs = pl.pallas_call(
        body, name=name, grid=(nj, nk), in_specs=[act, act, act, tok, _vec_spec(D), tok] + [hbm] * ng,
        out_specs=[pl.BlockSpec((3, tf, D), lambda j, k: (0, j, 0), pipeline_mode=pl.Buffered(1))] + [hbm] * ng,
        out_shape=[jax.ShapeDtypeStruct((3, F, D), BF)] + _scattered_shapes(scatter),
        scratch_shapes=[pltpu.VMEM((tf, D), F32)] * 3 + _scatter_sems(ng),
        compiler_params=_params("arbitrary", "arbitrary"),
    )(da, db, s, h, g.reshape(1, D), dhb, *[piece[0] for piece in scatter])
    return outs[0], list(outs[1:])


def _ple_fwd(h, g, wg, wp, p, name):
    T, D = h.shape
    P = p.shape[1]
    tm = _tile(T, TM_EW, 16)

    def body(h_ref, g_ref, wg_ref, wp_ref, p_ref, o_ref, n_ref, gate_ref, pp_ref):
        x = h_ref[...]
        n = _rms_parts(x, g_ref[...])[0].astype(BF)
        gate = _sigmoid(_dot(n, wg_ref[...]))
        pp = _dot_nt(p_ref[...].astype(BF), wp_ref[...])
        o_ref[...] = x + gate * pp
        n_ref[...] = n
        gate_ref[...] = gate.astype(BF)
        pp_ref[...] = pp.astype(BF)

    row = _row_spec(tm, D)
    return pl.pallas_call(
        body, name=name, grid=(T // tm,),
        in_specs=[row, _vec_spec(D), _vec_spec(D, D), _vec_spec(P, D), _row_spec(tm, P)], out_specs=[row] * 4,
        out_shape=[jax.ShapeDtypeStruct((T, D), F32)] + [jax.ShapeDtypeStruct((T, D), BF)] * 3,
        compiler_params=_params("parallel"),
    )(h, g.reshape(1, D), wg, wp, p)


def _ple_bwd(dh, gate, pp, h, g, wg, name):
    T, D = dh.shape
    tm = _tile(T, TM_EW, 16)

    def body(dh_ref, gate_ref, pp_ref, h_ref, g_ref, wg_ref, o_ref, dz_ref, dp_ref, dg_ref):
        @pl.when(pl.program_id(0) == 0)
        def _():
            dg_ref[...] = jnp.zeros_like(dg_ref)

        d, gate = dh_ref[...], gate_ref[...].astype(F32)
        dz = (d * pp_ref[...].astype(F32) * gate * (1.0 - gate)).astype(BF)
        dx, dg = _rms_bwd_parts(h_ref[...], g_ref[...], _dot_nt(dz, wg_ref[...]))
        o_ref[...] = d + dx
        dz_ref[...] = dz
        dp_ref[...] = (d * gate).astype(BF)
        dg_ref[...] += dg

    row = _row_spec(tm, D)
    return pl.pallas_call(
        body, name=name, grid=(T // tm,), in_specs=[row, row, row, row, _vec_spec(D), _vec_spec(D, D)],
        out_specs=[row, row, row, _vec_spec(D)],
        out_shape=[jax.ShapeDtypeStruct((T, D), F32), jax.ShapeDtypeStruct((T, D), BF), jax.ShapeDtypeStruct((T, D), BF),
                   jax.ShapeDtypeStruct((1, D), F32)],
        compiler_params=_params("arbitrary"),
    )(dh, gate, pp, h, g.reshape(1, D), wg)


def _lru_fwd(z, conv_w, conv_b, wa, wx, b_a, b_x, a_param, wout, h, name):
    T, R2 = z.shape
    R, D = R2 // 2, h.shape[1]
    tb = _tile(T, TB_SEQ, HALO)
    per, ng = tb // HALO, tb // SUBLANES

    def body(g_ref, x_ref, halo_ref, cw_ref, cb_ref, wa_ref, wx_ref, ba_ref, bx_ref, ap_ref, wo_ref, h_ref,
             o_ref, xc_ref, r_ref, ig_ref, a_ref, hs_ref, y_ref, ext, carry, a_sc, b_sc):
        i = pl.program_id(0)

        @pl.when(i == 0)
        def _():
            carry[...] = jnp.zeros_like(carry)

        ext[pl.ds(0, HALO), :] = jnp.where(i > 0, halo_ref[...], 0.0)
        ext[pl.ds(HALO, tb), :] = x_ref[...]
        xc = cb_ref[...] + cw_ref[0:1, :] * ext[pl.ds(HALO - 3, tb), :]
        for k in range(1, CONV_WIDTH):
            xc = xc + cw_ref[k:k + 1, :] * ext[pl.ds(HALO - 3 + k, tb), :]
        xcb = xc.astype(BF)
        r = _sigmoid(_dot(xcb, wa_ref[...]) + ba_ref[...])
        ig = _sigmoid(_dot(xcb, wx_ref[...]) + bx_ref[...])
        la = -LRU_C * r * _softplus_neg(ap_ref[...])
        av = jnp.exp(la)
        xc_ref[...] = xc
        r_ref[...] = r
        ig_ref[...] = ig
        a_ref[...] = av
        A = av.reshape(ng, SUBLANES, R)
        B = (jnp.sqrt(_neg_expm1(2.0 * la)) * (ig * xc)).reshape(ng, SUBLANES, R)
        sub = lax.broadcasted_iota(jnp.int32, (1, SUBLANES, R), 1)
        for k in (1, 2, 4):
            m = sub >= k
            a_n = jnp.where(m, pltpu.roll(A, k, 1), 1.0)
            b_n = jnp.where(m, pltpu.roll(B, k, 1), 0.0)
            B = A * b_n + B
            A = A * a_n
        a_sc[...] = A.reshape(tb, R)
        b_sc[...] = B.reshape(tb, R)

        def group(q, c):
            rows = pl.ds(pl.multiple_of(q * SUBLANES, SUBLANES), SUBLANES)
            hg = a_sc[rows, :] * c + b_sc[rows, :]
            hs_ref[rows, :] = hg
            return hg[SUBLANES - 1:SUBLANES, :]

        carry[...] = lax.fori_loop(0, ng, group, carry[...])
        y = (hs_ref[...] * _gelu_parts(g_ref[...])[0]).astype(BF)
        y_ref[...] = y
        o_ref[...] = h_ref[...] + _dot(y, wo_ref[...])

    once = lambda rows, cols: pl.BlockSpec((rows, cols), lambda i: (0, 0), pipeline_mode=pl.Buffered(1))
    gate = pl.BlockSpec((tb, R), lambda i: (i, 0))
    tile = pl.BlockSpec((tb, R), lambda i: (i, 1))
    halo = pl.BlockSpec((HALO, R), lambda i: (jnp.maximum(i * per - 1, 0), 1))
    return pl.pallas_call(
        body, name=name, grid=(T // tb,),
        in_specs=[gate, tile, halo, _vec_spec(R, CONV_WIDTH), _vec_spec(R), once(R, R), once(R, R), _vec_spec(R),
                  _vec_spec(R), _vec_spec(R), once(R, D), _row_spec(tb, D)],
        out_specs=[_row_spec(tb, D)] + [gate] * 6,
        out_shape=[jax.ShapeDtypeStruct((T, D), F32)] + [jax.ShapeDtypeStruct((T, R), F32)] * 5
        + [jax.ShapeDtypeStruct((T, R), BF)],
        scratch_shapes=[pltpu.VMEM((HALO + tb, R), F32), pltpu.VMEM((1, R), F32), pltpu.VMEM((tb, R), F32),
                        pltpu.VMEM((tb, R), F32)],
        compiler_params=_params("arbitrary"),
    )(z, z, z, conv_w, conv_b.reshape(1, R), wa, wx, b_a.reshape(1, R), b_x.reshape(1, R), a_param.reshape(1, R), wout, h)


def _lru_bwd(dh, hs, z, a, r, ig, xc, wa, wx, wout, conv_w, a_param, name):
    T, R = hs.shape
    D = dh.shape[1]
    tb = _tile(T, TB_SEQ, HALO)
    per, nt, ng = tb // HALO, T // tb, tb // SUBLANES

    def body(dh_ref, h_ref, hp_ref, g_ref, x_ref, xp_ref, a_ref, r_ref, ig_ref, xc_ref, wa_ref, wx_ref, wo_ref, cw_ref,
             ap_ref, dgb_ref, dxb_ref, dpa_ref, dpx_ref, dsp_ref, dba_ref, dbx_ref, dcb_ref, dcw_ref,
             hext, xext, dext, carry, later, a_sc, b_sc, d_sc, l_sc):
        i = pl.program_id(0)

        @pl.when(i == 0)
        def _():
            for ref in (dsp_ref, dba_ref, dbx_ref, dcb_ref, dcw_ref, carry, later):
                ref[...] = jnp.zeros_like(ref)

        dy = _dot_nt(dh_ref[...].astype(BF), wo_ref[...])
        gl, dgl = _gelu_parts(g_ref[...])
        hv, av = h_ref[...], a_ref[...]
        dhd = dy * gl
        dgb_ref[...] = (dy * hv * dgl).astype(BF)
        d_sc[...] = dhd
        A = av.reshape(ng, SUBLANES, R)
        B = A * dhd.reshape(ng, SUBLANES, R)
        sub = lax.broadcasted_iota(jnp.int32, (1, SUBLANES, R), 1)
        for k in (1, 2, 4):
            m = sub < SUBLANES - k
            a_n = jnp.where(m, pltpu.roll(A, SUBLANES - k, 1), 1.0)
            b_n = jnp.where(m, pltpu.roll(B, SUBLANES - k, 1), 0.0)
            B = A * b_n + B
            A = A * a_n
        a_sc[...] = A.reshape(tb, R)
        b_sc[...] = B.reshape(tb, R)
        sub8 = lax.broadcasted_iota(jnp.int32, (SUBLANES, R), 0)

        def group(q, c):
            rows = pl.ds(pl.multiple_of((ng - 1 - q) * SUBLANES, SUBLANES), SUBLANES)
            mu = a_sc[rows, :] * c + b_sc[rows, :]
            l_sc[rows, :] = d_sc[rows, :] + jnp.where(sub8 == SUBLANES - 1, c, pltpu.roll(mu, SUBLANES - 1, 0))
            return mu[0:1, :]

        carry[...] = lax.fori_loop(0, ng, group, carry[...])
        lam = l_sc[...]
        hext[pl.ds(0, HALO), :] = jnp.where(i < nt - 1, hp_ref[...], 0.0)
        hext[pl.ds(HALO, tb), :] = hv
        h_prev = hext[pl.ds(HALO - 1, tb), :]
        rv, igv, xcv = r_ref[...], ig_ref[...], xc_ref[...]
        sp = _softplus_neg(ap_ref[...])
        mult = jnp.sqrt(_neg_expm1(2.0 * (-LRU_C * rv * sp)))
        dla = lam * h_prev * av - lam * (igv * xcv) * (av * av) / mult
        du = lam * mult
        dpa = (dla * (-LRU_C) * sp) * rv * (1.0 - rv)
        dpx = (du * xcv) * igv * (1.0 - igv)
        dsp_ref[...] += jnp.sum(dla * (-LRU_C) * rv, axis=0, keepdims=True)
        dba_ref[...] += jnp.sum(dpa, axis=0, keepdims=True)
        dbx_ref[...] += jnp.sum(dpx, axis=0, keepdims=True)
        dpab, dpxb = dpa.astype(BF), dpx.astype(BF)
        dpa_ref[...] = dpab
        dpx_ref[...] = dpxb
        dxc = du * igv + _dot_nt(dpab, wa_ref[...]) + _dot_nt(dpxb, wx_ref[...])
        dext[pl.ds(0, tb), :] = dxc
        dext[pl.ds(tb, SUBLANES), :] = later[...]
        later[...] = dxc[0:SUBLANES, :]
        xext[pl.ds(0, HALO), :] = jnp.where(i < nt - 1, xp_ref[...], 0.0)
        xext[pl.ds(HALO, tb), :] = x_ref[...]
        dxb = cw_ref[CONV_WIDTH - 1:CONV_WIDTH, :] * dxc
        for k in range(CONV_WIDTH - 1):
            dxb = dxb + cw_ref[k:k + 1, :] * dext[pl.ds(CONV_WIDTH - 1 - k, tb), :]
        dxb_ref[...] = dxb.astype(BF)
        for k in range(CONV_WIDTH):
            dcw_ref[k:k + 1, :] += jnp.sum(dxc * xext[pl.ds(HALO - 3 + k, tb), :], axis=0, keepdims=True)
        dcb_ref[...] += jnp.sum(dxc, axis=0, keepdims=True)

        @pl.when(i == nt - 1)
        def _():
            dsp_ref[...] = dsp_ref[...] * (-_sigmoid(-ap_ref[...]))

    once = lambda rows, cols: pl.BlockSpec((rows, cols), lambda i: (0, 0), pipeline_mode=pl.Buffered(1))
    tile = lambda col: pl.BlockSpec((tb, R), lambda i: (nt - 1 - i, col))
    prev = lambda col: pl.BlockSpec((HALO, R), lambda i: (jnp.maximum((nt - 1 - i) * per - 1, 0), col))
    t0 = tile(0)
    return pl.pallas_call(
        body, name=name, grid=(nt,),
        in_specs=[pl.BlockSpec((tb, D), lambda i: (nt - 1 - i, 0)), t0, prev(0), t0, tile(1), prev(1), t0, t0, t0, t0,
                  once(R, R), once(R, R), once(R, D), _vec_spec(R, CONV_WIDTH), _vec_spec(R)],
        out_specs=[t0] * 4 + [_vec_spec(R)] * 4 + [_vec_spec(R, SUBLANES)],
        out_shape=[jax.ShapeDtypeStruct((T, R), BF)] * 4 + [jax.ShapeDtypeStruct((1, R), F32)] * 4
        + [jax.ShapeDtypeStruct((SUBLANES, R), F32)],
        scratch_shapes=[pltpu.VMEM((HALO + tb, R), F32), pltpu.VMEM((HALO + tb, R), F32),
                        pltpu.VMEM((tb + SUBLANES, R), F32), pltpu.VMEM((1, R), F32), pltpu.VMEM((SUBLANES, R), F32)]
        + [pltpu.VMEM((tb, R), F32)] * 4,
        compiler_params=_params("arbitrary"),
    )(dh, hs, hs, z, z, z, a, r, ig, xc, wa, wx, wout, conv_w, a_param.reshape(1, R))


def _gate_spans(R):
    d = R // LRU_HEADS
    spans = [min((j * GATE_COLS // d) * d // LANES * LANES, R - GATE_SPAN) for j in range(R // GATE_COLS)]
    assert R % GATE_COLS == 0 and all(lo + GATE_SPAN >= (((j + 1) * GATE_COLS - 1) // d + 1) * d for j, lo in enumerate(spans))
    return spans


def _lru_gates_dw(xc, dpa, dpx, name):
    T, R = xc.shape
    tk = _tile(T, 1024, 16)
    spans = _gate_spans(R)
    nb = len(spans)

    def body(x_ref, a_ref, b_ref, o_ref):
        @pl.when(pl.program_id(0) == 0)
        def _():
            o_ref[...] = jnp.zeros_like(o_ref)

        for j, lo in enumerate(spans):
            xs = x_ref[:, pl.ds(lo, GATE_SPAN)].astype(BF)
            cols = pl.ds(j * GATE_COLS, GATE_COLS)
            o_ref[0, j] += _dot_tn(xs, a_ref[:, cols])
            o_ref[1, j] += _dot_tn(xs, b_ref[:, cols])

    row = _row_spec(tk, R)
    out = pl.pallas_call(
        body, name=name, grid=(T // tk,), in_specs=[row, row, row],
        out_specs=pl.BlockSpec((2, nb, GATE_SPAN, GATE_COLS), lambda i: (0, 0, 0, 0)),
        out_shape=jax.ShapeDtypeStruct((2, nb, GATE_SPAN, GATE_COLS), F32), compiler_params=_params("arbitrary"),
    )(xc, dpa, dpx)
    dense = jnp.zeros((2, R, R), F32)
    for j, lo in enumerate(spans):
        dense = dense.at[:, lo:lo + GATE_SPAN, j * GATE_COLS:(j + 1) * GATE_COLS].set(out[:, j])
    return dense[0], dense[1]


def _window_sums(e, n, back):
    out, s = [], e
    for k in (1, 2, 4, 8):
        s = s + pltpu.roll(s, k if back else n - k, 0)
        out.append(s)
    return out


def _pool_fwd(h, g, w, b, scale, name):
    T, D = h.shape
    G = len(POOL_WINDOWS)
    gd = D // G
    tb = _tile(T, TB_SEQ, HALO)
    per = tb // HALO

    def body(h_ref, hp_ref, g_ref, w_ref, b_ref, s_ref, o_ref, u_ref, yb_ref):
        i = pl.program_id(0)
        t = i * tb + lax.broadcasted_iota(jnp.int32, (tb, gd), 0) + 1
        hv = h_ref[...]
        xn = _rms_parts(hv, g_ref[...])[0]
        xp = jnp.where(i > 0, _rms_parts(hp_ref[...], g_ref[...])[0], 0.0)
        for k, win in enumerate(POOL_WINDOWS):
            cols = slice(k * gd, (k + 1) * gd)
            x = xn[:, cols]
            e = jnp.concatenate([xp[:, cols], x], axis=0)
            sw = _window_sums(e, HALO + tb, True)[k][HALO:, :]
            u = (sw / jnp.minimum(t, win).astype(F32) - x).astype(BF)
            yb = _dot(u, w_ref[k]) + b_ref[:, cols]
            u_ref[:, cols] = u
            yb_ref[:, cols] = yb
            o_ref[:, cols] = hv[:, cols] + yb * s_ref[:, cols]

    tile = _row_spec(tb, D)
    prev = pl.BlockSpec((HALO, D), lambda i: (jnp.maximum(i * per - 1, 0), 0))
    return pl.pallas_call(
        body, name=name, grid=(T // tb,),
        in_specs=[tile, prev, _vec_spec(D), pl.BlockSpec((G, gd, gd), lambda i: (0, 0, 0)), _vec_spec(D), _vec_spec(D)],
        out_specs=[tile, tile, tile],
        out_shape=[jax.ShapeDtypeStruct((T, D), F32), jax.ShapeDtypeStruct((T, D), BF), jax.ShapeDtypeStruct((T, D), F32)],
        compiler_params=_params("parallel"),
    )(h, h, g.reshape(1, D), w, b.reshape(1, D), scale.reshape(1, D))


def _pool_bwd(dm, u, yb, w, scale, name):
    T, D = dm.shape
    G = len(POOL_WINDOWS)
    gd = D // G
    tb = _tile(T, TB_SEQ, HALO)

    def body(d_ref, u_ref, yb_ref, w_ref, s_ref, du_ref, v_ref, dw_ref, db_ref, ds_ref):
        i = pl.program_id(0)

        @pl.when(i == 0)
        def _():
            dw_ref[...] = jnp.zeros_like(dw_ref)
            db_ref[...] = jnp.zeros_like(db_ref)
            ds_ref[...] = jnp.zeros_like(ds_ref)

        d, sc = d_ref[...], s_ref[...]
        ds_ref[...] += jnp.sum(d * yb_ref[...], axis=0, keepdims=True)
        db_ref[...] += jnp.sum(d * sc, axis=0, keepdims=True)
        t = i * tb + lax.broadcasted_iota(jnp.int32, (tb, gd), 0) + 1
        for g, win in enumerate(POOL_WINDOWS):
            cols = pl.ds(g * gd, gd)
            dy = (d_ref[:, cols] * s_ref[:, cols]).astype(BF)
            du = _dot_nt(dy, w_ref[g])
            dw_ref[g] += _dot_tn(u_ref[:, cols], dy)
            du_ref[:, cols] = du
            v_ref[:, cols] = du / jnp.minimum(t, win).astype(F32)

    tile = _row_spec(tb, D)
    return pl.pallas_call(
        body, name=name, grid=(T // tb,),
        in_specs=[tile, tile, tile, pl.BlockSpec((G, gd, gd), lambda i: (0, 0, 0)), _vec_spec(D)],
        out_specs=[tile, tile, pl.BlockSpec((G, gd, gd), lambda i: (0, 0, 0)), _vec_spec(D), _vec_spec(D)],
        out_shape=[jax.ShapeDtypeStruct((T, D), F32), jax.ShapeDtypeStruct((T, D), F32),
                   jax.ShapeDtypeStruct((G, gd, gd), F32), jax.ShapeDtypeStruct((1, D), F32),
                   jax.ShapeDtypeStruct((1, D), F32)],
        compiler_params=_params("arbitrary"),
    )(dm, u, yb, w, scale.reshape(1, D))


def _pool_bwd_win(v, du, h, g, dh, name):
    T, D = v.shape
    G = len(POOL_WINDOWS)
    gd = D // G
    tb = _tile(T, TB_SEQ, HALO)
    per = tb // HALO
    nt = T // tb

    def body(v_ref, vn_ref, du_ref, h_ref, g_ref, dh_ref, o_ref, dg_ref):
        i = pl.program_id(0)

        @pl.when(i == 0)
        def _():
            dg_ref[...] = jnp.zeros_like(dg_ref)

        parts = []
        for k in range(G):
            cols = pl.ds(k * gd, gd)
            e = jnp.concatenate([v_ref[:, cols], jnp.where(i < nt - 1, vn_ref[:, cols], 0.0)], axis=0)
            parts.append(_window_sums(e, tb + HALO, False)[k][:tb, :] - du_ref[:, cols])
        dx, dg = _rms_bwd_parts(h_ref[...], g_ref[...], jnp.concatenate(parts, axis=1))
        o_ref[...] = dh_ref[...] + dx
        dg_ref[...] += dg

    tile = _row_spec(tb, D)
    nxt = pl.BlockSpec((HALO, D), lambda i: (jnp.minimum((i + 1) * per, T // HALO - 1), 0))
    return pl.pallas_call(
        body, name=name, grid=(nt,), in_specs=[tile, nxt, tile, tile, _vec_spec(D), tile], out_specs=[tile, _vec_spec(D)],
        out_shape=[jax.ShapeDtypeStruct((T, D), F32), jax.ShapeDtypeStruct((1, D), F32)],
        compiler_params=_params("arbitrary"),
    )(v, v, du, h, g.reshape(1, D), dh)


def _adamw(w, g, m, v, name):
    shape = w.shape
    cols = shape[-1] if w.ndim > 1 else shape[0]
    rows = w.size // cols
    tr = _tile(rows, TR_ADAM, SUBLANES)
    c1, c2 = 1.0 / (1.0 - ADAM_B1 ** ADAM_STEP), 1.0 / (1.0 - ADAM_B2 ** ADAM_STEP)

    def body(w_ref, g_ref, m_ref, v_ref, d_ref, mo_ref, vo_ref):
        gv = g_ref[...]
        mn = ADAM_B1 * m_ref[...] + (1.0 - ADAM_B1) * gv
        vn = ADAM_B2 * v_ref[...] + (1.0 - ADAM_B2) * (gv * gv)
        d_ref[...] = -ADAM_LR * ((mn * c1) / (jnp.sqrt(vn * c2) + ADAM_EPS) + ADAM_WD * w_ref[...])
        mo_ref[...] = mn
        vo_ref[...] = vn

    spec = _row_spec(tr, cols)
    outs = pl.pallas_call(
        body, name=name, grid=(rows // tr,), in_specs=[spec] * 4, out_specs=[spec] * 3,
        out_shape=[jax.ShapeDtypeStruct((rows, cols), F32)] * 3, compiler_params=_params("parallel"),
    )(*[t.reshape(rows, cols) for t in (w, g, m, v)])
    return [o.reshape(shape) for o in outs]


def _sum_devices(parts, name):
    n, rows, cols = parts.shape
    tr = _tile(rows, 1024, SUBLANES)

    def body(p_ref, o_ref):
        acc = p_ref[0].astype(F32)
        for k in range(1, n):
            acc = acc + p_ref[k].astype(F32)
        o_ref[...] = acc

    return pl.pallas_call(
        body, name=name, grid=(rows // tr,), in_specs=[pl.BlockSpec((n, tr, cols), lambda i: (0, i, 0))],
        out_specs=_row_spec(tr, cols), out_shape=jax.ShapeDtypeStruct((rows, cols), F32),
        compiler_params=_params("parallel"),
    )(parts)


def _position():
    return lax.axis_index("x"), lax.axis_index("y"), lax.axis_index("c")


def _gathered_shapes(blocks):
    return [jax.ShapeDtypeStruct((b.shape[0], N_DEV * b.shape[1], b.shape[2]), b.dtype) for b in blocks]


def _gather_sems(ng):
    return [pltpu.SemaphoreType.DMA((ng, 7)), pltpu.SemaphoreType.DMA((ng, 7)), pltpu.SemaphoreType.DMA((ng,))] if ng else []


def _gather_plan(blocks, srcs, outs, send_sems, recv_sems, local_sems):
    ng = len(blocks)
    x, y, c = _position()
    me, sibling = (x, y, c), (x, y, 1 - c)
    chips = [(1 - x, y), (x, 1 - y), (1 - x, 1 - y)]

    def rows(g, px, py, pc):
        r = blocks[g].shape[1]
        return outs[g].at[:, pl.ds((4 * px + 2 * py + pc) * r, r), :]

    def copy(g, k, block, to, src=None):
        return pltpu.make_async_remote_copy(
            src_ref=rows(g, *block) if src is None else src, dst_ref=rows(g, *block),
            send_sem=send_sems.at[g, k], recv_sem=recv_sems.at[g, k], device_id=to, device_id_type=MESH)

    def mine(g):
        return pltpu.make_async_copy(srcs[g], rows(g, *me), local_sems.at[g])

    def first(g):
        return [copy(g, 0, me, sibling, src=srcs[g])] + [copy(g, 1 + j, me, (*chip, c), src=srcs[g])
                                                         for j, chip in enumerate(chips)]

    def passed(g):
        return [copy(g, 4 + j, (*chip, c), sibling) for j, chip in enumerate(chips)]

    def start():
        for g in range(ng):
            mine(g).start()
            for cp in first(g):
                cp.start()

    def forward():
        for j, chip in enumerate(chips):
            for g in range(ng):
                copy(g, 1 + j, (*chip, c), me).wait_recv()
                copy(g, 4 + j, (*chip, c), sibling).start()

    def finish():
        for g in range(ng):
            copy(g, 0, sibling, me).wait_recv()
            for j, chip in enumerate(chips):
                copy(g, 4 + j, (*chip, 1 - c), me).wait_recv()
            for cp in first(g) + passed(g):
                cp.wait_send()
            mine(g).wait()

    return start, forward, finish


def _all_gather(blocks, name):
    ng = len(blocks)

    def body(*refs):
        start, forward, finish = _gather_plan(blocks, refs[:ng], refs[ng:2 * ng], *refs[2 * ng:])
        start()
        forward()
        finish()

    hbm = pl.BlockSpec(memory_space=pl.ANY)
    return pl.pallas_call(
        body, name=name, in_specs=[hbm] * ng, out_specs=[hbm] * ng, out_shape=_gathered_shapes(blocks),
        scratch_shapes=_gather_sems(ng),
    )(*blocks)


FLIPS = ((0, 0, 1), (1, 0, 0), (0, 1, 0), (1, 1, 0), (1, 0, 1), (0, 1, 1), (1, 1, 1))


def _piece_rows(piece):
    arr, m = piece
    return arr.shape[0] if m is None else 1


def _scattered_shapes(pieces):
    return [jax.ShapeDtypeStruct((N_DEV, _piece_rows(p), p[0].shape[1] // N_DEV, p[0].shape[2]), p[0].dtype)
            for p in pieces]


def _scatter_sems(ng):
    n = len(FLIPS)
    return [pltpu.SemaphoreType.DMA((ng, n)), pltpu.SemaphoreType.DMA((ng, n)), pltpu.SemaphoreType.DMA((ng,))] if ng else []


def _scatter_plan(pieces, srcs, outs, send_sems, recv_sems, local_sems):
    x, y, c = _position()

    def block(g, tx, ty, tc):
        arr, m = pieces[g]
        r = arr.shape[1] // N_DEV
        lead = slice(None) if m is None else pl.ds(m, 1)
        return srcs[g].at[lead, pl.ds((4 * tx + 2 * ty + tc) * r, r), :]

    def copies(g):
        out = []
        for k, (fx, fy, fc) in enumerate(FLIPS):
            tx, ty, tc = (1 - x if fx else x), (1 - y if fy else y), (1 - c if fc else c)
            out.append(pltpu.make_async_remote_copy(
                src_ref=block(g, tx, ty, tc), dst_ref=outs[g].at[k], send_sem=send_sems.at[g, k],
                recv_sem=recv_sems.at[g, k], device_id=(tx, ty, tc), device_id_type=MESH))
        return out

    def mine(g):
        return pltpu.make_async_copy(block(g, x, y, c), outs[g].at[len(FLIPS)], local_sems.at[g])

    def start():
        for g in range(len(pieces)):
            mine(g).start()
            for cp in copies(g):
                cp.start()

    def finish():
        for g in range(len(pieces)):
            for cp in copies(g):
                cp.wait()
            mine(g).wait()

    return start, finish


def _scatter_and_gather(pieces, blocks, name):
    n_p, n_b = len(pieces), len(blocks)

    def body(*refs):
        ins, outs, sems = refs[:n_p + n_b], refs[n_p + n_b:2 * (n_p + n_b)], refs[2 * (n_p + n_b):]
        s_start, s_finish = _scatter_plan(pieces, ins[:n_p], outs[:n_p], *sems[:3])
        g_start, g_forward, g_finish = _gather_plan(blocks, ins[n_p:], outs[n_p:], *sems[3:])
        s_start()
        g_start()
        g_forward()
        g_finish()
        s_finish()

    hbm = pl.BlockSpec(memory_space=pl.ANY)
    outs = pl.pallas_call(
        body, name=name, in_specs=[hbm] * (n_p + n_b), out_specs=[hbm] * (n_p + n_b),
        out_shape=_scattered_shapes(pieces) + _gathered_shapes(blocks),
        scratch_shapes=_scatter_sems(n_p) + _gather_sems(n_b),
    )(*[p[0] for p in pieces], *blocks)
    return list(outs[:n_p]), list(outs[n_p:])


def _scatter_sum(recv, name):
    _, n, r, c = recv.shape

    def body(r_ref, o_ref):
        acc = r_ref[len(FLIPS)].astype(F32)
        for k in range(len(FLIPS)):
            acc = acc + r_ref[k].astype(F32)
        o_ref[...] = acc

    return pl.pallas_call(
        body, name=name, grid=(n,), in_specs=[pl.BlockSpec((N_DEV, None, r, c), lambda i: (0, i, 0, 0))],
        out_specs=pl.BlockSpec((None, r, c), lambda i: (i, 0, 0)),
        out_shape=jax.ShapeDtypeStruct((n, r, c), F32), compiler_params=_params("parallel"),
    )(recv)


def _block_diag(w):
    H, d, _ = w.shape
    return (jnp.eye(H, dtype=w.dtype)[:, None, :, None] * w[:, :, None, :]).reshape(H * d, H * d)


def _diag_blocks(dense, H):
    d = dense.shape[0] // H
    return jnp.stack([dense[i * d:(i + 1) * d, i * d:(i + 1) * d] for i in range(H)])


def _local_step(x, p, tgt, W, blocks=None):
    dist = blocks is not None
    L = p.shape[0]
    W = dict(W)

    def gathering(keys):
        return [k for k in keys if k not in W] if dist else []

    def ffn_fwd(h, g, i, f, during_act, during_out):
        w = W[("ffn", i, f)]
        keys = gathering(during_act)
        a, b, s, got = _ffn_fwd_act(h, g, w, f"ffn{f}_fwd_act_{i}", gather=[blocks[k] for k in keys])
        W.update(zip(keys, got))
        keys = gathering(during_out)
        h, got = _ffn_fwd_out(s, w, h, f"ffn{f}_fwd_out_{i}", gather=[blocks[k] for k in keys])
        W.update(zip(keys, got))
        return a, b, s, h

    saved = []
    h = x
    for i in range(L):
        j = i // 2
        lru = i % 2 == 0
        s = {"h0": h}
        mixer = [("lru_in", j), ("lru_out", j)] if lru else [("pool_w", j)]
        s["a1"], s["b1"], s["s1"], h = ffn_fwd(h, W["ffn1_norm"][i], i, 1, [("ffn", i, 2)], mixer)
        s["h1"] = h
        if lru:
            hn = _rms_fwd(h, W["mix_norm"][i], BF, f"mix_norm_{i}")
            z = _mm(hn, W[("lru_in", j)][0], "nt", f"lru_in_{i}")
            wa, wx = _block_diag(W["lru_w_a"][j]).astype(BF), _block_diag(W["lru_w_x"][j]).astype(BF)
            h, xc, r, ig, a, hs, y = _lru_fwd(z, W["lru_conv_w"][j], W["lru_conv_b"][j], wa, wx, W["lru_b_a"][j],
                                              W["lru_b_x"][j], W["lru_a_param"][j], W[("lru_out", j)][0], h, f"lru_fwd_{i}")
            s.update(hn=hn, z=z, wa=wa, wx=wx, xc=xc, r=r, ig=ig, a=a, hs=hs, y=y)
        else:
            h, s["u"], s["yb"] = _pool_fwd(h, W["mix_norm"][i], W[("pool_w", j)], W["pool_b"][j], W["pool_scale"][j],
                                           f"pool_fwd_{i}")
        s["h2"] = h
        s["a2"], s["b2"], s["s2"], h = ffn_fwd(h, W["ffn2_norm"][i], i, 2, [("ffn", i + 1, 1)] if i + 1 < L else [],
                                               [("ple_gate", i), ("ple_proj", i)])
        s["h3"] = h
        h, s["n4"], s["gate"], s["pp"] = _ple_fwd(h, W["ple_norm"][i], W[("ple_gate", i)][0], W[("ple_proj", i)][0], p[i],
                                                  f"ple_fwd_{i}")
        saved.append(s)

    loss, dh, d_final = _loss_head(h, W["final_norm"], tgt)

    big, recv = {}, {}
    n_lru, n_pool = L // 2 + L % 2, L // 2
    small = {k: [None] * L for k in ("ffn1_norm", "mix_norm", "ffn2_norm", "ple_norm")}
    for k in ("lru_conv_w", "lru_conv_b", "lru_w_a", "lru_b_a", "lru_w_x", "lru_b_x", "lru_a_param"):
        small[k] = [None] * n_lru
    for k in ("pool_b", "pool_scale"):
        small[k] = [None] * n_pool

    def scattering(pieces):
        return [(k, m) for k, m in pieces if k in big] if dist else []

    def ffn_bwd(dh, h_in, g, a, b, sv, i, f, during):
        key, w = ("ffn", i, f), W[("ffn", i, f)]
        out = [scattering(d) for d in during]
        sent = [[(big[k], m) for k, m in o] for o in out]
        da, db, dhb, got0 = _ffn_bwd_act(dh, a, b, w, f"ffn{f}_bwd_act_{i}", scatter=sent[0])
        big[key], got1 = _ffn_bwd_w(da, db, sv, h_in, g, dhb, f"ffn{f}_dw_{i}", scatter=sent[1])
        out.append(scattering([(key, 0)] + ([(key, 1)] if (i, f) == (0, 1) else [])))
        dh, dg, got2 = _ffn_bwd_in(da, db, w, h_in, g, dh, f"ffn{f}_bwd_in_{i}", scatter=[(big[k], m) for k, m in out[2]])
        for o, got in zip(out, (got0, got1, got2)):
            recv.update(zip(o, got))
        return dh, dg

    for i in reversed(range(L)):
        j = i // 2
        lru = i % 2 == 0
        s = saved[i]
        dh, dz, dpp, dg = _ple_bwd(dh, s["gate"], s["pp"], s["h3"], W["ple_norm"][i], W[("ple_gate", i)][0],
                                   f"ple_bwd_{i}")
        big[("ple_gate", i)] = _mm(s["n4"], dz, "tn", f"ple_gate_dw_{i}", out_dtype=BF)[None]
        big[("ple_proj", i)] = _mm(dpp, p[i], "tn", f"ple_proj_dw_{i}", out_dtype=BF)[None]
        small["ple_norm"][i] = dg[0]
        above = ("ffn", i + 1, 1)
        dh, dg = ffn_bwd(dh, s["h2"], W["ffn2_norm"][i], s["a2"], s["b2"], s["s2"], i, 2, [
            [(above, 1)], [(above, 2), (("ple_gate", i), None), (("ple_proj", i), None)]])
        small["ffn2_norm"][i] = dg[0]
        if lru:
            big[("lru_out", j)] = _mm(s["y"], dh, "tn", f"lru_out_dw_{i}", out_dtype=BF)[None]
            dgb, dxb, dpa, dpx, dsp, dba, dbx, dcb, dcw = _lru_bwd(
                dh, s["hs"], s["z"], s["a"], s["r"], s["ig"], s["xc"], s["wa"], s["wx"], W[("lru_out", j)][0],
                W["lru_conv_w"][j], W["lru_a_param"][j], f"lru_bwd_{i}")
            small["lru_a_param"][j], small["lru_b_a"][j], small["lru_b_x"][j] = dsp[0], dba[0], dbx[0]
            dwa, dwx = _lru_gates_dw(s["xc"], dpa, dpx, f"lru_gates_dw_{i}")
            small["lru_w_a"][j], small["lru_w_x"][j] = _diag_blocks(dwa, LRU_HEADS), _diag_blocks(dwx, LRU_HEADS)
            small["lru_conv_w"][j], small["lru_conv_b"][j] = dcw[:CONV_WIDTH], dcb[0]
            big[("lru_in", j)] = jnp.concatenate([_mm(dgb, s["hn"], "tn", f"lru_in_dw_g_{i}", out_dtype=BF),
                                                  _mm(dxb, s["hn"], "tn", f"lru_in_dw_x_{i}", out_dtype=BF)])[None]
            dh, dg = _lru_in_bwd(dgb, dxb, W[("lru_in", j)][0], s["h1"], W["mix_norm"][i], dh, f"lru_in_bwd_{i}")
            mixer = [("lru_in", j), ("lru_out", j)]
        else:
            du, v, dw, dbp, dsc = _pool_bwd(dh, s["u"], s["yb"], W[("pool_w", j)], W["pool_scale"][j], f"pool_bwd_{i}")
            big[("pool_w", j)] = dw.astype(BF)
            small["pool_b"][j], small["pool_scale"][j] = dbp[0], dsc[0]
            dh, dg = _pool_bwd_win(v, du, s["h1"], W["mix_norm"][i], dh, f"pool_bwd_win_{i}")
            mixer = [("pool_w", j)]
        small["mix_norm"][i] = dg[0]
        second = ("ffn", i, 2)
        dh, dg = ffn_bwd(dh, s["h0"], W["ffn1_norm"][i], s["a1"], s["b1"], s["s1"], i, 1, [
            [(second, 1)], [(second, 2)] + [(k, None) for k in mixer]])
        small["ffn1_norm"][i] = dg[0]

    small = {k: jnp.stack(v) for k, v in small.items()}
    small["final_norm"] = d_final[0]
    return loss, dh, big, recv, small


SMALL_SHARDED = ("pool_b", "pool_scale", "lru_conv_w")
SMALL = ("ffn1_norm", "mix_norm", "ffn2_norm", "ple_norm", "final_norm", "lru_conv_b", "lru_w_a", "lru_b_a",
         "lru_w_x", "lru_b_x", "lru_a_param", "pool_b", "pool_scale", "lru_conv_w")


def _pack_big(w):
    t = lambda a: jnp.swapaxes(a, -1, -2)
    out = {}
    for i in range(w["ffn1_norm"].shape[0]):
        for f in (1, 2):
            out[("ffn", i, f)] = jnp.stack([t(w[f"ffn{f}_w_gate"][i]), t(w[f"ffn{f}_w_up"][i]), w[f"ffn{f}_w_down"][i]])
        out[("ple_gate", i)], out[("ple_proj", i)] = w["ple_w_gate"][i][None], t(w["ple_w_proj"][i])[None]
    for j in range(w["lru_w_in"].shape[0]):
        out[("lru_in", j)], out[("lru_out", j)] = t(w["lru_w_in"][j])[None], w["lru_w_out"][j][None]
    for j in range(w["pool_w"].shape[0]):
        out[("pool_w", j)] = w["pool_w"][j]
    return out


def _unpack_big(b, L):
    t = lambda a: jnp.swapaxes(a, -1, -2)
    n_lru, n_pool = L // 2 + L % 2, L // 2
    out = {"lru_w_in": jnp.stack([t(b[("lru_in", j)][0]) for j in range(n_lru)]),
           "lru_w_out": jnp.stack([b[("lru_out", j)][0] for j in range(n_lru)]),
           "pool_w": jnp.stack([b[("pool_w", j)] for j in range(n_pool)]),
           "ple_w_gate": jnp.stack([b[("ple_gate", i)][0] for i in range(L)]),
           "ple_w_proj": jnp.stack([t(b[("ple_proj", i)][0]) for i in range(L)])}
    for f in (1, 2):
        out[f"ffn{f}_w_gate"] = jnp.stack([t(b[("ffn", i, f)][0]) for i in range(L)])
        out[f"ffn{f}_w_up"] = jnp.stack([t(b[("ffn", i, f)][1]) for i in range(L)])
        out[f"ffn{f}_w_down"] = jnp.stack([b[("ffn", i, f)][2] for i in range(L)])
    return out


def _flatten(parts, names, rows_of=LANES):
    flat = jnp.concatenate([parts[k].reshape(-1) for k in names])
    pad = (-flat.size) % (16 * rows_of)
    return jnp.pad(flat, (0, pad)).reshape(1, -1, rows_of)


def _unflatten(flat, like, names):
    out, o = {}, 0
    flat = flat.reshape(-1)
    for k in names:
        n = like[k].size
        out[k] = flat[o:o + n].reshape(like[k].shape)
        o += n
    return out


def kernel(x, p, ffn1_norm, ffn1_w_gate, ffn1_w_up, ffn1_w_down, mix_norm, lru_w_in, lru_conv_w, lru_conv_b, lru_w_a, lru_b_a, lru_w_x, lru_b_x, lru_a_param, lru_w_out, pool_w, pool_b, pool_scale, ffn2_norm, ffn2_w_gate, ffn2_w_up, ffn2_w_down, ple_norm, ple_w_gate, ple_w_proj, final_norm, loss_target, m_ffn1_norm, m_ffn1_w_gate, m_ffn1_w_up, m_ffn1_w_down, m_mix_norm, m_lru_w_in, m_lru_conv_w, m_lru_conv_b, m_lru_w_a, m_lru_b_a, m_lru_w_x, m_lru_b_x, m_lru_a_param, m_lru_w_out, m_pool_w, m_pool_b, m_pool_scale, m_ffn2_norm, m_ffn2_w_gate, m_ffn2_w_up, m_ffn2_w_down, m_ple_norm, m_ple_w_gate, m_ple_w_proj, m_final_norm, v_ffn1_norm, v_ffn1_w_gate, v_ffn1_w_up, v_ffn1_w_down, v_mix_norm, v_lru_w_in, v_lru_conv_w, v_lru_conv_b, v_lru_w_a, v_lru_b_a, v_lru_w_x, v_lru_b_x, v_lru_a_param, v_lru_w_out, v_pool_w, v_pool_b, v_pool_scale, v_ffn2_norm, v_ffn2_w_gate, v_ffn2_w_up, v_ffn2_w_down, v_ple_norm, v_ple_w_gate, v_ple_w_proj, v_final_norm):
    names = ["ffn1_norm", "ffn1_w_gate", "ffn1_w_up", "ffn1_w_down", "mix_norm", "lru_w_in", "lru_conv_w", "lru_conv_b",
             "lru_w_a", "lru_b_a", "lru_w_x", "lru_b_x", "lru_a_param", "lru_w_out", "pool_w", "pool_b", "pool_scale",
             "ffn2_norm", "ffn2_w_gate", "ffn2_w_up", "ffn2_w_down", "ple_norm", "ple_w_gate", "ple_w_proj", "final_norm"]
    w = dict(zip(names, [ffn1_norm, ffn1_w_gate, ffn1_w_up, ffn1_w_down, mix_norm, lru_w_in, lru_conv_w, lru_conv_b, lru_w_a, lru_b_a, lru_w_x, lru_b_x, lru_a_param, lru_w_out, pool_w, pool_b, pool_scale, ffn2_norm, ffn2_w_gate, ffn2_w_up, ffn2_w_down, ple_norm, ple_w_gate, ple_w_proj, final_norm]))
    m = dict(zip(names, [m_ffn1_norm, m_ffn1_w_gate, m_ffn1_w_up, m_ffn1_w_down, m_mix_norm, m_lru_w_in, m_lru_conv_w, m_lru_conv_b, m_lru_w_a, m_lru_b_a, m_lru_w_x, m_lru_b_x, m_lru_a_param, m_lru_w_out, m_pool_w, m_pool_b, m_pool_scale, m_ffn2_norm, m_ffn2_w_gate, m_ffn2_w_up, m_ffn2_w_down, m_ple_norm, m_ple_w_gate, m_ple_w_proj, m_final_norm]))
    v = dict(zip(names, [v_ffn1_norm, v_ffn1_w_gate, v_ffn1_w_up, v_ffn1_w_down, v_mix_norm, v_lru_w_in, v_lru_conv_w, v_lru_conv_b, v_lru_w_a, v_lru_b_a, v_lru_w_x, v_lru_b_x, v_lru_a_param, v_lru_w_out, v_pool_w, v_pool_b, v_pool_scale, v_ffn2_norm, v_ffn2_w_gate, v_ffn2_w_up, v_ffn2_w_down, v_ple_norm, v_ple_w_gate, v_ple_w_proj, v_final_norm]))
    L = p.shape[0]
    px, py, pc = _position()
    me = 4 * px + 2 * py + pc

    blocks = {k: b.astype(BF) for k, b in _pack_big(w).items()}
    first = ("ffn", 0, 1)
    got, small_blocks = _all_gather([blocks[first], _flatten(w, SMALL_SHARDED)], "gather_first")
    W = {first: got}
    per_dev = small_blocks.reshape(N_DEV, -1)
    shards = [_unflatten(per_dev[k], w, SMALL_SHARDED) for k in range(N_DEV)]
    for k in SMALL:
        W[k] = jnp.concatenate([s[k] for s in shards], axis=-1) if k in SMALL_SHARDED else w[k]

    loss, dx, big, recv, small = _local_step(x[0], p[:, 0], loss_target[0], W, blocks)

    last = [(k, m) for k in big if (k, None) not in recv for m in range(big[k].shape[0]) if (k, m) not in recv]
    got, (parts,) = _scatter_and_gather([(big[k], m) for k, m in last], [_flatten(small, SMALL).astype(BF)],
                                        "scatter_last_gather_small")
    recv.update(zip(last, got))

    def total(k):
        tag = "sum_" + "_".join(map(str, k))
        if (k, None) in recv:
            return _scatter_sum(recv[(k, None)], tag)
        return jnp.concatenate([_scatter_sum(recv[(k, m)], f"{tag}_{m}") for m in range(big[k].shape[0])])

    grads = _unpack_big({k: total(k) for k in big}, L)
    total_small = _sum_devices(parts.reshape(N_DEV, -1, LANES), "sum_small_grads")
    full = _unflatten(total_small, {k: W[k] for k in SMALL}, SMALL)
    for k in SMALL:
        if k in SMALL_SHARDED:
            n = w[k].shape[-1]
            grads[k] = lax.dynamic_slice_in_dim(full[k], me * n, n, axis=-1)
        else:
            grads[k] = full[k]

    delta, new_m, new_v = {}, {}, {}
    for k in names:
        delta[k], new_m[k], new_v[k] = _adamw(w[k], grads[k], m[k], v[k], f"adamw_{k}")
    total_loss = lax.psum(loss[0, 0], ("x", "y", "c"))
    return (total_loss, dx[None], *[grads[k] for k in names], *[delta[k] for k in names],
            *[new_m[k] for k in names], *[new_v[k] for k in names])
```

```python
import functools

import jax
import jax.numpy as jnp
from jax import lax
from jax.experimental import pallas as pl
from jax.experimental.pallas import tpu as pltpu

F32 = jnp.float32
BF = jnp.bfloat16
MESH = pl.DeviceIdType.MESH

RMS_EPS = 1e-6
LRU_C = 8.0
LRU_HEADS = 16
CONV_WIDTH = 4
POOL_WINDOWS = (2, 4, 8, 16)
ADAM_LR, ADAM_B1, ADAM_B2, ADAM_EPS, ADAM_WD, ADAM_STEP = 0.001, 0.9, 0.999, 1e-08, 0.01, 10

N_DEV = 8
LANES = 128
SUBLANES = 8
GATE_COLS, GATE_SPAN = 256, 512
HALO = 16
VMEM_LIMIT = 56 * 1024 * 1024

TM_FFN = 1024
TM_FFN_ACT = 2048
TM_FFN_IN = 512
TF_FFN = 256
TF_FFN_WG = 1408
TK_FFN_WG = 512
TB_SEQ = 256
TM_EW = 512
TM_MM, TN_MM, TK_MM = 1024, 512, 1024
TR_ADAM = 512


def _tile(n, pref, align):
    if n <= pref:
        return n
    t = (pref // align) * align
    while t >= align:
        if n % t == 0:
            return t
        t -= align
    raise ValueError(f"no tile for {n} (pref {pref}, align {align})")


def _params(*sem):
    return pltpu.CompilerParams(dimension_semantics=sem, vmem_limit_bytes=VMEM_LIMIT)


def _dot(a, b):
    return lax.dot_general(a, b, (((1,), (0,)), ((), ())), preferred_element_type=F32)


def _dot_nt(a, b):
    return lax.dot_general(a, b, (((1,), (1,)), ((), ())), preferred_element_type=F32)


def _dot_tn(a, b):
    return lax.dot_general(a, b, (((0,), (0,)), ((), ())), preferred_element_type=F32)


def _sigmoid(x):
    return 1.0 / (1.0 + jnp.exp(-x))


def _gelu_parts(x):
    k0, k1 = 0.7978845608028654, 0.044715
    t = jnp.tanh(k0 * (x + k1 * x * x * x))
    g = 0.5 * x * (1.0 + t)
    dg = 0.5 * (1.0 + t) + 0.5 * x * (1.0 - t * t) * k0 * (1.0 + 3.0 * k1 * x * x)
    return g, dg


def _one_minus_sq(la, a):
    return jnp.tanh(-la) * (1.0 + a * a)


def _softplus_neg(l):
    u = jnp.exp(-jnp.abs(l))
    w = 1.0 + u
    log1p = jnp.where(w == 1.0, u, jnp.log(w) * (u / jnp.where(w == 1.0, 1.0, w - 1.0)))
    return jnp.maximum(-l, 0.0) + log1p


def _rms_parts(x, g):
    r = lax.rsqrt(jnp.mean(x * x, axis=-1, keepdims=True) + RMS_EPS)
    nhat = x * r
    return nhat * g, nhat, r


def _rms_bwd_parts(x, g, dn):
    _, nhat, r = _rms_parts(x, g)
    u = dn * g
    dx = r * (u - nhat * jnp.mean(u * nhat, axis=-1, keepdims=True))
    return dx, jnp.sum(dn * nhat, axis=0, keepdims=True)


def _row_spec(tm, d, single=False):
    if single:
        return pl.BlockSpec((tm, d), lambda i, *_: (i, 0), pipeline_mode=pl.Buffered(1))
    return pl.BlockSpec((tm, d), lambda i, *_: (i, 0))


def _vec_spec(d, rows=1):
    return pl.BlockSpec((rows, d), lambda *_: (0, 0))


def _mm(x, w, mode, name, out_dtype=F32, res=None, alpha=1.0, tm=None, tn=None, tk=None):
    if mode == "nn":
        (M, K), (_, N) = x.shape, w.shape
    elif mode == "nt":
        (M, K), (N, _) = x.shape, w.shape
    else:
        (K, M), (_, N) = x.shape, w.shape
    tm = _tile(M, tm or TM_MM, LANES if mode == "tn" else SUBLANES)
    tn = _tile(N, tn or TN_MM, LANES)
    tk = _tile(K, tk or TK_MM, LANES if mode != "tn" else 16)
    nk = K // tk
    dot = {"nn": _dot, "nt": _dot_nt, "tn": _dot_tn}[mode]

    def body(*refs):
        if res is None:
            x_ref, w_ref, o_ref, acc = refs
        else:
            x_ref, w_ref, r_ref, o_ref, acc = refs
        k = pl.program_id(2)

        @pl.when(k == 0)
        def _():
            acc[...] = jnp.zeros_like(acc)

        acc[...] += dot(x_ref[...].astype(BF), w_ref[...].astype(BF))

        @pl.when(k == nk - 1)
        def _():
            r = acc[...] if alpha == 1.0 else acc[...] * alpha
            if res is not None:
                r = r_ref[...] + r
            o_ref[...] = r.astype(out_dtype)

    if mode == "nn":
        specs = [pl.BlockSpec((tm, tk), lambda i, j, k: (i, k)), pl.BlockSpec((tk, tn), lambda i, j, k: (k, j))]
    elif mode == "nt":
        specs = [pl.BlockSpec((tm, tk), lambda i, j, k: (i, k)), pl.BlockSpec((tn, tk), lambda i, j, k: (j, k))]
    else:
        specs = [pl.BlockSpec((tk, tm), lambda i, j, k: (k, i)), pl.BlockSpec((tk, tn), lambda i, j, k: (k, j))]
    args = [x, w]
    if res is not None:
        specs.append(pl.BlockSpec((tm, tn), lambda i, j, k: (i, j)))
        args.append(res)
    return pl.pallas_call(
        body, name=name, grid=(M // tm, N // tn, nk), in_specs=specs,
        out_specs=pl.BlockSpec((tm, tn), lambda i, j, k: (i, j)),
        out_shape=jax.ShapeDtypeStruct((M, N), out_dtype),
        scratch_shapes=[pltpu.VMEM((tm, tn), F32)],
        compiler_params=_params("parallel", "parallel", "arbitrary"),
    )(*args)


def _rms_fwd(h, g, out_dtype, name):
    T, D = h.shape
    tm = _tile(T, TM_EW, 16)

    def body(h_ref, g_ref, o_ref):
        o_ref[...] = _rms_parts(h_ref[...], g_ref[...])[0].astype(out_dtype)

    return pl.pallas_call(
        body, name=name, grid=(T // tm,), in_specs=[_row_spec(tm, D), _vec_spec(D)], out_specs=_row_spec(tm, D),
        out_shape=jax.ShapeDtypeStruct((T, D), out_dtype), compiler_params=_params("parallel"),
    )(h, g.reshape(1, D))


def _loss_head(h, g, tgt):
    T, D = h.shape
    tm = _tile(T, TM_EW, 16)

    def body(h_ref, g_ref, t_ref, loss_ref, dh_ref, dg_ref):
        @pl.when(pl.program_id(0) == 0)
        def _():
            dg_ref[...] = jnp.zeros_like(dg_ref)
            loss_ref[...] = jnp.zeros_like(loss_ref)

        x, gg = h_ref[...], g_ref[...]
        y = _rms_parts(x, gg)[0]
        e = y - t_ref[...]
        part = jnp.sum(jnp.sum(e * e, axis=0, keepdims=True), axis=1, keepdims=True) * (0.5 / D)
        loss_ref[...] += jnp.broadcast_to(part, loss_ref.shape)
        dx, dg = _rms_bwd_parts(x, gg, e * (1.0 / D))
        dh_ref[...] = dx
        dg_ref[...] += dg

    return pl.pallas_call(
        body, name="loss_head", grid=(T // tm,),
        in_specs=[_row_spec(tm, D), _vec_spec(D), _row_spec(tm, D)],
        out_specs=[_vec_spec(LANES), _row_spec(tm, D), _vec_spec(D)],
        out_shape=[jax.ShapeDtypeStruct((1, LANES), F32), jax.ShapeDtypeStruct((T, D), F32),
                   jax.ShapeDtypeStruct((1, D), F32)],
        compiler_params=_params("arbitrary"),
    )(h, g.reshape(1, D), tgt)


def _carry(plan, first, mid, last):
    pl.when(first)(plan[0])
    if len(plan) == 3:
        pl.when(mid)(plan[1])
    pl.when(last)(plan[-1])


def _ffn_fwd_act(h, g, wffn, name, gather=()):
    T, D = h.shape
    F = wffn.shape[1]
    tm, tf = _tile(T, TM_FFN_ACT, 16), _tile(F, TF_FFN, LANES)
    ni, nf, ng = T // tm, F // tf, len(gather)

    def body(*refs):
        h_ref, g_ref, wg_ref, wu_ref = refs[:4]
        srcs, (a_ref, b_ref, s_ref, n_ref), outs = refs[4:4 + ng], refs[4 + ng:8 + ng], refs[8 + ng:8 + 2 * ng]
        i, j = pl.program_id(0), pl.program_id(1)
        if ng:
            _carry(_gather_plan(gather, srcs, outs, *refs[8 + 2 * ng:]), jnp.logical_and(i == 0, j == 0),
                   jnp.logical_and(i == (3 * ni) // 4, j == 0), jnp.logical_and(i == ni - 1, j == nf - 1))

        @pl.when(j == 0)
        def _():
            n_ref[...] = _rms_parts(h_ref[...], g_ref[...])[0].astype(BF)

        n = n_ref[...]
        a = _dot_nt(n, wg_ref[...])
        b = _dot_nt(n, wu_ref[...])
        a_ref[...] = a.astype(BF)
        b_ref[...] = b.astype(BF)
        s_ref[...] = (a * _sigmoid(a) * b).astype(BF)

    tile = pl.BlockSpec((tm, tf), lambda i, j: (i, j))
    w = [pl.BlockSpec((None, tf, D), functools.partial(lambda k, i, j: (k, j, 0), k)) for k in (0, 1)]
    hbm = pl.BlockSpec(memory_space=pl.ANY)
    outs = pl.pallas_call(
        body, name=name, grid=(ni, nf), in_specs=[_row_spec(tm, D), _vec_spec(D)] + w + [hbm] * ng,
        out_specs=[tile, tile, tile, _row_spec(tm, D)] + [hbm] * ng,
        out_shape=[jax.ShapeDtypeStruct((T, F), BF)] * 3 + [jax.ShapeDtypeStruct((T, D), BF)] + _gathered_shapes(gather),
        scratch_shapes=_gather_sems(ng), compiler_params=_params("arbitrary", "arbitrary"),
    )(h, g.reshape(1, D), wffn, wffn, *gather)
    return outs[0], outs[1], outs[2], outs[3], list(outs[4:])


def _ffn_fwd_out(s, wffn, h, name, gather=()):
    T, F = s.shape
    D = h.shape[1]
    tm = _tile(T, TM_FFN_IN, 16)
    ni, ng = T // tm, len(gather)

    def body(*refs):
        s_ref, w_ref, h_ref = refs[:3]
        srcs, o_ref, outs = refs[3:3 + ng], refs[3 + ng], refs[4 + ng:4 + 2 * ng]
        i = pl.program_id(0)
        if ng:
            _carry(_gather_plan(gather, srcs, outs, *refs[4 + 2 * ng:]), i == 0, i == (3 * ni) // 4, i == ni - 1)
        o_ref[...] = h_ref[...] + 0.5 * _dot(s_ref[...], w_ref[...])

    hbm = pl.BlockSpec(memory_space=pl.ANY)
    outs = pl.pallas_call(
        body, name=name, grid=(ni,),
        in_specs=[_row_spec(tm, F), pl.BlockSpec((None, F, D), lambda i: (2, 0, 0), pipeline_mode=pl.Buffered(1)),
                  _row_spec(tm, D)] + [hbm] * ng,
        out_specs=[_row_spec(tm, D)] + [hbm] * ng,
        out_shape=[jax.ShapeDtypeStruct((T, D), F32)] + _gathered_shapes(gather),
        scratch_shapes=_gather_sems(ng), compiler_params=_params("arbitrary"),
    )(s, wffn, h, *gather)
    return outs[0], list(outs[1:])


def _ffn_bwd_act(dh, a, b, wffn, name, scatter=()):
    T, D = dh.shape
    F = wffn.shape[1]
    tm, tf = _tile(T, TM_FFN_ACT, 16), _tile(F, TF_FFN, LANES)
    ni, nf, ng = T // tm, F // tf, len(scatter)

    def body(*refs):
        dh_ref, a_ref, b_ref, wd_ref = refs[:4]
        srcs, (da_ref, db_ref, dhb_ref), outs = refs[4:4 + ng], refs[4 + ng:7 + ng], refs[7 + ng:7 + 2 * ng]
        i, j = pl.program_id(0), pl.program_id(1)
        if ng:
            _carry(_scatter_plan(scatter, srcs, outs, *refs[7 + 2 * ng:]), jnp.logical_and(i == 0, j == 0), None,
                   jnp.logical_and(i == ni - 1, j == nf - 1))

        @pl.when(j == 0)
        def _():
            dhb_ref[...] = dh_ref[...].astype(BF)

        ds = 0.5 * _dot_nt(dhb_ref[...], wd_ref[...])
        av, bv = a_ref[...].astype(F32), b_ref[...].astype(F32)
        sig = _sigmoid(av)
        da_ref[...] = (ds * bv * (sig * (1.0 + av * (1.0 - sig)))).astype(BF)
        db_ref[...] = (ds * (av * sig)).astype(BF)

    tile = pl.BlockSpec((tm, tf), lambda i, j: (i, j))
    hbm = pl.BlockSpec(memory_space=pl.ANY)
    outs = pl.pallas_call(
        body, name=name, grid=(ni, nf),
        in_specs=[_row_spec(tm, D), tile, tile, pl.BlockSpec((None, tf, D), lambda i, j: (2, j, 0))] + [hbm] * ng,
        out_specs=[tile, tile, _row_spec(tm, D)] + [hbm] * ng,
        out_shape=[jax.ShapeDtypeStruct((T, F), BF)] * 2 + [jax.ShapeDtypeStruct((T, D), BF)]
        + _scattered_shapes(scatter),
        scratch_shapes=_scatter_sems(ng), compiler_params=_params("arbitrary", "arbitrary"),
    )(dh, a, b, wffn, *[piece[0] for piece in scatter])
    return outs[0], outs[1], outs[2], list(outs[3:])


def _two_dot_norm_bwd(x1, x2, w, w_specs, h, g, dh, name, scatter=()):
    T, K = x1.shape
    D = h.shape[1]
    tm = _tile(T, TM_FFN_IN, 16)
    ni, ng = T // tm, len(scatter)

    def body(*refs):
        x1_ref, x2_ref, w1_ref, w2_ref, h_ref, g_ref, dh_ref = refs[:7]
        srcs, (o_ref, dg_ref), outs = refs[7:7 + ng], refs[7 + ng:9 + ng], refs[9 + ng:9 + 2 * ng]
        i = pl.program_id(0)
        if ng:
            _carry(_scatter_plan(scatter, srcs, outs, *refs[9 + 2 * ng:]), i == 0, None, i == ni - 1)

        @pl.when(i == 0)
        def _():
            dg_ref[...] = jnp.zeros_like(dg_ref)

        dn = _dot(x1_ref[...], w1_ref[...]) + _dot(x2_ref[...], w2_ref[...])
        dx, dg = _rms_bwd_parts(h_ref[...], g_ref[...], dn)
        o_ref[...] = dh_ref[...] + dx
        dg_ref[...] += dg

    hbm = pl.BlockSpec(memory_space=pl.ANY)
    act = pl.BlockSpec((tm, K), lambda i: (i, 0))
    outs = pl.pallas_call(
        body, name=name, grid=(ni,),
        in_specs=[act, act] + w_specs + [_row_spec(tm, D), _vec_spec(D), _row_spec(tm, D)] + [hbm] * ng,
        out_specs=[_row_spec(tm, D), _vec_spec(D)] + [hbm] * ng,
        out_shape=[jax.ShapeDtypeStruct((T, D), F32), jax.ShapeDtypeStruct((1, D), F32)] + _scattered_shapes(scatter),
        scratch_shapes=_scatter_sems(ng), compiler_params=_params("arbitrary"),
    )(x1, x2, w, w, h, g.reshape(1, D), dh, *[piece[0] for piece in scatter])
    return outs[0], outs[1], list(outs[2:])


def _ffn_bwd_in(da, db, wffn, h, g, dh, name, scatter=()):
    F, D = wffn.shape[1:]
    specs = [pl.BlockSpec((None, F, D), functools.partial(lambda k, i: (k, 0, 0), k), pipeline_mode=pl.Buffered(1))
             for k in (0, 1)]
    return _two_dot_norm_bwd(da, db, wffn, specs, h, g, dh, name, scatter)


def _lru_in_bwd(dgb, dxb, win, h, g, dh, name):
    R, D = win.shape[0] // 2, win.shape[1]
    specs = [pl.BlockSpec((R, D), functools.partial(lambda k, i: (k, 0), k), pipeline_mode=pl.Buffered(1)) for k in (0, 1)]
    return _two_dot_norm_bwd(dgb, dxb, win, specs, h, g, dh, name)[:2]


def _ffn_bwd_w(da, db, s, n, dhb, name, scatter=()):
    T, F = da.shape
    D = n.shape[1]
    tf, tk = _tile(F, TF_FFN_WG, LANES), _tile(T, TK_FFN_WG, 16)
    nj, nk, ng = F // tf, T // tk, len(scatter)

    def body(*refs):
        da_ref, db_ref, s_ref, n_ref, dh_ref = refs[:5]
        srcs, o_ref, outs = refs[5:5 + ng], refs[5 + ng], refs[6 + ng:6 + 2 * ng]
        g_sc, u_sc, d_sc = refs[6 + 2 * ng:9 + 2 * ng]
        j, k = pl.program_id(0), pl.program_id(1)
        if ng:
            _carry(_scatter_plan(scatter, srcs, outs, *refs[9 + 2 * ng:]), jnp.logical_and(j == 0, k == 0), None,
                   jnp.logical_and(j == nj - 1, k == nk - 1))

        @pl.when(k == 0)
        def _():
            g_sc[...] = jnp.zeros_like(g_sc)
            u_sc[...] = jnp.zeros_like(u_sc)
            d_sc[...] = jnp.zeros_like(d_sc)

        nv = n_ref[...]
        g_sc[...] += _dot_tn(da_ref[...], nv)
        u_sc[...] += _dot_tn(db_ref[...], nv)
        d_sc[...] += _dot_tn(s_ref[...], dh_ref[...])

        @pl.when(k == nk - 1)
        def _():
            o_ref[0] = g_sc[...].astype(BF)
            o_ref[1] = u_sc[...].astype(BF)
            o_ref[2] = (0.5 * d_sc[...]).astype(BF)

    act = pl.BlockSpec((tk, tf), lambda j, k: (k, j))
    tok = pl.BlockSpec((tk, D), lambda j, k: (k, 0))
    hbm = pl.BlockSpec(memory_space=pl.ANY)
    outs = pl.pallas_call(
        body, name=name, grid=(nj, nk), in_specs=[act, act, act, tok, tok] + [hbm] * ng,
        out_specs=[pl.BlockSpec((3, tf, D), lambda j, k: (0, j, 0), pipeline_mode=pl.Buffered(1))] + [hbm] * ng,
        out_shape=[jax.ShapeDtypeStruct((3, F, D), BF)] + _scattered_shapes(scatter),
        scratch_shapes=[pltpu.VMEM((tf, D), F32)] * 3 + _scatter_sems(ng),
        compiler_params=_params("arbitrary", "arbitrary"),
    )(da, db, s, n, dhb, *[piece[0] for piece in scatter])
    return outs[0], list(outs[1:])


def _ple_fwd(h, g, wg, wp, p, name):
    T, D = h.shape
    P = p.shape[1]
    tm = _tile(T, TM_EW, 16)

    def body(h_ref, g_ref, wg_ref, wp_ref, p_ref, o_ref, n_ref, gate_ref, pp_ref):
        x = h_ref[...]
        n = _rms_parts(x, g_ref[...])[0].astype(BF)
        gate = _sigmoid(_dot(n, wg_ref[...]))
        pp = _dot_nt(p_ref[...].astype(BF), wp_ref[...])
        o_ref[...] = x + gate * pp
        n_ref[...] = n
        gate_ref[...] = gate.astype(BF)
        pp_ref[...] = pp.astype(BF)

    row = _row_spec(tm, D)
    return pl.pallas_call(
        body, name=name, grid=(T // tm,),
        in_specs=[row, _vec_spec(D), _vec_spec(D, D), _vec_spec(P, D), _row_spec(tm, P)], out_specs=[row] * 4,
        out_shape=[jax.ShapeDtypeStruct((T, D), F32)] + [jax.ShapeDtypeStruct((T, D), BF)] * 3,
        compiler_params=_params("parallel"),
    )(h, g.reshape(1, D), wg, wp, p)


def _ple_bwd(dh, gate, pp, h, g, wg, name):
    T, D = dh.shape
    tm = _tile(T, TM_EW, 16)

    def body(dh_ref, gate_ref, pp_ref, h_ref, g_ref, wg_ref, o_ref, dz_ref, dp_ref, dg_ref):
        @pl.when(pl.program_id(0) == 0)
        def _():
            dg_ref[...] = jnp.zeros_like(dg_ref)

        d, gate = dh_ref[...], gate_ref[...].astype(F32)
        dz = (d * pp_ref[...].astype(F32) * gate * (1.0 - gate)).astype(BF)
        dx, dg = _rms_bwd_parts(h_ref[...], g_ref[...], _dot_nt(dz, wg_ref[...]))
        o_ref[...] = d + dx
        dz_ref[...] = dz
        dp_ref[...] = (d * gate).astype(BF)
        dg_ref[...] += dg

    row = _row_spec(tm, D)
    return pl.pallas_call(
        body, name=name, grid=(T // tm,), in_specs=[row, row, row, row, _vec_spec(D), _vec_spec(D, D)],
        out_specs=[row, row, row, _vec_spec(D)],
        out_shape=[jax.ShapeDtypeStruct((T, D), F32), jax.ShapeDtypeStruct((T, D), BF), jax.ShapeDtypeStruct((T, D), BF),
                   jax.ShapeDtypeStruct((1, D), F32)],
        compiler_params=_params("arbitrary"),
    )(dh, gate, pp, h, g.reshape(1, D), wg)


def _lru_fwd(z, conv_w, conv_b, wa, wx, b_a, b_x, a_param, wout, h, name):
    T, R2 = z.shape
    R, D = R2 // 2, h.shape[1]
    tb = _tile(T, TB_SEQ, HALO)
    per, ng = tb // HALO, tb // SUBLANES

    def body(g_ref, x_ref, halo_ref, cw_ref, cb_ref, wa_ref, wx_ref, ba_ref, bx_ref, ap_ref, wo_ref, h_ref,
             o_ref, xc_ref, r_ref, ig_ref, a_ref, hs_ref, y_ref, ext, carry, a_sc, b_sc):
        i = pl.program_id(0)

        @pl.when(i == 0)
        def _():
            carry[...] = jnp.zeros_like(carry)

        ext[pl.ds(0, HALO), :] = jnp.where(i > 0, halo_ref[...], 0.0)
        ext[pl.ds(HALO, tb), :] = x_ref[...]
        xc = cb_ref[...] + cw_ref[0:1, :] * ext[pl.ds(HALO - 3, tb), :]
        for k in range(1, CONV_WIDTH):
            xc = xc + cw_ref[k:k + 1, :] * ext[pl.ds(HALO - 3 + k, tb), :]
        xcb = xc.astype(BF)
        r = _sigmoid(_dot(xcb, wa_ref[...]) + ba_ref[...])
        ig = _sigmoid(_dot(xcb, wx_ref[...]) + bx_ref[...])
        la = -LRU_C * r * _softplus_neg(ap_ref[...])
        av = jnp.exp(la)
        xc_ref[...] = xc
        r_ref[...] = r
        ig_ref[...] = ig
        a_ref[...] = av
        A = av.reshape(ng, SUBLANES, R)
        B = (jnp.sqrt(_one_minus_sq(la, av)) * (ig * xc)).reshape(ng, SUBLANES, R)
        sub = lax.broadcasted_iota(jnp.int32, (1, SUBLANES, R), 1)
        for k in (1, 2, 4):
            m = sub >= k
            a_n = jnp.where(m, pltpu.roll(A, k, 1), 1.0)
            b_n = jnp.where(m, pltpu.roll(B, k, 1), 0.0)
            B = A * b_n + B
            A = A * a_n
        a_sc[...] = A.reshape(tb, R)
        b_sc[...] = B.reshape(tb, R)

        def group(q, c):
            rows = pl.ds(pl.multiple_of(q * SUBLANES, SUBLANES), SUBLANES)
            hg = a_sc[rows, :] * c + b_sc[rows, :]
            hs_ref[rows, :] = hg
            return hg[SUBLANES - 1:SUBLANES, :]

        carry[...] = lax.fori_loop(0, ng, group, carry[...])
        y = (hs_ref[...] * _gelu_parts(g_ref[...])[0]).astype(BF)
        y_ref[...] = y
        o_ref[...] = h_ref[...] + _dot(y, wo_ref[...])

    once = lambda rows, cols: pl.BlockSpec((rows, cols), lambda i: (0, 0), pipeline_mode=pl.Buffered(1))
    gate = pl.BlockSpec((tb, R), lambda i: (i, 0))
    tile = pl.BlockSpec((tb, R), lambda i: (i, 1))
    halo = pl.BlockSpec((HALO, R), lambda i: (jnp.maximum(i * per - 1, 0), 1))
    return pl.pallas_call(
        body, name=name, grid=(T // tb,),
        in_specs=[gate, tile, halo, _vec_spec(R, CONV_WIDTH), _vec_spec(R), once(R, R), once(R, R), _vec_spec(R),
                  _vec_spec(R), _vec_spec(R), once(R, D), _row_spec(tb, D)],
        out_specs=[_row_spec(tb, D)] + [gate] * 6,
        out_shape=[jax.ShapeDtypeStruct((T, D), F32)] + [jax.ShapeDtypeStruct((T, R), F32)] * 5
        + [jax.ShapeDtypeStruct((T, R), BF)],
        scratch_shapes=[pltpu.VMEM((HALO + tb, R), F32), pltpu.VMEM((1, R), F32), pltpu.VMEM((tb, R), F32),
                        pltpu.VMEM((tb, R), F32)],
        compiler_params=_params("arbitrary"),
    )(z, z, z, conv_w, conv_b.reshape(1, R), wa, wx, b_a.reshape(1, R), b_x.reshape(1, R), a_param.reshape(1, R), wout, h)


def _lru_bwd(dh, hs, z, a, r, ig, xc, wa, wx, wout, conv_w, a_param, name):
    T, R = hs.shape
    D = dh.shape[1]
    tb = _tile(T, TB_SEQ, HALO)
    per, nt, ng = tb // HALO, T // tb, tb // SUBLANES

    def body(dh_ref, h_ref, hp_ref, g_ref, x_ref, xp_ref, a_ref, r_ref, ig_ref, xc_ref, wa_ref, wx_ref, wo_ref, cw_ref,
             ap_ref, dgb_ref, dxb_ref, dpa_ref, dpx_ref, dsp_ref, dba_ref, dbx_ref, dcb_ref, dcw_ref,
             hext, xext, dext, carry, later, a_sc, b_sc, d_sc, l_sc):
        i = pl.program_id(0)

        @pl.when(i == 0)
        def _():
            for ref in (dsp_ref, dba_ref, dbx_ref, dcb_ref, dcw_ref, carry, later):
                ref[...] = jnp.zeros_like(ref)

        dy = _dot_nt(dh_ref[...].astype(BF), wo_ref[...])
        gl, dgl = _gelu_parts(g_ref[...])
        hv, av = h_ref[...], a_ref[...]
        dhd = dy * gl
        dgb_ref[...] = (dy * hv * dgl).astype(BF)
        d_sc[...] = dhd
        A = av.reshape(ng, SUBLANES, R)
        B = A * dhd.reshape(ng, SUBLANES, R)
        sub = lax.broadcasted_iota(jnp.int32, (1, SUBLANES, R), 1)
        for k in (1, 2, 4):
            m = sub < SUBLANES - k
            a_n = jnp.where(m, pltpu.roll(A, SUBLANES - k, 1), 1.0)
            b_n = jnp.where(m, pltpu.roll(B, SUBLANES - k, 1), 0.0)
            B = A * b_n + B
            A = A * a_n
        a_sc[...] = A.reshape(tb, R)
        b_sc[...] = B.reshape(tb, R)
        sub8 = lax.broadcasted_iota(jnp.int32, (SUBLANES, R), 0)

        def group(q, c):
            rows = pl.ds(pl.multiple_of((ng - 1 - q) * SUBLANES, SUBLANES), SUBLANES)
            mu = a_sc[rows, :] * c + b_sc[rows, :]
            l_sc[rows, :] = d_sc[rows, :] + jnp.where(sub8 == SUBLANES - 1, c, pltpu.roll(mu, SUBLANES - 1, 0))
            return mu[0:1, :]

        carry[...] = lax.fori_loop(0, ng, group, carry[...])
        lam = l_sc[...]
        hext[pl.ds(0, HALO), :] = jnp.where(i < nt - 1, hp_ref[...], 0.0)
        hext[pl.ds(HALO, tb), :] = hv
        h_prev = hext[pl.ds(HALO - 1, tb), :]
        rv, igv, xcv = r_ref[...], ig_ref[...], xc_ref[...]
        sp = _softplus_neg(ap_ref[...])
        mult = jnp.sqrt(_one_minus_sq(-LRU_C * rv * sp, av))
        dla = lam * h_prev * av - lam * (igv * xcv) * (av * av) / mult
        du = lam * mult
        dpa = (dla * (-LRU_C) * sp) * rv * (1.0 - rv)
        dpx = (du * xcv) * igv * (1.0 - igv)
        dsp_ref[...] += jnp.sum(dla * (-LRU_C) * rv, axis=0, keepdims=True)
        dba_ref[...] += jnp.sum(dpa, axis=0, keepdims=True)
        dbx_ref[...] += jnp.sum(dpx, axis=0, keepdims=True)
        dpab, dpxb = dpa.astype(BF), dpx.astype(BF)
        dpa_ref[...] = dpab
        dpx_ref[...] = dpxb
        dxc = du * igv + _dot_nt(dpab, wa_ref[...]) + _dot_nt(dpxb, wx_ref[...])
        dext[pl.ds(0, tb), :] = dxc
        dext[pl.ds(tb, SUBLANES), :] = later[...]
        later[...] = dxc[0:SUBLANES, :]
        xext[pl.ds(0, HALO), :] = jnp.where(i < nt - 1, xp_ref[...], 0.0)
        xext[pl.ds(HALO, tb), :] = x_ref[...]
        dxb = cw_ref[CONV_WIDTH - 1:CONV_WIDTH, :] * dxc
        for k in range(CONV_WIDTH - 1):
            dxb = dxb + cw_ref[k:k + 1, :] * dext[pl.ds(CONV_WIDTH - 1 - k, tb), :]
        dxb_ref[...] = dxb.astype(BF)
        for k in range(CONV_WIDTH):
            dcw_ref[k:k + 1, :] += jnp.sum(dxc * xext[pl.ds(HALO - 3 + k, tb), :], axis=0, keepdims=True)
        dcb_ref[...] += jnp.sum(dxc, axis=0, keepdims=True)

        @pl.when(i == nt - 1)
        def _():
            dsp_ref[...] = dsp_ref[...] * (-_sigmoid(-ap_ref[...]))

    once = lambda rows, cols: pl.BlockSpec((rows, cols), lambda i: (0, 0), pipeline_mode=pl.Buffered(1))
    tile = lambda col: pl.BlockSpec((tb, R), lambda i: (nt - 1 - i, col))
    prev = lambda col: pl.BlockSpec((HALO, R), lambda i: (jnp.maximum((nt - 1 - i) * per - 1, 0), col))
    t0 = tile(0)
    return pl.pallas_call(
        body, name=name, grid=(nt,),
        in_specs=[pl.BlockSpec((tb, D), lambda i: (nt - 1 - i, 0)), t0, prev(0), t0, tile(1), prev(1), t0, t0, t0, t0,
                  once(R, R), once(R, R), once(R, D), _vec_spec(R, CONV_WIDTH), _vec_spec(R)],
        out_specs=[t0] * 4 + [_vec_spec(R)] * 4 + [_vec_spec(R, SUBLANES)],
        out_shape=[jax.ShapeDtypeStruct((T, R), BF)] * 4 + [jax.ShapeDtypeStruct((1, R), F32)] * 4
        + [jax.ShapeDtypeStruct((SUBLANES, R), F32)],
        scratch_shapes=[pltpu.VMEM((HALO + tb, R), F32), pltpu.VMEM((HALO + tb, R), F32),
                        pltpu.VMEM((tb + SUBLANES, R), F32), pltpu.VMEM((1, R), F32), pltpu.VMEM((SUBLANES, R), F32)]
        + [pltpu.VMEM((tb, R), F32)] * 4,
        compiler_params=_params("arbitrary"),
    )(dh, hs, hs, z, z, z, a, r, ig, xc, wa, wx, wout, conv_w, a_param.reshape(1, R))


def _gate_spans(R):
    d = R // LRU_HEADS
    spans = [min((j * GATE_COLS // d) * d // LANES * LANES, R - GATE_SPAN) for j in range(R // GATE_COLS)]
    assert R % GATE_COLS == 0 and all(lo + GATE_SPAN >= (((j + 1) * GATE_COLS - 1) // d + 1) * d for j, lo in enumerate(spans))
    return spans


def _lru_gates_dw(xc, dpa, dpx, name):
    T, R = xc.shape
    tk = _tile(T, 1024, 16)
    spans = _gate_spans(R)
    nb = len(spans)

    def body(x_ref, a_ref, b_ref, o_ref):
        @pl.when(pl.program_id(0) == 0)
        def _():
            o_ref[...] = jnp.zeros_like(o_ref)

        for j, lo in enumerate(spans):
            xs = x_ref[:, pl.ds(lo, GATE_SPAN)].astype(BF)
            cols = pl.ds(j * GATE_COLS, GATE_COLS)
            o_ref[0, j] += _dot_tn(xs, a_ref[:, cols])
            o_ref[1, j] += _dot_tn(xs, b_ref[:, cols])

    row = _row_spec(tk, R)
    out = pl.pallas_call(
        body, name=name, grid=(T // tk,), in_specs=[row, row, row],
        out_specs=pl.BlockSpec((2, nb, GATE_SPAN, GATE_COLS), lambda i: (0, 0, 0, 0)),
        out_shape=jax.ShapeDtypeStruct((2, nb, GATE_SPAN, GATE_COLS), F32), compiler_params=_params("arbitrary"),
    )(xc, dpa, dpx)
    dense = jnp.zeros((2, R, R), F32)
    for j, lo in enumerate(spans):
        dense = dense.at[:, lo:lo + GATE_SPAN, j * GATE_COLS:(j + 1) * GATE_COLS].set(out[:, j])
    return dense[0], dense[1]


def _window_sums(e, n, back):
    out, s = [], e
    for k in (1, 2, 4, 8):
        s = s + pltpu.roll(s, k if back else n - k, 0)
        out.append(s)
    return out


def _pool_fwd(h, g, w, b, scale, name):
    T, D = h.shape
    G = len(POOL_WINDOWS)
    gd = D // G
    tb = _tile(T, TB_SEQ, HALO)
    per = tb // HALO

    def body(h_ref, hp_ref, g_ref, w_ref, b_ref, s_ref, o_ref, u_ref, yb_ref):
        i = pl.program_id(0)
        t = i * tb + lax.broadcasted_iota(jnp.int32, (tb, gd), 0) + 1
        hv = h_ref[...]
        xn = _rms_parts(hv, g_ref[...])[0]
        xp = jnp.where(i > 0, _rms_parts(hp_ref[...], g_ref[...])[0], 0.0)
        for k, win in enumerate(POOL_WINDOWS):
            cols = slice(k * gd, (k + 1) * gd)
            x = xn[:, cols]
            e = jnp.concatenate([xp[:, cols], x], axis=0)
            sw = _window_sums(e, HALO + tb, True)[k][HALO:, :]
            u = (sw / jnp.minimum(t, win).astype(F32) - x).astype(BF)
            yb = _dot(u, w_ref[k]) + b_ref[:, cols]
            u_ref[:, cols] = u
            yb_ref[:, cols] = yb
            o_ref[:, cols] = hv[:, cols] + yb * s_ref[:, cols]

    tile = _row_spec(tb, D)
    prev = pl.BlockSpec((HALO, D), lambda i: (jnp.maximum(i * per - 1, 0), 0))
    return pl.pallas_call(
        body, name=name, grid=(T // tb,),
        in_specs=[tile, prev, _vec_spec(D), pl.BlockSpec((G, gd, gd), lambda i: (0, 0, 0)), _vec_spec(D), _vec_spec(D)],
        out_specs=[tile, tile, tile],
        out_shape=[jax.ShapeDtypeStruct((T, D), F32), jax.ShapeDtypeStruct((T, D), BF), jax.ShapeDtypeStruct((T, D), F32)],
        compiler_params=_params("parallel"),
    )(h, h, g.reshape(1, D), w, b.reshape(1, D), scale.reshape(1, D))


def _pool_bwd(dm, u, yb, w, scale, name):
    T, D = dm.shape
    G = len(POOL_WINDOWS)
    gd = D // G
    tb = _tile(T, TB_SEQ, HALO)

    def body(d_ref, u_ref, yb_ref, w_ref, s_ref, du_ref, v_ref, dw_ref, db_ref, ds_ref):
        i = pl.program_id(0)

        @pl.when(i == 0)
        def _():
            dw_ref[...] = jnp.zeros_like(dw_ref)
            db_ref[...] = jnp.zeros_like(db_ref)
            ds_ref[...] = jnp.zeros_like(ds_ref)

        d, sc = d_ref[...], s_ref[...]
        ds_ref[...] += jnp.sum(d * yb_ref[...], axis=0, keepdims=True)
        db_ref[...] += jnp.sum(d * sc, axis=0, keepdims=True)
        t = i * tb + lax.broadcasted_iota(jnp.int32, (tb, gd), 0) + 1
        for g, win in enumerate(POOL_WINDOWS):
            cols = pl.ds(g * gd, gd)
            dy = (d_ref[:, cols] * s_ref[:, cols]).astype(BF)
            du = _dot_nt(dy, w_ref[g])
            dw_ref[g] += _dot_tn(u_ref[:, cols], dy)
            du_ref[:, cols] = du
            v_ref[:, cols] = du / jnp.minimum(t, win).astype(F32)

    tile = _row_spec(tb, D)
    return pl.pallas_call(
        body, name=name, grid=(T // tb,),
        in_specs=[tile, tile, tile, pl.BlockSpec((G, gd, gd), lambda i: (0, 0, 0)), _vec_spec(D)],
        out_specs=[tile, tile, pl.BlockSpec((G, gd, gd), lambda i: (0, 0, 0)), _vec_spec(D), _vec_spec(D)],
        out_shape=[jax.ShapeDtypeStruct((T, D), F32), jax.ShapeDtypeStruct((T, D), F32),
                   jax.ShapeDtypeStruct((G, gd, gd), F32), jax.ShapeDtypeStruct((1, D), F32),
                   jax.ShapeDtypeStruct((1, D), F32)],
        compiler_params=_params("arbitrary"),
    )(dm, u, yb, w, scale.reshape(1, D))


def _pool_bwd_win(v, du, h, g, dh, name):
    T, D = v.shape
    G = len(POOL_WINDOWS)
    gd = D // G
    tb = _tile(T, TB_SEQ, HALO)
    per = tb // HALO
    nt = T // tb

    def body(v_ref, vn_ref, du_ref, h_ref, g_ref, dh_ref, o_ref, dg_ref):
        i = pl.program_id(0)

        @pl.when(i == 0)
        def _():
            dg_ref[...] = jnp.zeros_like(dg_ref)

        parts = []
        for k in range(G):
            cols = pl.ds(k * gd, gd)
            e = jnp.concatenate([v_ref[:, cols], jnp.where(i < nt - 1, vn_ref[:, cols], 0.0)], axis=0)
            parts.append(_window_sums(e, tb + HALO, False)[k][:tb, :] - du_ref[:, cols])
        dx, dg = _rms_bwd_parts(h_ref[...], g_ref[...], jnp.concatenate(parts, axis=1))
        o_ref[...] = dh_ref[...] + dx
        dg_ref[...] += dg

    tile = _row_spec(tb, D)
    nxt = pl.BlockSpec((HALO, D), lambda i: (jnp.minimum((i + 1) * per, T // HALO - 1), 0))
    return pl.pallas_call(
        body, name=name, grid=(nt,), in_specs=[tile, nxt, tile, tile, _vec_spec(D), tile], out_specs=[tile, _vec_spec(D)],
        out_shape=[jax.ShapeDtypeStruct((T, D), F32), jax.ShapeDtypeStruct((1, D), F32)],
        compiler_params=_params("arbitrary"),
    )(v, v, du, h, g.reshape(1, D), dh)


def _adamw(w, g, m, v, name):
    shape = w.shape
    cols = shape[-1] if w.ndim > 1 else shape[0]
    rows = w.size // cols
    tr = _tile(rows, TR_ADAM, SUBLANES)
    c1, c2 = 1.0 / (1.0 - ADAM_B1 ** ADAM_STEP), 1.0 / (1.0 - ADAM_B2 ** ADAM_STEP)

    def body(w_ref, g_ref, m_ref, v_ref, d_ref, mo_ref, vo_ref):
        gv = g_ref[...]
        mn = ADAM_B1 * m_ref[...] + (1.0 - ADAM_B1) * gv
        vn = ADAM_B2 * v_ref[...] + (1.0 - ADAM_B2) * (gv * gv)
        d_ref[...] = -ADAM_LR * ((mn * c1) / (jnp.sqrt(vn * c2) + ADAM_EPS) + ADAM_WD * w_ref[...])
        mo_ref[...] = mn
        vo_ref[...] = vn

    spec = _row_spec(tr, cols)
    outs = pl.pallas_call(
        body, name=name, grid=(rows // tr,), in_specs=[spec] * 4, out_specs=[spec] * 3,
        out_shape=[jax.ShapeDtypeStruct((rows, cols), F32)] * 3, compiler_params=_params("parallel"),
    )(*[t.reshape(rows, cols) for t in (w, g, m, v)])
    return [o.reshape(shape) for o in outs]


def _sum_devices(parts, name):
    n, rows, cols = parts.shape
    tr = _tile(rows, 1024, SUBLANES)

    def body(p_ref, o_ref):
        acc = p_ref[0].astype(F32)
        for k in range(1, n):
            acc = acc + p_ref[k].astype(F32)
        o_ref[...] = acc

    return pl.pallas_call(
        body, name=name, grid=(rows // tr,), in_specs=[pl.BlockSpec((n, tr, cols), lambda i: (0, i, 0))],
        out_specs=_row_spec(tr, cols), out_shape=jax.ShapeDtypeStruct((rows, cols), F32),
        compiler_params=_params("parallel"),
    )(parts)


def _position():
    return lax.axis_index("x"), lax.axis_index("y"), lax.axis_index("c")


def _gathered_shapes(blocks):
    return [jax.ShapeDtypeStruct((b.shape[0], N_DEV * b.shape[1], b.shape[2]), b.dtype) for b in blocks]


def _gather_sems(ng):
    return [pltpu.SemaphoreType.DMA((ng, 7)), pltpu.SemaphoreType.DMA((ng, 7)), pltpu.SemaphoreType.DMA((ng,))] if ng else []


def _gather_plan(blocks, srcs, outs, send_sems, recv_sems, local_sems):
    ng = len(blocks)
    x, y, c = _position()
    me, sibling = (x, y, c), (x, y, 1 - c)
    chips = [(1 - x, y), (x, 1 - y), (1 - x, 1 - y)]

    def rows(g, px, py, pc):
        r = blocks[g].shape[1]
        return outs[g].at[:, pl.ds((4 * px + 2 * py + pc) * r, r), :]

    def copy(g, k, block, to, src=None):
        return pltpu.make_async_remote_copy(
            src_ref=rows(g, *block) if src is None else src, dst_ref=rows(g, *block),
            send_sem=send_sems.at[g, k], recv_sem=recv_sems.at[g, k], device_id=to, device_id_type=MESH)

    def mine(g):
        return pltpu.make_async_copy(srcs[g], rows(g, *me), local_sems.at[g])

    def first(g):
        return [copy(g, 0, me, sibling, src=srcs[g])] + [copy(g, 1 + j, me, (*chip, c), src=srcs[g])
                                                         for j, chip in enumerate(chips)]

    def passed(g):
        return [copy(g, 4 + j, (*chip, c), sibling) for j, chip in enumerate(chips)]

    def start():
        for g in range(ng):
            mine(g).start()
            for cp in first(g):
                cp.start()

    def forward():
        for j, chip in enumerate(chips):
            for g in range(ng):
                copy(g, 1 + j, (*chip, c), me).wait_recv()
                copy(g, 4 + j, (*chip, c), sibling).start()

    def finish():
        for g in range(ng):
            copy(g, 0, sibling, me).wait_recv()
            for j, chip in enumerate(chips):
                copy(g, 4 + j, (*chip, 1 - c), me).wait_recv()
            for cp in first(g) + passed(g):
                cp.wait_send()
            mine(g).wait()

    return start, forward, finish


def _all_gather(blocks, name):
    ng = len(blocks)

    def body(*refs):
        start, forward, finish = _gather_plan(blocks, refs[:ng], refs[ng:2 * ng], *refs[2 * ng:])
        start()
        forward()
        finish()

    hbm = pl.BlockSpec(memory_space=pl.ANY)
    return pl.pallas_call(
        body, name=name, in_specs=[hbm] * ng, out_specs=[hbm] * ng, out_shape=_gathered_shapes(blocks),
        scratch_shapes=_gather_sems(ng),
    )(*blocks)


FLIPS = ((0, 0, 1), (1, 0, 0), (0, 1, 0), (1, 1, 0), (1, 0, 1), (0, 1, 1), (1, 1, 1))


def _piece_rows(piece):
    arr, m = piece
    return arr.shape[0] if m is None else 1


def _scattered_shapes(pieces):
    return [jax.ShapeDtypeStruct((N_DEV, _piece_rows(p), p[0].shape[1] // N_DEV, p[0].shape[2]), p[0].dtype)
            for p in pieces]


def _scatter_sems(ng):
    n = len(FLIPS)
    return [pltpu.SemaphoreType.DMA((ng, n)), pltpu.SemaphoreType.DMA((ng, n)), pltpu.SemaphoreType.DMA((ng,))] if ng else []


def _scatter_plan(pieces, srcs, outs, send_sems, recv_sems, local_sems):
    x, y, c = _position()

    def block(g, tx, ty, tc):
        arr, m = pieces[g]
        r = arr.shape[1] // N_DEV
        lead = slice(None) if m is None else pl.ds(m, 1)
        return srcs[g].at[lead, pl.ds((4 * tx + 2 * ty + tc) * r, r), :]

    def copies(g):
        out = []
        for k, (fx, fy, fc) in enumerate(FLIPS):
            tx, ty, tc = (1 - x if fx else x), (1 - y if fy else y), (1 - c if fc else c)
            out.append(pltpu.make_async_remote_copy(
                src_ref=block(g, tx, ty, tc), dst_ref=outs[g].at[k], send_sem=send_sems.at[g, k],
                recv_sem=recv_sems.at[g, k], device_id=(tx, ty, tc), device_id_type=MESH))
        return out

    def mine(g):
        return pltpu.make_async_copy(block(g, x, y, c), outs[g].at[len(FLIPS)], local_sems.at[g])

    def start():
        for g in range(len(pieces)):
            mine(g).start()
            for cp in copies(g):
                cp.start()

    def finish():
        for g in range(len(pieces)):
            for cp in copies(g):
                cp.wait()
            mine(g).wait()

    return start, finish


def _scatter_and_gather(pieces, blocks, name):
    n_p, n_b = len(pieces), len(blocks)

    def body(*refs):
        ins, outs, sems = refs[:n_p + n_b], refs[n_p + n_b:2 * (n_p + n_b)], refs[2 * (n_p + n_b):]
        s_start, s_finish = _scatter_plan(pieces, ins[:n_p], outs[:n_p], *sems[:3])
        g_start, g_forward, g_finish = _gather_plan(blocks, ins[n_p:], outs[n_p:], *sems[3:])
        s_start()
        g_start()
        g_forward()
        g_finish()
        s_finish()

    hbm = pl.BlockSpec(memory_space=pl.ANY)
    outs = pl.pallas_call(
        body, name=name, in_specs=[hbm] * (n_p + n_b), out_specs=[hbm] * (n_p + n_b),
        out_shape=_scattered_shapes(pieces) + _gathered_shapes(blocks),
        scratch_shapes=_scatter_sems(n_p) + _gather_sems(n_b),
    )(*[p[0] for p in pieces], *blocks)
    return list(outs[:n_p]), list(outs[n_p:])


def _scatter_sum(recv, name):
    _, n, r, c = recv.shape

    def body(r_ref, o_ref):
        acc = r_ref[len(FLIPS)].astype(F32)
        for k in range(len(FLIPS)):
            acc = acc + r_ref[k].astype(F32)
        o_ref[...] = acc

    return pl.pallas_call(
        body, name=name, grid=(n,), in_specs=[pl.BlockSpec((N_DEV, None, r, c), lambda i: (0, i, 0, 0))],
        out_specs=pl.BlockSpec((None, r, c), lambda i: (i, 0, 0)),
        out_shape=jax.ShapeDtypeStruct((n, r, c), F32), compiler_params=_params("parallel"),
    )(recv)


def _block_diag(w):
    H, d, _ = w.shape
    return (jnp.eye(H, dtype=w.dtype)[:, None, :, None] * w[:, :, None, :]).reshape(H * d, H * d)


def _diag_blocks(dense, H):
    d = dense.shape[0] // H
    return jnp.stack([dense[i * d:(i + 1) * d, i * d:(i + 1) * d] for i in range(H)])


def _local_step(x, p, tgt, W, blocks=None):
    dist = blocks is not None
    L = p.shape[0]
    W = dict(W)

    def gathering(keys):
        return [k for k in keys if k not in W] if dist else []

    def ffn_fwd(h, g, i, f, during_act, during_out):
        w = W[("ffn", i, f)]
        keys = gathering(during_act)
        a, b, s, n, got = _ffn_fwd_act(h, g, w, f"ffn{f}_fwd_act_{i}", gather=[blocks[k] for k in keys])
        W.update(zip(keys, got))
        keys = gathering(during_out)
        h, got = _ffn_fwd_out(s, w, h, f"ffn{f}_fwd_out_{i}", gather=[blocks[k] for k in keys])
        W.update(zip(keys, got))
        return (a, b, s, n), h

    saved = []
    h = x
    for i in range(L):
        j = i // 2
        lru = i % 2 == 0
        s = {"h0": h}
        mixer = [("lru_in", j), ("lru_out", j)] if lru else [("pool_w", j)]
        s["ffn1"], h = ffn_fwd(h, W["ffn1_norm"][i], i, 1, [("ffn", i, 2)], mixer)
        s["h1"] = h
        if lru:
            hn = _rms_fwd(h, W["mix_norm"][i], BF, f"mix_norm_{i}")
            z = _mm(hn, W[("lru_in", j)][0], "nt", f"lru_in_{i}")
            wa, wx = _block_diag(W["lru_w_a"][j]).astype(BF), _block_diag(W["lru_w_x"][j]).astype(BF)
            h, xc, r, ig, a, hs, y = _lru_fwd(z, W["lru_conv_w"][j], W["lru_conv_b"][j], wa, wx, W["lru_b_a"][j],
                                              W["lru_b_x"][j], W["lru_a_param"][j], W[("lru_out", j)][0], h, f"lru_fwd_{i}")
            s.update(hn=hn, z=z, wa=wa, wx=wx, xc=xc, r=r, ig=ig, a=a, hs=hs, y=y)
        else:
            h, s["u"], s["yb"] = _pool_fwd(h, W["mix_norm"][i], W[("pool_w", j)], W["pool_b"][j], W["pool_scale"][j],
                                           f"pool_fwd_{i}")
        s["h2"] = h
        s["ffn2"], h = ffn_fwd(h, W["ffn2_norm"][i], i, 2, [("ffn", i + 1, 1)] if i + 1 < L else [],
                               [("ple_gate", i), ("ple_proj", i)])
        s["h3"] = h
        h, s["n4"], s["gate"], s["pp"] = _ple_fwd(h, W["ple_norm"][i], W[("ple_gate", i)][0], W[("ple_proj", i)][0], p[i],
                                                  f"ple_fwd_{i}")
        saved.append(s)

    loss, dh, d_final = _loss_head(h, W["final_norm"], tgt)

    big, recv = {}, {}
    n_lru, n_pool = L // 2 + L % 2, L // 2
    small = {k: [None] * L for k in ("ffn1_norm", "mix_norm", "ffn2_norm", "ple_norm")}
    for k in ("lru_conv_w", "lru_conv_b", "lru_w_a", "lru_b_a", "lru_w_x", "lru_b_x", "lru_a_param"):
        small[k] = [None] * n_lru
    for k in ("pool_b", "pool_scale"):
        small[k] = [None] * n_pool

    def scattering(pieces):
        return [(k, m) for k, m in pieces if k in big] if dist else []

    def ffn_bwd(dh, h_in, g, acts, i, f, during):
        key, w = ("ffn", i, f), W[("ffn", i, f)]
        a, b, sv, n = acts
        out = [scattering(d) for d in during]
        sent = [[(big[k], m) for k, m in o] for o in out]
        da, db, dhb, got0 = _ffn_bwd_act(dh, a, b, w, f"ffn{f}_bwd_act_{i}", scatter=sent[0])
        big[key], got1 = _ffn_bwd_w(da, db, sv, n, dhb, f"ffn{f}_dw_{i}", scatter=sent[1])
        out.append(scattering([(key, 0)] + ([(key, 1)] if (i, f) == (0, 1) else [])))
        dh, dg, got2 = _ffn_bwd_in(da, db, w, h_in, g, dh, f"ffn{f}_bwd_in_{i}", scatter=[(big[k], m) for k, m in out[2]])
        for o, got in zip(out, (got0, got1, got2)):
            recv.update(zip(o, got))
        return dh, dg

    for i in reversed(range(L)):
        j = i // 2
        lru = i % 2 == 0
        s = saved[i]
        dh, dz, dpp, dg = _ple_bwd(dh, s["gate"], s["pp"], s["h3"], W["ple_norm"][i], W[("ple_gate", i)][0],
                                   f"ple_bwd_{i}")
        big[("ple_gate", i)] = _mm(s["n4"], dz, "tn", f"ple_gate_dw_{i}", out_dtype=BF)[None]
        big[("ple_proj", i)] = _mm(dpp, p[i], "tn", f"ple_proj_dw_{i}", out_dtype=BF)[None]
        small["ple_norm"][i] = dg[0]
        above = ("ffn", i + 1, 1)
        dh, dg = ffn_bwd(dh, s["h2"], W["ffn2_norm"][i], s["ffn2"], i, 2, [
            [(above, 1)], [(above, 2), (("ple_gate", i), None), (("ple_proj", i), None)]])
        small["ffn2_norm"][i] = dg[0]
        if lru:
            big[("lru_out", j)] = _mm(s["y"], dh, "tn", f"lru_out_dw_{i}", out_dtype=BF)[None]
            dgb, dxb, dpa, dpx, dsp, dba, dbx, dcb, dcw = _lru_bwd(
                dh, s["hs"], s["z"], s["a"], s["r"], s["ig"], s["xc"], s["wa"], s["wx"], W[("lru_out", j)][0],
                W["lru_conv_w"][j], W["lru_a_param"][j], f"lru_bwd_{i}")
            small["lru_a_param"][j], small["lru_b_a"][j], small["lru_b_x"][j] = dsp[0], dba[0], dbx[0]
            dwa, dwx = _lru_gates_dw(s["xc"], dpa, dpx, f"lru_gates_dw_{i}")
            small["lru_w_a"][j], small["lru_w_x"][j] = _diag_blocks(dwa, LRU_HEADS), _diag_blocks(dwx, LRU_HEADS)
            small["lru_conv_w"][j], small["lru_conv_b"][j] = dcw[:CONV_WIDTH], dcb[0]
            big[("lru_in", j)] = jnp.concatenate([_mm(dgb, s["hn"], "tn", f"lru_in_dw_g_{i}", out_dtype=BF),
                                                  _mm(dxb, s["hn"], "tn", f"lru_in_dw_x_{i}", out_dtype=BF)])[None]
            dh, dg = _lru_in_bwd(dgb, dxb, W[("lru_in", j)][0], s["h1"], W["mix_norm"][i], dh, f"lru_in_bwd_{i}")
            mixer = [("lru_in", j), ("lru_out", j)]
        else:
            du, v, dw, dbp, dsc = _pool_bwd(dh, s["u"], s["yb"], W[("pool_w", j)], W["pool_scale"][j], f"pool_bwd_{i}")
            big[("pool_w", j)] = dw.astype(BF)
            small["pool_b"][j], small["pool_scale"][j] = dbp[0], dsc[0]
            dh, dg = _pool_bwd_win(v, du, s["h1"], W["mix_norm"][i], dh, f"pool_bwd_win_{i}")
            mixer = [("pool_w", j)]
        small["mix_norm"][i] = dg[0]
        second = ("ffn", i, 2)
        dh, dg = ffn_bwd(dh, s["h0"], W["ffn1_norm"][i], s["ffn1"], i, 1, [
            [(second, 1)], [(second, 2)] + [(k, None) for k in mixer]])
        small["ffn1_norm"][i] = dg[0]

    small = {k: jnp.stack(v) for k, v in small.items()}
    small["final_norm"] = d_final[0]
    return loss, dh, big, recv, small


SMALL_SHARDED = ("pool_b", "pool_scale", "lru_conv_w")
SMALL = ("ffn1_norm", "mix_norm", "ffn2_norm", "ple_norm", "final_norm", "lru_conv_b", "lru_w_a", "lru_b_a",
         "lru_w_x", "lru_b_x", "lru_a_param", "pool_b", "pool_scale", "lru_conv_w")


def _pack_big(w):
    t = lambda a: jnp.swapaxes(a, -1, -2)
    out = {}
    for i in range(w["ffn1_norm"].shape[0]):
        for f in (1, 2):
            out[("ffn", i, f)] = jnp.stack([t(w[f"ffn{f}_w_gate"][i]), t(w[f"ffn{f}_w_up"][i]), w[f"ffn{f}_w_down"][i]])
        out[("ple_gate", i)], out[("ple_proj", i)] = w["ple_w_gate"][i][None], t(w["ple_w_proj"][i])[None]
    for j in range(w["lru_w_in"].shape[0]):
        out[("lru_in", j)], out[("lru_out", j)] = t(w["lru_w_in"][j])[None], w["lru_w_out"][j][None]
    for j in range(w["pool_w"].shape[0]):
        out[("pool_w", j)] = w["pool_w"][j]
    return out


def _unpack_big(b, L):
    t = lambda a: jnp.swapaxes(a, -1, -2)
    n_lru, n_pool = L // 2 + L % 2, L // 2
    out = {"lru_w_in": jnp.stack([t(b[("lru_in", j)][0]) for j in range(n_lru)]),
           "lru_w_out": jnp.stack([b[("lru_out", j)][0] for j in range(n_lru)]),
           "pool_w": jnp.stack([b[("pool_w", j)] for j in range(n_pool)]),
           "ple_w_gate": jnp.stack([b[("ple_gate", i)][0] for i in range(L)]),
           "ple_w_proj": jnp.stack([t(b[("ple_proj", i)][0]) for i in range(L)])}
    for f in (1, 2):
        out[f"ffn{f}_w_gate"] = jnp.stack([t(b[("ffn", i, f)][0]) for i in range(L)])
        out[f"ffn{f}_w_up"] = jnp.stack([t(b[("ffn", i, f)][1]) for i in range(L)])
        out[f"ffn{f}_w_down"] = jnp.stack([b[("ffn", i, f)][2] for i in range(L)])
    return out


def _flatten(parts, names, rows_of=LANES):
    flat = jnp.concatenate([parts[k].reshape(-1) for k in names])
    pad = (-flat.size) % (16 * rows_of)
    return jnp.pad(flat, (0, pad)).reshape(1, -1, rows_of)


def _unflatten(flat, like, names):
    out, o = {}, 0
    flat = flat.reshape(-1)
    for k in names:
        n = like[k].size
        out[k] = flat[o:o + n].reshape(like[k].shape)
        o += n
    return out


def kernel(x, p, ffn1_norm, ffn1_w_gate, ffn1_w_up, ffn1_w_down, mix_norm, lru_w_in, lru_conv_w, lru_conv_b, lru_w_a, lru_b_a, lru_w_x, lru_b_x, lru_a_param, lru_w_out, pool_w, pool_b, pool_scale, ffn2_norm, ffn2_w_gate, ffn2_w_up, ffn2_w_down, ple_norm, ple_w_gate, ple_w_proj, final_norm, loss_target, m_ffn1_norm, m_ffn1_w_gate, m_ffn1_w_up, m_ffn1_w_down, m_mix_norm, m_lru_w_in, m_lru_conv_w, m_lru_conv_b, m_lru_w_a, m_lru_b_a, m_lru_w_x, m_lru_b_x, m_lru_a_param, m_lru_w_out, m_pool_w, m_pool_b, m_pool_scale, m_ffn2_norm, m_ffn2_w_gate, m_ffn2_w_up, m_ffn2_w_down, m_ple_norm, m_ple_w_gate, m_ple_w_proj, m_final_norm, v_ffn1_norm, v_ffn1_w_gate, v_ffn1_w_up, v_ffn1_w_down, v_mix_norm, v_lru_w_in, v_lru_conv_w, v_lru_conv_b, v_lru_w_a, v_lru_b_a, v_lru_w_x, v_lru_b_x, v_lru_a_param, v_lru_w_out, v_pool_w, v_pool_b, v_pool_scale, v_ffn2_norm, v_ffn2_w_gate, v_ffn2_w_up, v_ffn2_w_down, v_ple_norm, v_ple_w_gate, v_ple_w_proj, v_final_norm):
    names = ["ffn1_norm", "ffn1_w_gate", "ffn1_w_up", "ffn1_w_down", "mix_norm", "lru_w_in", "lru_conv_w", "lru_conv_b",
             "lru_w_a", "lru_b_a", "lru_w_x", "lru_b_x", "lru_a_param", "lru_w_out", "pool_w", "pool_b", "pool_scale",
             "ffn2_norm", "ffn2_w_gate", "ffn2_w_up", "ffn2_w_down", "ple_norm", "ple_w_gate", "ple_w_proj", "final_norm"]
    w = dict(zip(names, [ffn1_norm, ffn1_w_gate, ffn1_w_up, ffn1_w_down, mix_norm, lru_w_in, lru_conv_w, lru_conv_b, lru_w_a, lru_b_a, lru_w_x, lru_b_x, lru_a_param, lru_w_out, pool_w, pool_b, pool_scale, ffn2_norm, ffn2_w_gate, ffn2_w_up, ffn2_w_down, ple_norm, ple_w_gate, ple_w_proj, final_norm]))
    m = dict(zip(names, [m_ffn1_norm, m_ffn1_w_gate, m_ffn1_w_up, m_ffn1_w_down, m_mix_norm, m_lru_w_in, m_lru_conv_w, m_lru_conv_b, m_lru_w_a, m_lru_b_a, m_lru_w_x, m_lru_b_x, m_lru_a_param, m_lru_w_out, m_pool_w, m_pool_b, m_pool_scale, m_ffn2_norm, m_ffn2_w_gate, m_ffn2_w_up, m_ffn2_w_down, m_ple_norm, m_ple_w_gate, m_ple_w_proj, m_final_norm]))
    v = dict(zip(names, [v_ffn1_norm, v_ffn1_w_gate, v_ffn1_w_up, v_ffn1_w_down, v_mix_norm, v_lru_w_in, v_lru_conv_w, v_lru_conv_b, v_lru_w_a, v_lru_b_a, v_lru_w_x, v_lru_b_x, v_lru_a_param, v_lru_w_out, v_pool_w, v_pool_b, v_pool_scale, v_ffn2_norm, v_ffn2_w_gate, v_ffn2_w_up, v_ffn2_w_down, v_ple_norm, v_ple_w_gate, v_ple_w_proj, v_final_norm]))
    L = p.shape[0]
    px, py, pc = _position()
    me = 4 * px + 2 * py + pc

    blocks = {k: b.astype(BF) for k, b in _pack_big(w).items()}
    first = ("ffn", 0, 1)
    got, small_blocks = _all_gather([blocks[first], _flatten(w, SMALL_SHARDED)], "gather_first")
    W = {first: got}
    per_dev = small_blocks.reshape(N_DEV, -1)
    shards = [_unflatten(per_dev[k], w, SMALL_SHARDED) for k in range(N_DEV)]
    for k in SMALL:
        W[k] = jnp.concatenate([s[k] for s in shards], axis=-1) if k in SMALL_SHARDED else w[k]

    loss, dx, big, recv, small = _local_step(x[0], p[:, 0], loss_target[0], W, blocks)

    last = [(k, m) for k in big if (k, None) not in recv for m in range(big[k].shape[0]) if (k, m) not in recv]
    got, (parts,) = _scatter_and_gather([(big[k], m) for k, m in last], [_flatten(small, SMALL).astype(BF)],
                                        "scatter_last_gather_small")
    recv.update(zip(last, got))

    def total(k):
        tag = "sum_" + "_".join(map(str, k))
        if (k, None) in recv:
            return _scatter_sum(recv[(k, None)], tag)
        return jnp.concatenate([_scatter_sum(recv[(k, m)], f"{tag}_{m}") for m in range(big[k].shape[0])])

    grads = _unpack_big({k: total(k) for k in big}, L)
    total_small = _sum_devices(parts.reshape(N_DEV, -1, LANES), "sum_small_grads")
    full = _unflatten(total_small, {k: W[k] for k in SMALL}, SMALL)
    for k in SMALL:
        if k in SMALL_SHARDED:
            n = w[k].shape[-1]
            grads[k] = lax.dynamic_slice_in_dim(full[k], me * n, n, axis=-1)
        else:
            grads[k] = full[k]

    delta, new_m, new_v = {}, {}, {}
    for k in names:
        delta[k], new_m[k], new_v[k] = _adamw(w[k], grads[k], m[k], v[k], f"adamw_{k}")
    total_loss = lax.psum(loss[0, 0], ("x", "y", "c"))
    return (total_loss, dx[None], *[grads[k] for k in names], *[delta[k] for k in names],
            *[new_m[k] for k in names], *[new_v[k] for k in names])
```

```python
import functools

import jax
import jax.numpy as jnp
from jax import lax
from jax.experimental import pallas as pl
from jax.experimental.pallas import tpu as pltpu

F32 = jnp.float32
BF = jnp.bfloat16
MESH = pl.DeviceIdType.MESH

RMS_EPS = 1e-6
LRU_C = 8.0
LRU_HEADS = 16
CONV_WIDTH = 4
POOL_WINDOWS = (2, 4, 8, 16)
ADAM_LR, ADAM_B1, ADAM_B2, ADAM_EPS, ADAM_WD, ADAM_STEP = 0.001, 0.9, 0.999, 1e-08, 0.01, 10

N_DEV = 8
LANES = 128
SUBLANES = 8
GATE_COLS, GATE_SPAN = 256, 512
HALO = 16
VMEM_LIMIT = 56 * 1024 * 1024

TM_FFN = 1024
TM_FFN_ACT = 2048
TM_FFN_IN = 512
TF_FFN = 256
TF_FFN_WG = 1408
TK_FFN_WG = 512
TB_SEQ = 256
TM_EW = 512
TM_MM, TN_MM, TK_MM = 1024, 512, 1024
TR_ADAM = 512


def _tile(n, pref, align):
    if n <= pref:
        return n
    t = (pref // align) * align
    while t >= align:
        if n % t == 0:
            return t
        t -= align
    raise ValueError(f"no tile for {n} (pref {pref}, align {align})")


def _params(*sem):
    return pltpu.CompilerParams(dimension_semantics=sem, vmem_limit_bytes=VMEM_LIMIT)


def _dot(a, b):
    return lax.dot_general(a, b, (((1,), (0,)), ((), ())), preferred_element_type=F32)


def _dot_nt(a, b):
    return lax.dot_general(a, b, (((1,), (1,)), ((), ())), preferred_element_type=F32)


def _dot_tn(a, b):
    return lax.dot_general(a, b, (((0,), (0,)), ((), ())), preferred_element_type=F32)


def _sigmoid(x):
    return 0.5 + 0.5 * jnp.tanh(0.5 * x)


def _gelu_parts(x):
    k0, k1 = 0.7978845608028654, 0.044715
    t = jnp.tanh(k0 * (x + k1 * x * x * x))
    g = 0.5 * x * (1.0 + t)
    dg = 0.5 * (1.0 + t) + 0.5 * x * (1.0 - t * t) * k0 * (1.0 + 3.0 * k1 * x * x)
    return g, dg


def _one_minus_sq(la, a):
    return jnp.tanh(-la) * (1.0 + a * a)


def _softplus_neg(l):
    u = jnp.exp(-jnp.abs(l))
    w = 1.0 + u
    log1p = jnp.where(w == 1.0, u, jnp.log(w) * (u / jnp.where(w == 1.0, 1.0, w - 1.0)))
    return jnp.maximum(-l, 0.0) + log1p


def _rms_parts(x, g):
    r = lax.rsqrt(jnp.mean(x * x, axis=-1, keepdims=True) + RMS_EPS)
    nhat = x * r
    return nhat * g, nhat, r


def _rms_bwd_parts(x, g, dn):
    _, nhat, r = _rms_parts(x, g)
    u = dn * g
    dx = r * (u - nhat * jnp.mean(u * nhat, axis=-1, keepdims=True))
    return dx, jnp.sum(dn * nhat, axis=0, keepdims=True)


def _row_spec(tm, d, single=False):
    if single:
        return pl.BlockSpec((tm, d), lambda i, *_: (i, 0), pipeline_mode=pl.Buffered(1))
    return pl.BlockSpec((tm, d), lambda i, *_: (i, 0))


def _vec_spec(d, rows=1):
    return pl.BlockSpec((rows, d), lambda *_: (0, 0))


def _mm(x, w, mode, name, out_dtype=F32, res=None, alpha=1.0, tm=None, tn=None, tk=None):
    if mode == "nn":
        (M, K), (_, N) = x.shape, w.shape
    elif mode == "nt":
        (M, K), (N, _) = x.shape, w.shape
    else:
        (K, M), (_, N) = x.shape, w.shape
    tm = _tile(M, tm or TM_MM, LANES if mode == "tn" else SUBLANES)
    tn = _tile(N, tn or TN_MM, LANES)
    tk = _tile(K, tk or TK_MM, LANES if mode != "tn" else 16)
    nk = K // tk
    dot = {"nn": _dot, "nt": _dot_nt, "tn": _dot_tn}[mode]

    def body(*refs):
        if res is None:
            x_ref, w_ref, o_ref, acc = refs
        else:
            x_ref, w_ref, r_ref, o_ref, acc = refs
        k = pl.program_id(2)

        @pl.when(k == 0)
        def _():
            acc[...] = jnp.zeros_like(acc)

        acc[...] += dot(x_ref[...].astype(BF), w_ref[...].astype(BF))

        @pl.when(k == nk - 1)
        def _():
            r = acc[...] if alpha == 1.0 else acc[...] * alpha
            if res is not None:
                r = r_ref[...] + r
            o_ref[...] = r.astype(out_dtype)

    if mode == "nn":
        specs = [pl.BlockSpec((tm, tk), lambda i, j, k: (i, k)), pl.BlockSpec((tk, tn), lambda i, j, k: (k, j))]
    elif mode == "nt":
        specs = [pl.BlockSpec((tm, tk), lambda i, j, k: (i, k)), pl.BlockSpec((tn, tk), lambda i, j, k: (j, k))]
    else:
        specs = [pl.BlockSpec((tk, tm), lambda i, j, k: (k, i)), pl.BlockSpec((tk, tn), lambda i, j, k: (k, j))]
    args = [x, w]
    if res is not None:
        specs.append(pl.BlockSpec((tm, tn), lambda i, j, k: (i, j)))
        args.append(res)
    return pl.pallas_call(
        body, name=name, grid=(M // tm, N // tn, nk), in_specs=specs,
        out_specs=pl.BlockSpec((tm, tn), lambda i, j, k: (i, j)),
        out_shape=jax.ShapeDtypeStruct((M, N), out_dtype),
        scratch_shapes=[pltpu.VMEM((tm, tn), F32)],
        compiler_params=_params("parallel", "parallel", "arbitrary"),
    )(*args)


def _loss_head(h, g, tgt):
    T, D = h.shape
    tm = _tile(T, TM_EW, 16)

    def body(h_ref, g_ref, t_ref, loss_ref, dh_ref, dg_ref):
        @pl.when(pl.program_id(0) == 0)
        def _():
            dg_ref[...] = jnp.zeros_like(dg_ref)
            loss_ref[...] = jnp.zeros_like(loss_ref)

        x, gg = h_ref[...], g_ref[...]
        y = _rms_parts(x, gg)[0]
        e = y - t_ref[...]
        part = jnp.sum(jnp.sum(e * e, axis=0, keepdims=True), axis=1, keepdims=True) * (0.5 / D)
        loss_ref[...] += jnp.broadcast_to(part, loss_ref.shape)
        dx, dg = _rms_bwd_parts(x, gg, e * (1.0 / D))
        dh_ref[...] = dx
        dg_ref[...] += dg

    return pl.pallas_call(
        body, name="loss_head", grid=(T // tm,),
        in_specs=[_row_spec(tm, D), _vec_spec(D), _row_spec(tm, D)],
        out_specs=[_vec_spec(LANES), _row_spec(tm, D), _vec_spec(D)],
        out_shape=[jax.ShapeDtypeStruct((1, LANES), F32), jax.ShapeDtypeStruct((T, D), F32),
                   jax.ShapeDtypeStruct((1, D), F32)],
        compiler_params=_params("arbitrary"),
    )(h, g.reshape(1, D), tgt)


def _carry(plan, first, mid, last):
    pl.when(first)(plan[0])
    if len(plan) == 3:
        pl.when(mid)(plan[1])
    pl.when(last)(plan[-1])


def _ffn_fwd_act(h, g, wffn, name, gather=()):
    T, D = h.shape
    F = wffn.shape[1]
    tm, tf = _tile(T, TM_FFN_ACT, 16), _tile(F, TF_FFN, LANES)
    ni, nf, ng = T // tm, F // tf, len(gather)

    def body(*refs):
        h_ref, g_ref, wg_ref, wu_ref = refs[:4]
        srcs, (a_ref, b_ref, s_ref, n_ref), outs = refs[4:4 + ng], refs[4 + ng:8 + ng], refs[8 + ng:8 + 2 * ng]
        i, j = pl.program_id(0), pl.program_id(1)
        if ng:
            _carry(_gather_plan(gather, srcs, outs, *refs[8 + 2 * ng:]), jnp.logical_and(i == 0, j == 0),
                   jnp.logical_and(i == (3 * ni) // 4, j == 0), jnp.logical_and(i == ni - 1, j == nf - 1))

        @pl.when(j == 0)
        def _():
            n_ref[...] = _rms_parts(h_ref[...], g_ref[...])[0].astype(BF)

        n = n_ref[...]
        a = _dot_nt(n, wg_ref[...])
        b = _dot_nt(n, wu_ref[...])
        a_ref[...] = a.astype(BF)
        b_ref[...] = b.astype(BF)
        s_ref[...] = (a * _sigmoid(a) * b).astype(BF)

    tile = pl.BlockSpec((tm, tf), lambda i, j: (i, j))
    w = [pl.BlockSpec((None, tf, D), functools.partial(lambda k, i, j: (k, j, 0), k)) for k in (0, 1)]
    hbm = pl.BlockSpec(memory_space=pl.ANY)
    outs = pl.pallas_call(
        body, name=name, grid=(ni, nf), in_specs=[_row_spec(tm, D), _vec_spec(D)] + w + [hbm] * ng,
        out_specs=[tile, tile, tile, _row_spec(tm, D)] + [hbm] * ng,
        out_shape=[jax.ShapeDtypeStruct((T, F), BF)] * 3 + [jax.ShapeDtypeStruct((T, D), BF)] + _gathered_shapes(gather),
        scratch_shapes=_gather_sems(ng), compiler_params=_params("arbitrary", "arbitrary"),
    )(h, g.reshape(1, D), wffn, wffn, *gather)
    return outs[0], outs[1], outs[2], outs[3], list(outs[4:])


def _ffn_fwd_out(s, wffn, h, name, gather=()):
    T, F = s.shape
    D = h.shape[1]
    tm = _tile(T, TM_FFN_IN, 16)
    ni, ng = T // tm, len(gather)

    def body(*refs):
        s_ref, w_ref, h_ref = refs[:3]
        srcs, o_ref, outs = refs[3:3 + ng], refs[3 + ng], refs[4 + ng:4 + 2 * ng]
        i = pl.program_id(0)
        if ng:
            _carry(_gather_plan(gather, srcs, outs, *refs[4 + 2 * ng:]), i == 0, i == (3 * ni) // 4, i == ni - 1)
        o_ref[...] = h_ref[...] + 0.5 * _dot(s_ref[...], w_ref[...])

    hbm = pl.BlockSpec(memory_space=pl.ANY)
    outs = pl.pallas_call(
        body, name=name, grid=(ni,),
        in_specs=[_row_spec(tm, F), pl.BlockSpec((None, F, D), lambda i: (2, 0, 0), pipeline_mode=pl.Buffered(1)),
                  _row_spec(tm, D)] + [hbm] * ng,
        out_specs=[_row_spec(tm, D)] + [hbm] * ng,
        out_shape=[jax.ShapeDtypeStruct((T, D), F32)] + _gathered_shapes(gather),
        scratch_shapes=_gather_sems(ng), compiler_params=_params("arbitrary"),
    )(s, wffn, h, *gather)
    return outs[0], list(outs[1:])


def _ffn_bwd_act(dh, a, b, wffn, name, scatter=()):
    T, D = dh.shape
    F = wffn.shape[1]
    tm, tf = _tile(T, TM_FFN_ACT, 16), _tile(F, TF_FFN, LANES)
    ni, nf, ng = T // tm, F // tf, len(scatter)

    def body(*refs):
        dh_ref, a_ref, b_ref, wd_ref = refs[:4]
        srcs, (da_ref, db_ref, dhb_ref), outs = refs[4:4 + ng], refs[4 + ng:7 + ng], refs[7 + ng:7 + 2 * ng]
        i, j = pl.program_id(0), pl.program_id(1)
        if ng:
            _carry(_scatter_plan(scatter, srcs, outs, *refs[7 + 2 * ng:]), jnp.logical_and(i == 0, j == 0), None,
                   jnp.logical_and(i == ni - 1, j == nf - 1))

        @pl.when(j == 0)
        def _():
            dhb_ref[...] = dh_ref[...].astype(BF)

        ds = 0.5 * _dot_nt(dhb_ref[...], wd_ref[...])
        av, bv = a_ref[...].astype(F32), b_ref[...].astype(F32)
        sig = _sigmoid(av)
        da_ref[...] = (ds * bv * (sig * (1.0 + av * (1.0 - sig)))).astype(BF)
        db_ref[...] = (ds * (av * sig)).astype(BF)

    tile = pl.BlockSpec((tm, tf), lambda i, j: (i, j))
    hbm = pl.BlockSpec(memory_space=pl.ANY)
    outs = pl.pallas_call(
        body, name=name, grid=(ni, nf),
        in_specs=[_row_spec(tm, D), tile, tile, pl.BlockSpec((None, tf, D), lambda i, j: (2, j, 0))] + [hbm] * ng,
        out_specs=[tile, tile, _row_spec(tm, D)] + [hbm] * ng,
        out_shape=[jax.ShapeDtypeStruct((T, F), BF)] * 2 + [jax.ShapeDtypeStruct((T, D), BF)]
        + _scattered_shapes(scatter),
        scratch_shapes=_scatter_sems(ng), compiler_params=_params("arbitrary", "arbitrary"),
    )(dh, a, b, wffn, *[piece[0] for piece in scatter])
    return outs[0], outs[1], outs[2], list(outs[3:])


def _two_dot_norm_bwd(x1, x2, w, w_specs, h, g, dh, name, scatter=()):
    T, K = x1.shape
    D = h.shape[1]
    tm = _tile(T, TM_FFN_IN, 16)
    ni, ng = T // tm, len(scatter)

    def body(*refs):
        x1_ref, x2_ref, w1_ref, w2_ref, h_ref, g_ref, dh_ref = refs[:7]
        srcs, (o_ref, dg_ref), outs = refs[7:7 + ng], refs[7 + ng:9 + ng], refs[9 + ng:9 + 2 * ng]
        i = pl.program_id(0)
        if ng:
            _carry(_scatter_plan(scatter, srcs, outs, *refs[9 + 2 * ng:]), i == 0, None, i == ni - 1)

        @pl.when(i == 0)
        def _():
            dg_ref[...] = jnp.zeros_like(dg_ref)

        dn = _dot(x1_ref[...], w1_ref[...]) + _dot(x2_ref[...], w2_ref[...])
        dx, dg = _rms_bwd_parts(h_ref[...], g_ref[...], dn)
        o_ref[...] = dh_ref[...] + dx
        dg_ref[...] += dg

    hbm = pl.BlockSpec(memory_space=pl.ANY)
    act = pl.BlockSpec((tm, K), lambda i: (i, 0))
    outs = pl.pallas_call(
        body, name=name, grid=(ni,),
        in_specs=[act, act] + w_specs + [_row_spec(tm, D), _vec_spec(D), _row_spec(tm, D)] + [hbm] * ng,
        out_specs=[_row_spec(tm, D), _vec_spec(D)] + [hbm] * ng,
        out_shape=[jax.ShapeDtypeStruct((T, D), F32), jax.ShapeDtypeStruct((1, D), F32)] + _scattered_shapes(scatter),
        scratch_shapes=_scatter_sems(ng), compiler_params=_params("arbitrary"),
    )(x1, x2, w, w, h, g.reshape(1, D), dh, *[piece[0] for piece in scatter])
    return outs[0], outs[1], list(outs[2:])


def _ffn_bwd_in(da, db, wffn, h, g, dh, name, scatter=()):
    F, D = wffn.shape[1:]
    specs = [pl.BlockSpec((None, F, D), functools.partial(lambda k, i: (k, 0, 0), k), pipeline_mode=pl.Buffered(1))
             for k in (0, 1)]
    return _two_dot_norm_bwd(da, db, wffn, specs, h, g, dh, name, scatter)


def _lru_in_bwd(dgb, dxb, win, h, g, dh, name):
    R, D = win.shape[0] // 2, win.shape[1]
    specs = [pl.BlockSpec((R, D), functools.partial(lambda k, i: (k, 0), k), pipeline_mode=pl.Buffered(1)) for k in (0, 1)]
    return _two_dot_norm_bwd(dgb, dxb, win, specs, h, g, dh, name)[:2]


def _ffn_bwd_w(da, db, s, n, dhb, name, scatter=()):
    T, F = da.shape
    D = n.shape[1]
    tf, tk = _tile(F, TF_FFN_WG, LANES), _tile(T, TK_FFN_WG, 16)
    nj, nk, ng = F // tf, T // tk, len(scatter)

    def body(*refs):
        da_ref, db_ref, s_ref, n_ref, dh_ref = refs[:5]
        srcs, o_ref, outs = refs[5:5 + ng], refs[5 + ng], refs[6 + ng:6 + 2 * ng]
        g_sc, u_sc, d_sc = refs[6 + 2 * ng:9 + 2 * ng]
        j, k = pl.program_id(0), pl.program_id(1)
        if ng:
            _carry(_scatter_plan(scatter, srcs, outs, *refs[9 + 2 * ng:]), jnp.logical_and(j == 0, k == 0), None,
                   jnp.logical_and(j == nj - 1, k == nk - 1))

        @pl.when(k == 0)
        def _():
            g_sc[...] = jnp.zeros_like(g_sc)
            u_sc[...] = jnp.zeros_like(u_sc)
            d_sc[...] = jnp.zeros_like(d_sc)

        nv = n_ref[...]
        g_sc[...] += _dot_tn(da_ref[...], nv)
        u_sc[...] += _dot_tn(db_ref[...], nv)
        d_sc[...] += _dot_tn(s_ref[...], dh_ref[...])

        @pl.when(k == nk - 1)
        def _():
            o_ref[0] = g_sc[...].astype(BF)
            o_ref[1] = u_sc[...].astype(BF)
            o_ref[2] = (0.5 * d_sc[...]).astype(BF)

    act = pl.BlockSpec((tk, tf), lambda j, k: (k, j))
    tok = pl.BlockSpec((tk, D), lambda j, k: (k, 0))
    hbm = pl.BlockSpec(memory_space=pl.ANY)
    outs = pl.pallas_call(
        body, name=name, grid=(nj, nk), in_specs=[act, act, act, tok, tok] + [hbm] * ng,
        out_specs=[pl.BlockSpec((3, tf, D), lambda j, k: (0, j, 0), pipeline_mode=pl.Buffered(1))] + [hbm] * ng,
        out_shape=[jax.ShapeDtypeStruct((3, F, D), BF)] + _scattered_shapes(scatter),
        scratch_shapes=[pltpu.VMEM((tf, D), F32)] * 3 + _scatter_sems(ng),
        compiler_params=_params("arbitrary", "arbitrary"),
    )(da, db, s, n, dhb, *[piece[0] for piece in scatter])
    return outs[0], list(outs[1:])


def _ple_fwd(h, g, wg, wp, p, name):
    T, D = h.shape
    P = p.shape[1]
    tm = _tile(T, TM_EW, 16)

    def body(h_ref, g_ref, wg_ref, wp_ref, p_ref, o_ref, n_ref, gate_ref, pp_ref):
        x = h_ref[...]
        n = _rms_parts(x, g_ref[...])[0].astype(BF)
        gate = _sigmoid(_dot(n, wg_ref[...]))
        pp = _dot_nt(p_ref[...].astype(BF), wp_ref[...])
        o_ref[...] = x + gate * pp
        n_ref[...] = n
        gate_ref[...] = gate.astype(BF)
        pp_ref[...] = pp.astype(BF)

    row = _row_spec(tm, D)
    return pl.pallas_call(
        body, name=name, grid=(T // tm,),
        in_specs=[row, _vec_spec(D), _vec_spec(D, D), _vec_spec(P, D), _row_spec(tm, P)], out_specs=[row] * 4,
        out_shape=[jax.ShapeDtypeStruct((T, D), F32)] + [jax.ShapeDtypeStruct((T, D), BF)] * 3,
        compiler_params=_params("parallel"),
    )(h, g.reshape(1, D), wg, wp, p)


def _ple_bwd(dh, gate, pp, h, g, wg, name):
    T, D = dh.shape
    tm = _tile(T, TM_EW, 16)

    def body(dh_ref, gate_ref, pp_ref, h_ref, g_ref, wg_ref, o_ref, dz_ref, dp_ref, dg_ref):
        @pl.when(pl.program_id(0) == 0)
        def _():
            dg_ref[...] = jnp.zeros_like(dg_ref)

        d, gate = dh_ref[...], gate_ref[...].astype(F32)
        dz = (d * pp_ref[...].astype(F32) * gate * (1.0 - gate)).astype(BF)
        dx, dg = _rms_bwd_parts(h_ref[...], g_ref[...], _dot_nt(dz, wg_ref[...]))
        o_ref[...] = d + dx
        dz_ref[...] = dz
        dp_ref[...] = (d * gate).astype(BF)
        dg_ref[...] += dg

    row = _row_spec(tm, D)
    return pl.pallas_call(
        body, name=name, grid=(T // tm,), in_specs=[row, row, row, row, _vec_spec(D), _vec_spec(D, D)],
        out_specs=[row, row, row, _vec_spec(D)],
        out_shape=[jax.ShapeDtypeStruct((T, D), F32), jax.ShapeDtypeStruct((T, D), BF), jax.ShapeDtypeStruct((T, D), BF),
                   jax.ShapeDtypeStruct((1, D), F32)],
        compiler_params=_params("arbitrary"),
    )(dh, gate, pp, h, g.reshape(1, D), wg)


def _lru_in(h, g, win, name):
    T, D = h.shape
    R = win.shape[0] // 2
    tm = _tile(T, TM_EW, 16)

    def body(h_ref, g_ref, w_ref, n_ref, gb_ref, xb_ref):
        n = _rms_parts(h_ref[...], g_ref[...])[0].astype(BF)
        n_ref[...] = n
        z = _dot_nt(n, w_ref[...])
        gb_ref[...] = z[:, :R].astype(BF)
        xb_ref[...] = z[:, R:]

    return pl.pallas_call(
        body, name=name, grid=(T // tm,),
        in_specs=[_row_spec(tm, D), _vec_spec(D), pl.BlockSpec((2 * R, D), lambda i: (0, 0), pipeline_mode=pl.Buffered(1))],
        out_specs=[_row_spec(tm, D), _row_spec(tm, R), _row_spec(tm, R)],
        out_shape=[jax.ShapeDtypeStruct((T, D), BF), jax.ShapeDtypeStruct((T, R), BF), jax.ShapeDtypeStruct((T, R), F32)],
        compiler_params=_params("parallel"),
    )(h, g.reshape(1, D), win)


def _lru_fwd(gb, xb, conv_w, conv_b, wa, wx, b_a, b_x, a_param, wout, h, name):
    T, R = xb.shape
    D = h.shape[1]
    tb = _tile(T, TB_SEQ, HALO)
    per, ng = tb // HALO, tb // SUBLANES

    def body(g_ref, x_ref, halo_ref, cw_ref, cb_ref, wa_ref, wx_ref, ba_ref, bx_ref, ap_ref, wo_ref, h_ref,
             o_ref, xc_ref, r_ref, ig_ref, a_ref, hs_ref, y_ref, ext, carry, a_sc, b_sc):
        i = pl.program_id(0)

        @pl.when(i == 0)
        def _():
            carry[...] = jnp.zeros_like(carry)

        ext[pl.ds(0, HALO), :] = jnp.where(i > 0, halo_ref[...], 0.0)
        ext[pl.ds(HALO, tb), :] = x_ref[...]
        xc = cb_ref[...] + cw_ref[0:1, :] * ext[pl.ds(HALO - 3, tb), :]
        for k in range(1, CONV_WIDTH):
            xc = xc + cw_ref[k:k + 1, :] * ext[pl.ds(HALO - 3 + k, tb), :]
        xcb = xc.astype(BF)
        r = _sigmoid(_dot(xcb, wa_ref[...]) + ba_ref[...])
        ig = _sigmoid(_dot(xcb, wx_ref[...]) + bx_ref[...])
        la = -LRU_C * r * _softplus_neg(ap_ref[...])
        av = jnp.exp(la)
        xc_ref[...] = xc
        r_ref[...] = r
        ig_ref[...] = ig
        a_ref[...] = av
        A = av.reshape(ng, SUBLANES, R)
        B = (jnp.sqrt(_one_minus_sq(la, av)) * (ig * xc)).reshape(ng, SUBLANES, R)
        sub = lax.broadcasted_iota(jnp.int32, (1, SUBLANES, R), 1)
        for k in (1, 2, 4):
            m = sub >= k
            a_n = jnp.where(m, pltpu.roll(A, k, 1), 1.0)
            b_n = jnp.where(m, pltpu.roll(B, k, 1), 0.0)
            B = A * b_n + B
            A = A * a_n
        a_sc[...] = A.reshape(tb, R)
        b_sc[...] = B.reshape(tb, R)

        def group(q, c):
            rows = pl.ds(pl.multiple_of(q * SUBLANES, SUBLANES), SUBLANES)
            hg = a_sc[rows, :] * c + b_sc[rows, :]
            hs_ref[rows, :] = hg
            return hg[SUBLANES - 1:SUBLANES, :]

        carry[...] = lax.fori_loop(0, ng, group, carry[...])
        y = (hs_ref[...] * _gelu_parts(g_ref[...].astype(F32))[0]).astype(BF)
        y_ref[...] = y
        o_ref[...] = h_ref[...] + _dot(y, wo_ref[...])

    once = lambda rows, cols: pl.BlockSpec((rows, cols), lambda i: (0, 0), pipeline_mode=pl.Buffered(1))
    gate = pl.BlockSpec((tb, R), lambda i: (i, 0))
    halo = pl.BlockSpec((HALO, R), lambda i: (jnp.maximum(i * per - 1, 0), 0))
    return pl.pallas_call(
        body, name=name, grid=(T // tb,),
        in_specs=[gate, gate, halo, _vec_spec(R, CONV_WIDTH), _vec_spec(R), once(R, R), once(R, R), _vec_spec(R),
                  _vec_spec(R), _vec_spec(R), once(R, D), _row_spec(tb, D)],
        out_specs=[_row_spec(tb, D)] + [gate] * 6,
        out_shape=[jax.ShapeDtypeStruct((T, D), F32)] + [jax.ShapeDtypeStruct((T, R), F32)] * 5
        + [jax.ShapeDtypeStruct((T, R), BF)],
        scratch_shapes=[pltpu.VMEM((HALO + tb, R), F32), pltpu.VMEM((1, R), F32), pltpu.VMEM((tb, R), F32),
                        pltpu.VMEM((tb, R), F32)],
        compiler_params=_params("arbitrary"),
    )(gb, xb, xb, conv_w, conv_b.reshape(1, R), wa, wx, b_a.reshape(1, R), b_x.reshape(1, R), a_param.reshape(1, R), wout, h)


def _lru_bwd(dh, hs, gb, xb, a, r, ig, xc, wa, wx, wout, conv_w, a_param, name):
    T, R = hs.shape
    D = dh.shape[1]
    tb = _tile(T, TB_SEQ, HALO)
    per, nt, ng = tb // HALO, T // tb, tb // SUBLANES

    def body(dh_ref, h_ref, hp_ref, g_ref, x_ref, xp_ref, a_ref, r_ref, ig_ref, xc_ref, wa_ref, wx_ref, wo_ref, cw_ref,
             ap_ref, dgb_ref, dxb_ref, dpa_ref, dpx_ref, dsp_ref, dba_ref, dbx_ref, dcb_ref, dcw_ref,
             hext, xext, dext, carry, later, a_sc, b_sc, d_sc, l_sc):
        i = pl.program_id(0)

        @pl.when(i == 0)
        def _():
            for ref in (dsp_ref, dba_ref, dbx_ref, dcb_ref, dcw_ref, carry, later):
                ref[...] = jnp.zeros_like(ref)

        dy = _dot_nt(dh_ref[...].astype(BF), wo_ref[...])
        gl, dgl = _gelu_parts(g_ref[...].astype(F32))
        hv, av = h_ref[...], a_ref[...]
        dhd = dy * gl
        dgb_ref[...] = (dy * hv * dgl).astype(BF)
        d_sc[...] = dhd
        A = av.reshape(ng, SUBLANES, R)
        B = A * dhd.reshape(ng, SUBLANES, R)
        sub = lax.broadcasted_iota(jnp.int32, (1, SUBLANES, R), 1)
        for k in (1, 2, 4):
            m = sub < SUBLANES - k
            a_n = jnp.where(m, pltpu.roll(A, SUBLANES - k, 1), 1.0)
            b_n = jnp.where(m, pltpu.roll(B, SUBLANES - k, 1), 0.0)
            B = A * b_n + B
            A = A * a_n
        a_sc[...] = A.reshape(tb, R)
        b_sc[...] = B.reshape(tb, R)
        sub8 = lax.broadcasted_iota(jnp.int32, (SUBLANES, R), 0)

        def group(q, c):
            rows = pl.ds(pl.multiple_of((ng - 1 - q) * SUBLANES, SUBLANES), SUBLANES)
            mu = a_sc[rows, :] * c + b_sc[rows, :]
            l_sc[rows, :] = d_sc[rows, :] + jnp.where(sub8 == SUBLANES - 1, c, pltpu.roll(mu, SUBLANES - 1, 0))
            return mu[0:1, :]

        carry[...] = lax.fori_loop(0, ng, group, carry[...])
        lam = l_sc[...]
        hext[pl.ds(0, HALO), :] = jnp.where(i < nt - 1, hp_ref[...], 0.0)
        hext[pl.ds(HALO, tb), :] = hv
        h_prev = hext[pl.ds(HALO - 1, tb), :]
        rv, igv, xcv = r_ref[...], ig_ref[...], xc_ref[...]
        sp = _softplus_neg(ap_ref[...])
        mult = jnp.sqrt(_one_minus_sq(-LRU_C * rv * sp, av))
        dla = lam * h_prev * av - lam * (igv * xcv) * (av * av) / mult
        du = lam * mult
        dpa = (dla * (-LRU_C) * sp) * rv * (1.0 - rv)
        dpx = (du * xcv) * igv * (1.0 - igv)
        dsp_ref[...] += jnp.sum(dla * (-LRU_C) * rv, axis=0, keepdims=True)
        dba_ref[...] += jnp.sum(dpa, axis=0, keepdims=True)
        dbx_ref[...] += jnp.sum(dpx, axis=0, keepdims=True)
        dpab, dpxb = dpa.astype(BF), dpx.astype(BF)
        dpa_ref[...] = dpab
        dpx_ref[...] = dpxb
        dxc = du * igv + _dot_nt(dpab, wa_ref[...]) + _dot_nt(dpxb, wx_ref[...])
        dext[pl.ds(0, tb), :] = dxc
        dext[pl.ds(tb, SUBLANES), :] = later[...]
        later[...] = dxc[0:SUBLANES, :]
        xext[pl.ds(0, HALO), :] = jnp.where(i < nt - 1, xp_ref[...], 0.0)
        xext[pl.ds(HALO, tb), :] = x_ref[...]
        dxb = cw_ref[CONV_WIDTH - 1:CONV_WIDTH, :] * dxc
        for k in range(CONV_WIDTH - 1):
            dxb = dxb + cw_ref[k:k + 1, :] * dext[pl.ds(CONV_WIDTH - 1 - k, tb), :]
        dxb_ref[...] = dxb.astype(BF)
        for k in range(CONV_WIDTH):
            dcw_ref[k:k + 1, :] += jnp.sum(dxc * xext[pl.ds(HALO - 3 + k, tb), :], axis=0, keepdims=True)
        dcb_ref[...] += jnp.sum(dxc, axis=0, keepdims=True)

        @pl.when(i == nt - 1)
        def _():
            dsp_ref[...] = dsp_ref[...] * (-_sigmoid(-ap_ref[...]))

    once = lambda rows, cols: pl.BlockSpec((rows, cols), lambda i: (0, 0), pipeline_mode=pl.Buffered(1))
    t0 = pl.BlockSpec((tb, R), lambda i: (nt - 1 - i, 0))
    prev = pl.BlockSpec((HALO, R), lambda i: (jnp.maximum((nt - 1 - i) * per - 1, 0), 0))
    return pl.pallas_call(
        body, name=name, grid=(nt,),
        in_specs=[pl.BlockSpec((tb, D), lambda i: (nt - 1 - i, 0)), t0, prev, t0, t0, prev, t0, t0, t0, t0,
                  once(R, R), once(R, R), once(R, D), _vec_spec(R, CONV_WIDTH), _vec_spec(R)],
        out_specs=[t0] * 4 + [_vec_spec(R)] * 4 + [_vec_spec(R, SUBLANES)],
        out_shape=[jax.ShapeDtypeStruct((T, R), BF)] * 4 + [jax.ShapeDtypeStruct((1, R), F32)] * 4
        + [jax.ShapeDtypeStruct((SUBLANES, R), F32)],
        scratch_shapes=[pltpu.VMEM((HALO + tb, R), F32), pltpu.VMEM((HALO + tb, R), F32),
                        pltpu.VMEM((tb + SUBLANES, R), F32), pltpu.VMEM((1, R), F32), pltpu.VMEM((SUBLANES, R), F32)]
        + [pltpu.VMEM((tb, R), F32)] * 4,
        compiler_params=_params("arbitrary"),
    )(dh, hs, hs, gb, xb, xb, a, r, ig, xc, wa, wx, wout, conv_w, a_param.reshape(1, R))


def _gate_spans(R):
    d = R // LRU_HEADS
    spans = [min((j * GATE_COLS // d) * d // LANES * LANES, R - GATE_SPAN) for j in range(R // GATE_COLS)]
    assert R % GATE_COLS == 0 and all(lo + GATE_SPAN >= (((j + 1) * GATE_COLS - 1) // d + 1) * d for j, lo in enumerate(spans))
    return spans


def _lru_gates_dw(xc, dpa, dpx, name):
    T, R = xc.shape
    tk = _tile(T, 1024, 16)
    spans = _gate_spans(R)
    nb = len(spans)

    def body(x_ref, a_ref, b_ref, o_ref):
        @pl.when(pl.program_id(0) == 0)
        def _():
            o_ref[...] = jnp.zeros_like(o_ref)

        for j, lo in enumerate(spans):
            xs = x_ref[:, pl.ds(lo, GATE_SPAN)].astype(BF)
            cols = pl.ds(j * GATE_COLS, GATE_COLS)
            o_ref[0, j] += _dot_tn(xs, a_ref[:, cols])
            o_ref[1, j] += _dot_tn(xs, b_ref[:, cols])

    row = _row_spec(tk, R)
    out = pl.pallas_call(
        body, name=name, grid=(T // tk,), in_specs=[row, row, row],
        out_specs=pl.BlockSpec((2, nb, GATE_SPAN, GATE_COLS), lambda i: (0, 0, 0, 0)),
        out_shape=jax.ShapeDtypeStruct((2, nb, GATE_SPAN, GATE_COLS), F32), compiler_params=_params("arbitrary"),
    )(xc, dpa, dpx)
    dense = jnp.zeros((2, R, R), F32)
    for j, lo in enumerate(spans):
        dense = dense.at[:, lo:lo + GATE_SPAN, j * GATE_COLS:(j + 1) * GATE_COLS].set(out[:, j])
    return dense[0], dense[1]


def _window_sums(e, n, back):
    out, s = [], e
    for k in (1, 2, 4, 8):
        s = s + pltpu.roll(s, k if back else n - k, 0)
        out.append(s)
    return out


def _pool_fwd(h, g, w, b, scale, name):
    T, D = h.shape
    G = len(POOL_WINDOWS)
    gd = D // G
    tb = _tile(T, TB_SEQ, HALO)
    per = tb // HALO

    def body(h_ref, hp_ref, g_ref, w_ref, b_ref, s_ref, o_ref, u_ref, yb_ref):
        i = pl.program_id(0)
        t = i * tb + lax.broadcasted_iota(jnp.int32, (tb, gd), 0) + 1
        hv = h_ref[...]
        xn = _rms_parts(hv, g_ref[...])[0]
        xp = jnp.where(i > 0, _rms_parts(hp_ref[...], g_ref[...])[0], 0.0)
        for k, win in enumerate(POOL_WINDOWS):
            cols = slice(k * gd, (k + 1) * gd)
            x = xn[:, cols]
            e = jnp.concatenate([xp[:, cols], x], axis=0)
            sw = _window_sums(e, HALO + tb, True)[k][HALO:, :]
            u = (sw / jnp.minimum(t, win).astype(F32) - x).astype(BF)
            yb = _dot(u, w_ref[k]) + b_ref[:, cols]
            u_ref[:, cols] = u
            yb_ref[:, cols] = yb
            o_ref[:, cols] = hv[:, cols] + yb * s_ref[:, cols]

    tile = _row_spec(tb, D)
    prev = pl.BlockSpec((HALO, D), lambda i: (jnp.maximum(i * per - 1, 0), 0))
    return pl.pallas_call(
        body, name=name, grid=(T // tb,),
        in_specs=[tile, prev, _vec_spec(D), pl.BlockSpec((G, gd, gd), lambda i: (0, 0, 0)), _vec_spec(D), _vec_spec(D)],
        out_specs=[tile, tile, tile],
        out_shape=[jax.ShapeDtypeStruct((T, D), F32), jax.ShapeDtypeStruct((T, D), BF), jax.ShapeDtypeStruct((T, D), F32)],
        compiler_params=_params("parallel"),
    )(h, h, g.reshape(1, D), w, b.reshape(1, D), scale.reshape(1, D))


def _pool_bwd(dm, u, yb, w, scale, name):
    T, D = dm.shape
    G = len(POOL_WINDOWS)
    gd = D // G
    tb = _tile(T, TB_SEQ, HALO)

    def body(d_ref, u_ref, yb_ref, w_ref, s_ref, du_ref, v_ref, dw_ref, db_ref, ds_ref):
        i = pl.program_id(0)

        @pl.when(i == 0)
        def _():
            dw_ref[...] = jnp.zeros_like(dw_ref)
            db_ref[...] = jnp.zeros_like(db_ref)
            ds_ref[...] = jnp.zeros_like(ds_ref)

        d, sc = d_ref[...], s_ref[...]
        ds_ref[...] += jnp.sum(d * yb_ref[...], axis=0, keepdims=True)
        db_ref[...] += jnp.sum(d * sc, axis=0, keepdims=True)
        t = i * tb + lax.broadcasted_iota(jnp.int32, (tb, gd), 0) + 1
        for g, win in enumerate(POOL_WINDOWS):
            cols = pl.ds(g * gd, gd)
            dy = (d_ref[:, cols] * s_ref[:, cols]).astype(BF)
            du = _dot_nt(dy, w_ref[g])
            dw_ref[g] += _dot_tn(u_ref[:, cols], dy)
            du_ref[:, cols] = du
            v_ref[:, cols] = du / jnp.minimum(t, win).astype(F32)

    tile = _row_spec(tb, D)
    return pl.pallas_call(
        body, name=name, grid=(T // tb,),
        in_specs=[tile, tile, tile, pl.BlockSpec((G, gd, gd), lambda i: (0, 0, 0)), _vec_spec(D)],
        out_specs=[tile, tile, pl.BlockSpec((G, gd, gd), lambda i: (0, 0, 0)), _vec_spec(D), _vec_spec(D)],
        out_shape=[jax.ShapeDtypeStruct((T, D), F32), jax.ShapeDtypeStruct((T, D), F32),
                   jax.ShapeDtypeStruct((G, gd, gd), F32), jax.ShapeDtypeStruct((1, D), F32),
                   jax.ShapeDtypeStruct((1, D), F32)],
        compiler_params=_params("arbitrary"),
    )(dm, u, yb, w, scale.reshape(1, D))


def _pool_bwd_win(v, du, h, g, dh, name):
    T, D = v.shape
    G = len(POOL_WINDOWS)
    gd = D // G
    tb = _tile(T, TB_SEQ, HALO)
    per = tb // HALO
    nt = T // tb

    def body(v_ref, vn_ref, du_ref, h_ref, g_ref, dh_ref, o_ref, dg_ref):
        i = pl.program_id(0)

        @pl.when(i == 0)
        def _():
            dg_ref[...] = jnp.zeros_like(dg_ref)

        parts = []
        for k in range(G):
            cols = pl.ds(k * gd, gd)
            e = jnp.concatenate([v_ref[:, cols], jnp.where(i < nt - 1, vn_ref[:, cols], 0.0)], axis=0)
            parts.append(_window_sums(e, tb + HALO, False)[k][:tb, :] - du_ref[:, cols])
        dx, dg = _rms_bwd_parts(h_ref[...], g_ref[...], jnp.concatenate(parts, axis=1))
        o_ref[...] = dh_ref[...] + dx
        dg_ref[...] += dg

    tile = _row_spec(tb, D)
    nxt = pl.BlockSpec((HALO, D), lambda i: (jnp.minimum((i + 1) * per, T // HALO - 1), 0))
    return pl.pallas_call(
        body, name=name, grid=(nt,), in_specs=[tile, nxt, tile, tile, _vec_spec(D), tile], out_specs=[tile, _vec_spec(D)],
        out_shape=[jax.ShapeDtypeStruct((T, D), F32), jax.ShapeDtypeStruct((1, D), F32)],
        compiler_params=_params("arbitrary"),
    )(v, v, du, h, g.reshape(1, D), dh)


def _adamw(w, g, m, v, name):
    shape = w.shape
    cols = shape[-1] if w.ndim > 1 else shape[0]
    rows = w.size // cols
    tr = _tile(rows, TR_ADAM, SUBLANES)
    c1, c2 = 1.0 / (1.0 - ADAM_B1 ** ADAM_STEP), 1.0 / (1.0 - ADAM_B2 ** ADAM_STEP)

    def body(w_ref, g_ref, m_ref, v_ref, d_ref, mo_ref, vo_ref):
        gv = g_ref[...]
        mn = ADAM_B1 * m_ref[...] + (1.0 - ADAM_B1) * gv
        vn = ADAM_B2 * v_ref[...] + (1.0 - ADAM_B2) * (gv * gv)
        d_ref[...] = -ADAM_LR * ((mn * c1) / (jnp.sqrt(vn * c2) + ADAM_EPS) + ADAM_WD * w_ref[...])
        mo_ref[...] = mn
        vo_ref[...] = vn

    spec = _row_spec(tr, cols)
    outs = pl.pallas_call(
        body, name=name, grid=(rows // tr,), in_specs=[spec] * 4, out_specs=[spec] * 3,
        out_shape=[jax.ShapeDtypeStruct((rows, cols), F32)] * 3, compiler_params=_params("parallel"),
    )(*[t.reshape(rows, cols) for t in (w, g, m, v)])
    return [o.reshape(shape) for o in outs]


def _sum_devices(parts, name):
    n, rows, cols = parts.shape
    tr = _tile(rows, 1024, SUBLANES)

    def body(p_ref, o_ref):
        acc = p_ref[0].astype(F32)
        for k in range(1, n):
            acc = acc + p_ref[k].astype(F32)
        o_ref[...] = acc

    return pl.pallas_call(
        body, name=name, grid=(rows // tr,), in_specs=[pl.BlockSpec((n, tr, cols), lambda i: (0, i, 0))],
        out_specs=_row_spec(tr, cols), out_shape=jax.ShapeDtypeStruct((rows, cols), F32),
        compiler_params=_params("parallel"),
    )(parts)


def _position():
    return lax.axis_index("x"), lax.axis_index("y"), lax.axis_index("c")


def _gathered_shapes(blocks):
    return [jax.ShapeDtypeStruct((b.shape[0], N_DEV * b.shape[1], b.shape[2]), b.dtype) for b in blocks]


def _gather_sems(ng):
    return [pltpu.SemaphoreType.DMA((ng, 7)), pltpu.SemaphoreType.DMA((ng, 7)), pltpu.SemaphoreType.DMA((ng,))] if ng else []


def _gather_plan(blocks, srcs, outs, send_sems, recv_sems, local_sems):
    ng = len(blocks)
    x, y, c = _position()
    me, sibling = (x, y, c), (x, y, 1 - c)
    chips = [(1 - x, y), (x, 1 - y), (1 - x, 1 - y)]

    def rows(g, px, py, pc):
        r = blocks[g].shape[1]
        return outs[g].at[:, pl.ds((4 * px + 2 * py + pc) * r, r), :]

    def copy(g, k, block, to, src=None):
        return pltpu.make_async_remote_copy(
            src_ref=rows(g, *block) if src is None else src, dst_ref=rows(g, *block),
            send_sem=send_sems.at[g, k], recv_sem=recv_sems.at[g, k], device_id=to, device_id_type=MESH)

    def mine(g):
        return pltpu.make_async_copy(srcs[g], rows(g, *me), local_sems.at[g])

    def first(g):
        return [copy(g, 0, me, sibling, src=srcs[g])] + [copy(g, 1 + j, me, (*chip, c), src=srcs[g])
                                                         for j, chip in enumerate(chips)]

    def passed(g):
        return [copy(g, 4 + j, (*chip, c), sibling) for j, chip in enumerate(chips)]

    def start():
        for g in range(ng):
            mine(g).start()
            for cp in first(g):
                cp.start()

    def forward():
        for j, chip in enumerate(chips):
            for g in range(ng):
                copy(g, 1 + j, (*chip, c), me).wait_recv()
                copy(g, 4 + j, (*chip, c), sibling).start()

    def finish():
        for g in range(ng):
            copy(g, 0, sibling, me).wait_recv()
            for j, chip in enumerate(chips):
                copy(g, 4 + j, (*chip, 1 - c), me).wait_recv()
            for cp in first(g) + passed(g):
                cp.wait_send()
            mine(g).wait()

    return start, forward, finish


def _all_gather(blocks, name):
    ng = len(blocks)

    def body(*refs):
        start, forward, finish = _gather_plan(blocks, refs[:ng], refs[ng:2 * ng], *refs[2 * ng:])
        start()
        forward()
        finish()

    hbm = pl.BlockSpec(memory_space=pl.ANY)
    return pl.pallas_call(
        body, name=name, in_specs=[hbm] * ng, out_specs=[hbm] * ng, out_shape=_gathered_shapes(blocks),
        scratch_shapes=_gather_sems(ng),
    )(*blocks)


FLIPS = ((0, 0, 1), (1, 0, 0), (0, 1, 0), (1, 1, 0), (1, 0, 1), (0, 1, 1), (1, 1, 1))


def _piece_rows(piece):
    arr, m = piece
    return arr.shape[0] if m is None else 1


def _scattered_shapes(pieces):
    return [jax.ShapeDtypeStruct((N_DEV, _piece_rows(p), p[0].shape[1] // N_DEV, p[0].shape[2]), p[0].dtype)
            for p in pieces]


def _scatter_sems(ng):
    n = len(FLIPS)
    return [pltpu.SemaphoreType.DMA((ng, n)), pltpu.SemaphoreType.DMA((ng, n)), pltpu.SemaphoreType.DMA((ng,))] if ng else []


def _scatter_plan(pieces, srcs, outs, send_sems, recv_sems, local_sems):
    x, y, c = _position()

    def block(g, tx, ty, tc):
        arr, m = pieces[g]
        r = arr.shape[1] // N_DEV
        lead = slice(None) if m is None else pl.ds(m, 1)
        return srcs[g].at[lead, pl.ds((4 * tx + 2 * ty + tc) * r, r), :]

    def copies(g):
        out = []
        for k, (fx, fy, fc) in enumerate(FLIPS):
            tx, ty, tc = (1 - x if fx else x), (1 - y if fy else y), (1 - c if fc else c)
            out.append(pltpu.make_async_remote_copy(
                src_ref=block(g, tx, ty, tc), dst_ref=outs[g].at[k], send_sem=send_sems.at[g, k],
                recv_sem=recv_sems.at[g, k], device_id=(tx, ty, tc), device_id_type=MESH))
        return out

    def mine(g):
        return pltpu.make_async_copy(block(g, x, y, c), outs[g].at[len(FLIPS)], local_sems.at[g])

    def start():
        for g in range(len(pieces)):
            mine(g).start()
            for cp in copies(g):
                cp.start()

    def finish():
        for g in range(len(pieces)):
            for cp in copies(g):
                cp.wait()
            mine(g).wait()

    return start, finish


def _scatter_and_gather(pieces, blocks, name):
    n_p, n_b = len(pieces), len(blocks)

    def body(*refs):
        ins, outs, sems = refs[:n_p + n_b], refs[n_p + n_b:2 * (n_p + n_b)], refs[2 * (n_p + n_b):]
        s_start, s_finish = _scatter_plan(pieces, ins[:n_p], outs[:n_p], *sems[:3])
        g_start, g_forward, g_finish = _gather_plan(blocks, ins[n_p:], outs[n_p:], *sems[3:])
        s_start()
        g_start()
        g_forward()
        g_finish()
        s_finish()

    hbm = pl.BlockSpec(memory_space=pl.ANY)
    outs = pl.pallas_call(
        body, name=name, in_specs=[hbm] * (n_p + n_b), out_specs=[hbm] * (n_p + n_b),
        out_shape=_scattered_shapes(pieces) + _gathered_shapes(blocks),
        scratch_shapes=_scatter_sems(n_p) + _gather_sems(n_b),
    )(*[p[0] for p in pieces], *blocks)
    return list(outs[:n_p]), list(outs[n_p:])


def _scatter_sum(recv, name):
    _, n, r, c = recv.shape

    def body(r_ref, o_ref):
        acc = r_ref[len(FLIPS)].astype(F32)
        for k in range(len(FLIPS)):
            acc = acc + r_ref[k].astype(F32)
        o_ref[...] = acc

    return pl.pallas_call(
        body, name=name, grid=(n,), in_specs=[pl.BlockSpec((N_DEV, None, r, c), lambda i: (0, i, 0, 0))],
        out_specs=pl.BlockSpec((None, r, c), lambda i: (i, 0, 0)),
        out_shape=jax.ShapeDtypeStruct((n, r, c), F32), compiler_params=_params("parallel"),
    )(recv)


def _block_diag(w):
    H, d, _ = w.shape
    return (jnp.eye(H, dtype=w.dtype)[:, None, :, None] * w[:, :, None, :]).reshape(H * d, H * d)


def _diag_blocks(dense, H):
    d = dense.shape[0] // H
    return jnp.stack([dense[i * d:(i + 1) * d, i * d:(i + 1) * d] for i in range(H)])


def _local_step(x, p, tgt, W, blocks=None):
    dist = blocks is not None
    L = p.shape[0]
    W = dict(W)

    def gathering(keys):
        return [k for k in keys if k not in W] if dist else []

    def ffn_fwd(h, g, i, f, during_act, during_out):
        w = W[("ffn", i, f)]
        keys = gathering(during_act)
        a, b, s, n, got = _ffn_fwd_act(h, g, w, f"ffn{f}_fwd_act_{i}", gather=[blocks[k] for k in keys])
        W.update(zip(keys, got))
        keys = gathering(during_out)
        h, got = _ffn_fwd_out(s, w, h, f"ffn{f}_fwd_out_{i}", gather=[blocks[k] for k in keys])
        W.update(zip(keys, got))
        return (a, b, s, n), h

    saved = []
    h = x
    for i in range(L):
        j = i // 2
        lru = i % 2 == 0
        s = {"h0": h}
        mixer = [("lru_in", j), ("lru_out", j)] if lru else [("pool_w", j)]
        s["ffn1"], h = ffn_fwd(h, W["ffn1_norm"][i], i, 1, [("ffn", i, 2)], mixer)
        s["h1"] = h
        if lru:
            hn, gb, xb = _lru_in(h, W["mix_norm"][i], W[("lru_in", j)][0], f"lru_in_{i}")
            wa, wx = _block_diag(W["lru_w_a"][j]).astype(BF), _block_diag(W["lru_w_x"][j]).astype(BF)
            h, xc, r, ig, a, hs, y = _lru_fwd(gb, xb, W["lru_conv_w"][j], W["lru_conv_b"][j], wa, wx, W["lru_b_a"][j],
                                              W["lru_b_x"][j], W["lru_a_param"][j], W[("lru_out", j)][0], h, f"lru_fwd_{i}")
            s.update(hn=hn, gb=gb, xb=xb, wa=wa, wx=wx, xc=xc, r=r, ig=ig, a=a, hs=hs, y=y)
        else:
            h, s["u"], s["yb"] = _pool_fwd(h, W["mix_norm"][i], W[("pool_w", j)], W["pool_b"][j], W["pool_scale"][j],
                                           f"pool_fwd_{i}")
        s["h2"] = h
        s["ffn2"], h = ffn_fwd(h, W["ffn2_norm"][i], i, 2, [("ffn", i + 1, 1)] if i + 1 < L else [],
                               [("ple_gate", i), ("ple_proj", i)])
        s["h3"] = h
        h, s["n4"], s["gate"], s["pp"] = _ple_fwd(h, W["ple_norm"][i], W[("ple_gate", i)][0], W[("ple_proj", i)][0], p[i],
                                                  f"ple_fwd_{i}")
        saved.append(s)

    loss, dh, d_final = _loss_head(h, W["final_norm"], tgt)

    big, recv = {}, {}
    n_lru, n_pool = L // 2 + L % 2, L // 2
    small = {k: [None] * L for k in ("ffn1_norm", "mix_norm", "ffn2_norm", "ple_norm")}
    for k in ("lru_conv_w", "lru_conv_b", "lru_w_a", "lru_b_a", "lru_w_x", "lru_b_x", "lru_a_param"):
        small[k] = [None] * n_lru
    for k in ("pool_b", "pool_scale"):
        small[k] = [None] * n_pool

    def scattering(pieces):
        return [(k, m) for k, m in pieces if k in big] if dist else []

    def ffn_bwd(dh, h_in, g, acts, i, f, during):
        key, w = ("ffn", i, f), W[("ffn", i, f)]
        a, b, sv, n = acts
        out = [scattering(d) for d in during]
        sent = [[(big[k], m) for k, m in o] for o in out]
        da, db, dhb, got0 = _ffn_bwd_act(dh, a, b, w, f"ffn{f}_bwd_act_{i}", scatter=sent[0])
        big[key], got1 = _ffn_bwd_w(da, db, sv, n, dhb, f"ffn{f}_dw_{i}", scatter=sent[1])
        out.append(scattering([(key, 0)] + ([(key, 1)] if (i, f) == (0, 1) else [])))
        dh, dg, got2 = _ffn_bwd_in(da, db, w, h_in, g, dh, f"ffn{f}_bwd_in_{i}", scatter=[(big[k], m) for k, m in out[2]])
        for o, got in zip(out, (got0, got1, got2)):
            recv.update(zip(o, got))
        return dh, dg

    for i in reversed(range(L)):
        j = i // 2
        lru = i % 2 == 0
        s = saved[i]
        dh, dz, dpp, dg = _ple_bwd(dh, s["gate"], s["pp"], s["h3"], W["ple_norm"][i], W[("ple_gate", i)][0],
                                   f"ple_bwd_{i}")
        big[("ple_gate", i)] = _mm(s["n4"], dz, "tn", f"ple_gate_dw_{i}", out_dtype=BF)[None]
        big[("ple_proj", i)] = _mm(dpp, p[i], "tn", f"ple_proj_dw_{i}", out_dtype=BF)[None]
        small["ple_norm"][i] = dg[0]
        above = ("ffn", i + 1, 1)
        dh, dg = ffn_bwd(dh, s["h2"], W["ffn2_norm"][i], s["ffn2"], i, 2, [
            [(above, 1)], [(above, 2), (("ple_gate", i), None), (("ple_proj", i), None)]])
        small["ffn2_norm"][i] = dg[0]
        if lru:
            big[("lru_out", j)] = _mm(s["y"], dh, "tn", f"lru_out_dw_{i}", out_dtype=BF)[None]
            dgb, dxb, dpa, dpx, dsp, dba, dbx, dcb, dcw = _lru_bwd(
                dh, s["hs"], s["gb"], s["xb"], s["a"], s["r"], s["ig"], s["xc"], s["wa"], s["wx"], W[("lru_out", j)][0],
                W["lru_conv_w"][j], W["lru_a_param"][j], f"lru_bwd_{i}")
            small["lru_a_param"][j], small["lru_b_a"][j], small["lru_b_x"][j] = dsp[0], dba[0], dbx[0]
            dwa, dwx = _lru_gates_dw(s["xc"], dpa, dpx, f"lru_gates_dw_{i}")
            small["lru_w_a"][j], small["lru_w_x"][j] = _diag_blocks(dwa, LRU_HEADS), _diag_blocks(dwx, LRU_HEADS)
            small["lru_conv_w"][j], small["lru_conv_b"][j] = dcw[:CONV_WIDTH], dcb[0]
            big[("lru_in", j)] = jnp.concatenate([_mm(dgb, s["hn"], "tn", f"lru_in_dw_g_{i}", out_dtype=BF),
                                                  _mm(dxb, s["hn"], "tn", f"lru_in_dw_x_{i}", out_dtype=BF)])[None]
            dh, dg = _lru_in_bwd(dgb, dxb, W[("lru_in", j)][0], s["h1"], W["mix_norm"][i], dh, f"lru_in_bwd_{i}")
            mixer = [("lru_in", j), ("lru_out", j)]
        else:
            du, v, dw, dbp, dsc = _pool_bwd(dh, s["u"], s["yb"], W[("pool_w", j)], W["pool_scale"][j], f"pool_bwd_{i}")
            big[("pool_w", j)] = dw.astype(BF)
            small["pool_b"][j], small["pool_scale"][j] = dbp[0], dsc[0]
            dh, dg = _pool_bwd_win(v, du, s["h1"], W["mix_norm"][i], dh, f"pool_bwd_win_{i}")
            mixer = [("pool_w", j)]
        small["mix_norm"][i] = dg[0]
        second = ("ffn", i, 2)
        dh, dg = ffn_bwd(dh, s["h0"], W["ffn1_norm"][i], s["ffn1"], i, 1, [
            [(second, 1)], [(second, 2)] + [(k, None) for k in mixer]])
        small["ffn1_norm"][i] = dg[0]

    small = {k: jnp.stack(v) for k, v in small.items()}
    small["final_norm"] = d_final[0]
    return loss, dh, big, recv, small


SMALL_SHARDED = ("pool_b", "pool_scale", "lru_conv_w")
SMALL = ("ffn1_norm", "mix_norm", "ffn2_norm", "ple_norm", "final_norm", "lru_conv_b", "lru_w_a", "lru_b_a",
         "lru_w_x", "lru_b_x", "lru_a_param", "pool_b", "pool_scale", "lru_conv_w")


def _pack_big(w):
    t = lambda a: jnp.swapaxes(a, -1, -2)
    out = {}
    for i in range(w["ffn1_norm"].shape[0]):
        for f in (1, 2):
            out[("ffn", i, f)] = jnp.stack([t(w[f"ffn{f}_w_gate"][i]), t(w[f"ffn{f}_w_up"][i]), w[f"ffn{f}_w_down"][i]])
        out[("ple_gate", i)], out[("ple_proj", i)] = w["ple_w_gate"][i][None], t(w["ple_w_proj"][i])[None]
    for j in range(w["lru_w_in"].shape[0]):
        out[("lru_in", j)], out[("lru_out", j)] = t(w["lru_w_in"][j])[None], w["lru_w_out"][j][None]
    for j in range(w["pool_w"].shape[0]):
        out[("pool_w", j)] = w["pool_w"][j]
    return out


def _unpack_big(b, L):
    t = lambda a: jnp.swapaxes(a, -1, -2)
    n_lru, n_pool = L // 2 + L % 2, L // 2
    out = {"lru_w_in": jnp.stack([t(b[("lru_in", j)][0]) for j in range(n_lru)]),
           "lru_w_out": jnp.stack([b[("lru_out", j)][0] for j in range(n_lru)]),
           "pool_w": jnp.stack([b[("pool_w", j)] for j in range(n_pool)]),
           "ple_w_gate": jnp.stack([b[("ple_gate", i)][0] for i in range(L)]),
           "ple_w_proj": jnp.stack([t(b[("ple_proj", i)][0]) for i in range(L)])}
    for f in (1, 2):
        out[f"ffn{f}_w_gate"] = jnp.stack([t(b[("ffn", i, f)][0]) for i in range(L)])
        out[f"ffn{f}_w_up"] = jnp.stack([t(b[("ffn", i, f)][1]) for i in range(L)])
        out[f"ffn{f}_w_down"] = jnp.stack([b[("ffn", i, f)][2] for i in range(L)])
    return out


def _flatten(parts, names, rows_of=LANES):
    flat = jnp.concatenate([parts[k].reshape(-1) for k in names])
    pad = (-flat.size) % (16 * rows_of)
    return jnp.pad(flat, (0, pad)).reshape(1, -1, rows_of)


def _unflatten(flat, like, names):
    out, o = {}, 0
    flat = flat.reshape(-1)
    for k in names:
        n = like[k].size
        out[k] = flat[o:o + n].reshape(like[k].shape)
        o += n
    return out


def kernel(x, p, ffn1_norm, ffn1_w_gate, ffn1_w_up, ffn1_w_down, mix_norm, lru_w_in, lru_conv_w, lru_conv_b, lru_w_a, lru_b_a, lru_w_x, lru_b_x, lru_a_param, lru_w_out, pool_w, pool_b, pool_scale, ffn2_norm, ffn2_w_gate, ffn2_w_up, ffn2_w_down, ple_norm, ple_w_gate, ple_w_proj, final_norm, loss_target, m_ffn1_norm, m_ffn1_w_gate, m_ffn1_w_up, m_ffn1_w_down, m_mix_norm, m_lru_w_in, m_lru_conv_w, m_lru_conv_b, m_lru_w_a, m_lru_b_a, m_lru_w_x, m_lru_b_x, m_lru_a_param, m_lru_w_out, m_pool_w, m_pool_b, m_pool_scale, m_ffn2_norm, m_ffn2_w_gate, m_ffn2_w_up, m_ffn2_w_down, m_ple_norm, m_ple_w_gate, m_ple_w_proj, m_final_norm, v_ffn1_norm, v_ffn1_w_gate, v_ffn1_w_up, v_ffn1_w_down, v_mix_norm, v_lru_w_in, v_lru_conv_w, v_lru_conv_b, v_lru_w_a, v_lru_b_a, v_lru_w_x, v_lru_b_x, v_lru_a_param, v_lru_w_out, v_pool_w, v_pool_b, v_pool_scale, v_ffn2_norm, v_ffn2_w_gate, v_ffn2_w_up, v_ffn2_w_down, v_ple_norm, v_ple_w_gate, v_ple_w_proj, v_final_norm):
    names = ["ffn1_norm", "ffn1_w_gate", "ffn1_w_up", "ffn1_w_down", "mix_norm", "lru_w_in", "lru_conv_w", "lru_conv_b",
             "lru_w_a", "lru_b_a", "lru_w_x", "lru_b_x", "lru_a_param", "lru_w_out", "pool_w", "pool_b", "pool_scale",
             "ffn2_norm", "ffn2_w_gate", "ffn2_w_up", "ffn2_w_down", "ple_norm", "ple_w_gate", "ple_w_proj", "final_norm"]
    w = dict(zip(names, [ffn1_norm, ffn1_w_gate, ffn1_w_up, ffn1_w_down, mix_norm, lru_w_in, lru_conv_w, lru_conv_b, lru_w_a, lru_b_a, lru_w_x, lru_b_x, lru_a_param, lru_w_out, pool_w, pool_b, pool_scale, ffn2_norm, ffn2_w_gate, ffn2_w_up, ffn2_w_down, ple_norm, ple_w_gate, ple_w_proj, final_norm]))
    m = dict(zip(names, [m_ffn1_norm, m_ffn1_w_gate, m_ffn1_w_up, m_ffn1_w_down, m_mix_norm, m_lru_w_in, m_lru_conv_w, m_lru_conv_b, m_lru_w_a, m_lru_b_a, m_lru_w_x, m_lru_b_x, m_lru_a_param, m_lru_w_out, m_pool_w, m_pool_b, m_pool_scale, m_ffn2_norm, m_ffn2_w_gate, m_ffn2_w_up, m_ffn2_w_down, m_ple_norm, m_ple_w_gate, m_ple_w_proj, m_final_norm]))
    v = dict(zip(names, [v_ffn1_norm, v_ffn1_w_gate, v_ffn1_w_up, v_ffn1_w_down, v_mix_norm, v_lru_w_in, v_lru_conv_w, v_lru_conv_b, v_lru_w_a, v_lru_b_a, v_lru_w_x, v_lru_b_x, v_lru_a_param, v_lru_w_out, v_pool_w, v_pool_b, v_pool_scale, v_ffn2_norm, v_ffn2_w_gate, v_ffn2_w_up, v_ffn2_w_down, v_ple_norm, v_ple_w_gate, v_ple_w_proj, v_final_norm]))
    L = p.shape[0]
    px, py, pc = _position()
    me = 4 * px + 2 * py + pc

    blocks = {k: b.astype(BF) for k, b in _pack_big(w).items()}
    first = ("ffn", 0, 1)
    got, small_blocks = _all_gather([blocks[first], _flatten(w, SMALL_SHARDED)], "gather_first")
    W = {first: got}
    per_dev = small_blocks.reshape(N_DEV, -1)
    shards = [_unflatten(per_dev[k], w, SMALL_SHARDED) for k in range(N_DEV)]
    for k in SMALL:
        W[k] = jnp.concatenate([s[k] for s in shards], axis=-1) if k in SMALL_SHARDED else w[k]

    loss, dx, big, recv, small = _local_step(x[0], p[:, 0], loss_target[0], W, blocks)

    last = [(k, m) for k in big if (k, None) not in recv for m in range(big[k].shape[0]) if (k, m) not in recv]
    got, (parts,) = _scatter_and_gather([(big[k], m) for k, m in last], [_flatten(small, SMALL).astype(BF)],
                                        "scatter_last_gather_small")
    recv.update(zip(last, got))

    def total(k):
        tag = "sum_" + "_".join(map(str, k))
        if (k, None) in recv:
            return _scatter_sum(recv[(k, None)], tag)
        return jnp.concatenate([_scatter_sum(recv[(k, m)], f"{tag}_{m}") for m in range(big[k].shape[0])])

    grads = _unpack_big({k: total(k) for k in big}, L)
    total_small = _sum_devices(parts.reshape(N_DEV, -1, LANES), "sum_small_grads")
    full = _unflatten(total_small, {k: W[k] for k in SMALL}, SMALL)
    for k in SMALL:
        if k in SMALL_SHARDED:
            n = w[k].shape[-1]
            grads[k] = lax.dynamic_slice_in_dim(full[k], me * n, n, axis=-1)
        else:
            grads[k] = full[k]

    delta, new_m, new_v = {}, {}, {}
    for k in names:
        delta[k], new_m[k], new_v[k] = _adamw(w[k], grads[k], m[k], v[k], f"adamw_{k}")
    total_loss = lax.psum(loss[0, 0], ("x", "y", "c"))
    return (total_loss, dx[None], *[grads[k] for k in names], *[delta[k] for k in names],
            *[new_m[k] for k in names], *[new_v[k] for k in names])
```

```python
import functools

import jax
import jax.numpy as jnp
from jax import lax
from jax.experimental import pallas as pl
from jax.experimental.pallas import tpu as pltpu

F32 = jnp.float32
BF = jnp.bfloat16
MESH = pl.DeviceIdType.MESH

RMS_EPS = 1e-6
LRU_C = 8.0
LRU_HEADS = 16
CONV_WIDTH = 4
POOL_WINDOWS = (2, 4, 8, 16)
ADAM_LR, ADAM_B1, ADAM_B2, ADAM_EPS, ADAM_WD, ADAM_STEP = 0.001, 0.9, 0.999, 1e-08, 0.01, 10

N_DEV = 8
LANES = 128
SUBLANES = 8
GATE_COLS, GATE_SPAN = 256, 512
HALO = 16
VMEM_LIMIT = 56 * 1024 * 1024

TM_FFN = 1024
TM_FFN_ACT = 2048
TM_FFN_IN = 512
TF_FFN = 256
TF_FFN_WG = 1408
TK_FFN_WG = 512
TB_SEQ = 256
TM_EW = 512
TM_MM, TN_MM, TK_MM = 1024, 512, 1024
TR_ADAM = 512


def _tile(n, pref, align):
    if n <= pref:
        return n
    t = (pref // align) * align
    while t >= align:
        if n % t == 0:
            return t
        t -= align
    raise ValueError(f"no tile for {n} (pref {pref}, align {align})")


def _params(*sem):
    return pltpu.CompilerParams(dimension_semantics=sem, vmem_limit_bytes=VMEM_LIMIT)


def _dot(a, b):
    return lax.dot_general(a, b, (((1,), (0,)), ((), ())), preferred_element_type=F32)


def _dot_nt(a, b):
    return lax.dot_general(a, b, (((1,), (1,)), ((), ())), preferred_element_type=F32)


def _dot_tn(a, b):
    return lax.dot_general(a, b, (((0,), (0,)), ((), ())), preferred_element_type=F32)


def _sigmoid(x):
    return 0.5 + 0.5 * jnp.tanh(0.5 * x)


def _gelu_parts(x):
    k0, k1 = 0.7978845608028654, 0.044715
    t = jnp.tanh(k0 * (x + k1 * x * x * x))
    g = 0.5 * x * (1.0 + t)
    dg = 0.5 * (1.0 + t) + 0.5 * x * (1.0 - t * t) * k0 * (1.0 + 3.0 * k1 * x * x)
    return g, dg


def _one_minus_sq(la, a):
    return jnp.tanh(-la) * (1.0 + a * a)


def _softplus_neg(l):
    u = jnp.exp(-jnp.abs(l))
    w = 1.0 + u
    log1p = jnp.where(w == 1.0, u, jnp.log(w) * (u / jnp.where(w == 1.0, 1.0, w - 1.0)))
    return jnp.maximum(-l, 0.0) + log1p


def _rms_parts(x, g):
    r = lax.rsqrt(jnp.mean(x * x, axis=-1, keepdims=True) + RMS_EPS)
    nhat = x * r
    return nhat * g, nhat, r


def _rms_bwd_parts(x, g, dn):
    _, nhat, r = _rms_parts(x, g)
    u = dn * g
    dx = r * (u - nhat * jnp.mean(u * nhat, axis=-1, keepdims=True))
    return dx, jnp.sum(dn * nhat, axis=0, keepdims=True)


def _row_spec(tm, d, single=False):
    if single:
        return pl.BlockSpec((tm, d), lambda i, *_: (i, 0), pipeline_mode=pl.Buffered(1))
    return pl.BlockSpec((tm, d), lambda i, *_: (i, 0))


def _vec_spec(d, rows=1):
    return pl.BlockSpec((rows, d), lambda *_: (0, 0))


def _mm(x, w, mode, name, out_dtype=F32, res=None, alpha=1.0, tm=None, tn=None, tk=None):
    if mode == "nn":
        (M, K), (_, N) = x.shape, w.shape
    elif mode == "nt":
        (M, K), (N, _) = x.shape, w.shape
    else:
        (K, M), (_, N) = x.shape, w.shape
    tm = _tile(M, tm or TM_MM, LANES if mode == "tn" else SUBLANES)
    tn = _tile(N, tn or TN_MM, LANES)
    tk = _tile(K, tk or TK_MM, LANES if mode != "tn" else 16)
    nk = K // tk
    dot = {"nn": _dot, "nt": _dot_nt, "tn": _dot_tn}[mode]

    def body(*refs):
        if res is None:
            x_ref, w_ref, o_ref, acc = refs
        else:
            x_ref, w_ref, r_ref, o_ref, acc = refs
        k = pl.program_id(2)

        @pl.when(k == 0)
        def _():
            acc[...] = jnp.zeros_like(acc)

        acc[...] += dot(x_ref[...].astype(BF), w_ref[...].astype(BF))

        @pl.when(k == nk - 1)
        def _():
            r = acc[...] if alpha == 1.0 else acc[...] * alpha
            if res is not None:
                r = r_ref[...] + r
            o_ref[...] = r.astype(out_dtype)

    if mode == "nn":
        specs = [pl.BlockSpec((tm, tk), lambda i, j, k: (i, k)), pl.BlockSpec((tk, tn), lambda i, j, k: (k, j))]
    elif mode == "nt":
        specs = [pl.BlockSpec((tm, tk), lambda i, j, k: (i, k)), pl.BlockSpec((tn, tk), lambda i, j, k: (j, k))]
    else:
        specs = [pl.BlockSpec((tk, tm), lambda i, j, k: (k, i)), pl.BlockSpec((tk, tn), lambda i, j, k: (k, j))]
    args = [x, w]
    if res is not None:
        specs.append(pl.BlockSpec((tm, tn), lambda i, j, k: (i, j)))
        args.append(res)
    return pl.pallas_call(
        body, name=name, grid=(M // tm, N // tn, nk), in_specs=specs,
        out_specs=pl.BlockSpec((tm, tn), lambda i, j, k: (i, j)),
        out_shape=jax.ShapeDtypeStruct((M, N), out_dtype),
        scratch_shapes=[pltpu.VMEM((tm, tn), F32)],
        compiler_params=_params("parallel", "parallel", "arbitrary"),
    )(*args)


def _loss_head(h, g, tgt):
    T, D = h.shape
    tm = _tile(T, TM_EW, 16)

    def body(h_ref, g_ref, t_ref, loss_ref, dh_ref, dg_ref):
        @pl.when(pl.program_id(0) == 0)
        def _():
            dg_ref[...] = jnp.zeros_like(dg_ref)
            loss_ref[...] = jnp.zeros_like(loss_ref)

        x, gg = h_ref[...], g_ref[...]
        y = _rms_parts(x, gg)[0]
        e = y - t_ref[...]
        part = jnp.sum(jnp.sum(e * e, axis=0, keepdims=True), axis=1, keepdims=True) * (0.5 / D)
        loss_ref[...] += jnp.broadcast_to(part, loss_ref.shape)
        dx, dg = _rms_bwd_parts(x, gg, e * (1.0 / D))
        dh_ref[...] = dx
        dg_ref[...] += dg

    return pl.pallas_call(
        body, name="loss_head", grid=(T // tm,),
        in_specs=[_row_spec(tm, D), _vec_spec(D), _row_spec(tm, D)],
        out_specs=[_vec_spec(LANES), _row_spec(tm, D), _vec_spec(D)],
        out_shape=[jax.ShapeDtypeStruct((1, LANES), F32), jax.ShapeDtypeStruct((T, D), F32),
                   jax.ShapeDtypeStruct((1, D), F32)],
        compiler_params=_params("arbitrary"),
    )(h, g.reshape(1, D), tgt)


def _carry(plan, first, mid, last):
    pl.when(first)(plan[0])
    if len(plan) == 3:
        pl.when(mid)(plan[1])
    pl.when(last)(plan[-1])


def _ffn_fwd_act(h, g, wffn, name, gather=()):
    T, D = h.shape
    F = wffn.shape[1]
    tm, tf = _tile(T, TM_FFN_ACT, 16), _tile(F, TF_FFN, LANES)
    ni, nf, ng = T // tm, F // tf, len(gather)

    def body(*refs):
        h_ref, g_ref, wg_ref, wu_ref = refs[:4]
        srcs, (a_ref, b_ref, s_ref, n_ref), outs = refs[4:4 + ng], refs[4 + ng:8 + ng], refs[8 + ng:8 + 2 * ng]
        i, j = pl.program_id(0), pl.program_id(1)
        if ng:
            _carry(_gather_plan(gather, srcs, outs, *refs[8 + 2 * ng:]), jnp.logical_and(i == 0, j == 0),
                   jnp.logical_and(i == (3 * ni) // 4, j == 0), jnp.logical_and(i == ni - 1, j == nf - 1))

        @pl.when(j == 0)
        def _():
            n_ref[...] = _rms_parts(h_ref[...], g_ref[...])[0].astype(BF)

        n = n_ref[...]
        a = _dot_nt(n, wg_ref[...])
        b = _dot_nt(n, wu_ref[...])
        a_ref[...] = a.astype(BF)
        b_ref[...] = b.astype(BF)
        s_ref[...] = (a * _sigmoid(a) * b).astype(BF)

    tile = pl.BlockSpec((tm, tf), lambda i, j: (i, j))
    w = [pl.BlockSpec((None, tf, D), functools.partial(lambda k, i, j: (k, j, 0), k)) for k in (0, 1)]
    hbm = pl.BlockSpec(memory_space=pl.ANY)
    outs = pl.pallas_call(
        body, name=name, grid=(ni, nf), in_specs=[_row_spec(tm, D), _vec_spec(D)] + w + [hbm] * ng,
        out_specs=[tile, tile, tile, _row_spec(tm, D)] + [hbm] * ng,
        out_shape=[jax.ShapeDtypeStruct((T, F), BF)] * 3 + [jax.ShapeDtypeStruct((T, D), BF)] + _gathered_shapes(gather),
        scratch_shapes=_gather_sems(ng), compiler_params=_params("arbitrary", "arbitrary"),
    )(h, g.reshape(1, D), wffn, wffn, *gather)
    return outs[0], outs[1], outs[2], outs[3], list(outs[4:])


def _ffn_fwd_out(s, wffn, h, name, gather=(), ple=None):
    T, F = s.shape
    D = h.shape[1]
    tm = _tile(T, TM_FFN_IN, 16)
    ni, ng, ne = T // tm, len(gather), 4 if ple else 0

    def body(*refs):
        s_ref, w_ref, h_ref = refs[:3]
        pin, rest = refs[3:3 + ne], refs[3 + ne:]
        srcs, o_ref, pout, outs = rest[:ng], rest[ng], rest[ng + 1:ng + 1 + ne], rest[ng + 1 + ne:2 * ng + 1 + ne]
        i = pl.program_id(0)
        if ng:
            _carry(_gather_plan(gather, srcs, outs, *rest[2 * ng + 1 + ne:]), i == 0, i == (3 * ni) // 4, i == ni - 1)
        x = h_ref[...] + 0.5 * _dot(s_ref[...], w_ref[...])
        o_ref[...] = x
        if ple:
            g_ref, wg_ref, wp_ref, p_ref = pin
            e_ref, n_ref, gate_ref, pp_ref = pout
            n = _rms_parts(x, g_ref[...])[0].astype(BF)
            gate = _sigmoid(_dot(n, wg_ref[...]))
            pp = _dot_nt(p_ref[...].astype(BF), wp_ref[...])
            e_ref[...] = x + gate * pp
            n_ref[...] = n
            gate_ref[...] = gate.astype(BF)
            pp_ref[...] = pp.astype(BF)

    hbm = pl.BlockSpec(memory_space=pl.ANY)
    row = _row_spec(tm, D)
    once = lambda rows, cols: pl.BlockSpec((rows, cols), lambda i: (0, 0), pipeline_mode=pl.Buffered(1))
    extra_in, extra_out, extra_shape, extra_args = [], [], [], []
    if ple:
        g, wg, wp, p = ple
        P = p.shape[1]
        extra_in = [_vec_spec(D), once(D, D), once(D, P), _row_spec(tm, P)]
        extra_out = [row] * 4
        extra_shape = [jax.ShapeDtypeStruct((T, D), F32)] + [jax.ShapeDtypeStruct((T, D), BF)] * 3
        extra_args = [g.reshape(1, D), wg, wp, p]
    outs = pl.pallas_call(
        body, name=name, grid=(ni,),
        in_specs=[_row_spec(tm, F), pl.BlockSpec((None, F, D), lambda i: (2, 0, 0), pipeline_mode=pl.Buffered(1)), row]
        + extra_in + [hbm] * ng,
        out_specs=[row] + extra_out + [hbm] * ng,
        out_shape=[jax.ShapeDtypeStruct((T, D), F32)] + extra_shape + _gathered_shapes(gather),
        scratch_shapes=_gather_sems(ng), compiler_params=_params("arbitrary"),
    )(s, wffn, h, *extra_args, *gather)
    return outs[0], list(outs[1:1 + ne]), list(outs[1 + ne:])


def _ffn_bwd_act(dh, a, b, wffn, name, scatter=()):
    T, D = dh.shape
    F = wffn.shape[1]
    tm, tf = _tile(T, TM_FFN_ACT, 16), _tile(F, TF_FFN, LANES)
    ni, nf, ng = T // tm, F // tf, len(scatter)

    def body(*refs):
        dh_ref, a_ref, b_ref, wd_ref = refs[:4]
        srcs, (da_ref, db_ref, dhb_ref), outs = refs[4:4 + ng], refs[4 + ng:7 + ng], refs[7 + ng:7 + 2 * ng]
        i, j = pl.program_id(0), pl.program_id(1)
        if ng:
            _carry(_scatter_plan(scatter, srcs, outs, *refs[7 + 2 * ng:]), jnp.logical_and(i == 0, j == 0), None,
                   jnp.logical_and(i == ni - 1, j == nf - 1))

        @pl.when(j == 0)
        def _():
            dhb_ref[...] = dh_ref[...].astype(BF)

        ds = 0.5 * _dot_nt(dhb_ref[...], wd_ref[...])
        av, bv = a_ref[...].astype(F32), b_ref[...].astype(F32)
        sig = _sigmoid(av)
        da_ref[...] = (ds * bv * (sig * (1.0 + av * (1.0 - sig)))).astype(BF)
        db_ref[...] = (ds * (av * sig)).astype(BF)

    tile = pl.BlockSpec((tm, tf), lambda i, j: (i, j))
    hbm = pl.BlockSpec(memory_space=pl.ANY)
    outs = pl.pallas_call(
        body, name=name, grid=(ni, nf),
        in_specs=[_row_spec(tm, D), tile, tile, pl.BlockSpec((None, tf, D), lambda i, j: (2, j, 0))] + [hbm] * ng,
        out_specs=[tile, tile, _row_spec(tm, D)] + [hbm] * ng,
        out_shape=[jax.ShapeDtypeStruct((T, F), BF)] * 2 + [jax.ShapeDtypeStruct((T, D), BF)]
        + _scattered_shapes(scatter),
        scratch_shapes=_scatter_sems(ng), compiler_params=_params("arbitrary", "arbitrary"),
    )(dh, a, b, wffn, *[piece[0] for piece in scatter])
    return outs[0], outs[1], outs[2], list(outs[3:])


def _two_dot_norm_bwd(x1, x2, w, w_specs, h, g, dh, name, scatter=()):
    T, K = x1.shape
    D = h.shape[1]
    tm = _tile(T, TM_FFN_IN, 16)
    ni, ng = T // tm, len(scatter)

    def body(*refs):
        x1_ref, x2_ref, w1_ref, w2_ref, h_ref, g_ref, dh_ref = refs[:7]
        srcs, (o_ref, dg_ref), outs = refs[7:7 + ng], refs[7 + ng:9 + ng], refs[9 + ng:9 + 2 * ng]
        i = pl.program_id(0)
        if ng:
            _carry(_scatter_plan(scatter, srcs, outs, *refs[9 + 2 * ng:]), i == 0, None, i == ni - 1)

        @pl.when(i == 0)
        def _():
            dg_ref[...] = jnp.zeros_like(dg_ref)

        dn = _dot(x1_ref[...], w1_ref[...]) + _dot(x2_ref[...], w2_ref[...])
        dx, dg = _rms_bwd_parts(h_ref[...], g_ref[...], dn)
        o_ref[...] = dh_ref[...] + dx
        dg_ref[...] += dg

    hbm = pl.BlockSpec(memory_space=pl.ANY)
    act = pl.BlockSpec((tm, K), lambda i: (i, 0))
    outs = pl.pallas_call(
        body, name=name, grid=(ni,),
        in_specs=[act, act] + w_specs + [_row_spec(tm, D), _vec_spec(D), _row_spec(tm, D)] + [hbm] * ng,
        out_specs=[_row_spec(tm, D), _vec_spec(D)] + [hbm] * ng,
        out_shape=[jax.ShapeDtypeStruct((T, D), F32), jax.ShapeDtypeStruct((1, D), F32)] + _scattered_shapes(scatter),
        scratch_shapes=_scatter_sems(ng), compiler_params=_params("arbitrary"),
    )(x1, x2, w, w, h, g.reshape(1, D), dh, *[piece[0] for piece in scatter])
    return outs[0], outs[1], list(outs[2:])


def _ffn_bwd_in(da, db, wffn, h, g, dh, name, scatter=()):
    F, D = wffn.shape[1:]
    specs = [pl.BlockSpec((None, F, D), functools.partial(lambda k, i: (k, 0, 0), k), pipeline_mode=pl.Buffered(1))
             for k in (0, 1)]
    return _two_dot_norm_bwd(da, db, wffn, specs, h, g, dh, name, scatter)


def _lru_in_bwd(dgb, dxb, win, h, g, dh, name):
    R, D = win.shape[0] // 2, win.shape[1]
    specs = [pl.BlockSpec((R, D), functools.partial(lambda k, i: (k, 0), k), pipeline_mode=pl.Buffered(1)) for k in (0, 1)]
    return _two_dot_norm_bwd(dgb, dxb, win, specs, h, g, dh, name)[:2]


def _ffn_bwd_w(da, db, s, n, dhb, name, scatter=()):
    T, F = da.shape
    D = n.shape[1]
    tf, tk = _tile(F, TF_FFN_WG, LANES), _tile(T, TK_FFN_WG, 16)
    nj, nk, ng = F // tf, T // tk, len(scatter)

    def body(*refs):
        da_ref, db_ref, s_ref, n_ref, dh_ref = refs[:5]
        srcs, o_ref, outs = refs[5:5 + ng], refs[5 + ng], refs[6 + ng:6 + 2 * ng]
        g_sc, u_sc, d_sc = refs[6 + 2 * ng:9 + 2 * ng]
        j, k = pl.program_id(0), pl.program_id(1)
        if ng:
            _carry(_scatter_plan(scatter, srcs, outs, *refs[9 + 2 * ng:]), jnp.logical_and(j == 0, k == 0), None,
                   jnp.logical_and(j == nj - 1, k == nk - 1))

        @pl.when(k == 0)
        def _():
            g_sc[...] = jnp.zeros_like(g_sc)
            u_sc[...] = jnp.zeros_like(u_sc)
            d_sc[...] = jnp.zeros_like(d_sc)

        nv = n_ref[...]
        g_sc[...] += _dot_tn(da_ref[...], nv)
        u_sc[...] += _dot_tn(db_ref[...], nv)
        d_sc[...] += _dot_tn(s_ref[...], dh_ref[...])

        @pl.when(k == nk - 1)
        def _():
            o_ref[0] = g_sc[...].astype(BF)
            o_ref[1] = u_sc[...].astype(BF)
            o_ref[2] = (0.5 * d_sc[...]).astype(BF)

    act = pl.BlockSpec((tk, tf), lambda j, k: (k, j))
    tok = pl.BlockSpec((tk, D), lambda j, k: (k, 0))
    hbm = pl.BlockSpec(memory_space=pl.ANY)
    outs = pl.pallas_call(
        body, name=name, grid=(nj, nk), in_specs=[act, act, act, tok, tok] + [hbm] * ng,
        out_specs=[pl.BlockSpec((3, tf, D), lambda j, k: (0, j, 0), pipeline_mode=pl.Buffered(1))] + [hbm] * ng,
        out_shape=[jax.ShapeDtypeStruct((3, F, D), BF)] + _scattered_shapes(scatter),
        scratch_shapes=[pltpu.VMEM((tf, D), F32)] * 3 + _scatter_sems(ng),
        compiler_params=_params("arbitrary", "arbitrary"),
    )(da, db, s, n, dhb, *[piece[0] for piece in scatter])
    return outs[0], list(outs[1:])


def _ple_bwd(dh, gate, pp, h, g, wg, name):
    T, D = dh.shape
    tm = _tile(T, TM_EW, 16)

    def body(dh_ref, gate_ref, pp_ref, h_ref, g_ref, wg_ref, o_ref, dz_ref, dp_ref, dg_ref):
        @pl.when(pl.program_id(0) == 0)
        def _():
            dg_ref[...] = jnp.zeros_like(dg_ref)

        d, gate = dh_ref[...], gate_ref[...].astype(F32)
        dz = (d * pp_ref[...].astype(F32) * gate * (1.0 - gate)).astype(BF)
        dx, dg = _rms_bwd_parts(h_ref[...], g_ref[...], _dot_nt(dz, wg_ref[...]))
        o_ref[...] = d + dx
        dz_ref[...] = dz
        dp_ref[...] = (d * gate).astype(BF)
        dg_ref[...] += dg

    row = _row_spec(tm, D)
    return pl.pallas_call(
        body, name=name, grid=(T // tm,), in_specs=[row, row, row, row, _vec_spec(D), _vec_spec(D, D)],
        out_specs=[row, row, row, _vec_spec(D)],
        out_shape=[jax.ShapeDtypeStruct((T, D), F32), jax.ShapeDtypeStruct((T, D), BF), jax.ShapeDtypeStruct((T, D), BF),
                   jax.ShapeDtypeStruct((1, D), F32)],
        compiler_params=_params("arbitrary"),
    )(dh, gate, pp, h, g.reshape(1, D), wg)


def _lru_in(h, g, win, name):
    T, D = h.shape
    R = win.shape[0] // 2
    tm = _tile(T, TM_EW, 16)

    def body(h_ref, g_ref, w_ref, n_ref, gb_ref, xb_ref):
        n = _rms_parts(h_ref[...], g_ref[...])[0].astype(BF)
        n_ref[...] = n
        z = _dot_nt(n, w_ref[...])
        gb_ref[...] = z[:, :R].astype(BF)
        xb_ref[...] = z[:, R:]

    return pl.pallas_call(
        body, name=name, grid=(T // tm,),
        in_specs=[_row_spec(tm, D), _vec_spec(D), pl.BlockSpec((2 * R, D), lambda i: (0, 0), pipeline_mode=pl.Buffered(1))],
        out_specs=[_row_spec(tm, D), _row_spec(tm, R), _row_spec(tm, R)],
        out_shape=[jax.ShapeDtypeStruct((T, D), BF), jax.ShapeDtypeStruct((T, R), BF), jax.ShapeDtypeStruct((T, R), F32)],
        compiler_params=_params("parallel"),
    )(h, g.reshape(1, D), win)


def _lru_fwd(gb, xb, conv_w, conv_b, wa, wx, b_a, b_x, a_param, wout, h, name):
    T, R = xb.shape
    D = h.shape[1]
    tb = _tile(T, TB_SEQ, HALO)
    per, ng = tb // HALO, tb // SUBLANES

    def body(g_ref, x_ref, halo_ref, cw_ref, cb_ref, wa_ref, wx_ref, ba_ref, bx_ref, ap_ref, wo_ref, h_ref,
             o_ref, xc_ref, r_ref, ig_ref, a_ref, hs_ref, y_ref, ext, carry, a_sc, b_sc):
        i = pl.program_id(0)

        @pl.when(i == 0)
        def _():
            carry[...] = jnp.zeros_like(carry)

        ext[pl.ds(0, HALO), :] = jnp.where(i > 0, halo_ref[...], 0.0)
        ext[pl.ds(HALO, tb), :] = x_ref[...]
        xc = cb_ref[...] + cw_ref[0:1, :] * ext[pl.ds(HALO - 3, tb), :]
        for k in range(1, CONV_WIDTH):
            xc = xc + cw_ref[k:k + 1, :] * ext[pl.ds(HALO - 3 + k, tb), :]
        xcb = xc.astype(BF)
        r = _sigmoid(_dot(xcb, wa_ref[...]) + ba_ref[...])
        ig = _sigmoid(_dot(xcb, wx_ref[...]) + bx_ref[...])
        la = -LRU_C * r * _softplus_neg(ap_ref[...])
        av = jnp.exp(la)
        xc_ref[...] = xc
        r_ref[...] = r
        ig_ref[...] = ig
        a_ref[...] = av
        A = av.reshape(ng, SUBLANES, R)
        B = (jnp.sqrt(_one_minus_sq(la, av)) * (ig * xc)).reshape(ng, SUBLANES, R)
        sub = lax.broadcasted_iota(jnp.int32, (1, SUBLANES, R), 1)
        for k in (1, 2, 4):
            m = sub >= k
            a_n = jnp.where(m, pltpu.roll(A, k, 1), 1.0)
            b_n = jnp.where(m, pltpu.roll(B, k, 1), 0.0)
            B = A * b_n + B
            A = A * a_n
        a_sc[...] = A.reshape(tb, R)
        b_sc[...] = B.reshape(tb, R)

        def group(q, c):
            rows = pl.ds(pl.multiple_of(q * SUBLANES, SUBLANES), SUBLANES)
            hg = a_sc[rows, :] * c + b_sc[rows, :]
            hs_ref[rows, :] = hg
            return hg[SUBLANES - 1:SUBLANES, :]

        carry[...] = lax.fori_loop(0, ng, group, carry[...])
        y = (hs_ref[...] * _gelu_parts(g_ref[...].astype(F32))[0]).astype(BF)
        y_ref[...] = y
        o_ref[...] = h_ref[...] + _dot(y, wo_ref[...])

    once = lambda rows, cols: pl.BlockSpec((rows, cols), lambda i: (0, 0), pipeline_mode=pl.Buffered(1))
    gate = pl.BlockSpec((tb, R), lambda i: (i, 0))
    halo = pl.BlockSpec((HALO, R), lambda i: (jnp.maximum(i * per - 1, 0), 0))
    return pl.pallas_call(
        body, name=name, grid=(T // tb,),
        in_specs=[gate, gate, halo, _vec_spec(R, CONV_WIDTH), _vec_spec(R), once(R, R), once(R, R), _vec_spec(R),
                  _vec_spec(R), _vec_spec(R), once(R, D), _row_spec(tb, D)],
        out_specs=[_row_spec(tb, D)] + [gate] * 6,
        out_shape=[jax.ShapeDtypeStruct((T, D), F32)] + [jax.ShapeDtypeStruct((T, R), F32)] * 5
        + [jax.ShapeDtypeStruct((T, R), BF)],
        scratch_shapes=[pltpu.VMEM((HALO + tb, R), F32), pltpu.VMEM((1, R), F32), pltpu.VMEM((tb, R), F32),
                        pltpu.VMEM((tb, R), F32)],
        compiler_params=_params("arbitrary"),
    )(gb, xb, xb, conv_w, conv_b.reshape(1, R), wa, wx, b_a.reshape(1, R), b_x.reshape(1, R), a_param.reshape(1, R), wout, h)


def _lru_bwd(dh, hs, gb, xb, a, r, ig, xc, wa, wx, wout, conv_w, a_param, name):
    T, R = hs.shape
    D = dh.shape[1]
    tb = _tile(T, TB_SEQ, HALO)
    per, nt, ng = tb // HALO, T // tb, tb // SUBLANES

    def body(dh_ref, h_ref, hp_ref, g_ref, x_ref, xp_ref, a_ref, r_ref, ig_ref, xc_ref, wa_ref, wx_ref, wo_ref, cw_ref,
             ap_ref, dgb_ref, dxb_ref, dpa_ref, dpx_ref, dsp_ref, dba_ref, dbx_ref, dcb_ref, dcw_ref,
             hext, xext, dext, carry, later, a_sc, b_sc, d_sc, l_sc):
        i = pl.program_id(0)

        @pl.when(i == 0)
        def _():
            for ref in (dsp_ref, dba_ref, dbx_ref, dcb_ref, dcw_ref, carry, later):
                ref[...] = jnp.zeros_like(ref)

        dy = _dot_nt(dh_ref[...].astype(BF), wo_ref[...])
        gl, dgl = _gelu_parts(g_ref[...].astype(F32))
        hv, av = h_ref[...], a_ref[...]
        dhd = dy * gl
        dgb_ref[...] = (dy * hv * dgl).astype(BF)
        d_sc[...] = dhd
        A = av.reshape(ng, SUBLANES, R)
        B = A * dhd.reshape(ng, SUBLANES, R)
        sub = lax.broadcasted_iota(jnp.int32, (1, SUBLANES, R), 1)
        for k in (1, 2, 4):
            m = sub < SUBLANES - k
            a_n = jnp.where(m, pltpu.roll(A, SUBLANES - k, 1), 1.0)
            b_n = jnp.where(m, pltpu.roll(B, SUBLANES - k, 1), 0.0)
            B = A * b_n + B
            A = A * a_n
        a_sc[...] = A.reshape(tb, R)
        b_sc[...] = B.reshape(tb, R)
        sub8 = lax.broadcasted_iota(jnp.int32, (SUBLANES, R), 0)

        def group(q, c):
            rows = pl.ds(pl.multiple_of((ng - 1 - q) * SUBLANES, SUBLANES), SUBLANES)
            mu = a_sc[rows, :] * c + b_sc[rows, :]
            l_sc[rows, :] = d_sc[rows, :] + jnp.where(sub8 == SUBLANES - 1, c, pltpu.roll(mu, SUBLANES - 1, 0))
            return mu[0:1, :]

        carry[...] = lax.fori_loop(0, ng, group, carry[...])
        lam = l_sc[...]
        hext[pl.ds(0, HALO), :] = jnp.where(i < nt - 1, hp_ref[...], 0.0)
        hext[pl.ds(HALO, tb), :] = hv
        h_prev = hext[pl.ds(HALO - 1, tb), :]
        rv, igv, xcv = r_ref[...], ig_ref[...], xc_ref[...]
        sp = _softplus_neg(ap_ref[...])
        mult = jnp.sqrt(_one_minus_sq(-LRU_C * rv * sp, av))
        dla = lam * h_prev * av - lam * (igv * xcv) * (av * av) / mult
        du = lam * mult
        dpa = (dla * (-LRU_C) * sp) * rv * (1.0 - rv)
        dpx = (du * xcv) * igv * (1.0 - igv)
        dsp_ref[...] += jnp.sum(dla * (-LRU_C) * rv, axis=0, keepdims=True)
        dba_ref[...] += jnp.sum(dpa, axis=0, keepdims=True)
        dbx_ref[...] += jnp.sum(dpx, axis=0, keepdims=True)
        dpab, dpxb = dpa.astype(BF), dpx.astype(BF)
        dpa_ref[...] = dpab
        dpx_ref[...] = dpxb
        dxc = du * igv + _dot_nt(dpab, wa_ref[...]) + _dot_nt(dpxb, wx_ref[...])
        dext[pl.ds(0, tb), :] = dxc
        dext[pl.ds(tb, SUBLANES), :] = later[...]
        later[...] = dxc[0:SUBLANES, :]
        xext[pl.ds(0, HALO), :] = jnp.where(i < nt - 1, xp_ref[...], 0.0)
        xext[pl.ds(HALO, tb), :] = x_ref[...]
        dxb = cw_ref[CONV_WIDTH - 1:CONV_WIDTH, :] * dxc
        for k in range(CONV_WIDTH - 1):
            dxb = dxb + cw_ref[k:k + 1, :] * dext[pl.ds(CONV_WIDTH - 1 - k, tb), :]
        dxb_ref[...] = dxb.astype(BF)
        for k in range(CONV_WIDTH):
            dcw_ref[k:k + 1, :] += jnp.sum(dxc * xext[pl.ds(HALO - 3 + k, tb), :], axis=0, keepdims=True)
        dcb_ref[...] += jnp.sum(dxc, axis=0, keepdims=True)

        @pl.when(i == nt - 1)
        def _():
            dsp_ref[...] = dsp_ref[...] * (-_sigmoid(-ap_ref[...]))

    once = lambda rows, cols: pl.BlockSpec((rows, cols), lambda i: (0, 0), pipeline_mode=pl.Buffered(1))
    t0 = pl.BlockSpec((tb, R), lambda i: (nt - 1 - i, 0))
    prev = pl.BlockSpec((HALO, R), lambda i: (jnp.maximum((nt - 1 - i) * per - 1, 0), 0))
    return pl.pallas_call(
        body, name=name, grid=(nt,),
        in_specs=[pl.BlockSpec((tb, D), lambda i: (nt - 1 - i, 0)), t0, prev, t0, t0, prev, t0, t0, t0, t0,
                  once(R, R), once(R, R), once(R, D), _vec_spec(R, CONV_WIDTH), _vec_spec(R)],
        out_specs=[t0] * 4 + [_vec_spec(R)] * 4 + [_vec_spec(R, SUBLANES)],
        out_shape=[jax.ShapeDtypeStruct((T, R), BF)] * 4 + [jax.ShapeDtypeStruct((1, R), F32)] * 4
        + [jax.ShapeDtypeStruct((SUBLANES, R), F32)],
        scratch_shapes=[pltpu.VMEM((HALO + tb, R), F32), pltpu.VMEM((HALO + tb, R), F32),
                        pltpu.VMEM((tb + SUBLANES, R), F32), pltpu.VMEM((1, R), F32), pltpu.VMEM((SUBLANES, R), F32)]
        + [pltpu.VMEM((tb, R), F32)] * 4,
        compiler_params=_params("arbitrary"),
    )(dh, hs, hs, gb, xb, xb, a, r, ig, xc, wa, wx, wout, conv_w, a_param.reshape(1, R))


def _gate_spans(R):
    d = R // LRU_HEADS
    spans = [min((j * GATE_COLS // d) * d // LANES * LANES, R - GATE_SPAN) for j in range(R // GATE_COLS)]
    assert R % GATE_COLS == 0 and all(lo + GATE_SPAN >= (((j + 1) * GATE_COLS - 1) // d + 1) * d for j, lo in enumerate(spans))
    return spans


def _lru_gates_dw(xc, dpa, dpx, name):
    T, R = xc.shape
    tk = _tile(T, 1024, 16)
    spans = _gate_spans(R)
    nb = len(spans)

    def body(x_ref, a_ref, b_ref, o_ref):
        @pl.when(pl.program_id(0) == 0)
        def _():
            o_ref[...] = jnp.zeros_like(o_ref)

        for j, lo in enumerate(spans):
            xs = x_ref[:, pl.ds(lo, GATE_SPAN)].astype(BF)
            cols = pl.ds(j * GATE_COLS, GATE_COLS)
            o_ref[0, j] += _dot_tn(xs, a_ref[:, cols])
            o_ref[1, j] += _dot_tn(xs, b_ref[:, cols])

    row = _row_spec(tk, R)
    out = pl.pallas_call(
        body, name=name, grid=(T // tk,), in_specs=[row, row, row],
        out_specs=pl.BlockSpec((2, nb, GATE_SPAN, GATE_COLS), lambda i: (0, 0, 0, 0)),
        out_shape=jax.ShapeDtypeStruct((2, nb, GATE_SPAN, GATE_COLS), F32), compiler_params=_params("arbitrary"),
    )(xc, dpa, dpx)
    dense = jnp.zeros((2, R, R), F32)
    for j, lo in enumerate(spans):
        dense = dense.at[:, lo:lo + GATE_SPAN, j * GATE_COLS:(j + 1) * GATE_COLS].set(out[:, j])
    return dense[0], dense[1]


def _window_sums(e, n, back):
    out, s = [], e
    for k in (1, 2, 4, 8):
        s = s + pltpu.roll(s, k if back else n - k, 0)
        out.append(s)
    return out


def _pool_fwd(h, g, w, b, scale, name):
    T, D = h.shape
    G = len(POOL_WINDOWS)
    gd = D // G
    tb = _tile(T, TB_SEQ, HALO)
    per = tb // HALO

    def body(h_ref, hp_ref, g_ref, w_ref, b_ref, s_ref, o_ref, u_ref, yb_ref):
        i = pl.program_id(0)
        t = i * tb + lax.broadcasted_iota(jnp.int32, (tb, gd), 0) + 1
        hv = h_ref[...]
        xn = _rms_parts(hv, g_ref[...])[0]
        xp = jnp.where(i > 0, _rms_parts(hp_ref[...], g_ref[...])[0], 0.0)
        for k, win in enumerate(POOL_WINDOWS):
            cols = slice(k * gd, (k + 1) * gd)
            x = xn[:, cols]
            e = jnp.concatenate([xp[:, cols], x], axis=0)
            sw = _window_sums(e, HALO + tb, True)[k][HALO:, :]
            u = (sw / jnp.minimum(t, win).astype(F32) - x).astype(BF)
            yb = _dot(u, w_ref[k]) + b_ref[:, cols]
            u_ref[:, cols] = u
            yb_ref[:, cols] = yb
            o_ref[:, cols] = hv[:, cols] + yb * s_ref[:, cols]

    tile = _row_spec(tb, D)
    prev = pl.BlockSpec((HALO, D), lambda i: (jnp.maximum(i * per - 1, 0), 0))
    return pl.pallas_call(
        body, name=name, grid=(T // tb,),
        in_specs=[tile, prev, _vec_spec(D), pl.BlockSpec((G, gd, gd), lambda i: (0, 0, 0)), _vec_spec(D), _vec_spec(D)],
        out_specs=[tile, tile, tile],
        out_shape=[jax.ShapeDtypeStruct((T, D), F32), jax.ShapeDtypeStruct((T, D), BF), jax.ShapeDtypeStruct((T, D), F32)],
        compiler_params=_params("parallel"),
    )(h, h, g.reshape(1, D), w, b.reshape(1, D), scale.reshape(1, D))


def _pool_bwd(dm, u, yb, w, scale, h, g, name):
    T, D = dm.shape
    G = len(POOL_WINDOWS)
    gd = D // G
    tb = _tile(T, TB_SEQ, HALO)
    nt = T // tb

    def body(d_ref, u_ref, yb_ref, w_ref, s_ref, h_ref, g_ref, o_ref, dw_ref, db_ref, ds_ref, dg_ref, later):
        i = pl.program_id(0)

        @pl.when(i == 0)
        def _():
            for ref in (dw_ref, db_ref, ds_ref, dg_ref, later):
                ref[...] = jnp.zeros_like(ref)

        d, sc = d_ref[...], s_ref[...]
        ds_ref[...] += jnp.sum(d * yb_ref[...], axis=0, keepdims=True)
        db_ref[...] += jnp.sum(d * sc, axis=0, keepdims=True)
        t = (nt - 1 - i) * tb + lax.broadcasted_iota(jnp.int32, (tb, gd), 0) + 1
        parts = []
        for k, win in enumerate(POOL_WINDOWS):
            cols = slice(k * gd, (k + 1) * gd)
            dy = (d[:, cols] * sc[:, cols]).astype(BF)
            du = _dot_nt(dy, w_ref[k])
            dw_ref[k] += _dot_tn(u_ref[:, cols], dy)
            v = du / jnp.minimum(t, win).astype(F32)
            e = jnp.concatenate([v, later[:, cols]], axis=0)
            later[:, cols] = v[0:HALO, :]
            parts.append(_window_sums(e, tb + HALO, False)[k][:tb, :] - du)
        dx, dg = _rms_bwd_parts(h_ref[...], g_ref[...], jnp.concatenate(parts, axis=1))
        o_ref[...] = d + dx
        dg_ref[...] += dg

    tile = pl.BlockSpec((tb, D), lambda i: (nt - 1 - i, 0))
    whole = pl.BlockSpec((G, gd, gd), lambda i: (0, 0, 0))
    return pl.pallas_call(
        body, name=name, grid=(nt,), in_specs=[tile, tile, tile, whole, _vec_spec(D), tile, _vec_spec(D)],
        out_specs=[tile, whole, _vec_spec(D), _vec_spec(D), _vec_spec(D)],
        out_shape=[jax.ShapeDtypeStruct((T, D), F32), jax.ShapeDtypeStruct((G, gd, gd), F32)]
        + [jax.ShapeDtypeStruct((1, D), F32)] * 3,
        scratch_shapes=[pltpu.VMEM((HALO, D), F32)], compiler_params=_params("arbitrary"),
    )(dm, u, yb, w, scale.reshape(1, D), h, g.reshape(1, D))


def _adamw(w, g, m, v, name):
    shape = w.shape
    cols = shape[-1] if w.ndim > 1 else shape[0]
    rows = w.size // cols
    tr = _tile(rows, TR_ADAM, SUBLANES)
    c1, c2 = 1.0 / (1.0 - ADAM_B1 ** ADAM_STEP), 1.0 / (1.0 - ADAM_B2 ** ADAM_STEP)

    def body(w_ref, g_ref, m_ref, v_ref, d_ref, mo_ref, vo_ref):
        gv = g_ref[...]
        mn = ADAM_B1 * m_ref[...] + (1.0 - ADAM_B1) * gv
        vn = ADAM_B2 * v_ref[...] + (1.0 - ADAM_B2) * (gv * gv)
        d_ref[...] = -ADAM_LR * ((mn * c1) / (jnp.sqrt(vn * c2) + ADAM_EPS) + ADAM_WD * w_ref[...])
        mo_ref[...] = mn
        vo_ref[...] = vn

    spec = _row_spec(tr, cols)
    outs = pl.pallas_call(
        body, name=name, grid=(rows // tr,), in_specs=[spec] * 4, out_specs=[spec] * 3,
        out_shape=[jax.ShapeDtypeStruct((rows, cols), F32)] * 3, compiler_params=_params("parallel"),
    )(*[t.reshape(rows, cols) for t in (w, g, m, v)])
    return [o.reshape(shape) for o in outs]


def _sum_devices(parts, name):
    n, rows, cols = parts.shape
    tr = _tile(rows, 1024, SUBLANES)

    def body(p_ref, o_ref):
        acc = p_ref[0].astype(F32)
        for k in range(1, n):
            acc = acc + p_ref[k].astype(F32)
        o_ref[...] = acc

    return pl.pallas_call(
        body, name=name, grid=(rows // tr,), in_specs=[pl.BlockSpec((n, tr, cols), lambda i: (0, i, 0))],
        out_specs=_row_spec(tr, cols), out_shape=jax.ShapeDtypeStruct((rows, cols), F32),
        compiler_params=_params("parallel"),
    )(parts)


def _position():
    return lax.axis_index("x"), lax.axis_index("y"), lax.axis_index("c")


def _gathered_shapes(blocks):
    return [jax.ShapeDtypeStruct((b.shape[0], N_DEV * b.shape[1], b.shape[2]), b.dtype) for b in blocks]


def _gather_sems(ng):
    return [pltpu.SemaphoreType.DMA((ng, 7)), pltpu.SemaphoreType.DMA((ng, 7)), pltpu.SemaphoreType.DMA((ng,))] if ng else []


def _gather_plan(blocks, srcs, outs, send_sems, recv_sems, local_sems):
    ng = len(blocks)
    x, y, c = _position()
    me, sibling = (x, y, c), (x, y, 1 - c)
    chips = [(1 - x, y), (x, 1 - y), (1 - x, 1 - y)]

    def rows(g, px, py, pc):
        r = blocks[g].shape[1]
        return outs[g].at[:, pl.ds((4 * px + 2 * py + pc) * r, r), :]

    def copy(g, k, block, to, src=None):
        return pltpu.make_async_remote_copy(
            src_ref=rows(g, *block) if src is None else src, dst_ref=rows(g, *block),
            send_sem=send_sems.at[g, k], recv_sem=recv_sems.at[g, k], device_id=to, device_id_type=MESH)

    def mine(g):
        return pltpu.make_async_copy(srcs[g], rows(g, *me), local_sems.at[g])

    def first(g):
        return [copy(g, 0, me, sibling, src=srcs[g])] + [copy(g, 1 + j, me, (*chip, c), src=srcs[g])
                                                         for j, chip in enumerate(chips)]

    def passed(g):
        return [copy(g, 4 + j, (*chip, c), sibling) for j, chip in enumerate(chips)]

    def start():
        for g in range(ng):
            mine(g).start()
            for cp in first(g):
                cp.start()

    def forward():
        for j, chip in enumerate(chips):
            for g in range(ng):
                copy(g, 1 + j, (*chip, c), me).wait_recv()
                copy(g, 4 + j, (*chip, c), sibling).start()

    def finish():
        for g in range(ng):
            copy(g, 0, sibling, me).wait_recv()
            for j, chip in enumerate(chips):
                copy(g, 4 + j, (*chip, 1 - c), me).wait_recv()
            for cp in first(g) + passed(g):
                cp.wait_send()
            mine(g).wait()

    return start, forward, finish


def _all_gather(blocks, name):
    ng = len(blocks)

    def body(*refs):
        start, forward, finish = _gather_plan(blocks, refs[:ng], refs[ng:2 * ng], *refs[2 * ng:])
        start()
        forward()
        finish()

    hbm = pl.BlockSpec(memory_space=pl.ANY)
    return pl.pallas_call(
        body, name=name, in_specs=[hbm] * ng, out_specs=[hbm] * ng, out_shape=_gathered_shapes(blocks),
        scratch_shapes=_gather_sems(ng),
    )(*blocks)


FLIPS = ((0, 0, 1), (1, 0, 0), (0, 1, 0), (1, 1, 0), (1, 0, 1), (0, 1, 1), (1, 1, 1))


def _piece_rows(piece):
    arr, m = piece
    return arr.shape[0] if m is None else 1


def _scattered_shapes(pieces):
    return [jax.ShapeDtypeStruct((N_DEV, _piece_rows(p), p[0].shape[1] // N_DEV, p[0].shape[2]), p[0].dtype)
            for p in pieces]


def _scatter_sems(ng):
    n = len(FLIPS)
    return [pltpu.SemaphoreType.DMA((ng, n)), pltpu.SemaphoreType.DMA((ng, n)), pltpu.SemaphoreType.DMA((ng,))] if ng else []


def _scatter_plan(pieces, srcs, outs, send_sems, recv_sems, local_sems):
    x, y, c = _position()

    def block(g, tx, ty, tc):
        arr, m = pieces[g]
        r = arr.shape[1] // N_DEV
        lead = slice(None) if m is None else pl.ds(m, 1)
        return srcs[g].at[lead, pl.ds((4 * tx + 2 * ty + tc) * r, r), :]

    def copies(g):
        out = []
        for k, (fx, fy, fc) in enumerate(FLIPS):
            tx, ty, tc = (1 - x if fx else x), (1 - y if fy else y), (1 - c if fc else c)
            out.append(pltpu.make_async_remote_copy(
                src_ref=block(g, tx, ty, tc), dst_ref=outs[g].at[k], send_sem=send_sems.at[g, k],
                recv_sem=recv_sems.at[g, k], device_id=(tx, ty, tc), device_id_type=MESH))
        return out

    def mine(g):
        return pltpu.make_async_copy(block(g, x, y, c), outs[g].at[len(FLIPS)], local_sems.at[g])

    def start():
        for g in range(len(pieces)):
            mine(g).start()
            for cp in copies(g):
                cp.start()

    def finish():
        for g in range(len(pieces)):
            for cp in copies(g):
                cp.wait()
            mine(g).wait()

    return start, finish


def _scatter_and_gather(pieces, blocks, name):
    n_p, n_b = len(pieces), len(blocks)

    def body(*refs):
        ins, outs, sems = refs[:n_p + n_b], refs[n_p + n_b:2 * (n_p + n_b)], refs[2 * (n_p + n_b):]
        s_start, s_finish = _scatter_plan(pieces, ins[:n_p], outs[:n_p], *sems[:3])
        g_start, g_forward, g_finish = _gather_plan(blocks, ins[n_p:], outs[n_p:], *sems[3:])
        s_start()
        g_start()
        g_forward()
        g_finish()
        s_finish()

    hbm = pl.BlockSpec(memory_space=pl.ANY)
    outs = pl.pallas_call(
        body, name=name, in_specs=[hbm] * (n_p + n_b), out_specs=[hbm] * (n_p + n_b),
        out_shape=_scattered_shapes(pieces) + _gathered_shapes(blocks),
        scratch_shapes=_scatter_sems(n_p) + _gather_sems(n_b),
    )(*[p[0] for p in pieces], *blocks)
    return list(outs[:n_p]), list(outs[n_p:])


def _scatter_sum(recv, name):
    _, n, r, c = recv.shape

    def body(r_ref, o_ref):
        acc = r_ref[len(FLIPS)].astype(F32)
        for k in range(len(FLIPS)):
            acc = acc + r_ref[k].astype(F32)
        o_ref[...] = acc

    return pl.pallas_call(
        body, name=name, grid=(n,), in_specs=[pl.BlockSpec((N_DEV, None, r, c), lambda i: (0, i, 0, 0))],
        out_specs=pl.BlockSpec((None, r, c), lambda i: (i, 0, 0)),
        out_shape=jax.ShapeDtypeStruct((n, r, c), F32), compiler_params=_params("parallel"),
    )(recv)


def _block_diag(w):
    H, d, _ = w.shape
    return (jnp.eye(H, dtype=w.dtype)[:, None, :, None] * w[:, :, None, :]).reshape(H * d, H * d)


def _diag_blocks(dense, H):
    d = dense.shape[0] // H
    return jnp.stack([dense[i * d:(i + 1) * d, i * d:(i + 1) * d] for i in range(H)])


def _local_step(x, p, tgt, W, blocks=None):
    dist = blocks is not None
    L = p.shape[0]
    W = dict(W)

    def gathering(keys):
        return [k for k in keys if k not in W] if dist else []

    def ffn_fwd(h, g, i, f, during_act, during_out, ple=None):
        w = W[("ffn", i, f)]
        keys = gathering(during_act)
        a, b, s, n, got = _ffn_fwd_act(h, g, w, f"ffn{f}_fwd_act_{i}", gather=[blocks[k] for k in keys])
        W.update(zip(keys, got))
        keys = gathering(during_out)
        if ple is not None:
            ple = (W["ple_norm"][i], W[("ple_gate", i)][0], W[("ple_proj", i)][0], ple)
        h, emb, got = _ffn_fwd_out(s, w, h, f"ffn{f}_fwd_out_{i}", gather=[blocks[k] for k in keys], ple=ple)
        W.update(zip(keys, got))
        return (a, b, s, n), h, emb

    saved = []
    h = x
    for i in range(L):
        j = i // 2
        lru = i % 2 == 0
        s = {"h0": h}
        mixer = [("lru_in", j), ("lru_out", j)] if lru else [("pool_w", j)]
        s["ffn1"], h, _ = ffn_fwd(h, W["ffn1_norm"][i], i, 1, [("ffn", i, 2)], mixer)
        s["h1"] = h
        if lru:
            hn, gb, xb = _lru_in(h, W["mix_norm"][i], W[("lru_in", j)][0], f"lru_in_{i}")
            wa, wx = _block_diag(W["lru_w_a"][j]).astype(BF), _block_diag(W["lru_w_x"][j]).astype(BF)
            h, xc, r, ig, a, hs, y = _lru_fwd(gb, xb, W["lru_conv_w"][j], W["lru_conv_b"][j], wa, wx, W["lru_b_a"][j],
                                              W["lru_b_x"][j], W["lru_a_param"][j], W[("lru_out", j)][0], h, f"lru_fwd_{i}")
            s.update(hn=hn, gb=gb, xb=xb, wa=wa, wx=wx, xc=xc, r=r, ig=ig, a=a, hs=hs, y=y)
        else:
            h, s["u"], s["yb"] = _pool_fwd(h, W["mix_norm"][i], W[("pool_w", j)], W["pool_b"][j], W["pool_scale"][j],
                                           f"pool_fwd_{i}")
        s["h2"] = h
        nxt = [("ffn", i + 1, 1)] if i + 1 < L else []
        s["ffn2"], s["h3"], (h, s["n4"], s["gate"], s["pp"]) = ffn_fwd(
            h, W["ffn2_norm"][i], i, 2, [("ple_gate", i), ("ple_proj", i)] + nxt, [], ple=p[i])
        saved.append(s)

    loss, dh, d_final = _loss_head(h, W["final_norm"], tgt)

    big, recv = {}, {}
    n_lru, n_pool = L // 2 + L % 2, L // 2
    small = {k: [None] * L for k in ("ffn1_norm", "mix_norm", "ffn2_norm", "ple_norm")}
    for k in ("lru_conv_w", "lru_conv_b", "lru_w_a", "lru_b_a", "lru_w_x", "lru_b_x", "lru_a_param"):
        small[k] = [None] * n_lru
    for k in ("pool_b", "pool_scale"):
        small[k] = [None] * n_pool

    def scattering(pieces):
        return [(k, m) for k, m in pieces if k in big] if dist else []

    def ffn_bwd(dh, h_in, g, acts, i, f, during):
        key, w = ("ffn", i, f), W[("ffn", i, f)]
        a, b, sv, n = acts
        out = [scattering(d) for d in during]
        sent = [[(big[k], m) for k, m in o] for o in out]
        da, db, dhb, got0 = _ffn_bwd_act(dh, a, b, w, f"ffn{f}_bwd_act_{i}", scatter=sent[0])
        big[key], got1 = _ffn_bwd_w(da, db, sv, n, dhb, f"ffn{f}_dw_{i}", scatter=sent[1])
        out.append(scattering([(key, 0)] + ([(key, 1)] if (i, f) == (0, 1) else [])))
        dh, dg, got2 = _ffn_bwd_in(da, db, w, h_in, g, dh, f"ffn{f}_bwd_in_{i}", scatter=[(big[k], m) for k, m in out[2]])
        for o, got in zip(out, (got0, got1, got2)):
            recv.update(zip(o, got))
        return dh, dg

    for i in reversed(range(L)):
        j = i // 2
        lru = i % 2 == 0
        s = saved[i]
        dh, dz, dpp, dg = _ple_bwd(dh, s["gate"], s["pp"], s["h3"], W["ple_norm"][i], W[("ple_gate", i)][0],
                                   f"ple_bwd_{i}")
        big[("ple_gate", i)] = _mm(s["n4"], dz, "tn", f"ple_gate_dw_{i}", out_dtype=BF)[None]
        big[("ple_proj", i)] = _mm(dpp, p[i], "tn", f"ple_proj_dw_{i}", out_dtype=BF)[None]
        small["ple_norm"][i] = dg[0]
        above = ("ffn", i + 1, 1)
        dh, dg = ffn_bwd(dh, s["h2"], W["ffn2_norm"][i], s["ffn2"], i, 2, [
            [(above, 1)], [(above, 2), (("ple_gate", i), None), (("ple_proj", i), None)]])
        small["ffn2_norm"][i] = dg[0]
        if lru:
            big[("lru_out", j)] = _mm(s["y"], dh, "tn", f"lru_out_dw_{i}", out_dtype=BF)[None]
            dgb, dxb, dpa, dpx, dsp, dba, dbx, dcb, dcw = _lru_bwd(
                dh, s["hs"], s["gb"], s["xb"], s["a"], s["r"], s["ig"], s["xc"], s["wa"], s["wx"], W[("lru_out", j)][0],
                W["lru_conv_w"][j], W["lru_a_param"][j], f"lru_bwd_{i}")
            small["lru_a_param"][j], small["lru_b_a"][j], small["lru_b_x"][j] = dsp[0], dba[0], dbx[0]
            dwa, dwx = _lru_gates_dw(s["xc"], dpa, dpx, f"lru_gates_dw_{i}")
            small["lru_w_a"][j], small["lru_w_x"][j] = _diag_blocks(dwa, LRU_HEADS), _diag_blocks(dwx, LRU_HEADS)
            small["lru_conv_w"][j], small["lru_conv_b"][j] = dcw[:CONV_WIDTH], dcb[0]
            big[("lru_in", j)] = jnp.concatenate([_mm(dgb, s["hn"], "tn", f"lru_in_dw_g_{i}", out_dtype=BF),
                                                  _mm(dxb, s["hn"], "tn", f"lru_in_dw_x_{i}", out_dtype=BF)])[None]
            dh, dg = _lru_in_bwd(dgb, dxb, W[("lru_in", j)][0], s["h1"], W["mix_norm"][i], dh, f"lru_in_bwd_{i}")
            mixer = [("lru_in", j), ("lru_out", j)]
        else:
            dh, dw, dbp, dsc, dg = _pool_bwd(dh, s["u"], s["yb"], W[("pool_w", j)], W["pool_scale"][j], s["h1"],
                                             W["mix_norm"][i], f"pool_bwd_{i}")
            big[("pool_w", j)] = dw.astype(BF)
            small["pool_b"][j], small["pool_scale"][j] = dbp[0], dsc[0]
            mixer = [("pool_w", j)]
        small["mix_norm"][i] = dg[0]
        second = ("ffn", i, 2)
        dh, dg = ffn_bwd(dh, s["h0"], W["ffn1_norm"][i], s["ffn1"], i, 1, [
            [(second, 1)], [(second, 2)] + [(k, None) for k in mixer]])
        small["ffn1_norm"][i] = dg[0]

    small = {k: jnp.stack(v) for k, v in small.items()}
    small["final_norm"] = d_final[0]
    return loss, dh, big, recv, small


SMALL_SHARDED = ("pool_b", "pool_scale", "lru_conv_w")
SMALL = ("ffn1_norm", "mix_norm", "ffn2_norm", "ple_norm", "final_norm", "lru_conv_b", "lru_w_a", "lru_b_a",
         "lru_w_x", "lru_b_x", "lru_a_param", "pool_b", "pool_scale", "lru_conv_w")


def _pack_big(w):
    t = lambda a: jnp.swapaxes(a, -1, -2)
    out = {}
    for i in range(w["ffn1_norm"].shape[0]):
        for f in (1, 2):
            out[("ffn", i, f)] = jnp.stack([t(w[f"ffn{f}_w_gate"][i]), t(w[f"ffn{f}_w_up"][i]), w[f"ffn{f}_w_down"][i]])
        out[("ple_gate", i)], out[("ple_proj", i)] = w["ple_w_gate"][i][None], t(w["ple_w_proj"][i])[None]
    for j in range(w["lru_w_in"].shape[0]):
        out[("lru_in", j)], out[("lru_out", j)] = t(w["lru_w_in"][j])[None], w["lru_w_out"][j][None]
    for j in range(w["pool_w"].shape[0]):
        out[("pool_w", j)] = w["pool_w"][j]
    return out


def _unpack_big(b, L):
    t = lambda a: jnp.swapaxes(a, -1, -2)
    n_lru, n_pool = L // 2 + L % 2, L // 2
    out = {"lru_w_in": jnp.stack([t(b[("lru_in", j)][0]) for j in range(n_lru)]),
           "lru_w_out": jnp.stack([b[("lru_out", j)][0] for j in range(n_lru)]),
           "pool_w": jnp.stack([b[("pool_w", j)] for j in range(n_pool)]),
           "ple_w_gate": jnp.stack([b[("ple_gate", i)][0] for i in range(L)]),
           "ple_w_proj": jnp.stack([t(b[("ple_proj", i)][0]) for i in range(L)])}
    for f in (1, 2):
        out[f"ffn{f}_w_gate"] = jnp.stack([t(b[("ffn", i, f)][0]) for i in range(L)])
        out[f"ffn{f}_w_up"] = jnp.stack([t(b[("ffn", i, f)][1]) for i in range(L)])
        out[f"ffn{f}_w_down"] = jnp.stack([b[("ffn", i, f)][2] for i in range(L)])
    return out


def _flatten(parts, names, rows_of=LANES):
    flat = jnp.concatenate([parts[k].reshape(-1) for k in names])
    pad = (-flat.size) % (16 * rows_of)
    return jnp.pad(flat, (0, pad)).reshape(1, -1, rows_of)


def _unflatten(flat, like, names):
    out, o = {}, 0
    flat = flat.reshape(-1)
    for k in names:
        n = like[k].size
        out[k] = flat[o:o + n].reshape(like[k].shape)
        o += n
    return out


def kernel(x, p, ffn1_norm, ffn1_w_gate, ffn1_w_up, ffn1_w_down, mix_norm, lru_w_in, lru_conv_w, lru_conv_b, lru_w_a, lru_b_a, lru_w_x, lru_b_x, lru_a_param, lru_w_out, pool_w, pool_b, pool_scale, ffn2_norm, ffn2_w_gate, ffn2_w_up, ffn2_w_down, ple_norm, ple_w_gate, ple_w_proj, final_norm, loss_target, m_ffn1_norm, m_ffn1_w_gate, m_ffn1_w_up, m_ffn1_w_down, m_mix_norm, m_lru_w_in, m_lru_conv_w, m_lru_conv_b, m_lru_w_a, m_lru_b_a, m_lru_w_x, m_lru_b_x, m_lru_a_param, m_lru_w_out, m_pool_w, m_pool_b, m_pool_scale, m_ffn2_norm, m_ffn2_w_gate, m_ffn2_w_up, m_ffn2_w_down, m_ple_norm, m_ple_w_gate, m_ple_w_proj, m_final_norm, v_ffn1_norm, v_ffn1_w_gate, v_ffn1_w_up, v_ffn1_w_down, v_mix_norm, v_lru_w_in, v_lru_conv_w, v_lru_conv_b, v_lru_w_a, v_lru_b_a, v_lru_w_x, v_lru_b_x, v_lru_a_param, v_lru_w_out, v_pool_w, v_pool_b, v_pool_scale, v_ffn2_norm, v_ffn2_w_gate, v_ffn2_w_up, v_ffn2_w_down, v_ple_norm, v_ple_w_gate, v_ple_w_proj, v_final_norm):
    names = ["ffn1_norm", "ffn1_w_gate", "ffn1_w_up", "ffn1_w_down", "mix_norm", "lru_w_in", "lru_conv_w", "lru_conv_b",
             "lru_w_a", "lru_b_a", "lru_w_x", "lru_b_x", "lru_a_param", "lru_w_out", "pool_w", "pool_b", "pool_scale",
             "ffn2_norm", "ffn2_w_gate", "ffn2_w_up", "ffn2_w_down", "ple_norm", "ple_w_gate", "ple_w_proj", "final_norm"]
    w = dict(zip(names, [ffn1_norm, ffn1_w_gate, ffn1_w_up, ffn1_w_down, mix_norm, lru_w_in, lru_conv_w, lru_conv_b, lru_w_a, lru_b_a, lru_w_x, lru_b_x, lru_a_param, lru_w_out, pool_w, pool_b, pool_scale, ffn2_norm, ffn2_w_gate, ffn2_w_up, ffn2_w_down, ple_norm, ple_w_gate, ple_w_proj, final_norm]))
    m = dict(zip(names, [m_ffn1_norm, m_ffn1_w_gate, m_ffn1_w_up, m_ffn1_w_down, m_mix_norm, m_lru_w_in, m_lru_conv_w, m_lru_conv_b, m_lru_w_a, m_lru_b_a, m_lru_w_x, m_lru_b_x, m_lru_a_param, m_lru_w_out, m_pool_w, m_pool_b, m_pool_scale, m_ffn2_norm, m_ffn2_w_gate, m_ffn2_w_up, m_ffn2_w_down, m_ple_norm, m_ple_w_gate, m_ple_w_proj, m_final_norm]))
    v = dict(zip(names, [v_ffn1_norm, v_ffn1_w_gate, v_ffn1_w_up, v_ffn1_w_down, v_mix_norm, v_lru_w_in, v_lru_conv_w, v_lru_conv_b, v_lru_w_a, v_lru_b_a, v_lru_w_x, v_lru_b_x, v_lru_a_param, v_lru_w_out, v_pool_w, v_pool_b, v_pool_scale, v_ffn2_norm, v_ffn2_w_gate, v_ffn2_w_up, v_ffn2_w_down, v_ple_norm, v_ple_w_gate, v_ple_w_proj, v_final_norm]))
    L = p.shape[0]
    px, py, pc = _position()
    me = 4 * px + 2 * py + pc

    blocks = {k: b.astype(BF) for k, b in _pack_big(w).items()}
    first = ("ffn", 0, 1)
    got, small_blocks = _all_gather([blocks[first], _flatten(w, SMALL_SHARDED)], "gather_first")
    W = {first: got}
    per_dev = small_blocks.reshape(N_DEV, -1)
    shards = [_unflatten(per_dev[k], w, SMALL_SHARDED) for k in range(N_DEV)]
    for k in SMALL:
        W[k] = jnp.concatenate([s[k] for s in shards], axis=-1) if k in SMALL_SHARDED else w[k]

    loss, dx, big, recv, small = _local_step(x[0], p[:, 0], loss_target[0], W, blocks)

    last = [(k, m) for k in big if (k, None) not in recv for m in range(big[k].shape[0]) if (k, m) not in recv]
    got, (parts,) = _scatter_and_gather([(big[k], m) for k, m in last], [_flatten(small, SMALL).astype(BF)],
                                        "scatter_last_gather_small")
    recv.update(zip(last, got))

    def total(k):
        tag = "sum_" + "_".join(map(str, k))
        if (k, None) in recv:
            return _scatter_sum(recv[(k, None)], tag)
        return jnp.concatenate([_scatter_sum(recv[(k, m)], f"{tag}_{m}") for m in range(big[k].shape[0])])

    grads = _unpack_big({k: total(k) for k in big}, L)
    total_small = _sum_devices(parts.reshape(N_DEV, -1, LANES), "sum_small_grads")
    full = _unflatten(total_small, {k: W[k] for k in SMALL}, SMALL)
    for k in SMALL:
        if k in SMALL_SHARDED:
            n = w[k].shape[-1]
            grads[k] = lax.dynamic_slice_in_dim(full[k], me * n, n, axis=-1)
        else:
            grads[k] = full[k]

    delta, new_m, new_v = {}, {}, {}
    for k in names:
        delta[k], new_m[k], new_v[k] = _adamw(w[k], grads[k], m[k], v[k], f"adamw_{k}")
    total_loss = lax.psum(loss[0, 0], ("x", "y", "c"))
    return (total_loss, dx[None], *[grads[k] for k in names], *[delta[k] for k in names],
            *[new_m[k] for k in names], *[new_v[k] for k in names])
```

```python
import functools

import jax
import jax.numpy as jnp
from jax import lax
from jax.experimental import pallas as pl
from jax.experimental.pallas import tpu as pltpu

F32 = jnp.float32
BF = jnp.bfloat16
MESH = pl.DeviceIdType.MESH

RMS_EPS = 1e-6
LRU_C = 8.0
LRU_HEADS = 16
CONV_WIDTH = 4
POOL_WINDOWS = (2, 4, 8, 16)
ADAM_LR, ADAM_B1, ADAM_B2, ADAM_EPS, ADAM_WD, ADAM_STEP = 0.001, 0.9, 0.999, 1e-08, 0.01, 10

N_DEV = 8
LANES = 128
SUBLANES = 8
GATE_COLS, GATE_SPAN = 256, 512
HALO = 16
VMEM_LIMIT = 56 * 1024 * 1024

TM_FFN = 1024
TM_FFN_ACT = 2048
TM_FFN_IN = 512
TF_FFN = 256
TF_FFN_WG = 1408
TK_FFN_WG = 512
TB_SEQ = 256
TM_EW = 512
TM_MM, TN_MM, TK_MM = 1024, 512, 1024
TR_ADAM = 512


def _tile(n, pref, align):
    if n <= pref:
        return n
    t = (pref // align) * align
    while t >= align:
        if n % t == 0:
            return t
        t -= align
    raise ValueError(f"no tile for {n} (pref {pref}, align {align})")


def _params(*sem):
    return pltpu.CompilerParams(dimension_semantics=sem, vmem_limit_bytes=VMEM_LIMIT)


def _dot(a, b):
    return lax.dot_general(a, b, (((1,), (0,)), ((), ())), preferred_element_type=F32)


def _dot_nt(a, b):
    return lax.dot_general(a, b, (((1,), (1,)), ((), ())), preferred_element_type=F32)


def _dot_tn(a, b):
    return lax.dot_general(a, b, (((0,), (0,)), ((), ())), preferred_element_type=F32)


def _sigmoid(x):
    return 0.5 + 0.5 * jnp.tanh(0.5 * x)


def _sigmoid_pos(x):
    return 1.0 / (1.0 + jnp.exp(-x))


def _gelu_parts(x):
    k0, k1 = 0.7978845608028654, 0.044715
    t = jnp.tanh(k0 * (x + k1 * x * x * x))
    g = 0.5 * x * (1.0 + t)
    dg = 0.5 * (1.0 + t) + 0.5 * x * (1.0 - t * t) * k0 * (1.0 + 3.0 * k1 * x * x)
    return g, dg


def _one_minus_sq(la, a):
    return jnp.tanh(-la) * (1.0 + a * a)


def _softplus_neg(l):
    u = jnp.exp(-jnp.abs(l))
    w = 1.0 + u
    log1p = jnp.where(w == 1.0, u, jnp.log(w) * (u / jnp.where(w == 1.0, 1.0, w - 1.0)))
    return jnp.maximum(-l, 0.0) + log1p


def _rms_parts(x, g):
    r = lax.rsqrt(jnp.mean(x * x, axis=-1, keepdims=True) + RMS_EPS)
    nhat = x * r
    return nhat * g, nhat, r


def _rms_bwd_parts(x, g, dn):
    _, nhat, r = _rms_parts(x, g)
    u = dn * g
    dx = r * (u - nhat * jnp.mean(u * nhat, axis=-1, keepdims=True))
    return dx, jnp.sum(dn * nhat, axis=0, keepdims=True)


def _row_spec(tm, d, single=False):
    if single:
        return pl.BlockSpec((tm, d), lambda i, *_: (i, 0), pipeline_mode=pl.Buffered(1))
    return pl.BlockSpec((tm, d), lambda i, *_: (i, 0))


def _vec_spec(d, rows=1):
    return pl.BlockSpec((rows, d), lambda *_: (0, 0))


def _mm(x, w, mode, name, out_dtype=F32, res=None, alpha=1.0, tm=None, tn=None, tk=None):
    if mode == "nn":
        (M, K), (_, N) = x.shape, w.shape
    elif mode == "nt":
        (M, K), (N, _) = x.shape, w.shape
    else:
        (K, M), (_, N) = x.shape, w.shape
    tm = _tile(M, tm or TM_MM, LANES if mode == "tn" else SUBLANES)
    tn = _tile(N, tn or TN_MM, LANES)
    tk = _tile(K, tk or TK_MM, LANES if mode != "tn" else 16)
    nk = K // tk
    dot = {"nn": _dot, "nt": _dot_nt, "tn": _dot_tn}[mode]

    def body(*refs):
        if res is None:
            x_ref, w_ref, o_ref, acc = refs
        else:
            x_ref, w_ref, r_ref, o_ref, acc = refs
        k = pl.program_id(2)

        @pl.when(k == 0)
        def _():
            acc[...] = jnp.zeros_like(acc)

        acc[...] += dot(x_ref[...].astype(BF), w_ref[...].astype(BF))

        @pl.when(k == nk - 1)
        def _():
            r = acc[...] if alpha == 1.0 else acc[...] * alpha
            if res is not None:
                r = r_ref[...] + r
            o_ref[...] = r.astype(out_dtype)

    if mode == "nn":
        specs = [pl.BlockSpec((tm, tk), lambda i, j, k: (i, k)), pl.BlockSpec((tk, tn), lambda i, j, k: (k, j))]
    elif mode == "nt":
        specs = [pl.BlockSpec((tm, tk), lambda i, j, k: (i, k)), pl.BlockSpec((tn, tk), lambda i, j, k: (j, k))]
    else:
        specs = [pl.BlockSpec((tk, tm), lambda i, j, k: (k, i)), pl.BlockSpec((tk, tn), lambda i, j, k: (k, j))]
    args = [x, w]
    if res is not None:
        specs.append(pl.BlockSpec((tm, tn), lambda i, j, k: (i, j)))
        args.append(res)
    return pl.pallas_call(
        body, name=name, grid=(M // tm, N // tn, nk), in_specs=specs,
        out_specs=pl.BlockSpec((tm, tn), lambda i, j, k: (i, j)),
        out_shape=jax.ShapeDtypeStruct((M, N), out_dtype),
        scratch_shapes=[pltpu.VMEM((tm, tn), F32)],
        compiler_params=_params("parallel", "parallel", "arbitrary"),
    )(*args)


def _loss_head(h, g, tgt):
    T, D = h.shape
    tm = _tile(T, TM_EW, 16)

    def body(h_ref, g_ref, t_ref, loss_ref, dh_ref, dg_ref):
        @pl.when(pl.program_id(0) == 0)
        def _():
            dg_ref[...] = jnp.zeros_like(dg_ref)
            loss_ref[...] = jnp.zeros_like(loss_ref)

        x, gg = h_ref[...], g_ref[...]
        y = _rms_parts(x, gg)[0]
        e = y - t_ref[...]
        part = jnp.sum(jnp.sum(e * e, axis=0, keepdims=True), axis=1, keepdims=True) * (0.5 / D)
        loss_ref[...] += jnp.broadcast_to(part, loss_ref.shape)
        dx, dg = _rms_bwd_parts(x, gg, e * (1.0 / D))
        dh_ref[...] = dx
        dg_ref[...] += dg

    return pl.pallas_call(
        body, name="loss_head", grid=(T // tm,),
        in_specs=[_row_spec(tm, D), _vec_spec(D), _row_spec(tm, D)],
        out_specs=[_vec_spec(LANES), _row_spec(tm, D), _vec_spec(D)],
        out_shape=[jax.ShapeDtypeStruct((1, LANES), F32), jax.ShapeDtypeStruct((T, D), F32),
                   jax.ShapeDtypeStruct((1, D), F32)],
        compiler_params=_params("arbitrary"),
    )(h, g.reshape(1, D), tgt)


def _carry(plan, first, mid, last):
    pl.when(first)(plan[0])
    if len(plan) == 3:
        pl.when(mid)(plan[1])
    pl.when(last)(plan[-1])


def _ffn_fwd_act(h, g, wffn, name, gather=()):
    T, D = h.shape
    F = wffn.shape[1]
    tm, tf = _tile(T, TM_FFN_ACT, 16), _tile(F, TF_FFN, LANES)
    ni, nf, ng = T // tm, F // tf, len(gather)

    def body(*refs):
        h_ref, g_ref, wg_ref, wu_ref = refs[:4]
        srcs, (a_ref, b_ref, s_ref, n_ref), outs = refs[4:4 + ng], refs[4 + ng:8 + ng], refs[8 + ng:8 + 2 * ng]
        i, j = pl.program_id(0), pl.program_id(1)
        if ng:
            _carry(_gather_plan(gather, srcs, outs, *refs[8 + 2 * ng:]), jnp.logical_and(i == 0, j == 0),
                   jnp.logical_and(i == (3 * ni) // 4, j == 0), jnp.logical_and(i == ni - 1, j == nf - 1))

        @pl.when(j == 0)
        def _():
            n_ref[...] = _rms_parts(h_ref[...], g_ref[...])[0].astype(BF)

        n = n_ref[...]
        a = _dot_nt(n, wg_ref[...])
        b = _dot_nt(n, wu_ref[...])
        a_ref[...] = a.astype(BF)
        b_ref[...] = b.astype(BF)
        s_ref[...] = (a * _sigmoid(a) * b).astype(BF)

    tile = pl.BlockSpec((tm, tf), lambda i, j: (i, j))
    w = [pl.BlockSpec((None, tf, D), functools.partial(lambda k, i, j: (k, j, 0), k)) for k in (0, 1)]
    hbm = pl.BlockSpec(memory_space=pl.ANY)
    outs = pl.pallas_call(
        body, name=name, grid=(ni, nf), in_specs=[_row_spec(tm, D), _vec_spec(D)] + w + [hbm] * ng,
        out_specs=[tile, tile, tile, _row_spec(tm, D)] + [hbm] * ng,
        out_shape=[jax.ShapeDtypeStruct((T, F), BF)] * 3 + [jax.ShapeDtypeStruct((T, D), BF)] + _gathered_shapes(gather),
        scratch_shapes=_gather_sems(ng), compiler_params=_params("arbitrary", "arbitrary"),
    )(h, g.reshape(1, D), wffn, wffn, *gather)
    return outs[0], outs[1], outs[2], outs[3], list(outs[4:])


def _ffn_fwd_out(s, wffn, h, name, gather=(), ple=None):
    T, F = s.shape
    D = h.shape[1]
    tm = _tile(T, TM_FFN_IN, 16)
    ni, ng, ne = T // tm, len(gather), 4 if ple else 0

    def body(*refs):
        s_ref, w_ref, h_ref = refs[:3]
        pin, rest = refs[3:3 + ne], refs[3 + ne:]
        srcs, o_ref, pout, outs = rest[:ng], rest[ng], rest[ng + 1:ng + 1 + ne], rest[ng + 1 + ne:2 * ng + 1 + ne]
        i = pl.program_id(0)
        if ng:
            _carry(_gather_plan(gather, srcs, outs, *rest[2 * ng + 1 + ne:]), i == 0, i == (3 * ni) // 4, i == ni - 1)
        x = h_ref[...] + 0.5 * _dot(s_ref[...], w_ref[...])
        o_ref[...] = x
        if ple:
            g_ref, wg_ref, wp_ref, p_ref = pin
            e_ref, n_ref, gate_ref, pp_ref = pout
            n = _rms_parts(x, g_ref[...])[0].astype(BF)
            gate = _sigmoid(_dot(n, wg_ref[...]))
            pp = _dot_nt(p_ref[...].astype(BF), wp_ref[...])
            e_ref[...] = x + gate * pp
            n_ref[...] = n
            gate_ref[...] = gate.astype(BF)
            pp_ref[...] = pp.astype(BF)

    hbm = pl.BlockSpec(memory_space=pl.ANY)
    row = _row_spec(tm, D)
    once = lambda rows, cols: pl.BlockSpec((rows, cols), lambda i: (0, 0), pipeline_mode=pl.Buffered(1))
    extra_in, extra_out, extra_shape, extra_args = [], [], [], []
    if ple:
        g, wg, wp, p = ple
        P = p.shape[1]
        extra_in = [_vec_spec(D), once(D, D), once(D, P), _row_spec(tm, P)]
        extra_out = [row] * 4
        extra_shape = [jax.ShapeDtypeStruct((T, D), F32)] + [jax.ShapeDtypeStruct((T, D), BF)] * 3
        extra_args = [g.reshape(1, D), wg, wp, p]
    outs = pl.pallas_call(
        body, name=name, grid=(ni,),
        in_specs=[_row_spec(tm, F), pl.BlockSpec((None, F, D), lambda i: (2, 0, 0), pipeline_mode=pl.Buffered(1)), row]
        + extra_in + [hbm] * ng,
        out_specs=[row] + extra_out + [hbm] * ng,
        out_shape=[jax.ShapeDtypeStruct((T, D), F32)] + extra_shape + _gathered_shapes(gather),
        scratch_shapes=_gather_sems(ng), compiler_params=_params("arbitrary"),
    )(s, wffn, h, *extra_args, *gather)
    return outs[0], list(outs[1:1 + ne]), list(outs[1 + ne:])


def _ffn_bwd_act(dh, a, b, wffn, name, scatter=()):
    T, D = dh.shape
    F = wffn.shape[1]
    tm, tf = _tile(T, TM_FFN_ACT, 16), _tile(F, TF_FFN, LANES)
    ni, nf, ng = T // tm, F // tf, len(scatter)

    def body(*refs):
        dh_ref, a_ref, b_ref, wd_ref = refs[:4]
        srcs, (da_ref, db_ref, dhb_ref), outs = refs[4:4 + ng], refs[4 + ng:7 + ng], refs[7 + ng:7 + 2 * ng]
        i, j = pl.program_id(0), pl.program_id(1)
        if ng:
            _carry(_scatter_plan(scatter, srcs, outs, *refs[7 + 2 * ng:]), jnp.logical_and(i == 0, j == 0), None,
                   jnp.logical_and(i == ni - 1, j == nf - 1))

        @pl.when(j == 0)
        def _():
            dhb_ref[...] = dh_ref[...].astype(BF)

        ds = 0.5 * _dot_nt(dhb_ref[...], wd_ref[...])
        av, bv = a_ref[...].astype(F32), b_ref[...].astype(F32)
        sig = _sigmoid(av)
        da_ref[...] = (ds * bv * (sig * (1.0 + av * (1.0 - sig)))).astype(BF)
        db_ref[...] = (ds * (av * sig)).astype(BF)

    tile = pl.BlockSpec((tm, tf), lambda i, j: (i, j))
    hbm = pl.BlockSpec(memory_space=pl.ANY)
    outs = pl.pallas_call(
        body, name=name, grid=(ni, nf),
        in_specs=[_row_spec(tm, D), tile, tile, pl.BlockSpec((None, tf, D), lambda i, j: (2, j, 0))] + [hbm] * ng,
        out_specs=[tile, tile, _row_spec(tm, D)] + [hbm] * ng,
        out_shape=[jax.ShapeDtypeStruct((T, F), BF)] * 2 + [jax.ShapeDtypeStruct((T, D), BF)]
        + _scattered_shapes(scatter),
        scratch_shapes=_scatter_sems(ng), compiler_params=_params("arbitrary", "arbitrary"),
    )(dh, a, b, wffn, *[piece[0] for piece in scatter])
    return outs[0], outs[1], outs[2], list(outs[3:])


def _two_dot_norm_bwd(x1, x2, w, w_specs, h, g, dh, name, scatter=(), x_specs=None):
    T, K = x1.shape
    D = h.shape[1]
    tm = _tile(T, TM_FFN_IN, 16)
    ni, ng = T // tm, len(scatter)
    x_specs = x_specs or (lambda tm: [pl.BlockSpec((tm, K), lambda i: (i, 0))] * 2)

    def body(*refs):
        x1_ref, x2_ref, w1_ref, w2_ref, h_ref, g_ref, dh_ref = refs[:7]
        srcs, (o_ref, dg_ref), outs = refs[7:7 + ng], refs[7 + ng:9 + ng], refs[9 + ng:9 + 2 * ng]
        i = pl.program_id(0)
        if ng:
            _carry(_scatter_plan(scatter, srcs, outs, *refs[9 + 2 * ng:]), i == 0, None, i == ni - 1)

        @pl.when(i == 0)
        def _():
            dg_ref[...] = jnp.zeros_like(dg_ref)

        dn = _dot(x1_ref[...], w1_ref[...]) + _dot(x2_ref[...], w2_ref[...])
        dx, dg = _rms_bwd_parts(h_ref[...], g_ref[...], dn)
        o_ref[...] = dh_ref[...] + dx
        dg_ref[...] += dg

    hbm = pl.BlockSpec(memory_space=pl.ANY)
    outs = pl.pallas_call(
        body, name=name, grid=(ni,),
        in_specs=x_specs(tm) + w_specs + [_row_spec(tm, D), _vec_spec(D), _row_spec(tm, D)] + [hbm] * ng,
        out_specs=[_row_spec(tm, D), _vec_spec(D)] + [hbm] * ng,
        out_shape=[jax.ShapeDtypeStruct((T, D), F32), jax.ShapeDtypeStruct((1, D), F32)] + _scattered_shapes(scatter),
        scratch_shapes=_scatter_sems(ng), compiler_params=_params("arbitrary"),
    )(x1, x2, w, w, h, g.reshape(1, D), dh, *[piece[0] for piece in scatter])
    return outs[0], outs[1], list(outs[2:])


def _ffn_bwd_in(da, db, wffn, h, g, dh, name, scatter=()):
    F, D = wffn.shape[1:]
    specs = [pl.BlockSpec((None, F, D), functools.partial(lambda k, i: (k, 0, 0), k), pipeline_mode=pl.Buffered(1))
             for k in (0, 1)]
    return _two_dot_norm_bwd(da, db, wffn, specs, h, g, dh, name, scatter)


def _lru_in_bwd(dz, win, h, g, dh, name):
    R, D = win.shape[0] // 2, win.shape[1]
    specs = [pl.BlockSpec((R, D), functools.partial(lambda k, i: (k, 0), k), pipeline_mode=pl.Buffered(1)) for k in (0, 1)]
    halves = lambda tm: [pl.BlockSpec((tm, R), functools.partial(lambda k, i: (i, k), k)) for k in (0, 1)]
    return _two_dot_norm_bwd(dz, dz, win, specs, h, g, dh, name, x_specs=halves)[:2]


def _ffn_bwd_w(da, db, s, n, dhb, name, scatter=()):
    T, F = da.shape
    D = n.shape[1]
    tf, tk = _tile(F, TF_FFN_WG, LANES), _tile(T, TK_FFN_WG, 16)
    nj, nk, ng = F // tf, T // tk, len(scatter)

    def body(*refs):
        da_ref, db_ref, s_ref, n_ref, dh_ref = refs[:5]
        srcs, o_ref, outs = refs[5:5 + ng], refs[5 + ng], refs[6 + ng:6 + 2 * ng]
        g_sc, u_sc, d_sc = refs[6 + 2 * ng:9 + 2 * ng]
        j, k = pl.program_id(0), pl.program_id(1)
        if ng:
            _carry(_scatter_plan(scatter, srcs, outs, *refs[9 + 2 * ng:]), jnp.logical_and(j == 0, k == 0), None,
                   jnp.logical_and(j == nj - 1, k == nk - 1))

        @pl.when(k == 0)
        def _():
            g_sc[...] = jnp.zeros_like(g_sc)
            u_sc[...] = jnp.zeros_like(u_sc)
            d_sc[...] = jnp.zeros_like(d_sc)

        nv = n_ref[...]
        g_sc[...] += _dot_tn(da_ref[...], nv)
        u_sc[...] += _dot_tn(db_ref[...], nv)
        d_sc[...] += _dot_tn(s_ref[...], dh_ref[...])

        @pl.when(k == nk - 1)
        def _():
            o_ref[0] = g_sc[...].astype(BF)
            o_ref[1] = u_sc[...].astype(BF)
            o_ref[2] = (0.5 * d_sc[...]).astype(BF)

    act = pl.BlockSpec((tk, tf), lambda j, k: (k, j))
    tok = pl.BlockSpec((tk, D), lambda j, k: (k, 0))
    hbm = pl.BlockSpec(memory_space=pl.ANY)
    outs = pl.pallas_call(
        body, name=name, grid=(nj, nk), in_specs=[act, act, act, tok, tok] + [hbm] * ng,
        out_specs=[pl.BlockSpec((3, tf, D), lambda j, k: (0, j, 0), pipeline_mode=pl.Buffered(1))] + [hbm] * ng,
        out_shape=[jax.ShapeDtypeStruct((3, F, D), BF)] + _scattered_shapes(scatter),
        scratch_shapes=[pltpu.VMEM((tf, D), F32)] * 3 + _scatter_sems(ng),
        compiler_params=_params("arbitrary", "arbitrary"),
    )(da, db, s, n, dhb, *[piece[0] for piece in scatter])
    return outs[0], list(outs[1:])


def _ple_bwd(dh, gate, pp, h, g, wg, name):
    T, D = dh.shape
    tm = _tile(T, TM_EW, 16)

    def body(dh_ref, gate_ref, pp_ref, h_ref, g_ref, wg_ref, o_ref, dz_ref, dp_ref, dg_ref):
        @pl.when(pl.program_id(0) == 0)
        def _():
            dg_ref[...] = jnp.zeros_like(dg_ref)

        d, gate = dh_ref[...], gate_ref[...].astype(F32)
        dz = (d * pp_ref[...].astype(F32) * gate * (1.0 - gate)).astype(BF)
        dx, dg = _rms_bwd_parts(h_ref[...], g_ref[...], _dot_nt(dz, wg_ref[...]))
        o_ref[...] = d + dx
        dz_ref[...] = dz
        dp_ref[...] = (d * gate).astype(BF)
        dg_ref[...] += dg

    row = _row_spec(tm, D)
    return pl.pallas_call(
        body, name=name, grid=(T // tm,), in_specs=[row, row, row, row, _vec_spec(D), _vec_spec(D, D)],
        out_specs=[row, row, row, _vec_spec(D)],
        out_shape=[jax.ShapeDtypeStruct((T, D), F32), jax.ShapeDtypeStruct((T, D), BF), jax.ShapeDtypeStruct((T, D), BF),
                   jax.ShapeDtypeStruct((1, D), F32)],
        compiler_params=_params("arbitrary"),
    )(dh, gate, pp, h, g.reshape(1, D), wg)


def _lru_in(h, g, win, name):
    T, D = h.shape
    R = win.shape[0] // 2
    tm = _tile(T, TM_EW, 16)

    def body(h_ref, g_ref, w_ref, n_ref, gb_ref, xb_ref):
        n = _rms_parts(h_ref[...], g_ref[...])[0].astype(BF)
        n_ref[...] = n
        z = _dot_nt(n, w_ref[...])
        gb_ref[...] = z[:, :R].astype(BF)
        xb_ref[...] = z[:, R:]

    return pl.pallas_call(
        body, name=name, grid=(T // tm,),
        in_specs=[_row_spec(tm, D), _vec_spec(D), pl.BlockSpec((2 * R, D), lambda i: (0, 0), pipeline_mode=pl.Buffered(1))],
        out_specs=[_row_spec(tm, D), _row_spec(tm, R), _row_spec(tm, R)],
        out_shape=[jax.ShapeDtypeStruct((T, D), BF), jax.ShapeDtypeStruct((T, R), BF), jax.ShapeDtypeStruct((T, R), F32)],
        compiler_params=_params("parallel"),
    )(h, g.reshape(1, D), win)


def _lru_fwd(gb, xb, conv_w, conv_b, wa, wx, b_a, b_x, a_param, wout, h, name):
    T, R = xb.shape
    D = h.shape[1]
    tb = _tile(T, TB_SEQ, HALO)
    per, ng = tb // HALO, tb // SUBLANES

    def body(g_ref, x_ref, halo_ref, cw_ref, cb_ref, wa_ref, wx_ref, ba_ref, bx_ref, ap_ref, wo_ref, h_ref,
             o_ref, xc_ref, r_ref, ig_ref, a_ref, hs_ref, y_ref, ext, carry, a_sc, b_sc):
        i = pl.program_id(0)

        @pl.when(i == 0)
        def _():
            carry[...] = jnp.zeros_like(carry)

        ext[pl.ds(0, HALO), :] = jnp.where(i > 0, halo_ref[...], 0.0)
        ext[pl.ds(HALO, tb), :] = x_ref[...]
        xc = cb_ref[...] + cw_ref[0:1, :] * ext[pl.ds(HALO - 3, tb), :]
        for k in range(1, CONV_WIDTH):
            xc = xc + cw_ref[k:k + 1, :] * ext[pl.ds(HALO - 3 + k, tb), :]
        xcb = xc.astype(BF)
        r = _sigmoid_pos(_dot(xcb, wa_ref[...]) + ba_ref[...])
        ig = _sigmoid(_dot(xcb, wx_ref[...]) + bx_ref[...])
        la = -LRU_C * r * _softplus_neg(ap_ref[...])
        av = jnp.exp(la)
        xc_ref[...] = xc
        r_ref[...] = r
        ig_ref[...] = ig
        a_ref[...] = av
        A = av.reshape(ng, SUBLANES, R)
        B = (jnp.sqrt(_one_minus_sq(la, av)) * (ig * xc)).reshape(ng, SUBLANES, R)
        sub = lax.broadcasted_iota(jnp.int32, (1, SUBLANES, R), 1)
        for k in (1, 2, 4):
            m = sub >= k
            a_n = jnp.where(m, pltpu.roll(A, k, 1), 1.0)
            b_n = jnp.where(m, pltpu.roll(B, k, 1), 0.0)
            B = A * b_n + B
            A = A * a_n
        a_sc[...] = A.reshape(tb, R)
        b_sc[...] = B.reshape(tb, R)

        def group(q, c):
            rows = pl.ds(pl.multiple_of(q * SUBLANES, SUBLANES), SUBLANES)
            hg = a_sc[rows, :] * c + b_sc[rows, :]
            hs_ref[rows, :] = hg
            return hg[SUBLANES - 1:SUBLANES, :]

        carry[...] = lax.fori_loop(0, ng, group, carry[...])
        y = (hs_ref[...] * _gelu_parts(g_ref[...].astype(F32))[0]).astype(BF)
        y_ref[...] = y
        o_ref[...] = h_ref[...] + _dot(y, wo_ref[...])

    once = lambda rows, cols: pl.BlockSpec((rows, cols), lambda i: (0, 0), pipeline_mode=pl.Buffered(1))
    gate = pl.BlockSpec((tb, R), lambda i: (i, 0))
    halo = pl.BlockSpec((HALO, R), lambda i: (jnp.maximum(i * per - 1, 0), 0))
    return pl.pallas_call(
        body, name=name, grid=(T // tb,),
        in_specs=[gate, gate, halo, _vec_spec(R, CONV_WIDTH), _vec_spec(R), once(R, R), once(R, R), _vec_spec(R),
                  _vec_spec(R), _vec_spec(R), once(R, D), _row_spec(tb, D)],
        out_specs=[_row_spec(tb, D)] + [gate] * 6,
        out_shape=[jax.ShapeDtypeStruct((T, D), F32)] + [jax.ShapeDtypeStruct((T, R), F32)] * 5
        + [jax.ShapeDtypeStruct((T, R), BF)],
        scratch_shapes=[pltpu.VMEM((HALO + tb, R), F32), pltpu.VMEM((1, R), F32), pltpu.VMEM((tb, R), F32),
                        pltpu.VMEM((tb, R), F32)],
        compiler_params=_params("arbitrary"),
    )(gb, xb, xb, conv_w, conv_b.reshape(1, R), wa, wx, b_a.reshape(1, R), b_x.reshape(1, R), a_param.reshape(1, R), wout, h)


def _lru_bwd(dh, hs, gb, xb, a, r, ig, xc, wa, wx, wout, conv_w, a_param, name):
    T, R = hs.shape
    D = dh.shape[1]
    tb = _tile(T, TB_SEQ, HALO)
    per, nt, ng = tb // HALO, T // tb, tb // SUBLANES

    def body(dh_ref, h_ref, hp_ref, g_ref, x_ref, xp_ref, a_ref, r_ref, ig_ref, xc_ref, wa_ref, wx_ref, wo_ref, cw_ref,
             ap_ref, dz_ref, dpa_ref, dpx_ref, dsp_ref, dba_ref, dbx_ref, dcb_ref, dcw_ref,
             hext, xext, dext, carry, later, a_sc, b_sc, d_sc, l_sc):
        i = pl.program_id(0)

        @pl.when(i == 0)
        def _():
            for ref in (dsp_ref, dba_ref, dbx_ref, dcb_ref, dcw_ref, carry, later):
                ref[...] = jnp.zeros_like(ref)

        dy = _dot_nt(dh_ref[...].astype(BF), wo_ref[...])
        gl, dgl = _gelu_parts(g_ref[...].astype(F32))
        hv, av = h_ref[...], a_ref[...]
        dhd = dy * gl
        dz_ref[:, pl.ds(0, R)] = (dy * hv * dgl).astype(BF)
        d_sc[...] = dhd
        A = av.reshape(ng, SUBLANES, R)
        B = A * dhd.reshape(ng, SUBLANES, R)
        sub = lax.broadcasted_iota(jnp.int32, (1, SUBLANES, R), 1)
        for k in (1, 2, 4):
            m = sub < SUBLANES - k
            a_n = jnp.where(m, pltpu.roll(A, SUBLANES - k, 1), 1.0)
            b_n = jnp.where(m, pltpu.roll(B, SUBLANES - k, 1), 0.0)
            B = A * b_n + B
            A = A * a_n
        a_sc[...] = A.reshape(tb, R)
        b_sc[...] = B.reshape(tb, R)
        sub8 = lax.broadcasted_iota(jnp.int32, (SUBLANES, R), 0)

        def group(q, c):
            rows = pl.ds(pl.multiple_of((ng - 1 - q) * SUBLANES, SUBLANES), SUBLANES)
            mu = a_sc[rows, :] * c + b_sc[rows, :]
            l_sc[rows, :] = d_sc[rows, :] + jnp.where(sub8 == SUBLANES - 1, c, pltpu.roll(mu, SUBLANES - 1, 0))
            return mu[0:1, :]

        carry[...] = lax.fori_loop(0, ng, group, carry[...])
        lam = l_sc[...]
        hext[pl.ds(0, HALO), :] = jnp.where(i < nt - 1, hp_ref[...], 0.0)
        hext[pl.ds(HALO, tb), :] = hv
        h_prev = hext[pl.ds(HALO - 1, tb), :]
        rv, igv, xcv = r_ref[...], ig_ref[...], xc_ref[...]
        sp = _softplus_neg(ap_ref[...])
        mult = jnp.sqrt(_one_minus_sq(-LRU_C * rv * sp, av))
        dla = lam * h_prev * av - lam * (igv * xcv) * (av * av) / mult
        du = lam * mult
        dpa = (dla * (-LRU_C) * sp) * rv * (1.0 - rv)
        dpx = (du * xcv) * igv * (1.0 - igv)
        dsp_ref[...] += jnp.sum(dla * (-LRU_C) * rv, axis=0, keepdims=True)
        dba_ref[...] += jnp.sum(dpa, axis=0, keepdims=True)
        dbx_ref[...] += jnp.sum(dpx, axis=0, keepdims=True)
        dpab, dpxb = dpa.astype(BF), dpx.astype(BF)
        dpa_ref[...] = dpab
        dpx_ref[...] = dpxb
        dxc = du * igv + _dot_nt(dpab, wa_ref[...]) + _dot_nt(dpxb, wx_ref[...])
        dext[pl.ds(0, tb), :] = dxc
        dext[pl.ds(tb, SUBLANES), :] = later[...]
        later[...] = dxc[0:SUBLANES, :]
        xext[pl.ds(0, HALO), :] = jnp.where(i < nt - 1, xp_ref[...], 0.0)
        xext[pl.ds(HALO, tb), :] = x_ref[...]
        dxb = cw_ref[CONV_WIDTH - 1:CONV_WIDTH, :] * dxc
        for k in range(CONV_WIDTH - 1):
            dxb = dxb + cw_ref[k:k + 1, :] * dext[pl.ds(CONV_WIDTH - 1 - k, tb), :]
        dz_ref[:, pl.ds(R, R)] = dxb.astype(BF)
        for k in range(CONV_WIDTH):
            dcw_ref[k:k + 1, :] += jnp.sum(dxc * xext[pl.ds(HALO - 3 + k, tb), :], axis=0, keepdims=True)
        dcb_ref[...] += jnp.sum(dxc, axis=0, keepdims=True)

        @pl.when(i == nt - 1)
        def _():
            dsp_ref[...] = dsp_ref[...] * (-_sigmoid(-ap_ref[...]))

    once = lambda rows, cols: pl.BlockSpec((rows, cols), lambda i: (0, 0), pipeline_mode=pl.Buffered(1))
    t0 = pl.BlockSpec((tb, R), lambda i: (nt - 1 - i, 0))
    prev = pl.BlockSpec((HALO, R), lambda i: (jnp.maximum((nt - 1 - i) * per - 1, 0), 0))
    return pl.pallas_call(
        body, name=name, grid=(nt,),
        in_specs=[pl.BlockSpec((tb, D), lambda i: (nt - 1 - i, 0)), t0, prev, t0, t0, prev, t0, t0, t0, t0,
                  once(R, R), once(R, R), once(R, D), _vec_spec(R, CONV_WIDTH), _vec_spec(R)],
        out_specs=[pl.BlockSpec((tb, 2 * R), lambda i: (nt - 1 - i, 0)), t0, t0] + [_vec_spec(R)] * 4
        + [_vec_spec(R, SUBLANES)],
        out_shape=[jax.ShapeDtypeStruct((T, 2 * R), BF)] + [jax.ShapeDtypeStruct((T, R), BF)] * 2
        + [jax.ShapeDtypeStruct((1, R), F32)] * 4
        + [jax.ShapeDtypeStruct((SUBLANES, R), F32)],
        scratch_shapes=[pltpu.VMEM((HALO + tb, R), F32), pltpu.VMEM((HALO + tb, R), F32),
                        pltpu.VMEM((tb + SUBLANES, R), F32), pltpu.VMEM((1, R), F32), pltpu.VMEM((SUBLANES, R), F32)]
        + [pltpu.VMEM((tb, R), F32)] * 4,
        compiler_params=_params("arbitrary"),
    )(dh, hs, hs, gb, xb, xb, a, r, ig, xc, wa, wx, wout, conv_w, a_param.reshape(1, R))


def _gate_spans(R):
    d = R // LRU_HEADS
    spans = [min((j * GATE_COLS // d) * d // LANES * LANES, R - GATE_SPAN) for j in range(R // GATE_COLS)]
    assert R % GATE_COLS == 0 and all(lo + GATE_SPAN >= (((j + 1) * GATE_COLS - 1) // d + 1) * d for j, lo in enumerate(spans))
    return spans


def _lru_gates_dw(xc, dpa, dpx, name):
    T, R = xc.shape
    tk = _tile(T, 1024, 16)
    spans = _gate_spans(R)
    nb = len(spans)

    def body(x_ref, a_ref, b_ref, o_ref):
        @pl.when(pl.program_id(0) == 0)
        def _():
            o_ref[...] = jnp.zeros_like(o_ref)

        for j, lo in enumerate(spans):
            xs = x_ref[:, pl.ds(lo, GATE_SPAN)].astype(BF)
            cols = pl.ds(j * GATE_COLS, GATE_COLS)
            o_ref[0, j] += _dot_tn(xs, a_ref[:, cols])
            o_ref[1, j] += _dot_tn(xs, b_ref[:, cols])

    row = _row_spec(tk, R)
    out = pl.pallas_call(
        body, name=name, grid=(T // tk,), in_specs=[row, row, row],
        out_specs=pl.BlockSpec((2, nb, GATE_SPAN, GATE_COLS), lambda i: (0, 0, 0, 0)),
        out_shape=jax.ShapeDtypeStruct((2, nb, GATE_SPAN, GATE_COLS), F32), compiler_params=_params("arbitrary"),
    )(xc, dpa, dpx)
    dense = jnp.zeros((2, R, R), F32)
    for j, lo in enumerate(spans):
        dense = dense.at[:, lo:lo + GATE_SPAN, j * GATE_COLS:(j + 1) * GATE_COLS].set(out[:, j])
    return dense[0], dense[1]


def _window_sums(e, n, back):
    out, s = [], e
    for k in (1, 2, 4, 8):
        s = s + pltpu.roll(s, k if back else n - k, 0)
        out.append(s)
    return out


def _pool_fwd(h, g, w, b, scale, name):
    T, D = h.shape
    G = len(POOL_WINDOWS)
    gd = D // G
    tb = _tile(T, TB_SEQ, HALO)
    per = tb // HALO

    def body(h_ref, hp_ref, g_ref, w_ref, b_ref, s_ref, o_ref, u_ref, yb_ref):
        i = pl.program_id(0)
        t = i * tb + lax.broadcasted_iota(jnp.int32, (tb, gd), 0) + 1
        hv = h_ref[...]
        xn = _rms_parts(hv, g_ref[...])[0]
        xp = jnp.where(i > 0, _rms_parts(hp_ref[...], g_ref[...])[0], 0.0)
        for k, win in enumerate(POOL_WINDOWS):
            cols = slice(k * gd, (k + 1) * gd)
            x = xn[:, cols]
            e = jnp.concatenate([xp[:, cols], x], axis=0)
            sw = _window_sums(e, HALO + tb, True)[k][HALO:, :]
            u = (sw / jnp.minimum(t, win).astype(F32) - x).astype(BF)
            yb = _dot(u, w_ref[k]) + b_ref[:, cols]
            u_ref[:, cols] = u
            yb_ref[:, cols] = yb
            o_ref[:, cols] = hv[:, cols] + yb * s_ref[:, cols]

    tile = _row_spec(tb, D)
    prev = pl.BlockSpec((HALO, D), lambda i: (jnp.maximum(i * per - 1, 0), 0))
    return pl.pallas_call(
        body, name=name, grid=(T // tb,),
        in_specs=[tile, prev, _vec_spec(D), pl.BlockSpec((G, gd, gd), lambda i: (0, 0, 0)), _vec_spec(D), _vec_spec(D)],
        out_specs=[tile, tile, tile],
        out_shape=[jax.ShapeDtypeStruct((T, D), F32), jax.ShapeDtypeStruct((T, D), BF), jax.ShapeDtypeStruct((T, D), F32)],
        compiler_params=_params("parallel"),
    )(h, h, g.reshape(1, D), w, b.reshape(1, D), scale.reshape(1, D))


def _pool_bwd(dm, u, yb, w, scale, h, g, name):
    T, D = dm.shape
    G = len(POOL_WINDOWS)
    gd = D // G
    tb = _tile(T, TB_SEQ, HALO)
    nt = T // tb

    def body(d_ref, u_ref, yb_ref, w_ref, s_ref, h_ref, g_ref, o_ref, dw_ref, db_ref, ds_ref, dg_ref, later):
        i = pl.program_id(0)

        @pl.when(i == 0)
        def _():
            for ref in (dw_ref, db_ref, ds_ref, dg_ref, later):
                ref[...] = jnp.zeros_like(ref)

        d, sc = d_ref[...], s_ref[...]
        ds_ref[...] += jnp.sum(d * yb_ref[...], axis=0, keepdims=True)
        db_ref[...] += jnp.sum(d * sc, axis=0, keepdims=True)
        t = (nt - 1 - i) * tb + lax.broadcasted_iota(jnp.int32, (tb, gd), 0) + 1
        parts = []
        for k, win in enumerate(POOL_WINDOWS):
            cols = slice(k * gd, (k + 1) * gd)
            dy = (d[:, cols] * sc[:, cols]).astype(BF)
            du = _dot_nt(dy, w_ref[k])
            dw_ref[k] += _dot_tn(u_ref[:, cols], dy)
            v = du / jnp.minimum(t, win).astype(F32)
            e = jnp.concatenate([v, later[:, cols]], axis=0)
            later[:, cols] = v[0:HALO, :]
            parts.append(_window_sums(e, tb + HALO, False)[k][:tb, :] - du)
        dx, dg = _rms_bwd_parts(h_ref[...], g_ref[...], jnp.concatenate(parts, axis=1))
        o_ref[...] = d + dx
        dg_ref[...] += dg

    tile = pl.BlockSpec((tb, D), lambda i: (nt - 1 - i, 0))
    whole = pl.BlockSpec((G, gd, gd), lambda i: (0, 0, 0))
    return pl.pallas_call(
        body, name=name, grid=(nt,), in_specs=[tile, tile, tile, whole, _vec_spec(D), tile, _vec_spec(D)],
        out_specs=[tile, whole, _vec_spec(D), _vec_spec(D), _vec_spec(D)],
        out_shape=[jax.ShapeDtypeStruct((T, D), F32), jax.ShapeDtypeStruct((G, gd, gd), F32)]
        + [jax.ShapeDtypeStruct((1, D), F32)] * 3,
        scratch_shapes=[pltpu.VMEM((HALO, D), F32)], compiler_params=_params("arbitrary"),
    )(dm, u, yb, w, scale.reshape(1, D), h, g.reshape(1, D))


def _adamw(w, g, m, v, name):
    shape = w.shape
    cols = shape[-1] if w.ndim > 1 else shape[0]
    rows = w.size // cols
    tr = _tile(rows, TR_ADAM, SUBLANES)
    c1, c2 = 1.0 / (1.0 - ADAM_B1 ** ADAM_STEP), 1.0 / (1.0 - ADAM_B2 ** ADAM_STEP)

    def body(w_ref, g_ref, m_ref, v_ref, d_ref, mo_ref, vo_ref):
        gv = g_ref[...]
        mn = ADAM_B1 * m_ref[...] + (1.0 - ADAM_B1) * gv
        vn = ADAM_B2 * v_ref[...] + (1.0 - ADAM_B2) * (gv * gv)
        d_ref[...] = -ADAM_LR * ((mn * c1) / (jnp.sqrt(vn * c2) + ADAM_EPS) + ADAM_WD * w_ref[...])
        mo_ref[...] = mn
        vo_ref[...] = vn

    spec = _row_spec(tr, cols)
    outs = pl.pallas_call(
        body, name=name, grid=(rows // tr,), in_specs=[spec] * 4, out_specs=[spec] * 3,
        out_shape=[jax.ShapeDtypeStruct((rows, cols), F32)] * 3, compiler_params=_params("parallel"),
    )(*[t.reshape(rows, cols) for t in (w, g, m, v)])
    return [o.reshape(shape) for o in outs]


def _sum_devices(parts, name):
    n, rows, cols = parts.shape
    tr = _tile(rows, 1024, SUBLANES)

    def body(p_ref, o_ref):
        acc = p_ref[0].astype(F32)
        for k in range(1, n):
            acc = acc + p_ref[k].astype(F32)
        o_ref[...] = acc

    return pl.pallas_call(
        body, name=name, grid=(rows // tr,), in_specs=[pl.BlockSpec((n, tr, cols), lambda i: (0, i, 0))],
        out_specs=_row_spec(tr, cols), out_shape=jax.ShapeDtypeStruct((rows, cols), F32),
        compiler_params=_params("parallel"),
    )(parts)


def _position():
    return lax.axis_index("x"), lax.axis_index("y"), lax.axis_index("c")


def _gathered_shapes(blocks):
    return [jax.ShapeDtypeStruct((b.shape[0], N_DEV * b.shape[1], b.shape[2]), b.dtype) for b in blocks]


def _gather_sems(ng):
    return [pltpu.SemaphoreType.DMA((ng, 7)), pltpu.SemaphoreType.DMA((ng, 7)), pltpu.SemaphoreType.DMA((ng,))] if ng else []


def _gather_plan(blocks, srcs, outs, send_sems, recv_sems, local_sems):
    ng = len(blocks)
    x, y, c = _position()
    me, sibling = (x, y, c), (x, y, 1 - c)
    chips = [(1 - x, y), (x, 1 - y), (1 - x, 1 - y)]

    def rows(g, px, py, pc):
        r = blocks[g].shape[1]
        return outs[g].at[:, pl.ds((4 * px + 2 * py + pc) * r, r), :]

    def copy(g, k, block, to, src=None):
        return pltpu.make_async_remote_copy(
            src_ref=rows(g, *block) if src is None else src, dst_ref=rows(g, *block),
            send_sem=send_sems.at[g, k], recv_sem=recv_sems.at[g, k], device_id=to, device_id_type=MESH)

    def mine(g):
        return pltpu.make_async_copy(srcs[g], rows(g, *me), local_sems.at[g])

    def first(g):
        return [copy(g, 0, me, sibling, src=srcs[g])] + [copy(g, 1 + j, me, (*chip, c), src=srcs[g])
                                                         for j, chip in enumerate(chips)]

    def passed(g):
        return [copy(g, 4 + j, (*chip, c), sibling) for j, chip in enumerate(chips)]

    def start():
        for g in range(ng):
            mine(g).start()
            for cp in first(g):
                cp.start()

    def forward():
        for j, chip in enumerate(chips):
            for g in range(ng):
                copy(g, 1 + j, (*chip, c), me).wait_recv()
                copy(g, 4 + j, (*chip, c), sibling).start()

    def finish():
        for g in range(ng):
            copy(g, 0, sibling, me).wait_recv()
            for j, chip in enumerate(chips):
                copy(g, 4 + j, (*chip, 1 - c), me).wait_recv()
            for cp in first(g) + passed(g):
                cp.wait_send()
            mine(g).wait()

    return start, forward, finish


def _all_gather(blocks, name):
    ng = len(blocks)

    def body(*refs):
        start, forward, finish = _gather_plan(blocks, refs[:ng], refs[ng:2 * ng], *refs[2 * ng:])
        start()
        forward()
        finish()

    hbm = pl.BlockSpec(memory_space=pl.ANY)
    return pl.pallas_call(
        body, name=name, in_specs=[hbm] * ng, out_specs=[hbm] * ng, out_shape=_gathered_shapes(blocks),
        scratch_shapes=_gather_sems(ng),
    )(*blocks)


FLIPS = ((0, 0, 1), (1, 0, 0), (0, 1, 0), (1, 1, 0), (1, 0, 1), (0, 1, 1), (1, 1, 1))


def _piece_rows(piece):
    arr, m = piece
    return arr.shape[0] if m is None else 1


def _scattered_shapes(pieces):
    return [jax.ShapeDtypeStruct((N_DEV, _piece_rows(p), p[0].shape[1] // N_DEV, p[0].shape[2]), p[0].dtype)
            for p in pieces]


def _scatter_sems(ng):
    n = len(FLIPS)
    return [pltpu.SemaphoreType.DMA((ng, n)), pltpu.SemaphoreType.DMA((ng, n)), pltpu.SemaphoreType.DMA((ng,))] if ng else []


def _scatter_plan(pieces, srcs, outs, send_sems, recv_sems, local_sems):
    x, y, c = _position()

    def block(g, tx, ty, tc):
        arr, m = pieces[g]
        r = arr.shape[1] // N_DEV
        lead = slice(None) if m is None else pl.ds(m, 1)
        return srcs[g].at[lead, pl.ds((4 * tx + 2 * ty + tc) * r, r), :]

    def copies(g):
        out = []
        for k, (fx, fy, fc) in enumerate(FLIPS):
            tx, ty, tc = (1 - x if fx else x), (1 - y if fy else y), (1 - c if fc else c)
            out.append(pltpu.make_async_remote_copy(
                src_ref=block(g, tx, ty, tc), dst_ref=outs[g].at[k], send_sem=send_sems.at[g, k],
                recv_sem=recv_sems.at[g, k], device_id=(tx, ty, tc), device_id_type=MESH))
        return out

    def mine(g):
        return pltpu.make_async_copy(block(g, x, y, c), outs[g].at[len(FLIPS)], local_sems.at[g])

    def start():
        for g in range(len(pieces)):
            mine(g).start()
            for cp in copies(g):
                cp.start()

    def finish():
        for g in range(len(pieces)):
            for cp in copies(g):
                cp.wait()
            mine(g).wait()

    return start, finish


def _scatter_and_gather(pieces, blocks, name):
    n_p, n_b = len(pieces), len(blocks)

    def body(*refs):
        ins, outs, sems = refs[:n_p + n_b], refs[n_p + n_b:2 * (n_p + n_b)], refs[2 * (n_p + n_b):]
        s_start, s_finish = _scatter_plan(pieces, ins[:n_p], outs[:n_p], *sems[:3])
        g_start, g_forward, g_finish = _gather_plan(blocks, ins[n_p:], outs[n_p:], *sems[3:])
        s_start()
        g_start()
        g_forward()
        g_finish()
        s_finish()

    hbm = pl.BlockSpec(memory_space=pl.ANY)
    outs = pl.pallas_call(
        body, name=name, in_specs=[hbm] * (n_p + n_b), out_specs=[hbm] * (n_p + n_b),
        out_shape=_scattered_shapes(pieces) + _gathered_shapes(blocks),
        scratch_shapes=_scatter_sems(n_p) + _gather_sems(n_b),
    )(*[p[0] for p in pieces], *blocks)
    return list(outs[:n_p]), list(outs[n_p:])


def _scatter_sum(recv, name):
    _, n, r, c = recv.shape

    def body(r_ref, o_ref):
        acc = r_ref[len(FLIPS)].astype(F32)
        for k in range(len(FLIPS)):
            acc = acc + r_ref[k].astype(F32)
        o_ref[...] = acc

    return pl.pallas_call(
        body, name=name, grid=(n,), in_specs=[pl.BlockSpec((N_DEV, None, r, c), lambda i: (0, i, 0, 0))],
        out_specs=pl.BlockSpec((None, r, c), lambda i: (i, 0, 0)),
        out_shape=jax.ShapeDtypeStruct((n, r, c), F32), compiler_params=_params("parallel"),
    )(recv)


def _block_diag(w):
    H, d, _ = w.shape
    return (jnp.eye(H, dtype=w.dtype)[:, None, :, None] * w[:, :, None, :]).reshape(H * d, H * d)


def _diag_blocks(dense, H):
    d = dense.shape[0] // H
    return jnp.stack([dense[i * d:(i + 1) * d, i * d:(i + 1) * d] for i in range(H)])


def _local_step(x, p, tgt, W, blocks=None):
    dist = blocks is not None
    L = p.shape[0]
    W = dict(W)

    def gathering(keys):
        return [k for k in keys if k not in W] if dist else []

    def ffn_fwd(h, g, i, f, during_act, during_out, ple=None):
        w = W[("ffn", i, f)]
        keys = gathering(during_act)
        a, b, s, n, got = _ffn_fwd_act(h, g, w, f"ffn{f}_fwd_act_{i}", gather=[blocks[k] for k in keys])
        W.update(zip(keys, got))
        keys = gathering(during_out)
        if ple is not None:
            ple = (W["ple_norm"][i], W[("ple_gate", i)][0], W[("ple_proj", i)][0], ple)
        h, emb, got = _ffn_fwd_out(s, w, h, f"ffn{f}_fwd_out_{i}", gather=[blocks[k] for k in keys], ple=ple)
        W.update(zip(keys, got))
        return (a, b, s, n), h, emb

    saved = []
    h = x
    for i in range(L):
        j = i // 2
        lru = i % 2 == 0
        s = {"h0": h}
        mixer = [("lru_in", j), ("lru_out", j)] if lru else [("pool_w", j)]
        s["ffn1"], h, _ = ffn_fwd(h, W["ffn1_norm"][i], i, 1, [("ffn", i, 2)], mixer)
        s["h1"] = h
        if lru:
            hn, gb, xb = _lru_in(h, W["mix_norm"][i], W[("lru_in", j)][0], f"lru_in_{i}")
            wa, wx = _block_diag(W["lru_w_a"][j]).astype(BF), _block_diag(W["lru_w_x"][j]).astype(BF)
            h, xc, r, ig, a, hs, y = _lru_fwd(gb, xb, W["lru_conv_w"][j], W["lru_conv_b"][j], wa, wx, W["lru_b_a"][j],
                                              W["lru_b_x"][j], W["lru_a_param"][j], W[("lru_out", j)][0], h, f"lru_fwd_{i}")
            s.update(hn=hn, gb=gb, xb=xb, wa=wa, wx=wx, xc=xc, r=r, ig=ig, a=a, hs=hs, y=y)
        else:
            h, s["u"], s["yb"] = _pool_fwd(h, W["mix_norm"][i], W[("pool_w", j)], W["pool_b"][j], W["pool_scale"][j],
                                           f"pool_fwd_{i}")
        s["h2"] = h
        nxt = [("ffn", i + 1, 1)] if i + 1 < L else []
        s["ffn2"], s["h3"], (h, s["n4"], s["gate"], s["pp"]) = ffn_fwd(
            h, W["ffn2_norm"][i], i, 2, [("ple_gate", i), ("ple_proj", i)] + nxt, [], ple=p[i])
        saved.append(s)

    loss, dh, d_final = _loss_head(h, W["final_norm"], tgt)

    big, recv = {}, {}
    n_lru, n_pool = L // 2 + L % 2, L // 2
    small = {k: [None] * L for k in ("ffn1_norm", "mix_norm", "ffn2_norm", "ple_norm")}
    for k in ("lru_conv_w", "lru_conv_b", "lru_w_a", "lru_b_a", "lru_w_x", "lru_b_x", "lru_a_param"):
        small[k] = [None] * n_lru
    for k in ("pool_b", "pool_scale"):
        small[k] = [None] * n_pool

    def scattering(pieces):
        return [(k, m) for k, m in pieces if k in big] if dist else []

    def ffn_bwd(dh, h_in, g, acts, i, f, during):
        key, w = ("ffn", i, f), W[("ffn", i, f)]
        a, b, sv, n = acts
        out = [scattering(d) for d in during]
        sent = [[(big[k], m) for k, m in o] for o in out]
        da, db, dhb, got0 = _ffn_bwd_act(dh, a, b, w, f"ffn{f}_bwd_act_{i}", scatter=sent[0])
        big[key], got1 = _ffn_bwd_w(da, db, sv, n, dhb, f"ffn{f}_dw_{i}", scatter=sent[1])
        out.append(scattering([(key, 0)] + ([(key, 1)] if (i, f) == (0, 1) else [])))
        dh, dg, got2 = _ffn_bwd_in(da, db, w, h_in, g, dh, f"ffn{f}_bwd_in_{i}", scatter=[(big[k], m) for k, m in out[2]])
        for o, got in zip(out, (got0, got1, got2)):
            recv.update(zip(o, got))
        return dh, dg

    for i in reversed(range(L)):
        j = i // 2
        lru = i % 2 == 0
        s = saved[i]
        dh, dz, dpp, dg = _ple_bwd(dh, s["gate"], s["pp"], s["h3"], W["ple_norm"][i], W[("ple_gate", i)][0],
                                   f"ple_bwd_{i}")
        big[("ple_gate", i)] = _mm(s["n4"], dz, "tn", f"ple_gate_dw_{i}", out_dtype=BF)[None]
        big[("ple_proj", i)] = _mm(dpp, p[i], "tn", f"ple_proj_dw_{i}", out_dtype=BF)[None]
        small["ple_norm"][i] = dg[0]
        above = ("ffn", i + 1, 1)
        dh, dg = ffn_bwd(dh, s["h2"], W["ffn2_norm"][i], s["ffn2"], i, 2, [
            [(above, 1)], [(above, 2), (("ple_gate", i), None), (("ple_proj", i), None)]])
        small["ffn2_norm"][i] = dg[0]
        if lru:
            big[("lru_out", j)] = _mm(s["y"], dh, "tn", f"lru_out_dw_{i}", out_dtype=BF)[None]
            dz, dpa, dpx, dsp, dba, dbx, dcb, dcw = _lru_bwd(
                dh, s["hs"], s["gb"], s["xb"], s["a"], s["r"], s["ig"], s["xc"], s["wa"], s["wx"], W[("lru_out", j)][0],
                W["lru_conv_w"][j], W["lru_a_param"][j], f"lru_bwd_{i}")
            small["lru_a_param"][j], small["lru_b_a"][j], small["lru_b_x"][j] = dsp[0], dba[0], dbx[0]
            dwa, dwx = _lru_gates_dw(s["xc"], dpa, dpx, f"lru_gates_dw_{i}")
            small["lru_w_a"][j], small["lru_w_x"][j] = _diag_blocks(dwa, LRU_HEADS), _diag_blocks(dwx, LRU_HEADS)
            small["lru_conv_w"][j], small["lru_conv_b"][j] = dcw[:CONV_WIDTH], dcb[0]
            big[("lru_in", j)] = _mm(dz, s["hn"], "tn", f"lru_in_dw_{i}", out_dtype=BF)[None]
            dh, dg = _lru_in_bwd(dz, W[("lru_in", j)][0], s["h1"], W["mix_norm"][i], dh, f"lru_in_bwd_{i}")
            mixer = [("lru_in", j), ("lru_out", j)]
        else:
            dh, dw, dbp, dsc, dg = _pool_bwd(dh, s["u"], s["yb"], W[("pool_w", j)], W["pool_scale"][j], s["h1"],
                                             W["mix_norm"][i], f"pool_bwd_{i}")
            big[("pool_w", j)] = dw.astype(BF)
            small["pool_b"][j], small["pool_scale"][j] = dbp[0], dsc[0]
            mixer = [("pool_w", j)]
        small["mix_norm"][i] = dg[0]
        second = ("ffn", i, 2)
        dh, dg = ffn_bwd(dh, s["h0"], W["ffn1_norm"][i], s["ffn1"], i, 1, [
            [(second, 1)], [(second, 2)] + [(k, None) for k in mixer]])
        small["ffn1_norm"][i] = dg[0]

    small = {k: jnp.stack(v) for k, v in small.items()}
    small["final_norm"] = d_final[0]
    return loss, dh, big, recv, small


SMALL_SHARDED = ("pool_b", "pool_scale", "lru_conv_w")
SMALL = ("ffn1_norm", "mix_norm", "ffn2_norm", "ple_norm", "final_norm", "lru_conv_b", "lru_w_a", "lru_b_a",
         "lru_w_x", "lru_b_x", "lru_a_param", "pool_b", "pool_scale", "lru_conv_w")


def _pack_big(w):
    t = lambda a: jnp.swapaxes(a, -1, -2)
    out = {}
    for i in range(w["ffn1_norm"].shape[0]):
        for f in (1, 2):
            out[("ffn", i, f)] = jnp.stack([t(w[f"ffn{f}_w_gate"][i]), t(w[f"ffn{f}_w_up"][i]), w[f"ffn{f}_w_down"][i]])
        out[("ple_gate", i)], out[("ple_proj", i)] = w["ple_w_gate"][i][None], t(w["ple_w_proj"][i])[None]
    for j in range(w["lru_w_in"].shape[0]):
        out[("lru_in", j)], out[("lru_out", j)] = t(w["lru_w_in"][j])[None], w["lru_w_out"][j][None]
    for j in range(w["pool_w"].shape[0]):
        out[("pool_w", j)] = w["pool_w"][j]
    return out


def _unpack_big(b, L):
    t = lambda a: jnp.swapaxes(a, -1, -2)
    n_lru, n_pool = L // 2 + L % 2, L // 2
    out = {"lru_w_in": jnp.stack([t(b[("lru_in", j)][0]) for j in range(n_lru)]),
           "lru_w_out": jnp.stack([b[("lru_out", j)][0] for j in range(n_lru)]),
           "pool_w": jnp.stack([b[("pool_w", j)] for j in range(n_pool)]),
           "ple_w_gate": jnp.stack([b[("ple_gate", i)][0] for i in range(L)]),
           "ple_w_proj": jnp.stack([t(b[("ple_proj", i)][0]) for i in range(L)])}
    for f in (1, 2):
        out[f"ffn{f}_w_gate"] = jnp.stack([t(b[("ffn", i, f)][0]) for i in range(L)])
        out[f"ffn{f}_w_up"] = jnp.stack([t(b[("ffn", i, f)][1]) for i in range(L)])
        out[f"ffn{f}_w_down"] = jnp.stack([b[("ffn", i, f)][2] for i in range(L)])
    return out


def _flatten(parts, names, rows_of=LANES):
    flat = jnp.concatenate([parts[k].reshape(-1) for k in names])
    pad = (-flat.size) % (16 * rows_of)
    return jnp.pad(flat, (0, pad)).reshape(1, -1, rows_of)


def _unflatten(flat, like, names):
    out, o = {}, 0
    flat = flat.reshape(-1)
    for k in names:
        n = like[k].size
        out[k] = flat[o:o + n].reshape(like[k].shape)
        o += n
    return out


def kernel(x, p, ffn1_norm, ffn1_w_gate, ffn1_w_up, ffn1_w_down, mix_norm, lru_w_in, lru_conv_w, lru_conv_b, lru_w_a, lru_b_a, lru_w_x, lru_b_x, lru_a_param, lru_w_out, pool_w, pool_b, pool_scale, ffn2_norm, ffn2_w_gate, ffn2_w_up, ffn2_w_down, ple_norm, ple_w_gate, ple_w_proj, final_norm, loss_target, m_ffn1_norm, m_ffn1_w_gate, m_ffn1_w_up, m_ffn1_w_down, m_mix_norm, m_lru_w_in, m_lru_conv_w, m_lru_conv_b, m_lru_w_a, m_lru_b_a, m_lru_w_x, m_lru_b_x, m_lru_a_param, m_lru_w_out, m_pool_w, m_pool_b, m_pool_scale, m_ffn2_norm, m_ffn2_w_gate, m_ffn2_w_up, m_ffn2_w_down, m_ple_norm, m_ple_w_gate, m_ple_w_proj, m_final_norm, v_ffn1_norm, v_ffn1_w_gate, v_ffn1_w_up, v_ffn1_w_down, v_mix_norm, v_lru_w_in, v_lru_conv_w, v_lru_conv_b, v_lru_w_a, v_lru_b_a, v_lru_w_x, v_lru_b_x, v_lru_a_param, v_lru_w_out, v_pool_w, v_pool_b, v_pool_scale, v_ffn2_norm, v_ffn2_w_gate, v_ffn2_w_up, v_ffn2_w_down, v_ple_norm, v_ple_w_gate, v_ple_w_proj, v_final_norm):
    names = ["ffn1_norm", "ffn1_w_gate", "ffn1_w_up", "ffn1_w_down", "mix_norm", "lru_w_in", "lru_conv_w", "lru_conv_b",
             "lru_w_a", "lru_b_a", "lru_w_x", "lru_b_x", "lru_a_param", "lru_w_out", "pool_w", "pool_b", "pool_scale",
             "ffn2_norm", "ffn2_w_gate", "ffn2_w_up", "ffn2_w_down", "ple_norm", "ple_w_gate", "ple_w_proj", "final_norm"]
    w = dict(zip(names, [ffn1_norm, ffn1_w_gate, ffn1_w_up, ffn1_w_down, mix_norm, lru_w_in, lru_conv_w, lru_conv_b, lru_w_a, lru_b_a, lru_w_x, lru_b_x, lru_a_param, lru_w_out, pool_w, pool_b, pool_scale, ffn2_norm, ffn2_w_gate, ffn2_w_up, ffn2_w_down, ple_norm, ple_w_gate, ple_w_proj, final_norm]))
    m = dict(zip(names, [m_ffn1_norm, m_ffn1_w_gate, m_ffn1_w_up, m_ffn1_w_down, m_mix_norm, m_lru_w_in, m_lru_conv_w, m_lru_conv_b, m_lru_w_a, m_lru_b_a, m_lru_w_x, m_lru_b_x, m_lru_a_param, m_lru_w_out, m_pool_w, m_pool_b, m_pool_scale, m_ffn2_norm, m_ffn2_w_gate, m_ffn2_w_up, m_ffn2_w_down, m_ple_norm, m_ple_w_gate, m_ple_w_proj, m_final_norm]))
    v = dict(zip(names, [v_ffn1_norm, v_ffn1_w_gate, v_ffn1_w_up, v_ffn1_w_down, v_mix_norm, v_lru_w_in, v_lru_conv_w, v_lru_conv_b, v_lru_w_a, v_lru_b_a, v_lru_w_x, v_lru_b_x, v_lru_a_param, v_lru_w_out, v_pool_w, v_pool_b, v_pool_scale, v_ffn2_norm, v_ffn2_w_gate, v_ffn2_w_up, v_ffn2_w_down, v_ple_norm, v_ple_w_gate, v_ple_w_proj, v_final_norm]))
    L = p.shape[0]
    px, py, pc = _position()
    me = 4 * px + 2 * py + pc

    blocks = {k: b.astype(BF) for k, b in _pack_big(w).items()}
    first = ("ffn", 0, 1)
    got, small_blocks = _all_gather([blocks[first], _flatten(w, SMALL_SHARDED)], "gather_first")
    W = {first: got}
    per_dev = small_blocks.reshape(N_DEV, -1)
    shards = [_unflatten(per_dev[k], w, SMALL_SHARDED) for k in range(N_DEV)]
    for k in SMALL:
        W[k] = jnp.concatenate([s[k] for s in shards], axis=-1) if k in SMALL_SHARDED else w[k]

    loss, dx, big, recv, small = _local_step(x[0], p[:, 0], loss_target[0], W, blocks)

    last = [(k, m) for k in big if (k, None) not in recv for m in range(big[k].shape[0]) if (k, m) not in recv]
    got, (parts,) = _scatter_and_gather([(big[k], m) for k, m in last], [_flatten(small, SMALL).astype(BF)],
                                        "scatter_last_gather_small")
    recv.update(zip(last, got))

    def total(k):
        tag = "sum_" + "_".join(map(str, k))
        if (k, None) in recv:
            return _scatter_sum(recv[(k, None)], tag)
        return jnp.concatenate([_scatter_sum(recv[(k, m)], f"{tag}_{m}") for m in range(big[k].shape[0])])

    grads = _unpack_big({k: total(k) for k in big}, L)
    total_small = _sum_devices(parts.reshape(N_DEV, -1, LANES), "sum_small_grads")
    full = _unflatten(total_small, {k: W[k] for k in SMALL}, SMALL)
    for k in SMALL:
        if k in SMALL_SHARDED:
            n = w[k].shape[-1]
            grads[k] = lax.dynamic_slice_in_dim(full[k], me * n, n, axis=-1)
        else:
            grads[k] = full[k]

    delta, new_m, new_v = {}, {}, {}
    for k in names:
        delta[k], new_m[k], new_v[k] = _adamw(w[k], grads[k], m[k], v[k], f"adamw_{k}")
    total_loss = lax.psum(loss[0, 0], ("x", "y", "c"))
    return (total_loss, dx[None], *[grads[k] for k in names], *[delta[k] for k in names],
            *[new_m[k] for k in names], *[new_v[k] for k in names])
```

```python
import functools

import jax
import jax.numpy as jnp
from jax import lax
from jax.experimental import pallas as pl
from jax.experimental.pallas import tpu as pltpu

F32 = jnp.float32
BF = jnp.bfloat16
MESH = pl.DeviceIdType.MESH

RMS_EPS = 1e-6
LRU_C = 8.0
LRU_HEADS = 16
CONV_WIDTH = 4
POOL_WINDOWS = (2, 4, 8, 16)
ADAM_LR, ADAM_B1, ADAM_B2, ADAM_EPS, ADAM_WD, ADAM_STEP = 0.001, 0.9, 0.999, 1e-08, 0.01, 10

N_DEV = 8
LANES = 128
SUBLANES = 8
GATE_COLS, GATE_SPAN = 256, 512
HALO = 16
VMEM_LIMIT = 56 * 1024 * 1024

TM_FFN = 1024
TM_FFN_ACT = 2048
TM_FFN_IN = 512
TF_FFN = 256
TF_FFN_WG = 1408
TK_FFN_WG = 512
TB_SEQ = 256
TM_EW = 512
TM_MM, TN_MM, TK_MM = 1280, 1024, 512
TR_ADAM = 512


def _tile(n, pref, align):
    if n <= pref:
        return n
    t = (pref // align) * align
    while t >= align:
        if n % t == 0:
            return t
        t -= align
    raise ValueError(f"no tile for {n} (pref {pref}, align {align})")


def _params(*sem):
    return pltpu.CompilerParams(dimension_semantics=sem, vmem_limit_bytes=VMEM_LIMIT)


def _dot(a, b):
    return lax.dot_general(a, b, (((1,), (0,)), ((), ())), preferred_element_type=F32)


def _dot_nt(a, b):
    return lax.dot_general(a, b, (((1,), (1,)), ((), ())), preferred_element_type=F32)


def _dot_tn(a, b):
    return lax.dot_general(a, b, (((0,), (0,)), ((), ())), preferred_element_type=F32)


def _sigmoid(x):
    return 0.5 + 0.5 * jnp.tanh(0.5 * x)


def _sigmoid_pos(x):
    return 1.0 / (1.0 + jnp.exp(-x))


def _gelu_parts(x):
    k0, k1 = 0.7978845608028654, 0.044715
    t = jnp.tanh(k0 * (x + k1 * x * x * x))
    g = 0.5 * x * (1.0 + t)
    dg = 0.5 * (1.0 + t) + 0.5 * x * (1.0 - t * t) * k0 * (1.0 + 3.0 * k1 * x * x)
    return g, dg


def _one_minus_sq(la, a):
    return jnp.tanh(-la) * (1.0 + a * a)


def _softplus_neg(l):
    u = jnp.exp(-jnp.abs(l))
    w = 1.0 + u
    log1p = jnp.where(w == 1.0, u, jnp.log(w) * (u / jnp.where(w == 1.0, 1.0, w - 1.0)))
    return jnp.maximum(-l, 0.0) + log1p


def _rms_parts(x, g):
    r = lax.rsqrt(jnp.mean(x * x, axis=-1, keepdims=True) + RMS_EPS)
    nhat = x * r
    return nhat * g, nhat, r


def _rms_bwd_parts(x, g, dn):
    _, nhat, r = _rms_parts(x, g)
    u = dn * g
    dx = r * (u - nhat * jnp.mean(u * nhat, axis=-1, keepdims=True))
    return dx, jnp.sum(dn * nhat, axis=0, keepdims=True)


def _row_spec(tm, d, single=False):
    if single:
        return pl.BlockSpec((tm, d), lambda i, *_: (i, 0), pipeline_mode=pl.Buffered(1))
    return pl.BlockSpec((tm, d), lambda i, *_: (i, 0))


def _vec_spec(d, rows=1):
    return pl.BlockSpec((rows, d), lambda *_: (0, 0))


def _mm_tn(x, w, name):
    (K, M), (_, N) = x.shape, w.shape
    tm, tn, tk = _tile(M, TM_MM, LANES), _tile(N, TN_MM, LANES), _tile(K, TK_MM, 16)
    nk = K // tk

    def body(x_ref, w_ref, o_ref, acc):
        k = pl.program_id(2)

        @pl.when(k == 0)
        def _():
            acc[...] = jnp.zeros_like(acc)

        acc[...] += _dot_tn(x_ref[...].astype(BF), w_ref[...].astype(BF))

        @pl.when(k == nk - 1)
        def _():
            o_ref[...] = acc[...].astype(BF)

    return pl.pallas_call(
        body, name=name, grid=(M // tm, N // tn, nk),
        in_specs=[pl.BlockSpec((tk, tm), lambda i, j, k: (k, i)), pl.BlockSpec((tk, tn), lambda i, j, k: (k, j))],
        out_specs=pl.BlockSpec((tm, tn), lambda i, j, k: (i, j)),
        out_shape=jax.ShapeDtypeStruct((M, N), BF),
        scratch_shapes=[pltpu.VMEM((tm, tn), F32)],
        compiler_params=_params("parallel", "parallel", "arbitrary"),
    )(x, w)


def _loss_head(h, g, tgt):
    T, D = h.shape
    tm = _tile(T, TM_EW, 16)

    def body(h_ref, g_ref, t_ref, loss_ref, dh_ref, dg_ref):
        @pl.when(pl.program_id(0) == 0)
        def _():
            dg_ref[...] = jnp.zeros_like(dg_ref)
            loss_ref[...] = jnp.zeros_like(loss_ref)

        x, gg = h_ref[...], g_ref[...]
        y = _rms_parts(x, gg)[0]
        e = y - t_ref[...]
        part = jnp.sum(jnp.sum(e * e, axis=0, keepdims=True), axis=1, keepdims=True) * (0.5 / D)
        loss_ref[...] += jnp.broadcast_to(part, loss_ref.shape)
        dx, dg = _rms_bwd_parts(x, gg, e * (1.0 / D))
        dh_ref[...] = dx
        dg_ref[...] += dg

    return pl.pallas_call(
        body, name="loss_head", grid=(T // tm,),
        in_specs=[_row_spec(tm, D), _vec_spec(D), _row_spec(tm, D)],
        out_specs=[_vec_spec(LANES), _row_spec(tm, D), _vec_spec(D)],
        out_shape=[jax.ShapeDtypeStruct((1, LANES), F32), jax.ShapeDtypeStruct((T, D), F32),
                   jax.ShapeDtypeStruct((1, D), F32)],
        compiler_params=_params("arbitrary"),
    )(h, g.reshape(1, D), tgt)


def _carry(plan, first, mid, last):
    pl.when(first)(plan[0])
    if len(plan) == 3:
        pl.when(mid)(plan[1])
    pl.when(last)(plan[-1])


def _ffn_fwd_act(h, g, wffn, name, gather=()):
    T, D = h.shape
    F = wffn.shape[1]
    tm, tf = _tile(T, TM_FFN_ACT, 16), _tile(F, TF_FFN, LANES)
    ni, nf, ng = T // tm, F // tf, len(gather)

    def body(*refs):
        h_ref, g_ref, wg_ref, wu_ref = refs[:4]
        srcs, (a_ref, b_ref, s_ref, n_ref), outs = refs[4:4 + ng], refs[4 + ng:8 + ng], refs[8 + ng:8 + 2 * ng]
        i, j = pl.program_id(0), pl.program_id(1)
        if ng:
            _carry(_gather_plan(gather, srcs, outs, *refs[8 + 2 * ng:]), jnp.logical_and(i == 0, j == 0),
                   jnp.logical_and(i == (3 * ni) // 4, j == 0), jnp.logical_and(i == ni - 1, j == nf - 1))

        @pl.when(j == 0)
        def _():
            n_ref[...] = _rms_parts(h_ref[...], g_ref[...])[0].astype(BF)

        n = n_ref[...]
        a = _dot_nt(n, wg_ref[...])
        b = _dot_nt(n, wu_ref[...])
        a_ref[...] = a.astype(BF)
        b_ref[...] = b.astype(BF)
        s_ref[...] = (a * _sigmoid(a) * b).astype(BF)

    tile = pl.BlockSpec((tm, tf), lambda i, j: (i, j))
    w = [pl.BlockSpec((None, tf, D), functools.partial(lambda k, i, j: (k, j, 0), k)) for k in (0, 1)]
    hbm = pl.BlockSpec(memory_space=pl.ANY)
    outs = pl.pallas_call(
        body, name=name, grid=(ni, nf), in_specs=[_row_spec(tm, D), _vec_spec(D)] + w + [hbm] * ng,
        out_specs=[tile, tile, tile, _row_spec(tm, D)] + [hbm] * ng,
        out_shape=[jax.ShapeDtypeStruct((T, F), BF)] * 3 + [jax.ShapeDtypeStruct((T, D), BF)] + _gathered_shapes(gather),
        scratch_shapes=_gather_sems(ng), compiler_params=_params("arbitrary", "arbitrary"),
    )(h, g.reshape(1, D), wffn, wffn, *gather)
    return outs[0], outs[1], outs[2], outs[3], list(outs[4:])


def _ffn_fwd_out(s, wffn, h, name, gather=(), ple=None):
    T, F = s.shape
    D = h.shape[1]
    tm = _tile(T, TM_FFN_IN, 16)
    ni, ng, ne = T // tm, len(gather), 4 if ple else 0

    def body(*refs):
        s_ref, w_ref, h_ref = refs[:3]
        pin, rest = refs[3:3 + ne], refs[3 + ne:]
        srcs, o_ref, pout, outs = rest[:ng], rest[ng], rest[ng + 1:ng + 1 + ne], rest[ng + 1 + ne:2 * ng + 1 + ne]
        i = pl.program_id(0)
        if ng:
            _carry(_gather_plan(gather, srcs, outs, *rest[2 * ng + 1 + ne:]), i == 0, i == (3 * ni) // 4, i == ni - 1)
        x = h_ref[...] + 0.5 * _dot(s_ref[...], w_ref[...])
        o_ref[...] = x
        if ple:
            g_ref, wg_ref, wp_ref, p_ref = pin
            e_ref, n_ref, gate_ref, pp_ref = pout
            n = _rms_parts(x, g_ref[...])[0].astype(BF)
            gate = _sigmoid(_dot(n, wg_ref[...]))
            pp = _dot_nt(p_ref[...].astype(BF), wp_ref[...])
            e_ref[...] = x + gate * pp
            n_ref[...] = n
            gate_ref[...] = gate.astype(BF)
            pp_ref[...] = pp.astype(BF)

    hbm = pl.BlockSpec(memory_space=pl.ANY)
    row = _row_spec(tm, D)
    once = lambda rows, cols: pl.BlockSpec((rows, cols), lambda i: (0, 0), pipeline_mode=pl.Buffered(1))
    extra_in, extra_out, extra_shape, extra_args = [], [], [], []
    if ple:
        g, wg, wp, p = ple
        P = p.shape[1]
        extra_in = [_vec_spec(D), once(D, D), once(D, P), _row_spec(tm, P)]
        extra_out = [row] * 4
        extra_shape = [jax.ShapeDtypeStruct((T, D), F32)] + [jax.ShapeDtypeStruct((T, D), BF)] * 3
        extra_args = [g.reshape(1, D), wg, wp, p]
    outs = pl.pallas_call(
        body, name=name, grid=(ni,),
        in_specs=[_row_spec(tm, F), pl.BlockSpec((None, F, D), lambda i: (2, 0, 0), pipeline_mode=pl.Buffered(1)), row]
        + extra_in + [hbm] * ng,
        out_specs=[row] + extra_out + [hbm] * ng,
        out_shape=[jax.ShapeDtypeStruct((T, D), F32)] + extra_shape + _gathered_shapes(gather),
        scratch_shapes=_gather_sems(ng), compiler_params=_params("arbitrary"),
    )(s, wffn, h, *extra_args, *gather)
    return outs[0], list(outs[1:1 + ne]), list(outs[1 + ne:])


def _ffn_bwd_act(dh, a, b, wffn, name, scatter=()):
    T, D = dh.shape
    F = wffn.shape[1]
    tm, tf = _tile(T, TM_FFN_ACT, 16), _tile(F, TF_FFN, LANES)
    ni, nf, ng = T // tm, F // tf, len(scatter)

    def body(*refs):
        dh_ref, a_ref, b_ref, wd_ref = refs[:4]
        srcs, (da_ref, db_ref, dhb_ref), outs = refs[4:4 + ng], refs[4 + ng:7 + ng], refs[7 + ng:7 + 2 * ng]
        i, j = pl.program_id(0), pl.program_id(1)
        if ng:
            _carry(_scatter_plan(scatter, srcs, outs, *refs[7 + 2 * ng:]), jnp.logical_and(i == 0, j == 0), None,
                   jnp.logical_and(i == ni - 1, j == nf - 1))

        @pl.when(j == 0)
        def _():
            dhb_ref[...] = dh_ref[...].astype(BF)

        ds = 0.5 * _dot_nt(dhb_ref[...], wd_ref[...])
        av, bv = a_ref[...].astype(F32), b_ref[...].astype(F32)
        sig = _sigmoid(av)
        da_ref[...] = (ds * bv * (sig * (1.0 + av * (1.0 - sig)))).astype(BF)
        db_ref[...] = (ds * (av * sig)).astype(BF)

    tile = pl.BlockSpec((tm, tf), lambda i, j: (i, j))
    hbm = pl.BlockSpec(memory_space=pl.ANY)
    outs = pl.pallas_call(
        body, name=name, grid=(ni, nf),
        in_specs=[_row_spec(tm, D), tile, tile, pl.BlockSpec((None, tf, D), lambda i, j: (2, j, 0))] + [hbm] * ng,
        out_specs=[tile, tile, _row_spec(tm, D)] + [hbm] * ng,
        out_shape=[jax.ShapeDtypeStruct((T, F), BF)] * 2 + [jax.ShapeDtypeStruct((T, D), BF)]
        + _scattered_shapes(scatter),
        scratch_shapes=_scatter_sems(ng), compiler_params=_params("arbitrary", "arbitrary"),
    )(dh, a, b, wffn, *[piece[0] for piece in scatter])
    return outs[0], outs[1], outs[2], list(outs[3:])


def _two_dot_norm_bwd(x1, x2, w, w_specs, h, g, dh, name, scatter=(), x_specs=None):
    T, K = x1.shape
    D = h.shape[1]
    tm = _tile(T, TM_FFN_IN, 16)
    ni, ng = T // tm, len(scatter)
    x_specs = x_specs or (lambda tm: [pl.BlockSpec((tm, K), lambda i: (i, 0))] * 2)

    def body(*refs):
        x1_ref, x2_ref, w1_ref, w2_ref, h_ref, g_ref, dh_ref = refs[:7]
        srcs, (o_ref, dg_ref), outs = refs[7:7 + ng], refs[7 + ng:9 + ng], refs[9 + ng:9 + 2 * ng]
        i = pl.program_id(0)
        if ng:
            _carry(_scatter_plan(scatter, srcs, outs, *refs[9 + 2 * ng:]), i == 0, None, i == ni - 1)

        @pl.when(i == 0)
        def _():
            dg_ref[...] = jnp.zeros_like(dg_ref)

        dn = _dot(x1_ref[...], w1_ref[...]) + _dot(x2_ref[...], w2_ref[...])
        dx, dg = _rms_bwd_parts(h_ref[...], g_ref[...], dn)
        o_ref[...] = dh_ref[...] + dx
        dg_ref[...] += dg

    hbm = pl.BlockSpec(memory_space=pl.ANY)
    outs = pl.pallas_call(
        body, name=name, grid=(ni,),
        in_specs=x_specs(tm) + w_specs + [_row_spec(tm, D), _vec_spec(D), _row_spec(tm, D)] + [hbm] * ng,
        out_specs=[_row_spec(tm, D), _vec_spec(D)] + [hbm] * ng,
        out_shape=[jax.ShapeDtypeStruct((T, D), F32), jax.ShapeDtypeStruct((1, D), F32)] + _scattered_shapes(scatter),
        scratch_shapes=_scatter_sems(ng), compiler_params=_params("arbitrary"),
    )(x1, x2, w, w, h, g.reshape(1, D), dh, *[piece[0] for piece in scatter])
    return outs[0], outs[1], list(outs[2:])


def _ffn_bwd_in(da, db, wffn, h, g, dh, name, scatter=()):
    F, D = wffn.shape[1:]
    specs = [pl.BlockSpec((None, F, D), functools.partial(lambda k, i: (k, 0, 0), k), pipeline_mode=pl.Buffered(1))
             for k in (0, 1)]
    return _two_dot_norm_bwd(da, db, wffn, specs, h, g, dh, name, scatter)


def _lru_in_bwd(dz, win, h, g, dh, name):
    R, D = win.shape[0] // 2, win.shape[1]
    specs = [pl.BlockSpec((R, D), functools.partial(lambda k, i: (k, 0), k), pipeline_mode=pl.Buffered(1)) for k in (0, 1)]
    halves = lambda tm: [pl.BlockSpec((tm, R), functools.partial(lambda k, i: (i, k), k)) for k in (0, 1)]
    return _two_dot_norm_bwd(dz, dz, win, specs, h, g, dh, name, x_specs=halves)[:2]


def _ffn_bwd_w(da, db, s, n, dhb, name, scatter=()):
    T, F = da.shape
    D = n.shape[1]
    tf, tk = _tile(F, TF_FFN_WG, LANES), _tile(T, TK_FFN_WG, 16)
    nj, nk, ng = F // tf, T // tk, len(scatter)

    def body(*refs):
        da_ref, db_ref, s_ref, n_ref, dh_ref = refs[:5]
        srcs, o_ref, outs = refs[5:5 + ng], refs[5 + ng], refs[6 + ng:6 + 2 * ng]
        g_sc, u_sc, d_sc = refs[6 + 2 * ng:9 + 2 * ng]
        j, k = pl.program_id(0), pl.program_id(1)
        if ng:
            _carry(_scatter_plan(scatter, srcs, outs, *refs[9 + 2 * ng:]), jnp.logical_and(j == 0, k == 0), None,
                   jnp.logical_and(j == nj - 1, k == nk - 1))

        @pl.when(k == 0)
        def _():
            g_sc[...] = jnp.zeros_like(g_sc)
            u_sc[...] = jnp.zeros_like(u_sc)
            d_sc[...] = jnp.zeros_like(d_sc)

        nv = n_ref[...]
        g_sc[...] += _dot_tn(da_ref[...], nv)
        u_sc[...] += _dot_tn(db_ref[...], nv)
        d_sc[...] += _dot_tn(s_ref[...], dh_ref[...])

        @pl.when(k == nk - 1)
        def _():
            o_ref[0] = g_sc[...].astype(BF)
            o_ref[1] = u_sc[...].astype(BF)
            o_ref[2] = (0.5 * d_sc[...]).astype(BF)

    act = pl.BlockSpec((tk, tf), lambda j, k: (k, j))
    tok = pl.BlockSpec((tk, D), lambda j, k: (k, 0))
    hbm = pl.BlockSpec(memory_space=pl.ANY)
    outs = pl.pallas_call(
        body, name=name, grid=(nj, nk), in_specs=[act, act, act, tok, tok] + [hbm] * ng,
        out_specs=[pl.BlockSpec((3, tf, D), lambda j, k: (0, j, 0), pipeline_mode=pl.Buffered(1))] + [hbm] * ng,
        out_shape=[jax.ShapeDtypeStruct((3, F, D), BF)] + _scattered_shapes(scatter),
        scratch_shapes=[pltpu.VMEM((tf, D), F32)] * 3 + _scatter_sems(ng),
        compiler_params=_params("arbitrary", "arbitrary"),
    )(da, db, s, n, dhb, *[piece[0] for piece in scatter])
    return outs[0], list(outs[1:])


def _ple_bwd(dh, gate, pp, h, g, wg, name):
    T, D = dh.shape
    tm = _tile(T, TM_EW, 16)

    def body(dh_ref, gate_ref, pp_ref, h_ref, g_ref, wg_ref, o_ref, dz_ref, dp_ref, dg_ref):
        @pl.when(pl.program_id(0) == 0)
        def _():
            dg_ref[...] = jnp.zeros_like(dg_ref)

        d, gate = dh_ref[...], gate_ref[...].astype(F32)
        dz = (d * pp_ref[...].astype(F32) * gate * (1.0 - gate)).astype(BF)
        dx, dg = _rms_bwd_parts(h_ref[...], g_ref[...], _dot_nt(dz, wg_ref[...]))
        o_ref[...] = d + dx
        dz_ref[...] = dz
        dp_ref[...] = (d * gate).astype(BF)
        dg_ref[...] += dg

    row = _row_spec(tm, D)
    return pl.pallas_call(
        body, name=name, grid=(T // tm,), in_specs=[row, row, row, row, _vec_spec(D), _vec_spec(D, D)],
        out_specs=[row, row, row, _vec_spec(D)],
        out_shape=[jax.ShapeDtypeStruct((T, D), F32), jax.ShapeDtypeStruct((T, D), BF), jax.ShapeDtypeStruct((T, D), BF),
                   jax.ShapeDtypeStruct((1, D), F32)],
        compiler_params=_params("arbitrary"),
    )(dh, gate, pp, h, g.reshape(1, D), wg)


def _lru_in(h, g, win, name):
    T, D = h.shape
    R = win.shape[0] // 2
    tm = _tile(T, TM_EW, 16)

    def body(h_ref, g_ref, w_ref, n_ref, gb_ref, xb_ref):
        n = _rms_parts(h_ref[...], g_ref[...])[0].astype(BF)
        n_ref[...] = n
        z = _dot_nt(n, w_ref[...])
        gb_ref[...] = z[:, :R].astype(BF)
        xb_ref[...] = z[:, R:]

    return pl.pallas_call(
        body, name=name, grid=(T // tm,),
        in_specs=[_row_spec(tm, D), _vec_spec(D), pl.BlockSpec((2 * R, D), lambda i: (0, 0), pipeline_mode=pl.Buffered(1))],
        out_specs=[_row_spec(tm, D), _row_spec(tm, R), _row_spec(tm, R)],
        out_shape=[jax.ShapeDtypeStruct((T, D), BF), jax.ShapeDtypeStruct((T, R), BF), jax.ShapeDtypeStruct((T, R), F32)],
        compiler_params=_params("parallel"),
    )(h, g.reshape(1, D), win)


def _lru_fwd(gb, xb, conv_w, conv_b, wa, wx, b_a, b_x, a_param, wout, h, name):
    T, R = xb.shape
    D = h.shape[1]
    tb = _tile(T, TB_SEQ, HALO)
    per, ng = tb // HALO, tb // SUBLANES

    def body(g_ref, x_ref, halo_ref, cw_ref, cb_ref, wa_ref, wx_ref, ba_ref, bx_ref, ap_ref, wo_ref, h_ref,
             o_ref, xc_ref, r_ref, ig_ref, a_ref, hs_ref, y_ref, ext, carry, a_sc, b_sc):
        i = pl.program_id(0)

        @pl.when(i == 0)
        def _():
            carry[...] = jnp.zeros_like(carry)

        ext[pl.ds(0, HALO), :] = jnp.where(i > 0, halo_ref[...], 0.0)
        ext[pl.ds(HALO, tb), :] = x_ref[...]
        xc = cb_ref[...] + cw_ref[0:1, :] * ext[pl.ds(HALO - 3, tb), :]
        for k in range(1, CONV_WIDTH):
            xc = xc + cw_ref[k:k + 1, :] * ext[pl.ds(HALO - 3 + k, tb), :]
        xcb = xc.astype(BF)
        r = _sigmoid_pos(_dot(xcb, wa_ref[...]) + ba_ref[...])
        ig = _sigmoid(_dot(xcb, wx_ref[...]) + bx_ref[...])
        la = -LRU_C * r * _softplus_neg(ap_ref[...])
        av = jnp.exp(la)
        xc_ref[...] = xc
        r_ref[...] = r
        ig_ref[...] = ig
        a_ref[...] = av
        A = av.reshape(ng, SUBLANES, R)
        B = (jnp.sqrt(_one_minus_sq(la, av)) * (ig * xc)).reshape(ng, SUBLANES, R)
        sub = lax.broadcasted_iota(jnp.int32, (1, SUBLANES, R), 1)
        for k in (1, 2, 4):
            m = sub >= k
            a_n = jnp.where(m, pltpu.roll(A, k, 1), 1.0)
            b_n = jnp.where(m, pltpu.roll(B, k, 1), 0.0)
            B = A * b_n + B
            A = A * a_n
        a_sc[...] = A.reshape(tb, R)
        b_sc[...] = B.reshape(tb, R)

        def group(q, c):
            rows = pl.ds(pl.multiple_of(q * SUBLANES, SUBLANES), SUBLANES)
            hg = a_sc[rows, :] * c + b_sc[rows, :]
            hs_ref[rows, :] = hg
            return hg[SUBLANES - 1:SUBLANES, :]

        carry[...] = lax.fori_loop(0, ng, group, carry[...])
        y = (hs_ref[...] * _gelu_parts(g_ref[...].astype(F32))[0]).astype(BF)
        y_ref[...] = y
        o_ref[...] = h_ref[...] + _dot(y, wo_ref[...])

    once = lambda rows, cols: pl.BlockSpec((rows, cols), lambda i: (0, 0), pipeline_mode=pl.Buffered(1))
    gate = pl.BlockSpec((tb, R), lambda i: (i, 0))
    halo = pl.BlockSpec((HALO, R), lambda i: (jnp.maximum(i * per - 1, 0), 0))
    return pl.pallas_call(
        body, name=name, grid=(T // tb,),
        in_specs=[gate, gate, halo, _vec_spec(R, CONV_WIDTH), _vec_spec(R), once(R, R), once(R, R), _vec_spec(R),
                  _vec_spec(R), _vec_spec(R), once(R, D), _row_spec(tb, D)],
        out_specs=[_row_spec(tb, D)] + [gate] * 6,
        out_shape=[jax.ShapeDtypeStruct((T, D), F32)] + [jax.ShapeDtypeStruct((T, R), F32)] * 5
        + [jax.ShapeDtypeStruct((T, R), BF)],
        scratch_shapes=[pltpu.VMEM((HALO + tb, R), F32), pltpu.VMEM((1, R), F32), pltpu.VMEM((tb, R), F32),
                        pltpu.VMEM((tb, R), F32)],
        compiler_params=_params("arbitrary"),
    )(gb, xb, xb, conv_w, conv_b.reshape(1, R), wa, wx, b_a.reshape(1, R), b_x.reshape(1, R), a_param.reshape(1, R), wout, h)


def _lru_bwd(dh, hs, gb, xb, a, r, ig, xc, wa, wx, wout, conv_w, a_param, name):
    T, R = hs.shape
    D = dh.shape[1]
    tb = _tile(T, TB_SEQ, HALO)
    per, nt, ng = tb // HALO, T // tb, tb // SUBLANES

    def body(dh_ref, h_ref, hp_ref, g_ref, x_ref, xp_ref, a_ref, r_ref, ig_ref, xc_ref, wa_ref, wx_ref, wo_ref, cw_ref,
             ap_ref, dz_ref, dpa_ref, dpx_ref, dsp_ref, dba_ref, dbx_ref, dcb_ref, dcw_ref,
             hext, xext, dext, carry, later, a_sc, b_sc, d_sc, l_sc):
        i = pl.program_id(0)

        @pl.when(i == 0)
        def _():
            for ref in (dsp_ref, dba_ref, dbx_ref, dcb_ref, dcw_ref, carry, later):
                ref[...] = jnp.zeros_like(ref)

        dy = _dot_nt(dh_ref[...].astype(BF), wo_ref[...])
        gl, dgl = _gelu_parts(g_ref[...].astype(F32))
        hv, av = h_ref[...], a_ref[...]
        dhd = dy * gl
        dz_ref[:, pl.ds(0, R)] = (dy * hv * dgl).astype(BF)
        d_sc[...] = dhd
        A = av.reshape(ng, SUBLANES, R)
        B = A * dhd.reshape(ng, SUBLANES, R)
        sub = lax.broadcasted_iota(jnp.int32, (1, SUBLANES, R), 1)
        for k in (1, 2, 4):
            m = sub < SUBLANES - k
            a_n = jnp.where(m, pltpu.roll(A, SUBLANES - k, 1), 1.0)
            b_n = jnp.where(m, pltpu.roll(B, SUBLANES - k, 1), 0.0)
            B = A * b_n + B
            A = A * a_n
        a_sc[...] = A.reshape(tb, R)
        b_sc[...] = B.reshape(tb, R)
        sub8 = lax.broadcasted_iota(jnp.int32, (SUBLANES, R), 0)

        def group(q, c):
            rows = pl.ds(pl.multiple_of((ng - 1 - q) * SUBLANES, SUBLANES), SUBLANES)
            mu = a_sc[rows, :] * c + b_sc[rows, :]
            l_sc[rows, :] = d_sc[rows, :] + jnp.where(sub8 == SUBLANES - 1, c, pltpu.roll(mu, SUBLANES - 1, 0))
            return mu[0:1, :]

        carry[...] = lax.fori_loop(0, ng, group, carry[...])
        lam = l_sc[...]
        hext[pl.ds(0, HALO), :] = jnp.where(i < nt - 1, hp_ref[...], 0.0)
        hext[pl.ds(HALO, tb), :] = hv
        h_prev = hext[pl.ds(HALO - 1, tb), :]
        rv, igv, xcv = r_ref[...], ig_ref[...], xc_ref[...]
        sp = _softplus_neg(ap_ref[...])
        mult = jnp.sqrt(_one_minus_sq(-LRU_C * rv * sp, av))
        dla = lam * h_prev * av - lam * (igv * xcv) * (av * av) / mult
        du = lam * mult
        dpa = (dla * (-LRU_C) * sp) * rv * (1.0 - rv)
        dpx = (du * xcv) * igv * (1.0 - igv)
        dsp_ref[...] += jnp.sum(dla * (-LRU_C) * rv, axis=0, keepdims=True)
        dba_ref[...] += jnp.sum(dpa, axis=0, keepdims=True)
        dbx_ref[...] += jnp.sum(dpx, axis=0, keepdims=True)
        dpab, dpxb = dpa.astype(BF), dpx.astype(BF)
        dpa_ref[...] = dpab
        dpx_ref[...] = dpxb
        dxc = du * igv + _dot_nt(dpab, wa_ref[...]) + _dot_nt(dpxb, wx_ref[...])
        dext[pl.ds(0, tb), :] = dxc
        dext[pl.ds(tb, SUBLANES), :] = later[...]
        later[...] = dxc[0:SUBLANES, :]
        xext[pl.ds(0, HALO), :] = jnp.where(i < nt - 1, xp_ref[...], 0.0)
        xext[pl.ds(HALO, tb), :] = x_ref[...]
        dxb = cw_ref[CONV_WIDTH - 1:CONV_WIDTH, :] * dxc
        for k in range(CONV_WIDTH - 1):
            dxb = dxb + cw_ref[k:k + 1, :] * dext[pl.ds(CONV_WIDTH - 1 - k, tb), :]
        dz_ref[:, pl.ds(R, R)] = dxb.astype(BF)
        for k in range(CONV_WIDTH):
            dcw_ref[k:k + 1, :] += jnp.sum(dxc * xext[pl.ds(HALO - 3 + k, tb), :], axis=0, keepdims=True)
        dcb_ref[...] += jnp.sum(dxc, axis=0, keepdims=True)

        @pl.when(i == nt - 1)
        def _():
            dsp_ref[...] = dsp_ref[...] * (-_sigmoid(-ap_ref[...]))

    once = lambda rows, cols: pl.BlockSpec((rows, cols), lambda i: (0, 0), pipeline_mode=pl.Buffered(1))
    t0 = pl.BlockSpec((tb, R), lambda i: (nt - 1 - i, 0))
    prev = pl.BlockSpec((HALO, R), lambda i: (jnp.maximum((nt - 1 - i) * per - 1, 0), 0))
    return pl.pallas_call(
        body, name=name, grid=(nt,),
        in_specs=[pl.BlockSpec((tb, D), lambda i: (nt - 1 - i, 0)), t0, prev, t0, t0, prev, t0, t0, t0, t0,
                  once(R, R), once(R, R), once(R, D), _vec_spec(R, CONV_WIDTH), _vec_spec(R)],
        out_specs=[pl.BlockSpec((tb, 2 * R), lambda i: (nt - 1 - i, 0)), t0, t0] + [_vec_spec(R)] * 4
        + [_vec_spec(R, SUBLANES)],
        out_shape=[jax.ShapeDtypeStruct((T, 2 * R), BF)] + [jax.ShapeDtypeStruct((T, R), BF)] * 2
        + [jax.ShapeDtypeStruct((1, R), F32)] * 4
        + [jax.ShapeDtypeStruct((SUBLANES, R), F32)],
        scratch_shapes=[pltpu.VMEM((HALO + tb, R), F32), pltpu.VMEM((HALO + tb, R), F32),
                        pltpu.VMEM((tb + SUBLANES, R), F32), pltpu.VMEM((1, R), F32), pltpu.VMEM((SUBLANES, R), F32)]
        + [pltpu.VMEM((tb, R), F32)] * 4,
        compiler_params=_params("arbitrary"),
    )(dh, hs, hs, gb, xb, xb, a, r, ig, xc, wa, wx, wout, conv_w, a_param.reshape(1, R))


def _gate_spans(R):
    d = R // LRU_HEADS
    spans = [min((j * GATE_COLS // d) * d // LANES * LANES, R - GATE_SPAN) for j in range(R // GATE_COLS)]
    assert R % GATE_COLS == 0 and all(lo + GATE_SPAN >= (((j + 1) * GATE_COLS - 1) // d + 1) * d for j, lo in enumerate(spans))
    return spans


def _lru_gates_dw(xc, dpa, dpx, name):
    T, R = xc.shape
    tk = _tile(T, 1024, 16)
    spans = _gate_spans(R)
    nb = len(spans)

    def body(x_ref, a_ref, b_ref, o_ref):
        @pl.when(pl.program_id(0) == 0)
        def _():
            o_ref[...] = jnp.zeros_like(o_ref)

        for j, lo in enumerate(spans):
            xs = x_ref[:, pl.ds(lo, GATE_SPAN)].astype(BF)
            cols = pl.ds(j * GATE_COLS, GATE_COLS)
            o_ref[0, j] += _dot_tn(xs, a_ref[:, cols])
            o_ref[1, j] += _dot_tn(xs, b_ref[:, cols])

    row = _row_spec(tk, R)
    out = pl.pallas_call(
        body, name=name, grid=(T // tk,), in_specs=[row, row, row],
        out_specs=pl.BlockSpec((2, nb, GATE_SPAN, GATE_COLS), lambda i: (0, 0, 0, 0)),
        out_shape=jax.ShapeDtypeStruct((2, nb, GATE_SPAN, GATE_COLS), F32), compiler_params=_params("arbitrary"),
    )(xc, dpa, dpx)
    dense = jnp.zeros((2, R, R), F32)
    for j, lo in enumerate(spans):
        dense = dense.at[:, lo:lo + GATE_SPAN, j * GATE_COLS:(j + 1) * GATE_COLS].set(out[:, j])
    return dense[0], dense[1]


def _window_sums(e, n, back):
    out, s = [], e
    for k in (1, 2, 4, 8):
        s = s + pltpu.roll(s, k if back else n - k, 0)
        out.append(s)
    return out


def _pool_fwd(h, g, w, b, scale, name):
    T, D = h.shape
    G = len(POOL_WINDOWS)
    gd = D // G
    tb = _tile(T, TB_SEQ, HALO)
    per = tb // HALO

    def body(h_ref, hp_ref, g_ref, w_ref, b_ref, s_ref, o_ref, u_ref, yb_ref):
        i = pl.program_id(0)
        t = i * tb + lax.broadcasted_iota(jnp.int32, (tb, gd), 0) + 1
        hv = h_ref[...]
        xn = _rms_parts(hv, g_ref[...])[0]
        xp = jnp.where(i > 0, _rms_parts(hp_ref[...], g_ref[...])[0], 0.0)
        for k, win in enumerate(POOL_WINDOWS):
            cols = slice(k * gd, (k + 1) * gd)
            x = xn[:, cols]
            e = jnp.concatenate([xp[:, cols], x], axis=0)
            sw = _window_sums(e, HALO + tb, True)[k][HALO:, :]
            u = (sw / jnp.minimum(t, win).astype(F32) - x).astype(BF)
            yb = _dot(u, w_ref[k]) + b_ref[:, cols]
            u_ref[:, cols] = u
            yb_ref[:, cols] = yb
            o_ref[:, cols] = hv[:, cols] + yb * s_ref[:, cols]

    tile = _row_spec(tb, D)
    prev = pl.BlockSpec((HALO, D), lambda i: (jnp.maximum(i * per - 1, 0), 0))
    return pl.pallas_call(
        body, name=name, grid=(T // tb,),
        in_specs=[tile, prev, _vec_spec(D), pl.BlockSpec((G, gd, gd), lambda i: (0, 0, 0)), _vec_spec(D), _vec_spec(D)],
        out_specs=[tile, tile, tile],
        out_shape=[jax.ShapeDtypeStruct((T, D), F32), jax.ShapeDtypeStruct((T, D), BF), jax.ShapeDtypeStruct((T, D), F32)],
        compiler_params=_params("parallel"),
    )(h, h, g.reshape(1, D), w, b.reshape(1, D), scale.reshape(1, D))


def _pool_bwd(dm, u, yb, w, scale, h, g, name):
    T, D = dm.shape
    G = len(POOL_WINDOWS)
    gd = D // G
    tb = _tile(T, TB_SEQ, HALO)
    nt = T // tb

    def body(d_ref, u_ref, yb_ref, w_ref, s_ref, h_ref, g_ref, o_ref, dw_ref, db_ref, ds_ref, dg_ref, later):
        i = pl.program_id(0)

        @pl.when(i == 0)
        def _():
            for ref in (dw_ref, db_ref, ds_ref, dg_ref, later):
                ref[...] = jnp.zeros_like(ref)

        d, sc = d_ref[...], s_ref[...]
        ds_ref[...] += jnp.sum(d * yb_ref[...], axis=0, keepdims=True)
        db_ref[...] += jnp.sum(d * sc, axis=0, keepdims=True)
        t = (nt - 1 - i) * tb + lax.broadcasted_iota(jnp.int32, (tb, gd), 0) + 1
        parts = []
        for k, win in enumerate(POOL_WINDOWS):
            cols = slice(k * gd, (k + 1) * gd)
            dy = (d[:, cols] * sc[:, cols]).astype(BF)
            du = _dot_nt(dy, w_ref[k])
            dw_ref[k] += _dot_tn(u_ref[:, cols], dy)
            v = du / jnp.minimum(t, win).astype(F32)
            e = jnp.concatenate([v, later[:, cols]], axis=0)
            later[:, cols] = v[0:HALO, :]
            parts.append(_window_sums(e, tb + HALO, False)[k][:tb, :] - du)
        dx, dg = _rms_bwd_parts(h_ref[...], g_ref[...], jnp.concatenate(parts, axis=1))
        o_ref[...] = d + dx
        dg_ref[...] += dg

    tile = pl.BlockSpec((tb, D), lambda i: (nt - 1 - i, 0))
    whole = pl.BlockSpec((G, gd, gd), lambda i: (0, 0, 0))
    return pl.pallas_call(
        body, name=name, grid=(nt,), in_specs=[tile, tile, tile, whole, _vec_spec(D), tile, _vec_spec(D)],
        out_specs=[tile, whole, _vec_spec(D), _vec_spec(D), _vec_spec(D)],
        out_shape=[jax.ShapeDtypeStruct((T, D), F32), jax.ShapeDtypeStruct((G, gd, gd), F32)]
        + [jax.ShapeDtypeStruct((1, D), F32)] * 3,
        scratch_shapes=[pltpu.VMEM((HALO, D), F32)], compiler_params=_params("arbitrary"),
    )(dm, u, yb, w, scale.reshape(1, D), h, g.reshape(1, D))


def _adamw(w, g, m, v, name):
    shape = w.shape
    cols = shape[-1] if w.ndim > 1 else shape[0]
    rows = w.size // cols
    tr = _tile(rows, TR_ADAM, SUBLANES)
    c1, c2 = 1.0 / (1.0 - ADAM_B1 ** ADAM_STEP), 1.0 / (1.0 - ADAM_B2 ** ADAM_STEP)

    def body(w_ref, g_ref, m_ref, v_ref, d_ref, mo_ref, vo_ref):
        gv = g_ref[...]
        mn = ADAM_B1 * m_ref[...] + (1.0 - ADAM_B1) * gv
        vn = ADAM_B2 * v_ref[...] + (1.0 - ADAM_B2) * (gv * gv)
        d_ref[...] = -ADAM_LR * ((mn * c1) / (jnp.sqrt(vn * c2) + ADAM_EPS) + ADAM_WD * w_ref[...])
        mo_ref[...] = mn
        vo_ref[...] = vn

    spec = _row_spec(tr, cols)
    outs = pl.pallas_call(
        body, name=name, grid=(rows // tr,), in_specs=[spec] * 4, out_specs=[spec] * 3,
        out_shape=[jax.ShapeDtypeStruct((rows, cols), F32)] * 3, compiler_params=_params("parallel"),
    )(*[t.reshape(rows, cols) for t in (w, g, m, v)])
    return [o.reshape(shape) for o in outs]


def _sum_devices(parts, name):
    n, rows, cols = parts.shape
    tr = _tile(rows, 1024, SUBLANES)

    def body(p_ref, o_ref):
        acc = p_ref[0].astype(F32)
        for k in range(1, n):
            acc = acc + p_ref[k].astype(F32)
        o_ref[...] = acc

    return pl.pallas_call(
        body, name=name, grid=(rows // tr,), in_specs=[pl.BlockSpec((n, tr, cols), lambda i: (0, i, 0))],
        out_specs=_row_spec(tr, cols), out_shape=jax.ShapeDtypeStruct((rows, cols), F32),
        compiler_params=_params("parallel"),
    )(parts)


def _position():
    return lax.axis_index("x"), lax.axis_index("y"), lax.axis_index("c")


def _gathered_shapes(blocks):
    return [jax.ShapeDtypeStruct((b.shape[0], N_DEV * b.shape[1], b.shape[2]), b.dtype) for b in blocks]


def _gather_sems(ng):
    return [pltpu.SemaphoreType.DMA((ng, 7)), pltpu.SemaphoreType.DMA((ng, 7)), pltpu.SemaphoreType.DMA((ng,))] if ng else []


def _gather_plan(blocks, srcs, outs, send_sems, recv_sems, local_sems):
    ng = len(blocks)
    x, y, c = _position()
    me, sibling = (x, y, c), (x, y, 1 - c)
    chips = [(1 - x, y), (x, 1 - y), (1 - x, 1 - y)]

    def rows(g, px, py, pc):
        r = blocks[g].shape[1]
        return outs[g].at[:, pl.ds((4 * px + 2 * py + pc) * r, r), :]

    def copy(g, k, block, to, src=None):
        return pltpu.make_async_remote_copy(
            src_ref=rows(g, *block) if src is None else src, dst_ref=rows(g, *block),
            send_sem=send_sems.at[g, k], recv_sem=recv_sems.at[g, k], device_id=to, device_id_type=MESH)

    def mine(g):
        return pltpu.make_async_copy(srcs[g], rows(g, *me), local_sems.at[g])

    def first(g):
        return [copy(g, 0, me, sibling, src=srcs[g])] + [copy(g, 1 + j, me, (*chip, c), src=srcs[g])
                                                         for j, chip in enumerate(chips)]

    def passed(g):
        return [copy(g, 4 + j, (*chip, c), sibling) for j, chip in enumerate(chips)]

    def start():
        for g in range(ng):
            mine(g).start()
            for cp in first(g):
                cp.start()

    def forward():
        for j, chip in enumerate(chips):
            for g in range(ng):
                copy(g, 1 + j, (*chip, c), me).wait_recv()
                copy(g, 4 + j, (*chip, c), sibling).start()

    def finish():
        for g in range(ng):
            copy(g, 0, sibling, me).wait_recv()
            for j, chip in enumerate(chips):
                copy(g, 4 + j, (*chip, 1 - c), me).wait_recv()
            for cp in first(g) + passed(g):
                cp.wait_send()
            mine(g).wait()

    return start, forward, finish


def _all_gather(blocks, name):
    ng = len(blocks)

    def body(*refs):
        start, forward, finish = _gather_plan(blocks, refs[:ng], refs[ng:2 * ng], *refs[2 * ng:])
        start()
        forward()
        finish()

    hbm = pl.BlockSpec(memory_space=pl.ANY)
    return pl.pallas_call(
        body, name=name, in_specs=[hbm] * ng, out_specs=[hbm] * ng, out_shape=_gathered_shapes(blocks),
        scratch_shapes=_gather_sems(ng),
    )(*blocks)


FLIPS = ((0, 0, 1), (1, 0, 0), (0, 1, 0), (1, 1, 0), (1, 0, 1), (0, 1, 1), (1, 1, 1))


def _piece_rows(piece):
    arr, m = piece
    return arr.shape[0] if m is None else 1


def _scattered_shapes(pieces):
    return [jax.ShapeDtypeStruct((N_DEV, _piece_rows(p), p[0].shape[1] // N_DEV, p[0].shape[2]), p[0].dtype)
            for p in pieces]


def _scatter_sems(ng):
    n = len(FLIPS)
    return [pltpu.SemaphoreType.DMA((ng, n)), pltpu.SemaphoreType.DMA((ng, n)), pltpu.SemaphoreType.DMA((ng,))] if ng else []


def _scatter_plan(pieces, srcs, outs, send_sems, recv_sems, local_sems):
    x, y, c = _position()

    def block(g, tx, ty, tc):
        arr, m = pieces[g]
        r = arr.shape[1] // N_DEV
        lead = slice(None) if m is None else pl.ds(m, 1)
        return srcs[g].at[lead, pl.ds((4 * tx + 2 * ty + tc) * r, r), :]

    def copies(g):
        out = []
        for k, (fx, fy, fc) in enumerate(FLIPS):
            tx, ty, tc = (1 - x if fx else x), (1 - y if fy else y), (1 - c if fc else c)
            out.append(pltpu.make_async_remote_copy(
                src_ref=block(g, tx, ty, tc), dst_ref=outs[g].at[k], send_sem=send_sems.at[g, k],
                recv_sem=recv_sems.at[g, k], device_id=(tx, ty, tc), device_id_type=MESH))
        return out

    def mine(g):
        return pltpu.make_async_copy(block(g, x, y, c), outs[g].at[len(FLIPS)], local_sems.at[g])

    def start():
        for g in range(len(pieces)):
            mine(g).start()
            for cp in copies(g):
                cp.start()

    def finish():
        for g in range(len(pieces)):
            for cp in copies(g):
                cp.wait()
            mine(g).wait()

    return start, finish


def _scatter_and_gather(pieces, blocks, name):
    n_p, n_b = len(pieces), len(blocks)

    def body(*refs):
        ins, outs, sems = refs[:n_p + n_b], refs[n_p + n_b:2 * (n_p + n_b)], refs[2 * (n_p + n_b):]
        s_start, s_finish = _scatter_plan(pieces, ins[:n_p], outs[:n_p], *sems[:3])
        g_start, g_forward, g_finish = _gather_plan(blocks, ins[n_p:], outs[n_p:], *sems[3:])
        s_start()
        g_start()
        g_forward()
        g_finish()
        s_finish()

    hbm = pl.BlockSpec(memory_space=pl.ANY)
    outs = pl.pallas_call(
        body, name=name, in_specs=[hbm] * (n_p + n_b), out_specs=[hbm] * (n_p + n_b),
        out_shape=_scattered_shapes(pieces) + _gathered_shapes(blocks),
        scratch_shapes=_scatter_sems(n_p) + _gather_sems(n_b),
    )(*[p[0] for p in pieces], *blocks)
    return list(outs[:n_p]), list(outs[n_p:])


def _scatter_sum(recv, name):
    _, n, r, c = recv.shape

    def body(r_ref, o_ref):
        acc = r_ref[len(FLIPS)].astype(F32)
        for k in range(len(FLIPS)):
            acc = acc + r_ref[k].astype(F32)
        o_ref[...] = acc

    return pl.pallas_call(
        body, name=name, grid=(n,), in_specs=[pl.BlockSpec((N_DEV, None, r, c), lambda i: (0, i, 0, 0))],
        out_specs=pl.BlockSpec((None, r, c), lambda i: (i, 0, 0)),
        out_shape=jax.ShapeDtypeStruct((n, r, c), F32), compiler_params=_params("parallel"),
    )(recv)


def _block_diag(w):
    H, d, _ = w.shape
    return (jnp.eye(H, dtype=w.dtype)[:, None, :, None] * w[:, :, None, :]).reshape(H * d, H * d)


def _diag_blocks(dense, H):
    d = dense.shape[0] // H
    return jnp.stack([dense[i * d:(i + 1) * d, i * d:(i + 1) * d] for i in range(H)])


def _local_step(x, p, tgt, W, blocks=None):
    dist = blocks is not None
    L = p.shape[0]
    W = dict(W)

    def gathering(keys):
        return [k for k in keys if k not in W] if dist else []

    def ffn_fwd(h, g, i, f, during_act, during_out, ple=None):
        w = W[("ffn", i, f)]
        keys = gathering(during_act)
        a, b, s, n, got = _ffn_fwd_act(h, g, w, f"ffn{f}_fwd_act_{i}", gather=[blocks[k] for k in keys])
        W.update(zip(keys, got))
        keys = gathering(during_out)
        if ple is not None:
            ple = (W["ple_norm"][i], W[("ple_gate", i)][0], W[("ple_proj", i)][0], ple)
        h, emb, got = _ffn_fwd_out(s, w, h, f"ffn{f}_fwd_out_{i}", gather=[blocks[k] for k in keys], ple=ple)
        W.update(zip(keys, got))
        return (a, b, s, n), h, emb

    saved = []
    h = x
    for i in range(L):
        j = i // 2
        lru = i % 2 == 0
        s = {"h0": h}
        mixer = [("lru_in", j), ("lru_out", j)] if lru else [("pool_w", j)]
        s["ffn1"], h, _ = ffn_fwd(h, W["ffn1_norm"][i], i, 1, [("ffn", i, 2)], mixer)
        s["h1"] = h
        if lru:
            hn, gb, xb = _lru_in(h, W["mix_norm"][i], W[("lru_in", j)][0], f"lru_in_{i}")
            wa, wx = _block_diag(W["lru_w_a"][j]).astype(BF), _block_diag(W["lru_w_x"][j]).astype(BF)
            h, xc, r, ig, a, hs, y = _lru_fwd(gb, xb, W["lru_conv_w"][j], W["lru_conv_b"][j], wa, wx, W["lru_b_a"][j],
                                              W["lru_b_x"][j], W["lru_a_param"][j], W[("lru_out", j)][0], h, f"lru_fwd_{i}")
            s.update(hn=hn, gb=gb, xb=xb, wa=wa, wx=wx, xc=xc, r=r, ig=ig, a=a, hs=hs, y=y)
        else:
            h, s["u"], s["yb"] = _pool_fwd(h, W["mix_norm"][i], W[("pool_w", j)], W["pool_b"][j], W["pool_scale"][j],
                                           f"pool_fwd_{i}")
        s["h2"] = h
        nxt = [("ffn", i + 1, 1)] if i + 1 < L else []
        s["ffn2"], s["h3"], (h, s["n4"], s["gate"], s["pp"]) = ffn_fwd(
            h, W["ffn2_norm"][i], i, 2, [("ple_gate", i), ("ple_proj", i)] + nxt, [], ple=p[i])
        saved.append(s)

    loss, dh, d_final = _loss_head(h, W["final_norm"], tgt)

    big, recv = {}, {}
    n_lru, n_pool = L // 2 + L % 2, L // 2
    small = {k: [None] * L for k in ("ffn1_norm", "mix_norm", "ffn2_norm", "ple_norm")}
    for k in ("lru_conv_w", "lru_conv_b", "lru_w_a", "lru_b_a", "lru_w_x", "lru_b_x", "lru_a_param"):
        small[k] = [None] * n_lru
    for k in ("pool_b", "pool_scale"):
        small[k] = [None] * n_pool

    def scattering(pieces):
        return [(k, m) for k, m in pieces if k in big] if dist else []

    def ffn_bwd(dh, h_in, g, acts, i, f, during):
        key, w = ("ffn", i, f), W[("ffn", i, f)]
        a, b, sv, n = acts
        out = [scattering(d) for d in during]
        sent = [[(big[k], m) for k, m in o] for o in out]
        da, db, dhb, got0 = _ffn_bwd_act(dh, a, b, w, f"ffn{f}_bwd_act_{i}", scatter=sent[0])
        big[key], got1 = _ffn_bwd_w(da, db, sv, n, dhb, f"ffn{f}_dw_{i}", scatter=sent[1])
        out.append(scattering([(key, 0)] + ([(key, 1)] if (i, f) == (0, 1) else [])))
        dh, dg, got2 = _ffn_bwd_in(da, db, w, h_in, g, dh, f"ffn{f}_bwd_in_{i}", scatter=[(big[k], m) for k, m in out[2]])
        for o, got in zip(out, (got0, got1, got2)):
            recv.update(zip(o, got))
        return dh, dg

    for i in reversed(range(L)):
        j = i // 2
        lru = i % 2 == 0
        s = saved[i]
        dh, dz, dpp, dg = _ple_bwd(dh, s["gate"], s["pp"], s["h3"], W["ple_norm"][i], W[("ple_gate", i)][0],
                                   f"ple_bwd_{i}")
        big[("ple_gate", i)] = _mm_tn(s["n4"], dz, f"ple_gate_dw_{i}")[None]
        big[("ple_proj", i)] = _mm_tn(dpp, p[i], f"ple_proj_dw_{i}")[None]
        small["ple_norm"][i] = dg[0]
        above = ("ffn", i + 1, 1)
        dh, dg = ffn_bwd(dh, s["h2"], W["ffn2_norm"][i], s["ffn2"], i, 2, [
            [(above, 1)], [(above, 2), (("ple_gate", i), None), (("ple_proj", i), None)]])
        small["ffn2_norm"][i] = dg[0]
        if lru:
            big[("lru_out", j)] = _mm_tn(s["y"], dh, f"lru_out_dw_{i}")[None]
            dz, dpa, dpx, dsp, dba, dbx, dcb, dcw = _lru_bwd(
                dh, s["hs"], s["gb"], s["xb"], s["a"], s["r"], s["ig"], s["xc"], s["wa"], s["wx"], W[("lru_out", j)][0],
                W["lru_conv_w"][j], W["lru_a_param"][j], f"lru_bwd_{i}")
            small["lru_a_param"][j], small["lru_b_a"][j], small["lru_b_x"][j] = dsp[0], dba[0], dbx[0]
            dwa, dwx = _lru_gates_dw(s["xc"], dpa, dpx, f"lru_gates_dw_{i}")
            small["lru_w_a"][j], small["lru_w_x"][j] = _diag_blocks(dwa, LRU_HEADS), _diag_blocks(dwx, LRU_HEADS)
            small["lru_conv_w"][j], small["lru_conv_b"][j] = dcw[:CONV_WIDTH], dcb[0]
            big[("lru_in", j)] = _mm_tn(dz, s["hn"], f"lru_in_dw_{i}")[None]
            dh, dg = _lru_in_bwd(dz, W[("lru_in", j)][0], s["h1"], W["mix_norm"][i], dh, f"lru_in_bwd_{i}")
            mixer = [("lru_in", j), ("lru_out", j)]
        else:
            dh, dw, dbp, dsc, dg = _pool_bwd(dh, s["u"], s["yb"], W[("pool_w", j)], W["pool_scale"][j], s["h1"],
                                             W["mix_norm"][i], f"pool_bwd_{i}")
            big[("pool_w", j)] = dw.astype(BF)
            small["pool_b"][j], small["pool_scale"][j] = dbp[0], dsc[0]
            mixer = [("pool_w", j)]
        small["mix_norm"][i] = dg[0]
        second = ("ffn", i, 2)
        dh, dg = ffn_bwd(dh, s["h0"], W["ffn1_norm"][i], s["ffn1"], i, 1, [
            [(second, 1)], [(second, 2)] + [(k, None) for k in mixer]])
        small["ffn1_norm"][i] = dg[0]

    small = {k: jnp.stack(v) for k, v in small.items()}
    small["final_norm"] = d_final[0]
    return loss, dh, big, recv, small


SMALL_SHARDED = ("pool_b", "pool_scale", "lru_conv_w")
SMALL = ("ffn1_norm", "mix_norm", "ffn2_norm", "ple_norm", "final_norm", "lru_conv_b", "lru_w_a", "lru_b_a",
         "lru_w_x", "lru_b_x", "lru_a_param", "pool_b", "pool_scale", "lru_conv_w")


def _pack_big(w):
    t = lambda a: jnp.swapaxes(a, -1, -2)
    out = {}
    for i in range(w["ffn1_norm"].shape[0]):
        for f in (1, 2):
            out[("ffn", i, f)] = jnp.stack([t(w[f"ffn{f}_w_gate"][i]), t(w[f"ffn{f}_w_up"][i]), w[f"ffn{f}_w_down"][i]])
        out[("ple_gate", i)], out[("ple_proj", i)] = w["ple_w_gate"][i][None], t(w["ple_w_proj"][i])[None]
    for j in range(w["lru_w_in"].shape[0]):
        out[("lru_in", j)], out[("lru_out", j)] = t(w["lru_w_in"][j])[None], w["lru_w_out"][j][None]
    for j in range(w["pool_w"].shape[0]):
        out[("pool_w", j)] = w["pool_w"][j]
    return out


def _unpack_big(b, L):
    t = lambda a: jnp.swapaxes(a, -1, -2)
    n_lru, n_pool = L // 2 + L % 2, L // 2
    out = {"lru_w_in": jnp.stack([t(b[("lru_in", j)][0]) for j in range(n_lru)]),
           "lru_w_out": jnp.stack([b[("lru_out", j)][0] for j in range(n_lru)]),
           "pool_w": jnp.stack([b[("pool_w", j)] for j in range(n_pool)]),
           "ple_w_gate": jnp.stack([b[("ple_gate", i)][0] for i in range(L)]),
           "ple_w_proj": jnp.stack([t(b[("ple_proj", i)][0]) for i in range(L)])}
    for f in (1, 2):
        out[f"ffn{f}_w_gate"] = jnp.stack([t(b[("ffn", i, f)][0]) for i in range(L)])
        out[f"ffn{f}_w_up"] = jnp.stack([t(b[("ffn", i, f)][1]) for i in range(L)])
        out[f"ffn{f}_w_down"] = jnp.stack([b[("ffn", i, f)][2] for i in range(L)])
    return out


def _flatten(parts, names, rows_of=LANES):
    flat = jnp.concatenate([parts[k].reshape(-1) for k in names])
    pad = (-flat.size) % (16 * rows_of)
    return jnp.pad(flat, (0, pad)).reshape(1, -1, rows_of)


def _unflatten(flat, like, names):
    out, o = {}, 0
    flat = flat.reshape(-1)
    for k in names:
        n = like[k].size
        out[k] = flat[o:o + n].reshape(like[k].shape)
        o += n
    return out


def kernel(x, p, ffn1_norm, ffn1_w_gate, ffn1_w_up, ffn1_w_down, mix_norm, lru_w_in, lru_conv_w, lru_conv_b, lru_w_a, lru_b_a, lru_w_x, lru_b_x, lru_a_param, lru_w_out, pool_w, pool_b, pool_scale, ffn2_norm, ffn2_w_gate, ffn2_w_up, ffn2_w_down, ple_norm, ple_w_gate, ple_w_proj, final_norm, loss_target, m_ffn1_norm, m_ffn1_w_gate, m_ffn1_w_up, m_ffn1_w_down, m_mix_norm, m_lru_w_in, m_lru_conv_w, m_lru_conv_b, m_lru_w_a, m_lru_b_a, m_lru_w_x, m_lru_b_x, m_lru_a_param, m_lru_w_out, m_pool_w, m_pool_b, m_pool_scale, m_ffn2_norm, m_ffn2_w_gate, m_ffn2_w_up, m_ffn2_w_down, m_ple_norm, m_ple_w_gate, m_ple_w_proj, m_final_norm, v_ffn1_norm, v_ffn1_w_gate, v_ffn1_w_up, v_ffn1_w_down, v_mix_norm, v_lru_w_in, v_lru_conv_w, v_lru_conv_b, v_lru_w_a, v_lru_b_a, v_lru_w_x, v_lru_b_x, v_lru_a_param, v_lru_w_out, v_pool_w, v_pool_b, v_pool_scale, v_ffn2_norm, v_ffn2_w_gate, v_ffn2_w_up, v_ffn2_w_down, v_ple_norm, v_ple_w_gate, v_ple_w_proj, v_final_norm):
    names = ["ffn1_norm", "ffn1_w_gate", "ffn1_w_up", "ffn1_w_down", "mix_norm", "lru_w_in", "lru_conv_w", "lru_conv_b",
             "lru_w_a", "lru_b_a", "lru_w_x", "lru_b_x", "lru_a_param", "lru_w_out", "pool_w", "pool_b", "pool_scale",
             "ffn2_norm", "ffn2_w_gate", "ffn2_w_up", "ffn2_w_down", "ple_norm", "ple_w_gate", "ple_w_proj", "final_norm"]
    w = dict(zip(names, [ffn1_norm, ffn1_w_gate, ffn1_w_up, ffn1_w_down, mix_norm, lru_w_in, lru_conv_w, lru_conv_b, lru_w_a, lru_b_a, lru_w_x, lru_b_x, lru_a_param, lru_w_out, pool_w, pool_b, pool_scale, ffn2_norm, ffn2_w_gate, ffn2_w_up, ffn2_w_down, ple_norm, ple_w_gate, ple_w_proj, final_norm]))
    m = dict(zip(names, [m_ffn1_norm, m_ffn1_w_gate, m_ffn1_w_up, m_ffn1_w_down, m_mix_norm, m_lru_w_in, m_lru_conv_w, m_lru_conv_b, m_lru_w_a, m_lru_b_a, m_lru_w_x, m_lru_b_x, m_lru_a_param, m_lru_w_out, m_pool_w, m_pool_b, m_pool_scale, m_ffn2_norm, m_ffn2_w_gate, m_ffn2_w_up, m_ffn2_w_down, m_ple_norm, m_ple_w_gate, m_ple_w_proj, m_final_norm]))
    v = dict(zip(names, [v_ffn1_norm, v_ffn1_w_gate, v_ffn1_w_up, v_ffn1_w_down, v_mix_norm, v_lru_w_in, v_lru_conv_w, v_lru_conv_b, v_lru_w_a, v_lru_b_a, v_lru_w_x, v_lru_b_x, v_lru_a_param, v_lru_w_out, v_pool_w, v_pool_b, v_pool_scale, v_ffn2_norm, v_ffn2_w_gate, v_ffn2_w_up, v_ffn2_w_down, v_ple_norm, v_ple_w_gate, v_ple_w_proj, v_final_norm]))
    L = p.shape[0]
    px, py, pc = _position()
    me = 4 * px + 2 * py + pc

    blocks = {k: b.astype(BF) for k, b in _pack_big(w).items()}
    first = ("ffn", 0, 1)
    got, small_blocks = _all_gather([blocks[first], _flatten(w, SMALL_SHARDED)], "gather_first")
    W = {first: got}
    per_dev = small_blocks.reshape(N_DEV, -1)
    shards = [_unflatten(per_dev[k], w, SMALL_SHARDED) for k in range(N_DEV)]
    for k in SMALL:
        W[k] = jnp.concatenate([s[k] for s in shards], axis=-1) if k in SMALL_SHARDED else w[k]

    loss, dx, big, recv, small = _local_step(x[0], p[:, 0], loss_target[0], W, blocks)

    last = [(k, m) for k in big if (k, None) not in recv for m in range(big[k].shape[0]) if (k, m) not in recv]
    got, (parts,) = _scatter_and_gather([(big[k], m) for k, m in last], [_flatten(small, SMALL).astype(BF)],
                                        "scatter_last_gather_small")
    recv.update(zip(last, got))

    def total(k):
        tag = "sum_" + "_".join(map(str, k))
        if (k, None) in recv:
            return _scatter_sum(recv[(k, None)], tag)
        return jnp.concatenate([_scatter_sum(recv[(k, m)], f"{tag}_{m}") for m in range(big[k].shape[0])])

    grads = _unpack_big({k: total(k) for k in big}, L)
    total_small = _sum_devices(parts.reshape(N_DEV, -1, LANES), "sum_small_grads")
    full = _unflatten(total_small, {k: W[k] for k in SMALL}, SMALL)
    for k in SMALL:
        if k in SMALL_SHARDED:
            n = w[k].shape[-1]
            grads[k] = lax.dynamic_slice_in_dim(full[k], me * n, n, axis=-1)
        else:
            grads[k] = full[k]

    delta, new_m, new_v = {}, {}, {}
    for k in names:
        delta[k], new_m[k], new_v[k] = _adamw(w[k], grads[k], m[k], v[k], f"adamw_{k}")
    total_loss = lax.psum(loss[0, 0], ("x", "y", "c"))
    return (total_loss, dx[None], *[grads[k] for k in names], *[delta[k] for k in names],
            *[new_m[k] for k in names], *[new_v[k] for k in names])
```

```python
import functools

import jax
import jax.numpy as jnp
from jax import lax
from jax.experimental import pallas as pl
from jax.experimental.pallas import tpu as pltpu

F32 = jnp.float32
BF = jnp.bfloat16
MESH = pl.DeviceIdType.MESH

RMS_EPS = 1e-6
LRU_C = 8.0
LRU_HEADS = 16
CONV_WIDTH = 4
POOL_WINDOWS = (2, 4, 8, 16)
ADAM_LR, ADAM_B1, ADAM_B2, ADAM_EPS, ADAM_WD, ADAM_STEP = 0.001, 0.9, 0.999, 1e-08, 0.01, 10

N_DEV = 8
LANES = 128
SUBLANES = 8
GATE_COLS, GATE_SPAN = 256, 512
HALO = 16
VMEM_LIMIT = 56 * 1024 * 1024

TM_FFN = 1024
TM_FFN_ACT = 2048
TM_FFN_IN = 512
TF_FFN = 256
TF_FFN_WG = 1408
TK_FFN_WG = 512
TB_SEQ = 256
TM_EW = 512
TM_MM, TN_MM, TK_MM = 1280, 1024, 1024
TR_SUM = 176
TR_ADAM = 512


def _tile(n, pref, align):
    if n <= pref:
        return n
    t = (pref // align) * align
    while t >= align:
        if n % t == 0:
            return t
        t -= align
    raise ValueError(f"no tile for {n} (pref {pref}, align {align})")


def _params(*sem):
    return pltpu.CompilerParams(dimension_semantics=sem, vmem_limit_bytes=VMEM_LIMIT)


def _dot(a, b):
    return lax.dot_general(a, b, (((1,), (0,)), ((), ())), preferred_element_type=F32)


def _dot_nt(a, b):
    return lax.dot_general(a, b, (((1,), (1,)), ((), ())), preferred_element_type=F32)


def _dot_tn(a, b):
    return lax.dot_general(a, b, (((0,), (0,)), ((), ())), preferred_element_type=F32)


def _sigmoid(x):
    return 0.5 + 0.5 * jnp.tanh(0.5 * x)


def _sigmoid_pos(x):
    return 1.0 / (1.0 + jnp.exp(-x))


def _gelu_parts(x):
    k0, k1 = 0.7978845608028654, 0.044715
    t = jnp.tanh(k0 * (x + k1 * x * x * x))
    g = 0.5 * x * (1.0 + t)
    dg = 0.5 * (1.0 + t) + 0.5 * x * (1.0 - t * t) * k0 * (1.0 + 3.0 * k1 * x * x)
    return g, dg


def _one_minus_sq(la, a):
    return jnp.tanh(-la) * (1.0 + a * a)


def _softplus_neg(l):
    u = jnp.exp(-jnp.abs(l))
    w = 1.0 + u
    log1p = jnp.where(w == 1.0, u, jnp.log(w) * (u / jnp.where(w == 1.0, 1.0, w - 1.0)))
    return jnp.maximum(-l, 0.0) + log1p


def _rms_parts(x, g):
    r = lax.rsqrt(jnp.mean(x * x, axis=-1, keepdims=True) + RMS_EPS)
    nhat = x * r
    return nhat * g, nhat, r


def _rms_bwd_parts(x, g, dn):
    _, nhat, r = _rms_parts(x, g)
    u = dn * g
    dx = r * (u - nhat * jnp.mean(u * nhat, axis=-1, keepdims=True))
    return dx, jnp.sum(dn * nhat, axis=0, keepdims=True)


def _row_spec(tm, d, single=False):
    if single:
        return pl.BlockSpec((tm, d), lambda i, *_: (i, 0), pipeline_mode=pl.Buffered(1))
    return pl.BlockSpec((tm, d), lambda i, *_: (i, 0))


def _vec_spec(d, rows=1):
    return pl.BlockSpec((rows, d), lambda *_: (0, 0))


def _mm_tn(x, w, name):
    (K, M), (_, N) = x.shape, w.shape
    tm, tn, tk = _tile(M, TM_MM, LANES), _tile(N, TN_MM, LANES), _tile(K, TK_MM, 16)
    nk = K // tk

    def body(x_ref, w_ref, o_ref, acc):
        k = pl.program_id(2)

        @pl.when(k == 0)
        def _():
            acc[...] = jnp.zeros_like(acc)

        acc[...] += _dot_tn(x_ref[...].astype(BF), w_ref[...].astype(BF))

        @pl.when(k == nk - 1)
        def _():
            o_ref[...] = acc[...].astype(BF)

    return pl.pallas_call(
        body, name=name, grid=(M // tm, N // tn, nk),
        in_specs=[pl.BlockSpec((tk, tm), lambda i, j, k: (k, i)), pl.BlockSpec((tk, tn), lambda i, j, k: (k, j))],
        out_specs=pl.BlockSpec((tm, tn), lambda i, j, k: (i, j)),
        out_shape=jax.ShapeDtypeStruct((M, N), BF),
        scratch_shapes=[pltpu.VMEM((tm, tn), F32)],
        compiler_params=_params("parallel", "parallel", "arbitrary"),
    )(x, w)


def _loss_head(h, g, tgt):
    T, D = h.shape
    tm = _tile(T, TM_EW, 16)

    def body(h_ref, g_ref, t_ref, loss_ref, dh_ref, dg_ref):
        @pl.when(pl.program_id(0) == 0)
        def _():
            dg_ref[...] = jnp.zeros_like(dg_ref)
            loss_ref[...] = jnp.zeros_like(loss_ref)

        x, gg = h_ref[...], g_ref[...]
        y = _rms_parts(x, gg)[0]
        e = y - t_ref[...]
        part = jnp.sum(jnp.sum(e * e, axis=0, keepdims=True), axis=1, keepdims=True) * (0.5 / D)
        loss_ref[...] += jnp.broadcast_to(part, loss_ref.shape)
        dx, dg = _rms_bwd_parts(x, gg, e * (1.0 / D))
        dh_ref[...] = dx
        dg_ref[...] += dg

    return pl.pallas_call(
        body, name="loss_head", grid=(T // tm,),
        in_specs=[_row_spec(tm, D), _vec_spec(D), _row_spec(tm, D)],
        out_specs=[_vec_spec(LANES), _row_spec(tm, D), _vec_spec(D)],
        out_shape=[jax.ShapeDtypeStruct((1, LANES), F32), jax.ShapeDtypeStruct((T, D), F32),
                   jax.ShapeDtypeStruct((1, D), F32)],
        compiler_params=_params("arbitrary"),
    )(h, g.reshape(1, D), tgt)


def _carry(plan, first, mid, last):
    pl.when(first)(plan[0])
    if len(plan) == 3:
        pl.when(mid)(plan[1])
    pl.when(last)(plan[-1])


def _ffn_fwd_act(h, g, wffn, name, gather=()):
    T, D = h.shape
    F = wffn.shape[1]
    tm, tf = _tile(T, TM_FFN_ACT, 16), _tile(F, TF_FFN, LANES)
    ni, nf, ng = T // tm, F // tf, len(gather)

    def body(*refs):
        h_ref, g_ref, wg_ref, wu_ref = refs[:4]
        srcs, (a_ref, b_ref, s_ref, n_ref), outs = refs[4:4 + ng], refs[4 + ng:8 + ng], refs[8 + ng:8 + 2 * ng]
        i, j = pl.program_id(0), pl.program_id(1)
        if ng:
            _carry(_gather_plan(gather, srcs, outs, *refs[8 + 2 * ng:]), jnp.logical_and(i == 0, j == 0),
                   jnp.logical_and(i == (3 * ni) // 4, j == 0), jnp.logical_and(i == ni - 1, j == nf - 1))

        @pl.when(j == 0)
        def _():
            n_ref[...] = _rms_parts(h_ref[...], g_ref[...])[0].astype(BF)

        n = n_ref[...]
        a = _dot_nt(n, wg_ref[...])
        b = _dot_nt(n, wu_ref[...])
        a_ref[...] = a.astype(BF)
        b_ref[...] = b.astype(BF)
        s_ref[...] = (a * _sigmoid(a) * b).astype(BF)

    tile = pl.BlockSpec((tm, tf), lambda i, j: (i, j))
    w = [pl.BlockSpec((None, tf, D), functools.partial(lambda k, i, j: (k, j, 0), k)) for k in (0, 1)]
    hbm = pl.BlockSpec(memory_space=pl.ANY)
    outs = pl.pallas_call(
        body, name=name, grid=(ni, nf), in_specs=[_row_spec(tm, D), _vec_spec(D)] + w + [hbm] * ng,
        out_specs=[tile, tile, tile, _row_spec(tm, D)] + [hbm] * ng,
        out_shape=[jax.ShapeDtypeStruct((T, F), BF)] * 3 + [jax.ShapeDtypeStruct((T, D), BF)] + _gathered_shapes(gather),
        scratch_shapes=_gather_sems(ng), compiler_params=_params("arbitrary", "arbitrary"),
    )(h, g.reshape(1, D), wffn, wffn, *gather)
    return outs[0], outs[1], outs[2], outs[3], list(outs[4:])


def _ffn_fwd_out(s, wffn, h, name, gather=(), ple=None):
    T, F = s.shape
    D = h.shape[1]
    tm = _tile(T, TM_FFN_IN, 16)
    ni, ng, ne = T // tm, len(gather), 4 if ple else 0

    def body(*refs):
        s_ref, w_ref, h_ref = refs[:3]
        pin, rest = refs[3:3 + ne], refs[3 + ne:]
        srcs, o_ref, pout, outs = rest[:ng], rest[ng], rest[ng + 1:ng + 1 + ne], rest[ng + 1 + ne:2 * ng + 1 + ne]
        i = pl.program_id(0)
        if ng:
            _carry(_gather_plan(gather, srcs, outs, *rest[2 * ng + 1 + ne:]), i == 0, i == (3 * ni) // 4, i == ni - 1)
        x = h_ref[...] + 0.5 * _dot(s_ref[...], w_ref[...])
        o_ref[...] = x
        if ple:
            g_ref, wg_ref, wp_ref, p_ref = pin
            e_ref, n_ref, gate_ref, pp_ref = pout
            n = _rms_parts(x, g_ref[...])[0].astype(BF)
            gate = _sigmoid(_dot(n, wg_ref[...]))
            pp = _dot_nt(p_ref[...].astype(BF), wp_ref[...])
            e_ref[...] = x + gate * pp
            n_ref[...] = n
            gate_ref[...] = gate.astype(BF)
            pp_ref[...] = pp.astype(BF)

    hbm = pl.BlockSpec(memory_space=pl.ANY)
    row = _row_spec(tm, D)
    once = lambda rows, cols: pl.BlockSpec((rows, cols), lambda i: (0, 0), pipeline_mode=pl.Buffered(1))
    extra_in, extra_out, extra_shape, extra_args = [], [], [], []
    if ple:
        g, wg, wp, p = ple
        P = p.shape[1]
        extra_in = [_vec_spec(D), once(D, D), once(D, P), _row_spec(tm, P)]
        extra_out = [row] * 4
        extra_shape = [jax.ShapeDtypeStruct((T, D), F32)] + [jax.ShapeDtypeStruct((T, D), BF)] * 3
        extra_args = [g.reshape(1, D), wg, wp, p]
    outs = pl.pallas_call(
        body, name=name, grid=(ni,),
        in_specs=[_row_spec(tm, F), pl.BlockSpec((None, F, D), lambda i: (2, 0, 0), pipeline_mode=pl.Buffered(1)), row]
        + extra_in + [hbm] * ng,
        out_specs=[row] + extra_out + [hbm] * ng,
        out_shape=[jax.ShapeDtypeStruct((T, D), F32)] + extra_shape + _gathered_shapes(gather),
        scratch_shapes=_gather_sems(ng), compiler_params=_params("arbitrary"),
    )(s, wffn, h, *extra_args, *gather)
    return outs[0], list(outs[1:1 + ne]), list(outs[1 + ne:])


def _ffn_bwd_act(dh, a, b, wffn, name, scatter=()):
    T, D = dh.shape
    F = wffn.shape[1]
    tm, tf = _tile(T, TM_FFN_ACT, 16), _tile(F, TF_FFN, LANES)
    ni, nf, ng = T // tm, F // tf, len(scatter)

    def body(*refs):
        dh_ref, a_ref, b_ref, wd_ref = refs[:4]
        srcs, (da_ref, db_ref, dhb_ref), outs = refs[4:4 + ng], refs[4 + ng:7 + ng], refs[7 + ng:7 + 2 * ng]
        i, j = pl.program_id(0), pl.program_id(1)
        if ng:
            _carry(_scatter_plan(scatter, srcs, outs, *refs[7 + 2 * ng:]), jnp.logical_and(i == 0, j == 0), None,
                   jnp.logical_and(i == ni - 1, j == nf - 1))

        @pl.when(j == 0)
        def _():
            dhb_ref[...] = dh_ref[...].astype(BF)

        ds = 0.5 * _dot_nt(dhb_ref[...], wd_ref[...])
        av, bv = a_ref[...].astype(F32), b_ref[...].astype(F32)
        sig = _sigmoid(av)
        da_ref[...] = (ds * bv * (sig * (1.0 + av * (1.0 - sig)))).astype(BF)
        db_ref[...] = (ds * (av * sig)).astype(BF)

    tile = pl.BlockSpec((tm, tf), lambda i, j: (i, j))
    hbm = pl.BlockSpec(memory_space=pl.ANY)
    outs = pl.pallas_call(
        body, name=name, grid=(ni, nf),
        in_specs=[_row_spec(tm, D), tile, tile, pl.BlockSpec((None, tf, D), lambda i, j: (2, j, 0))] + [hbm] * ng,
        out_specs=[tile, tile, _row_spec(tm, D)] + [hbm] * ng,
        out_shape=[jax.ShapeDtypeStruct((T, F), BF)] * 2 + [jax.ShapeDtypeStruct((T, D), BF)]
        + _scattered_shapes(scatter),
        scratch_shapes=_scatter_sems(ng), compiler_params=_params("arbitrary", "arbitrary"),
    )(dh, a, b, wffn, *[piece[0] for piece in scatter])
    return outs[0], outs[1], outs[2], list(outs[3:])


def _two_dot_norm_bwd(x1, x2, w, w_specs, h, g, dh, name, scatter=(), x_specs=None):
    T, K = x1.shape
    D = h.shape[1]
    tm = _tile(T, TM_FFN_IN, 16)
    ni, ng = T // tm, len(scatter)
    x_specs = x_specs or (lambda tm: [pl.BlockSpec((tm, K), lambda i: (i, 0))] * 2)

    def body(*refs):
        x1_ref, x2_ref, w1_ref, w2_ref, h_ref, g_ref, dh_ref = refs[:7]
        srcs, (o_ref, dg_ref), outs = refs[7:7 + ng], refs[7 + ng:9 + ng], refs[9 + ng:9 + 2 * ng]
        i = pl.program_id(0)
        if ng:
            _carry(_scatter_plan(scatter, srcs, outs, *refs[9 + 2 * ng:]), i == 0, None, i == ni - 1)

        @pl.when(i == 0)
        def _():
            dg_ref[...] = jnp.zeros_like(dg_ref)

        dn = _dot(x1_ref[...], w1_ref[...]) + _dot(x2_ref[...], w2_ref[...])
        dx, dg = _rms_bwd_parts(h_ref[...], g_ref[...], dn)
        o_ref[...] = dh_ref[...] + dx
        dg_ref[...] += dg

    hbm = pl.BlockSpec(memory_space=pl.ANY)
    outs = pl.pallas_call(
        body, name=name, grid=(ni,),
        in_specs=x_specs(tm) + w_specs + [_row_spec(tm, D), _vec_spec(D), _row_spec(tm, D)] + [hbm] * ng,
        out_specs=[_row_spec(tm, D), _vec_spec(D)] + [hbm] * ng,
        out_shape=[jax.ShapeDtypeStruct((T, D), F32), jax.ShapeDtypeStruct((1, D), F32)] + _scattered_shapes(scatter),
        scratch_shapes=_scatter_sems(ng), compiler_params=_params("arbitrary"),
    )(x1, x2, w, w, h, g.reshape(1, D), dh, *[piece[0] for piece in scatter])
    return outs[0], outs[1], list(outs[2:])


def _ffn_bwd_in(da, db, wffn, h, g, dh, name, scatter=()):
    F, D = wffn.shape[1:]
    specs = [pl.BlockSpec((None, F, D), functools.partial(lambda k, i: (k, 0, 0), k), pipeline_mode=pl.Buffered(1))
             for k in (0, 1)]
    return _two_dot_norm_bwd(da, db, wffn, specs, h, g, dh, name, scatter)


def _lru_in_bwd(dz, win, h, g, dh, name):
    R, D = win.shape[0] // 2, win.shape[1]
    specs = [pl.BlockSpec((R, D), functools.partial(lambda k, i: (k, 0), k), pipeline_mode=pl.Buffered(1)) for k in (0, 1)]
    halves = lambda tm: [pl.BlockSpec((tm, R), functools.partial(lambda k, i: (i, k), k)) for k in (0, 1)]
    return _two_dot_norm_bwd(dz, dz, win, specs, h, g, dh, name, x_specs=halves)[:2]


def _ffn_bwd_w(da, db, s, n, dhb, name, scatter=()):
    T, F = da.shape
    D = n.shape[1]
    tf, tk = _tile(F, TF_FFN_WG, LANES), _tile(T, TK_FFN_WG, 16)
    nj, nk, ng = F // tf, T // tk, len(scatter)

    def body(*refs):
        da_ref, db_ref, s_ref, n_ref, dh_ref = refs[:5]
        srcs, o_ref, outs = refs[5:5 + ng], refs[5 + ng], refs[6 + ng:6 + 2 * ng]
        g_sc, u_sc, d_sc = refs[6 + 2 * ng:9 + 2 * ng]
        j, k = pl.program_id(0), pl.program_id(1)
        if ng:
            _carry(_scatter_plan(scatter, srcs, outs, *refs[9 + 2 * ng:]), jnp.logical_and(j == 0, k == 0), None,
                   jnp.logical_and(j == nj - 1, k == nk - 1))

        @pl.when(k == 0)
        def _():
            g_sc[...] = jnp.zeros_like(g_sc)
            u_sc[...] = jnp.zeros_like(u_sc)
            d_sc[...] = jnp.zeros_like(d_sc)

        nv = n_ref[...]
        g_sc[...] += _dot_tn(da_ref[...], nv)
        u_sc[...] += _dot_tn(db_ref[...], nv)
        d_sc[...] += _dot_tn(s_ref[...], dh_ref[...])

        @pl.when(k == nk - 1)
        def _():
            o_ref[0] = g_sc[...].astype(BF)
            o_ref[1] = u_sc[...].astype(BF)
            o_ref[2] = (0.5 * d_sc[...]).astype(BF)

    act = pl.BlockSpec((tk, tf), lambda j, k: (k, j))
    tok = pl.BlockSpec((tk, D), lambda j, k: (k, 0))
    hbm = pl.BlockSpec(memory_space=pl.ANY)
    outs = pl.pallas_call(
        body, name=name, grid=(nj, nk), in_specs=[act, act, act, tok, tok] + [hbm] * ng,
        out_specs=[pl.BlockSpec((3, tf, D), lambda j, k: (0, j, 0), pipeline_mode=pl.Buffered(1))] + [hbm] * ng,
        out_shape=[jax.ShapeDtypeStruct((3, F, D), BF)] + _scattered_shapes(scatter),
        scratch_shapes=[pltpu.VMEM((tf, D), F32)] * 3 + _scatter_sems(ng),
        compiler_params=_params("arbitrary", "arbitrary"),
    )(da, db, s, n, dhb, *[piece[0] for piece in scatter])
    return outs[0], list(outs[1:])


def _ple_bwd(dh, gate, pp, h, g, wg, name):
    T, D = dh.shape
    tm = _tile(T, TM_EW, 16)

    def body(dh_ref, gate_ref, pp_ref, h_ref, g_ref, wg_ref, o_ref, dz_ref, dp_ref, dg_ref):
        @pl.when(pl.program_id(0) == 0)
        def _():
            dg_ref[...] = jnp.zeros_like(dg_ref)

        d, gate = dh_ref[...], gate_ref[...].astype(F32)
        dz = (d * pp_ref[...].astype(F32) * gate * (1.0 - gate)).astype(BF)
        dx, dg = _rms_bwd_parts(h_ref[...], g_ref[...], _dot_nt(dz, wg_ref[...]))
        o_ref[...] = d + dx
        dz_ref[...] = dz
        dp_ref[...] = (d * gate).astype(BF)
        dg_ref[...] += dg

    row = _row_spec(tm, D)
    return pl.pallas_call(
        body, name=name, grid=(T // tm,), in_specs=[row, row, row, row, _vec_spec(D), _vec_spec(D, D)],
        out_specs=[row, row, row, _vec_spec(D)],
        out_shape=[jax.ShapeDtypeStruct((T, D), F32), jax.ShapeDtypeStruct((T, D), BF), jax.ShapeDtypeStruct((T, D), BF),
                   jax.ShapeDtypeStruct((1, D), F32)],
        compiler_params=_params("arbitrary"),
    )(dh, gate, pp, h, g.reshape(1, D), wg)


def _lru_in(h, g, win, name):
    T, D = h.shape
    R = win.shape[0] // 2
    tm = _tile(T, TM_EW, 16)

    def body(h_ref, g_ref, w_ref, n_ref, gb_ref, xb_ref):
        n = _rms_parts(h_ref[...], g_ref[...])[0].astype(BF)
        n_ref[...] = n
        z = _dot_nt(n, w_ref[...])
        gb_ref[...] = z[:, :R].astype(BF)
        xb_ref[...] = z[:, R:]

    return pl.pallas_call(
        body, name=name, grid=(T // tm,),
        in_specs=[_row_spec(tm, D), _vec_spec(D), pl.BlockSpec((2 * R, D), lambda i: (0, 0), pipeline_mode=pl.Buffered(1))],
        out_specs=[_row_spec(tm, D), _row_spec(tm, R), _row_spec(tm, R)],
        out_shape=[jax.ShapeDtypeStruct((T, D), BF), jax.ShapeDtypeStruct((T, R), BF), jax.ShapeDtypeStruct((T, R), F32)],
        compiler_params=_params("parallel"),
    )(h, g.reshape(1, D), win)


def _lru_fwd(gb, xb, conv_w, conv_b, wa, wx, b_a, b_x, a_param, wout, h, name):
    T, R = xb.shape
    D = h.shape[1]
    tb = _tile(T, TB_SEQ, HALO)
    per, ng = tb // HALO, tb // SUBLANES

    def body(g_ref, x_ref, halo_ref, cw_ref, cb_ref, wa_ref, wx_ref, ba_ref, bx_ref, ap_ref, wo_ref, h_ref,
             o_ref, xc_ref, r_ref, ig_ref, a_ref, hs_ref, y_ref, ext, carry, a_sc, b_sc):
        i = pl.program_id(0)

        @pl.when(i == 0)
        def _():
            carry[...] = jnp.zeros_like(carry)

        ext[pl.ds(0, HALO), :] = jnp.where(i > 0, halo_ref[...], 0.0)
        ext[pl.ds(HALO, tb), :] = x_ref[...]
        xc = cb_ref[...] + cw_ref[0:1, :] * ext[pl.ds(HALO - 3, tb), :]
        for k in range(1, CONV_WIDTH):
            xc = xc + cw_ref[k:k + 1, :] * ext[pl.ds(HALO - 3 + k, tb), :]
        xcb = xc.astype(BF)
        r = _sigmoid_pos(_dot(xcb, wa_ref[...]) + ba_ref[...])
        ig = _sigmoid(_dot(xcb, wx_ref[...]) + bx_ref[...])
        la = -LRU_C * r * _softplus_neg(ap_ref[...])
        av = jnp.exp(la)
        xc_ref[...] = xc
        r_ref[...] = r
        ig_ref[...] = ig
        a_ref[...] = av
        A = av.reshape(ng, SUBLANES, R)
        B = (jnp.sqrt(_one_minus_sq(la, av)) * (ig * xc)).reshape(ng, SUBLANES, R)
        sub = lax.broadcasted_iota(jnp.int32, (1, SUBLANES, R), 1)
        for k in (1, 2, 4):
            m = sub >= k
            a_n = jnp.where(m, pltpu.roll(A, k, 1), 1.0)
            b_n = jnp.where(m, pltpu.roll(B, k, 1), 0.0)
            B = A * b_n + B
            A = A * a_n
        a_sc[...] = A.reshape(tb, R)
        b_sc[...] = B.reshape(tb, R)

        def group(q, c):
            rows = pl.ds(pl.multiple_of(q * SUBLANES, SUBLANES), SUBLANES)
            hg = a_sc[rows, :] * c + b_sc[rows, :]
            hs_ref[rows, :] = hg
            return hg[SUBLANES - 1:SUBLANES, :]

        carry[...] = lax.fori_loop(0, ng, group, carry[...])
        y = (hs_ref[...] * _gelu_parts(g_ref[...].astype(F32))[0]).astype(BF)
        y_ref[...] = y
        o_ref[...] = h_ref[...] + _dot(y, wo_ref[...])

    once = lambda rows, cols: pl.BlockSpec((rows, cols), lambda i: (0, 0), pipeline_mode=pl.Buffered(1))
    gate = pl.BlockSpec((tb, R), lambda i: (i, 0))
    halo = pl.BlockSpec((HALO, R), lambda i: (jnp.maximum(i * per - 1, 0), 0))
    return pl.pallas_call(
        body, name=name, grid=(T // tb,),
        in_specs=[gate, gate, halo, _vec_spec(R, CONV_WIDTH), _vec_spec(R), once(R, R), once(R, R), _vec_spec(R),
                  _vec_spec(R), _vec_spec(R), once(R, D), _row_spec(tb, D)],
        out_specs=[_row_spec(tb, D)] + [gate] * 6,
        out_shape=[jax.ShapeDtypeStruct((T, D), F32)] + [jax.ShapeDtypeStruct((T, R), F32)] * 5
        + [jax.ShapeDtypeStruct((T, R), BF)],
        scratch_shapes=[pltpu.VMEM((HALO + tb, R), F32), pltpu.VMEM((1, R), F32), pltpu.VMEM((tb, R), F32),
                        pltpu.VMEM((tb, R), F32)],
        compiler_params=_params("arbitrary"),
    )(gb, xb, xb, conv_w, conv_b.reshape(1, R), wa, wx, b_a.reshape(1, R), b_x.reshape(1, R), a_param.reshape(1, R), wout, h)


def _lru_bwd(dh, hs, gb, xb, a, r, ig, xc, wa, wx, wout, conv_w, a_param, name):
    T, R = hs.shape
    D = dh.shape[1]
    tb = _tile(T, TB_SEQ, HALO)
    per, nt, ng = tb // HALO, T // tb, tb // SUBLANES

    def body(dh_ref, h_ref, hp_ref, g_ref, x_ref, xp_ref, a_ref, r_ref, ig_ref, xc_ref, wa_ref, wx_ref, wo_ref, cw_ref,
             ap_ref, dz_ref, dpa_ref, dpx_ref, dsp_ref, dba_ref, dbx_ref, dcb_ref, dcw_ref,
             hext, xext, dext, carry, later, a_sc, b_sc, d_sc, l_sc):
        i = pl.program_id(0)

        @pl.when(i == 0)
        def _():
            for ref in (dsp_ref, dba_ref, dbx_ref, dcb_ref, dcw_ref, carry, later):
                ref[...] = jnp.zeros_like(ref)

        dy = _dot_nt(dh_ref[...].astype(BF), wo_ref[...])
        gl, dgl = _gelu_parts(g_ref[...].astype(F32))
        hv, av = h_ref[...], a_ref[...]
        dhd = dy * gl
        dz_ref[:, pl.ds(0, R)] = (dy * hv * dgl).astype(BF)
        d_sc[...] = dhd
        A = av.reshape(ng, SUBLANES, R)
        B = A * dhd.reshape(ng, SUBLANES, R)
        sub = lax.broadcasted_iota(jnp.int32, (1, SUBLANES, R), 1)
        for k in (1, 2, 4):
            m = sub < SUBLANES - k
            a_n = jnp.where(m, pltpu.roll(A, SUBLANES - k, 1), 1.0)
            b_n = jnp.where(m, pltpu.roll(B, SUBLANES - k, 1), 0.0)
            B = A * b_n + B
            A = A * a_n
        a_sc[...] = A.reshape(tb, R)
        b_sc[...] = B.reshape(tb, R)
        sub8 = lax.broadcasted_iota(jnp.int32, (SUBLANES, R), 0)

        def group(q, c):
            rows = pl.ds(pl.multiple_of((ng - 1 - q) * SUBLANES, SUBLANES), SUBLANES)
            mu = a_sc[rows, :] * c + b_sc[rows, :]
            l_sc[rows, :] = d_sc[rows, :] + jnp.where(sub8 == SUBLANES - 1, c, pltpu.roll(mu, SUBLANES - 1, 0))
            return mu[0:1, :]

        carry[...] = lax.fori_loop(0, ng, group, carry[...])
        lam = l_sc[...]
        hext[pl.ds(0, HALO), :] = jnp.where(i < nt - 1, hp_ref[...], 0.0)
        hext[pl.ds(HALO, tb), :] = hv
        h_prev = hext[pl.ds(HALO - 1, tb), :]
        rv, igv, xcv = r_ref[...], ig_ref[...], xc_ref[...]
        sp = _softplus_neg(ap_ref[...])
        mult = jnp.sqrt(_one_minus_sq(-LRU_C * rv * sp, av))
        dla = lam * h_prev * av - lam * (igv * xcv) * (av * av) / mult
        du = lam * mult
        dpa = (dla * (-LRU_C) * sp) * rv * (1.0 - rv)
        dpx = (du * xcv) * igv * (1.0 - igv)
        dsp_ref[...] += jnp.sum(dla * (-LRU_C) * rv, axis=0, keepdims=True)
        dba_ref[...] += jnp.sum(dpa, axis=0, keepdims=True)
        dbx_ref[...] += jnp.sum(dpx, axis=0, keepdims=True)
        dpab, dpxb = dpa.astype(BF), dpx.astype(BF)
        dpa_ref[...] = dpab
        dpx_ref[...] = dpxb
        dxc = du * igv + _dot_nt(dpab, wa_ref[...]) + _dot_nt(dpxb, wx_ref[...])
        dext[pl.ds(0, tb), :] = dxc
        dext[pl.ds(tb, SUBLANES), :] = later[...]
        later[...] = dxc[0:SUBLANES, :]
        xext[pl.ds(0, HALO), :] = jnp.where(i < nt - 1, xp_ref[...], 0.0)
        xext[pl.ds(HALO, tb), :] = x_ref[...]
        dxb = cw_ref[CONV_WIDTH - 1:CONV_WIDTH, :] * dxc
        for k in range(CONV_WIDTH - 1):
            dxb = dxb + cw_ref[k:k + 1, :] * dext[pl.ds(CONV_WIDTH - 1 - k, tb), :]
        dz_ref[:, pl.ds(R, R)] = dxb.astype(BF)
        for k in range(CONV_WIDTH):
            dcw_ref[k:k + 1, :] += jnp.sum(dxc * xext[pl.ds(HALO - 3 + k, tb), :], axis=0, keepdims=True)
        dcb_ref[...] += jnp.sum(dxc, axis=0, keepdims=True)

        @pl.when(i == nt - 1)
        def _():
            dsp_ref[...] = dsp_ref[...] * (-_sigmoid(-ap_ref[...]))

    once = lambda rows, cols: pl.BlockSpec((rows, cols), lambda i: (0, 0), pipeline_mode=pl.Buffered(1))
    t0 = pl.BlockSpec((tb, R), lambda i: (nt - 1 - i, 0))
    prev = pl.BlockSpec((HALO, R), lambda i: (jnp.maximum((nt - 1 - i) * per - 1, 0), 0))
    return pl.pallas_call(
        body, name=name, grid=(nt,),
        in_specs=[pl.BlockSpec((tb, D), lambda i: (nt - 1 - i, 0)), t0, prev, t0, t0, prev, t0, t0, t0, t0,
                  once(R, R), once(R, R), once(R, D), _vec_spec(R, CONV_WIDTH), _vec_spec(R)],
        out_specs=[pl.BlockSpec((tb, 2 * R), lambda i: (nt - 1 - i, 0)), t0, t0] + [_vec_spec(R)] * 4
        + [_vec_spec(R, SUBLANES)],
        out_shape=[jax.ShapeDtypeStruct((T, 2 * R), BF)] + [jax.ShapeDtypeStruct((T, R), BF)] * 2
        + [jax.ShapeDtypeStruct((1, R), F32)] * 4
        + [jax.ShapeDtypeStruct((SUBLANES, R), F32)],
        scratch_shapes=[pltpu.VMEM((HALO + tb, R), F32), pltpu.VMEM((HALO + tb, R), F32),
                        pltpu.VMEM((tb + SUBLANES, R), F32), pltpu.VMEM((1, R), F32), pltpu.VMEM((SUBLANES, R), F32)]
        + [pltpu.VMEM((tb, R), F32)] * 4,
        compiler_params=_params("arbitrary"),
    )(dh, hs, hs, gb, xb, xb, a, r, ig, xc, wa, wx, wout, conv_w, a_param.reshape(1, R))


def _gate_spans(R):
    d = R // LRU_HEADS
    spans = [min((j * GATE_COLS // d) * d // LANES * LANES, R - GATE_SPAN) for j in range(R // GATE_COLS)]
    assert R % GATE_COLS == 0 and all(lo + GATE_SPAN >= (((j + 1) * GATE_COLS - 1) // d + 1) * d for j, lo in enumerate(spans))
    return spans


def _lru_gates_dw(xc, dpa, dpx, name):
    T, R = xc.shape
    tk = _tile(T, 1024, 16)
    spans = _gate_spans(R)
    nb = len(spans)

    def body(x_ref, a_ref, b_ref, o_ref):
        @pl.when(pl.program_id(0) == 0)
        def _():
            o_ref[...] = jnp.zeros_like(o_ref)

        for j, lo in enumerate(spans):
            xs = x_ref[:, pl.ds(lo, GATE_SPAN)].astype(BF)
            cols = pl.ds(j * GATE_COLS, GATE_COLS)
            o_ref[0, j] += _dot_tn(xs, a_ref[:, cols])
            o_ref[1, j] += _dot_tn(xs, b_ref[:, cols])

    row = _row_spec(tk, R)
    out = pl.pallas_call(
        body, name=name, grid=(T // tk,), in_specs=[row, row, row],
        out_specs=pl.BlockSpec((2, nb, GATE_SPAN, GATE_COLS), lambda i: (0, 0, 0, 0)),
        out_shape=jax.ShapeDtypeStruct((2, nb, GATE_SPAN, GATE_COLS), F32), compiler_params=_params("arbitrary"),
    )(xc, dpa, dpx)
    dense = jnp.zeros((2, R, R), F32)
    for j, lo in enumerate(spans):
        dense = dense.at[:, lo:lo + GATE_SPAN, j * GATE_COLS:(j + 1) * GATE_COLS].set(out[:, j])
    return dense[0], dense[1]


def _window_sums(e, n, back):
    out, s = [], e
    for k in (1, 2, 4, 8):
        s = s + pltpu.roll(s, k if back else n - k, 0)
        out.append(s)
    return out


def _pool_fwd(h, g, w, b, scale, name):
    T, D = h.shape
    G = len(POOL_WINDOWS)
    gd = D // G
    tb = _tile(T, TB_SEQ, HALO)
    per = tb // HALO

    def body(h_ref, hp_ref, g_ref, w_ref, b_ref, s_ref, o_ref, u_ref, yb_ref):
        i = pl.program_id(0)
        t = i * tb + lax.broadcasted_iota(jnp.int32, (tb, gd), 0) + 1
        hv = h_ref[...]
        xn = _rms_parts(hv, g_ref[...])[0]
        xp = jnp.where(i > 0, _rms_parts(hp_ref[...], g_ref[...])[0], 0.0)
        for k, win in enumerate(POOL_WINDOWS):
            cols = slice(k * gd, (k + 1) * gd)
            x = xn[:, cols]
            e = jnp.concatenate([xp[:, cols], x], axis=0)
            sw = _window_sums(e, HALO + tb, True)[k][HALO:, :]
            u = (sw / jnp.minimum(t, win).astype(F32) - x).astype(BF)
            yb = _dot(u, w_ref[k]) + b_ref[:, cols]
            u_ref[:, cols] = u
            yb_ref[:, cols] = yb
            o_ref[:, cols] = hv[:, cols] + yb * s_ref[:, cols]

    tile = _row_spec(tb, D)
    prev = pl.BlockSpec((HALO, D), lambda i: (jnp.maximum(i * per - 1, 0), 0))
    return pl.pallas_call(
        body, name=name, grid=(T // tb,),
        in_specs=[tile, prev, _vec_spec(D), pl.BlockSpec((G, gd, gd), lambda i: (0, 0, 0)), _vec_spec(D), _vec_spec(D)],
        out_specs=[tile, tile, tile],
        out_shape=[jax.ShapeDtypeStruct((T, D), F32), jax.ShapeDtypeStruct((T, D), BF), jax.ShapeDtypeStruct((T, D), F32)],
        compiler_params=_params("parallel"),
    )(h, h, g.reshape(1, D), w, b.reshape(1, D), scale.reshape(1, D))


def _pool_bwd(dm, u, yb, w, scale, h, g, name):
    T, D = dm.shape
    G = len(POOL_WINDOWS)
    gd = D // G
    tb = _tile(T, TB_SEQ, HALO)
    nt = T // tb

    def body(d_ref, u_ref, yb_ref, w_ref, s_ref, h_ref, g_ref, o_ref, dw_ref, db_ref, ds_ref, dg_ref, later):
        i = pl.program_id(0)

        @pl.when(i == 0)
        def _():
            for ref in (dw_ref, db_ref, ds_ref, dg_ref, later):
                ref[...] = jnp.zeros_like(ref)

        d, sc = d_ref[...], s_ref[...]
        ds_ref[...] += jnp.sum(d * yb_ref[...], axis=0, keepdims=True)
        db_ref[...] += jnp.sum(d * sc, axis=0, keepdims=True)
        t = (nt - 1 - i) * tb + lax.broadcasted_iota(jnp.int32, (tb, gd), 0) + 1
        parts = []
        for k, win in enumerate(POOL_WINDOWS):
            cols = slice(k * gd, (k + 1) * gd)
            dy = (d[:, cols] * sc[:, cols]).astype(BF)
            du = _dot_nt(dy, w_ref[k])
            dw_ref[k] += _dot_tn(u_ref[:, cols], dy)
            v = du / jnp.minimum(t, win).astype(F32)
            e = jnp.concatenate([v, later[:, cols]], axis=0)
            later[:, cols] = v[0:HALO, :]
            parts.append(_window_sums(e, tb + HALO, False)[k][:tb, :] - du)
        dx, dg = _rms_bwd_parts(h_ref[...], g_ref[...], jnp.concatenate(parts, axis=1))
        o_ref[...] = d + dx
        dg_ref[...] += dg

    tile = pl.BlockSpec((tb, D), lambda i: (nt - 1 - i, 0))
    whole = pl.BlockSpec((G, gd, gd), lambda i: (0, 0, 0))
    return pl.pallas_call(
        body, name=name, grid=(nt,), in_specs=[tile, tile, tile, whole, _vec_spec(D), tile, _vec_spec(D)],
        out_specs=[tile, whole, _vec_spec(D), _vec_spec(D), _vec_spec(D)],
        out_shape=[jax.ShapeDtypeStruct((T, D), F32), jax.ShapeDtypeStruct((G, gd, gd), F32)]
        + [jax.ShapeDtypeStruct((1, D), F32)] * 3,
        scratch_shapes=[pltpu.VMEM((HALO, D), F32)], compiler_params=_params("arbitrary"),
    )(dm, u, yb, w, scale.reshape(1, D), h, g.reshape(1, D))


def _adamw(w, g, m, v, name):
    shape = w.shape
    cols = shape[-1] if w.ndim > 1 else shape[0]
    rows = w.size // cols
    tr = _tile(rows, TR_ADAM, SUBLANES)
    c1, c2 = 1.0 / (1.0 - ADAM_B1 ** ADAM_STEP), 1.0 / (1.0 - ADAM_B2 ** ADAM_STEP)

    def body(w_ref, g_ref, m_ref, v_ref, d_ref, mo_ref, vo_ref):
        gv = g_ref[...]
        mn = ADAM_B1 * m_ref[...] + (1.0 - ADAM_B1) * gv
        vn = ADAM_B2 * v_ref[...] + (1.0 - ADAM_B2) * (gv * gv)
        d_ref[...] = -ADAM_LR * ((mn * c1) / (jnp.sqrt(vn * c2) + ADAM_EPS) + ADAM_WD * w_ref[...])
        mo_ref[...] = mn
        vo_ref[...] = vn

    spec = _row_spec(tr, cols)
    outs = pl.pallas_call(
        body, name=name, grid=(rows // tr,), in_specs=[spec] * 4, out_specs=[spec] * 3,
        out_shape=[jax.ShapeDtypeStruct((rows, cols), F32)] * 3, compiler_params=_params("parallel"),
    )(*[t.reshape(rows, cols) for t in (w, g, m, v)])
    return [o.reshape(shape) for o in outs]


def _sum_devices(parts, name):
    n, rows, cols = parts.shape
    tr = _tile(rows, 1024, SUBLANES)

    def body(p_ref, o_ref):
        acc = p_ref[0].astype(F32)
        for k in range(1, n):
            acc = acc + p_ref[k].astype(F32)
        o_ref[...] = acc

    return pl.pallas_call(
        body, name=name, grid=(rows // tr,), in_specs=[pl.BlockSpec((n, tr, cols), lambda i: (0, i, 0))],
        out_specs=_row_spec(tr, cols), out_shape=jax.ShapeDtypeStruct((rows, cols), F32),
        compiler_params=_params("parallel"),
    )(parts)


def _position():
    return lax.axis_index("x"), lax.axis_index("y"), lax.axis_index("c")


def _gathered_shapes(blocks):
    return [jax.ShapeDtypeStruct((b.shape[0], N_DEV * b.shape[1], b.shape[2]), b.dtype) for b in blocks]


def _gather_sems(ng):
    return [pltpu.SemaphoreType.DMA((ng, 7)), pltpu.SemaphoreType.DMA((ng, 7)), pltpu.SemaphoreType.DMA((ng,))] if ng else []


def _gather_plan(blocks, srcs, outs, send_sems, recv_sems, local_sems):
    ng = len(blocks)
    x, y, c = _position()
    me, sibling = (x, y, c), (x, y, 1 - c)
    chips = [(1 - x, y), (x, 1 - y), (1 - x, 1 - y)]

    def rows(g, px, py, pc):
        r = blocks[g].shape[1]
        return outs[g].at[:, pl.ds((4 * px + 2 * py + pc) * r, r), :]

    def copy(g, k, block, to, src=None):
        return pltpu.make_async_remote_copy(
            src_ref=rows(g, *block) if src is None else src, dst_ref=rows(g, *block),
            send_sem=send_sems.at[g, k], recv_sem=recv_sems.at[g, k], device_id=to, device_id_type=MESH)

    def mine(g):
        return pltpu.make_async_copy(srcs[g], rows(g, *me), local_sems.at[g])

    def first(g):
        return [copy(g, 0, me, sibling, src=srcs[g])] + [copy(g, 1 + j, me, (*chip, c), src=srcs[g])
                                                         for j, chip in enumerate(chips)]

    def passed(g):
        return [copy(g, 4 + j, (*chip, c), sibling) for j, chip in enumerate(chips)]

    def start():
        for g in range(ng):
            mine(g).start()
            for cp in first(g):
                cp.start()

    def forward():
        for j, chip in enumerate(chips):
            for g in range(ng):
                copy(g, 1 + j, (*chip, c), me).wait_recv()
                copy(g, 4 + j, (*chip, c), sibling).start()

    def finish():
        for g in range(ng):
            copy(g, 0, sibling, me).wait_recv()
            for j, chip in enumerate(chips):
                copy(g, 4 + j, (*chip, 1 - c), me).wait_recv()
            for cp in first(g) + passed(g):
                cp.wait_send()
            mine(g).wait()

    return start, forward, finish


def _all_gather(blocks, name):
    ng = len(blocks)

    def body(*refs):
        start, forward, finish = _gather_plan(blocks, refs[:ng], refs[ng:2 * ng], *refs[2 * ng:])
        start()
        forward()
        finish()

    hbm = pl.BlockSpec(memory_space=pl.ANY)
    return pl.pallas_call(
        body, name=name, in_specs=[hbm] * ng, out_specs=[hbm] * ng, out_shape=_gathered_shapes(blocks),
        scratch_shapes=_gather_sems(ng),
    )(*blocks)


FLIPS = ((0, 0, 1), (1, 0, 0), (0, 1, 0), (1, 1, 0), (1, 0, 1), (0, 1, 1), (1, 1, 1))


def _piece_rows(piece):
    arr, m = piece
    return arr.shape[0] if m is None else 1


def _scattered_shapes(pieces):
    return [jax.ShapeDtypeStruct((N_DEV, _piece_rows(p), p[0].shape[1] // N_DEV, p[0].shape[2]), p[0].dtype)
            for p in pieces]


def _scatter_sems(ng):
    n = len(FLIPS)
    return [pltpu.SemaphoreType.DMA((ng, n)), pltpu.SemaphoreType.DMA((ng, n)), pltpu.SemaphoreType.DMA((ng,))] if ng else []


def _scatter_plan(pieces, srcs, outs, send_sems, recv_sems, local_sems):
    x, y, c = _position()

    def block(g, tx, ty, tc):
        arr, m = pieces[g]
        r = arr.shape[1] // N_DEV
        lead = slice(None) if m is None else pl.ds(m, 1)
        return srcs[g].at[lead, pl.ds((4 * tx + 2 * ty + tc) * r, r), :]

    def copies(g):
        out = []
        for k, (fx, fy, fc) in enumerate(FLIPS):
            tx, ty, tc = (1 - x if fx else x), (1 - y if fy else y), (1 - c if fc else c)
            out.append(pltpu.make_async_remote_copy(
                src_ref=block(g, tx, ty, tc), dst_ref=outs[g].at[k], send_sem=send_sems.at[g, k],
                recv_sem=recv_sems.at[g, k], device_id=(tx, ty, tc), device_id_type=MESH))
        return out

    def mine(g):
        return pltpu.make_async_copy(block(g, x, y, c), outs[g].at[len(FLIPS)], local_sems.at[g])

    def start():
        for g in range(len(pieces)):
            mine(g).start()
            for cp in copies(g):
                cp.start()

    def finish():
        for g in range(len(pieces)):
            for cp in copies(g):
                cp.wait()
            mine(g).wait()

    return start, finish


def _scatter_and_gather(pieces, blocks, name):
    n_p, n_b = len(pieces), len(blocks)

    def body(*refs):
        ins, outs, sems = refs[:n_p + n_b], refs[n_p + n_b:2 * (n_p + n_b)], refs[2 * (n_p + n_b):]
        s_start, s_finish = _scatter_plan(pieces, ins[:n_p], outs[:n_p], *sems[:3])
        g_start, g_forward, g_finish = _gather_plan(blocks, ins[n_p:], outs[n_p:], *sems[3:])
        s_start()
        g_start()
        g_forward()
        g_finish()
        s_finish()

    hbm = pl.BlockSpec(memory_space=pl.ANY)
    outs = pl.pallas_call(
        body, name=name, in_specs=[hbm] * (n_p + n_b), out_specs=[hbm] * (n_p + n_b),
        out_shape=_scattered_shapes(pieces) + _gathered_shapes(blocks),
        scratch_shapes=_scatter_sems(n_p) + _gather_sems(n_b),
    )(*[p[0] for p in pieces], *blocks)
    return list(outs[:n_p]), list(outs[n_p:])


def _scatter_sum(recvs, name):
    _, n, r, c = recvs[0].shape
    tr = _tile(r, TR_SUM, 16)

    def body(*refs):
        o_ref = refs[-1]
        for p, r_ref in enumerate(refs[:-1]):
            acc = r_ref[len(FLIPS)].astype(F32)
            for k in range(len(FLIPS)):
                acc = acc + r_ref[k].astype(F32)
            o_ref[p] = acc

    return pl.pallas_call(
        body, name=name, grid=(n, r // tr),
        in_specs=[pl.BlockSpec((N_DEV, None, tr, c), lambda i, j: (0, i, j, 0))] * len(recvs),
        out_specs=pl.BlockSpec((len(recvs), None, tr, c), lambda i, j: (0, i, j, 0)),
        out_shape=jax.ShapeDtypeStruct((len(recvs), n, r, c), F32), compiler_params=_params("parallel", "parallel"),
    )(*recvs)


def _block_diag(w):
    H, d, _ = w.shape
    return (jnp.eye(H, dtype=w.dtype)[:, None, :, None] * w[:, :, None, :]).reshape(H * d, H * d)


def _diag_blocks(dense, H):
    d = dense.shape[0] // H
    return jnp.stack([dense[i * d:(i + 1) * d, i * d:(i + 1) * d] for i in range(H)])


def _local_step(x, p, tgt, W, blocks=None):
    dist = blocks is not None
    L = p.shape[0]
    W = dict(W)

    def gathering(keys):
        return [k for k in keys if k not in W] if dist else []

    def ffn_fwd(h, g, i, f, during_act, during_out, ple=None):
        w = W[("ffn", i, f)]
        keys = gathering(during_act)
        a, b, s, n, got = _ffn_fwd_act(h, g, w, f"ffn{f}_fwd_act_{i}", gather=[blocks[k] for k in keys])
        W.update(zip(keys, got))
        keys = gathering(during_out)
        if ple is not None:
            ple = (W["ple_norm"][i], W[("ple_gate", i)][0], W[("ple_proj", i)][0], ple)
        h, emb, got = _ffn_fwd_out(s, w, h, f"ffn{f}_fwd_out_{i}", gather=[blocks[k] for k in keys], ple=ple)
        W.update(zip(keys, got))
        return (a, b, s, n), h, emb

    saved = []
    h = x
    for i in range(L):
        j = i // 2
        lru = i % 2 == 0
        s = {"h0": h}
        mixer = [("lru_in", j), ("lru_out", j)] if lru else [("pool_w", j)]
        s["ffn1"], h, _ = ffn_fwd(h, W["ffn1_norm"][i], i, 1, [("ffn", i, 2)], mixer)
        s["h1"] = h
        if lru:
            hn, gb, xb = _lru_in(h, W["mix_norm"][i], W[("lru_in", j)][0], f"lru_in_{i}")
            wa, wx = _block_diag(W["lru_w_a"][j]).astype(BF), _block_diag(W["lru_w_x"][j]).astype(BF)
            h, xc, r, ig, a, hs, y = _lru_fwd(gb, xb, W["lru_conv_w"][j], W["lru_conv_b"][j], wa, wx, W["lru_b_a"][j],
                                              W["lru_b_x"][j], W["lru_a_param"][j], W[("lru_out", j)][0], h, f"lru_fwd_{i}")
            s.update(hn=hn, gb=gb, xb=xb, wa=wa, wx=wx, xc=xc, r=r, ig=ig, a=a, hs=hs, y=y)
        else:
            h, s["u"], s["yb"] = _pool_fwd(h, W["mix_norm"][i], W[("pool_w", j)], W["pool_b"][j], W["pool_scale"][j],
                                           f"pool_fwd_{i}")
        s["h2"] = h
        nxt = [("ffn", i + 1, 1)] if i + 1 < L else []
        s["ffn2"], s["h3"], (h, s["n4"], s["gate"], s["pp"]) = ffn_fwd(
            h, W["ffn2_norm"][i], i, 2, [("ple_gate", i), ("ple_proj", i)] + nxt, [], ple=p[i])
        saved.append(s)

    loss, dh, d_final = _loss_head(h, W["final_norm"], tgt)

    big, recv = {}, {}
    n_lru, n_pool = L // 2 + L % 2, L // 2
    small = {k: [None] * L for k in ("ffn1_norm", "mix_norm", "ffn2_norm", "ple_norm")}
    for k in ("lru_conv_w", "lru_conv_b", "lru_w_a", "lru_b_a", "lru_w_x", "lru_b_x", "lru_a_param"):
        small[k] = [None] * n_lru
    for k in ("pool_b", "pool_scale"):
        small[k] = [None] * n_pool

    def scattering(pieces):
        return [(k, m) for k, m in pieces if k in big] if dist else []

    def ffn_bwd(dh, h_in, g, acts, i, f, during):
        key, w = ("ffn", i, f), W[("ffn", i, f)]
        a, b, sv, n = acts
        out = [scattering(d) for d in during]
        sent = [[(big[k], m) for k, m in o] for o in out]
        da, db, dhb, got0 = _ffn_bwd_act(dh, a, b, w, f"ffn{f}_bwd_act_{i}", scatter=sent[0])
        big[key], got1 = _ffn_bwd_w(da, db, sv, n, dhb, f"ffn{f}_dw_{i}", scatter=sent[1])
        out.append(scattering([(key, 0)] + ([(key, 1)] if (i, f) == (0, 1) else [])))
        dh, dg, got2 = _ffn_bwd_in(da, db, w, h_in, g, dh, f"ffn{f}_bwd_in_{i}", scatter=[(big[k], m) for k, m in out[2]])
        for o, got in zip(out, (got0, got1, got2)):
            recv.update(zip(o, got))
        return dh, dg

    for i in reversed(range(L)):
        j = i // 2
        lru = i % 2 == 0
        s = saved[i]
        dh, dz, dpp, dg = _ple_bwd(dh, s["gate"], s["pp"], s["h3"], W["ple_norm"][i], W[("ple_gate", i)][0],
                                   f"ple_bwd_{i}")
        big[("ple_gate", i)] = _mm_tn(s["n4"], dz, f"ple_gate_dw_{i}")[None]
        big[("ple_proj", i)] = _mm_tn(dpp, p[i], f"ple_proj_dw_{i}")[None]
        small["ple_norm"][i] = dg[0]
        above = ("ffn", i + 1, 1)
        dh, dg = ffn_bwd(dh, s["h2"], W["ffn2_norm"][i], s["ffn2"], i, 2, [
            [(above, 1)], [(above, 2), (("ple_gate", i), None), (("ple_proj", i), None)]])
        small["ffn2_norm"][i] = dg[0]
        if lru:
            big[("lru_out", j)] = _mm_tn(s["y"], dh, f"lru_out_dw_{i}")[None]
            dz, dpa, dpx, dsp, dba, dbx, dcb, dcw = _lru_bwd(
                dh, s["hs"], s["gb"], s["xb"], s["a"], s["r"], s["ig"], s["xc"], s["wa"], s["wx"], W[("lru_out", j)][0],
                W["lru_conv_w"][j], W["lru_a_param"][j], f"lru_bwd_{i}")
            small["lru_a_param"][j], small["lru_b_a"][j], small["lru_b_x"][j] = dsp[0], dba[0], dbx[0]
            dwa, dwx = _lru_gates_dw(s["xc"], dpa, dpx, f"lru_gates_dw_{i}")
            small["lru_w_a"][j], small["lru_w_x"][j] = _diag_blocks(dwa, LRU_HEADS), _diag_blocks(dwx, LRU_HEADS)
            small["lru_conv_w"][j], small["lru_conv_b"][j] = dcw[:CONV_WIDTH], dcb[0]
            big[("lru_in", j)] = _mm_tn(dz, s["hn"], f"lru_in_dw_{i}")[None]
            dh, dg = _lru_in_bwd(dz, W[("lru_in", j)][0], s["h1"], W["mix_norm"][i], dh, f"lru_in_bwd_{i}")
            mixer = [("lru_in", j), ("lru_out", j)]
        else:
            dh, dw, dbp, dsc, dg = _pool_bwd(dh, s["u"], s["yb"], W[("pool_w", j)], W["pool_scale"][j], s["h1"],
                                             W["mix_norm"][i], f"pool_bwd_{i}")
            big[("pool_w", j)] = dw.astype(BF)
            small["pool_b"][j], small["pool_scale"][j] = dbp[0], dsc[0]
            mixer = [("pool_w", j)]
        small["mix_norm"][i] = dg[0]
        second = ("ffn", i, 2)
        dh, dg = ffn_bwd(dh, s["h0"], W["ffn1_norm"][i], s["ffn1"], i, 1, [
            [(second, 1)], [(second, 2)] + [(k, None) for k in mixer]])
        small["ffn1_norm"][i] = dg[0]

    small = {k: jnp.stack(v) for k, v in small.items()}
    small["final_norm"] = d_final[0]
    return loss, dh, big, recv, small


SMALL_SHARDED = ("pool_b", "pool_scale", "lru_conv_w")
SMALL = ("ffn1_norm", "mix_norm", "ffn2_norm", "ple_norm", "final_norm", "lru_conv_b", "lru_w_a", "lru_b_a",
         "lru_w_x", "lru_b_x", "lru_a_param", "pool_b", "pool_scale", "lru_conv_w")


def _pack_big(w):
    t = lambda a: jnp.swapaxes(a, -1, -2)
    out = {}
    for i in range(w["ffn1_norm"].shape[0]):
        for f in (1, 2):
            out[("ffn", i, f)] = jnp.stack([t(w[f"ffn{f}_w_gate"][i]), t(w[f"ffn{f}_w_up"][i]), w[f"ffn{f}_w_down"][i]])
        out[("ple_gate", i)], out[("ple_proj", i)] = w["ple_w_gate"][i][None], t(w["ple_w_proj"][i])[None]
    for j in range(w["lru_w_in"].shape[0]):
        out[("lru_in", j)], out[("lru_out", j)] = t(w["lru_w_in"][j])[None], w["lru_w_out"][j][None]
    for j in range(w["pool_w"].shape[0]):
        out[("pool_w", j)] = w["pool_w"][j]
    return out


def _unpack_big(b, L):
    t = lambda a: jnp.swapaxes(a, -1, -2)
    n_lru, n_pool = L // 2 + L % 2, L // 2
    out = {"lru_w_in": jnp.stack([t(b[("lru_in", j)][0]) for j in range(n_lru)]),
           "lru_w_out": jnp.stack([b[("lru_out", j)][0] for j in range(n_lru)]),
           "pool_w": jnp.stack([b[("pool_w", j)] for j in range(n_pool)]),
           "ple_w_gate": jnp.stack([b[("ple_gate", i)][0] for i in range(L)]),
           "ple_w_proj": jnp.stack([t(b[("ple_proj", i)][0]) for i in range(L)])}
    for f in (1, 2):
        out[f"ffn{f}_w_gate"] = jnp.stack([t(b[("ffn", i, f)][0]) for i in range(L)])
        out[f"ffn{f}_w_up"] = jnp.stack([t(b[("ffn", i, f)][1]) for i in range(L)])
        out[f"ffn{f}_w_down"] = jnp.stack([b[("ffn", i, f)][2] for i in range(L)])
    return out


def _flatten(parts, names, rows_of=LANES):
    flat = jnp.concatenate([parts[k].reshape(-1) for k in names])
    pad = (-flat.size) % (16 * rows_of)
    return jnp.pad(flat, (0, pad)).reshape(1, -1, rows_of)


def _unflatten(flat, like, names):
    out, o = {}, 0
    flat = flat.reshape(-1)
    for k in names:
        n = like[k].size
        out[k] = flat[o:o + n].reshape(like[k].shape)
        o += n
    return out


def kernel(x, p, ffn1_norm, ffn1_w_gate, ffn1_w_up, ffn1_w_down, mix_norm, lru_w_in, lru_conv_w, lru_conv_b, lru_w_a, lru_b_a, lru_w_x, lru_b_x, lru_a_param, lru_w_out, pool_w, pool_b, pool_scale, ffn2_norm, ffn2_w_gate, ffn2_w_up, ffn2_w_down, ple_norm, ple_w_gate, ple_w_proj, final_norm, loss_target, m_ffn1_norm, m_ffn1_w_gate, m_ffn1_w_up, m_ffn1_w_down, m_mix_norm, m_lru_w_in, m_lru_conv_w, m_lru_conv_b, m_lru_w_a, m_lru_b_a, m_lru_w_x, m_lru_b_x, m_lru_a_param, m_lru_w_out, m_pool_w, m_pool_b, m_pool_scale, m_ffn2_norm, m_ffn2_w_gate, m_ffn2_w_up, m_ffn2_w_down, m_ple_norm, m_ple_w_gate, m_ple_w_proj, m_final_norm, v_ffn1_norm, v_ffn1_w_gate, v_ffn1_w_up, v_ffn1_w_down, v_mix_norm, v_lru_w_in, v_lru_conv_w, v_lru_conv_b, v_lru_w_a, v_lru_b_a, v_lru_w_x, v_lru_b_x, v_lru_a_param, v_lru_w_out, v_pool_w, v_pool_b, v_pool_scale, v_ffn2_norm, v_ffn2_w_gate, v_ffn2_w_up, v_ffn2_w_down, v_ple_norm, v_ple_w_gate, v_ple_w_proj, v_final_norm):
    names = ["ffn1_norm", "ffn1_w_gate", "ffn1_w_up", "ffn1_w_down", "mix_norm", "lru_w_in", "lru_conv_w", "lru_conv_b",
             "lru_w_a", "lru_b_a", "lru_w_x", "lru_b_x", "lru_a_param", "lru_w_out", "pool_w", "pool_b", "pool_scale",
             "ffn2_norm", "ffn2_w_gate", "ffn2_w_up", "ffn2_w_down", "ple_norm", "ple_w_gate", "ple_w_proj", "final_norm"]
    w = dict(zip(names, [ffn1_norm, ffn1_w_gate, ffn1_w_up, ffn1_w_down, mix_norm, lru_w_in, lru_conv_w, lru_conv_b, lru_w_a, lru_b_a, lru_w_x, lru_b_x, lru_a_param, lru_w_out, pool_w, pool_b, pool_scale, ffn2_norm, ffn2_w_gate, ffn2_w_up, ffn2_w_down, ple_norm, ple_w_gate, ple_w_proj, final_norm]))
    m = dict(zip(names, [m_ffn1_norm, m_ffn1_w_gate, m_ffn1_w_up, m_ffn1_w_down, m_mix_norm, m_lru_w_in, m_lru_conv_w, m_lru_conv_b, m_lru_w_a, m_lru_b_a, m_lru_w_x, m_lru_b_x, m_lru_a_param, m_lru_w_out, m_pool_w, m_pool_b, m_pool_scale, m_ffn2_norm, m_ffn2_w_gate, m_ffn2_w_up, m_ffn2_w_down, m_ple_norm, m_ple_w_gate, m_ple_w_proj, m_final_norm]))
    v = dict(zip(names, [v_ffn1_norm, v_ffn1_w_gate, v_ffn1_w_up, v_ffn1_w_down, v_mix_norm, v_lru_w_in, v_lru_conv_w, v_lru_conv_b, v_lru_w_a, v_lru_b_a, v_lru_w_x, v_lru_b_x, v_lru_a_param, v_lru_w_out, v_pool_w, v_pool_b, v_pool_scale, v_ffn2_norm, v_ffn2_w_gate, v_ffn2_w_up, v_ffn2_w_down, v_ple_norm, v_ple_w_gate, v_ple_w_proj, v_final_norm]))
    L = p.shape[0]
    px, py, pc = _position()
    me = 4 * px + 2 * py + pc

    blocks = {k: b.astype(BF) for k, b in _pack_big(w).items()}
    first = ("ffn", 0, 1)
    got, small_blocks = _all_gather([blocks[first], _flatten(w, SMALL_SHARDED)], "gather_first")
    W = {first: got}
    per_dev = small_blocks.reshape(N_DEV, -1)
    shards = [_unflatten(per_dev[k], w, SMALL_SHARDED) for k in range(N_DEV)]
    for k in SMALL:
        W[k] = jnp.concatenate([s[k] for s in shards], axis=-1) if k in SMALL_SHARDED else w[k]

    loss, dx, big, recv, small = _local_step(x[0], p[:, 0], loss_target[0], W, blocks)

    last = [(k, m) for k in big if (k, None) not in recv for m in range(big[k].shape[0]) if (k, m) not in recv]
    got, (parts,) = _scatter_and_gather([(big[k], m) for k, m in last], [_flatten(small, SMALL).astype(BF)],
                                        "scatter_last_gather_small")
    recv.update(zip(last, got))

    def total(k):
        tag = "sum_" + "_".join(map(str, k))
        if (k, None) in recv:
            return _scatter_sum([recv[(k, None)]], tag)[0]
        return _scatter_sum([recv[(k, m)] for m in range(big[k].shape[0])], tag)[:, 0]

    grads = _unpack_big({k: total(k) for k in big}, L)
    total_small = _sum_devices(parts.reshape(N_DEV, -1, LANES), "sum_small_grads")
    full = _unflatten(total_small, {k: W[k] for k in SMALL}, SMALL)
    for k in SMALL:
        if k in SMALL_SHARDED:
            n = w[k].shape[-1]
            grads[k] = lax.dynamic_slice_in_dim(full[k], me * n, n, axis=-1)
        else:
            grads[k] = full[k]

    delta, new_m, new_v = {}, {}, {}
    for k in names:
        delta[k], new_m[k], new_v[k] = _adamw(w[k], grads[k], m[k], v[k], f"adamw_{k}")
    total_loss = lax.psum(loss[0, 0], ("x", "y", "c"))
    return (total_loss, dx[None], *[grads[k] for k in names], *[delta[k] for k in names],
            *[new_m[k] for k in names], *[new_v[k] for k in names])
```

```python
import functools

import jax
import jax.numpy as jnp
from jax import lax
from jax.experimental import pallas as pl
from jax.experimental.pallas import tpu as pltpu

F32 = jnp.float32
BF = jnp.bfloat16
MESH = pl.DeviceIdType.MESH

RMS_EPS = 1e-6
LRU_C = 8.0
LRU_HEADS = 16
CONV_WIDTH = 4
POOL_WINDOWS = (2, 4, 8, 16)
ADAM_LR, ADAM_B1, ADAM_B2, ADAM_EPS, ADAM_WD, ADAM_STEP = 0.001, 0.9, 0.999, 1e-08, 0.01, 10

N_DEV = 8
LANES = 128
SUBLANES = 8
GATE_COLS, GATE_SPAN = 256, 512
HALO = 16
VMEM_LIMIT = 56 * 1024 * 1024

TM_FFN_ACT = 2048
TM_FFN_IN = 512
TF_FFN = 256
TF_FFN_WG = 1408
TK_FFN_WG = 512
TB_SEQ = 256
TM_EW = 512
TM_MM, TN_MM, TK_MM = 1280, 1024, 1024
TR_SUM = 176
TR_ADAM = 1024


def _tile(n, pref, align):
    if n <= pref:
        return n
    t = (pref // align) * align
    while t >= align:
        if n % t == 0:
            return t
        t -= align
    raise ValueError(f"no tile for {n} (pref {pref}, align {align})")


def _params(*sem):
    return pltpu.CompilerParams(dimension_semantics=sem, vmem_limit_bytes=VMEM_LIMIT)


def _dot(a, b):
    return lax.dot_general(a, b, (((1,), (0,)), ((), ())), preferred_element_type=F32)


def _dot_nt(a, b):
    return lax.dot_general(a, b, (((1,), (1,)), ((), ())), preferred_element_type=F32)


def _dot_tn(a, b):
    return lax.dot_general(a, b, (((0,), (0,)), ((), ())), preferred_element_type=F32)


def _sigmoid(x):
    return 0.5 + 0.5 * jnp.tanh(0.5 * x)


def _sigmoid_pos(x):
    return 1.0 / (1.0 + jnp.exp(-x))


def _gelu_parts(x):
    k0, k1 = 0.7978845608028654, 0.044715
    t = jnp.tanh(k0 * (x + k1 * x * x * x))
    g = 0.5 * x * (1.0 + t)
    dg = 0.5 * (1.0 + t) + 0.5 * x * (1.0 - t * t) * k0 * (1.0 + 3.0 * k1 * x * x)
    return g, dg


def _one_minus_sq(la, a):
    return jnp.tanh(-la) * (1.0 + a * a)


def _softplus_neg(l):
    u = jnp.exp(-jnp.abs(l))
    w = 1.0 + u
    log1p = jnp.where(w == 1.0, u, jnp.log(w) * (u / jnp.where(w == 1.0, 1.0, w - 1.0)))
    return jnp.maximum(-l, 0.0) + log1p


def _rms_parts(x, g):
    r = lax.rsqrt(jnp.mean(x * x, axis=-1, keepdims=True) + RMS_EPS)
    nhat = x * r
    return nhat * g, nhat, r


def _rms_bwd_parts(x, g, dn):
    _, nhat, r = _rms_parts(x, g)
    u = dn * g
    dx = r * (u - nhat * jnp.mean(u * nhat, axis=-1, keepdims=True))
    return dx, jnp.sum(dn * nhat, axis=0, keepdims=True)


def _row_spec(tm, d, single=False):
    if single:
        return pl.BlockSpec((tm, d), lambda i, *_: (i, 0), pipeline_mode=pl.Buffered(1))
    return pl.BlockSpec((tm, d), lambda i, *_: (i, 0))


def _vec_spec(d, rows=1):
    return pl.BlockSpec((rows, d), lambda *_: (0, 0))


def _mm_tn(x, w, name):
    (K, M), (_, N) = x.shape, w.shape
    tm, tn, tk = _tile(M, TM_MM, LANES), _tile(N, TN_MM, LANES), _tile(K, TK_MM, 16)
    nk = K // tk

    def body(x_ref, w_ref, o_ref, acc):
        k = pl.program_id(2)

        @pl.when(k == 0)
        def _():
            acc[...] = jnp.zeros_like(acc)

        acc[...] += _dot_tn(x_ref[...].astype(BF), w_ref[...].astype(BF))

        @pl.when(k == nk - 1)
        def _():
            o_ref[...] = acc[...].astype(BF)

    return pl.pallas_call(
        body, name=name, grid=(M // tm, N // tn, nk),
        in_specs=[pl.BlockSpec((tk, tm), lambda i, j, k: (k, i)), pl.BlockSpec((tk, tn), lambda i, j, k: (k, j))],
        out_specs=pl.BlockSpec((tm, tn), lambda i, j, k: (i, j)),
        out_shape=jax.ShapeDtypeStruct((M, N), BF),
        scratch_shapes=[pltpu.VMEM((tm, tn), F32)],
        compiler_params=_params("parallel", "parallel", "arbitrary"),
    )(x, w)


def _loss_head(h, g, tgt):
    T, D = h.shape
    tm = _tile(T, TM_EW, 16)

    def body(h_ref, g_ref, t_ref, loss_ref, dh_ref, dg_ref):
        @pl.when(pl.program_id(0) == 0)
        def _():
            dg_ref[...] = jnp.zeros_like(dg_ref)
            loss_ref[...] = jnp.zeros_like(loss_ref)

        x, gg = h_ref[...], g_ref[...]
        y = _rms_parts(x, gg)[0]
        e = y - t_ref[...]
        part = jnp.sum(jnp.sum(e * e, axis=0, keepdims=True), axis=1, keepdims=True) * (0.5 / D)
        loss_ref[...] += jnp.broadcast_to(part, loss_ref.shape)
        dx, dg = _rms_bwd_parts(x, gg, e * (1.0 / D))
        dh_ref[...] = dx
        dg_ref[...] += dg

    return pl.pallas_call(
        body, name="loss_head", grid=(T // tm,),
        in_specs=[_row_spec(tm, D), _vec_spec(D), _row_spec(tm, D)],
        out_specs=[_vec_spec(LANES), _row_spec(tm, D), _vec_spec(D)],
        out_shape=[jax.ShapeDtypeStruct((1, LANES), F32), jax.ShapeDtypeStruct((T, D), F32),
                   jax.ShapeDtypeStruct((1, D), F32)],
        compiler_params=_params("arbitrary"),
    )(h, g.reshape(1, D), tgt)


def _carry(plan, first, mid, last):
    pl.when(first)(plan[0])
    if len(plan) == 3:
        pl.when(mid)(plan[1])
    pl.when(last)(plan[-1])


def _ffn_fwd_act(h, g, wffn, name, gather=()):
    T, D = h.shape
    F = wffn.shape[1]
    tm, tf = _tile(T, TM_FFN_ACT, 16), _tile(F, TF_FFN, LANES)
    ni, nf, ng = T // tm, F // tf, len(gather)

    def body(*refs):
        h_ref, g_ref, wg_ref, wu_ref = refs[:4]
        srcs, (a_ref, b_ref, s_ref, n_ref), outs = refs[4:4 + ng], refs[4 + ng:8 + ng], refs[8 + ng:8 + 2 * ng]
        i, j = pl.program_id(0), pl.program_id(1)
        if ng:
            _carry(_gather_plan(gather, srcs, outs, *refs[8 + 2 * ng:]), jnp.logical_and(i == 0, j == 0),
                   jnp.logical_and(i == (3 * ni) // 4, j == 0), jnp.logical_and(i == ni - 1, j == nf - 1))

        @pl.when(j == 0)
        def _():
            n_ref[...] = _rms_parts(h_ref[...], g_ref[...])[0].astype(BF)

        n = n_ref[...]
        a = _dot_nt(n, wg_ref[...])
        b = _dot_nt(n, wu_ref[...])
        a_ref[...] = a.astype(BF)
        b_ref[...] = b.astype(BF)
        s_ref[...] = (a * _sigmoid(a) * b).astype(BF)

    tile = pl.BlockSpec((tm, tf), lambda i, j: (i, j))
    w = [pl.BlockSpec((None, tf, D), functools.partial(lambda k, i, j: (k, j, 0), k)) for k in (0, 1)]
    hbm = pl.BlockSpec(memory_space=pl.ANY)
    outs = pl.pallas_call(
        body, name=name, grid=(ni, nf), in_specs=[_row_spec(tm, D), _vec_spec(D)] + w + [hbm] * ng,
        out_specs=[tile, tile, tile, _row_spec(tm, D)] + [hbm] * ng,
        out_shape=[jax.ShapeDtypeStruct((T, F), BF)] * 3 + [jax.ShapeDtypeStruct((T, D), BF)] + _gathered_shapes(gather),
        scratch_shapes=_gather_sems(ng), compiler_params=_params("arbitrary", "arbitrary"),
    )(h, g.reshape(1, D), wffn, wffn, *gather)
    return outs[0], outs[1], outs[2], outs[3], list(outs[4:])


def _ffn_fwd_out(s, wffn, h, name, gather=(), ple=None):
    T, F = s.shape
    D = h.shape[1]
    tm = _tile(T, TM_FFN_IN, 16)
    ni, ng, ne = T // tm, len(gather), 4 if ple else 0

    def body(*refs):
        s_ref, w_ref, h_ref = refs[:3]
        pin, rest = refs[3:3 + ne], refs[3 + ne:]
        srcs, o_ref, pout, outs = rest[:ng], rest[ng], rest[ng + 1:ng + 1 + ne], rest[ng + 1 + ne:2 * ng + 1 + ne]
        i = pl.program_id(0)
        if ng:
            _carry(_gather_plan(gather, srcs, outs, *rest[2 * ng + 1 + ne:]), i == 0, i == (3 * ni) // 4, i == ni - 1)
        x = h_ref[...] + 0.5 * _dot(s_ref[...], w_ref[...])
        o_ref[...] = x
        if ple:
            g_ref, wg_ref, wp_ref, p_ref = pin
            e_ref, n_ref, gate_ref, pp_ref = pout
            n = _rms_parts(x, g_ref[...])[0].astype(BF)
            gate = _sigmoid(_dot(n, wg_ref[...]))
            pp = _dot_nt(p_ref[...].astype(BF), wp_ref[...])
            e_ref[...] = x + gate * pp
            n_ref[...] = n
            gate_ref[...] = gate.astype(BF)
            pp_ref[...] = pp.astype(BF)

    hbm = pl.BlockSpec(memory_space=pl.ANY)
    row = _row_spec(tm, D)
    once = lambda rows, cols: pl.BlockSpec((rows, cols), lambda i: (0, 0), pipeline_mode=pl.Buffered(1))
    extra_in, extra_out, extra_shape, extra_args = [], [], [], []
    if ple:
        g, wg, wp, p = ple
        P = p.shape[1]
        extra_in = [_vec_spec(D), once(D, D), once(D, P), _row_spec(tm, P)]
        extra_out = [row] * 4
        extra_shape = [jax.ShapeDtypeStruct((T, D), F32)] + [jax.ShapeDtypeStruct((T, D), BF)] * 3
        extra_args = [g.reshape(1, D), wg, wp, p]
    outs = pl.pallas_call(
        body, name=name, grid=(ni,),
        in_specs=[_row_spec(tm, F), pl.BlockSpec((None, F, D), lambda i: (2, 0, 0), pipeline_mode=pl.Buffered(1)), row]
        + extra_in + [hbm] * ng,
        out_specs=[row] + extra_out + [hbm] * ng,
        out_shape=[jax.ShapeDtypeStruct((T, D), F32)] + extra_shape + _gathered_shapes(gather),
        scratch_shapes=_gather_sems(ng), compiler_params=_params("arbitrary"),
    )(s, wffn, h, *extra_args, *gather)
    return outs[0], list(outs[1:1 + ne]), list(outs[1 + ne:])


def _ffn_bwd_act(dh, a, b, wffn, name, scatter=()):
    T, D = dh.shape
    F = wffn.shape[1]
    tm, tf = _tile(T, TM_FFN_ACT, 16), _tile(F, TF_FFN, LANES)
    ni, nf, ng = T // tm, F // tf, len(scatter)

    def body(*refs):
        dh_ref, a_ref, b_ref, wd_ref = refs[:4]
        srcs, (da_ref, db_ref, dhb_ref), outs = refs[4:4 + ng], refs[4 + ng:7 + ng], refs[7 + ng:7 + 2 * ng]
        i, j = pl.program_id(0), pl.program_id(1)
        if ng:
            _carry(_scatter_plan(scatter, srcs, outs, *refs[7 + 2 * ng:]), jnp.logical_and(i == 0, j == 0), None,
                   jnp.logical_and(i == ni - 1, j == nf - 1))

        @pl.when(j == 0)
        def _():
            dhb_ref[...] = dh_ref[...].astype(BF)

        ds = 0.5 * _dot_nt(dhb_ref[...], wd_ref[...])
        av, bv = a_ref[...].astype(F32), b_ref[...].astype(F32)
        sig = _sigmoid(av)
        da_ref[...] = (ds * bv * (sig * (1.0 + av * (1.0 - sig)))).astype(BF)
        db_ref[...] = (ds * (av * sig)).astype(BF)

    tile = pl.BlockSpec((tm, tf), lambda i, j: (i, j))
    hbm = pl.BlockSpec(memory_space=pl.ANY)
    outs = pl.pallas_call(
        body, name=name, grid=(ni, nf),
        in_specs=[_row_spec(tm, D), tile, tile, pl.BlockSpec((None, tf, D), lambda i, j: (2, j, 0))] + [hbm] * ng,
        out_specs=[tile, tile, _row_spec(tm, D)] + [hbm] * ng,
        out_shape=[jax.ShapeDtypeStruct((T, F), BF)] * 2 + [jax.ShapeDtypeStruct((T, D), BF)]
        + _scattered_shapes(scatter),
        scratch_shapes=_scatter_sems(ng), compiler_params=_params("arbitrary", "arbitrary"),
    )(dh, a, b, wffn, *[piece[0] for piece in scatter])
    return outs[0], outs[1], outs[2], list(outs[3:])


def _two_dot_norm_bwd(x1, x2, w, w_specs, h, g, dh, name, scatter=(), x_specs=None):
    T, K = x1.shape
    D = h.shape[1]
    tm = _tile(T, TM_FFN_IN, 16)
    ni, ng = T // tm, len(scatter)
    x_specs = x_specs or (lambda tm: [pl.BlockSpec((tm, K), lambda i: (i, 0))] * 2)

    def body(*refs):
        x1_ref, x2_ref, w1_ref, w2_ref, h_ref, g_ref, dh_ref = refs[:7]
        srcs, (o_ref, dg_ref), outs = refs[7:7 + ng], refs[7 + ng:9 + ng], refs[9 + ng:9 + 2 * ng]
        i = pl.program_id(0)
        if ng:
            _carry(_scatter_plan(scatter, srcs, outs, *refs[9 + 2 * ng:]), i == 0, None, i == ni - 1)

        @pl.when(i == 0)
        def _():
            dg_ref[...] = jnp.zeros_like(dg_ref)

        dn = _dot(x1_ref[...], w1_ref[...]) + _dot(x2_ref[...], w2_ref[...])
        dx, dg = _rms_bwd_parts(h_ref[...], g_ref[...], dn)
        o_ref[...] = dh_ref[...] + dx
        dg_ref[...] += dg

    hbm = pl.BlockSpec(memory_space=pl.ANY)
    outs = pl.pallas_call(
        body, name=name, grid=(ni,),
        in_specs=x_specs(tm) + w_specs + [_row_spec(tm, D), _vec_spec(D), _row_spec(tm, D)] + [hbm] * ng,
        out_specs=[_row_spec(tm, D), _vec_spec(D)] + [hbm] * ng,
        out_shape=[jax.ShapeDtypeStruct((T, D), F32), jax.ShapeDtypeStruct((1, D), F32)] + _scattered_shapes(scatter),
        scratch_shapes=_scatter_sems(ng), compiler_params=_params("arbitrary"),
    )(x1, x2, w, w, h, g.reshape(1, D), dh, *[piece[0] for piece in scatter])
    return outs[0], outs[1], list(outs[2:])


def _ffn_bwd_in(da, db, wffn, h, g, dh, name, scatter=()):
    F, D = wffn.shape[1:]
    specs = [pl.BlockSpec((None, F, D), functools.partial(lambda k, i: (k, 0, 0), k), pipeline_mode=pl.Buffered(1))
             for k in (0, 1)]
    return _two_dot_norm_bwd(da, db, wffn, specs, h, g, dh, name, scatter)


def _lru_in_bwd(dz, win, h, g, dh, name):
    R, D = win.shape[0] // 2, win.shape[1]
    specs = [pl.BlockSpec((R, D), functools.partial(lambda k, i: (k, 0), k), pipeline_mode=pl.Buffered(1)) for k in (0, 1)]
    halves = lambda tm: [pl.BlockSpec((tm, R), functools.partial(lambda k, i: (i, k), k)) for k in (0, 1)]
    return _two_dot_norm_bwd(dz, dz, win, specs, h, g, dh, name, x_specs=halves)[:2]


def _ffn_bwd_w(da, db, s, n, dhb, name, scatter=()):
    T, F = da.shape
    D = n.shape[1]
    tf, tk = _tile(F, TF_FFN_WG, LANES), _tile(T, TK_FFN_WG, 16)
    nj, nk, ng = F // tf, T // tk, len(scatter)

    def body(*refs):
        da_ref, db_ref, s_ref, n_ref, dh_ref = refs[:5]
        srcs, o_ref, outs = refs[5:5 + ng], refs[5 + ng], refs[6 + ng:6 + 2 * ng]
        g_sc, u_sc, d_sc = refs[6 + 2 * ng:9 + 2 * ng]
        j, k = pl.program_id(0), pl.program_id(1)
        if ng:
            _carry(_scatter_plan(scatter, srcs, outs, *refs[9 + 2 * ng:]), jnp.logical_and(j == 0, k == 0), None,
                   jnp.logical_and(j == nj - 1, k == nk - 1))

        @pl.when(k == 0)
        def _():
            g_sc[...] = jnp.zeros_like(g_sc)
            u_sc[...] = jnp.zeros_like(u_sc)
            d_sc[...] = jnp.zeros_like(d_sc)

        nv = n_ref[...]
        g_sc[...] += _dot_tn(da_ref[...], nv)
        u_sc[...] += _dot_tn(db_ref[...], nv)
        d_sc[...] += _dot_tn(s_ref[...], dh_ref[...])

        @pl.when(k == nk - 1)
        def _():
            o_ref[0] = g_sc[...].astype(BF)
            o_ref[1] = u_sc[...].astype(BF)
            o_ref[2] = (0.5 * d_sc[...]).astype(BF)

    act = pl.BlockSpec((tk, tf), lambda j, k: (k, j))
    tok = pl.BlockSpec((tk, D), lambda j, k: (k, 0))
    hbm = pl.BlockSpec(memory_space=pl.ANY)
    outs = pl.pallas_call(
        body, name=name, grid=(nj, nk), in_specs=[act, act, act, tok, tok] + [hbm] * ng,
        out_specs=[pl.BlockSpec((3, tf, D), lambda j, k: (0, j, 0), pipeline_mode=pl.Buffered(1))] + [hbm] * ng,
        out_shape=[jax.ShapeDtypeStruct((3, F, D), BF)] + _scattered_shapes(scatter),
        scratch_shapes=[pltpu.VMEM((tf, D), F32)] * 3 + _scatter_sems(ng),
        compiler_params=_params("arbitrary", "arbitrary"),
    )(da, db, s, n, dhb, *[piece[0] for piece in scatter])
    return outs[0], list(outs[1:])


def _ple_bwd(dh, gate, pp, h, g, wg, name):
    T, D = dh.shape
    tm = _tile(T, TM_EW, 16)

    def body(dh_ref, gate_ref, pp_ref, h_ref, g_ref, wg_ref, o_ref, dz_ref, dp_ref, dg_ref):
        @pl.when(pl.program_id(0) == 0)
        def _():
            dg_ref[...] = jnp.zeros_like(dg_ref)

        d, gate = dh_ref[...], gate_ref[...].astype(F32)
        dz = (d * pp_ref[...].astype(F32) * gate * (1.0 - gate)).astype(BF)
        dx, dg = _rms_bwd_parts(h_ref[...], g_ref[...], _dot_nt(dz, wg_ref[...]))
        o_ref[...] = d + dx
        dz_ref[...] = dz
        dp_ref[...] = (d * gate).astype(BF)
        dg_ref[...] += dg

    row = _row_spec(tm, D)
    return pl.pallas_call(
        body, name=name, grid=(T // tm,), in_specs=[row, row, row, row, _vec_spec(D), _vec_spec(D, D)],
        out_specs=[row, row, row, _vec_spec(D)],
        out_shape=[jax.ShapeDtypeStruct((T, D), F32), jax.ShapeDtypeStruct((T, D), BF), jax.ShapeDtypeStruct((T, D), BF),
                   jax.ShapeDtypeStruct((1, D), F32)],
        compiler_params=_params("arbitrary"),
    )(dh, gate, pp, h, g.reshape(1, D), wg)


def _lru_in(h, g, win, name):
    T, D = h.shape
    R = win.shape[0] // 2
    tm = _tile(T, TM_EW, 16)

    def body(h_ref, g_ref, w_ref, n_ref, gb_ref, xb_ref):
        n = _rms_parts(h_ref[...], g_ref[...])[0].astype(BF)
        n_ref[...] = n
        z = _dot_nt(n, w_ref[...])
        gb_ref[...] = z[:, :R].astype(BF)
        xb_ref[...] = z[:, R:]

    return pl.pallas_call(
        body, name=name, grid=(T // tm,),
        in_specs=[_row_spec(tm, D), _vec_spec(D), pl.BlockSpec((2 * R, D), lambda i: (0, 0), pipeline_mode=pl.Buffered(1))],
        out_specs=[_row_spec(tm, D), _row_spec(tm, R), _row_spec(tm, R)],
        out_shape=[jax.ShapeDtypeStruct((T, D), BF), jax.ShapeDtypeStruct((T, R), BF), jax.ShapeDtypeStruct((T, R), F32)],
        compiler_params=_params("parallel"),
    )(h, g.reshape(1, D), win)


def _lru_fwd(gb, xb, conv_w, conv_b, wa, wx, b_a, b_x, a_param, wout, h, name):
    T, R = xb.shape
    D = h.shape[1]
    tb = _tile(T, TB_SEQ, HALO)
    per, ng = tb // HALO, tb // SUBLANES

    def body(g_ref, x_ref, halo_ref, cw_ref, cb_ref, wa_ref, wx_ref, ba_ref, bx_ref, ap_ref, wo_ref, h_ref,
             o_ref, xc_ref, r_ref, ig_ref, a_ref, hs_ref, y_ref, ext, carry, a_sc, b_sc):
        i = pl.program_id(0)

        @pl.when(i == 0)
        def _():
            carry[...] = jnp.zeros_like(carry)

        ext[pl.ds(0, HALO), :] = jnp.where(i > 0, halo_ref[...], 0.0)
        ext[pl.ds(HALO, tb), :] = x_ref[...]
        xc = cb_ref[...] + cw_ref[0:1, :] * ext[pl.ds(HALO - 3, tb), :]
        for k in range(1, CONV_WIDTH):
            xc = xc + cw_ref[k:k + 1, :] * ext[pl.ds(HALO - 3 + k, tb), :]
        xcb = xc.astype(BF)
        r = _sigmoid_pos(_dot(xcb, wa_ref[...]) + ba_ref[...])
        ig = _sigmoid(_dot(xcb, wx_ref[...]) + bx_ref[...])
        la = -LRU_C * r * _softplus_neg(ap_ref[...])
        av = jnp.exp(la)
        xc_ref[...] = xc
        r_ref[...] = r
        ig_ref[...] = ig
        a_ref[...] = av
        A = av.reshape(ng, SUBLANES, R)
        B = (jnp.sqrt(_one_minus_sq(la, av)) * (ig * xc)).reshape(ng, SUBLANES, R)
        sub = lax.broadcasted_iota(jnp.int32, (1, SUBLANES, R), 1)
        for k in (1, 2, 4):
            m = sub >= k
            a_n = jnp.where(m, pltpu.roll(A, k, 1), 1.0)
            b_n = jnp.where(m, pltpu.roll(B, k, 1), 0.0)
            B = A * b_n + B
            A = A * a_n
        a_sc[...] = A.reshape(tb, R)
        b_sc[...] = B.reshape(tb, R)

        def group(q, c):
            rows = pl.ds(pl.multiple_of(q * SUBLANES, SUBLANES), SUBLANES)
            hg = a_sc[rows, :] * c + b_sc[rows, :]
            hs_ref[rows, :] = hg
            return hg[SUBLANES - 1:SUBLANES, :]

        carry[...] = lax.fori_loop(0, ng, group, carry[...])
        y = (hs_ref[...] * _gelu_parts(g_ref[...].astype(F32))[0]).astype(BF)
        y_ref[...] = y
        o_ref[...] = h_ref[...] + _dot(y, wo_ref[...])

    once = lambda rows, cols: pl.BlockSpec((rows, cols), lambda i: (0, 0), pipeline_mode=pl.Buffered(1))
    gate = pl.BlockSpec((tb, R), lambda i: (i, 0))
    halo = pl.BlockSpec((HALO, R), lambda i: (jnp.maximum(i * per - 1, 0), 0))
    return pl.pallas_call(
        body, name=name, grid=(T // tb,),
        in_specs=[gate, gate, halo, _vec_spec(R, CONV_WIDTH), _vec_spec(R), once(R, R), once(R, R), _vec_spec(R),
                  _vec_spec(R), _vec_spec(R), once(R, D), _row_spec(tb, D)],
        out_specs=[_row_spec(tb, D)] + [gate] * 6,
        out_shape=[jax.ShapeDtypeStruct((T, D), F32)] + [jax.ShapeDtypeStruct((T, R), F32)] * 5
        + [jax.ShapeDtypeStruct((T, R), BF)],
        scratch_shapes=[pltpu.VMEM((HALO + tb, R), F32), pltpu.VMEM((1, R), F32), pltpu.VMEM((tb, R), F32),
                        pltpu.VMEM((tb, R), F32)],
        compiler_params=_params("arbitrary"),
    )(gb, xb, xb, conv_w, conv_b.reshape(1, R), wa, wx, b_a.reshape(1, R), b_x.reshape(1, R), a_param.reshape(1, R), wout, h)


def _lru_bwd(dh, hs, gb, xb, a, r, ig, xc, wa, wx, wout, conv_w, a_param, name):
    T, R = hs.shape
    D = dh.shape[1]
    tb = _tile(T, TB_SEQ, HALO)
    per, nt, ng = tb // HALO, T // tb, tb // SUBLANES

    def body(dh_ref, h_ref, hp_ref, g_ref, x_ref, xp_ref, a_ref, r_ref, ig_ref, xc_ref, wa_ref, wx_ref, wo_ref, cw_ref,
             ap_ref, dz_ref, dpa_ref, dpx_ref, dsp_ref, dba_ref, dbx_ref, dcb_ref, dcw_ref,
             hext, xext, dext, carry, later, a_sc, b_sc, d_sc, l_sc):
        i = pl.program_id(0)

        @pl.when(i == 0)
        def _():
            for ref in (dsp_ref, dba_ref, dbx_ref, dcb_ref, dcw_ref, carry, later):
                ref[...] = jnp.zeros_like(ref)

        dy = _dot_nt(dh_ref[...].astype(BF), wo_ref[...])
        gl, dgl = _gelu_parts(g_ref[...].astype(F32))
        hv, av = h_ref[...], a_ref[...]
        dhd = dy * gl
        dz_ref[:, pl.ds(0, R)] = (dy * hv * dgl).astype(BF)
        d_sc[...] = dhd
        A = av.reshape(ng, SUBLANES, R)
        B = A * dhd.reshape(ng, SUBLANES, R)
        sub = lax.broadcasted_iota(jnp.int32, (1, SUBLANES, R), 1)
        for k in (1, 2, 4):
            m = sub < SUBLANES - k
            a_n = jnp.where(m, pltpu.roll(A, SUBLANES - k, 1), 1.0)
            b_n = jnp.where(m, pltpu.roll(B, SUBLANES - k, 1), 0.0)
            B = A * b_n + B
            A = A * a_n
        a_sc[...] = A.reshape(tb, R)
        b_sc[...] = B.reshape(tb, R)
        sub8 = lax.broadcasted_iota(jnp.int32, (SUBLANES, R), 0)

        def group(q, c):
            rows = pl.ds(pl.multiple_of((ng - 1 - q) * SUBLANES, SUBLANES), SUBLANES)
            mu = a_sc[rows, :] * c + b_sc[rows, :]
            l_sc[rows, :] = d_sc[rows, :] + jnp.where(sub8 == SUBLANES - 1, c, pltpu.roll(mu, SUBLANES - 1, 0))
            return mu[0:1, :]

        carry[...] = lax.fori_loop(0, ng, group, carry[...])
        lam = l_sc[...]
        hext[pl.ds(0, HALO), :] = jnp.where(i < nt - 1, hp_ref[...], 0.0)
        hext[pl.ds(HALO, tb), :] = hv
        h_prev = hext[pl.ds(HALO - 1, tb), :]
        rv, igv, xcv = r_ref[...], ig_ref[...], xc_ref[...]
        sp = _softplus_neg(ap_ref[...])
        mult = jnp.sqrt(_one_minus_sq(-LRU_C * rv * sp, av))
        dla = lam * h_prev * av - lam * (igv * xcv) * (av * av) / mult
        du = lam * mult
        dpa = (dla * (-LRU_C) * sp) * rv * (1.0 - rv)
        dpx = (du * xcv) * igv * (1.0 - igv)
        dsp_ref[...] += jnp.sum(dla * (-LRU_C) * rv, axis=0, keepdims=True)
        dba_ref[...] += jnp.sum(dpa, axis=0, keepdims=True)
        dbx_ref[...] += jnp.sum(dpx, axis=0, keepdims=True)
        dpab, dpxb = dpa.astype(BF), dpx.astype(BF)
        dpa_ref[...] = dpab
        dpx_ref[...] = dpxb
        dxc = du * igv + _dot_nt(dpab, wa_ref[...]) + _dot_nt(dpxb, wx_ref[...])
        dext[pl.ds(0, tb), :] = dxc
        dext[pl.ds(tb, SUBLANES), :] = later[...]
        later[...] = dxc[0:SUBLANES, :]
        xext[pl.ds(0, HALO), :] = jnp.where(i < nt - 1, xp_ref[...], 0.0)
        xext[pl.ds(HALO, tb), :] = x_ref[...]
        dxb = cw_ref[CONV_WIDTH - 1:CONV_WIDTH, :] * dxc
        for k in range(CONV_WIDTH - 1):
            dxb = dxb + cw_ref[k:k + 1, :] * dext[pl.ds(CONV_WIDTH - 1 - k, tb), :]
        dz_ref[:, pl.ds(R, R)] = dxb.astype(BF)
        for k in range(CONV_WIDTH):
            dcw_ref[k:k + 1, :] += jnp.sum(dxc * xext[pl.ds(HALO - 3 + k, tb), :], axis=0, keepdims=True)
        dcb_ref[...] += jnp.sum(dxc, axis=0, keepdims=True)

        @pl.when(i == nt - 1)
        def _():
            dsp_ref[...] = dsp_ref[...] * (-_sigmoid(-ap_ref[...]))

    once = lambda rows, cols: pl.BlockSpec((rows, cols), lambda i: (0, 0), pipeline_mode=pl.Buffered(1))
    t0 = pl.BlockSpec((tb, R), lambda i: (nt - 1 - i, 0))
    prev = pl.BlockSpec((HALO, R), lambda i: (jnp.maximum((nt - 1 - i) * per - 1, 0), 0))
    return pl.pallas_call(
        body, name=name, grid=(nt,),
        in_specs=[pl.BlockSpec((tb, D), lambda i: (nt - 1 - i, 0)), t0, prev, t0, t0, prev, t0, t0, t0, t0,
                  once(R, R), once(R, R), once(R, D), _vec_spec(R, CONV_WIDTH), _vec_spec(R)],
        out_specs=[pl.BlockSpec((tb, 2 * R), lambda i: (nt - 1 - i, 0)), t0, t0] + [_vec_spec(R)] * 4
        + [_vec_spec(R, SUBLANES)],
        out_shape=[jax.ShapeDtypeStruct((T, 2 * R), BF)] + [jax.ShapeDtypeStruct((T, R), BF)] * 2
        + [jax.ShapeDtypeStruct((1, R), F32)] * 4
        + [jax.ShapeDtypeStruct((SUBLANES, R), F32)],
        scratch_shapes=[pltpu.VMEM((HALO + tb, R), F32), pltpu.VMEM((HALO + tb, R), F32),
                        pltpu.VMEM((tb + SUBLANES, R), F32), pltpu.VMEM((1, R), F32), pltpu.VMEM((SUBLANES, R), F32)]
        + [pltpu.VMEM((tb, R), F32)] * 4,
        compiler_params=_params("arbitrary"),
    )(dh, hs, hs, gb, xb, xb, a, r, ig, xc, wa, wx, wout, conv_w, a_param.reshape(1, R))


def _gate_spans(R):
    d = R // LRU_HEADS
    spans = [min((j * GATE_COLS // d) * d // LANES * LANES, R - GATE_SPAN) for j in range(R // GATE_COLS)]
    assert R % GATE_COLS == 0 and all(lo + GATE_SPAN >= (((j + 1) * GATE_COLS - 1) // d + 1) * d for j, lo in enumerate(spans))
    return spans


def _lru_gates_dw(xc, dpa, dpx, name):
    T, R = xc.shape
    tk = _tile(T, 1024, 16)
    spans = _gate_spans(R)
    nb = len(spans)

    def body(x_ref, a_ref, b_ref, o_ref):
        @pl.when(pl.program_id(0) == 0)
        def _():
            o_ref[...] = jnp.zeros_like(o_ref)

        for j, lo in enumerate(spans):
            xs = x_ref[:, pl.ds(lo, GATE_SPAN)].astype(BF)
            cols = pl.ds(j * GATE_COLS, GATE_COLS)
            o_ref[0, j] += _dot_tn(xs, a_ref[:, cols])
            o_ref[1, j] += _dot_tn(xs, b_ref[:, cols])

    row = _row_spec(tk, R)
    out = pl.pallas_call(
        body, name=name, grid=(T // tk,), in_specs=[row, row, row],
        out_specs=pl.BlockSpec((2, nb, GATE_SPAN, GATE_COLS), lambda i: (0, 0, 0, 0)),
        out_shape=jax.ShapeDtypeStruct((2, nb, GATE_SPAN, GATE_COLS), F32), compiler_params=_params("arbitrary"),
    )(xc, dpa, dpx)
    dense = jnp.zeros((2, R, R), F32)
    for j, lo in enumerate(spans):
        dense = dense.at[:, lo:lo + GATE_SPAN, j * GATE_COLS:(j + 1) * GATE_COLS].set(out[:, j])
    return dense[0], dense[1]


def _window_sums(e, n, back):
    out, s = [], e
    for k in (1, 2, 4, 8):
        s = s + pltpu.roll(s, k if back else n - k, 0)
        out.append(s)
    return out


def _pool_fwd(h, g, w, b, scale, name):
    T, D = h.shape
    G = len(POOL_WINDOWS)
    gd = D // G
    tb = _tile(T, TB_SEQ, HALO)
    per = tb // HALO

    def body(h_ref, hp_ref, g_ref, w_ref, b_ref, s_ref, o_ref, u_ref, yb_ref):
        i = pl.program_id(0)
        t = i * tb + lax.broadcasted_iota(jnp.int32, (tb, gd), 0) + 1
        hv = h_ref[...]
        xn = _rms_parts(hv, g_ref[...])[0]
        xp = jnp.where(i > 0, _rms_parts(hp_ref[...], g_ref[...])[0], 0.0)
        for k, win in enumerate(POOL_WINDOWS):
            cols = slice(k * gd, (k + 1) * gd)
            x = xn[:, cols]
            e = jnp.concatenate([xp[:, cols], x], axis=0)
            sw = _window_sums(e, HALO + tb, True)[k][HALO:, :]
            u = (sw / jnp.minimum(t, win).astype(F32) - x).astype(BF)
            yb = _dot(u, w_ref[k]) + b_ref[:, cols]
            u_ref[:, cols] = u
            yb_ref[:, cols] = yb
            o_ref[:, cols] = hv[:, cols] + yb * s_ref[:, cols]

    tile = _row_spec(tb, D)
    prev = pl.BlockSpec((HALO, D), lambda i: (jnp.maximum(i * per - 1, 0), 0))
    return pl.pallas_call(
        body, name=name, grid=(T // tb,),
        in_specs=[tile, prev, _vec_spec(D), pl.BlockSpec((G, gd, gd), lambda i: (0, 0, 0)), _vec_spec(D), _vec_spec(D)],
        out_specs=[tile, tile, tile],
        out_shape=[jax.ShapeDtypeStruct((T, D), F32), jax.ShapeDtypeStruct((T, D), BF), jax.ShapeDtypeStruct((T, D), F32)],
        compiler_params=_params("parallel"),
    )(h, h, g.reshape(1, D), w, b.reshape(1, D), scale.reshape(1, D))


def _pool_bwd(dm, u, yb, w, scale, h, g, name):
    T, D = dm.shape
    G = len(POOL_WINDOWS)
    gd = D // G
    tb = _tile(T, TB_SEQ, HALO)
    nt = T // tb

    def body(d_ref, u_ref, yb_ref, w_ref, s_ref, h_ref, g_ref, o_ref, dw_ref, db_ref, ds_ref, dg_ref, later):
        i = pl.program_id(0)

        @pl.when(i == 0)
        def _():
            for ref in (dw_ref, db_ref, ds_ref, dg_ref, later):
                ref[...] = jnp.zeros_like(ref)

        d, sc = d_ref[...], s_ref[...]
        ds_ref[...] += jnp.sum(d * yb_ref[...], axis=0, keepdims=True)
        db_ref[...] += jnp.sum(d * sc, axis=0, keepdims=True)
        t = (nt - 1 - i) * tb + lax.broadcasted_iota(jnp.int32, (tb, gd), 0) + 1
        parts = []
        for k, win in enumerate(POOL_WINDOWS):
            cols = slice(k * gd, (k + 1) * gd)
            dy = (d[:, cols] * sc[:, cols]).astype(BF)
            du = _dot_nt(dy, w_ref[k])
            dw_ref[k] += _dot_tn(u_ref[:, cols], dy)
            v = du / jnp.minimum(t, win).astype(F32)
            e = jnp.concatenate([v, later[:, cols]], axis=0)
            later[:, cols] = v[0:HALO, :]
            parts.append(_window_sums(e, tb + HALO, False)[k][:tb, :] - du)
        dx, dg = _rms_bwd_parts(h_ref[...], g_ref[...], jnp.concatenate(parts, axis=1))
        o_ref[...] = d + dx
        dg_ref[...] += dg

    tile = pl.BlockSpec((tb, D), lambda i: (nt - 1 - i, 0))
    whole = pl.BlockSpec((G, gd, gd), lambda i: (0, 0, 0))
    return pl.pallas_call(
        body, name=name, grid=(nt,), in_specs=[tile, tile, tile, whole, _vec_spec(D), tile, _vec_spec(D)],
        out_specs=[tile, whole, _vec_spec(D), _vec_spec(D), _vec_spec(D)],
        out_shape=[jax.ShapeDtypeStruct((T, D), F32), jax.ShapeDtypeStruct((G, gd, gd), F32)]
        + [jax.ShapeDtypeStruct((1, D), F32)] * 3,
        scratch_shapes=[pltpu.VMEM((HALO, D), F32)], compiler_params=_params("arbitrary"),
    )(dm, u, yb, w, scale.reshape(1, D), h, g.reshape(1, D))


def _adamw(w, g, m, v, name):
    shape = w.shape
    cols = shape[-1] if w.ndim > 1 else shape[0]
    rows = w.size // cols
    tr = _tile(rows, TR_ADAM, SUBLANES)
    c1, c2 = 1.0 / (1.0 - ADAM_B1 ** ADAM_STEP), 1.0 / (1.0 - ADAM_B2 ** ADAM_STEP)

    def body(w_ref, g_ref, m_ref, v_ref, d_ref, mo_ref, vo_ref):
        gv = g_ref[...]
        mn = ADAM_B1 * m_ref[...] + (1.0 - ADAM_B1) * gv
        vn = ADAM_B2 * v_ref[...] + (1.0 - ADAM_B2) * (gv * gv)
        d_ref[...] = -ADAM_LR * ((mn * c1) / (jnp.sqrt(vn * c2) + ADAM_EPS) + ADAM_WD * w_ref[...])
        mo_ref[...] = mn
        vo_ref[...] = vn

    spec = _row_spec(tr, cols)
    outs = pl.pallas_call(
        body, name=name, grid=(rows // tr,), in_specs=[spec] * 4, out_specs=[spec] * 3,
        out_shape=[jax.ShapeDtypeStruct((rows, cols), F32)] * 3, compiler_params=_params("parallel"),
    )(*[t.reshape(rows, cols) for t in (w, g, m, v)])
    return [o.reshape(shape) for o in outs]


def _sum_devices(parts, name):
    n, rows, cols = parts.shape
    tr = _tile(rows, 1024, SUBLANES)

    def body(p_ref, o_ref):
        acc = p_ref[0].astype(F32)
        for k in range(1, n):
            acc = acc + p_ref[k].astype(F32)
        o_ref[...] = acc

    return pl.pallas_call(
        body, name=name, grid=(rows // tr,), in_specs=[pl.BlockSpec((n, tr, cols), lambda i: (0, i, 0))],
        out_specs=_row_spec(tr, cols), out_shape=jax.ShapeDtypeStruct((rows, cols), F32),
        compiler_params=_params("parallel"),
    )(parts)


def _position():
    return lax.axis_index("x"), lax.axis_index("y"), lax.axis_index("c")


def _gathered_shapes(blocks):
    return [jax.ShapeDtypeStruct((b.shape[0], N_DEV * b.shape[1], b.shape[2]), b.dtype) for b in blocks]


def _gather_sems(ng):
    return [pltpu.SemaphoreType.DMA((ng, 7)), pltpu.SemaphoreType.DMA((ng, 7)), pltpu.SemaphoreType.DMA((ng,))] if ng else []


def _gather_plan(blocks, srcs, outs, send_sems, recv_sems, local_sems):
    ng = len(blocks)
    x, y, c = _position()
    me, sibling = (x, y, c), (x, y, 1 - c)
    chips = [(1 - x, y), (x, 1 - y), (1 - x, 1 - y)]

    def rows(g, px, py, pc):
        r = blocks[g].shape[1]
        return outs[g].at[:, pl.ds((4 * px + 2 * py + pc) * r, r), :]

    def copy(g, k, block, to, src=None):
        return pltpu.make_async_remote_copy(
            src_ref=rows(g, *block) if src is None else src, dst_ref=rows(g, *block),
            send_sem=send_sems.at[g, k], recv_sem=recv_sems.at[g, k], device_id=to, device_id_type=MESH)

    def mine(g):
        return pltpu.make_async_copy(srcs[g], rows(g, *me), local_sems.at[g])

    def first(g):
        return [copy(g, 0, me, sibling, src=srcs[g])] + [copy(g, 1 + j, me, (*chip, c), src=srcs[g])
                                                         for j, chip in enumerate(chips)]

    def passed(g):
        return [copy(g, 4 + j, (*chip, c), sibling) for j, chip in enumerate(chips)]

    def start():
        for g in range(ng):
            mine(g).start()
            for cp in first(g):
                cp.start()

    def forward():
        for j, chip in enumerate(chips):
            for g in range(ng):
                copy(g, 1 + j, (*chip, c), me).wait_recv()
                copy(g, 4 + j, (*chip, c), sibling).start()

    def finish():
        for g in range(ng):
            copy(g, 0, sibling, me).wait_recv()
            for j, chip in enumerate(chips):
                copy(g, 4 + j, (*chip, 1 - c), me).wait_recv()
            for cp in first(g) + passed(g):
                cp.wait_send()
            mine(g).wait()

    return start, forward, finish


def _all_gather(blocks, name):
    ng = len(blocks)

    def body(*refs):
        start, forward, finish = _gather_plan(blocks, refs[:ng], refs[ng:2 * ng], *refs[2 * ng:])
        start()
        forward()
        finish()

    hbm = pl.BlockSpec(memory_space=pl.ANY)
    return pl.pallas_call(
        body, name=name, in_specs=[hbm] * ng, out_specs=[hbm] * ng, out_shape=_gathered_shapes(blocks),
        scratch_shapes=_gather_sems(ng),
    )(*blocks)


FLIPS = ((0, 0, 1), (1, 0, 0), (0, 1, 0), (1, 1, 0), (1, 0, 1), (0, 1, 1), (1, 1, 1))


def _piece_rows(piece):
    arr, m = piece
    return arr.shape[0] if m is None else 1


def _scattered_shapes(pieces):
    return [jax.ShapeDtypeStruct((N_DEV, _piece_rows(p), p[0].shape[1] // N_DEV, p[0].shape[2]), p[0].dtype)
            for p in pieces]


def _scatter_sems(ng):
    n = len(FLIPS)
    return [pltpu.SemaphoreType.DMA((ng, n)), pltpu.SemaphoreType.DMA((ng, n)), pltpu.SemaphoreType.DMA((ng,))] if ng else []


def _scatter_plan(pieces, srcs, outs, send_sems, recv_sems, local_sems):
    x, y, c = _position()

    def block(g, tx, ty, tc):
        arr, m = pieces[g]
        r = arr.shape[1] // N_DEV
        lead = slice(None) if m is None else pl.ds(m, 1)
        return srcs[g].at[lead, pl.ds((4 * tx + 2 * ty + tc) * r, r), :]

    def copies(g):
        out = []
        for k, (fx, fy, fc) in enumerate(FLIPS):
            tx, ty, tc = (1 - x if fx else x), (1 - y if fy else y), (1 - c if fc else c)
            out.append(pltpu.make_async_remote_copy(
                src_ref=block(g, tx, ty, tc), dst_ref=outs[g].at[k], send_sem=send_sems.at[g, k],
                recv_sem=recv_sems.at[g, k], device_id=(tx, ty, tc), device_id_type=MESH))
        return out

    def mine(g):
        return pltpu.make_async_copy(block(g, x, y, c), outs[g].at[len(FLIPS)], local_sems.at[g])

    def start():
        for g in range(len(pieces)):
            mine(g).start()
            for cp in copies(g):
                cp.start()

    def finish():
        for g in range(len(pieces)):
            for cp in copies(g):
                cp.wait()
            mine(g).wait()

    return start, finish


def _scatter_and_gather(pieces, blocks, name):
    n_p, n_b = len(pieces), len(blocks)

    def body(*refs):
        ins, outs, sems = refs[:n_p + n_b], refs[n_p + n_b:2 * (n_p + n_b)], refs[2 * (n_p + n_b):]
        s_start, s_finish = _scatter_plan(pieces, ins[:n_p], outs[:n_p], *sems[:3])
        g_start, g_forward, g_finish = _gather_plan(blocks, ins[n_p:], outs[n_p:], *sems[3:])
        s_start()
        g_start()
        g_forward()
        g_finish()
        s_finish()

    hbm = pl.BlockSpec(memory_space=pl.ANY)
    outs = pl.pallas_call(
        body, name=name, in_specs=[hbm] * (n_p + n_b), out_specs=[hbm] * (n_p + n_b),
        out_shape=_scattered_shapes(pieces) + _gathered_shapes(blocks),
        scratch_shapes=_scatter_sems(n_p) + _gather_sems(n_b),
    )(*[p[0] for p in pieces], *blocks)
    return list(outs[:n_p]), list(outs[n_p:])


def _scatter_sum(recvs, name):
    _, n, r, c = recvs[0].shape
    tr = _tile(r, TR_SUM, 16)

    def body(*refs):
        o_ref = refs[-1]
        for p, r_ref in enumerate(refs[:-1]):
            acc = r_ref[len(FLIPS)].astype(F32)
            for k in range(len(FLIPS)):
                acc = acc + r_ref[k].astype(F32)
            o_ref[p] = acc

    return pl.pallas_call(
        body, name=name, grid=(n, r // tr),
        in_specs=[pl.BlockSpec((N_DEV, None, tr, c), lambda i, j: (0, i, j, 0))] * len(recvs),
        out_specs=pl.BlockSpec((len(recvs), None, tr, c), lambda i, j: (0, i, j, 0)),
        out_shape=jax.ShapeDtypeStruct((len(recvs), n, r, c), F32), compiler_params=_params("parallel", "parallel"),
    )(*recvs).reshape(len(recvs) * n, r, c)


def _block_diag(w):
    H, d, _ = w.shape
    return (jnp.eye(H, dtype=w.dtype)[:, None, :, None] * w[:, :, None, :]).reshape(H * d, H * d)


def _diag_blocks(dense, H):
    d = dense.shape[0] // H
    return jnp.stack([dense[i * d:(i + 1) * d, i * d:(i + 1) * d] for i in range(H)])


def _local_step(x, p, tgt, W, blocks=None):
    dist = blocks is not None
    L = p.shape[0]
    W = dict(W)

    def gathering(keys):
        return [k for k in keys if k not in W] if dist else []

    def ffn_fwd(h, g, i, f, during_act, during_out, ple=None):
        w = W[("ffn", i, f)]
        keys = gathering(during_act)
        a, b, s, n, got = _ffn_fwd_act(h, g, w, f"ffn{f}_fwd_act_{i}", gather=[blocks[k] for k in keys])
        W.update(zip(keys, got))
        keys = gathering(during_out)
        if ple is not None:
            ple = (W["ple_norm"][i], W[("ple_gate", i)][0], W[("ple_proj", i)][0], ple)
        h, emb, got = _ffn_fwd_out(s, w, h, f"ffn{f}_fwd_out_{i}", gather=[blocks[k] for k in keys], ple=ple)
        W.update(zip(keys, got))
        return (a, b, s, n), h, emb

    saved = []
    h = x
    for i in range(L):
        j = i // 2
        lru = i % 2 == 0
        s = {"h0": h}
        mixer = [("lru_in", j), ("lru_out", j)] if lru else [("pool_w", j)]
        s["ffn1"], h, _ = ffn_fwd(h, W["ffn1_norm"][i], i, 1, [("ffn", i, 2)], mixer)
        s["h1"] = h
        if lru:
            hn, gb, xb = _lru_in(h, W["mix_norm"][i], W[("lru_in", j)][0], f"lru_in_{i}")
            wa, wx = _block_diag(W["lru_w_a"][j]).astype(BF), _block_diag(W["lru_w_x"][j]).astype(BF)
            h, xc, r, ig, a, hs, y = _lru_fwd(gb, xb, W["lru_conv_w"][j], W["lru_conv_b"][j], wa, wx, W["lru_b_a"][j],
                                              W["lru_b_x"][j], W["lru_a_param"][j], W[("lru_out", j)][0], h, f"lru_fwd_{i}")
            s.update(hn=hn, gb=gb, xb=xb, wa=wa, wx=wx, xc=xc, r=r, ig=ig, a=a, hs=hs, y=y)
        else:
            h, s["u"], s["yb"] = _pool_fwd(h, W["mix_norm"][i], W[("pool_w", j)], W["pool_b"][j], W["pool_scale"][j],
                                           f"pool_fwd_{i}")
        s["h2"] = h
        nxt = [("ffn", i + 1, 1)] if i + 1 < L else []
        s["ffn2"], s["h3"], (h, s["n4"], s["gate"], s["pp"]) = ffn_fwd(
            h, W["ffn2_norm"][i], i, 2, [("ple_gate", i), ("ple_proj", i)] + nxt, [], ple=p[i])
        saved.append(s)

    loss, dh, d_final = _loss_head(h, W["final_norm"], tgt)

    big, recv = {}, {}
    n_lru, n_pool = L // 2 + L % 2, L // 2
    small = {k: [None] * L for k in ("ffn1_norm", "mix_norm", "ffn2_norm", "ple_norm")}
    for k in ("lru_conv_w", "lru_conv_b", "lru_w_a", "lru_b_a", "lru_w_x", "lru_b_x", "lru_a_param"):
        small[k] = [None] * n_lru
    for k in ("pool_b", "pool_scale"):
        small[k] = [None] * n_pool

    def scattering(pieces):
        return [(k, m) for k, m in pieces if k in big] if dist else []

    def ffn_bwd(dh, h_in, g, acts, i, f, during):
        key, w = ("ffn", i, f), W[("ffn", i, f)]
        a, b, sv, n = acts
        out = [scattering(d) for d in during]
        sent = [[(big[k], m) for k, m in o] for o in out]
        da, db, dhb, got0 = _ffn_bwd_act(dh, a, b, w, f"ffn{f}_bwd_act_{i}", scatter=sent[0])
        big[key], got1 = _ffn_bwd_w(da, db, sv, n, dhb, f"ffn{f}_dw_{i}", scatter=sent[1])
        out.append(scattering([(key, 0)] + ([(key, 1)] if (i, f) == (0, 1) else [])))
        dh, dg, got2 = _ffn_bwd_in(da, db, w, h_in, g, dh, f"ffn{f}_bwd_in_{i}", scatter=[(big[k], m) for k, m in out[2]])
        for o, got in zip(out, (got0, got1, got2)):
            recv.update(zip(o, got))
        return dh, dg

    for i in reversed(range(L)):
        j = i // 2
        lru = i % 2 == 0
        s = saved[i]
        dh, dz, dpp, dg = _ple_bwd(dh, s["gate"], s["pp"], s["h3"], W["ple_norm"][i], W[("ple_gate", i)][0],
                                   f"ple_bwd_{i}")
        big[("ple_gate", i)] = _mm_tn(s["n4"], dz, f"ple_gate_dw_{i}")[None]
        big[("ple_proj", i)] = _mm_tn(dpp, p[i], f"ple_proj_dw_{i}")[None]
        small["ple_norm"][i] = dg[0]
        above = ("ffn", i + 1, 1)
        dh, dg = ffn_bwd(dh, s["h2"], W["ffn2_norm"][i], s["ffn2"], i, 2, [
            [(above, 1)], [(above, 2), (("ple_gate", i), None), (("ple_proj", i), None)]])
        small["ffn2_norm"][i] = dg[0]
        if lru:
            big[("lru_out", j)] = _mm_tn(s["y"], dh, f"lru_out_dw_{i}")[None]
            dz, dpa, dpx, dsp, dba, dbx, dcb, dcw = _lru_bwd(
                dh, s["hs"], s["gb"], s["xb"], s["a"], s["r"], s["ig"], s["xc"], s["wa"], s["wx"], W[("lru_out", j)][0],
                W["lru_conv_w"][j], W["lru_a_param"][j], f"lru_bwd_{i}")
            small["lru_a_param"][j], small["lru_b_a"][j], small["lru_b_x"][j] = dsp[0], dba[0], dbx[0]
            dwa, dwx = _lru_gates_dw(s["xc"], dpa, dpx, f"lru_gates_dw_{i}")
            small["lru_w_a"][j], small["lru_w_x"][j] = _diag_blocks(dwa, LRU_HEADS), _diag_blocks(dwx, LRU_HEADS)
            small["lru_conv_w"][j], small["lru_conv_b"][j] = dcw[:CONV_WIDTH], dcb[0]
            big[("lru_in", j)] = _mm_tn(dz, s["hn"], f"lru_in_dw_{i}")[None]
            dh, dg = _lru_in_bwd(dz, W[("lru_in", j)][0], s["h1"], W["mix_norm"][i], dh, f"lru_in_bwd_{i}")
            mixer = [("lru_in", j), ("lru_out", j)]
        else:
            dh, dw, dbp, dsc, dg = _pool_bwd(dh, s["u"], s["yb"], W[("pool_w", j)], W["pool_scale"][j], s["h1"],
                                             W["mix_norm"][i], f"pool_bwd_{i}")
            big[("pool_w", j)] = dw.astype(BF)
            small["pool_b"][j], small["pool_scale"][j] = dbp[0], dsc[0]
            mixer = [("pool_w", j)]
        small["mix_norm"][i] = dg[0]
        second = ("ffn", i, 2)
        dh, dg = ffn_bwd(dh, s["h0"], W["ffn1_norm"][i], s["ffn1"], i, 1, [
            [(second, 1)], [(second, 2)] + [(k, None) for k in mixer]])
        small["ffn1_norm"][i] = dg[0]

    small = {k: jnp.stack(v) for k, v in small.items()}
    small["final_norm"] = d_final[0]
    return loss, dh, big, recv, small


SMALL_SHARDED = ("pool_b", "pool_scale", "lru_conv_w")
SMALL = ("ffn1_norm", "mix_norm", "ffn2_norm", "ple_norm", "final_norm", "lru_conv_b", "lru_w_a", "lru_b_a",
         "lru_w_x", "lru_b_x", "lru_a_param", "pool_b", "pool_scale", "lru_conv_w")


def _pack_big(w):
    t = lambda a: jnp.swapaxes(a, -1, -2)
    out = {}
    for i in range(w["ffn1_norm"].shape[0]):
        for f in (1, 2):
            out[("ffn", i, f)] = jnp.stack([t(w[f"ffn{f}_w_gate"][i]), t(w[f"ffn{f}_w_up"][i]), w[f"ffn{f}_w_down"][i]])
        out[("ple_gate", i)], out[("ple_proj", i)] = w["ple_w_gate"][i][None], t(w["ple_w_proj"][i])[None]
    for j in range(w["lru_w_in"].shape[0]):
        out[("lru_in", j)], out[("lru_out", j)] = t(w["lru_w_in"][j])[None], w["lru_w_out"][j][None]
    for j in range(w["pool_w"].shape[0]):
        out[("pool_w", j)] = w["pool_w"][j]
    return out


def _unpack_big(b, L):
    t = lambda a: jnp.swapaxes(a, -1, -2)
    n_lru, n_pool = L // 2 + L % 2, L // 2
    out = {"lru_w_in": jnp.stack([t(b[("lru_in", j)][0]) for j in range(n_lru)]),
           "lru_w_out": jnp.stack([b[("lru_out", j)][0] for j in range(n_lru)]),
           "pool_w": jnp.stack([b[("pool_w", j)] for j in range(n_pool)]),
           "ple_w_gate": jnp.stack([b[("ple_gate", i)][0] for i in range(L)]),
           "ple_w_proj": jnp.stack([t(b[("ple_proj", i)][0]) for i in range(L)])}
    for f in (1, 2):
        out[f"ffn{f}_w_gate"] = jnp.stack([t(b[("ffn", i, f)][0]) for i in range(L)])
        out[f"ffn{f}_w_up"] = jnp.stack([t(b[("ffn", i, f)][1]) for i in range(L)])
        out[f"ffn{f}_w_down"] = jnp.stack([b[("ffn", i, f)][2] for i in range(L)])
    return out


def _flatten(parts, names, rows_of=LANES):
    flat = jnp.concatenate([parts[k].reshape(-1) for k in names])
    pad = (-flat.size) % (16 * rows_of)
    return jnp.pad(flat, (0, pad)).reshape(1, -1, rows_of)


def _unflatten(flat, like, names):
    out, o = {}, 0
    flat = flat.reshape(-1)
    for k in names:
        n = like[k].size
        out[k] = flat[o:o + n].reshape(like[k].shape)
        o += n
    return out


def kernel(x, p, ffn1_norm, ffn1_w_gate, ffn1_w_up, ffn1_w_down, mix_norm, lru_w_in, lru_conv_w, lru_conv_b, lru_w_a, lru_b_a, lru_w_x, lru_b_x, lru_a_param, lru_w_out, pool_w, pool_b, pool_scale, ffn2_norm, ffn2_w_gate, ffn2_w_up, ffn2_w_down, ple_norm, ple_w_gate, ple_w_proj, final_norm, loss_target, m_ffn1_norm, m_ffn1_w_gate, m_ffn1_w_up, m_ffn1_w_down, m_mix_norm, m_lru_w_in, m_lru_conv_w, m_lru_conv_b, m_lru_w_a, m_lru_b_a, m_lru_w_x, m_lru_b_x, m_lru_a_param, m_lru_w_out, m_pool_w, m_pool_b, m_pool_scale, m_ffn2_norm, m_ffn2_w_gate, m_ffn2_w_up, m_ffn2_w_down, m_ple_norm, m_ple_w_gate, m_ple_w_proj, m_final_norm, v_ffn1_norm, v_ffn1_w_gate, v_ffn1_w_up, v_ffn1_w_down, v_mix_norm, v_lru_w_in, v_lru_conv_w, v_lru_conv_b, v_lru_w_a, v_lru_b_a, v_lru_w_x, v_lru_b_x, v_lru_a_param, v_lru_w_out, v_pool_w, v_pool_b, v_pool_scale, v_ffn2_norm, v_ffn2_w_gate, v_ffn2_w_up, v_ffn2_w_down, v_ple_norm, v_ple_w_gate, v_ple_w_proj, v_final_norm):
    names = ["ffn1_norm", "ffn1_w_gate", "ffn1_w_up", "ffn1_w_down", "mix_norm", "lru_w_in", "lru_conv_w", "lru_conv_b",
             "lru_w_a", "lru_b_a", "lru_w_x", "lru_b_x", "lru_a_param", "lru_w_out", "pool_w", "pool_b", "pool_scale",
             "ffn2_norm", "ffn2_w_gate", "ffn2_w_up", "ffn2_w_down", "ple_norm", "ple_w_gate", "ple_w_proj", "final_norm"]
    w = dict(zip(names, [ffn1_norm, ffn1_w_gate, ffn1_w_up, ffn1_w_down, mix_norm, lru_w_in, lru_conv_w, lru_conv_b, lru_w_a, lru_b_a, lru_w_x, lru_b_x, lru_a_param, lru_w_out, pool_w, pool_b, pool_scale, ffn2_norm, ffn2_w_gate, ffn2_w_up, ffn2_w_down, ple_norm, ple_w_gate, ple_w_proj, final_norm]))
    m = dict(zip(names, [m_ffn1_norm, m_ffn1_w_gate, m_ffn1_w_up, m_ffn1_w_down, m_mix_norm, m_lru_w_in, m_lru_conv_w, m_lru_conv_b, m_lru_w_a, m_lru_b_a, m_lru_w_x, m_lru_b_x, m_lru_a_param, m_lru_w_out, m_pool_w, m_pool_b, m_pool_scale, m_ffn2_norm, m_ffn2_w_gate, m_ffn2_w_up, m_ffn2_w_down, m_ple_norm, m_ple_w_gate, m_ple_w_proj, m_final_norm]))
    v = dict(zip(names, [v_ffn1_norm, v_ffn1_w_gate, v_ffn1_w_up, v_ffn1_w_down, v_mix_norm, v_lru_w_in, v_lru_conv_w, v_lru_conv_b, v_lru_w_a, v_lru_b_a, v_lru_w_x, v_lru_b_x, v_lru_a_param, v_lru_w_out, v_pool_w, v_pool_b, v_pool_scale, v_ffn2_norm, v_ffn2_w_gate, v_ffn2_w_up, v_ffn2_w_down, v_ple_norm, v_ple_w_gate, v_ple_w_proj, v_final_norm]))
    L = p.shape[0]
    px, py, pc = _position()
    me = 4 * px + 2 * py + pc

    blocks = {k: b.astype(BF) for k, b in _pack_big(w).items()}
    first = ("ffn", 0, 1)
    got, small_blocks = _all_gather([blocks[first], _flatten(w, SMALL_SHARDED)], "gather_first")
    W = {first: got}
    per_dev = small_blocks.reshape(N_DEV, -1)
    shards = [_unflatten(per_dev[k], w, SMALL_SHARDED) for k in range(N_DEV)]
    for k in SMALL:
        W[k] = jnp.concatenate([s[k] for s in shards], axis=-1) if k in SMALL_SHARDED else w[k]

    loss, dx, big, recv, small = _local_step(x[0], p[:, 0], loss_target[0], W, blocks)

    last = [(k, m) for k in big if (k, None) not in recv for m in range(big[k].shape[0]) if (k, m) not in recv]
    got, (parts,) = _scatter_and_gather([(big[k], m) for k, m in last], [_flatten(small, SMALL).astype(BF)],
                                        "scatter_last_gather_small")
    recv.update(zip(last, got))

    def total(k):
        tag = "sum_" + "_".join(map(str, k))
        if (k, None) in recv:
            return _scatter_sum([recv[(k, None)]], tag)
        return _scatter_sum([recv[(k, m)] for m in range(big[k].shape[0])], tag)

    grads = _unpack_big({k: total(k) for k in big}, L)
    total_small = _sum_devices(parts.reshape(N_DEV, -1, LANES), "sum_small_grads")
    full = _unflatten(total_small, {k: W[k] for k in SMALL}, SMALL)
    for k in SMALL:
        if k in SMALL_SHARDED:
            n = w[k].shape[-1]
            grads[k] = lax.dynamic_slice_in_dim(full[k], me * n, n, axis=-1)
        else:
            grads[k] = full[k]

    delta, new_m, new_v = {}, {}, {}
    for k in names:
        delta[k], new_m[k], new_v[k] = _adamw(w[k], grads[k], m[k], v[k], f"adamw_{k}")
    total_loss = lax.psum(loss[0, 0], ("x", "y", "c"))
    return (total_loss, dx[None], *[grads[k] for k in names], *[delta[k] for k in names],
            *[new_m[k] for k in names], *[new_v[k] for k in names])
```

```python
import functools

import jax
import jax.numpy as jnp
from jax import lax
from jax.experimental import pallas as pl
from jax.experimental.pallas import tpu as pltpu

F32 = jnp.float32
BF = jnp.bfloat16
MESH = pl.DeviceIdType.MESH

RMS_EPS = 1e-6
LRU_C = 8.0
LRU_HEADS = 16
CONV_WIDTH = 4
POOL_WINDOWS = (2, 4, 8, 16)
ADAM_LR, ADAM_B1, ADAM_B2, ADAM_EPS, ADAM_WD, ADAM_STEP = 0.001, 0.9, 0.999, 1e-08, 0.01, 10

N_DEV = 8
LANES = 128
SUBLANES = 8
GATE_COLS, GATE_SPAN = 256, 512
HALO = 16
VMEM_LIMIT = 56 * 1024 * 1024

TM_FFN_ACT = 2048
TM_FFN_IN = 512
TF_FFN = 256
TF_FFN_WG = 1408
TK_FFN_WG = 512
TB_SEQ = 256
TM_EW = 512
TM_MM, TN_MM, TK_MM = 1280, 1024, 1024
TR_SUM = 176
TR_ADAM = 1024


def _tile(n, pref, align):
    if n <= pref:
        return n
    t = (pref // align) * align
    while t >= align:
        if n % t == 0:
            return t
        t -= align
    raise ValueError(f"no tile for {n} (pref {pref}, align {align})")


def _params(*sem):
    return pltpu.CompilerParams(dimension_semantics=sem, vmem_limit_bytes=VMEM_LIMIT)


def _dot(a, b):
    return lax.dot_general(a, b, (((1,), (0,)), ((), ())), preferred_element_type=F32)


def _dot_nt(a, b):
    return lax.dot_general(a, b, (((1,), (1,)), ((), ())), preferred_element_type=F32)


def _dot_tn(a, b):
    return lax.dot_general(a, b, (((0,), (0,)), ((), ())), preferred_element_type=F32)


def _sigmoid(x):
    return 0.5 + 0.5 * jnp.tanh(0.5 * x)


def _sigmoid_pos(x):
    return 1.0 / (1.0 + jnp.exp(-x))


def _gelu_parts(x):
    k0, k1 = 0.7978845608028654, 0.044715
    t = jnp.tanh(k0 * (x + k1 * x * x * x))
    g = 0.5 * x * (1.0 + t)
    dg = 0.5 * (1.0 + t) + 0.5 * x * (1.0 - t * t) * k0 * (1.0 + 3.0 * k1 * x * x)
    return g, dg


def _one_minus_sq(la, a):
    return jnp.tanh(-la) * (1.0 + a * a)


def _softplus_neg(l):
    u = jnp.exp(-jnp.abs(l))
    w = 1.0 + u
    log1p = jnp.where(w == 1.0, u, jnp.log(w) * (u / jnp.where(w == 1.0, 1.0, w - 1.0)))
    return jnp.maximum(-l, 0.0) + log1p


def _rms_parts(x, g):
    r = lax.rsqrt(jnp.mean(x * x, axis=-1, keepdims=True) + RMS_EPS)
    nhat = x * r
    return nhat * g, nhat, r


def _rms_bwd_parts(x, g, dn):
    _, nhat, r = _rms_parts(x, g)
    u = dn * g
    dx = r * (u - nhat * jnp.mean(u * nhat, axis=-1, keepdims=True))
    return dx, jnp.sum(dn * nhat, axis=0, keepdims=True)


def _row_spec(tm, d, single=False):
    if single:
        return pl.BlockSpec((tm, d), lambda i, *_: (i, 0), pipeline_mode=pl.Buffered(1))
    return pl.BlockSpec((tm, d), lambda i, *_: (i, 0))


def _vec_spec(d, rows=1):
    return pl.BlockSpec((rows, d), lambda *_: (0, 0))


def _mm_tn(x, w, name):
    (K, M), (_, N) = x.shape, w.shape
    tm, tn, tk = _tile(M, TM_MM, LANES), _tile(N, TN_MM, LANES), _tile(K, TK_MM, 16)
    nk = K // tk

    def body(x_ref, w_ref, o_ref, acc):
        k = pl.program_id(2)

        @pl.when(k == 0)
        def _():
            acc[...] = jnp.zeros_like(acc)

        acc[...] += _dot_tn(x_ref[...].astype(BF), w_ref[...].astype(BF))

        @pl.when(k == nk - 1)
        def _():
            o_ref[...] = acc[...].astype(BF)

    return pl.pallas_call(
        body, name=name, grid=(M // tm, N // tn, nk),
        in_specs=[pl.BlockSpec((tk, tm), lambda i, j, k: (k, i)), pl.BlockSpec((tk, tn), lambda i, j, k: (k, j))],
        out_specs=pl.BlockSpec((tm, tn), lambda i, j, k: (i, j)),
        out_shape=jax.ShapeDtypeStruct((M, N), BF),
        scratch_shapes=[pltpu.VMEM((tm, tn), F32)],
        compiler_params=_params("parallel", "parallel", "arbitrary"),
    )(x, w)


def _loss_head(h, g, tgt):
    T, D = h.shape
    tm = _tile(T, TM_EW, 16)

    def body(h_ref, g_ref, t_ref, loss_ref, dh_ref, dg_ref):
        @pl.when(pl.program_id(0) == 0)
        def _():
            dg_ref[...] = jnp.zeros_like(dg_ref)
            loss_ref[...] = jnp.zeros_like(loss_ref)

        x, gg = h_ref[...], g_ref[...]
        y = _rms_parts(x, gg)[0]
        e = y - t_ref[...]
        part = jnp.sum(jnp.sum(e * e, axis=0, keepdims=True), axis=1, keepdims=True) * (0.5 / D)
        loss_ref[...] += jnp.broadcast_to(part, loss_ref.shape)
        dx, dg = _rms_bwd_parts(x, gg, e * (1.0 / D))
        dh_ref[...] = dx
        dg_ref[...] += dg

    return pl.pallas_call(
        body, name="loss_head", grid=(T // tm,),
        in_specs=[_row_spec(tm, D), _vec_spec(D), _row_spec(tm, D)],
        out_specs=[_vec_spec(LANES), _row_spec(tm, D), _vec_spec(D)],
        out_shape=[jax.ShapeDtypeStruct((1, LANES), F32), jax.ShapeDtypeStruct((T, D), F32),
                   jax.ShapeDtypeStruct((1, D), F32)],
        compiler_params=_params("arbitrary"),
    )(h, g.reshape(1, D), tgt)


def _carry(plan, first, mid, last):
    pl.when(first)(plan[0])
    if len(plan) == 3:
        pl.when(mid)(plan[1])
    pl.when(last)(plan[-1])


def _ffn_fwd_act(h, g, wffn, name, gather=()):
    T, D = h.shape
    F = wffn.shape[1]
    tm, tf = _tile(T, TM_FFN_ACT, 16), _tile(F, TF_FFN, LANES)
    ni, nf, ng = T // tm, F // tf, len(gather)

    def body(*refs):
        h_ref, g_ref, wg_ref, wu_ref = refs[:4]
        srcs, (a_ref, b_ref, s_ref, n_ref), outs = refs[4:4 + ng], refs[4 + ng:8 + ng], refs[8 + ng:8 + 2 * ng]
        i, j = pl.program_id(0), pl.program_id(1)
        if ng:
            _carry(_gather_plan(gather, srcs, outs, *refs[8 + 2 * ng:]), jnp.logical_and(i == 0, j == 0),
                   jnp.logical_and(i == (3 * ni) // 4, j == 0), jnp.logical_and(i == ni - 1, j == nf - 1))

        @pl.when(j == 0)
        def _():
            n_ref[...] = _rms_parts(h_ref[...], g_ref[...])[0].astype(BF)

        n = n_ref[...]
        a = _dot_nt(n, wg_ref[...])
        b = _dot_nt(n, wu_ref[...])
        a_ref[...] = a.astype(BF)
        b_ref[...] = b.astype(BF)
        s_ref[...] = (a * _sigmoid(a) * b).astype(BF)

    tile = pl.BlockSpec((tm, tf), lambda i, j: (i, j))
    w = [pl.BlockSpec((None, tf, D), functools.partial(lambda k, i, j: (k, j, 0), k)) for k in (0, 1)]
    hbm = pl.BlockSpec(memory_space=pl.ANY)
    outs = pl.pallas_call(
        body, name=name, grid=(ni, nf), in_specs=[_row_spec(tm, D), _vec_spec(D)] + w + [hbm] * ng,
        out_specs=[tile, tile, tile, _row_spec(tm, D)] + [hbm] * ng,
        out_shape=[jax.ShapeDtypeStruct((T, F), BF)] * 3 + [jax.ShapeDtypeStruct((T, D), BF)] + _gathered_shapes(gather),
        scratch_shapes=_gather_sems(ng), compiler_params=_params("arbitrary", "arbitrary"),
    )(h, g.reshape(1, D), wffn, wffn, *gather)
    return outs[0], outs[1], outs[2], outs[3], list(outs[4:])


def _ffn_fwd_out(s, wffn, h, name, gather=(), ple=None):
    T, F = s.shape
    D = h.shape[1]
    tm = _tile(T, TM_FFN_IN, 16)
    ni, ng, ne = T // tm, len(gather), 4 if ple else 0

    def body(*refs):
        s_ref, w_ref, h_ref = refs[:3]
        pin, rest = refs[3:3 + ne], refs[3 + ne:]
        srcs, o_ref, pout, outs = rest[:ng], rest[ng], rest[ng + 1:ng + 1 + ne], rest[ng + 1 + ne:2 * ng + 1 + ne]
        i = pl.program_id(0)
        if ng:
            _carry(_gather_plan(gather, srcs, outs, *rest[2 * ng + 1 + ne:]), i == 0, i == (3 * ni) // 4, i == ni - 1)
        x = h_ref[...] + 0.5 * _dot(s_ref[...], w_ref[...])
        o_ref[...] = x
        if ple:
            g_ref, wg_ref, wp_ref, p_ref = pin
            e_ref, n_ref, gate_ref, pp_ref = pout
            n = _rms_parts(x, g_ref[...])[0].astype(BF)
            gate = _sigmoid(_dot(n, wg_ref[...]))
            pp = _dot_nt(p_ref[...].astype(BF), wp_ref[...])
            e_ref[...] = x + gate * pp
            n_ref[...] = n
            gate_ref[...] = gate.astype(BF)
            pp_ref[...] = pp.astype(BF)

    hbm = pl.BlockSpec(memory_space=pl.ANY)
    row = _row_spec(tm, D)
    once = lambda rows, cols: pl.BlockSpec((rows, cols), lambda i: (0, 0), pipeline_mode=pl.Buffered(1))
    extra_in, extra_out, extra_shape, extra_args = [], [], [], []
    if ple:
        g, wg, wp, p = ple
        P = p.shape[1]
        extra_in = [_vec_spec(D), once(D, D), once(D, P), _row_spec(tm, P)]
        extra_out = [row] * 4
        extra_shape = [jax.ShapeDtypeStruct((T, D), F32)] + [jax.ShapeDtypeStruct((T, D), BF)] * 3
        extra_args = [g.reshape(1, D), wg, wp, p]
    outs = pl.pallas_call(
        body, name=name, grid=(ni,),
        in_specs=[_row_spec(tm, F), pl.BlockSpec((None, F, D), lambda i: (2, 0, 0), pipeline_mode=pl.Buffered(1)), row]
        + extra_in + [hbm] * ng,
        out_specs=[row] + extra_out + [hbm] * ng,
        out_shape=[jax.ShapeDtypeStruct((T, D), F32)] + extra_shape + _gathered_shapes(gather),
        scratch_shapes=_gather_sems(ng), compiler_params=_params("arbitrary"),
    )(s, wffn, h, *extra_args, *gather)
    return outs[0], list(outs[1:1 + ne]), list(outs[1 + ne:])


def _ffn_bwd_act(dh, a, b, wffn, name, scatter=()):
    T, D = dh.shape
    F = wffn.shape[1]
    tm, tf = _tile(T, TM_FFN_ACT, 16), _tile(F, TF_FFN, LANES)
    ni, nf, ng = T // tm, F // tf, len(scatter)

    def body(*refs):
        dh_ref, a_ref, b_ref, wd_ref = refs[:4]
        srcs, (da_ref, db_ref, dhb_ref), outs = refs[4:4 + ng], refs[4 + ng:7 + ng], refs[7 + ng:7 + 2 * ng]
        i, j = pl.program_id(0), pl.program_id(1)
        if ng:
            _carry(_scatter_plan(scatter, srcs, outs, *refs[7 + 2 * ng:]), jnp.logical_and(i == 0, j == 0), None,
                   jnp.logical_and(i == ni - 1, j == nf - 1))

        @pl.when(j == 0)
        def _():
            dhb_ref[...] = dh_ref[...].astype(BF)

        ds = 0.5 * _dot_nt(dhb_ref[...], wd_ref[...])
        av, bv = a_ref[...].astype(F32), b_ref[...].astype(F32)
        sig = _sigmoid(av)
        da_ref[...] = (ds * bv * (sig * (1.0 + av * (1.0 - sig)))).astype(BF)
        db_ref[...] = (ds * (av * sig)).astype(BF)

    tile = pl.BlockSpec((tm, tf), lambda i, j: (i, j))
    hbm = pl.BlockSpec(memory_space=pl.ANY)
    outs = pl.pallas_call(
        body, name=name, grid=(ni, nf),
        in_specs=[_row_spec(tm, D), tile, tile, pl.BlockSpec((None, tf, D), lambda i, j: (2, j, 0))] + [hbm] * ng,
        out_specs=[tile, tile, _row_spec(tm, D)] + [hbm] * ng,
        out_shape=[jax.ShapeDtypeStruct((T, F), BF)] * 2 + [jax.ShapeDtypeStruct((T, D), BF)]
        + _scattered_shapes(scatter),
        scratch_shapes=_scatter_sems(ng), compiler_params=_params("arbitrary", "arbitrary"),
    )(dh, a, b, wffn, *[piece[0] for piece in scatter])
    return outs[0], outs[1], outs[2], list(outs[3:])


def _two_dot_norm_bwd(x1, x2, w, w_specs, h, g, dh, name, scatter=(), x_specs=None):
    T, K = x1.shape
    D = h.shape[1]
    tm = _tile(T, TM_FFN_IN, 16)
    ni, ng = T // tm, len(scatter)
    x_specs = x_specs or (lambda tm: [pl.BlockSpec((tm, K), lambda i: (i, 0))] * 2)

    def body(*refs):
        x1_ref, x2_ref, w1_ref, w2_ref, h_ref, g_ref, dh_ref = refs[:7]
        srcs, (o_ref, dg_ref), outs = refs[7:7 + ng], refs[7 + ng:9 + ng], refs[9 + ng:9 + 2 * ng]
        i = pl.program_id(0)
        if ng:
            _carry(_scatter_plan(scatter, srcs, outs, *refs[9 + 2 * ng:]), i == 0, None, i == ni - 1)

        @pl.when(i == 0)
        def _():
            dg_ref[...] = jnp.zeros_like(dg_ref)

        dn = _dot(x1_ref[...], w1_ref[...]) + _dot(x2_ref[...], w2_ref[...])
        dx, dg = _rms_bwd_parts(h_ref[...], g_ref[...], dn)
        o_ref[...] = dh_ref[...] + dx
        dg_ref[...] += dg

    hbm = pl.BlockSpec(memory_space=pl.ANY)
    outs = pl.pallas_call(
        body, name=name, grid=(ni,),
        in_specs=x_specs(tm) + w_specs + [_row_spec(tm, D), _vec_spec(D), _row_spec(tm, D)] + [hbm] * ng,
        out_specs=[_row_spec(tm, D), _vec_spec(D)] + [hbm] * ng,
        out_shape=[jax.ShapeDtypeStruct((T, D), F32), jax.ShapeDtypeStruct((1, D), F32)] + _scattered_shapes(scatter),
        scratch_shapes=_scatter_sems(ng), compiler_params=_params("arbitrary"),
    )(x1, x2, w, w, h, g.reshape(1, D), dh, *[piece[0] for piece in scatter])
    return outs[0], outs[1], list(outs[2:])


def _ffn_bwd_in(da, db, wffn, h, g, dh, name, scatter=()):
    F, D = wffn.shape[1:]
    specs = [pl.BlockSpec((None, F, D), functools.partial(lambda k, i: (k, 0, 0), k), pipeline_mode=pl.Buffered(1))
             for k in (0, 1)]
    return _two_dot_norm_bwd(da, db, wffn, specs, h, g, dh, name, scatter)


def _lru_in_bwd(dz, win, h, g, dh, name):
    R, D = win.shape[0] // 2, win.shape[1]
    specs = [pl.BlockSpec((R, D), functools.partial(lambda k, i: (k, 0), k), pipeline_mode=pl.Buffered(1)) for k in (0, 1)]
    halves = lambda tm: [pl.BlockSpec((tm, R), functools.partial(lambda k, i: (i, k), k)) for k in (0, 1)]
    return _two_dot_norm_bwd(dz, dz, win, specs, h, g, dh, name, x_specs=halves)[:2]


def _ffn_bwd_w(da, db, s, n, dhb, name, scatter=()):
    T, F = da.shape
    D = n.shape[1]
    tf, tk = _tile(F, TF_FFN_WG, LANES), _tile(T, TK_FFN_WG, 16)
    nj, nk, ng = F // tf, T // tk, len(scatter)

    def body(*refs):
        da_ref, db_ref, s_ref, n_ref, dh_ref = refs[:5]
        srcs, o_ref, outs = refs[5:5 + ng], refs[5 + ng], refs[6 + ng:6 + 2 * ng]
        g_sc, u_sc, d_sc = refs[6 + 2 * ng:9 + 2 * ng]
        j, k = pl.program_id(0), pl.program_id(1)
        if ng:
            _carry(_scatter_plan(scatter, srcs, outs, *refs[9 + 2 * ng:]), jnp.logical_and(j == 0, k == 0), None,
                   jnp.logical_and(j == nj - 1, k == nk - 1))

        @pl.when(k == 0)
        def _():
            g_sc[...] = jnp.zeros_like(g_sc)
            u_sc[...] = jnp.zeros_like(u_sc)
            d_sc[...] = jnp.zeros_like(d_sc)

        nv = n_ref[...]
        g_sc[...] += _dot_tn(da_ref[...], nv)
        u_sc[...] += _dot_tn(db_ref[...], nv)
        d_sc[...] += _dot_tn(s_ref[...], dh_ref[...])

        @pl.when(k == nk - 1)
        def _():
            o_ref[0] = g_sc[...].astype(BF)
            o_ref[1] = u_sc[...].astype(BF)
            o_ref[2] = (0.5 * d_sc[...]).astype(BF)

    act = pl.BlockSpec((tk, tf), lambda j, k: (k, j))
    tok = pl.BlockSpec((tk, D), lambda j, k: (k, 0))
    hbm = pl.BlockSpec(memory_space=pl.ANY)
    outs = pl.pallas_call(
        body, name=name, grid=(nj, nk), in_specs=[act, act, act, tok, tok] + [hbm] * ng,
        out_specs=[pl.BlockSpec((3, tf, D), lambda j, k: (0, j, 0), pipeline_mode=pl.Buffered(1))] + [hbm] * ng,
        out_shape=[jax.ShapeDtypeStruct((3, F, D), BF)] + _scattered_shapes(scatter),
        scratch_shapes=[pltpu.VMEM((tf, D), F32)] * 3 + _scatter_sems(ng),
        compiler_params=_params("arbitrary", "arbitrary"),
    )(da, db, s, n, dhb, *[piece[0] for piece in scatter])
    return outs[0], list(outs[1:])


def _ple_bwd(dh, gate, pp, h, g, wg, name):
    T, D = dh.shape
    tm = _tile(T, TM_EW, 16)

    def body(dh_ref, gate_ref, pp_ref, h_ref, g_ref, wg_ref, o_ref, dz_ref, dp_ref, dg_ref):
        @pl.when(pl.program_id(0) == 0)
        def _():
            dg_ref[...] = jnp.zeros_like(dg_ref)

        d, gate = dh_ref[...], gate_ref[...].astype(F32)
        dz = (d * pp_ref[...].astype(F32) * gate * (1.0 - gate)).astype(BF)
        dx, dg = _rms_bwd_parts(h_ref[...], g_ref[...], _dot_nt(dz, wg_ref[...]))
        o_ref[...] = d + dx
        dz_ref[...] = dz
        dp_ref[...] = (d * gate).astype(BF)
        dg_ref[...] += dg

    row = _row_spec(tm, D)
    return pl.pallas_call(
        body, name=name, grid=(T // tm,), in_specs=[row, row, row, row, _vec_spec(D), _vec_spec(D, D)],
        out_specs=[row, row, row, _vec_spec(D)],
        out_shape=[jax.ShapeDtypeStruct((T, D), F32), jax.ShapeDtypeStruct((T, D), BF), jax.ShapeDtypeStruct((T, D), BF),
                   jax.ShapeDtypeStruct((1, D), F32)],
        compiler_params=_params("arbitrary"),
    )(dh, gate, pp, h, g.reshape(1, D), wg)


def _gate_dot(x, w_ref, transposed):
    out = []
    for j, lo in enumerate(_gate_spans(w_ref.shape[0])):
        xs, blk, span = x[:, lo:lo + GATE_SPAN], pl.ds(j * GATE_COLS, GATE_COLS), pl.ds(lo, GATE_SPAN)
        out.append(_dot_nt(xs, w_ref[blk, span]) if transposed else _dot(xs, w_ref[span, blk]))
    return jnp.concatenate(out, axis=1)


def _lru_in(h, g, win, name):
    T, D = h.shape
    R = win.shape[0] // 2
    tm = _tile(T, TM_EW, 16)

    def body(h_ref, g_ref, w_ref, n_ref, gb_ref, xb_ref):
        n = _rms_parts(h_ref[...], g_ref[...])[0].astype(BF)
        n_ref[...] = n
        z = _dot_nt(n, w_ref[...])
        gb_ref[...] = z[:, :R].astype(BF)
        xb_ref[...] = z[:, R:]

    return pl.pallas_call(
        body, name=name, grid=(T // tm,),
        in_specs=[_row_spec(tm, D), _vec_spec(D), pl.BlockSpec((2 * R, D), lambda i: (0, 0), pipeline_mode=pl.Buffered(1))],
        out_specs=[_row_spec(tm, D), _row_spec(tm, R), _row_spec(tm, R)],
        out_shape=[jax.ShapeDtypeStruct((T, D), BF), jax.ShapeDtypeStruct((T, R), BF), jax.ShapeDtypeStruct((T, R), F32)],
        compiler_params=_params("parallel"),
    )(h, g.reshape(1, D), win)


def _lru_fwd(gb, xb, conv_w, conv_b, wa, wx, b_a, b_x, a_param, wout, h, name):
    T, R = xb.shape
    D = h.shape[1]
    tb = _tile(T, TB_SEQ, HALO)
    per, ng = tb // HALO, tb // SUBLANES

    def body(g_ref, x_ref, halo_ref, cw_ref, cb_ref, wa_ref, wx_ref, ba_ref, bx_ref, ap_ref, wo_ref, h_ref,
             o_ref, xc_ref, r_ref, ig_ref, a_ref, hs_ref, y_ref, ext, carry, a_sc, b_sc):
        i = pl.program_id(0)

        @pl.when(i == 0)
        def _():
            carry[...] = jnp.zeros_like(carry)

        ext[pl.ds(0, HALO), :] = jnp.where(i > 0, halo_ref[...], 0.0)
        ext[pl.ds(HALO, tb), :] = x_ref[...]
        xc = cb_ref[...] + cw_ref[0:1, :] * ext[pl.ds(HALO - 3, tb), :]
        for k in range(1, CONV_WIDTH):
            xc = xc + cw_ref[k:k + 1, :] * ext[pl.ds(HALO - 3 + k, tb), :]
        xcb = xc.astype(BF)
        r = _sigmoid_pos(_gate_dot(xcb, wa_ref, False) + ba_ref[...])
        ig = _sigmoid(_gate_dot(xcb, wx_ref, False) + bx_ref[...])
        la = -LRU_C * r * _softplus_neg(ap_ref[...])
        av = jnp.exp(la)
        xc_ref[...] = xc
        r_ref[...] = r
        ig_ref[...] = ig
        a_ref[...] = av
        A = av.reshape(ng, SUBLANES, R)
        B = (jnp.sqrt(_one_minus_sq(la, av)) * (ig * xc)).reshape(ng, SUBLANES, R)
        sub = lax.broadcasted_iota(jnp.int32, (1, SUBLANES, R), 1)
        for k in (1, 2, 4):
            m = sub >= k
            a_n = jnp.where(m, pltpu.roll(A, k, 1), 1.0)
            b_n = jnp.where(m, pltpu.roll(B, k, 1), 0.0)
            B = A * b_n + B
            A = A * a_n
        a_sc[...] = A.reshape(tb, R)
        b_sc[...] = B.reshape(tb, R)

        def group(q, c):
            rows = pl.ds(pl.multiple_of(q * SUBLANES, SUBLANES), SUBLANES)
            hg = a_sc[rows, :] * c + b_sc[rows, :]
            hs_ref[rows, :] = hg
            return hg[SUBLANES - 1:SUBLANES, :]

        carry[...] = lax.fori_loop(0, ng, group, carry[...])
        y = (hs_ref[...] * _gelu_parts(g_ref[...].astype(F32))[0]).astype(BF)
        y_ref[...] = y
        o_ref[...] = h_ref[...] + _dot(y, wo_ref[...])

    once = lambda rows, cols: pl.BlockSpec((rows, cols), lambda i: (0, 0), pipeline_mode=pl.Buffered(1))
    gate = pl.BlockSpec((tb, R), lambda i: (i, 0))
    halo = pl.BlockSpec((HALO, R), lambda i: (jnp.maximum(i * per - 1, 0), 0))
    return pl.pallas_call(
        body, name=name, grid=(T // tb,),
        in_specs=[gate, gate, halo, _vec_spec(R, CONV_WIDTH), _vec_spec(R), once(R, R), once(R, R), _vec_spec(R),
                  _vec_spec(R), _vec_spec(R), once(R, D), _row_spec(tb, D)],
        out_specs=[_row_spec(tb, D)] + [gate] * 6,
        out_shape=[jax.ShapeDtypeStruct((T, D), F32)] + [jax.ShapeDtypeStruct((T, R), F32)] * 5
        + [jax.ShapeDtypeStruct((T, R), BF)],
        scratch_shapes=[pltpu.VMEM((HALO + tb, R), F32), pltpu.VMEM((1, R), F32), pltpu.VMEM((tb, R), F32),
                        pltpu.VMEM((tb, R), F32)],
        compiler_params=_params("arbitrary"),
    )(gb, xb, xb, conv_w, conv_b.reshape(1, R), wa, wx, b_a.reshape(1, R), b_x.reshape(1, R), a_param.reshape(1, R), wout, h)


def _lru_bwd(dh, hs, gb, xb, a, r, ig, xc, wa, wx, wout, conv_w, a_param, name):
    T, R = hs.shape
    D = dh.shape[1]
    tb = _tile(T, TB_SEQ, HALO)
    per, nt, ng = tb // HALO, T // tb, tb // SUBLANES

    def body(dh_ref, h_ref, hp_ref, g_ref, x_ref, xp_ref, a_ref, r_ref, ig_ref, xc_ref, wa_ref, wx_ref, wo_ref, cw_ref,
             ap_ref, dz_ref, dpa_ref, dpx_ref, dsp_ref, dba_ref, dbx_ref, dcb_ref, dcw_ref,
             hext, xext, dext, carry, later, a_sc, b_sc, d_sc, l_sc):
        i = pl.program_id(0)

        @pl.when(i == 0)
        def _():
            for ref in (dsp_ref, dba_ref, dbx_ref, dcb_ref, dcw_ref, carry, later):
                ref[...] = jnp.zeros_like(ref)

        dy = _dot_nt(dh_ref[...].astype(BF), wo_ref[...])
        gl, dgl = _gelu_parts(g_ref[...].astype(F32))
        hv, av = h_ref[...], a_ref[...]
        dhd = dy * gl
        dz_ref[:, pl.ds(0, R)] = (dy * hv * dgl).astype(BF)
        d_sc[...] = dhd
        A = av.reshape(ng, SUBLANES, R)
        B = A * dhd.reshape(ng, SUBLANES, R)
        sub = lax.broadcasted_iota(jnp.int32, (1, SUBLANES, R), 1)
        for k in (1, 2, 4):
            m = sub < SUBLANES - k
            a_n = jnp.where(m, pltpu.roll(A, SUBLANES - k, 1), 1.0)
            b_n = jnp.where(m, pltpu.roll(B, SUBLANES - k, 1), 0.0)
            B = A * b_n + B
            A = A * a_n
        a_sc[...] = A.reshape(tb, R)
        b_sc[...] = B.reshape(tb, R)
        sub8 = lax.broadcasted_iota(jnp.int32, (SUBLANES, R), 0)

        def group(q, c):
            rows = pl.ds(pl.multiple_of((ng - 1 - q) * SUBLANES, SUBLANES), SUBLANES)
            mu = a_sc[rows, :] * c + b_sc[rows, :]
            l_sc[rows, :] = d_sc[rows, :] + jnp.where(sub8 == SUBLANES - 1, c, pltpu.roll(mu, SUBLANES - 1, 0))
            return mu[0:1, :]

        carry[...] = lax.fori_loop(0, ng, group, carry[...])
        lam = l_sc[...]
        hext[pl.ds(0, HALO), :] = jnp.where(i < nt - 1, hp_ref[...], 0.0)
        hext[pl.ds(HALO, tb), :] = hv
        h_prev = hext[pl.ds(HALO - 1, tb), :]
        rv, igv, xcv = r_ref[...], ig_ref[...], xc_ref[...]
        sp = _softplus_neg(ap_ref[...])
        mult = jnp.sqrt(_one_minus_sq(-LRU_C * rv * sp, av))
        dla = lam * h_prev * av - lam * (igv * xcv) * (av * av) / mult
        du = lam * mult
        dpa = (dla * (-LRU_C) * sp) * rv * (1.0 - rv)
        dpx = (du * xcv) * igv * (1.0 - igv)
        dsp_ref[...] += jnp.sum(dla * (-LRU_C) * rv, axis=0, keepdims=True)
        dba_ref[...] += jnp.sum(dpa, axis=0, keepdims=True)
        dbx_ref[...] += jnp.sum(dpx, axis=0, keepdims=True)
        dpab, dpxb = dpa.astype(BF), dpx.astype(BF)
        dpa_ref[...] = dpab
        dpx_ref[...] = dpxb
        dxc = du * igv + _gate_dot(dpab, wa_ref, True) + _gate_dot(dpxb, wx_ref, True)
        dext[pl.ds(0, tb), :] = dxc
        dext[pl.ds(tb, SUBLANES), :] = later[...]
        later[...] = dxc[0:SUBLANES, :]
        xext[pl.ds(0, HALO), :] = jnp.where(i < nt - 1, xp_ref[...], 0.0)
        xext[pl.ds(HALO, tb), :] = x_ref[...]
        dxb = cw_ref[CONV_WIDTH - 1:CONV_WIDTH, :] * dxc
        for k in range(CONV_WIDTH - 1):
            dxb = dxb + cw_ref[k:k + 1, :] * dext[pl.ds(CONV_WIDTH - 1 - k, tb), :]
        dz_ref[:, pl.ds(R, R)] = dxb.astype(BF)
        for k in range(CONV_WIDTH):
            dcw_ref[k:k + 1, :] += jnp.sum(dxc * xext[pl.ds(HALO - 3 + k, tb), :], axis=0, keepdims=True)
        dcb_ref[...] += jnp.sum(dxc, axis=0, keepdims=True)

        @pl.when(i == nt - 1)
        def _():
            dsp_ref[...] = dsp_ref[...] * (-_sigmoid(-ap_ref[...]))

    once = lambda rows, cols: pl.BlockSpec((rows, cols), lambda i: (0, 0), pipeline_mode=pl.Buffered(1))
    t0 = pl.BlockSpec((tb, R), lambda i: (nt - 1 - i, 0))
    prev = pl.BlockSpec((HALO, R), lambda i: (jnp.maximum((nt - 1 - i) * per - 1, 0), 0))
    return pl.pallas_call(
        body, name=name, grid=(nt,),
        in_specs=[pl.BlockSpec((tb, D), lambda i: (nt - 1 - i, 0)), t0, prev, t0, t0, prev, t0, t0, t0, t0,
                  once(R, R), once(R, R), once(R, D), _vec_spec(R, CONV_WIDTH), _vec_spec(R)],
        out_specs=[pl.BlockSpec((tb, 2 * R), lambda i: (nt - 1 - i, 0)), t0, t0] + [_vec_spec(R)] * 4
        + [_vec_spec(R, SUBLANES)],
        out_shape=[jax.ShapeDtypeStruct((T, 2 * R), BF)] + [jax.ShapeDtypeStruct((T, R), BF)] * 2
        + [jax.ShapeDtypeStruct((1, R), F32)] * 4
        + [jax.ShapeDtypeStruct((SUBLANES, R), F32)],
        scratch_shapes=[pltpu.VMEM((HALO + tb, R), F32), pltpu.VMEM((HALO + tb, R), F32),
                        pltpu.VMEM((tb + SUBLANES, R), F32), pltpu.VMEM((1, R), F32), pltpu.VMEM((SUBLANES, R), F32)]
        + [pltpu.VMEM((tb, R), F32)] * 4,
        compiler_params=_params("arbitrary"),
    )(dh, hs, hs, gb, xb, xb, a, r, ig, xc, wa, wx, wout, conv_w, a_param.reshape(1, R))


def _gate_spans(R):
    d = R // LRU_HEADS
    spans = [min((j * GATE_COLS // d) * d // LANES * LANES, R - GATE_SPAN) for j in range(R // GATE_COLS)]
    assert R % GATE_COLS == 0 and all(lo + GATE_SPAN >= (((j + 1) * GATE_COLS - 1) // d + 1) * d for j, lo in enumerate(spans))
    return spans


def _lru_gates_dw(xc, dpa, dpx, name):
    T, R = xc.shape
    tk = _tile(T, 1024, 16)
    spans = _gate_spans(R)
    nb = len(spans)

    def body(x_ref, a_ref, b_ref, o_ref):
        @pl.when(pl.program_id(0) == 0)
        def _():
            o_ref[...] = jnp.zeros_like(o_ref)

        for j, lo in enumerate(spans):
            xs = x_ref[:, pl.ds(lo, GATE_SPAN)].astype(BF)
            cols = pl.ds(j * GATE_COLS, GATE_COLS)
            o_ref[0, j] += _dot_tn(xs, a_ref[:, cols])
            o_ref[1, j] += _dot_tn(xs, b_ref[:, cols])

    row = _row_spec(tk, R)
    out = pl.pallas_call(
        body, name=name, grid=(T // tk,), in_specs=[row, row, row],
        out_specs=pl.BlockSpec((2, nb, GATE_SPAN, GATE_COLS), lambda i: (0, 0, 0, 0)),
        out_shape=jax.ShapeDtypeStruct((2, nb, GATE_SPAN, GATE_COLS), F32), compiler_params=_params("arbitrary"),
    )(xc, dpa, dpx)
    dense = jnp.zeros((2, R, R), F32)
    for j, lo in enumerate(spans):
        dense = dense.at[:, lo:lo + GATE_SPAN, j * GATE_COLS:(j + 1) * GATE_COLS].set(out[:, j])
    return dense[0], dense[1]


def _window_sums(e, n, back):
    out, s = [], e
    for k in (1, 2, 4, 8):
        s = s + pltpu.roll(s, k if back else n - k, 0)
        out.append(s)
    return out


def _pool_fwd(h, g, w, b, scale, name):
    T, D = h.shape
    G = len(POOL_WINDOWS)
    gd = D // G
    tb = _tile(T, TB_SEQ, HALO)
    per = tb // HALO

    def body(h_ref, hp_ref, g_ref, w_ref, b_ref, s_ref, o_ref, u_ref, yb_ref):
        i = pl.program_id(0)
        t = i * tb + lax.broadcasted_iota(jnp.int32, (tb, gd), 0) + 1
        hv = h_ref[...]
        xn = _rms_parts(hv, g_ref[...])[0]
        xp = jnp.where(i > 0, _rms_parts(hp_ref[...], g_ref[...])[0], 0.0)
        for k, win in enumerate(POOL_WINDOWS):
            cols = slice(k * gd, (k + 1) * gd)
            x = xn[:, cols]
            e = jnp.concatenate([xp[:, cols], x], axis=0)
            sw = _window_sums(e, HALO + tb, True)[k][HALO:, :]
            u = (sw / jnp.minimum(t, win).astype(F32) - x).astype(BF)
            yb = _dot(u, w_ref[k]) + b_ref[:, cols]
            u_ref[:, cols] = u
            yb_ref[:, cols] = yb
            o_ref[:, cols] = hv[:, cols] + yb * s_ref[:, cols]

    tile = _row_spec(tb, D)
    prev = pl.BlockSpec((HALO, D), lambda i: (jnp.maximum(i * per - 1, 0), 0))
    return pl.pallas_call(
        body, name=name, grid=(T // tb,),
        in_specs=[tile, prev, _vec_spec(D), pl.BlockSpec((G, gd, gd), lambda i: (0, 0, 0)), _vec_spec(D), _vec_spec(D)],
        out_specs=[tile, tile, tile],
        out_shape=[jax.ShapeDtypeStruct((T, D), F32), jax.ShapeDtypeStruct((T, D), BF), jax.ShapeDtypeStruct((T, D), F32)],
        compiler_params=_params("parallel"),
    )(h, h, g.reshape(1, D), w, b.reshape(1, D), scale.reshape(1, D))


def _pool_bwd(dm, u, yb, w, scale, h, g, name):
    T, D = dm.shape
    G = len(POOL_WINDOWS)
    gd = D // G
    tb = _tile(T, TB_SEQ, HALO)
    nt = T // tb

    def body(d_ref, u_ref, yb_ref, w_ref, s_ref, h_ref, g_ref, o_ref, dw_ref, db_ref, ds_ref, dg_ref, later):
        i = pl.program_id(0)

        @pl.when(i == 0)
        def _():
            for ref in (dw_ref, db_ref, ds_ref, dg_ref, later):
                ref[...] = jnp.zeros_like(ref)

        d, sc = d_ref[...], s_ref[...]
        ds_ref[...] += jnp.sum(d * yb_ref[...], axis=0, keepdims=True)
        db_ref[...] += jnp.sum(d * sc, axis=0, keepdims=True)
        t = (nt - 1 - i) * tb + lax.broadcasted_iota(jnp.int32, (tb, gd), 0) + 1
        parts = []
        for k, win in enumerate(POOL_WINDOWS):
            cols = slice(k * gd, (k + 1) * gd)
            dy = (d[:, cols] * sc[:, cols]).astype(BF)
            du = _dot_nt(dy, w_ref[k])
            dw_ref[k] += _dot_tn(u_ref[:, cols], dy)
            v = du / jnp.minimum(t, win).astype(F32)
            e = jnp.concatenate([v, later[:, cols]], axis=0)
            later[:, cols] = v[0:HALO, :]
            parts.append(_window_sums(e, tb + HALO, False)[k][:tb, :] - du)
        dx, dg = _rms_bwd_parts(h_ref[...], g_ref[...], jnp.concatenate(parts, axis=1))
        o_ref[...] = d + dx
        dg_ref[...] += dg

    tile = pl.BlockSpec((tb, D), lambda i: (nt - 1 - i, 0))
    whole = pl.BlockSpec((G, gd, gd), lambda i: (0, 0, 0))
    return pl.pallas_call(
        body, name=name, grid=(nt,), in_specs=[tile, tile, tile, whole, _vec_spec(D), tile, _vec_spec(D)],
        out_specs=[tile, whole, _vec_spec(D), _vec_spec(D), _vec_spec(D)],
        out_shape=[jax.ShapeDtypeStruct((T, D), F32), jax.ShapeDtypeStruct((G, gd, gd), F32)]
        + [jax.ShapeDtypeStruct((1, D), F32)] * 3,
        scratch_shapes=[pltpu.VMEM((HALO, D), F32)], compiler_params=_params("arbitrary"),
    )(dm, u, yb, w, scale.reshape(1, D), h, g.reshape(1, D))


def _adamw(w, g, m, v, name):
    shape = w.shape
    cols = shape[-1] if w.ndim > 1 else shape[0]
    rows = w.size // cols
    tr = _tile(rows, TR_ADAM, SUBLANES)
    c1, c2 = 1.0 / (1.0 - ADAM_B1 ** ADAM_STEP), 1.0 / (1.0 - ADAM_B2 ** ADAM_STEP)

    def body(w_ref, g_ref, m_ref, v_ref, d_ref, mo_ref, vo_ref):
        gv = g_ref[...]
        mn = ADAM_B1 * m_ref[...] + (1.0 - ADAM_B1) * gv
        vn = ADAM_B2 * v_ref[...] + (1.0 - ADAM_B2) * (gv * gv)
        d_ref[...] = -ADAM_LR * ((mn * c1) / (jnp.sqrt(vn * c2) + ADAM_EPS) + ADAM_WD * w_ref[...])
        mo_ref[...] = mn
        vo_ref[...] = vn

    spec = _row_spec(tr, cols)
    outs = pl.pallas_call(
        body, name=name, grid=(rows // tr,), in_specs=[spec] * 4, out_specs=[spec] * 3,
        out_shape=[jax.ShapeDtypeStruct((rows, cols), F32)] * 3, compiler_params=_params("parallel"),
    )(*[t.reshape(rows, cols) for t in (w, g, m, v)])
    return [o.reshape(shape) for o in outs]


def _sum_devices(parts, name):
    n, rows, cols = parts.shape
    tr = _tile(rows, 1024, SUBLANES)

    def body(p_ref, o_ref):
        acc = p_ref[0].astype(F32)
        for k in range(1, n):
            acc = acc + p_ref[k].astype(F32)
        o_ref[...] = acc

    return pl.pallas_call(
        body, name=name, grid=(rows // tr,), in_specs=[pl.BlockSpec((n, tr, cols), lambda i: (0, i, 0))],
        out_specs=_row_spec(tr, cols), out_shape=jax.ShapeDtypeStruct((rows, cols), F32),
        compiler_params=_params("parallel"),
    )(parts)


def _position():
    return lax.axis_index("x"), lax.axis_index("y"), lax.axis_index("c")


def _gathered_shapes(blocks):
    return [jax.ShapeDtypeStruct((b.shape[0], N_DEV * b.shape[1], b.shape[2]), b.dtype) for b in blocks]


def _gather_sems(ng):
    return [pltpu.SemaphoreType.DMA((ng, 7)), pltpu.SemaphoreType.DMA((ng, 7)), pltpu.SemaphoreType.DMA((ng,))] if ng else []


def _gather_plan(blocks, srcs, outs, send_sems, recv_sems, local_sems):
    ng = len(blocks)
    x, y, c = _position()
    me, sibling = (x, y, c), (x, y, 1 - c)
    chips = [(1 - x, y), (x, 1 - y), (1 - x, 1 - y)]

    def rows(g, px, py, pc):
        r = blocks[g].shape[1]
        return outs[g].at[:, pl.ds((4 * px + 2 * py + pc) * r, r), :]

    def copy(g, k, block, to, src=None):
        return pltpu.make_async_remote_copy(
            src_ref=rows(g, *block) if src is None else src, dst_ref=rows(g, *block),
            send_sem=send_sems.at[g, k], recv_sem=recv_sems.at[g, k], device_id=to, device_id_type=MESH)

    def mine(g):
        return pltpu.make_async_copy(srcs[g], rows(g, *me), local_sems.at[g])

    def first(g):
        return [copy(g, 0, me, sibling, src=srcs[g])] + [copy(g, 1 + j, me, (*chip, c), src=srcs[g])
                                                         for j, chip in enumerate(chips)]

    def passed(g):
        return [copy(g, 4 + j, (*chip, c), sibling) for j, chip in enumerate(chips)]

    def start():
        for g in range(ng):
            mine(g).start()
            for cp in first(g):
                cp.start()

    def forward():
        for j, chip in enumerate(chips):
            for g in range(ng):
                copy(g, 1 + j, (*chip, c), me).wait_recv()
                copy(g, 4 + j, (*chip, c), sibling).start()

    def finish():
        for g in range(ng):
            copy(g, 0, sibling, me).wait_recv()
            for j, chip in enumerate(chips):
                copy(g, 4 + j, (*chip, 1 - c), me).wait_recv()
            for cp in first(g) + passed(g):
                cp.wait_send()
            mine(g).wait()

    return start, forward, finish


def _all_gather(blocks, name):
    ng = len(blocks)

    def body(*refs):
        start, forward, finish = _gather_plan(blocks, refs[:ng], refs[ng:2 * ng], *refs[2 * ng:])
        start()
        forward()
        finish()

    hbm = pl.BlockSpec(memory_space=pl.ANY)
    return pl.pallas_call(
        body, name=name, in_specs=[hbm] * ng, out_specs=[hbm] * ng, out_shape=_gathered_shapes(blocks),
        scratch_shapes=_gather_sems(ng),
    )(*blocks)


FLIPS = ((0, 0, 1), (1, 0, 0), (0, 1, 0), (1, 1, 0), (1, 0, 1), (0, 1, 1), (1, 1, 1))


def _piece_rows(piece):
    arr, m = piece
    return arr.shape[0] if m is None else 1


def _scattered_shapes(pieces):
    return [jax.ShapeDtypeStruct((N_DEV, _piece_rows(p), p[0].shape[1] // N_DEV, p[0].shape[2]), p[0].dtype)
            for p in pieces]


def _scatter_sems(ng):
    n = len(FLIPS)
    return [pltpu.SemaphoreType.DMA((ng, n)), pltpu.SemaphoreType.DMA((ng, n)), pltpu.SemaphoreType.DMA((ng,))] if ng else []


def _scatter_plan(pieces, srcs, outs, send_sems, recv_sems, local_sems):
    x, y, c = _position()

    def block(g, tx, ty, tc):
        arr, m = pieces[g]
        r = arr.shape[1] // N_DEV
        lead = slice(None) if m is None else pl.ds(m, 1)
        return srcs[g].at[lead, pl.ds((4 * tx + 2 * ty + tc) * r, r), :]

    def copies(g):
        out = []
        for k, (fx, fy, fc) in enumerate(FLIPS):
            tx, ty, tc = (1 - x if fx else x), (1 - y if fy else y), (1 - c if fc else c)
            out.append(pltpu.make_async_remote_copy(
                src_ref=block(g, tx, ty, tc), dst_ref=outs[g].at[k], send_sem=send_sems.at[g, k],
                recv_sem=recv_sems.at[g, k], device_id=(tx, ty, tc), device_id_type=MESH))
        return out

    def mine(g):
        return pltpu.make_async_copy(block(g, x, y, c), outs[g].at[len(FLIPS)], local_sems.at[g])

    def start():
        for g in range(len(pieces)):
            mine(g).start()
            for cp in copies(g):
                cp.start()

    def finish():
        for g in range(len(pieces)):
            for cp in copies(g):
                cp.wait()
            mine(g).wait()

    return start, finish


def _scatter_and_gather(pieces, blocks, name):
    n_p, n_b = len(pieces), len(blocks)

    def body(*refs):
        ins, outs, sems = refs[:n_p + n_b], refs[n_p + n_b:2 * (n_p + n_b)], refs[2 * (n_p + n_b):]
        s_start, s_finish = _scatter_plan(pieces, ins[:n_p], outs[:n_p], *sems[:3])
        g_start, g_forward, g_finish = _gather_plan(blocks, ins[n_p:], outs[n_p:], *sems[3:])
        s_start()
        g_start()
        g_forward()
        g_finish()
        s_finish()

    hbm = pl.BlockSpec(memory_space=pl.ANY)
    outs = pl.pallas_call(
        body, name=name, in_specs=[hbm] * (n_p + n_b), out_specs=[hbm] * (n_p + n_b),
        out_shape=_scattered_shapes(pieces) + _gathered_shapes(blocks),
        scratch_shapes=_scatter_sems(n_p) + _gather_sems(n_b),
    )(*[p[0] for p in pieces], *blocks)
    return list(outs[:n_p]), list(outs[n_p:])


def _scatter_sum(recvs, name):
    _, n, r, c = recvs[0].shape
    tr = _tile(r, TR_SUM, 16)

    def body(*refs):
        o_ref = refs[-1]
        for p, r_ref in enumerate(refs[:-1]):
            acc = r_ref[len(FLIPS)].astype(F32)
            for k in range(len(FLIPS)):
                acc = acc + r_ref[k].astype(F32)
            o_ref[p] = acc

    return pl.pallas_call(
        body, name=name, grid=(n, r // tr),
        in_specs=[pl.BlockSpec((N_DEV, None, tr, c), lambda i, j: (0, i, j, 0))] * len(recvs),
        out_specs=pl.BlockSpec((len(recvs), None, tr, c), lambda i, j: (0, i, j, 0)),
        out_shape=jax.ShapeDtypeStruct((len(recvs), n, r, c), F32), compiler_params=_params("parallel", "parallel"),
    )(*recvs).reshape(len(recvs) * n, r, c)


def _block_diag(w):
    H, d, _ = w.shape
    return (jnp.eye(H, dtype=w.dtype)[:, None, :, None] * w[:, :, None, :]).reshape(H * d, H * d)


def _diag_blocks(dense, H):
    d = dense.shape[0] // H
    return jnp.stack([dense[i * d:(i + 1) * d, i * d:(i + 1) * d] for i in range(H)])


def _local_step(x, p, tgt, W, blocks=None):
    dist = blocks is not None
    L = p.shape[0]
    W = dict(W)

    def gathering(keys):
        return [k for k in keys if k not in W] if dist else []

    def ffn_fwd(h, g, i, f, during_act, during_out, ple=None):
        w = W[("ffn", i, f)]
        keys = gathering(during_act)
        a, b, s, n, got = _ffn_fwd_act(h, g, w, f"ffn{f}_fwd_act_{i}", gather=[blocks[k] for k in keys])
        W.update(zip(keys, got))
        keys = gathering(during_out)
        if ple is not None:
            ple = (W["ple_norm"][i], W[("ple_gate", i)][0], W[("ple_proj", i)][0], ple)
        h, emb, got = _ffn_fwd_out(s, w, h, f"ffn{f}_fwd_out_{i}", gather=[blocks[k] for k in keys], ple=ple)
        W.update(zip(keys, got))
        return (a, b, s, n), h, emb

    saved = []
    h = x
    for i in range(L):
        j = i // 2
        lru = i % 2 == 0
        s = {"h0": h}
        mixer = [("lru_in", j), ("lru_out", j)] if lru else [("pool_w", j)]
        s["ffn1"], h, _ = ffn_fwd(h, W["ffn1_norm"][i], i, 1, [("ffn", i, 2)], mixer)
        s["h1"] = h
        if lru:
            hn, gb, xb = _lru_in(h, W["mix_norm"][i], W[("lru_in", j)][0], f"lru_in_{i}")
            wa, wx = _block_diag(W["lru_w_a"][j]).astype(BF), _block_diag(W["lru_w_x"][j]).astype(BF)
            h, xc, r, ig, a, hs, y = _lru_fwd(gb, xb, W["lru_conv_w"][j], W["lru_conv_b"][j], wa, wx, W["lru_b_a"][j],
                                              W["lru_b_x"][j], W["lru_a_param"][j], W[("lru_out", j)][0], h, f"lru_fwd_{i}")
            s.update(hn=hn, gb=gb, xb=xb, wa=wa, wx=wx, xc=xc, r=r, ig=ig, a=a, hs=hs, y=y)
        else:
            h, s["u"], s["yb"] = _pool_fwd(h, W["mix_norm"][i], W[("pool_w", j)], W["pool_b"][j], W["pool_scale"][j],
                                           f"pool_fwd_{i}")
        s["h2"] = h
        nxt = [("ffn", i + 1, 1)] if i + 1 < L else []
        s["ffn2"], s["h3"], (h, s["n4"], s["gate"], s["pp"]) = ffn_fwd(
            h, W["ffn2_norm"][i], i, 2, [("ple_gate", i), ("ple_proj", i)] + nxt, [], ple=p[i])
        saved.append(s)

    loss, dh, d_final = _loss_head(h, W["final_norm"], tgt)

    big, recv = {}, {}
    n_lru, n_pool = L // 2 + L % 2, L // 2
    small = {k: [None] * L for k in ("ffn1_norm", "mix_norm", "ffn2_norm", "ple_norm")}
    for k in ("lru_conv_w", "lru_conv_b", "lru_w_a", "lru_b_a", "lru_w_x", "lru_b_x", "lru_a_param"):
        small[k] = [None] * n_lru
    for k in ("pool_b", "pool_scale"):
        small[k] = [None] * n_pool

    def scattering(pieces):
        return [(k, m) for k, m in pieces if k in big] if dist else []

    def ffn_bwd(dh, h_in, g, acts, i, f, during):
        key, w = ("ffn", i, f), W[("ffn", i, f)]
        a, b, sv, n = acts
        out = [scattering(d) for d in during]
        sent = [[(big[k], m) for k, m in o] for o in out]
        da, db, dhb, got0 = _ffn_bwd_act(dh, a, b, w, f"ffn{f}_bwd_act_{i}", scatter=sent[0])
        big[key], got1 = _ffn_bwd_w(da, db, sv, n, dhb, f"ffn{f}_dw_{i}", scatter=sent[1])
        out.append(scattering([(key, 0)] + ([(key, 1)] if (i, f) == (0, 1) else [])))
        dh, dg, got2 = _ffn_bwd_in(da, db, w, h_in, g, dh, f"ffn{f}_bwd_in_{i}", scatter=[(big[k], m) for k, m in out[2]])
        for o, got in zip(out, (got0, got1, got2)):
            recv.update(zip(o, got))
        return dh, dg

    for i in reversed(range(L)):
        j = i // 2
        lru = i % 2 == 0
        s = saved[i]
        dh, dz, dpp, dg = _ple_bwd(dh, s["gate"], s["pp"], s["h3"], W["ple_norm"][i], W[("ple_gate", i)][0],
                                   f"ple_bwd_{i}")
        big[("ple_gate", i)] = _mm_tn(s["n4"], dz, f"ple_gate_dw_{i}")[None]
        big[("ple_proj", i)] = _mm_tn(dpp, p[i], f"ple_proj_dw_{i}")[None]
        small["ple_norm"][i] = dg[0]
        above = ("ffn", i + 1, 1)
        dh, dg = ffn_bwd(dh, s["h2"], W["ffn2_norm"][i], s["ffn2"], i, 2, [
            [(above, 1)], [(above, 2), (("ple_gate", i), None), (("ple_proj", i), None)]])
        small["ffn2_norm"][i] = dg[0]
        if lru:
            big[("lru_out", j)] = _mm_tn(s["y"], dh, f"lru_out_dw_{i}")[None]
            dz, dpa, dpx, dsp, dba, dbx, dcb, dcw = _lru_bwd(
                dh, s["hs"], s["gb"], s["xb"], s["a"], s["r"], s["ig"], s["xc"], s["wa"], s["wx"], W[("lru_out", j)][0],
                W["lru_conv_w"][j], W["lru_a_param"][j], f"lru_bwd_{i}")
            small["lru_a_param"][j], small["lru_b_a"][j], small["lru_b_x"][j] = dsp[0], dba[0], dbx[0]
            dwa, dwx = _lru_gates_dw(s["xc"], dpa, dpx, f"lru_gates_dw_{i}")
            small["lru_w_a"][j], small["lru_w_x"][j] = _diag_blocks(dwa, LRU_HEADS), _diag_blocks(dwx, LRU_HEADS)
            small["lru_conv_w"][j], small["lru_conv_b"][j] = dcw[:CONV_WIDTH], dcb[0]
            big[("lru_in", j)] = _mm_tn(dz, s["hn"], f"lru_in_dw_{i}")[None]
            dh, dg = _lru_in_bwd(dz, W[("lru_in", j)][0], s["h1"], W["mix_norm"][i], dh, f"lru_in_bwd_{i}")
            mixer = [("lru_in", j), ("lru_out", j)]
        else:
            dh, dw, dbp, dsc, dg = _pool_bwd(dh, s["u"], s["yb"], W[("pool_w", j)], W["pool_scale"][j], s["h1"],
                                             W["mix_norm"][i], f"pool_bwd_{i}")
            big[("pool_w", j)] = dw.astype(BF)
            small["pool_b"][j], small["pool_scale"][j] = dbp[0], dsc[0]
            mixer = [("pool_w", j)]
        small["mix_norm"][i] = dg[0]
        second = ("ffn", i, 2)
        dh, dg = ffn_bwd(dh, s["h0"], W["ffn1_norm"][i], s["ffn1"], i, 1, [
            [(second, 1)], [(second, 2)] + [(k, None) for k in mixer]])
        small["ffn1_norm"][i] = dg[0]

    small = {k: jnp.stack(v) for k, v in small.items()}
    small["final_norm"] = d_final[0]
    return loss, dh, big, recv, small


SMALL_SHARDED = ("pool_b", "pool_scale", "lru_conv_w")
SMALL = ("ffn1_norm", "mix_norm", "ffn2_norm", "ple_norm", "final_norm", "lru_conv_b", "lru_w_a", "lru_b_a",
         "lru_w_x", "lru_b_x", "lru_a_param", "pool_b", "pool_scale", "lru_conv_w")


def _pack_big(w):
    t = lambda a: jnp.swapaxes(a, -1, -2)
    out = {}
    for i in range(w["ffn1_norm"].shape[0]):
        for f in (1, 2):
            out[("ffn", i, f)] = jnp.stack([t(w[f"ffn{f}_w_gate"][i]), t(w[f"ffn{f}_w_up"][i]), w[f"ffn{f}_w_down"][i]])
        out[("ple_gate", i)], out[("ple_proj", i)] = w["ple_w_gate"][i][None], t(w["ple_w_proj"][i])[None]
    for j in range(w["lru_w_in"].shape[0]):
        out[("lru_in", j)], out[("lru_out", j)] = t(w["lru_w_in"][j])[None], w["lru_w_out"][j][None]
    for j in range(w["pool_w"].shape[0]):
        out[("pool_w", j)] = w["pool_w"][j]
    return out


def _unpack_big(b, L):
    t = lambda a: jnp.swapaxes(a, -1, -2)
    n_lru, n_pool = L // 2 + L % 2, L // 2
    out = {"lru_w_in": jnp.stack([t(b[("lru_in", j)][0]) for j in range(n_lru)]),
           "lru_w_out": jnp.stack([b[("lru_out", j)][0] for j in range(n_lru)]),
           "pool_w": jnp.stack([b[("pool_w", j)] for j in range(n_pool)]),
           "ple_w_gate": jnp.stack([b[("ple_gate", i)][0] for i in range(L)]),
           "ple_w_proj": jnp.stack([t(b[("ple_proj", i)][0]) for i in range(L)])}
    for f in (1, 2):
        out[f"ffn{f}_w_gate"] = jnp.stack([t(b[("ffn", i, f)][0]) for i in range(L)])
        out[f"ffn{f}_w_up"] = jnp.stack([t(b[("ffn", i, f)][1]) for i in range(L)])
        out[f"ffn{f}_w_down"] = jnp.stack([b[("ffn", i, f)][2] for i in range(L)])
    return out


def _flatten(parts, names, rows_of=LANES):
    flat = jnp.concatenate([parts[k].reshape(-1) for k in names])
    pad = (-flat.size) % (16 * rows_of)
    return jnp.pad(flat, (0, pad)).reshape(1, -1, rows_of)


def _unflatten(flat, like, names):
    out, o = {}, 0
    flat = flat.reshape(-1)
    for k in names:
        n = like[k].size
        out[k] = flat[o:o + n].reshape(like[k].shape)
        o += n
    return out


def kernel(x, p, ffn1_norm, ffn1_w_gate, ffn1_w_up, ffn1_w_down, mix_norm, lru_w_in, lru_conv_w, lru_conv_b, lru_w_a, lru_b_a, lru_w_x, lru_b_x, lru_a_param, lru_w_out, pool_w, pool_b, pool_scale, ffn2_norm, ffn2_w_gate, ffn2_w_up, ffn2_w_down, ple_norm, ple_w_gate, ple_w_proj, final_norm, loss_target, m_ffn1_norm, m_ffn1_w_gate, m_ffn1_w_up, m_ffn1_w_down, m_mix_norm, m_lru_w_in, m_lru_conv_w, m_lru_conv_b, m_lru_w_a, m_lru_b_a, m_lru_w_x, m_lru_b_x, m_lru_a_param, m_lru_w_out, m_pool_w, m_pool_b, m_pool_scale, m_ffn2_norm, m_ffn2_w_gate, m_ffn2_w_up, m_ffn2_w_down, m_ple_norm, m_ple_w_gate, m_ple_w_proj, m_final_norm, v_ffn1_norm, v_ffn1_w_gate, v_ffn1_w_up, v_ffn1_w_down, v_mix_norm, v_lru_w_in, v_lru_conv_w, v_lru_conv_b, v_lru_w_a, v_lru_b_a, v_lru_w_x, v_lru_b_x, v_lru_a_param, v_lru_w_out, v_pool_w, v_pool_b, v_pool_scale, v_ffn2_norm, v_ffn2_w_gate, v_ffn2_w_up, v_ffn2_w_down, v_ple_norm, v_ple_w_gate, v_ple_w_proj, v_final_norm):
    names = ["ffn1_norm", "ffn1_w_gate", "ffn1_w_up", "ffn1_w_down", "mix_norm", "lru_w_in", "lru_conv_w", "lru_conv_b",
             "lru_w_a", "lru_b_a", "lru_w_x", "lru_b_x", "lru_a_param", "lru_w_out", "pool_w", "pool_b", "pool_scale",
             "ffn2_norm", "ffn2_w_gate", "ffn2_w_up", "ffn2_w_down", "ple_norm", "ple_w_gate", "ple_w_proj", "final_norm"]
    w = dict(zip(names, [ffn1_norm, ffn1_w_gate, ffn1_w_up, ffn1_w_down, mix_norm, lru_w_in, lru_conv_w, lru_conv_b, lru_w_a, lru_b_a, lru_w_x, lru_b_x, lru_a_param, lru_w_out, pool_w, pool_b, pool_scale, ffn2_norm, ffn2_w_gate, ffn2_w_up, ffn2_w_down, ple_norm, ple_w_gate, ple_w_proj, final_norm]))
    m = dict(zip(names, [m_ffn1_norm, m_ffn1_w_gate, m_ffn1_w_up, m_ffn1_w_down, m_mix_norm, m_lru_w_in, m_lru_conv_w, m_lru_conv_b, m_lru_w_a, m_lru_b_a, m_lru_w_x, m_lru_b_x, m_lru_a_param, m_lru_w_out, m_pool_w, m_pool_b, m_pool_scale, m_ffn2_norm, m_ffn2_w_gate, m_ffn2_w_up, m_ffn2_w_down, m_ple_norm, m_ple_w_gate, m_ple_w_proj, m_final_norm]))
    v = dict(zip(names, [v_ffn1_norm, v_ffn1_w_gate, v_ffn1_w_up, v_ffn1_w_down, v_mix_norm, v_lru_w_in, v_lru_conv_w, v_lru_conv_b, v_lru_w_a, v_lru_b_a, v_lru_w_x, v_lru_b_x, v_lru_a_param, v_lru_w_out, v_pool_w, v_pool_b, v_pool_scale, v_ffn2_norm, v_ffn2_w_gate, v_ffn2_w_up, v_ffn2_w_down, v_ple_norm, v_ple_w_gate, v_ple_w_proj, v_final_norm]))
    L = p.shape[0]
    px, py, pc = _position()
    me = 4 * px + 2 * py + pc

    blocks = {k: b.astype(BF) for k, b in _pack_big(w).items()}
    first = ("ffn", 0, 1)
    got, small_blocks = _all_gather([blocks[first], _flatten(w, SMALL_SHARDED)], "gather_first")
    W = {first: got}
    per_dev = small_blocks.reshape(N_DEV, -1)
    shards = [_unflatten(per_dev[k], w, SMALL_SHARDED) for k in range(N_DEV)]
    for k in SMALL:
        W[k] = jnp.concatenate([s[k] for s in shards], axis=-1) if k in SMALL_SHARDED else w[k]

    loss, dx, big, recv, small = _local_step(x[0], p[:, 0], loss_target[0], W, blocks)

    last = [(k, m) for k in big if (k, None) not in recv for m in range(big[k].shape[0]) if (k, m) not in recv]
    got, (parts,) = _scatter_and_gather([(big[k], m) for k, m in last], [_flatten(small, SMALL).astype(BF)],
                                        "scatter_last_gather_small")
    recv.update(zip(last, got))

    def total(k):
        tag = "sum_" + "_".join(map(str, k))
        if (k, None) in recv:
            return _scatter_sum([recv[(k, None)]], tag)
        return _scatter_sum([recv[(k, m)] for m in range(big[k].shape[0])], tag)

    grads = _unpack_big({k: total(k) for k in big}, L)
    total_small = _sum_devices(parts.reshape(N_DEV, -1, LANES), "sum_small_grads")
    full = _unflatten(total_small, {k: W[k] for k in SMALL}, SMALL)
    for k in SMALL:
        if k in SMALL_SHARDED:
            n = w[k].shape[-1]
            grads[k] = lax.dynamic_slice_in_dim(full[k], me * n, n, axis=-1)
        else:
            grads[k] = full[k]

    delta, new_m, new_v = {}, {}, {}
    for k in names:
        delta[k], new_m[k], new_v[k] = _adamw(w[k], grads[k], m[k], v[k], f"adamw_{k}")
    total_loss = lax.psum(loss[0, 0], ("x", "y", "c"))
    return (total_loss, dx[None], *[grads[k] for k in names], *[delta[k] for k in names],
            *[new_m[k] for k in names], *[new_v[k] for k in names])
```
